```python
import math
import jax, jax.numpy as jnp
from jax import lax
import numpy as np

D_MODEL = 1024
BATCH = 8
SEQ = 8192
DEPTH = 2

N_MIXERS = 2
N_A_LAYERS = (DEPTH + 1) // 2
N_B_LAYERS = DEPTH // 2

GDN_HEADS = 8
GDN_DK = 128
GDN_DV = 128
GDN_CONV = 4
GDN_CHUNK = 64
GDN_HK = GDN_HEADS * GDN_DK
GDN_HV = GDN_HEADS * GDN_DV
GDN_IN = 2 * GDN_HK + 2 * GDN_HV + 2 * GDN_HEADS
GDN_CONV_CH = 2 * GDN_HK + GDN_HV

DSW_GROUPS = ((128, 1), (512, 4), (2048, 16))
DSW_N_GROUPS = len(DSW_GROUPS)
DSW_HEADS = 8
DSW_DH = 64
DSW_HG = DSW_HEADS * DSW_DH
DSW_IN = 3 * DSW_N_GROUPS * DSW_HG

REL_BUCKETS = 32
REL_MAX_DIST = 2048
REL_HEADS = DSW_N_GROUPS * DSW_HEADS

FFN_HIDDEN = -(-8 * D_MODEL // (3 * 256)) * 256

RMS_EPS = 1e-6

kernel_name = "hybrid_gdn_dilated_swa_adaln"


def rms_norm(x, gain):
    xf = x.astype(jnp.float32)
    y = xf * lax.rsqrt(jnp.mean(xf * xf, axis=-1, keepdims=True) + RMS_EPS)
    return y * gain.astype(jnp.float32)


def l2_norm(x):
    xf = x.astype(jnp.float32)
    return xf * lax.rsqrt(jnp.sum(xf * xf, axis=-1, keepdims=True) + RMS_EPS)


def causal_depthwise_conv(x, w):
    K, C = w.shape
    return lax.conv_general_dilated(
        x, w[:, None, :].astype(x.dtype), window_strides=(1,),
        padding=((K - 1, 0),), dimension_numbers=("NWC", "WIO", "NWC"),
        feature_group_count=C)


def chunk_gated_delta_rule(q, k, v, g, beta):
    Bsz, S, H, dk = q.shape
    dv = v.shape[-1]
    C = GDN_CHUNK
    N = S // C
    to_chunks = lambda t: jnp.transpose(t.reshape(Bsz, N, C, H, t.shape[-1]), (0, 3, 1, 2, 4))
    q = to_chunks(q * (dk ** -0.5))
    k = to_chunks(k)
    v = to_chunks(v.astype(jnp.float32))
    beta = jnp.transpose(beta.reshape(Bsz, N, C, H), (0, 3, 1, 2))
    g = jnp.cumsum(jnp.transpose(g.reshape(Bsz, N, C, H), (0, 3, 1, 2)), axis=-1)

    causal = jnp.tril(jnp.ones((C, C), dtype=bool))
    strict = jnp.tril(jnp.ones((C, C), dtype=bool), -1)
    decay = jnp.exp(jnp.where(causal, g[..., :, None] - g[..., None, :], -jnp.inf))
    kb = k * beta[..., None]
    vb = v * beta[..., None]
    Lmat = jnp.where(strict, jnp.einsum('bhncd,bhnmd->bhncm', kb, k) * decay, 0.0)
    rhs = jnp.concatenate([vb, kb * jnp.exp(g)[..., None]], axis=-1)
    sol = lax.linalg.triangular_solve(Lmat, rhs, left_side=True, lower=True, unit_diagonal=True)
    u = sol[..., :dv]
    w = sol[..., dv:]
    qk = jnp.where(causal, jnp.einsum('bhncd,bhnmd->bhncm', q, k) * decay, 0.0)

    g_last = g[..., -1]
    q_dec = q * jnp.exp(g)[..., None]
    k_dec = k * jnp.exp(g_last[..., None] - g)[..., None]

    def step(state, inp):
        qd, a, uu, ww, kd, gl = inp
        v_new = uu - jnp.einsum('bhck,bhkv->bhcv', ww, state)
        o = jnp.einsum('bhck,bhkv->bhcv', qd, state) + jnp.einsum('bhcm,bhmv->bhcv', a, v_new)
        state = state * jnp.exp(gl)[..., None, None] + jnp.einsum('bhck,bhcv->bhkv', kd, v_new)
        return state, o

    xs = tuple(jnp.moveaxis(t, 2, 0) for t in (q_dec, qk, u, w, k_dec, g_last))
    state0 = jnp.zeros((Bsz, H, dk, dv), jnp.float32)
    _, o = lax.scan(step, state0, xs)
    return jnp.transpose(o, (1, 0, 3, 2, 4)).reshape(Bsz, S, H, dv)


def gated_deltanet_mixer(h, w_in, conv_w, a_log, dt_bias, out_gain, w_out):
    Bsz, S, _ = h.shape
    proj = h @ w_in
    qkv, z, a, b = jnp.split(proj, [GDN_CONV_CH, GDN_CONV_CH + GDN_HV,
                                    GDN_CONV_CH + GDN_HV + GDN_HEADS], axis=-1)
    qkv = jax.nn.silu(causal_depthwise_conv(qkv, conv_w))
    q, k, v = jnp.split(qkv, [GDN_HK, 2 * GDN_HK], axis=-1)
    q = l2_norm(q.reshape(Bsz, S, GDN_HEADS, GDN_DK))
    k = l2_norm(k.reshape(Bsz, S, GDN_HEADS, GDN_DK))
    v = v.reshape(Bsz, S, GDN_HEADS, GDN_DV)
    beta = jax.nn.sigmoid(b.astype(jnp.float32))
    g = -jnp.exp(a_log.astype(jnp.float32)) * jax.nn.softplus(a.astype(jnp.float32) + dt_bias.astype(jnp.float32))
    o = chunk_gated_delta_rule(q, k, v, g, beta)
    zf = z.reshape(Bsz, S, GDN_HEADS, GDN_DV).astype(jnp.float32)
    o = rms_norm(o, out_gain) * jax.nn.silu(zf)
    return o.reshape(Bsz, S, GDN_HV).astype(h.dtype) @ w_out


def t5_causal_bucket(dist):
    max_exact = REL_BUCKETS // 2
    scaled = jnp.log(jnp.maximum(dist, 1).astype(jnp.float32) / max_exact) / math.log(REL_MAX_DIST / max_exact)
    large = max_exact + (scaled * (REL_BUCKETS - max_exact)).astype(jnp.int32)
    large = jnp.minimum(large, REL_BUCKETS - 1)
    return jnp.where(dist < max_exact, dist, large)


def dilated_window_group(q, k, v, bias_table, window, dilation):
    Bsz, S, H, dh = q.shape
    span = window // dilation
    blk = span
    unit = dilation * blk
    S_pad = -(-S // unit) * unit
    nb = S_pad // unit

    def sub(t):
        t = jnp.pad(t, ((0, 0), (0, S_pad - S), (0, 0), (0, 0)))
        return jnp.transpose(t.reshape(Bsz, nb, blk, dilation, H, dh), (0, 3, 1, 2, 4, 5))

    qs, ks, vs = sub(q), sub(k), sub(v)
    shift = lambda t: jnp.concatenate([jnp.zeros_like(t[:, :, :1]), t[:, :, :-1]], axis=2)
    kb = jnp.concatenate([shift(ks), ks], axis=3)
    vb = jnp.concatenate([shift(vs), vs], axis=3)

    qi = jnp.arange(blk)[:, None] + blk
    ki = jnp.arange(2 * blk)[None, :]
    dist = qi - ki
    band = (dist >= 0) & (dist <= span)
    valid = band[None] & ((jnp.arange(nb) > 0)[:, None, None] | (ki >= blk)[None])
    bias = jnp.transpose(bias_table.astype(jnp.float32)[t5_causal_bucket(jnp.maximum(dist, 0) * dilation)], (2, 0, 1))

    logits = jnp.einsum('brnqhd,brnkhd->brnhqk', qs, kb) + bias[None, None, None]
    logits = jnp.where(valid[None, None, :, None], logits, -jnp.inf)
    m = jnp.max(logits, axis=-1, keepdims=True)
    p = jnp.exp(logits - m)
    l = jnp.sum(p, axis=-1, keepdims=True)
    o = jnp.einsum('brnhqk,brnkhd->brnqhd', p / l, vb)
    lse = (m + jnp.log(l))[..., 0]
    o = jnp.transpose(o, (0, 2, 3, 1, 4, 5)).reshape(Bsz, S_pad, H, dh)[:, :S]
    lse = jnp.transpose(lse, (0, 2, 4, 1, 3)).reshape(Bsz, S_pad, H)[:, :S]
    return o, lse


def dilated_attention_mixer(h, w_in, q_gain, k_gain, rel_bias, w_out):
    Bsz, S, _ = h.shape
    proj = (h @ w_in).reshape(Bsz, S, 3, DSW_N_GROUPS, DSW_HEADS, DSW_DH)
    q = rms_norm(proj[:, :, 0], q_gain) * (DSW_DH ** -0.5)
    k = rms_norm(proj[:, :, 1], k_gain)
    v = proj[:, :, 2].astype(jnp.float32)
    outs, lses = [], []
    for gi, (window, dilation) in enumerate(DSW_GROUPS):
        o, lse = dilated_window_group(q[:, :, gi], k[:, :, gi], v[:, :, gi],
                                      rel_bias[:, gi * DSW_HEADS:(gi + 1) * DSW_HEADS], window, dilation)
        outs.append(o)
        lses.append(lse)
    wts = jax.nn.softmax(jnp.stack(lses), axis=0)
    o = jnp.sum(wts[..., None] * jnp.stack(outs), axis=0)
    return o.reshape(Bsz, S, DSW_HG).astype(h.dtype) @ w_out


def swiglu(h, w_in, w_out):
    gate, up = jnp.split(h @ w_in, 2, axis=-1)
    return (jax.nn.silu(gate) * up) @ w_out


def _fwd_setup_inputs(seed: int = 0) -> dict:
    key = jax.random.key(seed)
    ks = jax.random.split(key, 20)
    nrm = lambda k, shape, s: jax.random.normal(k, shape, jnp.float32) * s
    D = D_MODEL
    a_init = jax.random.uniform(ks[10], (N_A_LAYERS, GDN_HEADS), jnp.float32, 1.0, 16.0)
    dt = jnp.exp(jax.random.uniform(ks[11], (N_A_LAYERS, GDN_HEADS), jnp.float32,
                                    math.log(1e-3), math.log(1e-1)))
    return {
        "x": nrm(ks[0], (BATCH, SEQ, D), 1.0),
        "c": nrm(ks[1], (BATCH, D), 1.0),
        "w_ada": nrm(ks[2], (DEPTH, D, 6 * D), D ** -0.5),
        "b_ada": nrm(ks[3], (DEPTH, 6 * D), 0.02),
        "norm_mix": 1.0 + nrm(ks[4], (DEPTH, D), 0.02),
        "norm_ffn": 1.0 + nrm(ks[5], (DEPTH, D), 0.02),
        "w_ffn_in": nrm(ks[6], (DEPTH, D, 2 * FFN_HIDDEN), D ** -0.5),
        "w_ffn_out": nrm(ks[7], (DEPTH, FFN_HIDDEN, D), FFN_HIDDEN ** -0.5),
        "gdn_w_in": nrm(ks[8], (N_A_LAYERS, D, GDN_IN), D ** -0.5),
        "gdn_conv": nrm(ks[9], (N_A_LAYERS, GDN_CONV, GDN_CONV_CH), GDN_CONV ** -0.5),
        "gdn_a_log": jnp.log(a_init),
        "gdn_dt_bias": dt + jnp.log(-jnp.expm1(-dt)),
        "gdn_out_norm": 1.0 + nrm(ks[12], (N_A_LAYERS, GDN_DV), 0.02),
        "gdn_w_out": nrm(ks[13], (N_A_LAYERS, GDN_HV, D), GDN_HV ** -0.5),
        "dsw_w_in": nrm(ks[14], (N_B_LAYERS, D, DSW_IN), D ** -0.5),
        "dsw_q_norm": 1.0 + nrm(ks[15], (N_B_LAYERS, DSW_DH), 0.02),
        "dsw_k_norm": 1.0 + nrm(ks[16], (N_B_LAYERS, DSW_DH), 0.02),
        "dsw_w_out": nrm(ks[17], (N_B_LAYERS, DSW_HG, D), DSW_HG ** -0.5),
        "rel_bias": nrm(ks[18], (REL_BUCKETS, REL_HEADS), 0.5),
    }


def _fwd_reference(x, c, w_ada, b_ada, norm_mix, norm_ffn, w_ffn_in, w_ffn_out,
              gdn_w_in, gdn_conv, gdn_a_log, gdn_dt_bias, gdn_out_norm, gdn_w_out,
              dsw_w_in, dsw_q_norm, dsw_k_norm, dsw_w_out, rel_bias):
    cond = jax.nn.silu(c.astype(jnp.float32))
    for layer in range(DEPTH):
        mod = (cond @ w_ada[layer].astype(jnp.float32) + b_ada[layer].astype(jnp.float32)).astype(x.dtype)
        sh1, sc1, g1, sh2, sc2, g2 = jnp.split(mod[:, None, :], 6, axis=-1)

        h = (rms_norm(x, norm_mix[layer]) * (1.0 + sc1) + sh1).astype(x.dtype)
        j = layer // N_MIXERS
        if layer % N_MIXERS == 0:
            y = gated_deltanet_mixer(h, gdn_w_in[j], gdn_conv[j], gdn_a_log[j], gdn_dt_bias[j],
                                     gdn_out_norm[j], gdn_w_out[j])
        else:
            y = dilated_attention_mixer(h, dsw_w_in[j], dsw_q_norm[j], dsw_k_norm[j],
                                        rel_bias, dsw_w_out[j])
        x = x + g1 * y

        h = (rms_norm(x, norm_ffn[layer]) * (1.0 + sc2) + sh2).astype(x.dtype)
        x = x + g2 * swiglu(h, w_ffn_in[layer], w_ffn_out[layer])
    return x


import jax as _jax
import jax.numpy as _jnp

TWIN_FORMAT = 'train_step'
FWD_PARAMS = ['x', 'c', 'w_ada', 'b_ada', 'norm_mix', 'norm_ffn', 'w_ffn_in', 'w_ffn_out', 'gdn_w_in', 'gdn_conv', 'gdn_a_log', 'gdn_dt_bias', 'gdn_out_norm', 'gdn_w_out', 'dsw_w_in', 'dsw_q_norm', 'dsw_k_norm', 'dsw_w_out', 'rel_bias']
TWIN_WEIGHTS = ['w_ada', 'b_ada', 'norm_mix', 'norm_ffn', 'w_ffn_in', 'w_ffn_out', 'gdn_w_in', 'gdn_conv', 'gdn_a_log', 'gdn_dt_bias', 'gdn_out_norm', 'gdn_w_out', 'dsw_w_in', 'dsw_q_norm', 'dsw_k_norm', 'dsw_w_out', 'rel_bias']
TWIN_DIFF_INPUT = 'x'
TWIN_INPUTS = ['x', 'c', 'w_ada', 'b_ada', 'norm_mix', 'norm_ffn', 'w_ffn_in', 'w_ffn_out', 'gdn_w_in', 'gdn_conv', 'gdn_a_log', 'gdn_dt_bias', 'gdn_out_norm', 'gdn_w_out', 'dsw_w_in', 'dsw_q_norm', 'dsw_k_norm', 'dsw_w_out', 'rel_bias', 'loss_target', 'm_w_ada', 'm_b_ada', 'm_norm_mix', 'm_norm_ffn', 'm_w_ffn_in', 'm_w_ffn_out', 'm_gdn_w_in', 'm_gdn_conv', 'm_gdn_a_log', 'm_gdn_dt_bias', 'm_gdn_out_norm', 'm_gdn_w_out', 'm_dsw_w_in', 'm_dsw_q_norm', 'm_dsw_k_norm', 'm_dsw_w_out', 'm_rel_bias', 'v_w_ada', 'v_b_ada', 'v_norm_mix', 'v_norm_ffn', 'v_w_ffn_in', 'v_w_ffn_out', 'v_gdn_w_in', 'v_gdn_conv', 'v_gdn_a_log', 'v_gdn_dt_bias', 'v_gdn_out_norm', 'v_gdn_w_out', 'v_dsw_w_in', 'v_dsw_q_norm', 'v_dsw_k_norm', 'v_dsw_w_out', 'v_rel_bias']
TWIN_OUTPUTS = ['loss', 'grad_x', 'grad_w_ada', 'grad_b_ada', 'grad_norm_mix', 'grad_norm_ffn', 'grad_w_ffn_in', 'grad_w_ffn_out', 'grad_gdn_w_in', 'grad_gdn_conv', 'grad_gdn_a_log', 'grad_gdn_dt_bias', 'grad_gdn_out_norm', 'grad_gdn_w_out', 'grad_dsw_w_in', 'grad_dsw_q_norm', 'grad_dsw_k_norm', 'grad_dsw_w_out', 'grad_rel_bias', 'delta_w_ada', 'delta_b_ada', 'delta_norm_mix', 'delta_norm_ffn', 'delta_w_ffn_in', 'delta_w_ffn_out', 'delta_gdn_w_in', 'delta_gdn_conv', 'delta_gdn_a_log', 'delta_gdn_dt_bias', 'delta_gdn_out_norm', 'delta_gdn_w_out', 'delta_dsw_w_in', 'delta_dsw_q_norm', 'delta_dsw_k_norm', 'delta_dsw_w_out', 'delta_rel_bias', 'new_m_w_ada', 'new_m_b_ada', 'new_m_norm_mix', 'new_m_norm_ffn', 'new_m_w_ffn_in', 'new_m_w_ffn_out', 'new_m_gdn_w_in', 'new_m_gdn_conv', 'new_m_gdn_a_log', 'new_m_gdn_dt_bias', 'new_m_gdn_out_norm', 'new_m_gdn_w_out', 'new_m_dsw_w_in', 'new_m_dsw_q_norm', 'new_m_dsw_k_norm', 'new_m_dsw_w_out', 'new_m_rel_bias', 'new_v_w_ada', 'new_v_b_ada', 'new_v_norm_mix', 'new_v_norm_ffn', 'new_v_w_ffn_in', 'new_v_w_ffn_out', 'new_v_gdn_w_in', 'new_v_gdn_conv', 'new_v_gdn_a_log', 'new_v_gdn_dt_bias', 'new_v_gdn_out_norm', 'new_v_gdn_w_out', 'new_v_dsw_w_in', 'new_v_dsw_q_norm', 'new_v_dsw_k_norm', 'new_v_dsw_w_out', 'new_v_rel_bias']
TWIN_LEAF_KINDS = {'loss': 'loss', 'grad_x': 'grad_x', 'grad_w_ada': 'grad_w', 'grad_b_ada': 'grad_w', 'grad_norm_mix': 'grad_w', 'grad_norm_ffn': 'grad_w', 'grad_w_ffn_in': 'grad_w', 'grad_w_ffn_out': 'grad_w', 'grad_gdn_w_in': 'grad_w', 'grad_gdn_conv': 'grad_w', 'grad_gdn_a_log': 'grad_w', 'grad_gdn_dt_bias': 'grad_w', 'grad_gdn_out_norm': 'grad_w', 'grad_gdn_w_out': 'grad_w', 'grad_dsw_w_in': 'grad_w', 'grad_dsw_q_norm': 'grad_w', 'grad_dsw_k_norm': 'grad_w', 'grad_dsw_w_out': 'grad_w', 'grad_rel_bias': 'grad_w', 'delta_w_ada': 'delta_w', 'delta_b_ada': 'delta_w', 'delta_norm_mix': 'delta_w', 'delta_norm_ffn': 'delta_w', 'delta_w_ffn_in': 'delta_w', 'delta_w_ffn_out': 'delta_w', 'delta_gdn_w_in': 'delta_w', 'delta_gdn_conv': 'delta_w', 'delta_gdn_a_log': 'delta_w', 'delta_gdn_dt_bias': 'delta_w', 'delta_gdn_out_norm': 'delta_w', 'delta_gdn_w_out': 'delta_w', 'delta_dsw_w_in': 'delta_w', 'delta_dsw_q_norm': 'delta_w', 'delta_dsw_k_norm': 'delta_w', 'delta_dsw_w_out': 'delta_w', 'delta_rel_bias': 'delta_w', 'new_m_w_ada': 'new_m', 'new_m_b_ada': 'new_m', 'new_m_norm_mix': 'new_m', 'new_m_norm_ffn': 'new_m', 'new_m_w_ffn_in': 'new_m', 'new_m_w_ffn_out': 'new_m', 'new_m_gdn_w_in': 'new_m', 'new_m_gdn_conv': 'new_m', 'new_m_gdn_a_log': 'new_m', 'new_m_gdn_dt_bias': 'new_m', 'new_m_gdn_out_norm': 'new_m', 'new_m_gdn_w_out': 'new_m', 'new_m_dsw_w_in': 'new_m', 'new_m_dsw_q_norm': 'new_m', 'new_m_dsw_k_norm': 'new_m', 'new_m_dsw_w_out': 'new_m', 'new_m_rel_bias': 'new_m', 'new_v_w_ada': 'new_v', 'new_v_b_ada': 'new_v', 'new_v_norm_mix': 'new_v', 'new_v_norm_ffn': 'new_v', 'new_v_w_ffn_in': 'new_v', 'new_v_w_ffn_out': 'new_v', 'new_v_gdn_w_in': 'new_v', 'new_v_gdn_conv': 'new_v', 'new_v_gdn_a_log': 'new_v', 'new_v_gdn_dt_bias': 'new_v', 'new_v_gdn_out_norm': 'new_v', 'new_v_gdn_w_out': 'new_v', 'new_v_dsw_w_in': 'new_v', 'new_v_dsw_q_norm': 'new_v', 'new_v_dsw_k_norm': 'new_v', 'new_v_dsw_w_out': 'new_v', 'new_v_rel_bias': 'new_v'}


def _forward(args):
    return _fwd_reference(*[args[k] for k in FWD_PARAMS])


def _output_shape():
    def fwd():
        inp = _fwd_setup_inputs(0)
        return _fwd_reference(*[inp[k] for k in FWD_PARAMS])
    out = _jax.eval_shape(fwd)
    return out.shape, out.dtype

N_MICROBATCH = 1
ADAM_LR = 0.001
ADAM_B1 = 0.9
ADAM_B2 = 0.999
ADAM_EPS = 1e-08
ADAM_WD = 0.01
ADAM_STEP = 10
PER_EXAMPLE_BATCH_AXIS = {'x': 0, 'c': 0, 'loss_target': 0}
SHARED_INPUTS = []
_WEIGHT_DTYPES = {'w_ada': _jnp.float32, 'b_ada': _jnp.float32, 'norm_mix': _jnp.float32, 'norm_ffn': _jnp.float32, 'w_ffn_in': _jnp.float32, 'w_ffn_out': _jnp.float32, 'gdn_w_in': _jnp.float32, 'gdn_conv': _jnp.float32, 'gdn_a_log': _jnp.float32, 'gdn_dt_bias': _jnp.float32, 'gdn_out_norm': _jnp.float32, 'gdn_w_out': _jnp.float32, 'dsw_w_in': _jnp.float32, 'dsw_q_norm': _jnp.float32, 'dsw_k_norm': _jnp.float32, 'dsw_w_out': _jnp.float32, 'rel_bias': _jnp.float32}
MOMENT_SCALE = {'w_ada': 8.186909e+00, 'b_ada': 2.092476e+01, 'norm_mix': 1.190649e+01, 'norm_ffn': 5.285073e+01, 'w_ffn_in': 2.279129e+00, 'w_ffn_out': 2.477375e+00, 'gdn_w_in': 4.269402e+00, 'gdn_conv': 4.910897e+00, 'gdn_a_log': 3.330155e+01, 'gdn_dt_bias': 3.094100e+01, 'gdn_out_norm': 1.401102e+02, 'gdn_w_out': 4.156726e+00, 'dsw_w_in': 1.727620e+00, 'dsw_q_norm': 8.266418e+00, 'dsw_k_norm': 8.266794e+00, 'dsw_w_out': 3.198902e+00, 'rel_bias': 1.729041e+00}


def _to_microbatches(a, axis):
    t = _jnp.moveaxis(a, axis, 0)
    t = t.reshape((N_MICROBATCH, t.shape[0] // N_MICROBATCH) + t.shape[1:])
    return _jnp.moveaxis(t, 1, axis + 1)


def setup_inputs(seed: int = 0) -> dict:
    inp = _fwd_setup_inputs(seed)
    key = _jax.random.fold_in(_jax.random.key(seed), 7919)
    shape, _ = _output_shape()
    out = dict(inp)
    out["loss_target"] = _jax.random.normal(_jax.random.fold_in(key, 0), shape, _jnp.float32)
    for i, name in enumerate(TWIN_WEIGHTS):
        w = inp[name].astype(_jnp.float32)
        if MOMENT_SCALE is None:
            s = _jnp.sqrt(_jnp.mean(_jnp.square(w)) + 1e-30)
        else:
            s = MOMENT_SCALE[name]
        km, kv = _jax.random.split(_jax.random.fold_in(key, i + 1))
        out[name] = w
        out["m_" + name] = s * _jax.random.normal(km, w.shape, _jnp.float32)
        out["v_" + name] = (s * s) * _jax.random.uniform(kv, w.shape, _jnp.float32, 0.5, 1.5)
    if N_MICROBATCH > 1:
        for name, axis in PER_EXAMPLE_BATCH_AXIS.items():
            out[name] = _to_microbatches(out[name], axis)
    return {'x': out['x'], 'c': out['c'], 'w_ada': out['w_ada'], 'b_ada': out['b_ada'], 'norm_mix': out['norm_mix'], 'norm_ffn': out['norm_ffn'], 'w_ffn_in': out['w_ffn_in'], 'w_ffn_out': out['w_ffn_out'], 'gdn_w_in': out['gdn_w_in'], 'gdn_conv': out['gdn_conv'], 'gdn_a_log': out['gdn_a_log'], 'gdn_dt_bias': out['gdn_dt_bias'], 'gdn_out_norm': out['gdn_out_norm'], 'gdn_w_out': out['gdn_w_out'], 'dsw_w_in': out['dsw_w_in'], 'dsw_q_norm': out['dsw_q_norm'], 'dsw_k_norm': out['dsw_k_norm'], 'dsw_w_out': out['dsw_w_out'], 'rel_bias': out['rel_bias'], 'loss_target': out['loss_target'], 'm_w_ada': out['m_w_ada'], 'm_b_ada': out['m_b_ada'], 'm_norm_mix': out['m_norm_mix'], 'm_norm_ffn': out['m_norm_ffn'], 'm_w_ffn_in': out['m_w_ffn_in'], 'm_w_ffn_out': out['m_w_ffn_out'], 'm_gdn_w_in': out['m_gdn_w_in'], 'm_gdn_conv': out['m_gdn_conv'], 'm_gdn_a_log': out['m_gdn_a_log'], 'm_gdn_dt_bias': out['m_gdn_dt_bias'], 'm_gdn_out_norm': out['m_gdn_out_norm'], 'm_gdn_w_out': out['m_gdn_w_out'], 'm_dsw_w_in': out['m_dsw_w_in'], 'm_dsw_q_norm': out['m_dsw_q_norm'], 'm_dsw_k_norm': out['m_dsw_k_norm'], 'm_dsw_w_out': out['m_dsw_w_out'], 'm_rel_bias': out['m_rel_bias'], 'v_w_ada': out['v_w_ada'], 'v_b_ada': out['v_b_ada'], 'v_norm_mix': out['v_norm_mix'], 'v_norm_ffn': out['v_norm_ffn'], 'v_w_ffn_in': out['v_w_ffn_in'], 'v_w_ffn_out': out['v_w_ffn_out'], 'v_gdn_w_in': out['v_gdn_w_in'], 'v_gdn_conv': out['v_gdn_conv'], 'v_gdn_a_log': out['v_gdn_a_log'], 'v_gdn_dt_bias': out['v_gdn_dt_bias'], 'v_gdn_out_norm': out['v_gdn_out_norm'], 'v_gdn_w_out': out['v_gdn_w_out'], 'v_dsw_w_in': out['v_dsw_w_in'], 'v_dsw_q_norm': out['v_dsw_q_norm'], 'v_dsw_k_norm': out['v_dsw_k_norm'], 'v_dsw_w_out': out['v_dsw_w_out'], 'v_rel_bias': out['v_rel_bias']}


def _loss(weights, diff, rest, loss_target):
    with _jax.named_scope("forward"):
        args = {**rest, TWIN_DIFF_INPUT: diff, **{k: w.astype(_WEIGHT_DTYPES[k]) for k, w in weights.items()}}
        y = _forward(args)
    with _jax.named_scope("loss_head"):
        err = _jnp.square(y.astype(_jnp.float32) - loss_target)
        return 0.5 * _jnp.sum(_jnp.mean(err, axis=-1)) if err.ndim else 0.5 * err


def _adamw(w, g, m, v):
    m = ADAM_B1 * m + (1.0 - ADAM_B1) * g
    v = ADAM_B2 * v + (1.0 - ADAM_B2) * _jnp.square(g)
    m_hat = m / (1.0 - ADAM_B1 ** ADAM_STEP)
    v_hat = v / (1.0 - ADAM_B2 ** ADAM_STEP)
    delta = -ADAM_LR * (m_hat / (_jnp.sqrt(v_hat) + ADAM_EPS) + ADAM_WD * w)
    return delta, m, v


def reference(x, c, w_ada, b_ada, norm_mix, norm_ffn, w_ffn_in, w_ffn_out, gdn_w_in, gdn_conv, gdn_a_log, gdn_dt_bias, gdn_out_norm, gdn_w_out, dsw_w_in, dsw_q_norm, dsw_k_norm, dsw_w_out, rel_bias, loss_target, m_w_ada, m_b_ada, m_norm_mix, m_norm_ffn, m_w_ffn_in, m_w_ffn_out, m_gdn_w_in, m_gdn_conv, m_gdn_a_log, m_gdn_dt_bias, m_gdn_out_norm, m_gdn_w_out, m_dsw_w_in, m_dsw_q_norm, m_dsw_k_norm, m_dsw_w_out, m_rel_bias, v_w_ada, v_b_ada, v_norm_mix, v_norm_ffn, v_w_ffn_in, v_w_ffn_out, v_gdn_w_in, v_gdn_conv, v_gdn_a_log, v_gdn_dt_bias, v_gdn_out_norm, v_gdn_w_out, v_dsw_w_in, v_dsw_q_norm, v_dsw_k_norm, v_dsw_w_out, v_rel_bias):
    given = dict(x=x, c=c, w_ada=w_ada, b_ada=b_ada, norm_mix=norm_mix, norm_ffn=norm_ffn, w_ffn_in=w_ffn_in, w_ffn_out=w_ffn_out, gdn_w_in=gdn_w_in, gdn_conv=gdn_conv, gdn_a_log=gdn_a_log, gdn_dt_bias=gdn_dt_bias, gdn_out_norm=gdn_out_norm, gdn_w_out=gdn_w_out, dsw_w_in=dsw_w_in, dsw_q_norm=dsw_q_norm, dsw_k_norm=dsw_k_norm, dsw_w_out=dsw_w_out, rel_bias=rel_bias, loss_target=loss_target, m_w_ada=m_w_ada, m_b_ada=m_b_ada, m_norm_mix=m_norm_mix, m_norm_ffn=m_norm_ffn, m_w_ffn_in=m_w_ffn_in, m_w_ffn_out=m_w_ffn_out, m_gdn_w_in=m_gdn_w_in, m_gdn_conv=m_gdn_conv, m_gdn_a_log=m_gdn_a_log, m_gdn_dt_bias=m_gdn_dt_bias, m_gdn_out_norm=m_gdn_out_norm, m_gdn_w_out=m_gdn_w_out, m_dsw_w_in=m_dsw_w_in, m_dsw_q_norm=m_dsw_q_norm, m_dsw_k_norm=m_dsw_k_norm, m_dsw_w_out=m_dsw_w_out, m_rel_bias=m_rel_bias, v_w_ada=v_w_ada, v_b_ada=v_b_ada, v_norm_mix=v_norm_mix, v_norm_ffn=v_norm_ffn, v_w_ffn_in=v_w_ffn_in, v_w_ffn_out=v_w_ffn_out, v_gdn_w_in=v_gdn_w_in, v_gdn_conv=v_gdn_conv, v_gdn_a_log=v_gdn_a_log, v_gdn_dt_bias=v_gdn_dt_bias, v_gdn_out_norm=v_gdn_out_norm, v_gdn_w_out=v_gdn_w_out, v_dsw_w_in=v_dsw_w_in, v_dsw_q_norm=v_dsw_q_norm, v_dsw_k_norm=v_dsw_k_norm, v_dsw_w_out=v_dsw_w_out, v_rel_bias=v_rel_bias)
    weights = {n: given[n] for n in TWIN_WEIGHTS}
    shared = {n: given[n] for n in SHARED_INPUTS}
    per_example = {n: given[n] for n in ['x', 'c']}
    grad_fn = _jax.value_and_grad(_loss, argnums=(0, 1))

    def one_microbatch(ex, loss_target):
        ex = dict(ex)
        diff = ex.pop(TWIN_DIFF_INPUT)
        return grad_fn(weights, diff, {**shared, **ex}, loss_target)

    if N_MICROBATCH == 1:
        loss, (grad_w, grad_x) = one_microbatch(per_example, given["loss_target"])
    else:
        def body(carry, xs):
            loss_sum, grad_sum = carry
            l_k, (gw_k, gx_k) = one_microbatch(xs[0], xs[1])
            with _jax.named_scope("update"):
                return (loss_sum + l_k, _jax.tree.map(_jnp.add, grad_sum, gw_k)), gx_k

        init = (_jnp.zeros((), _jnp.float32), _jax.tree.map(_jnp.zeros_like, weights))
        (loss, grad_w), grad_x = _jax.lax.scan(body, init, (per_example, given["loss_target"]))
    with _jax.named_scope("update"):
        delta_w, new_m, new_v = {}, {}, {}
        for n in TWIN_WEIGHTS:
            delta_w[n], new_m[n], new_v[n] = _adamw(weights[n], grad_w[n], given["m_" + n], given["v_" + n])
    return (loss, grad_x, *[grad_w[n] for n in TWIN_WEIGHTS], *[delta_w[n] for n in TWIN_WEIGHTS],
            *[new_m[n] for n in TWIN_WEIGHTS], *[new_v[n] for n in TWIN_WEIGHTS])
```

```python
import functools
import math

import numpy as np
import jax
import jax.numpy as jnp
from jax import lax
from jax.experimental import pallas as pl
from jax.experimental.pallas import tpu as pltpu

F32 = jnp.float32
BF16 = jnp.bfloat16

N_DEV = 8
RMS_EPS = 1e-6
LANES = 128
V7X_VMEM_LIMIT = 48 * 1024 * 1024

GDN_HEADS = 8
GDN_DK = 128
GDN_CHUNK = 64
GDN_CONV = 4
DSW_GROUPS = ((128, 1), (512, 4), (2048, 16))
DSW_HEADS = 8
DSW_DH = 64
DSW_BLK = 128
REL_BUCKETS = 32
REL_MAX_DIST = 2048

ADAM_LR = 0.001
ADAM_B1 = 0.9
ADAM_B2 = 0.999
ADAM_EPS = 1e-08
ADAM_WD = 0.01
ADAM_STEP = 10

NEG_BIG = -1e30


def _params(*sem):
    return pltpu.CompilerParams(dimension_semantics=sem, vmem_limit_bytes=V7X_VMEM_LIMIT)


def _sigmoid(x):
    return 1.0 / (1.0 + jnp.exp(-x))


def _silu(x):
    return x * _sigmoid(x)


_DOT_DIMS = {
    "nn": (((1,), (0,)), ((), ())),
    "nt": (((1,), (1,)), ((), ())),
    "tn": (((0,), (0,)), ((), ())),
}


def _mm(a, b, *, mode, name, tm, tn, tk, out_dtype=F32, a_scale=None, out_scale=None, resid=None, a_silu=False):
    if mode == "nn":
        (M, K), N = a.shape, b.shape[1]
    elif mode == "nt":
        (M, K), N = a.shape, b.shape[0]
    else:
        (K, M), N = a.shape, b.shape[1]
    tm, tn, tk = min(tm, M), min(tn, N), min(tk, K)
    assert M % tm == 0 and N % tn == 0 and K % tk == 0, (name, M, N, K, tm, tn, tk)
    nk = K // tk

    def body(*refs):
        refs = list(refs)
        a_ref, b_ref = refs.pop(0), refs.pop(0)
        as_ref = refs.pop(0) if a_scale is not None else None
        os_ref = refs.pop(0) if out_scale is not None else None
        r_ref = refs.pop(0) if resid is not None else None
        o_ref = refs.pop(0)
        acc_ref = refs.pop(0) if nk > 1 else None

        av = a_ref[...]
        if a_silu:
            av = _silu(av.astype(F32))
        if as_ref is not None:
            av = av.astype(F32) * as_ref[...]
        part = lax.dot_general(av.astype(BF16), b_ref[...].astype(BF16), _DOT_DIMS[mode],
                               preferred_element_type=F32)

        def finish(r):
            if os_ref is not None:
                r = r * os_ref[...]
            if r_ref is not None:
                r = r + r_ref[...].astype(F32)
            o_ref[...] = r.astype(out_dtype)

        if nk == 1:
            finish(part)
        else:
            k = pl.program_id(2)

            @pl.when(k == 0)
            def _():
                acc_ref[...] = part

            @pl.when(k > 0)
            def _():
                acc_ref[...] += part

            @pl.when(k == nk - 1)
            def _():
                finish(acc_ref[...])

    if mode == "nn":
        a_spec = pl.BlockSpec((tm, tk), lambda i, j, k: (i, k))
        b_spec = pl.BlockSpec((tk, tn), lambda i, j, k: (k, j))
        as_spec = pl.BlockSpec((1, tk), lambda i, j, k: (0, k))
    elif mode == "nt":
        a_spec = pl.BlockSpec((tm, tk), lambda i, j, k: (i, k))
        b_spec = pl.BlockSpec((tn, tk), lambda i, j, k: (j, k))
        as_spec = pl.BlockSpec((1, tk), lambda i, j, k: (0, k))
    else:
        a_spec = pl.BlockSpec((tk, tm), lambda i, j, k: (k, i))
        b_spec = pl.BlockSpec((tk, tn), lambda i, j, k: (k, j))
        as_spec = None
    in_specs, args = [a_spec, b_spec], [a, b]
    if a_scale is not None:
        in_specs.append(as_spec)
        args.append(a_scale)
    if out_scale is not None:
        in_specs.append(pl.BlockSpec((1, tn), lambda i, j, k: (0, j)))
        args.append(out_scale)
    if resid is not None:
        in_specs.append(pl.BlockSpec((tm, tn), lambda i, j, k: (i, j)))
        args.append(resid)
    return pl.pallas_call(
        body, name=name, grid=(M // tm, N // tn, nk),
        in_specs=in_specs, out_specs=pl.BlockSpec((tm, tn), lambda i, j, k: (i, j)),
        out_shape=jax.ShapeDtypeStruct((M, N), out_dtype),
        scratch_shapes=[pltpu.VMEM((tm, tn), F32)] if nk > 1 else [],
        compiler_params=_params("parallel", "parallel", "arbitrary"),
    )(*args)


def _norm_mod_fwd(x, gain, sc, sh, *, name):
    S, D = x.shape
    tr = min(512, S)

    def body(x_ref, g_ref, sc_ref, sh_ref, h_ref):
        xv = x_ref[...]
        r = lax.rsqrt(jnp.mean(xv * xv, axis=-1, keepdims=True) + RMS_EPS)
        h_ref[...] = ((xv * r) * g_ref[...] * (1.0 + sc_ref[...]) + sh_ref[...]).astype(BF16)

    row = pl.BlockSpec((tr, D), lambda i: (i, 0))
    vec = pl.BlockSpec((1, D), lambda i: (0, 0))
    return pl.pallas_call(
        body, name=name, grid=(S // tr,), in_specs=[row, vec, vec, vec], out_specs=row,
        out_shape=jax.ShapeDtypeStruct((S, D), BF16), compiler_params=_params("parallel"),
    )(x, gain, sc, sh)


def _norm_mod_bwd(dh, x, dx_res, gain, sc, *, name):
    S, D = x.shape
    tr = min(256, S)
    n_steps = S // tr

    def body(dh_ref, x_ref, dxr_ref, g_ref, sc_ref, dx_ref, dsh_ref, dsc_ref, dgain_ref, acc_sh, acc_a):
        i = pl.program_id(0)
        xv = x_ref[...]
        r = lax.rsqrt(jnp.mean(xv * xv, axis=-1, keepdims=True) + RMS_EPS)
        n = xv * r
        dhv = dh_ref[...].astype(F32)
        dn = dhv * (g_ref[...] * (1.0 + sc_ref[...]))
        dx_ref[...] = dxr_ref[...] + r * (dn - n * jnp.mean(dn * n, axis=-1, keepdims=True))
        p_sh = jnp.sum(dhv, axis=0, keepdims=True)
        p_a = jnp.sum(dhv * n, axis=0, keepdims=True)

        @pl.when(i == 0)
        def _():
            acc_sh[...] = p_sh
            acc_a[...] = p_a

        @pl.when(i > 0)
        def _():
            acc_sh[...] += p_sh
            acc_a[...] += p_a

        @pl.when(i == n_steps - 1)
        def _():
            dsh_ref[...] = acc_sh[...]
            dsc_ref[...] = acc_a[...] * g_ref[...]
            dgain_ref[...] = acc_a[...] * (1.0 + sc_ref[...])

    row = pl.BlockSpec((tr, D), lambda i: (i, 0))
    vec = pl.BlockSpec((1, D), lambda i: (0, 0))
    vshape = jax.ShapeDtypeStruct((1, D), F32)
    return pl.pallas_call(
        body, name=name, grid=(n_steps,), in_specs=[row, row, row, vec, vec],
        out_specs=[row, vec, vec, vec],
        out_shape=[jax.ShapeDtypeStruct((S, D), F32), vshape, vshape, vshape],
        scratch_shapes=[pltpu.VMEM((1, D), F32), pltpu.VMEM((1, D), F32)],
        compiler_params=_params("arbitrary"),
    )(dh, x, dx_res, gain, sc)


def _wout_grad(gmat, w, gate, *, name):
    K, D = w.shape
    tr = min(256, K)
    n_steps = K // tr

    def body(g_ref, w_ref, gate_ref, dw_ref, dgate_ref, acc):
        i = pl.program_id(0)
        gv = g_ref[...]
        dw_ref[...] = gv * gate_ref[...]
        part = jnp.sum(gv * w_ref[...], axis=0, keepdims=True)

        @pl.when(i == 0)
        def _():
            acc[...] = part

        @pl.when(i > 0)
        def _():
            acc[...] += part

        @pl.when(i == n_steps - 1)
        def _():
            dgate_ref[...] = acc[...]

    row = pl.BlockSpec((tr, D), lambda i: (i, 0))
    vec = pl.BlockSpec((1, D), lambda i: (0, 0))
    return pl.pallas_call(
        body, name=name, grid=(n_steps,), in_specs=[row, row, vec], out_specs=[row, vec],
        out_shape=[jax.ShapeDtypeStruct((K, D), F32), jax.ShapeDtypeStruct((1, D), F32)],
        scratch_shapes=[pltpu.VMEM((1, D), F32)], compiler_params=_params("arbitrary"),
    )(gmat, w, gate)


def _swiglu_fwd(p, *, name):
    S, F2 = p.shape
    F = F2 // 2
    tr = min(256, S)

    def body(p_ref, a_ref):
        gate = p_ref[:, :F].astype(F32)
        up = p_ref[:, F:].astype(F32)
        a_ref[...] = (_silu(gate) * up).astype(BF16)

    return pl.pallas_call(
        body, name=name, grid=(S // tr,), in_specs=[pl.BlockSpec((tr, F2), lambda i: (i, 0))],
        out_specs=pl.BlockSpec((tr, F), lambda i: (i, 0)),
        out_shape=jax.ShapeDtypeStruct((S, F), BF16), compiler_params=_params("parallel"),
    )(p)


def _swiglu_bwd(da, p, *, name):
    S, F2 = p.shape
    F = F2 // 2
    tr = min(256, S)

    def body(da_ref, p_ref, dp_ref):
        gate = p_ref[:, :F].astype(F32)
        up = p_ref[:, F:].astype(F32)
        dav = da_ref[...].astype(F32)
        sg = _sigmoid(gate)
        dp_ref[:, :F] = (dav * up * (sg * (1.0 + gate * (1.0 - sg)))).astype(BF16)
        dp_ref[:, F:] = (dav * (gate * sg)).astype(BF16)

    return pl.pallas_call(
        body, name=name, grid=(S // tr,),
        in_specs=[pl.BlockSpec((tr, F), lambda i: (i, 0)), pl.BlockSpec((tr, F2), lambda i: (i, 0))],
        out_specs=pl.BlockSpec((tr, F2), lambda i: (i, 0)),
        out_shape=jax.ShapeDtypeStruct((S, F2), BF16), compiler_params=_params("parallel"),
    )(da, p)


def _loss_head(y, target, *, name):
    S, D = y.shape
    tr = min(512, S)
    n_steps = S // tr

    def body(y_ref, t_ref, dy_ref, sse_ref, acc):
        i = pl.program_id(0)
        e = y_ref[...] - t_ref[...]
        dy_ref[...] = e * (1.0 / D)
        part = jnp.sum(e * e, axis=0, keepdims=True)

        @pl.when(i == 0)
        def _():
            acc[...] = part

        @pl.when(i > 0)
        def _():
            acc[...] += part

        @pl.when(i == n_steps - 1)
        def _():
            sse_ref[...] = jnp.sum(acc[...], axis=1, keepdims=True)

    row = pl.BlockSpec((tr, D), lambda i: (i, 0))
    return pl.pallas_call(
        body, name=name, grid=(n_steps,), in_specs=[row, row],
        out_specs=[row, pl.BlockSpec((1, 1), lambda i: (0, 0))],
        out_shape=[jax.ShapeDtypeStruct((S, D), F32), jax.ShapeDtypeStruct((1, 1), F32)],
        scratch_shapes=[pltpu.VMEM((1, D), F32)], compiler_params=_params("arbitrary"),
    )(y, target)


def _adamw(w, g_parts, m, v, *, name):
    R, C = w.shape
    P = g_parts.shape[0]
    tr = _tile(R, max(8, 1024 * LANES // C))
    c1 = 1.0 / (1.0 - ADAM_B1 ** ADAM_STEP)
    c2 = 1.0 / (1.0 - ADAM_B2 ** ADAM_STEP)

    def body(w_ref, g_ref, m_ref, v_ref, go_ref, d_ref, mo_ref, vo_ref):
        g = g_ref[0]
        for q in range(1, P):
            g = g + g_ref[q]
        mn = ADAM_B1 * m_ref[...] + (1.0 - ADAM_B1) * g
        vn = ADAM_B2 * v_ref[...] + (1.0 - ADAM_B2) * (g * g)
        go_ref[...] = g
        mo_ref[...] = mn
        vo_ref[...] = vn
        d_ref[...] = -ADAM_LR * ((mn * c1) / (jnp.sqrt(vn * c2) + ADAM_EPS) + ADAM_WD * w_ref[...])

    row = pl.BlockSpec((tr, C), lambda i: (i, 0))
    shp = jax.ShapeDtypeStruct((R, C), F32)
    return pl.pallas_call(
        body, name=name, grid=(R // tr,),
        in_specs=[row, pl.BlockSpec((P, tr, C), lambda i: (0, i, 0)), row, row],
        out_specs=[row, row, row, row], out_shape=[shp, shp, shp, shp],
        compiler_params=_params("parallel"),
    )(w, g_parts, m, v)


_HALO = 8


def _conv_taps(buf, w_ref, rows):
    acc = None
    for j in range(GDN_CONV):
        term = buf[pl.ds(_HALO - (GDN_CONV - 1) + j, rows), :] * w_ref[j:j + 1, :]
        acc = term if acc is None else acc + term
    return acc


def _fill_conv_buf(buf, halo_ref, x_ref, rows, first):
    buf[0:_HALO, :] = jnp.where(first, 0.0, halo_ref[...])
    buf[_HALO:_HALO + rows, :] = x_ref[...]


_HM = 3 * GDN_DK


def _l2n(seg):
    return lax.rsqrt(jnp.sum(seg * seg, axis=-1, keepdims=True) + RMS_EPS)


def _gdn_prep_fwd(x, conv_w, *, name):
    S, C3 = x.shape
    CB = _HM
    RB = min(256, S)

    def body(x_ref, halo_ref, w_ref, o_ref, buf):
        i = pl.program_id(0)
        _fill_conv_buf(buf, halo_ref, x_ref, RB, i == 0)
        y = _silu(_conv_taps(buf, w_ref, RB))
        q, k = y[:, :GDN_DK], y[:, GDN_DK:2 * GDN_DK]
        o_ref[:, :GDN_DK] = q * (_l2n(q) * GDN_DK ** -0.5)
        o_ref[:, GDN_DK:2 * GDN_DK] = k * _l2n(k)
        o_ref[:, 2 * GDN_DK:] = y[:, 2 * GDN_DK:]

    hb = RB // _HALO
    return pl.pallas_call(
        body, name=name, grid=(S // RB, C3 // CB),
        in_specs=[pl.BlockSpec((RB, CB), lambda i, j: (i, j)),
                  pl.BlockSpec((_HALO, CB), lambda i, j: (jnp.maximum(i * hb - 1, 0), j)),
                  pl.BlockSpec((GDN_CONV, CB), lambda i, j: (0, j))],
        out_specs=pl.BlockSpec((RB, CB), lambda i, j: (i, j)),
        out_shape=jax.ShapeDtypeStruct((S, C3), F32),
        scratch_shapes=[pltpu.VMEM((RB + _HALO, CB), F32)],
        compiler_params=_params("parallel", "parallel"),
    )(x, x, conv_w)


def _gdn_prep_bwd_pre(dn, x, conv_w, *, name):
    S, C3 = x.shape
    CB = _HM
    RB = min(256, S)
    n_steps = S // RB

    def body(dn_ref, x_ref, halo_ref, w_ref, dc_ref, dw_ref, buf):
        i = pl.program_id(1)
        _fill_conv_buf(buf, halo_ref, x_ref, RB, i == 0)
        acc = _conv_taps(buf, w_ref, RB)
        sg = _sigmoid(acc)
        y = acc * sg
        dsilu = sg * (1.0 + acc * (1.0 - sg))
        for part, scale in ((0, GDN_DK ** -0.5), (1, 1.0)):
            sl = slice(part * GDN_DK, (part + 1) * GDN_DK)
            seg = y[:, sl]
            r = _l2n(seg)
            n = seg * r
            d = dn_ref[:, sl] * scale
            dc_ref[:, sl] = r * (d - n * jnp.sum(d * n, axis=-1, keepdims=True)) * dsilu[:, sl]
        dc_ref[:, 2 * GDN_DK:] = dn_ref[:, 2 * GDN_DK:] * dsilu[:, 2 * GDN_DK:]
        dc = dc_ref[...]
        parts = [jnp.sum(dc * buf[pl.ds(_HALO - (GDN_CONV - 1) + t, RB), :], axis=0, keepdims=True)
                 for t in range(GDN_CONV)]
        part = jnp.concatenate(parts + [jnp.zeros((8 - GDN_CONV, CB), F32)], axis=0)

        @pl.when(i == 0)
        def _():
            dw_ref[...] = part

        @pl.when(i > 0)
        def _():
            dw_ref[...] += part

    hb = RB // _HALO
    return pl.pallas_call(
        body, name=name, grid=(C3 // CB, n_steps),
        in_specs=[pl.BlockSpec((RB, CB), lambda j, i: (i, j)),
                  pl.BlockSpec((RB, CB), lambda j, i: (i, j)),
                  pl.BlockSpec((_HALO, CB), lambda j, i: (jnp.maximum(i * hb - 1, 0), j)),
                  pl.BlockSpec((GDN_CONV, CB), lambda j, i: (0, j))],
        out_specs=[pl.BlockSpec((RB, CB), lambda j, i: (i, j)),
                   pl.BlockSpec((8, CB), lambda j, i: (0, j))],
        out_shape=[jax.ShapeDtypeStruct((S, C3), F32), jax.ShapeDtypeStruct((8, C3), F32)],
        scratch_shapes=[pltpu.VMEM((RB + _HALO, CB), F32)],
        compiler_params=_params("parallel", "arbitrary"),
    )(dn, x, x, conv_w)


def _gdn_conv_bwd_x(dc, conv_w, *, name):
    S, C3 = dc.shape
    CB = _HM
    RB = min(256, S)
    n_steps = S // RB

    def body(dc_ref, halo_ref, w_ref, dx_ref, buf):
        i = pl.program_id(0)
        buf[0:RB, :] = dc_ref[...]
        buf[RB:RB + _HALO, :] = jnp.where(i == n_steps - 1, 0.0, halo_ref[...])
        acc = None
        for j in range(GDN_CONV):
            term = buf[pl.ds(GDN_CONV - 1 - j, RB), :] * w_ref[j:j + 1, :]
            acc = term if acc is None else acc + term
        dx_ref[...] = acc.astype(BF16)

    hb = RB // _HALO
    last = S // _HALO - 1
    return pl.pallas_call(
        body, name=name, grid=(n_steps, C3 // CB),
        in_specs=[pl.BlockSpec((RB, CB), lambda i, j: (i, j)),
                  pl.BlockSpec((_HALO, CB), lambda i, j: (jnp.minimum((i + 1) * hb, last), j)),
                  pl.BlockSpec((GDN_CONV, CB), lambda i, j: (0, j))],
        out_specs=pl.BlockSpec((RB, CB), lambda i, j: (i, j)),
        out_shape=jax.ShapeDtypeStruct((S, C3), BF16),
        scratch_shapes=[pltpu.VMEM((RB + _HALO, CB), F32)],
        compiler_params=_params("parallel", "parallel"),
    )(dc, dc, conv_w)


def _dot(a, b, dims="nn", exact=False):
    if exact:
        return lax.dot_general(a, b, _DOT_DIMS[dims], precision=lax.Precision.HIGHEST,
                               preferred_element_type=F32)
    return lax.dot_general(a.astype(BF16), b.astype(BF16), _DOT_DIMS[dims], preferred_element_type=F32)


def _softplus(x):
    return jnp.maximum(x, 0.0) + jnp.log(1.0 + jnp.exp(-jnp.abs(x)))


def _to_col(row, eye):
    return jnp.sum(jnp.where(eye, row, 0.0), axis=1, keepdims=True)


def _to_row(col, eye):
    return jnp.sum(jnp.where(eye, col, 0.0), axis=0, keepdims=True)


def _unit_lower_inverse(low, ri, ci):
    C = low.shape[0]
    x = jnp.where(ri == ci, 1.0, 0.0) - jnp.where((ri >> 1) == (ci >> 1), low, 0.0)
    m, sh = 2, 1
    while m < C:
        join = ((ri >> (sh + 1)) == (ci >> (sh + 1))) & (((ri >> sh) & 1) == 1) & (((ci >> sh) & 1) == 0)
        x = x - _dot(_dot(x, jnp.where(join, low, 0.0), exact=True), x, exact=True)
        m, sh = 2 * m, sh + 1
    return x


def _gdn_chunk_local(qkv, g_row, beta_row, ri, ci):
    eye, tril, strict = ri == ci, ri >= ci, ri > ci
    q, k, v = qkv[:, :GDN_DK], qkv[:, GDN_DK:2 * GDN_DK], qkv[:, 2 * GDN_DK:]
    g_col = _to_col(g_row, eye)
    gc_col = jnp.sum(jnp.where(tril, g_row, 0.0), axis=1, keepdims=True)
    gc_row = jnp.sum(jnp.where(ri <= ci, g_col, 0.0), axis=0, keepdims=True)
    g_last = jnp.sum(g_row, axis=1, keepdims=True)
    beta_col = _to_col(beta_row, eye)
    decay = jnp.where(tril, jnp.exp(jnp.minimum(gc_col - gc_row, 0.0)), 0.0)
    e_col = jnp.exp(gc_col)
    f_col = jnp.exp(g_last - gc_col)
    e_last = jnp.exp(g_last)
    kb = k * beta_col
    vb = v * beta_col
    low = jnp.where(strict, _dot(kb, k, "nt") * decay, 0.0)
    att = _dot(q, k, "nt") * decay
    return dict(q=q, k=k, v=v, beta_col=beta_col, decay=decay, e_col=e_col, f_col=f_col, e_last=e_last,
                kb=kb, vb=vb, low=low, att=att, eye=eye, strict=strict, tril=tril)


def _chunk_iotas():
    C = GDN_CHUNK
    return lax.broadcasted_iota(jnp.int32, (C, C), 0), lax.broadcasted_iota(jnp.int32, (C, C), 1)


def _gdn_chunk_fwd(qkv, ab, a_log, dt_bias, *, name):
    S = qkv.shape[0]
    H, C, DK = GDN_HEADS, GDN_CHUNK, GDN_DK
    RB = min(256, S)
    NCB, NB, NC = RB // C, S // RB, S // C

    def body(qkv_ref, a_ref, b_ref, alog_ref, dtb_ref, o_ref, st_ref, t_ref, state):
        nb = pl.program_id(1)

        @pl.when(nb == 0)
        def _():
            state[...] = jnp.zeros_like(state)

        ri, ci = _chunk_iotas()
        neg_a = -jnp.exp(alog_ref[...])
        for c in range(NCB):
            rows = pl.ds(c * C, C)
            g_row = neg_a * _softplus(a_ref[c] + dtb_ref[...])
            beta_row = _sigmoid(b_ref[c])
            L = _gdn_chunk_local(qkv_ref[rows, :], g_row, beta_row, ri, ci)
            tinv = _unit_lower_inverse(L["low"], ri, ci)
            u = _dot(tinv, L["vb"], exact=True)
            w = _dot(tinv, L["kb"] * L["e_col"], exact=True)
            st = state[...]
            vn = u - _dot(w, st)
            o_ref[rows, :] = _dot(L["q"] * L["e_col"], st) + _dot(L["att"], vn)
            st_ref[c] = st
            t_ref[c] = tinv
            state[...] = st * L["e_last"] + _dot(L["k"] * L["f_col"], vn, "tn")

    return pl.pallas_call(
        body, name=name, grid=(H, NB),
        in_specs=[pl.BlockSpec((RB, _HM), lambda h, n: (n, h)),
                  pl.BlockSpec((None, NCB, 1, C), lambda h, n: (h, n, 0, 0)),
                  pl.BlockSpec((None, NCB, 1, C), lambda h, n: (H + h, n, 0, 0)),
                  pl.BlockSpec((None, 1, 1), lambda h, n: (h, 0, 0)),
                  pl.BlockSpec((None, 1, 1), lambda h, n: (h, 0, 0))],
        out_specs=[pl.BlockSpec((RB, DK), lambda h, n: (n, h)),
                   pl.BlockSpec((None, NCB, DK, DK), lambda h, n: (h, n, 0, 0)),
                   pl.BlockSpec((None, NCB, C, C), lambda h, n: (h, n, 0, 0))],
        out_shape=[jax.ShapeDtypeStruct((S, H * DK), F32),
                   jax.ShapeDtypeStruct((H, NC, DK, DK), F32),
                   jax.ShapeDtypeStruct((H, NC, C, C), F32)],
        scratch_shapes=[pltpu.VMEM((DK, DK), F32)],
        compiler_params=_params("parallel", "arbitrary"),
    )(qkv, ab, ab, a_log, dt_bias)


def _gdn_chunk_bwd(qkv, ab, a_log, dt_bias, states, tinvs, do, *, name):
    S = qkv.shape[0]
    H, C, DK = GDN_HEADS, GDN_CHUNK, GDN_DK
    RB = min(256, S)
    NCB, NB, NC = RB // C, S // RB, S // C

    def body(qkv_ref, a_ref, b_ref, alog_ref, dtb_ref, st_ref, t_ref, do_ref,
             dqkv_ref, da_ref, db_ref, dalog_ref, ddtb_ref, dstate):
        nb = pl.program_id(1)

        @pl.when(nb == 0)
        def _():
            dstate[...] = jnp.zeros_like(dstate)
            dalog_ref[...] = jnp.zeros_like(dalog_ref)
            ddtb_ref[...] = jnp.zeros_like(ddtb_ref)

        ri, ci = _chunk_iotas()
        neg_a = -jnp.exp(alog_ref[...])
        for c in reversed(range(NCB)):
            rows = pl.ds(c * C, C)
            a_pre = a_ref[c] + dtb_ref[...]
            g_row = neg_a * _softplus(a_pre)
            beta_row = _sigmoid(b_ref[c])
            L = _gdn_chunk_local(qkv_ref[rows, :], g_row, beta_row, ri, ci)
            q, k, v, kb, vb = L["q"], L["k"], L["v"], L["kb"], L["vb"]
            e_col, f_col, e_last, decay = L["e_col"], L["f_col"], L["e_last"], L["decay"]
            eye, strict, tril = L["eye"], L["strict"], L["tril"]
            tinv = t_ref[c]
            st = st_ref[c]
            dst = dstate[...]
            dov = do_ref[rows, :]
            kbe = kb * e_col
            u = _dot(tinv, vb, exact=True)
            w = _dot(tinv, kbe, exact=True)
            vn = u - _dot(w, st)
            kf = k * f_col
            qe = q * e_col

            dvn = _dot(L["att"], dov, "tn") + _dot(kf, dst)
            datt = jnp.where(tril, _dot(dov, vn, "nt"), 0.0)
            dqe = _dot(dov, st, "nt")
            dstate[...] = dst * e_last + _dot(qe, dov, "tn") - _dot(w, dvn, "tn")
            de_last = jnp.sum(jnp.sum(dst * st, axis=1, keepdims=True), axis=0, keepdims=True)
            dkf = _dot(vn, dst, "nt")
            dw = -_dot(dvn, st, "nt")
            dt = _dot(dvn, vb, "nt") + _dot(dw, kbe, "nt")
            dvb = _dot(tinv, dvn, "tn", exact=True)
            dkbe = _dot(tinv, dw, "tn", exact=True)
            dlow = -jnp.where(strict, _dot(_dot(tinv, dt, "tn", exact=True), tinv, "nt", exact=True), 0.0)
            dkk = dlow * decay
            dqk = datt * decay
            dkb = _dot(dkk, k) + dkbe * e_col
            dk = _dot(dkk, kb, "tn") + _dot(dqk, q, "tn") + dkf * f_col + dkb * L["beta_col"]
            dq = _dot(dqk, k) + dqe * e_col
            dv = dvb * L["beta_col"]
            dqkv_ref[rows, :GDN_DK] = dq
            dqkv_ref[rows, GDN_DK:2 * GDN_DK] = dk
            dqkv_ref[rows, 2 * GDN_DK:] = dv

            dbeta_col = jnp.sum(k * dkb + v * dvb, axis=1, keepdims=True)
            pmat = dlow * L["low"] + datt * L["att"]
            df_col = jnp.sum(k * dkf, axis=1, keepdims=True) * f_col
            dgc_col = (jnp.sum(pmat, axis=1, keepdims=True)
                       + jnp.sum(q * dqe + kb * dkbe, axis=1, keepdims=True) * e_col - df_col)
            dgc_row = _to_row(dgc_col, eye) - jnp.sum(pmat, axis=0, keepdims=True)
            dg_last = jnp.sum(df_col, axis=0, keepdims=True) + de_last * e_last
            dgc_c = _to_col(dgc_row, eye)
            dg_row = jnp.sum(jnp.where(ri >= ci, dgc_c, 0.0), axis=0, keepdims=True) + dg_last
            dbeta_row = _to_row(dbeta_col, eye)

            da_row = dg_row * neg_a * _sigmoid(a_pre)
            da_ref[c] = da_row
            db_ref[c] = dbeta_row * beta_row * (1.0 - beta_row)
            dalog_ref[...] += jnp.sum(dg_row * g_row, axis=1, keepdims=True)
            ddtb_ref[...] += jnp.sum(da_row, axis=1, keepdims=True)

    rev = lambda n: NB - 1 - n
    return pl.pallas_call(
        body, name=name, grid=(H, NB),
        in_specs=[pl.BlockSpec((RB, _HM), lambda h, n: (rev(n), h)),
                  pl.BlockSpec((None, NCB, 1, C), lambda h, n: (h, rev(n), 0, 0)),
                  pl.BlockSpec((None, NCB, 1, C), lambda h, n: (H + h, rev(n), 0, 0)),
                  pl.BlockSpec((None, 1, 1), lambda h, n: (h, 0, 0)),
                  pl.BlockSpec((None, 1, 1), lambda h, n: (h, 0, 0)),
                  pl.BlockSpec((None, NCB, DK, DK), lambda h, n: (h, rev(n), 0, 0)),
                  pl.BlockSpec((None, NCB, C, C), lambda h, n: (h, rev(n), 0, 0)),
                  pl.BlockSpec((RB, DK), lambda h, n: (rev(n), h))],
        out_specs=[pl.BlockSpec((RB, _HM), lambda h, n: (rev(n), h)),
                   pl.BlockSpec((None, NCB, 1, C), lambda h, n: (h, rev(n), 0, 0)),
                   pl.BlockSpec((None, NCB, 1, C), lambda h, n: (h, rev(n), 0, 0)),
                   pl.BlockSpec((None, 1, 1), lambda h, n: (h, 0, 0)),
                   pl.BlockSpec((None, 1, 1), lambda h, n: (h, 0, 0))],
        out_shape=[jax.ShapeDtypeStruct((S, H * _HM), F32),
                   jax.ShapeDtypeStruct((H, NC, 1, C), F32),
                   jax.ShapeDtypeStruct((H, NC, 1, C), F32),
                   jax.ShapeDtypeStruct((H, 1, 1), F32),
                   jax.ShapeDtypeStruct((H, 1, 1), F32)],
        scratch_shapes=[pltpu.VMEM((DK, DK), F32)],
        compiler_params=_params("parallel", "arbitrary"),
    )(qkv, ab, ab, a_log, dt_bias, states, tinvs, do)


def _gdn_outnorm_fwd(o, z, gain, *, name):
    S, HV = o.shape
    RB = min(512, S)

    def body(o_ref, z_ref, g_ref, y_ref):
        ov = o_ref[...]
        r = lax.rsqrt(jnp.mean(ov * ov, axis=-1, keepdims=True) + RMS_EPS)
        y_ref[...] = (ov * r * g_ref[...] * _silu(z_ref[...].astype(F32))).astype(BF16)

    blk = pl.BlockSpec((RB, GDN_DK), lambda i, h: (i, h))
    return pl.pallas_call(
        body, name=name, grid=(S // RB, HV // GDN_DK),
        in_specs=[blk, blk, pl.BlockSpec((1, GDN_DK), lambda i, h: (0, 0))], out_specs=blk,
        out_shape=jax.ShapeDtypeStruct((S, HV), BF16), compiler_params=_params("parallel", "parallel"),
    )(o, z, gain)


def _gdn_outnorm_bwd(dy, o, z, gain, *, name):
    S, HV = o.shape
    RB = min(512, S)

    def body(dy_ref, o_ref, z_ref, g_ref, do_ref, dz_ref, dg_ref):
        first = (pl.program_id(0) == 0) & (pl.program_id(1) == 0)
        ov = o_ref[...]
        zv = z_ref[...].astype(F32)
        dyv = dy_ref[...].astype(F32)
        r = lax.rsqrt(jnp.mean(ov * ov, axis=-1, keepdims=True) + RMS_EPS)
        n = ov * r
        sg = _sigmoid(zv)
        dng = dyv * (zv * sg)
        dn = dng * g_ref[...]
        do_ref[...] = r * (dn - n * jnp.mean(dn * n, axis=-1, keepdims=True))
        dz_ref[...] = (dyv * (n * g_ref[...]) * (sg * (1.0 + zv * (1.0 - sg)))).astype(BF16)
        part = jnp.sum(dng * n, axis=0, keepdims=True)

        @pl.when(first)
        def _():
            dg_ref[...] = part

        @pl.when(jnp.logical_not(first))
        def _():
            dg_ref[...] += part

    blk = pl.BlockSpec((RB, GDN_DK), lambda i, h: (i, h))
    vec = pl.BlockSpec((1, GDN_DK), lambda i, h: (0, 0))
    return pl.pallas_call(
        body, name=name, grid=(S // RB, HV // GDN_DK),
        in_specs=[blk, blk, blk, vec], out_specs=[blk, blk, vec],
        out_shape=[jax.ShapeDtypeStruct((S, HV), F32), jax.ShapeDtypeStruct((S, HV), BF16),
                   jax.ShapeDtypeStruct((1, GDN_DK), F32)],
        compiler_params=_params("arbitrary", "arbitrary"),
    )(dy, o, z, gain)


def _rms64(x, gain):
    r = lax.rsqrt(jnp.mean(x * x, axis=-1, keepdims=True) + RMS_EPS)
    xh = x * r
    return xh, r, xh * gain


def _rms64_bwd(dy, xh, r, gain):
    dxh = dy * gain
    return r * (dxh - xh * jnp.mean(dxh * xh, axis=-1, keepdims=True))


def _run_len(g, nj):
    return jnp.int32(nj) >> (2 * g)


def _dsw_attn_fwd(q, k, v, bias, q_gain, k_gain, *, name):
    G, H, NJ, B, DH = q.shape

    def body(q_ref, kp_ref, kc_ref, vp_ref, vc_ref, bias_ref, qg_ref, kg_ref, o_ref, lse_ref):
        g, j = pl.program_id(0), pl.program_id(2)
        first = (j & (_run_len(g, NJ) - 1)) == 0
        _, _, qn = _rms64(q_ref[...], qg_ref[...] * DH ** -0.5)
        kn = jnp.concatenate([_rms64(kp_ref[...], kg_ref[...])[2], _rms64(kc_ref[...], kg_ref[...])[2]], axis=0)
        vv = jnp.concatenate([vp_ref[...], vc_ref[...]], axis=0)
        col = lax.broadcasted_iota(jnp.int32, (B, 2 * B), 1)
        s = _dot(qn, kn, "nt") + bias_ref[...]
        s = jnp.where(first & (col < B), NEG_BIG, s)
        m = jnp.max(s, axis=-1, keepdims=True)
        p = jnp.exp(s - m)
        l = jnp.sum(p, axis=-1, keepdims=True)
        o_ref[...] = _dot(p, vv) / l
        lse_ref[...] = jnp.broadcast_to(m + jnp.log(l), (B, DH))

    cur = pl.BlockSpec((None, None, None, B, DH), lambda g, h, j: (g, h, j, 0, 0))
    prev = pl.BlockSpec((None, None, None, B, DH), lambda g, h, j: (g, h, jnp.maximum(j - 1, 0), 0, 0))
    vec = pl.BlockSpec((1, DH), lambda g, h, j: (0, 0))
    shp = jax.ShapeDtypeStruct(q.shape, F32)
    return pl.pallas_call(
        body, name=name, grid=(G, H, NJ),
        in_specs=[cur, prev, cur, prev, cur,
                  pl.BlockSpec((None, None, B, 2 * B), lambda g, h, j: (g, h, 0, 0)), vec, vec],
        out_specs=[cur, cur], out_shape=[shp, shp],
        compiler_params=_params("parallel", "parallel", "parallel"),
    )(q, k, k, v, v, bias, q_gain, k_gain)


def _dsw_merge(o_g, lse_g, *, name):
    G, S, W = o_g.shape
    tr = min(512, S)

    def body(o_ref, l_ref, out_ref, lse_ref):
        m = l_ref[0]
        for g in range(1, G):
            m = jnp.maximum(m, l_ref[g])
        den = jnp.zeros_like(m)
        acc = jnp.zeros_like(m)
        for g in range(G):
            wg = jnp.exp(l_ref[g] - m)
            den = den + wg
            acc = acc + wg * o_ref[g]
        out_ref[...] = acc / den
        lse_ref[...] = m + jnp.log(den)

    blk3 = pl.BlockSpec((G, tr, W), lambda i: (0, i, 0))
    blk = pl.BlockSpec((tr, W), lambda i: (i, 0))
    shp = jax.ShapeDtypeStruct((S, W), F32)
    return pl.pallas_call(
        body, name=name, grid=(S // tr,), in_specs=[blk3, blk3], out_specs=[blk, blk],
        out_shape=[shp, shp], compiler_params=_params("parallel"),
    )(o_g, lse_g)


def _dsw_attn_bwd(q, k, v, o, lse, do, bias, q_gain, k_gain, *, name):
    G, H, NJ, B, DH = q.shape
    scale = DH ** -0.5

    def body(q_ref, qn_ref, kp_ref, kc_ref, vp_ref, vc_ref, o_ref, on_ref, l_ref, ln_ref, do_ref, don_ref,
             bias_ref, qg_ref, kg_ref, dq_ref, dk_ref, dv_ref, db_ref, dqg_ref, dkg_ref):
        g, h, j = pl.program_id(0), pl.program_id(1), pl.program_id(2)
        run = _run_len(g, NJ)
        first = (j & (run - 1)) == 0
        has_next = ((j + 1) & (run - 1)) != 0
        qg, kg = qg_ref[...] * scale, kg_ref[...]
        col = lax.broadcasted_iota(jnp.int32, (B, 2 * B), 1)

        qh, rq, qn = _rms64(q_ref[...], qg)
        kph, _, kpn = _rms64(kp_ref[...], kg)
        kch, rk, kcn = _rms64(kc_ref[...], kg)
        kn = jnp.concatenate([kpn, kcn], axis=0)
        vv = jnp.concatenate([vp_ref[...], vc_ref[...]], axis=0)
        dov = do_ref[...]
        s = _dot(qn, kn, "nt") + bias_ref[...]
        s = jnp.where(first & (col < B), NEG_BIG, s)
        p = jnp.exp(s - l_ref[:, 0:1])
        delta = jnp.sum(dov * o_ref[...], axis=-1, keepdims=True)
        ds = p * (_dot(dov, vv, "nt") - delta)
        dqn = _dot(ds, kn)
        dq_ref[...] = _rms64_bwd(dqn, qh, rq, qg)
        dqg_part = jnp.sum(dqn * qh, axis=0, keepdims=True) * scale
        dv = _dot(p[:, B:], dov, "tn")
        dkn = _dot(ds[:, B:], qn, "tn")

        _, _, qn2 = _rms64(qn_ref[...], qg)
        dov2 = don_ref[...]
        s2 = _dot(qn2, kcn, "nt") + bias_ref[:, :B]
        p2 = jnp.where(has_next, jnp.exp(s2 - ln_ref[:, 0:1]), 0.0)
        delta2 = jnp.sum(dov2 * on_ref[...], axis=-1, keepdims=True)
        ds2 = p2 * (_dot(dov2, vc_ref[...], "nt") - delta2)
        dv_ref[...] = dv + _dot(p2, dov2, "tn")
        dkn = dkn + _dot(ds2, qn2, "tn")
        dk_ref[...] = _rms64_bwd(dkn, kch, rk, kg)
        dkg_part = jnp.sum(dkn * kch, axis=0, keepdims=True)

        @pl.when(j == 0)
        def _():
            db_ref[...] = ds

        @pl.when(j > 0)
        def _():
            db_ref[...] += ds

        start = (g == 0) & (h == 0) & (j == 0)

        @pl.when(start)
        def _():
            dqg_ref[...] = dqg_part
            dkg_ref[...] = dkg_part

        @pl.when(jnp.logical_not(start))
        def _():
            dqg_ref[...] += dqg_part
            dkg_ref[...] += dkg_part

    cur = pl.BlockSpec((None, None, None, B, DH), lambda g, h, j: (g, h, j, 0, 0))
    prev = pl.BlockSpec((None, None, None, B, DH), lambda g, h, j: (g, h, jnp.maximum(j - 1, 0), 0, 0))
    nxt = pl.BlockSpec((None, None, None, B, DH), lambda g, h, j: (g, h, jnp.minimum(j + 1, NJ - 1), 0, 0))
    vec = pl.BlockSpec((1, DH), lambda g, h, j: (0, 0))
    bspec = pl.BlockSpec((None, None, B, 2 * B), lambda g, h, j: (g, h, 0, 0))
    shp = jax.ShapeDtypeStruct(q.shape, F32)
    vshp = jax.ShapeDtypeStruct((1, DH), F32)
    return pl.pallas_call(
        body, name=name, grid=(G, H, NJ),
        in_specs=[cur, nxt, prev, cur, prev, cur, cur, nxt, cur, nxt, cur, nxt, bspec, vec, vec],
        out_specs=[cur, cur, cur, bspec, vec, vec],
        out_shape=[shp, shp, shp, jax.ShapeDtypeStruct(bias.shape, F32), vshp, vshp],
        compiler_params=_params("arbitrary", "arbitrary", "arbitrary"),
    )(q, q, k, k, v, v, o, o, lse, lse, do, do, bias, q_gain, k_gain)


def _to_blocks(a, d):
    n = a.shape[0] // (DSW_BLK * d)
    a = a.reshape(n, DSW_BLK, d, DSW_HEADS, DSW_DH)
    return jnp.transpose(a, (3, 2, 0, 1, 4)).reshape(DSW_HEADS, d * n, DSW_BLK, DSW_DH)


def _from_blocks(a, d):
    n = a.shape[1] // d
    a = a.reshape(DSW_HEADS, d, n, DSW_BLK, DSW_DH)
    return jnp.transpose(a, (2, 3, 1, 0, 4)).reshape(n * DSW_BLK * d, DSW_HEADS * DSW_DH)


def _group_blocks(a):
    w = DSW_HEADS * DSW_DH
    return jnp.stack([_to_blocks(a[:, g * w:(g + 1) * w], d) for g, (_, d) in enumerate(DSW_GROUPS)])


def _same_blocks(a):
    return jnp.stack([_to_blocks(a, d) for _, d in DSW_GROUPS])


def _ungroup(a):
    return [_from_blocks(a[g], d) for g, (_, d) in enumerate(DSW_GROUPS)]


def _t5_bucket(dist):
    max_exact = REL_BUCKETS // 2
    scaled = jnp.log(jnp.maximum(dist, 1).astype(F32) / max_exact) / math.log(REL_MAX_DIST / max_exact)
    large = jnp.minimum(max_exact + (scaled * (REL_BUCKETS - max_exact)).astype(jnp.int32), REL_BUCKETS - 1)
    return jnp.where(dist < max_exact, dist, large)


def _dsw_band():
    dist = (jnp.arange(DSW_BLK)[:, None] + DSW_BLK) - jnp.arange(2 * DSW_BLK)[None, :]
    return dist, (dist >= 0) & (dist <= DSW_BLK)


def _dsw_bias(rel_bias):
    dist, band = _dsw_band()
    out = []
    for g, (_, d) in enumerate(DSW_GROUPS):
        tab = rel_bias[:, g * DSW_HEADS:(g + 1) * DSW_HEADS][_t5_bucket(jnp.maximum(dist, 0) * d)]
        out.append(jnp.where(band[None], jnp.transpose(tab, (2, 0, 1)), NEG_BIG))
    return jnp.stack(out)


def _dsw_bucket_onehot():
    dist, band = _dsw_band()
    out = []
    for _, d in DSW_GROUPS:
        hot = jax.nn.one_hot(_t5_bucket(jnp.maximum(dist, 0) * d), LANES, dtype=BF16)
        out.append(jnp.where(band[..., None], hot, 0).reshape(-1, LANES))
    return jnp.stack(out)


def _exchange(send, *, gather, name):
    R, C = send.shape[-2:]

    def body(src_ref, dst_ref, send_sems, recv_sems, local_sem):
        x, y, c = lax.axis_index("x"), lax.axis_index("y"), lax.axis_index("c")
        me = 4 * x + 2 * y + c
        mine = pltpu.make_async_copy(src_ref if gather else src_ref.at[me], dst_ref.at[me], local_sem)
        mine.start()
        copies = []
        for rel in range(1, N_DEV):
            px = 1 - x if rel & 4 else x
            py = 1 - y if rel & 2 else y
            pc = 1 - c if rel & 1 else c
            peer = 4 * px + 2 * py + pc
            cp = pltpu.make_async_remote_copy(
                src_ref=src_ref if gather else src_ref.at[peer], dst_ref=dst_ref.at[me],
                send_sem=send_sems.at[rel - 1], recv_sem=recv_sems.at[rel - 1],
                device_id=(px, py, pc), device_id_type=pl.DeviceIdType.MESH)
            cp.start()
            copies.append(cp)
        for cp in copies:
            cp.wait()
        mine.wait()

    return pl.pallas_call(
        body, name=name,
        in_specs=[pl.BlockSpec(memory_space=pl.ANY)], out_specs=pl.BlockSpec(memory_space=pl.ANY),
        out_shape=jax.ShapeDtypeStruct((N_DEV, R, C), send.dtype),
        scratch_shapes=[pltpu.SemaphoreType.DMA((N_DEV - 1,)), pltpu.SemaphoreType.DMA((N_DEV - 1,)),
                        pltpu.SemaphoreType.DMA(())],
    )(send)


_BIG = ("w_ffn_in", "w_ffn_out", "gdn_w_in", "gdn_conv", "gdn_w_out", "dsw_w_in", "dsw_w_out")
_SHARD_AXIS = {"w_ffn_in": 2, "w_ffn_out": 1, "gdn_w_in": 2, "gdn_conv": 2, "gdn_w_out": 1, "dsw_w_in": 2,
               "dsw_w_out": 2}
_SMALL = ("b_ada", "norm_mix", "norm_ffn", "gdn_a_log", "gdn_dt_bias", "gdn_out_norm", "dsw_q_norm",
          "dsw_k_norm", "rel_bias")
_ROW_ALIGN = 16
_BIG_ALIGN = 1024


def _ceil_to(n, m):
    return -(-n // m) * m


def _seg_rows(shape):
    return _ceil_to(_ceil_to(int(np.prod(shape)), LANES) // LANES, _ROW_ALIGN)


def _pack(arrs, total_align):
    lead = arrs[0][1]
    segs = []
    for a, nlead in arrs:
        assert nlead == lead
        bshape = a.shape[:nlead]
        n = int(np.prod(a.shape[nlead:]))
        rows = _seg_rows(a.shape[nlead:])
        flat = a.reshape(bshape + (n,))
        flat = jnp.pad(flat, [(0, 0)] * nlead + [(0, rows * LANES - n)])
        segs.append(flat.reshape(bshape + (rows, LANES)))
    buf = jnp.concatenate(segs, axis=lead)
    total = _ceil_to(buf.shape[lead], total_align)
    return jnp.pad(buf, [(0, 0)] * lead + [(0, total - buf.shape[lead]), (0, 0)])


def _unpack(buf, shapes, nlead):
    out, off = [], 0
    for shp in shapes:
        n, rows = int(np.prod(shp)), _seg_rows(shp)
        seg = lax.slice_in_dim(buf, off, off + rows, axis=nlead)
        seg = seg.reshape(buf.shape[:nlead] + (rows * LANES,))[..., :n]
        out.append(seg.reshape(buf.shape[:nlead] + tuple(shp)))
        off += rows
    return out


def _to_natural(g, axis):
    n, L, r, c = g.shape
    if axis == 2:
        return jnp.transpose(g, (1, 2, 0, 3)).reshape(L, r, n * c)
    return jnp.transpose(g, (1, 0, 2, 3)).reshape(L, n * r, c)


def _to_blocked(w, axis):
    L, R, C = w.shape
    if axis == 2:
        return jnp.transpose(w.reshape(L, R, N_DEV, C // N_DEV), (2, 0, 1, 3))
    return jnp.transpose(w.reshape(L, N_DEV, R // N_DEV, C), (1, 0, 2, 3))


def _hm(a):
    lead = a.shape[:-1]
    return jnp.swapaxes(a.reshape(lead + (3, GDN_HEADS, GDN_DK)), -3, -2).reshape(lead + (3 * GDN_HEADS * GDN_DK,))


def _un_hm(a):
    lead = a.shape[:-1]
    return jnp.swapaxes(a.reshape(lead + (GDN_HEADS, 3, GDN_DK)), -3, -2).reshape(lead + (3 * GDN_HEADS * GDN_DK,))


_TILES = (1536, 1408, 1024, 768, 704, 512, 384, 256, 128, 64, 32, 16, 8)


def _tile(n, cap):
    for t in _TILES:
        if t <= cap and n % t == 0:
            return t
    return n


def _mm_auto(a, b, mode, name, **kw):
    if mode == "tn":
        (K, M), N = a.shape, b.shape[1]
        tm, tn, tk = _tile(M, 1408), _tile(N, 512), _tile(K, 512)
    else:
        M, K = a.shape
        N = b.shape[1] if mode == "nn" else b.shape[0]
        tm, tn, tk = _tile(M, 512), _tile(N, 1536), _tile(K, 1408)
    return _mm(a, b, mode=mode, name=name, tm=tm, tn=tn, tk=tk, **kw)


def _row(v):
    return v.reshape(1, -1)


def _ffn_fwd(x, mod, gain, w_in, w_out, tag):
    sh, sc, gate = mod
    h = _norm_mod_fwd(x, gain, sc, sh, name=f"ffn_norm_{tag}")
    p = _mm_auto(h, w_in, "nn", f"ffn_in_{tag}", out_dtype=BF16)
    a = _swiglu_fwd(p, name=f"ffn_act_{tag}")
    y = _mm_auto(a, w_out, "nn", f"ffn_out_{tag}", out_scale=gate, resid=x)
    return y, (x, h, p, a)


def _ffn_bwd(dy, saved, mod, gain, w_in, w_out, tag):
    sh, sc, gate = mod
    x, h, p, a = saved
    gmat = _mm_auto(a, dy, "tn", f"ffn_out_g_{tag}")
    dw_out, dgate = _wout_grad(gmat, w_out, gate, name=f"ffn_out_dw_{tag}")
    da = _mm_auto(dy, w_out, "nt", f"ffn_out_dx_{tag}", a_scale=gate, out_dtype=BF16)
    dp = _swiglu_bwd(da, p, name=f"ffn_act_bwd_{tag}")
    dw_in = _mm_auto(h, dp, "tn", f"ffn_in_dw_{tag}")
    dh = _mm_auto(dp, w_in, "nt", f"ffn_in_dx_{tag}")
    dx, dsh, dsc, dgain = _norm_mod_bwd(dh, x, dy, gain, sc, name=f"ffn_norm_bwd_{tag}")
    return dx, dict(w_in=dw_in, w_out=dw_out, gain=dgain, mod=(dsh, dsc, dgate))


def _gdn_fwd(x, mod, gain, W):
    sh, sc, gate = mod
    S = x.shape[0]
    h = _norm_mod_fwd(x, gain, sc, sh, name="gdn_norm")
    pq = _mm_auto(h, W["gdn_qkv"], "nn", "gdn_in_qkv")
    z = _mm_auto(h, W["gdn_z"], "nn", "gdn_in_z")
    ab = _mm_auto(h, W["gdn_ab"], "nn", "gdn_in_ab")
    qkvn = _gdn_prep_fwd(pq, W["gdn_conv"], name="gdn_prep")
    ab4 = jnp.transpose(ab[:, :2 * GDN_HEADS]).reshape(2 * GDN_HEADS, S // GDN_CHUNK, 1, GDN_CHUNK)
    o, states, tinvs = _gdn_chunk_fwd(qkvn, ab4, W["gdn_a_log"], W["gdn_dt_bias"], name="gdn_chunk")
    o2 = _gdn_outnorm_fwd(o, z, W["gdn_out_norm"], name="gdn_outnorm")
    y = _mm_auto(o2, W["gdn_out"], "nn", "gdn_out", out_scale=gate, resid=x)
    return y, (x, h, pq, z, qkvn, ab4, o, states, tinvs, o2)


def _gdn_bwd(dy, saved, mod, gain, W):
    sh, sc, gate = mod
    x, h, pq, z, qkvn, ab4, o, states, tinvs, o2 = saved
    S = x.shape[0]
    gmat = _mm_auto(o2, dy, "tn", "gdn_out_g")
    dw_out, dgate = _wout_grad(gmat, W["gdn_out"], gate, name="gdn_out_dw")
    do2 = _mm_auto(dy, W["gdn_out"], "nt", "gdn_out_dx", a_scale=gate)
    do, dz, dout_norm = _gdn_outnorm_bwd(do2, o, z, W["gdn_out_norm"], name="gdn_outnorm_bwd")
    dqkvn, da4, db4, da_log, ddt_bias = _gdn_chunk_bwd(
        qkvn, ab4, W["gdn_a_log"], W["gdn_dt_bias"], states, tinvs, do, name="gdn_chunk_bwd")
    dc, dconv8 = _gdn_prep_bwd_pre(dqkvn, pq, W["gdn_conv"], name="gdn_prep_bwd")
    dpq = _gdn_conv_bwd_x(dc, W["gdn_conv"], name="gdn_conv_bwd")
    dab = jnp.transpose(jnp.concatenate([da4, db4], axis=0).reshape(2 * GDN_HEADS, S))
    dab = jnp.pad(dab, ((0, 0), (0, LANES - 2 * GDN_HEADS))).astype(BF16)
    dw_qkv = _mm_auto(h, dpq, "tn", "gdn_in_qkv_dw")
    dw_z = _mm_auto(h, dz, "tn", "gdn_in_z_dw")
    dw_ab = _mm_auto(h, dab, "tn", "gdn_in_ab_dw")
    dh = _mm_auto(dpq, W["gdn_qkv"], "nt", "gdn_in_qkv_dx")
    dh = _mm_auto(dz, W["gdn_z"], "nt", "gdn_in_z_dx", resid=dh)
    dh = _mm_auto(dab, W["gdn_ab"], "nt", "gdn_in_ab_dx", resid=dh)
    dx, dsh, dsc, dgain = _norm_mod_bwd(dh, x, dy, gain, sc, name="gdn_norm_bwd")
    dw_in = jnp.concatenate([_un_hm(dw_qkv), dw_z, dw_ab[:, :2 * GDN_HEADS]], axis=1)
    return dx, dict(gdn_w_in=dw_in, gdn_conv=_un_hm(dconv8[:GDN_CONV]), gdn_w_out=dw_out, gdn_out_norm=dout_norm,
                    gdn_a_log=da_log.reshape(1, GDN_HEADS), gdn_dt_bias=ddt_bias.reshape(1, GDN_HEADS),
                    gain=dgain, mod=(dsh, dsc, dgate))


def _dsw_fwd(x, mod, gain, W):
    sh, sc, gate = mod
    h = _norm_mod_fwd(x, gain, sc, sh, name="dsw_norm")
    qb, kb, vb = (_group_blocks(_mm_auto(h, W[n], "nn", f"dsw_in_{n[-1]}")) for n in ("dsw_q", "dsw_k", "dsw_v"))
    og, lg = _dsw_attn_fwd(qb, kb, vb, W["dsw_bias"], W["dsw_q_norm"], W["dsw_k_norm"], name="dsw_attn")
    o, lse = _dsw_merge(jnp.stack(_ungroup(og)), jnp.stack(_ungroup(lg)), name="dsw_merge")
    y = _mm_auto(o, W["dsw_out"], "nn", "dsw_out", out_scale=gate, resid=x)
    return y, (x, h, qb, kb, vb, o, lse)


def _dsw_bwd(dy, saved, mod, gain, W):
    sh, sc, gate = mod
    x, h, qb, kb, vb, o, lse = saved
    gmat = _mm_auto(o, dy, "tn", "dsw_out_g")
    dw_out, dgate = _wout_grad(gmat, W["dsw_out"], gate, name="dsw_out_dw")
    do = _mm_auto(dy, W["dsw_out"], "nt", "dsw_out_dx", a_scale=gate)
    dqb, dkb, dvb, dbias, dq_norm, dk_norm = _dsw_attn_bwd(
        qb, kb, vb, _same_blocks(o), _same_blocks(lse), _same_blocks(do), W["dsw_bias"],
        W["dsw_q_norm"], W["dsw_k_norm"], name="dsw_attn_bwd")
    dws, dh = [], None
    for n, blocks in (("dsw_q", dqb), ("dsw_k", dkb), ("dsw_v", dvb)):
        d = jnp.concatenate(_ungroup(blocks), axis=1).astype(BF16)
        dws.append(_mm_auto(h, d, "tn", f"dsw_in_{n[-1]}_dw"))
        dh = _mm_auto(d, W[n], "nt", f"dsw_in_{n[-1]}_dx", **({} if dh is None else {"resid": dh}))
    dx, dsh, dsc, dgain = _norm_mod_bwd(dh, x, dy, gain, sc, name="dsw_norm_bwd")
    hot = _dsw_bucket_onehot()
    G = len(DSW_GROUPS)
    drel = [_mm_auto(dbias[g].reshape(DSW_HEADS, -1), hot[g], "nn", f"dsw_rel_bias_{g}")[:, :REL_BUCKETS]
            for g in range(G)]
    return dx, dict(dsw_w_in=jnp.concatenate(dws, axis=1), dsw_w_out=dw_out, dsw_q_norm=dq_norm,
                    dsw_k_norm=dk_norm, rel_bias=jnp.transpose(jnp.concatenate(drel, axis=0)),
                    gain=dgain, mod=(dsh, dsc, dgate))


def _local_step(x, target, mod, W):
    mods = [[_row(mod[l, i]) for i in range(6)] for l in range(2)]
    nmix = [_row(W["norm_mix"][l]) for l in range(2)]
    nffn = [_row(W["norm_ffn"][l]) for l in range(2)]
    x1, s_gdn = _gdn_fwd(x, mods[0][:3], nmix[0], W)
    x2, s_f0 = _ffn_fwd(x1, mods[0][3:], nffn[0], W["w_ffn_in"][0], W["w_ffn_out"][0], "0")
    x3, s_dsw = _dsw_fwd(x2, mods[1][:3], nmix[1], W)
    x4, s_f1 = _ffn_fwd(x3, mods[1][3:], nffn[1], W["w_ffn_in"][1], W["w_ffn_out"][1], "1")
    dx4, sse = _loss_head(x4, target, name="loss_head")
    dx3, g_f1 = _ffn_bwd(dx4, s_f1, mods[1][3:], nffn[1], W["w_ffn_in"][1], W["w_ffn_out"][1], "1")
    dx2, g_dsw = _dsw_bwd(dx3, s_dsw, mods[1][:3], nmix[1], W)
    dx1, g_f0 = _ffn_bwd(dx2, s_f0, mods[0][3:], nffn[0], W["w_ffn_in"][0], W["w_ffn_out"][0], "0")
    dx0, g_gdn = _gdn_bwd(dx1, s_gdn, mods[0][:3], nmix[0], W)
    dmod = jnp.stack([jnp.concatenate(list(g_gdn["mod"]) + list(g_f0["mod"]), axis=0),
                      jnp.concatenate(list(g_dsw["mod"]) + list(g_f1["mod"]), axis=0)])
    grads = dict(
        w_ffn_in=jnp.stack([g_f0["w_in"], g_f1["w_in"]]), w_ffn_out=jnp.stack([g_f0["w_out"], g_f1["w_out"]]),
        norm_mix=jnp.concatenate([g_gdn["gain"], g_dsw["gain"]], axis=0),
        norm_ffn=jnp.concatenate([g_f0["gain"], g_f1["gain"]], axis=0),
        gdn_w_in=g_gdn["gdn_w_in"][None], gdn_conv=g_gdn["gdn_conv"][None], gdn_w_out=g_gdn["gdn_w_out"][None],
        gdn_out_norm=g_gdn["gdn_out_norm"], gdn_a_log=g_gdn["gdn_a_log"], gdn_dt_bias=g_gdn["gdn_dt_bias"],
        dsw_w_in=g_dsw["dsw_w_in"][None], dsw_w_out=g_dsw["dsw_w_out"][None],
        dsw_q_norm=g_dsw["dsw_q_norm"], dsw_k_norm=g_dsw["dsw_k_norm"], rel_bias=g_dsw["rel_bias"])
    return sse, dx0, grads, dmod


def _prepare_weights(full, small):
    gw = full["gdn_w_in"][0]
    hk3 = 3 * GDN_HEADS * GDN_DK
    di = full["dsw_w_in"][0]
    dq = di.shape[1] // 3
    return dict(
        w_ffn_in=full["w_ffn_in"], w_ffn_out=full["w_ffn_out"],
        gdn_qkv=_hm(gw[:, :hk3]), gdn_z=gw[:, hk3:hk3 + GDN_HEADS * GDN_DK],
        gdn_ab=jnp.pad(gw[:, hk3 + GDN_HEADS * GDN_DK:], ((0, 0), (0, LANES - 2 * GDN_HEADS))),
        gdn_conv=_hm(full["gdn_conv"][0]), gdn_out=full["gdn_w_out"][0],
        dsw_q=di[:, :dq], dsw_k=di[:, dq:2 * dq], dsw_v=di[:, 2 * dq:], dsw_out=full["dsw_w_out"][0],
        norm_mix=small["norm_mix"], norm_ffn=small["norm_ffn"],
        gdn_a_log=small["gdn_a_log"].reshape(GDN_HEADS, 1, 1), gdn_dt_bias=small["gdn_dt_bias"].reshape(GDN_HEADS, 1, 1),
        gdn_out_norm=small["gdn_out_norm"], dsw_q_norm=small["dsw_q_norm"], dsw_k_norm=small["dsw_k_norm"],
        dsw_bias=_dsw_bias(small["rel_bias"]))


_W_NAMES = ("w_ada", "b_ada", "norm_mix", "norm_ffn", "w_ffn_in", "w_ffn_out", "gdn_w_in", "gdn_conv",
            "gdn_a_log", "gdn_dt_bias", "gdn_out_norm", "gdn_w_out", "dsw_w_in", "dsw_q_norm", "dsw_k_norm",
            "dsw_w_out", "rel_bias")
_PAD_BATCH = 16


def _pad_rows(a, rows):
    return jnp.pad(a, ((0, rows - a.shape[0]), (0, 0)))


def kernel(x, c, w_ada, b_ada, norm_mix, norm_ffn, w_ffn_in, w_ffn_out, gdn_w_in, gdn_conv, gdn_a_log, gdn_dt_bias, gdn_out_norm, gdn_w_out, dsw_w_in, dsw_q_norm, dsw_k_norm, dsw_w_out, rel_bias, loss_target, m_w_ada, m_b_ada, m_norm_mix, m_norm_ffn, m_w_ffn_in, m_w_ffn_out, m_gdn_w_in, m_gdn_conv, m_gdn_a_log, m_gdn_dt_bias, m_gdn_out_norm, m_gdn_w_out, m_dsw_w_in, m_dsw_q_norm, m_dsw_k_norm, m_dsw_w_out, m_rel_bias, v_w_ada, v_b_ada, v_norm_mix, v_norm_ffn, v_w_ffn_in, v_w_ffn_out, v_gdn_w_in, v_gdn_conv, v_gdn_a_log, v_gdn_dt_bias, v_gdn_out_norm, v_gdn_w_out, v_dsw_w_in, v_dsw_q_norm, v_dsw_k_norm, v_dsw_w_out, v_rel_bias):
    w = dict(zip(_W_NAMES, (w_ada, b_ada, norm_mix, norm_ffn, w_ffn_in, w_ffn_out, gdn_w_in, gdn_conv, gdn_a_log,
                            gdn_dt_bias, gdn_out_norm, gdn_w_out, dsw_w_in, dsw_q_norm, dsw_k_norm, dsw_w_out,
                            rel_bias)))
    m = dict(zip(_W_NAMES, (m_w_ada, m_b_ada, m_norm_mix, m_norm_ffn, m_w_ffn_in, m_w_ffn_out, m_gdn_w_in,
                            m_gdn_conv, m_gdn_a_log, m_gdn_dt_bias, m_gdn_out_norm, m_gdn_w_out, m_dsw_w_in,
                            m_dsw_q_norm, m_dsw_k_norm, m_dsw_w_out, m_rel_bias)))
    v = dict(zip(_W_NAMES, (v_w_ada, v_b_ada, v_norm_mix, v_norm_ffn, v_w_ffn_in, v_w_ffn_out, v_gdn_w_in,
                            v_gdn_conv, v_gdn_a_log, v_gdn_dt_bias, v_gdn_out_norm, v_gdn_w_out, v_dsw_w_in,
                            v_dsw_q_norm, v_dsw_k_norm, v_dsw_w_out, v_rel_bias)))
    D = x.shape[-1]
    n_layers, _, ada_cols = w_ada.shape

    c_all = _exchange(c.reshape(D // LANES, LANES), gather=True, name="gather_cond").reshape(N_DEV, D)
    c_pad = _pad_rows(c_all, _PAD_BATCH)
    proj = [_mm(c_pad, w_ada[l], mode="nn", name=f"ada_proj_{l}", tm=_PAD_BATCH, tn=ada_cols, tk=D, a_silu=True)
            for l in range(n_layers)]
    mod_send = _pack([(jnp.stack([p[:N_DEV] for p in proj], axis=1), 1)], _ROW_ALIGN)
    mod_recv = _exchange(mod_send, gather=False, name="scatter_mod")
    mod = _unpack(mod_recv, [(n_layers, ada_cols)], 1)[0]
    mod = jnp.transpose(mod, (1, 0, 2)).reshape(n_layers, N_DEV * ada_cols) + b_ada
    mod = mod.reshape(n_layers, 6, D)

    conv_hi = gdn_conv.astype(BF16)
    conv_lo = (gdn_conv - conv_hi.astype(F32)).astype(BF16)
    w_send = _pack([(conv_hi if n == "gdn_conv" else w[n].astype(BF16), 0) for n in _BIG] + [(conv_lo, 0)],
                   _ROW_ALIGN)
    w_all = _exchange(w_send, gather=True, name="gather_weights")
    parts = _unpack(w_all, [w[n].shape for n in _BIG] + [gdn_conv.shape], 1)
    full = {n: _to_natural(parts[i], _SHARD_AXIS[n]) for i, n in enumerate(_BIG)}
    full["gdn_conv"] = full["gdn_conv"].astype(F32) + _to_natural(parts[-1], _SHARD_AXIS["gdn_conv"]).astype(F32)
    W = _prepare_weights(full, {n: w[n] for n in _SMALL})

    sse, grad_x, grads, dmod = _local_step(x[0], loss_target[0], mod, W)
    loss = lax.psum(0.5 * sse[0, 0] / D, ("x", "y", "c"))

    grads["b_ada"] = dmod.reshape(n_layers, 6 * D)
    big_send = _pack([(_to_blocked(grads[n], _SHARD_AXIS[n]), 1) for n in _BIG], _BIG_ALIGN)
    dmod_send = _pack([(jnp.transpose(dmod.reshape(n_layers, N_DEV, ada_cols), (1, 0, 2)), 1)], _ROW_ALIGN)
    small_send = _pack([(grads[n].reshape(w[n].shape), 0) for n in _SMALL], _ROW_ALIGN)
    g_send = jnp.concatenate(
        [big_send, dmod_send, jnp.broadcast_to(small_send[None], (N_DEV,) + small_send.shape)], axis=1)
    g_recv = _exchange(g_send, gather=False, name="scatter_grads")
    big_rows, dmod_rows = big_send.shape[1], dmod_send.shape[1]

    out = {}
    packed = [_pack([(t[n], 0) for n in _BIG], _BIG_ALIGN) for t in (w, m, v)]
    res = _adamw(packed[0], g_recv, packed[1], packed[2], name="adamw_sharded")
    for kind, buf in zip(("grad", "delta", "new_m", "new_v"), res):
        for n, a in zip(_BIG, _unpack(buf, [w[n].shape for n in _BIG], 0)):
            out[kind, n] = a

    dmod_all = _unpack(lax.slice_in_dim(g_recv, big_rows, big_rows + dmod_rows, axis=1),
                       [(n_layers, ada_cols)], 1)[0]
    g_ada = jnp.stack([_mm(c_pad, _pad_rows(dmod_all[:, l], _PAD_BATCH), mode="tn", name=f"ada_dw_{l}",
                           tm=D, tn=ada_cols, tk=_PAD_BATCH, a_silu=True) for l in range(n_layers)])
    flat = lambda a: a.reshape(n_layers * D, ada_cols)
    res = _adamw(flat(w_ada), flat(g_ada)[None], flat(m_w_ada), flat(v_w_ada), name="adamw_ada")
    for kind, buf in zip(("grad", "delta", "new_m", "new_v"), res):
        out[kind, "w_ada"] = buf.reshape(w_ada.shape)

    small_parts = lax.slice_in_dim(g_recv, big_rows + dmod_rows, g_recv.shape[1], axis=1)
    packed = [_pack([(t[n], 0) for n in _SMALL], _ROW_ALIGN) for t in (w, m, v)]
    res = _adamw(packed[0], small_parts, packed[1], packed[2], name="adamw_replicated")
    for kind, buf in zip(("grad", "delta", "new_m", "new_v"), res):
        for n, a in zip(_SMALL, _unpack(buf, [w[n].shape for n in _SMALL], 0)):
            out[kind, n] = a

    return (loss, grad_x[None]) + tuple(out[kind, n] for kind in ("grad", "delta", "new_m", "new_v")
                                        for n in _W_NAMES)
```

```python
import functools
import math

import numpy as np
import jax
import jax.numpy as jnp
from jax import lax
from jax.experimental import pallas as pl
from jax.experimental.pallas import tpu as pltpu

F32 = jnp.float32
BF16 = jnp.bfloat16

N_DEV = 8
RMS_EPS = 1e-6
LANES = 128
V7X_VMEM_LIMIT = 48 * 1024 * 1024

GDN_HEADS = 8
GDN_DK = 128
GDN_CHUNK = 64
GDN_CONV = 4
DSW_GROUPS = ((128, 1), (512, 4), (2048, 16))
DSW_HEADS = 8
DSW_DH = 64
DSW_BLK = 128
REL_BUCKETS = 32
REL_MAX_DIST = 2048

ADAM_LR = 0.001
ADAM_B1 = 0.9
ADAM_B2 = 0.999
ADAM_EPS = 1e-08
ADAM_WD = 0.01
ADAM_STEP = 10

NEG_BIG = -1e30


def _params(*sem):
    return pltpu.CompilerParams(dimension_semantics=sem, vmem_limit_bytes=V7X_VMEM_LIMIT)


def _sigmoid(x):
    return 1.0 / (1.0 + jnp.exp(-x))


def _silu(x):
    return x * _sigmoid(x)


_DOT_DIMS = {
    "nn": (((1,), (0,)), ((), ())),
    "nt": (((1,), (1,)), ((), ())),
    "tn": (((0,), (0,)), ((), ())),
}


def _mm(a, b, *, mode, name, tm, tn, tk, out_dtype=F32, a_scale=None, out_scale=None, resid=None, a_silu=False):
    if mode == "nn":
        (M, K), N = a.shape, b.shape[1]
    elif mode == "nt":
        (M, K), N = a.shape, b.shape[0]
    else:
        (K, M), N = a.shape, b.shape[1]
    tm, tn, tk = min(tm, M), min(tn, N), min(tk, K)
    assert M % tm == 0 and N % tn == 0 and K % tk == 0, (name, M, N, K, tm, tn, tk)
    nk = K // tk

    def body(*refs):
        refs = list(refs)
        a_ref, b_ref = refs.pop(0), refs.pop(0)
        as_ref = refs.pop(0) if a_scale is not None else None
        os_ref = refs.pop(0) if out_scale is not None else None
        r_ref = refs.pop(0) if resid is not None else None
        o_ref = refs.pop(0)
        acc_ref = refs.pop(0) if nk > 1 else None

        av = a_ref[...]
        if a_silu:
            av = _silu(av.astype(F32))
        if as_ref is not None:
            av = av.astype(F32) * as_ref[...]
        part = lax.dot_general(av.astype(BF16), b_ref[...].astype(BF16), _DOT_DIMS[mode],
                               preferred_element_type=F32)

        def finish(r):
            if os_ref is not None:
                r = r * os_ref[...]
            if r_ref is not None:
                r = r + r_ref[...].astype(F32)
            o_ref[...] = r.astype(out_dtype)

        if nk == 1:
            finish(part)
        else:
            k = pl.program_id(2)

            @pl.when(k == 0)
            def _():
                acc_ref[...] = part

            @pl.when(k > 0)
            def _():
                acc_ref[...] += part

            @pl.when(k == nk - 1)
            def _():
                finish(acc_ref[...])

    if mode == "nn":
        a_spec = pl.BlockSpec((tm, tk), lambda i, j, k: (i, k))
        b_spec = pl.BlockSpec((tk, tn), lambda i, j, k: (k, j))
        as_spec = pl.BlockSpec((1, tk), lambda i, j, k: (0, k))
    elif mode == "nt":
        a_spec = pl.BlockSpec((tm, tk), lambda i, j, k: (i, k))
        b_spec = pl.BlockSpec((tn, tk), lambda i, j, k: (j, k))
        as_spec = pl.BlockSpec((1, tk), lambda i, j, k: (0, k))
    else:
        a_spec = pl.BlockSpec((tk, tm), lambda i, j, k: (k, i))
        b_spec = pl.BlockSpec((tk, tn), lambda i, j, k: (k, j))
        as_spec = None
    in_specs, args = [a_spec, b_spec], [a, b]
    if a_scale is not None:
        in_specs.append(as_spec)
        args.append(a_scale)
    if out_scale is not None:
        in_specs.append(pl.BlockSpec((1, tn), lambda i, j, k: (0, j)))
        args.append(out_scale)
    if resid is not None:
        in_specs.append(pl.BlockSpec((tm, tn), lambda i, j, k: (i, j)))
        args.append(resid)
    return pl.pallas_call(
        body, name=name, grid=(M // tm, N // tn, nk),
        in_specs=in_specs, out_specs=pl.BlockSpec((tm, tn), lambda i, j, k: (i, j)),
        out_shape=jax.ShapeDtypeStruct((M, N), out_dtype),
        scratch_shapes=[pltpu.VMEM((tm, tn), F32)] if nk > 1 else [],
        compiler_params=_params("parallel", "parallel", "arbitrary"),
    )(*args)


def _norm_mod_fwd(x, gain, sc, sh, *, name):
    S, D = x.shape
    tr = min(512, S)

    def body(x_ref, g_ref, sc_ref, sh_ref, h_ref):
        xv = x_ref[...]
        r = lax.rsqrt(jnp.mean(xv * xv, axis=-1, keepdims=True) + RMS_EPS)
        h_ref[...] = ((xv * r) * g_ref[...] * (1.0 + sc_ref[...]) + sh_ref[...]).astype(BF16)

    row = pl.BlockSpec((tr, D), lambda i: (i, 0))
    vec = pl.BlockSpec((1, D), lambda i: (0, 0))
    return pl.pallas_call(
        body, name=name, grid=(S // tr,), in_specs=[row, vec, vec, vec], out_specs=row,
        out_shape=jax.ShapeDtypeStruct((S, D), BF16), compiler_params=_params("parallel"),
    )(x, gain, sc, sh)


def _norm_mod_bwd(dh, x, dx_res, gain, sc, *, name):
    S, D = x.shape
    tr = min(256, S)
    n_steps = S // tr

    def body(dh_ref, x_ref, dxr_ref, g_ref, sc_ref, dx_ref, dsh_ref, dsc_ref, dgain_ref, acc_sh, acc_a):
        i = pl.program_id(0)
        xv = x_ref[...]
        r = lax.rsqrt(jnp.mean(xv * xv, axis=-1, keepdims=True) + RMS_EPS)
        n = xv * r
        dhv = dh_ref[...].astype(F32)
        dn = dhv * (g_ref[...] * (1.0 + sc_ref[...]))
        dx_ref[...] = dxr_ref[...] + r * (dn - n * jnp.mean(dn * n, axis=-1, keepdims=True))
        p_sh = jnp.sum(dhv, axis=0, keepdims=True)
        p_a = jnp.sum(dhv * n, axis=0, keepdims=True)

        @pl.when(i == 0)
        def _():
            acc_sh[...] = p_sh
            acc_a[...] = p_a

        @pl.when(i > 0)
        def _():
            acc_sh[...] += p_sh
            acc_a[...] += p_a

        @pl.when(i == n_steps - 1)
        def _():
            dsh_ref[...] = acc_sh[...]
            dsc_ref[...] = acc_a[...] * g_ref[...]
            dgain_ref[...] = acc_a[...] * (1.0 + sc_ref[...])

    row = pl.BlockSpec((tr, D), lambda i: (i, 0))
    vec = pl.BlockSpec((1, D), lambda i: (0, 0))
    vshape = jax.ShapeDtypeStruct((1, D), F32)
    return pl.pallas_call(
        body, name=name, grid=(n_steps,), in_specs=[row, row, row, vec, vec],
        out_specs=[row, vec, vec, vec],
        out_shape=[jax.ShapeDtypeStruct((S, D), F32), vshape, vshape, vshape],
        scratch_shapes=[pltpu.VMEM((1, D), F32), pltpu.VMEM((1, D), F32)],
        compiler_params=_params("arbitrary"),
    )(dh, x, dx_res, gain, sc)


def _wout_grad(gmat, w, gate, *, name):
    K, D = w.shape
    tr = min(256, K)
    n_steps = K // tr

    def body(g_ref, w_ref, gate_ref, dw_ref, dgate_ref, acc):
        i = pl.program_id(0)
        gv = g_ref[...]
        dw_ref[...] = gv * gate_ref[...]
        part = jnp.sum(gv * w_ref[...], axis=0, keepdims=True)

        @pl.when(i == 0)
        def _():
            acc[...] = part

        @pl.when(i > 0)
        def _():
            acc[...] += part

        @pl.when(i == n_steps - 1)
        def _():
            dgate_ref[...] = acc[...]

    row = pl.BlockSpec((tr, D), lambda i: (i, 0))
    vec = pl.BlockSpec((1, D), lambda i: (0, 0))
    return pl.pallas_call(
        body, name=name, grid=(n_steps,), in_specs=[row, row, vec], out_specs=[row, vec],
        out_shape=[jax.ShapeDtypeStruct((K, D), F32), jax.ShapeDtypeStruct((1, D), F32)],
        scratch_shapes=[pltpu.VMEM((1, D), F32)], compiler_params=_params("arbitrary"),
    )(gmat, w, gate)


def _swiglu_fwd(p, *, name):
    S, F2 = p.shape
    F = F2 // 2
    tr = min(256, S)

    def body(p_ref, a_ref):
        gate = p_ref[:, :F].astype(F32)
        up = p_ref[:, F:].astype(F32)
        a_ref[...] = (_silu(gate) * up).astype(BF16)

    return pl.pallas_call(
        body, name=name, grid=(S // tr,), in_specs=[pl.BlockSpec((tr, F2), lambda i: (i, 0))],
        out_specs=pl.BlockSpec((tr, F), lambda i: (i, 0)),
        out_shape=jax.ShapeDtypeStruct((S, F), BF16), compiler_params=_params("parallel"),
    )(p)


def _swiglu_bwd(da, p, *, name):
    S, F2 = p.shape
    F = F2 // 2
    tr = min(256, S)

    def body(da_ref, p_ref, dp_ref):
        gate = p_ref[:, :F].astype(F32)
        up = p_ref[:, F:].astype(F32)
        dav = da_ref[...].astype(F32)
        sg = _sigmoid(gate)
        dp_ref[:, :F] = (dav * up * (sg * (1.0 + gate * (1.0 - sg)))).astype(BF16)
        dp_ref[:, F:] = (dav * (gate * sg)).astype(BF16)

    return pl.pallas_call(
        body, name=name, grid=(S // tr,),
        in_specs=[pl.BlockSpec((tr, F), lambda i: (i, 0)), pl.BlockSpec((tr, F2), lambda i: (i, 0))],
        out_specs=pl.BlockSpec((tr, F2), lambda i: (i, 0)),
        out_shape=jax.ShapeDtypeStruct((S, F2), BF16), compiler_params=_params("parallel"),
    )(da, p)


def _loss_head(y, target, *, name):
    S, D = y.shape
    tr = min(512, S)
    n_steps = S // tr

    def body(y_ref, t_ref, dy_ref, sse_ref, acc):
        i = pl.program_id(0)
        e = y_ref[...] - t_ref[...]
        dy_ref[...] = e * (1.0 / D)
        part = jnp.sum(e * e, axis=0, keepdims=True)

        @pl.when(i == 0)
        def _():
            acc[...] = part

        @pl.when(i > 0)
        def _():
            acc[...] += part

        @pl.when(i == n_steps - 1)
        def _():
            sse_ref[...] = jnp.sum(acc[...], axis=1, keepdims=True)

    row = pl.BlockSpec((tr, D), lambda i: (i, 0))
    return pl.pallas_call(
        body, name=name, grid=(n_steps,), in_specs=[row, row],
        out_specs=[row, pl.BlockSpec((1, 1), lambda i: (0, 0))],
        out_shape=[jax.ShapeDtypeStruct((S, D), F32), jax.ShapeDtypeStruct((1, 1), F32)],
        scratch_shapes=[pltpu.VMEM((1, D), F32)], compiler_params=_params("arbitrary"),
    )(y, target)


def _adamw(w, g_parts, m, v, *, name):
    R, C = w.shape
    P = g_parts.shape[0]
    tr = _tile(R, max(8, 1024 * LANES // C))
    c1 = 1.0 / (1.0 - ADAM_B1 ** ADAM_STEP)
    c2 = 1.0 / (1.0 - ADAM_B2 ** ADAM_STEP)

    def body(w_ref, g_ref, m_ref, v_ref, go_ref, d_ref, mo_ref, vo_ref):
        g = g_ref[0].astype(F32)
        for q in range(1, P):
            g = g + g_ref[q].astype(F32)
        mn = ADAM_B1 * m_ref[...] + (1.0 - ADAM_B1) * g
        vn = ADAM_B2 * v_ref[...] + (1.0 - ADAM_B2) * (g * g)
        go_ref[...] = g
        mo_ref[...] = mn
        vo_ref[...] = vn
        d_ref[...] = -ADAM_LR * ((mn * c1) / (jnp.sqrt(vn * c2) + ADAM_EPS) + ADAM_WD * w_ref[...])

    row = pl.BlockSpec((tr, C), lambda i: (i, 0))
    shp = jax.ShapeDtypeStruct((R, C), F32)
    return pl.pallas_call(
        body, name=name, grid=(R // tr,),
        in_specs=[row, pl.BlockSpec((P, tr, C), lambda i: (0, i, 0)), row, row],
        out_specs=[row, row, row, row], out_shape=[shp, shp, shp, shp],
        compiler_params=_params("parallel"),
    )(w, g_parts, m, v)


_HALO = 8


def _conv_taps(buf, w_ref, rows):
    acc = None
    for j in range(GDN_CONV):
        term = buf[pl.ds(_HALO - (GDN_CONV - 1) + j, rows), :] * w_ref[j:j + 1, :]
        acc = term if acc is None else acc + term
    return acc


def _fill_conv_buf(buf, halo_ref, x_ref, rows, first):
    buf[0:_HALO, :] = jnp.where(first, 0.0, halo_ref[...])
    buf[_HALO:_HALO + rows, :] = x_ref[...]


_HM = 3 * GDN_DK


def _l2n(seg):
    return lax.rsqrt(jnp.sum(seg * seg, axis=-1, keepdims=True) + RMS_EPS)


def _gdn_prep_fwd(x, conv_w, *, name):
    S, C3 = x.shape
    CB = _HM
    RB = min(256, S)

    def body(x_ref, halo_ref, w_ref, o_ref, buf):
        i = pl.program_id(0)
        _fill_conv_buf(buf, halo_ref, x_ref, RB, i == 0)
        y = _silu(_conv_taps(buf, w_ref, RB))
        q, k = y[:, :GDN_DK], y[:, GDN_DK:2 * GDN_DK]
        o_ref[:, :GDN_DK] = q * (_l2n(q) * GDN_DK ** -0.5)
        o_ref[:, GDN_DK:2 * GDN_DK] = k * _l2n(k)
        o_ref[:, 2 * GDN_DK:] = y[:, 2 * GDN_DK:]

    hb = RB // _HALO
    return pl.pallas_call(
        body, name=name, grid=(S // RB, C3 // CB),
        in_specs=[pl.BlockSpec((RB, CB), lambda i, j: (i, j)),
                  pl.BlockSpec((_HALO, CB), lambda i, j: (jnp.maximum(i * hb - 1, 0), j)),
                  pl.BlockSpec((GDN_CONV, CB), lambda i, j: (0, j))],
        out_specs=pl.BlockSpec((RB, CB), lambda i, j: (i, j)),
        out_shape=jax.ShapeDtypeStruct((S, C3), F32),
        scratch_shapes=[pltpu.VMEM((RB + _HALO, CB), F32)],
        compiler_params=_params("parallel", "parallel"),
    )(x, x, conv_w)


def _gdn_prep_bwd_pre(dn, x, conv_w, *, name):
    S, C3 = x.shape
    CB = _HM
    RB = min(256, S)
    n_steps = S // RB

    def body(dn_ref, x_ref, halo_ref, w_ref, dc_ref, dw_ref, buf):
        i = pl.program_id(1)
        _fill_conv_buf(buf, halo_ref, x_ref, RB, i == 0)
        acc = _conv_taps(buf, w_ref, RB)
        sg = _sigmoid(acc)
        y = acc * sg
        dsilu = sg * (1.0 + acc * (1.0 - sg))
        for part, scale in ((0, GDN_DK ** -0.5), (1, 1.0)):
            sl = slice(part * GDN_DK, (part + 1) * GDN_DK)
            seg = y[:, sl]
            r = _l2n(seg)
            n = seg * r
            d = dn_ref[:, sl] * scale
            dc_ref[:, sl] = r * (d - n * jnp.sum(d * n, axis=-1, keepdims=True)) * dsilu[:, sl]
        dc_ref[:, 2 * GDN_DK:] = dn_ref[:, 2 * GDN_DK:] * dsilu[:, 2 * GDN_DK:]
        dc = dc_ref[...]
        parts = [jnp.sum(dc * buf[pl.ds(_HALO - (GDN_CONV - 1) + t, RB), :], axis=0, keepdims=True)
                 for t in range(GDN_CONV)]
        part = jnp.concatenate(parts + [jnp.zeros((8 - GDN_CONV, CB), F32)], axis=0)

        @pl.when(i == 0)
        def _():
            dw_ref[...] = part

        @pl.when(i > 0)
        def _():
            dw_ref[...] += part

    hb = RB // _HALO
    return pl.pallas_call(
        body, name=name, grid=(C3 // CB, n_steps),
        in_specs=[pl.BlockSpec((RB, CB), lambda j, i: (i, j)),
                  pl.BlockSpec((RB, CB), lambda j, i: (i, j)),
                  pl.BlockSpec((_HALO, CB), lambda j, i: (jnp.maximum(i * hb - 1, 0), j)),
                  pl.BlockSpec((GDN_CONV, CB), lambda j, i: (0, j))],
        out_specs=[pl.BlockSpec((RB, CB), lambda j, i: (i, j)),
                   pl.BlockSpec((8, CB), lambda j, i: (0, j))],
        out_shape=[jax.ShapeDtypeStruct((S, C3), F32), jax.ShapeDtypeStruct((8, C3), F32)],
        scratch_shapes=[pltpu.VMEM((RB + _HALO, CB), F32)],
        compiler_params=_params("parallel", "arbitrary"),
    )(dn, x, x, conv_w)


def _gdn_conv_bwd_x(dc, conv_w, *, name):
    S, C3 = dc.shape
    CB = _HM
    RB = min(256, S)
    n_steps = S // RB

    def body(dc_ref, halo_ref, w_ref, dx_ref, buf):
        i = pl.program_id(0)
        buf[0:RB, :] = dc_ref[...]
        buf[RB:RB + _HALO, :] = jnp.where(i == n_steps - 1, 0.0, halo_ref[...])
        acc = None
        for j in range(GDN_CONV):
            term = buf[pl.ds(GDN_CONV - 1 - j, RB), :] * w_ref[j:j + 1, :]
            acc = term if acc is None else acc + term
        dx_ref[...] = acc.astype(BF16)

    hb = RB // _HALO
    last = S // _HALO - 1
    return pl.pallas_call(
        body, name=name, grid=(n_steps, C3 // CB),
        in_specs=[pl.BlockSpec((RB, CB), lambda i, j: (i, j)),
                  pl.BlockSpec((_HALO, CB), lambda i, j: (jnp.minimum((i + 1) * hb, last), j)),
                  pl.BlockSpec((GDN_CONV, CB), lambda i, j: (0, j))],
        out_specs=pl.BlockSpec((RB, CB), lambda i, j: (i, j)),
        out_shape=jax.ShapeDtypeStruct((S, C3), BF16),
        scratch_shapes=[pltpu.VMEM((RB + _HALO, CB), F32)],
        compiler_params=_params("parallel", "parallel"),
    )(dc, dc, conv_w)


def _split_bf16(a):
    hi = a.astype(BF16)
    return hi, (a - hi.astype(F32)).astype(BF16)


def _dot(a, b, dims="nn", exact=False):
    def dot(p, q):
        return lax.dot_general(p, q, _DOT_DIMS[dims], preferred_element_type=F32)

    if exact:
        (ah, al), (bh, bl) = _split_bf16(a), _split_bf16(b)
        return dot(ah, bh) + (dot(ah, bl) + dot(al, bh))
    return dot(a.astype(BF16), b.astype(BF16))


def _softplus(x):
    return jnp.maximum(x, 0.0) + jnp.log(1.0 + jnp.exp(-jnp.abs(x)))


def _to_col(row, eye):
    return jnp.sum(jnp.where(eye, row, 0.0), axis=1, keepdims=True)


def _to_row(col, eye):
    return jnp.sum(jnp.where(eye, col, 0.0), axis=0, keepdims=True)


def _unit_lower_inverse(low, ri, ci):
    C = low.shape[0]
    eye = jnp.where(ri == ci, 1.0, 0.0)
    x = eye - jnp.where((ri >> 1) == (ci >> 1), low, 0.0)
    m, sh = 2, 1
    while m < C:
        join = ((ri >> (sh + 1)) == (ci >> (sh + 1))) & (((ri >> sh) & 1) == 1) & (((ci >> sh) & 1) == 0)
        x = x - _dot(_dot(x, jnp.where(join, low, 0.0)), x)
        m, sh = 2 * m, sh + 1
    resid = eye - x - _dot(low, x, exact=True)
    return x + _dot(x, resid)


def _gdn_chunk_local(qkv, g_row, beta_row, ri, ci):
    eye, tril, strict = ri == ci, ri >= ci, ri > ci
    q, k, v = qkv[:, :GDN_DK], qkv[:, GDN_DK:2 * GDN_DK], qkv[:, 2 * GDN_DK:]
    g_col = _to_col(g_row, eye)
    gc_col = jnp.sum(jnp.where(tril, g_row, 0.0), axis=1, keepdims=True)
    gc_row = jnp.sum(jnp.where(ri <= ci, g_col, 0.0), axis=0, keepdims=True)
    g_last = jnp.sum(g_row, axis=1, keepdims=True)
    beta_col = _to_col(beta_row, eye)
    decay = jnp.where(tril, jnp.exp(jnp.minimum(gc_col - gc_row, 0.0)), 0.0)
    e_col = jnp.exp(gc_col)
    f_col = jnp.exp(g_last - gc_col)
    e_last = jnp.exp(g_last)
    kb = k * beta_col
    vb = v * beta_col
    low = jnp.where(strict, _dot(kb, k, "nt") * decay, 0.0)
    att = _dot(q, k, "nt") * decay
    return dict(q=q, k=k, v=v, beta_col=beta_col, decay=decay, e_col=e_col, f_col=f_col, e_last=e_last,
                kb=kb, vb=vb, low=low, att=att, eye=eye, strict=strict, tril=tril)


def _chunk_iotas():
    C = GDN_CHUNK
    return lax.broadcasted_iota(jnp.int32, (C, C), 0), lax.broadcasted_iota(jnp.int32, (C, C), 1)


def _gdn_chunk_fwd(qkv, ab, a_log, dt_bias, *, name):
    S = qkv.shape[0]
    H, C, DK = GDN_HEADS, GDN_CHUNK, GDN_DK
    RB = min(256, S)
    NCB, NB, NC = RB // C, S // RB, S // C

    def body(qkv_ref, a_ref, b_ref, alog_ref, dtb_ref, o_ref, st_ref, t_ref, state):
        nb = pl.program_id(1)

        @pl.when(nb == 0)
        def _():
            state[...] = jnp.zeros_like(state)

        ri, ci = _chunk_iotas()
        neg_a = -jnp.exp(alog_ref[...])
        for c in range(NCB):
            rows = pl.ds(c * C, C)
            g_row = neg_a * _softplus(a_ref[c] + dtb_ref[...])
            beta_row = _sigmoid(b_ref[c])
            L = _gdn_chunk_local(qkv_ref[rows, :], g_row, beta_row, ri, ci)
            tinv = _unit_lower_inverse(L["low"], ri, ci)
            u = _dot(tinv, L["vb"], exact=True)
            w = _dot(tinv, L["kb"] * L["e_col"], exact=True)
            st = state[...]
            vn = u - _dot(w, st)
            o_ref[rows, :] = _dot(L["q"] * L["e_col"], st) + _dot(L["att"], vn)
            st_ref[c] = st
            t_ref[c] = tinv
            state[...] = st * L["e_last"] + _dot(L["k"] * L["f_col"], vn, "tn")

    return pl.pallas_call(
        body, name=name, grid=(H, NB),
        in_specs=[pl.BlockSpec((RB, _HM), lambda h, n: (n, h)),
                  pl.BlockSpec((None, NCB, 1, C), lambda h, n: (h, n, 0, 0)),
                  pl.BlockSpec((None, NCB, 1, C), lambda h, n: (H + h, n, 0, 0)),
                  pl.BlockSpec((None, 1, 1), lambda h, n: (h, 0, 0)),
                  pl.BlockSpec((None, 1, 1), lambda h, n: (h, 0, 0))],
        out_specs=[pl.BlockSpec((RB, DK), lambda h, n: (n, h)),
                   pl.BlockSpec((None, NCB, DK, DK), lambda h, n: (h, n, 0, 0)),
                   pl.BlockSpec((None, NCB, C, C), lambda h, n: (h, n, 0, 0))],
        out_shape=[jax.ShapeDtypeStruct((S, H * DK), F32),
                   jax.ShapeDtypeStruct((H, NC, DK, DK), F32),
                   jax.ShapeDtypeStruct((H, NC, C, C), F32)],
        scratch_shapes=[pltpu.VMEM((DK, DK), F32)],
        compiler_params=_params("parallel", "arbitrary"),
    )(qkv, ab, ab, a_log, dt_bias)


def _gdn_chunk_bwd(qkv, ab, a_log, dt_bias, states, tinvs, do, *, name):
    S = qkv.shape[0]
    H, C, DK = GDN_HEADS, GDN_CHUNK, GDN_DK
    RB = min(256, S)
    NCB, NB, NC = RB // C, S // RB, S // C

    def body(qkv_ref, a_ref, b_ref, alog_ref, dtb_ref, st_ref, t_ref, do_ref,
             dqkv_ref, da_ref, db_ref, dalog_ref, ddtb_ref, dstate):
        nb = pl.program_id(1)

        @pl.when(nb == 0)
        def _():
            dstate[...] = jnp.zeros_like(dstate)
            dalog_ref[...] = jnp.zeros_like(dalog_ref)
            ddtb_ref[...] = jnp.zeros_like(ddtb_ref)

        ri, ci = _chunk_iotas()
        neg_a = -jnp.exp(alog_ref[...])
        for c in reversed(range(NCB)):
            rows = pl.ds(c * C, C)
            a_pre = a_ref[c] + dtb_ref[...]
            g_row = neg_a * _softplus(a_pre)
            beta_row = _sigmoid(b_ref[c])
            L = _gdn_chunk_local(qkv_ref[rows, :], g_row, beta_row, ri, ci)
            q, k, v, kb, vb = L["q"], L["k"], L["v"], L["kb"], L["vb"]
            e_col, f_col, e_last, decay = L["e_col"], L["f_col"], L["e_last"], L["decay"]
            eye, strict, tril = L["eye"], L["strict"], L["tril"]
            tinv = t_ref[c]
            st = st_ref[c]
            dst = dstate[...]
            dov = do_ref[rows, :]
            kbe = kb * e_col
            u = _dot(tinv, vb, exact=True)
            w = _dot(tinv, kbe, exact=True)
            vn = u - _dot(w, st)
            kf = k * f_col
            qe = q * e_col

            dvn = _dot(L["att"], dov, "tn") + _dot(kf, dst)
            datt = jnp.where(tril, _dot(dov, vn, "nt"), 0.0)
            dqe = _dot(dov, st, "nt")
            dstate[...] = dst * e_last + _dot(qe, dov, "tn") - _dot(w, dvn, "tn")
            de_last = jnp.sum(jnp.sum(dst * st, axis=1, keepdims=True), axis=0, keepdims=True)
            dkf = _dot(vn, dst, "nt")
            dw = -_dot(dvn, st, "nt")
            dt = _dot(dvn, vb, "nt") + _dot(dw, kbe, "nt")
            dvb = _dot(tinv, dvn, "tn", exact=True)
            dkbe = _dot(tinv, dw, "tn", exact=True)
            dlow = -jnp.where(strict, _dot(_dot(tinv, dt, "tn", exact=True), tinv, "nt", exact=True), 0.0)
            dkk = dlow * decay
            dqk = datt * decay
            dkb = _dot(dkk, k) + dkbe * e_col
            dk = _dot(dkk, kb, "tn") + _dot(dqk, q, "tn") + dkf * f_col + dkb * L["beta_col"]
            dq = _dot(dqk, k) + dqe * e_col
            dv = dvb * L["beta_col"]
            dqkv_ref[rows, :GDN_DK] = dq
            dqkv_ref[rows, GDN_DK:2 * GDN_DK] = dk
            dqkv_ref[rows, 2 * GDN_DK:] = dv

            dbeta_col = jnp.sum(k * dkb + v * dvb, axis=1, keepdims=True)
            pmat = dlow * L["low"] + datt * L["att"]
            df_col = jnp.sum(k * dkf, axis=1, keepdims=True) * f_col
            dgc_col = (jnp.sum(pmat, axis=1, keepdims=True)
                       + jnp.sum(q * dqe + kb * dkbe, axis=1, keepdims=True) * e_col - df_col)
            dgc_row = _to_row(dgc_col, eye) - jnp.sum(pmat, axis=0, keepdims=True)
            dg_last = jnp.sum(df_col, axis=0, keepdims=True) + de_last * e_last
            dgc_c = _to_col(dgc_row, eye)
            dg_row = jnp.sum(jnp.where(ri >= ci, dgc_c, 0.0), axis=0, keepdims=True) + dg_last
            dbeta_row = _to_row(dbeta_col, eye)

            da_row = dg_row * neg_a * _sigmoid(a_pre)
            da_ref[c] = da_row
            db_ref[c] = dbeta_row * beta_row * (1.0 - beta_row)
            dalog_ref[...] += jnp.sum(dg_row * g_row, axis=1, keepdims=True)
            ddtb_ref[...] += jnp.sum(da_row, axis=1, keepdims=True)

    rev = lambda n: NB - 1 - n
    return pl.pallas_call(
        body, name=name, grid=(H, NB),
        in_specs=[pl.BlockSpec((RB, _HM), lambda h, n: (rev(n), h)),
                  pl.BlockSpec((None, NCB, 1, C), lambda h, n: (h, rev(n), 0, 0)),
                  pl.BlockSpec((None, NCB, 1, C), lambda h, n: (H + h, rev(n), 0, 0)),
                  pl.BlockSpec((None, 1, 1), lambda h, n: (h, 0, 0)),
                  pl.BlockSpec((None, 1, 1), lambda h, n: (h, 0, 0)),
                  pl.BlockSpec((None, NCB, DK, DK), lambda h, n: (h, rev(n), 0, 0)),
                  pl.BlockSpec((None, NCB, C, C), lambda h, n: (h, rev(n), 0, 0)),
                  pl.BlockSpec((RB, DK), lambda h, n: (rev(n), h))],
        out_specs=[pl.BlockSpec((RB, _HM), lambda h, n: (rev(n), h)),
                   pl.BlockSpec((None, NCB, 1, C), lambda h, n: (h, rev(n), 0, 0)),
                   pl.BlockSpec((None, NCB, 1, C), lambda h, n: (h, rev(n), 0, 0)),
                   pl.BlockSpec((None, 1, 1), lambda h, n: (h, 0, 0)),
                   pl.BlockSpec((None, 1, 1), lambda h, n: (h, 0, 0))],
        out_shape=[jax.ShapeDtypeStruct((S, H * _HM), F32),
                   jax.ShapeDtypeStruct((H, NC, 1, C), F32),
                   jax.ShapeDtypeStruct((H, NC, 1, C), F32),
                   jax.ShapeDtypeStruct((H, 1, 1), F32),
                   jax.ShapeDtypeStruct((H, 1, 1), F32)],
        scratch_shapes=[pltpu.VMEM((DK, DK), F32)],
        compiler_params=_params("parallel", "arbitrary"),
    )(qkv, ab, ab, a_log, dt_bias, states, tinvs, do)


def _gdn_outnorm_fwd(o, z, gain, *, name):
    S, HV = o.shape
    RB = min(512, S)

    def body(o_ref, z_ref, g_ref, y_ref):
        ov = o_ref[...]
        r = lax.rsqrt(jnp.mean(ov * ov, axis=-1, keepdims=True) + RMS_EPS)
        y_ref[...] = (ov * r * g_ref[...] * _silu(z_ref[...].astype(F32))).astype(BF16)

    blk = pl.BlockSpec((RB, GDN_DK), lambda i, h: (i, h))
    return pl.pallas_call(
        body, name=name, grid=(S // RB, HV // GDN_DK),
        in_specs=[blk, blk, pl.BlockSpec((1, GDN_DK), lambda i, h: (0, 0))], out_specs=blk,
        out_shape=jax.ShapeDtypeStruct((S, HV), BF16), compiler_params=_params("parallel", "parallel"),
    )(o, z, gain)


def _gdn_outnorm_bwd(dy, o, z, gain, *, name):
    S, HV = o.shape
    RB = min(512, S)

    def body(dy_ref, o_ref, z_ref, g_ref, do_ref, dz_ref, dg_ref):
        first = (pl.program_id(0) == 0) & (pl.program_id(1) == 0)
        ov = o_ref[...]
        zv = z_ref[...].astype(F32)
        dyv = dy_ref[...].astype(F32)
        r = lax.rsqrt(jnp.mean(ov * ov, axis=-1, keepdims=True) + RMS_EPS)
        n = ov * r
        sg = _sigmoid(zv)
        dng = dyv * (zv * sg)
        dn = dng * g_ref[...]
        do_ref[...] = r * (dn - n * jnp.mean(dn * n, axis=-1, keepdims=True))
        dz_ref[...] = (dyv * (n * g_ref[...]) * (sg * (1.0 + zv * (1.0 - sg)))).astype(BF16)
        part = jnp.sum(dng * n, axis=0, keepdims=True)

        @pl.when(first)
        def _():
            dg_ref[...] = part

        @pl.when(jnp.logical_not(first))
        def _():
            dg_ref[...] += part

    blk = pl.BlockSpec((RB, GDN_DK), lambda i, h: (i, h))
    vec = pl.BlockSpec((1, GDN_DK), lambda i, h: (0, 0))
    return pl.pallas_call(
        body, name=name, grid=(S // RB, HV // GDN_DK),
        in_specs=[blk, blk, blk, vec], out_specs=[blk, blk, vec],
        out_shape=[jax.ShapeDtypeStruct((S, HV), F32), jax.ShapeDtypeStruct((S, HV), BF16),
                   jax.ShapeDtypeStruct((1, GDN_DK), F32)],
        compiler_params=_params("arbitrary", "arbitrary"),
    )(dy, o, z, gain)


def _rms64(x, gain):
    r = lax.rsqrt(jnp.mean(x * x, axis=-1, keepdims=True) + RMS_EPS)
    xh = x * r
    return xh, r, xh * gain


def _rms64_bwd(dy, xh, r, gain):
    dxh = dy * gain
    return r * (dxh - xh * jnp.mean(dxh * xh, axis=-1, keepdims=True))


_HP = LANES // DSW_DH
_DSW_W = DSW_HEADS * DSW_DH
_DSW_ROWS = 1024
_DSW_BATCH = 8


def _dsw_geometry(S, g):
    d = DSW_GROUPS[g][1]
    slab = DSW_BLK * d
    tb = max(1, min(_DSW_ROWS, S) // slab)
    return d, slab, tb, S // (tb * slab)


def _block_rows(t, r, slab, d):
    return pl.ds(t * slab + r, DSW_BLK) if d == 1 else pl.ds(t * slab + r, DSW_BLK, stride=d)


def _head(x, h):
    return x[:, h * DSW_DH:(h + 1) * DSW_DH]


def _dsw_attn_fwd(q, k, v, bias, q_gain, k_gain, prev_out, *, g, name):
    S, WT = q.shape
    B = DSW_BLK
    d, slab, tb, n_tiles = _dsw_geometry(S, g)
    rt = tb * slab
    cb = g * (_DSW_W // LANES)
    batch_res = max(1, _DSW_BATCH // tb)

    def body(q_ref, kp_ref, kc_ref, vp_ref, vc_ref, bias_ref, qg_ref, kg_ref, *rest):
        o_ref, lse_ref = rest[-2:]
        i = pl.program_id(1)
        qg, kg = qg_ref[...] * DSW_DH ** -0.5, kg_ref[...]
        col = lax.broadcasted_iota(jnp.int32, (B, 2 * B), 1)
        for r0 in range(0, d, batch_res):
            res = range(r0, min(d, r0 + batch_res))
            heads = range(_HP)
            k_raw = {(r, -1): kp_ref[_block_rows(0, r, slab, d), :] for r in res}
            v_raw = {(r, -1): vp_ref[_block_rows(0, r, slab, d), :] for r in res}
            q_raw = {}
            for r in res:
                for t in range(tb):
                    rows = _block_rows(t, r, slab, d)
                    q_raw[r, t], k_raw[r, t], v_raw[r, t] = q_ref[rows, :], kc_ref[rows, :], vc_ref[rows, :]
            kn = {key: [_rms64(_head(x, h), kg)[2].astype(BF16) for h in heads] for key, x in k_raw.items()}
            vb = {key: [_head(x, h).astype(BF16) for h in heads] for key, x in v_raw.items()}
            qn = {key: [_rms64(_head(x, h), qg)[2] for h in heads] for key, x in q_raw.items()}
            items = [(r, t, h) for r in res for t in range(tb) for h in heads]
            s = {}
            for r, t, h in items:
                sv = _dot(qn[r, t][h], jnp.concatenate([kn[r, t - 1][h], kn[r, t][h]], axis=0), "nt") + bias_ref[h]
                s[r, t, h] = jnp.where((i == 0) & (col < B), NEG_BIG, sv) if t == 0 else sv
            m = {it: jnp.max(s[it], axis=-1, keepdims=True) for it in items}
            p = {it: jnp.exp(s[it] - m[it]) for it in items}
            l = {it: jnp.sum(p[it], axis=-1, keepdims=True) for it in items}
            o = {(r, t, h): _dot(p[r, t, h], jnp.concatenate([vb[r, t - 1][h], vb[r, t][h]], axis=0))
                 for r, t, h in items}
            for r in res:
                for t in range(tb):
                    rows = _block_rows(t, r, slab, d)
                    o_ref[rows, :] = jnp.concatenate([o[r, t, h] / l[r, t, h] for h in heads], axis=1)
                    lse_ref[rows, :] = jnp.concatenate(
                        [jnp.broadcast_to(m[r, t, h] + jnp.log(l[r, t, h]), (B, DSW_DH)) for h in heads], axis=1)

    cur = pl.BlockSpec((rt, LANES), lambda hp, i: (i, cb + hp))
    prev = pl.BlockSpec((slab, LANES), lambda hp, i: (jnp.maximum(i * tb - 1, 0), cb + hp))
    vec = pl.BlockSpec((1, DSW_DH), lambda hp, i: (0, 0))
    shp = jax.ShapeDtypeStruct((S, WT), F32)
    carried = [] if prev_out is None else list(prev_out)
    n_in = 8
    return pl.pallas_call(
        body, name=name, grid=(_DSW_W // LANES, n_tiles),
        in_specs=[cur, prev, cur, prev, cur, pl.BlockSpec((_HP, B, 2 * B), lambda hp, i: (hp, 0, 0)), vec, vec]
                 + [pl.BlockSpec(memory_space=pl.ANY)] * len(carried),
        out_specs=[cur, cur], out_shape=[shp, shp],
        input_output_aliases={n_in + j: j for j in range(len(carried))},
        compiler_params=_params("parallel", "parallel"),
    )(q, k, k, v, v, bias, q_gain, k_gain, *carried)


def _dsw_merge(o_g, lse_g, *, name):
    S = o_g.shape[0]
    W, G = _DSW_W, len(DSW_GROUPS)
    tr = min(512, S)

    def body(o_ref, l_ref, out_ref, lse_ref):
        ls = [l_ref[:, g * W:(g + 1) * W] for g in range(G)]
        m = ls[0]
        for g in range(1, G):
            m = jnp.maximum(m, ls[g])
        den = jnp.zeros_like(m)
        acc = jnp.zeros_like(m)
        for g in range(G):
            wg = jnp.exp(ls[g] - m)
            den = den + wg
            acc = acc + wg * o_ref[:, g * W:(g + 1) * W]
        out_ref[...] = acc / den
        lse_ref[...] = m + jnp.log(den)

    wide = pl.BlockSpec((tr, G * W), lambda i: (i, 0))
    blk = pl.BlockSpec((tr, W), lambda i: (i, 0))
    shp = jax.ShapeDtypeStruct((S, W), F32)
    return pl.pallas_call(
        body, name=name, grid=(S // tr,), in_specs=[wide, wide], out_specs=[blk, blk],
        out_shape=[shp, shp], compiler_params=_params("parallel"),
    )(o_g, lse_g)


def _dsw_attn_bwd(q, k, v, o, lse, do, bias, q_gain, k_gain, prev_out, *, g, name):
    S, WT = q.shape
    B = DSW_BLK
    d, slab, tb, n_tiles = _dsw_geometry(S, g)
    rt = tb * slab
    cb = g * (_DSW_W // LANES)
    n_slabs = S // slab
    scale = DSW_DH ** -0.5
    batch_res = max(1, _DSW_BATCH // tb)

    def body(q_ref, qx_ref, kp_ref, kc_ref, vp_ref, vc_ref, o_ref, ox_ref, l_ref, lx_ref, do_ref, dox_ref,
             bias_ref, qg_ref, kg_ref, *rest):
        dq_ref, dk_ref, dv_ref, db_ref, dqg_ref, dkg_ref = rest[-6:]
        hp, i = pl.program_id(0), pl.program_id(1)
        qg, kg = qg_ref[...] * scale, kg_ref[...]
        col = lax.broadcasted_iota(jnp.int32, (B, 2 * B), 1)
        has_next = i < n_tiles - 1

        @pl.when(i == 0)
        def _():
            db_ref[...] = jnp.zeros_like(db_ref)

        dqg_acc = jnp.zeros((1, DSW_DH), F32)
        dkg_acc = jnp.zeros((1, DSW_DH), F32)
        heads = range(_HP)
        for r0 in range(0, d, batch_res):
            res = range(r0, min(d, r0 + batch_res))
            q_raw, k_raw, v_raw, o_raw, l_raw, do_raw = {}, {}, {}, {}, {}, {}
            for r in res:
                first_rows = _block_rows(0, r, slab, d)
                k_raw[r, -1], v_raw[r, -1] = kp_ref[first_rows, :], vp_ref[first_rows, :]
                for t in range(tb):
                    rows = _block_rows(t, r, slab, d)
                    q_raw[r, t], o_raw[r, t], l_raw[r, t], do_raw[r, t] = (
                        q_ref[rows, :], o_ref[rows, :], l_ref[rows, :], do_ref[rows, :])
                    k_raw[r, t], v_raw[r, t] = kc_ref[rows, :], vc_ref[rows, :]
                q_raw[r, tb], o_raw[r, tb], l_raw[r, tb], do_raw[r, tb] = (
                    qx_ref[first_rows, :], ox_ref[first_rows, :], lx_ref[first_rows, :], dox_ref[first_rows, :])
            kk = {key: [_rms64(_head(x, h), kg) for h in heads] for key, x in k_raw.items()}
            qq = {key: [_rms64(_head(x, h), qg) for h in heads] for key, x in q_raw.items()}
            knb = {key: [kk[key][h][2].astype(BF16) for h in heads] for key in kk}
            qnb = {key: [qq[key][h][2].astype(BF16) for h in heads] for key in qq}
            vb = {key: [_head(x, h).astype(BF16) for h in heads] for key, x in v_raw.items()}
            dob = {key: [_head(x, h).astype(BF16) for h in heads] for key, x in do_raw.items()}
            delta = {key: [jnp.sum(_head(do_raw[key], h) * _head(o_raw[key], h), axis=-1, keepdims=True)
                           for h in heads] for key in q_raw}
            full = [(r, t, h) for r in res for t in range(tb) for h in heads]
            half = [(r, tb, h) for r in res for h in heads]
            s = {}
            for r, t, h in full:
                sv = _dot(qnb[r, t][h], jnp.concatenate([knb[r, t - 1][h], knb[r, t][h]], axis=0), "nt") + bias_ref[h]
                s[r, t, h] = jnp.where((i == 0) & (col < B), NEG_BIG, sv) if t == 0 else sv
            for r, t, h in half:
                s[r, t, h] = _dot(qnb[r, t][h], knb[r, t - 1][h], "nt") + bias_ref[h, :, 0:B]
            lse_of = lambda r, t, h: l_raw[r, t][:, h * DSW_DH:h * DSW_DH + 1]
            p = {(r, t, h): jnp.exp(s[r, t, h] - lse_of(r, t, h)) for r, t, h in full}
            for r, t, h in half:
                p[r, t, h] = jnp.where(has_next, jnp.exp(s[r, t, h] - lse_of(r, t, h)), 0.0)
            dp = {(r, t, h): _dot(dob[r, t][h], jnp.concatenate([vb[r, t - 1][h], vb[r, t][h]], axis=0), "nt")
                  for r, t, h in full}
            for r, t, h in half:
                dp[r, t, h] = _dot(dob[r, t][h], vb[r, t - 1][h], "nt")
            ds = {(r, t, h): p[r, t, h] * (dp[r, t, h] - delta[r, t][h]) for r, t, h in full + half}
            pb = {it: p[it].astype(BF16) for it in ds}
            dsb = {it: ds[it].astype(BF16) for it in ds}
            for h in heads:
                tot = None
                for r in res:
                    for t in range(tb):
                        tot = ds[r, t, h] if tot is None else tot + ds[r, t, h]
                db_ref[h] += tot
            dqn = {(r, t, h): _dot(dsb[r, t, h], jnp.concatenate([knb[r, t - 1][h], knb[r, t][h]], axis=0))
                   for r, t, h in full}
            prev_half = lambda x, r, t, h: x[r, t, h][:, :B] if t < tb else x[r, t, h]
            dkn = {(r, t, h): _dot(dsb[r, t, h][:, B:], qnb[r, t][h], "tn")
                   + _dot(prev_half(dsb, r, t + 1, h), qnb[r, t + 1][h], "tn") for r, t, h in full}
            dvv = {(r, t, h): _dot(pb[r, t, h][:, B:], dob[r, t][h], "tn")
                   + _dot(prev_half(pb, r, t + 1, h), dob[r, t + 1][h], "tn") for r, t, h in full}
            for r, t, h in full:
                dqg_acc = dqg_acc + jnp.sum(dqn[r, t, h] * qq[r, t][h][0], axis=0, keepdims=True)
                dkg_acc = dkg_acc + jnp.sum(dkn[r, t, h] * kk[r, t][h][0], axis=0, keepdims=True)
            for r in res:
                for t in range(tb):
                    rows = _block_rows(t, r, slab, d)
                    dq_ref[rows, :] = jnp.concatenate(
                        [_rms64_bwd(dqn[r, t, h], qq[r, t][h][0], qq[r, t][h][1], qg) for h in heads], axis=1)
                    dk_ref[rows, :] = jnp.concatenate(
                        [_rms64_bwd(dkn[r, t, h], kk[r, t][h][0], kk[r, t][h][1], kg) for h in heads], axis=1)
                    dv_ref[rows, :] = jnp.concatenate([dvv[r, t, h] for h in heads], axis=1)

        start = (hp == 0) & (i == 0)

        @pl.when(start)
        def _():
            dqg_ref[...] = dqg_acc * scale
            dkg_ref[...] = dkg_acc

        @pl.when(jnp.logical_not(start))
        def _():
            dqg_ref[...] += dqg_acc * scale
            dkg_ref[...] += dkg_acc

    def spec(rows, pick, base):
        return pl.BlockSpec((rows, LANES), lambda hp, i: (pick(i), base + hp))

    same = lambda i: i
    before = lambda i: jnp.maximum(i * tb - 1, 0)
    after = lambda i: jnp.minimum((i + 1) * tb, n_slabs - 1)
    cur, cur1 = spec(rt, same, cb), spec(rt, same, 0)
    vec = pl.BlockSpec((1, DSW_DH), lambda hp, i: (0, 0))
    bspec = pl.BlockSpec((_HP, B, 2 * B), lambda hp, i: (hp, 0, 0))
    shp = jax.ShapeDtypeStruct((S, WT), F32)
    vshp = jax.ShapeDtypeStruct((1, DSW_DH), F32)
    carried = [] if prev_out is None else list(prev_out)
    n_in = 15
    return pl.pallas_call(
        body, name=name, grid=(_DSW_W // LANES, n_tiles),
        in_specs=[cur, spec(slab, after, cb), spec(slab, before, cb), cur, spec(slab, before, cb), cur,
                  cur1, spec(slab, after, 0), cur1, spec(slab, after, 0), cur1, spec(slab, after, 0),
                  bspec, vec, vec] + [pl.BlockSpec(memory_space=pl.ANY)] * len(carried),
        out_specs=[cur, cur, cur, bspec, vec, vec],
        out_shape=[shp, shp, shp, jax.ShapeDtypeStruct(bias.shape, F32), vshp, vshp],
        input_output_aliases={n_in + j: j for j in range(len(carried))},
        compiler_params=_params("arbitrary", "arbitrary"),
    )(q, q, k, k, v, v, o, o, lse, lse, do, do, bias, q_gain, k_gain, *carried)


def _t5_bucket(dist):
    max_exact = REL_BUCKETS // 2
    scaled = jnp.log(jnp.maximum(dist, 1).astype(F32) / max_exact) / math.log(REL_MAX_DIST / max_exact)
    large = jnp.minimum(max_exact + (scaled * (REL_BUCKETS - max_exact)).astype(jnp.int32), REL_BUCKETS - 1)
    return jnp.where(dist < max_exact, dist, large)


def _dsw_band():
    dist = (jnp.arange(DSW_BLK)[:, None] + DSW_BLK) - jnp.arange(2 * DSW_BLK)[None, :]
    return dist, (dist >= 0) & (dist <= DSW_BLK)


def _dsw_bias(rel_bias):
    dist, band = _dsw_band()
    out = []
    for g, (_, d) in enumerate(DSW_GROUPS):
        hot = jax.nn.one_hot(_t5_bucket(jnp.maximum(dist, 0) * d), REL_BUCKETS, dtype=F32)
        tab = jnp.einsum("qkb,bh->hqk", hot, rel_bias[:, g * DSW_HEADS:(g + 1) * DSW_HEADS],
                         precision=lax.Precision.HIGHEST)
        out.append(jnp.where(band[None], tab, NEG_BIG))
    return jnp.stack(out)


def _dsw_bucket_onehot():
    dist, band = _dsw_band()
    out = []
    for _, d in DSW_GROUPS:
        hot = jax.nn.one_hot(_t5_bucket(jnp.maximum(dist, 0) * d), LANES, dtype=BF16)
        out.append(jnp.where(band[..., None], hot, 0).reshape(-1, LANES))
    return jnp.stack(out)


def _exchange(send, *, gather, name):
    R, C = send.shape[-2:]

    def body(src_ref, dst_ref, send_sems, recv_sems, local_sem):
        x, y, c = lax.axis_index("x"), lax.axis_index("y"), lax.axis_index("c")
        me = 4 * x + 2 * y + c
        mine = pltpu.make_async_copy(src_ref if gather else src_ref.at[me], dst_ref.at[me], local_sem)
        mine.start()
        copies = []
        for rel in range(1, N_DEV):
            px = 1 - x if rel & 4 else x
            py = 1 - y if rel & 2 else y
            pc = 1 - c if rel & 1 else c
            peer = 4 * px + 2 * py + pc
            cp = pltpu.make_async_remote_copy(
                src_ref=src_ref if gather else src_ref.at[peer], dst_ref=dst_ref.at[me],
                send_sem=send_sems.at[rel - 1], recv_sem=recv_sems.at[rel - 1],
                device_id=(px, py, pc), device_id_type=pl.DeviceIdType.MESH)
            cp.start()
            copies.append(cp)
        for cp in copies:
            cp.wait()
        mine.wait()

    return pl.pallas_call(
        body, name=name,
        in_specs=[pl.BlockSpec(memory_space=pl.ANY)], out_specs=pl.BlockSpec(memory_space=pl.ANY),
        out_shape=jax.ShapeDtypeStruct((N_DEV, R, C), send.dtype),
        scratch_shapes=[pltpu.SemaphoreType.DMA((N_DEV - 1,)), pltpu.SemaphoreType.DMA((N_DEV - 1,)),
                        pltpu.SemaphoreType.DMA(())],
    )(send)


_BIG = ("w_ffn_in", "w_ffn_out", "gdn_w_in", "gdn_conv", "gdn_w_out", "dsw_w_in", "dsw_w_out")
_SHARD_AXIS = {"w_ffn_in": 2, "w_ffn_out": 1, "gdn_w_in": 2, "gdn_conv": 2, "gdn_w_out": 1, "dsw_w_in": 2,
               "dsw_w_out": 2}
_SMALL = ("b_ada", "norm_mix", "norm_ffn", "gdn_a_log", "gdn_dt_bias", "gdn_out_norm", "dsw_q_norm",
          "dsw_k_norm", "rel_bias")
_ROW_ALIGN = 16
_BIG_ALIGN = 1024


def _ceil_to(n, m):
    return -(-n // m) * m


def _seg_rows(shape):
    return _ceil_to(_ceil_to(int(np.prod(shape)), LANES) // LANES, _ROW_ALIGN)


def _pack(arrs, total_align):
    lead = arrs[0][1]
    segs = []
    for a, nlead in arrs:
        assert nlead == lead
        bshape = a.shape[:nlead]
        n = int(np.prod(a.shape[nlead:]))
        rows = _seg_rows(a.shape[nlead:])
        flat = a.reshape(bshape + (n,))
        flat = jnp.pad(flat, [(0, 0)] * nlead + [(0, rows * LANES - n)])
        segs.append(flat.reshape(bshape + (rows, LANES)))
    buf = jnp.concatenate(segs, axis=lead)
    total = _ceil_to(buf.shape[lead], total_align)
    return jnp.pad(buf, [(0, 0)] * lead + [(0, total - buf.shape[lead]), (0, 0)])


def _unpack(buf, shapes, nlead):
    out, off = [], 0
    for shp in shapes:
        n, rows = int(np.prod(shp)), _seg_rows(shp)
        seg = lax.slice_in_dim(buf, off, off + rows, axis=nlead)
        seg = seg.reshape(buf.shape[:nlead] + (rows * LANES,))[..., :n]
        out.append(seg.reshape(buf.shape[:nlead] + tuple(shp)))
        off += rows
    return out


def _to_natural(g, axis):
    n, L, r, c = g.shape
    if axis == 2:
        return jnp.transpose(g, (1, 2, 0, 3)).reshape(L, r, n * c)
    return jnp.transpose(g, (1, 0, 2, 3)).reshape(L, n * r, c)


def _to_blocked(w, axis):
    L, R, C = w.shape
    if axis == 2:
        return jnp.transpose(w.reshape(L, R, N_DEV, C // N_DEV), (2, 0, 1, 3))
    return jnp.transpose(w.reshape(L, N_DEV, R // N_DEV, C), (1, 0, 2, 3))


def _hm(a):
    lead = a.shape[:-1]
    return jnp.swapaxes(a.reshape(lead + (3, GDN_HEADS, GDN_DK)), -3, -2).reshape(lead + (3 * GDN_HEADS * GDN_DK,))


def _un_hm(a):
    lead = a.shape[:-1]
    return jnp.swapaxes(a.reshape(lead + (GDN_HEADS, 3, GDN_DK)), -3, -2).reshape(lead + (3 * GDN_HEADS * GDN_DK,))


_TILES = (1536, 1408, 1024, 768, 704, 512, 384, 256, 128, 64, 32, 16, 8)


def _tile(n, cap):
    for t in _TILES:
        if t <= cap and n % t == 0:
            return t
    return n


def _mm_auto(a, b, mode, name, **kw):
    if mode == "tn":
        (K, M), N = a.shape, b.shape[1]
        tm, tn, tk = _tile(M, 1408), _tile(N, 512), _tile(K, 512)
    else:
        M, K = a.shape
        N = b.shape[1] if mode == "nn" else b.shape[0]
        tm, tn, tk = _tile(M, 512), _tile(N, 1536), _tile(K, 1408)
    return _mm(a, b, mode=mode, name=name, tm=tm, tn=tn, tk=tk, **kw)


def _row(v):
    return v.reshape(1, -1)


def _ffn_fwd(x, mod, gain, w_in, w_out, tag):
    sh, sc, gate = mod
    h = _norm_mod_fwd(x, gain, sc, sh, name=f"ffn_norm_{tag}")
    p = _mm_auto(h, w_in, "nn", f"ffn_in_{tag}", out_dtype=BF16)
    a = _swiglu_fwd(p, name=f"ffn_act_{tag}")
    y = _mm_auto(a, w_out, "nn", f"ffn_out_{tag}", out_scale=gate, resid=x)
    return y, (x, h, p, a)


def _ffn_bwd(dy, saved, mod, gain, w_in, w_out, tag):
    sh, sc, gate = mod
    x, h, p, a = saved
    gmat = _mm_auto(a, dy, "tn", f"ffn_out_g_{tag}")
    dw_out, dgate = _wout_grad(gmat, w_out, gate, name=f"ffn_out_dw_{tag}")
    da = _mm_auto(dy, w_out, "nt", f"ffn_out_dx_{tag}", a_scale=gate, out_dtype=BF16)
    dp = _swiglu_bwd(da, p, name=f"ffn_act_bwd_{tag}")
    dw_in = _mm_auto(h, dp, "tn", f"ffn_in_dw_{tag}")
    dh = _mm_auto(dp, w_in, "nt", f"ffn_in_dx_{tag}")
    dx, dsh, dsc, dgain = _norm_mod_bwd(dh, x, dy, gain, sc, name=f"ffn_norm_bwd_{tag}")
    return dx, dict(w_in=dw_in, w_out=dw_out, gain=dgain, mod=(dsh, dsc, dgate))


def _gdn_fwd(x, mod, gain, W):
    sh, sc, gate = mod
    S = x.shape[0]
    h = _norm_mod_fwd(x, gain, sc, sh, name="gdn_norm")
    pq = _mm_auto(h, W["gdn_qkv"], "nn", "gdn_in_qkv")
    z = _mm_auto(h, W["gdn_z"], "nn", "gdn_in_z")
    ab = _mm_auto(h, W["gdn_ab"], "nn", "gdn_in_ab")
    qkvn = _gdn_prep_fwd(pq, W["gdn_conv"], name="gdn_prep")
    ab4 = jnp.transpose(ab[:, :2 * GDN_HEADS]).reshape(2 * GDN_HEADS, S // GDN_CHUNK, 1, GDN_CHUNK)
    o, states, tinvs = _gdn_chunk_fwd(qkvn, ab4, W["gdn_a_log"], W["gdn_dt_bias"], name="gdn_chunk")
    o2 = _gdn_outnorm_fwd(o, z, W["gdn_out_norm"], name="gdn_outnorm")
    y = _mm_auto(o2, W["gdn_out"], "nn", "gdn_out", out_scale=gate, resid=x)
    return y, (x, h, pq, z, qkvn, ab4, o, states, tinvs, o2)


def _gdn_bwd(dy, saved, mod, gain, W):
    sh, sc, gate = mod
    x, h, pq, z, qkvn, ab4, o, states, tinvs, o2 = saved
    S = x.shape[0]
    gmat = _mm_auto(o2, dy, "tn", "gdn_out_g")
    dw_out, dgate = _wout_grad(gmat, W["gdn_out"], gate, name="gdn_out_dw")
    do2 = _mm_auto(dy, W["gdn_out"], "nt", "gdn_out_dx", a_scale=gate)
    do, dz, dout_norm = _gdn_outnorm_bwd(do2, o, z, W["gdn_out_norm"], name="gdn_outnorm_bwd")
    dqkvn, da4, db4, da_log, ddt_bias = _gdn_chunk_bwd(
        qkvn, ab4, W["gdn_a_log"], W["gdn_dt_bias"], states, tinvs, do, name="gdn_chunk_bwd")
    dc, dconv8 = _gdn_prep_bwd_pre(dqkvn, pq, W["gdn_conv"], name="gdn_prep_bwd")
    dpq = _gdn_conv_bwd_x(dc, W["gdn_conv"], name="gdn_conv_bwd")
    dab = jnp.transpose(jnp.concatenate([da4, db4], axis=0).reshape(2 * GDN_HEADS, S))
    dab = jnp.pad(dab, ((0, 0), (0, LANES - 2 * GDN_HEADS))).astype(BF16)
    dw_qkv = _mm_auto(h, dpq, "tn", "gdn_in_qkv_dw")
    dw_z = _mm_auto(h, dz, "tn", "gdn_in_z_dw")
    dw_ab = _mm_auto(h, dab, "tn", "gdn_in_ab_dw")
    dh = _mm_auto(dpq, W["gdn_qkv"], "nt", "gdn_in_qkv_dx")
    dh = _mm_auto(dz, W["gdn_z"], "nt", "gdn_in_z_dx", resid=dh)
    dh = _mm_auto(dab, W["gdn_ab"], "nt", "gdn_in_ab_dx", resid=dh)
    dx, dsh, dsc, dgain = _norm_mod_bwd(dh, x, dy, gain, sc, name="gdn_norm_bwd")
    dw_in = jnp.concatenate([_un_hm(dw_qkv), dw_z, dw_ab[:, :2 * GDN_HEADS]], axis=1)
    return dx, dict(gdn_w_in=dw_in, gdn_conv=_un_hm(dconv8[:GDN_CONV]), gdn_w_out=dw_out, gdn_out_norm=dout_norm,
                    gdn_a_log=da_log.reshape(1, GDN_HEADS), gdn_dt_bias=ddt_bias.reshape(1, GDN_HEADS),
                    gain=dgain, mod=(dsh, dsc, dgate))


def _dsw_fwd(x, mod, gain, W):
    sh, sc, gate = mod
    h = _norm_mod_fwd(x, gain, sc, sh, name="dsw_norm")
    q, k, v = (_mm_auto(h, W[n], "nn", f"dsw_in_{n[-1]}") for n in ("dsw_q", "dsw_k", "dsw_v"))
    outs = None
    for g in range(len(DSW_GROUPS)):
        outs = _dsw_attn_fwd(q, k, v, W["dsw_bias"][g], W["dsw_q_norm"], W["dsw_k_norm"], outs, g=g,
                             name=f"dsw_attn_{g}")
    o, lse = _dsw_merge(*outs, name="dsw_merge")
    y = _mm_auto(o, W["dsw_out"], "nn", "dsw_out", out_scale=gate, resid=x)
    return y, (x, h, q, k, v, o, lse)


def _dsw_bwd(dy, saved, mod, gain, W):
    sh, sc, gate = mod
    x, h, q, k, v, o, lse = saved
    gmat = _mm_auto(o, dy, "tn", "dsw_out_g")
    dw_out, dgate = _wout_grad(gmat, W["dsw_out"], gate, name="dsw_out_dw")
    do = _mm_auto(dy, W["dsw_out"], "nt", "dsw_out_dx", a_scale=gate)
    G = len(DSW_GROUPS)
    dqkv, dbias, dq_norm, dk_norm = None, [], 0.0, 0.0
    for g in range(G):
        *dqkv, db, dqg, dkg = _dsw_attn_bwd(q, k, v, o, lse, do, W["dsw_bias"][g], W["dsw_q_norm"],
                                            W["dsw_k_norm"], dqkv, g=g, name=f"dsw_attn_bwd_{g}")
        dbias.append(db)
        dq_norm, dk_norm = dq_norm + dqg, dk_norm + dkg
    dws, dh = [], None
    for n, d in zip(("dsw_q", "dsw_k", "dsw_v"), dqkv):
        dws.append(_mm_auto(h, d, "tn", f"dsw_in_{n[-1]}_dw"))
        dh = _mm_auto(d, W[n], "nt", f"dsw_in_{n[-1]}_dx", **({} if dh is None else {"resid": dh}))
    dx, dsh, dsc, dgain = _norm_mod_bwd(dh, x, dy, gain, sc, name="dsw_norm_bwd")
    hot = _dsw_bucket_onehot()
    drel = [_mm_auto(dbias[g].reshape(DSW_HEADS, -1), hot[g], "nn", f"dsw_rel_bias_{g}")[:, :REL_BUCKETS]
            for g in range(G)]
    return dx, dict(dsw_w_in=jnp.concatenate(dws, axis=1), dsw_w_out=dw_out, dsw_q_norm=dq_norm,
                    dsw_k_norm=dk_norm, rel_bias=jnp.transpose(jnp.concatenate(drel, axis=0)),
                    gain=dgain, mod=(dsh, dsc, dgate))


def _local_step(x, target, mod, W):
    mods = [[_row(mod[l, i]) for i in range(6)] for l in range(2)]
    nmix = [_row(W["norm_mix"][l]) for l in range(2)]
    nffn = [_row(W["norm_ffn"][l]) for l in range(2)]
    x1, s_gdn = _gdn_fwd(x, mods[0][:3], nmix[0], W)
    x2, s_f0 = _ffn_fwd(x1, mods[0][3:], nffn[0], W["w_ffn_in"][0], W["w_ffn_out"][0], "0")
    x3, s_dsw = _dsw_fwd(x2, mods[1][:3], nmix[1], W)
    x4, s_f1 = _ffn_fwd(x3, mods[1][3:], nffn[1], W["w_ffn_in"][1], W["w_ffn_out"][1], "1")
    dx4, sse = _loss_head(x4, target, name="loss_head")
    dx3, g_f1 = _ffn_bwd(dx4, s_f1, mods[1][3:], nffn[1], W["w_ffn_in"][1], W["w_ffn_out"][1], "1")
    dx2, g_dsw = _dsw_bwd(dx3, s_dsw, mods[1][:3], nmix[1], W)
    dx1, g_f0 = _ffn_bwd(dx2, s_f0, mods[0][3:], nffn[0], W["w_ffn_in"][0], W["w_ffn_out"][0], "0")
    dx0, g_gdn = _gdn_bwd(dx1, s_gdn, mods[0][:3], nmix[0], W)
    dmod = jnp.stack([jnp.concatenate(list(g_gdn["mod"]) + list(g_f0["mod"]), axis=0),
                      jnp.concatenate(list(g_dsw["mod"]) + list(g_f1["mod"]), axis=0)])
    grads = dict(
        w_ffn_in=jnp.stack([g_f0["w_in"], g_f1["w_in"]]), w_ffn_out=jnp.stack([g_f0["w_out"], g_f1["w_out"]]),
        norm_mix=jnp.concatenate([g_gdn["gain"], g_dsw["gain"]], axis=0),
        norm_ffn=jnp.concatenate([g_f0["gain"], g_f1["gain"]], axis=0),
        gdn_w_in=g_gdn["gdn_w_in"][None], gdn_conv=g_gdn["gdn_conv"][None], gdn_w_out=g_gdn["gdn_w_out"][None],
        gdn_out_norm=g_gdn["gdn_out_norm"], gdn_a_log=g_gdn["gdn_a_log"], gdn_dt_bias=g_gdn["gdn_dt_bias"],
        dsw_w_in=g_dsw["dsw_w_in"][None], dsw_w_out=g_dsw["dsw_w_out"][None],
        dsw_q_norm=g_dsw["dsw_q_norm"], dsw_k_norm=g_dsw["dsw_k_norm"], rel_bias=g_dsw["rel_bias"])
    return sse, dx0, grads, dmod


def _prepare_weights(full, small):
    gw = full["gdn_w_in"][0]
    hk3 = 3 * GDN_HEADS * GDN_DK
    di = full["dsw_w_in"][0]
    dq = di.shape[1] // 3
    return dict(
        w_ffn_in=full["w_ffn_in"], w_ffn_out=full["w_ffn_out"],
        gdn_qkv=_hm(gw[:, :hk3]), gdn_z=gw[:, hk3:hk3 + GDN_HEADS * GDN_DK],
        gdn_ab=jnp.pad(gw[:, hk3 + GDN_HEADS * GDN_DK:], ((0, 0), (0, LANES - 2 * GDN_HEADS))),
        gdn_conv=_hm(full["gdn_conv"][0]), gdn_out=full["gdn_w_out"][0],
        dsw_q=di[:, :dq], dsw_k=di[:, dq:2 * dq], dsw_v=di[:, 2 * dq:], dsw_out=full["dsw_w_out"][0],
        norm_mix=small["norm_mix"], norm_ffn=small["norm_ffn"],
        gdn_a_log=small["gdn_a_log"].reshape(GDN_HEADS, 1, 1), gdn_dt_bias=small["gdn_dt_bias"].reshape(GDN_HEADS, 1, 1),
        gdn_out_norm=small["gdn_out_norm"], dsw_q_norm=small["dsw_q_norm"], dsw_k_norm=small["dsw_k_norm"],
        dsw_bias=_dsw_bias(small["rel_bias"]))


_W_NAMES = ("w_ada", "b_ada", "norm_mix", "norm_ffn", "w_ffn_in", "w_ffn_out", "gdn_w_in", "gdn_conv",
            "gdn_a_log", "gdn_dt_bias", "gdn_out_norm", "gdn_w_out", "dsw_w_in", "dsw_q_norm", "dsw_k_norm",
            "dsw_w_out", "rel_bias")
_PAD_BATCH = 16


def _pad_rows(a, rows):
    return jnp.pad(a, ((0, rows - a.shape[0]), (0, 0)))


def kernel(x, c, w_ada, b_ada, norm_mix, norm_ffn, w_ffn_in, w_ffn_out, gdn_w_in, gdn_conv, gdn_a_log, gdn_dt_bias, gdn_out_norm, gdn_w_out, dsw_w_in, dsw_q_norm, dsw_k_norm, dsw_w_out, rel_bias, loss_target, m_w_ada, m_b_ada, m_norm_mix, m_norm_ffn, m_w_ffn_in, m_w_ffn_out, m_gdn_w_in, m_gdn_conv, m_gdn_a_log, m_gdn_dt_bias, m_gdn_out_norm, m_gdn_w_out, m_dsw_w_in, m_dsw_q_norm, m_dsw_k_norm, m_dsw_w_out, m_rel_bias, v_w_ada, v_b_ada, v_norm_mix, v_norm_ffn, v_w_ffn_in, v_w_ffn_out, v_gdn_w_in, v_gdn_conv, v_gdn_a_log, v_gdn_dt_bias, v_gdn_out_norm, v_gdn_w_out, v_dsw_w_in, v_dsw_q_norm, v_dsw_k_norm, v_dsw_w_out, v_rel_bias):
    w = dict(zip(_W_NAMES, (w_ada, b_ada, norm_mix, norm_ffn, w_ffn_in, w_ffn_out, gdn_w_in, gdn_conv, gdn_a_log,
                            gdn_dt_bias, gdn_out_norm, gdn_w_out, dsw_w_in, dsw_q_norm, dsw_k_norm, dsw_w_out,
                            rel_bias)))
    m = dict(zip(_W_NAMES, (m_w_ada, m_b_ada, m_norm_mix, m_norm_ffn, m_w_ffn_in, m_w_ffn_out, m_gdn_w_in,
                            m_gdn_conv, m_gdn_a_log, m_gdn_dt_bias, m_gdn_out_norm, m_gdn_w_out, m_dsw_w_in,
                            m_dsw_q_norm, m_dsw_k_norm, m_dsw_w_out, m_rel_bias)))
    v = dict(zip(_W_NAMES, (v_w_ada, v_b_ada, v_norm_mix, v_norm_ffn, v_w_ffn_in, v_w_ffn_out, v_gdn_w_in,
                            v_gdn_conv, v_gdn_a_log, v_gdn_dt_bias, v_gdn_out_norm, v_gdn_w_out, v_dsw_w_in,
                            v_dsw_q_norm, v_dsw_k_norm, v_dsw_w_out, v_rel_bias)))
    D = x.shape[-1]
    n_layers, _, ada_cols = w_ada.shape

    c_all = _exchange(c.reshape(D // LANES, LANES), gather=True, name="gather_cond").reshape(N_DEV, D)
    c_pad = _pad_rows(c_all, _PAD_BATCH)
    proj = [_mm(c_pad, w_ada[l], mode="nn", name=f"ada_proj_{l}", tm=_PAD_BATCH, tn=ada_cols, tk=D, a_silu=True)
            for l in range(n_layers)]
    mod_send = _pack([(jnp.stack([p[:N_DEV] for p in proj], axis=1), 1)], _ROW_ALIGN)
    mod_recv = _exchange(mod_send, gather=False, name="scatter_mod")
    mod = _unpack(mod_recv, [(n_layers, ada_cols)], 1)[0]
    mod = jnp.transpose(mod, (1, 0, 2)).reshape(n_layers, N_DEV * ada_cols) + b_ada
    mod = mod.reshape(n_layers, 6, D)

    conv_hi = gdn_conv.astype(BF16)
    conv_lo = (gdn_conv - conv_hi.astype(F32)).astype(BF16)
    w_send = _pack([(conv_hi if n == "gdn_conv" else w[n].astype(BF16), 0) for n in _BIG] + [(conv_lo, 0)],
                   _ROW_ALIGN)
    w_all = _exchange(w_send, gather=True, name="gather_weights")
    parts = _unpack(w_all, [w[n].shape for n in _BIG] + [gdn_conv.shape], 1)
    full = {n: _to_natural(parts[i], _SHARD_AXIS[n]) for i, n in enumerate(_BIG)}
    full["gdn_conv"] = full["gdn_conv"].astype(F32) + _to_natural(parts[-1], _SHARD_AXIS["gdn_conv"]).astype(F32)
    W = _prepare_weights(full, {n: w[n] for n in _SMALL})

    sse, grad_x, grads, dmod = _local_step(x[0], loss_target[0], mod, W)
    loss = lax.psum(0.5 * sse[0, 0] / D, ("x", "y", "c"))

    grads["b_ada"] = dmod.reshape(n_layers, 6 * D)
    big_send = _pack([(_to_blocked(grads[n], _SHARD_AXIS[n]), 1) for n in _BIG], _BIG_ALIGN)
    dmod_send = _pack([(jnp.transpose(dmod.reshape(n_layers, N_DEV, ada_cols), (1, 0, 2)), 1)], _ROW_ALIGN)
    small_send = _pack([(grads[n].reshape(w[n].shape), 0) for n in _SMALL], _ROW_ALIGN)
    g_send = jnp.concatenate(
        [big_send, dmod_send, jnp.broadcast_to(small_send[None], (N_DEV,) + small_send.shape)],
        axis=1).astype(BF16)
    g_recv = _exchange(g_send, gather=False, name="scatter_grads")
    big_rows, dmod_rows = big_send.shape[1], dmod_send.shape[1]

    out = {}
    packed = [_pack([(t[n], 0) for n in _BIG], _BIG_ALIGN) for t in (w, m, v)]
    res = _adamw(packed[0], g_recv, packed[1], packed[2], name="adamw_sharded")
    for kind, buf in zip(("grad", "delta", "new_m", "new_v"), res):
        for n, a in zip(_BIG, _unpack(buf, [w[n].shape for n in _BIG], 0)):
            out[kind, n] = a

    dmod_all = _unpack(lax.slice_in_dim(g_recv, big_rows, big_rows + dmod_rows, axis=1),
                       [(n_layers, ada_cols)], 1)[0]
    g_ada = jnp.stack([_mm(c_pad, _pad_rows(dmod_all[:, l], _PAD_BATCH), mode="tn", name=f"ada_dw_{l}",
                           tm=D, tn=ada_cols, tk=_PAD_BATCH, a_silu=True) for l in range(n_layers)])
    flat = lambda a: a.reshape(n_layers * D, ada_cols)
    res = _adamw(flat(w_ada), flat(g_ada)[None], flat(m_w_ada), flat(v_w_ada), name="adamw_ada")
    for kind, buf in zip(("grad", "delta", "new_m", "new_v"), res):
        out[kind, "w_ada"] = buf.reshape(w_ada.shape)

    small_parts = lax.slice_in_dim(g_recv, big_rows + dmod_rows, g_recv.shape[1], axis=1)
    packed = [_pack([(t[n], 0) for n in _SMALL], _ROW_ALIGN) for t in (w, m, v)]
    res = _adamw(packed[0], small_parts, packed[1], packed[2], name="adamw_replicated")
    for kind, buf in zip(("grad", "delta", "new_m", "new_v"), res):
        for n, a in zip(_SMALL, _unpack(buf, [w[n].shape for n in _SMALL], 0)):
            out[kind, n] = a

    return (loss, grad_x[None]) + tuple(out[kind, n] for kind in ("grad", "delta", "new_m", "new_v")
                                        for n in _W_NAMES)
```

```python
import functools
import math

import numpy as np
import jax
import jax.numpy as jnp
from jax import lax
from jax.experimental import pallas as pl
from jax.experimental.pallas import tpu as pltpu

F32 = jnp.float32
BF16 = jnp.bfloat16

N_DEV = 8
RMS_EPS = 1e-6
LANES = 128
V7X_VMEM_LIMIT = 48 * 1024 * 1024

GDN_HEADS = 8
GDN_DK = 128
GDN_CHUNK = 64
GDN_CONV = 4
DSW_GROUPS = ((128, 1), (512, 4), (2048, 16))
DSW_HEADS = 8
DSW_DH = 64
DSW_BLK = 128
REL_BUCKETS = 32
REL_MAX_DIST = 2048

ADAM_LR = 0.001
ADAM_B1 = 0.9
ADAM_B2 = 0.999
ADAM_EPS = 1e-08
ADAM_WD = 0.01
ADAM_STEP = 10

NEG_BIG = -1e30


def _params(*sem):
    return pltpu.CompilerParams(dimension_semantics=sem, vmem_limit_bytes=V7X_VMEM_LIMIT)


def _sigmoid(x):
    return 1.0 / (1.0 + jnp.exp(-x))


def _silu(x):
    return x * _sigmoid(x)


_DOT_DIMS = {
    "nn": (((1,), (0,)), ((), ())),
    "nt": (((1,), (1,)), ((), ())),
    "tn": (((0,), (0,)), ((), ())),
}


def _mm(a, b, *, mode, name, tm, tn, tk, out_dtype=F32, a_scale=None, out_scale=None, resid=None, a_silu=False):
    if mode == "nn":
        (M, K), N = a.shape, b.shape[1]
    elif mode == "nt":
        (M, K), N = a.shape, b.shape[0]
    else:
        (K, M), N = a.shape, b.shape[1]
    tm, tn, tk = min(tm, M), min(tn, N), min(tk, K)
    assert M % tm == 0 and N % tn == 0 and K % tk == 0, (name, M, N, K, tm, tn, tk)
    nk = K // tk

    def body(*refs):
        refs = list(refs)
        a_ref, b_ref = refs.pop(0), refs.pop(0)
        as_ref = refs.pop(0) if a_scale is not None else None
        os_ref = refs.pop(0) if out_scale is not None else None
        r_ref = refs.pop(0) if resid is not None else None
        o_ref = refs.pop(0)
        acc_ref = refs.pop(0) if nk > 1 else None

        av = a_ref[...]
        if a_silu:
            av = _silu(av.astype(F32))
        if as_ref is not None:
            av = av.astype(F32) * as_ref[...]
        part = lax.dot_general(av.astype(BF16), b_ref[...].astype(BF16), _DOT_DIMS[mode],
                               preferred_element_type=F32)

        def finish(r):
            if os_ref is not None:
                r = r * os_ref[...]
            if r_ref is not None:
                r = r + r_ref[...].astype(F32)
            o_ref[...] = r.astype(out_dtype)

        if nk == 1:
            finish(part)
        else:
            k = pl.program_id(2)

            @pl.when(k == 0)
            def _():
                acc_ref[...] = part

            @pl.when(k > 0)
            def _():
                acc_ref[...] += part

            @pl.when(k == nk - 1)
            def _():
                finish(acc_ref[...])

    if mode == "nn":
        a_spec = pl.BlockSpec((tm, tk), lambda i, j, k: (i, k))
        b_spec = pl.BlockSpec((tk, tn), lambda i, j, k: (k, j))
        as_spec = pl.BlockSpec((1, tk), lambda i, j, k: (0, k))
    elif mode == "nt":
        a_spec = pl.BlockSpec((tm, tk), lambda i, j, k: (i, k))
        b_spec = pl.BlockSpec((tn, tk), lambda i, j, k: (j, k))
        as_spec = pl.BlockSpec((1, tk), lambda i, j, k: (0, k))
    else:
        a_spec = pl.BlockSpec((tk, tm), lambda i, j, k: (k, i))
        b_spec = pl.BlockSpec((tk, tn), lambda i, j, k: (k, j))
        as_spec = None
    in_specs, args = [a_spec, b_spec], [a, b]
    if a_scale is not None:
        in_specs.append(as_spec)
        args.append(a_scale)
    if out_scale is not None:
        in_specs.append(pl.BlockSpec((1, tn), lambda i, j, k: (0, j)))
        args.append(out_scale)
    if resid is not None:
        in_specs.append(pl.BlockSpec((tm, tn), lambda i, j, k: (i, j)))
        args.append(resid)
    return pl.pallas_call(
        body, name=name, grid=(M // tm, N // tn, nk),
        in_specs=in_specs, out_specs=pl.BlockSpec((tm, tn), lambda i, j, k: (i, j)),
        out_shape=jax.ShapeDtypeStruct((M, N), out_dtype),
        scratch_shapes=[pltpu.VMEM((tm, tn), F32)] if nk > 1 else [],
        compiler_params=_params("parallel", "parallel", "arbitrary"),
    )(*args)


def _norm_mod_fwd(x, gain, sc, sh, *, name):
    S, D = x.shape
    tr = min(512, S)

    def body(x_ref, g_ref, sc_ref, sh_ref, h_ref):
        xv = x_ref[...]
        r = lax.rsqrt(jnp.mean(xv * xv, axis=-1, keepdims=True) + RMS_EPS)
        h_ref[...] = ((xv * r) * g_ref[...] * (1.0 + sc_ref[...]) + sh_ref[...]).astype(BF16)

    row = pl.BlockSpec((tr, D), lambda i: (i, 0))
    vec = pl.BlockSpec((1, D), lambda i: (0, 0))
    return pl.pallas_call(
        body, name=name, grid=(S // tr,), in_specs=[row, vec, vec, vec], out_specs=row,
        out_shape=jax.ShapeDtypeStruct((S, D), BF16), compiler_params=_params("parallel"),
    )(x, gain, sc, sh)


def _norm_mod_bwd(dh, x, dx_res, gain, sc, *, name):
    S, D = x.shape
    tr = min(256, S)
    n_steps = S // tr

    def body(dh_ref, x_ref, dxr_ref, g_ref, sc_ref, dx_ref, dsh_ref, dsc_ref, dgain_ref, acc_sh, acc_a):
        i = pl.program_id(0)
        xv = x_ref[...]
        r = lax.rsqrt(jnp.mean(xv * xv, axis=-1, keepdims=True) + RMS_EPS)
        n = xv * r
        dhv = dh_ref[...].astype(F32)
        dn = dhv * (g_ref[...] * (1.0 + sc_ref[...]))
        dx_ref[...] = dxr_ref[...] + r * (dn - n * jnp.mean(dn * n, axis=-1, keepdims=True))
        p_sh = jnp.sum(dhv, axis=0, keepdims=True)
        p_a = jnp.sum(dhv * n, axis=0, keepdims=True)

        @pl.when(i == 0)
        def _():
            acc_sh[...] = p_sh
            acc_a[...] = p_a

        @pl.when(i > 0)
        def _():
            acc_sh[...] += p_sh
            acc_a[...] += p_a

        @pl.when(i == n_steps - 1)
        def _():
            dsh_ref[...] = acc_sh[...]
            dsc_ref[...] = acc_a[...] * g_ref[...]
            dgain_ref[...] = acc_a[...] * (1.0 + sc_ref[...])

    row = pl.BlockSpec((tr, D), lambda i: (i, 0))
    vec = pl.BlockSpec((1, D), lambda i: (0, 0))
    vshape = jax.ShapeDtypeStruct((1, D), F32)
    return pl.pallas_call(
        body, name=name, grid=(n_steps,), in_specs=[row, row, row, vec, vec],
        out_specs=[row, vec, vec, vec],
        out_shape=[jax.ShapeDtypeStruct((S, D), F32), vshape, vshape, vshape],
        scratch_shapes=[pltpu.VMEM((1, D), F32), pltpu.VMEM((1, D), F32)],
        compiler_params=_params("arbitrary"),
    )(dh, x, dx_res, gain, sc)


def _wout_grad(gmat, w, gate, *, name):
    K, D = w.shape
    tr = min(256, K)
    n_steps = K // tr

    def body(g_ref, w_ref, gate_ref, dw_ref, dgate_ref, acc):
        i = pl.program_id(0)
        gv = g_ref[...]
        dw_ref[...] = gv * gate_ref[...]
        part = jnp.sum(gv * w_ref[...], axis=0, keepdims=True)

        @pl.when(i == 0)
        def _():
            acc[...] = part

        @pl.when(i > 0)
        def _():
            acc[...] += part

        @pl.when(i == n_steps - 1)
        def _():
            dgate_ref[...] = acc[...]

    row = pl.BlockSpec((tr, D), lambda i: (i, 0))
    vec = pl.BlockSpec((1, D), lambda i: (0, 0))
    return pl.pallas_call(
        body, name=name, grid=(n_steps,), in_specs=[row, row, vec], out_specs=[row, vec],
        out_shape=[jax.ShapeDtypeStruct((K, D), F32), jax.ShapeDtypeStruct((1, D), F32)],
        scratch_shapes=[pltpu.VMEM((1, D), F32)], compiler_params=_params("arbitrary"),
    )(gmat, w, gate)


def _swiglu_fwd(p, *, name):
    S, F2 = p.shape
    F = F2 // 2
    tr = min(256, S)

    def body(p_ref, a_ref):
        gate = p_ref[:, :F].astype(F32)
        up = p_ref[:, F:].astype(F32)
        a_ref[...] = (_silu(gate) * up).astype(BF16)

    return pl.pallas_call(
        body, name=name, grid=(S // tr,), in_specs=[pl.BlockSpec((tr, F2), lambda i: (i, 0))],
        out_specs=pl.BlockSpec((tr, F), lambda i: (i, 0)),
        out_shape=jax.ShapeDtypeStruct((S, F), BF16), compiler_params=_params("parallel"),
    )(p)


def _swiglu_bwd(da, p, *, name):
    S, F2 = p.shape
    F = F2 // 2
    tr = min(256, S)

    def body(da_ref, p_ref, dp_ref):
        gate = p_ref[:, :F].astype(F32)
        up = p_ref[:, F:].astype(F32)
        dav = da_ref[...].astype(F32)
        sg = _sigmoid(gate)
        dp_ref[:, :F] = (dav * up * (sg * (1.0 + gate * (1.0 - sg)))).astype(BF16)
        dp_ref[:, F:] = (dav * (gate * sg)).astype(BF16)

    return pl.pallas_call(
        body, name=name, grid=(S // tr,),
        in_specs=[pl.BlockSpec((tr, F), lambda i: (i, 0)), pl.BlockSpec((tr, F2), lambda i: (i, 0))],
        out_specs=pl.BlockSpec((tr, F2), lambda i: (i, 0)),
        out_shape=jax.ShapeDtypeStruct((S, F2), BF16), compiler_params=_params("parallel"),
    )(da, p)


def _loss_head(y, target, *, name):
    S, D = y.shape
    tr = min(512, S)
    n_steps = S // tr

    def body(y_ref, t_ref, dy_ref, sse_ref, acc):
        i = pl.program_id(0)
        e = y_ref[...] - t_ref[...]
        dy_ref[...] = e * (1.0 / D)
        part = jnp.sum(e * e, axis=0, keepdims=True)

        @pl.when(i == 0)
        def _():
            acc[...] = part

        @pl.when(i > 0)
        def _():
            acc[...] += part

        @pl.when(i == n_steps - 1)
        def _():
            sse_ref[...] = jnp.sum(acc[...], axis=1, keepdims=True)

    row = pl.BlockSpec((tr, D), lambda i: (i, 0))
    return pl.pallas_call(
        body, name=name, grid=(n_steps,), in_specs=[row, row],
        out_specs=[row, pl.BlockSpec((1, 1), lambda i: (0, 0))],
        out_shape=[jax.ShapeDtypeStruct((S, D), F32), jax.ShapeDtypeStruct((1, 1), F32)],
        scratch_shapes=[pltpu.VMEM((1, D), F32)], compiler_params=_params("arbitrary"),
    )(y, target)


def _adamw(w, g_parts, m, v, *, name):
    R, C = w.shape
    P = g_parts.shape[0]
    tr = _tile(R, max(8, 1024 * LANES // C))
    c1 = 1.0 / (1.0 - ADAM_B1 ** ADAM_STEP)
    c2 = 1.0 / (1.0 - ADAM_B2 ** ADAM_STEP)

    def body(w_ref, g_ref, m_ref, v_ref, go_ref, d_ref, mo_ref, vo_ref):
        g = g_ref[0].astype(F32)
        for q in range(1, P):
            g = g + g_ref[q].astype(F32)
        mn = ADAM_B1 * m_ref[...] + (1.0 - ADAM_B1) * g
        vn = ADAM_B2 * v_ref[...] + (1.0 - ADAM_B2) * (g * g)
        go_ref[...] = g
        mo_ref[...] = mn
        vo_ref[...] = vn
        d_ref[...] = -ADAM_LR * ((mn * c1) / (jnp.sqrt(vn * c2) + ADAM_EPS) + ADAM_WD * w_ref[...])

    row = pl.BlockSpec((tr, C), lambda i: (i, 0))
    shp = jax.ShapeDtypeStruct((R, C), F32)
    return pl.pallas_call(
        body, name=name, grid=(R // tr,),
        in_specs=[row, pl.BlockSpec((P, tr, C), lambda i: (0, i, 0)), row, row],
        out_specs=[row, row, row, row], out_shape=[shp, shp, shp, shp],
        compiler_params=_params("parallel"),
    )(w, g_parts, m, v)


_HALO = 8


def _conv_taps(buf, w_ref, rows):
    acc = None
    for j in range(GDN_CONV):
        term = buf[pl.ds(_HALO - (GDN_CONV - 1) + j, rows), :] * w_ref[j:j + 1, :]
        acc = term if acc is None else acc + term
    return acc


def _fill_conv_buf(buf, halo_ref, x_ref, rows, first):
    buf[0:_HALO, :] = jnp.where(first, 0.0, halo_ref[...])
    buf[_HALO:_HALO + rows, :] = x_ref[...]


_HM = 3 * GDN_DK
_GDN_ROWS = 256


def _l2n(seg):
    return lax.rsqrt(jnp.sum(seg * seg, axis=-1, keepdims=True) + RMS_EPS)


def _gdn_prep_fwd(x, conv_w, *, name):
    S, C3 = x.shape
    CB = _HM
    RB = min(256, S)

    def body(x_ref, halo_ref, w_ref, o_ref, buf):
        i = pl.program_id(0)
        _fill_conv_buf(buf, halo_ref, x_ref, RB, i == 0)
        y = _silu(_conv_taps(buf, w_ref, RB))
        q, k = y[:, :GDN_DK], y[:, GDN_DK:2 * GDN_DK]
        o_ref[:, :GDN_DK] = q * (_l2n(q) * GDN_DK ** -0.5)
        o_ref[:, GDN_DK:2 * GDN_DK] = k * _l2n(k)
        o_ref[:, 2 * GDN_DK:] = y[:, 2 * GDN_DK:]

    hb = RB // _HALO
    return pl.pallas_call(
        body, name=name, grid=(S // RB, C3 // CB),
        in_specs=[pl.BlockSpec((RB, CB), lambda i, j: (i, j)),
                  pl.BlockSpec((_HALO, CB), lambda i, j: (jnp.maximum(i * hb - 1, 0), j)),
                  pl.BlockSpec((GDN_CONV, CB), lambda i, j: (0, j))],
        out_specs=pl.BlockSpec((RB, CB), lambda i, j: (i, j)),
        out_shape=jax.ShapeDtypeStruct((S, C3), F32),
        scratch_shapes=[pltpu.VMEM((RB + _HALO, CB), F32)],
        compiler_params=_params("parallel", "parallel"),
    )(x, x, conv_w)


def _gdn_prep_bwd_pre(dn, x, conv_w, *, name):
    S, C3 = x.shape
    CB = _HM
    RB = min(256, S)
    n_steps = S // RB

    def body(dn_ref, x_ref, halo_ref, w_ref, dc_ref, dw_ref, buf):
        i = pl.program_id(1)
        _fill_conv_buf(buf, halo_ref, x_ref, RB, i == 0)
        acc = _conv_taps(buf, w_ref, RB)
        sg = _sigmoid(acc)
        y = acc * sg
        dsilu = sg * (1.0 + acc * (1.0 - sg))
        for part, scale in ((0, GDN_DK ** -0.5), (1, 1.0)):
            sl = slice(part * GDN_DK, (part + 1) * GDN_DK)
            seg = y[:, sl]
            r = _l2n(seg)
            n = seg * r
            d = dn_ref[:, sl] * scale
            dc_ref[:, sl] = r * (d - n * jnp.sum(d * n, axis=-1, keepdims=True)) * dsilu[:, sl]
        dc_ref[:, 2 * GDN_DK:] = dn_ref[:, 2 * GDN_DK:] * dsilu[:, 2 * GDN_DK:]
        dc = dc_ref[...]
        parts = [jnp.sum(dc * buf[pl.ds(_HALO - (GDN_CONV - 1) + t, RB), :], axis=0, keepdims=True)
                 for t in range(GDN_CONV)]
        part = jnp.concatenate(parts + [jnp.zeros((8 - GDN_CONV, CB), F32)], axis=0)

        @pl.when(i == 0)
        def _():
            dw_ref[...] = part

        @pl.when(i > 0)
        def _():
            dw_ref[...] += part

    hb = RB // _HALO
    return pl.pallas_call(
        body, name=name, grid=(C3 // CB, n_steps),
        in_specs=[pl.BlockSpec((RB, CB), lambda j, i: (i, j)),
                  pl.BlockSpec((RB, CB), lambda j, i: (i, j)),
                  pl.BlockSpec((_HALO, CB), lambda j, i: (jnp.maximum(i * hb - 1, 0), j)),
                  pl.BlockSpec((GDN_CONV, CB), lambda j, i: (0, j))],
        out_specs=[pl.BlockSpec((RB, CB), lambda j, i: (i, j)),
                   pl.BlockSpec((8, CB), lambda j, i: (0, j))],
        out_shape=[jax.ShapeDtypeStruct((S, C3), F32), jax.ShapeDtypeStruct((8, C3), F32)],
        scratch_shapes=[pltpu.VMEM((RB + _HALO, CB), F32)],
        compiler_params=_params("parallel", "arbitrary"),
    )(dn, x, x, conv_w)


def _gdn_conv_bwd_x(dc, conv_w, *, name):
    S, C3 = dc.shape
    CB = _HM
    RB = min(256, S)
    n_steps = S // RB

    def body(dc_ref, halo_ref, w_ref, dx_ref, buf):
        i = pl.program_id(0)
        buf[0:RB, :] = dc_ref[...]
        buf[RB:RB + _HALO, :] = jnp.where(i == n_steps - 1, 0.0, halo_ref[...])
        acc = None
        for j in range(GDN_CONV):
            term = buf[pl.ds(GDN_CONV - 1 - j, RB), :] * w_ref[j:j + 1, :]
            acc = term if acc is None else acc + term
        dx_ref[...] = acc.astype(BF16)

    hb = RB // _HALO
    last = S // _HALO - 1
    return pl.pallas_call(
        body, name=name, grid=(n_steps, C3 // CB),
        in_specs=[pl.BlockSpec((RB, CB), lambda i, j: (i, j)),
                  pl.BlockSpec((_HALO, CB), lambda i, j: (jnp.minimum((i + 1) * hb, last), j)),
                  pl.BlockSpec((GDN_CONV, CB), lambda i, j: (0, j))],
        out_specs=pl.BlockSpec((RB, CB), lambda i, j: (i, j)),
        out_shape=jax.ShapeDtypeStruct((S, C3), BF16),
        scratch_shapes=[pltpu.VMEM((RB + _HALO, CB), F32)],
        compiler_params=_params("parallel", "parallel"),
    )(dc, dc, conv_w)


def _split_bf16(a):
    hi = a.astype(BF16)
    return hi, (a - hi.astype(F32)).astype(BF16)


def _dot(a, b, dims="nn", exact=False):
    def dot(p, q):
        return lax.dot_general(p, q, _DOT_DIMS[dims], preferred_element_type=F32)

    if exact:
        (ah, al), (bh, bl) = _split_bf16(a), _split_bf16(b)
        return dot(ah, bh) + (dot(ah, bl) + dot(al, bh))
    return dot(a.astype(BF16), b.astype(BF16))


def _softplus(x):
    return jnp.maximum(x, 0.0) + jnp.log(1.0 + jnp.exp(-jnp.abs(x)))


def _to_col(row, eye):
    return jnp.sum(jnp.where(eye, row, 0.0), axis=1, keepdims=True)


def _to_row(col, eye):
    return jnp.sum(jnp.where(eye, col, 0.0), axis=0, keepdims=True)


def _unit_lower_inverse(low, ri, ci):
    n = range(len(low))
    C = low[0].shape[0]
    eye = jnp.where(ri == ci, 1.0, 0.0)
    pair = (ri >> 1) == (ci >> 1)
    x = [eye - jnp.where(pair, low[j], 0.0) for j in n]
    m, sh = 2, 1
    while m < C:
        join = ((ri >> (sh + 1)) == (ci >> (sh + 1))) & (((ri >> sh) & 1) == 1) & (((ci >> sh) & 1) == 0)
        y = [_dot(x[j], jnp.where(join, low[j], 0.0)) for j in n]
        x = [x[j] - _dot(y[j], x[j]) for j in n]
        m, sh = 2 * m, sh + 1
    lx = [_dot(low[j], x[j], exact=True) for j in n]
    corr = [_dot(x[j], eye - x[j] - lx[j]) for j in n]
    return [x[j] + corr[j] for j in n]


def _gdn_local_batch(qkv, g_row, beta_row, ri, ci):
    n = range(len(qkv))
    eye, tril, strict = ri == ci, ri >= ci, ri > ci
    q = [qkv[j][:, :GDN_DK] for j in n]
    k = [qkv[j][:, GDN_DK:2 * GDN_DK] for j in n]
    v = [qkv[j][:, 2 * GDN_DK:] for j in n]
    g_col = [_to_col(g_row[j], eye) for j in n]
    beta_col = [_to_col(beta_row[j], eye) for j in n]
    gc_col = [jnp.sum(jnp.where(tril, g_row[j], 0.0), axis=1, keepdims=True) for j in n]
    gc_row = [jnp.sum(jnp.where(ri <= ci, g_col[j], 0.0), axis=0, keepdims=True) for j in n]
    g_last = [jnp.sum(g_row[j], axis=1, keepdims=True) for j in n]
    decay = [jnp.where(tril, jnp.exp(jnp.minimum(gc_col[j] - gc_row[j], 0.0)), 0.0) for j in n]
    e_col = [jnp.exp(gc_col[j]) for j in n]
    f_col = [jnp.exp(g_last[j] - gc_col[j]) for j in n]
    e_last = [jnp.exp(g_last[j]) for j in n]
    kb = [k[j] * beta_col[j] for j in n]
    vb = [v[j] * beta_col[j] for j in n]
    kk = [_dot(kb[j], k[j], "nt") for j in n]
    qk = [_dot(q[j], k[j], "nt") for j in n]
    low = [jnp.where(strict, kk[j] * decay[j], 0.0) for j in n]
    att = [qk[j] * decay[j] for j in n]
    return dict(q=q, k=k, v=v, beta_col=beta_col, decay=decay, e_col=e_col, f_col=f_col, e_last=e_last,
                kb=kb, vb=vb, low=low, att=att, eye=eye, strict=strict, tril=tril)


def _chunk_iotas():
    C = GDN_CHUNK
    return lax.broadcasted_iota(jnp.int32, (C, C), 0), lax.broadcasted_iota(jnp.int32, (C, C), 1)


def _gdn_chunk_fwd(qkv, ab, a_log, dt_bias, *, name):
    S = qkv.shape[0]
    H, C, DK = GDN_HEADS, GDN_CHUNK, GDN_DK
    RB = min(_GDN_ROWS, S)
    NCB, NB, NC = RB // C, S // RB, S // C
    heads = range(H)

    def body(qkv_ref, ab_ref, alog_ref, dtb_ref, o_ref, st_ref, t_ref, state, u_s, w_s, qe_s, kf_s, att_s):
        nb = pl.program_id(0)

        @pl.when(nb == 0)
        def _():
            state[...] = jnp.zeros_like(state)

        ri, ci = _chunk_iotas()
        neg_a = [-jnp.exp(alog_ref[h]) for h in heads]
        e_last = []
        for c in range(NCB):
            rows = pl.ds(c * C, C)
            g_row = [neg_a[h] * _softplus(ab_ref[h, c] + dtb_ref[h]) for h in heads]
            beta_row = [_sigmoid(ab_ref[H + h, c]) for h in heads]
            L = _gdn_local_batch([qkv_ref[rows, h * _HM:(h + 1) * _HM] for h in heads], g_row, beta_row, ri, ci)
            tinv = _unit_lower_inverse(L["low"], ri, ci)
            u = [_dot(tinv[h], L["vb"][h], exact=True) for h in heads]
            w = [_dot(tinv[h], L["kb"][h] * L["e_col"][h], exact=True) for h in heads]
            for h in heads:
                t_ref[h, c] = tinv[h]
                u_s[c, h] = u[h]
                w_s[c, h] = w[h].astype(BF16)
                qe_s[c, h] = (L["q"][h] * L["e_col"][h]).astype(BF16)
                kf_s[c, h] = (L["k"][h] * L["f_col"][h]).astype(BF16)
                att_s[c, h] = L["att"][h].astype(BF16)
            e_last.append(L["e_last"])
        st = [state[h] for h in heads]
        for c in range(NCB):
            rows = pl.ds(c * C, C)
            stb = [st[h].astype(BF16) for h in heads]
            vn = [u_s[c, h] - _dot(w_s[c, h], stb[h]) for h in heads]
            vnb = [vn[h].astype(BF16) for h in heads]
            out = [_dot(qe_s[c, h], stb[h]) + _dot(att_s[c, h], vnb[h]) for h in heads]
            new = [st[h] * e_last[c][h] + _dot(kf_s[c, h], vnb[h], "tn") for h in heads]
            for h in heads:
                o_ref[rows, h * DK:(h + 1) * DK] = out[h]
                st_ref[h, c] = st[h]
            st = new
        for h in heads:
            state[h] = st[h]

    return pl.pallas_call(
        body, name=name, grid=(NB,),
        in_specs=[pl.BlockSpec((RB, H * _HM), lambda n: (n, 0)),
                  pl.BlockSpec((2 * H, NCB, 1, C), lambda n: (0, n, 0, 0)),
                  pl.BlockSpec((H, 1, 1), lambda n: (0, 0, 0)),
                  pl.BlockSpec((H, 1, 1), lambda n: (0, 0, 0))],
        out_specs=[pl.BlockSpec((RB, H * DK), lambda n: (n, 0)),
                   pl.BlockSpec((H, NCB, DK, DK), lambda n: (0, n, 0, 0)),
                   pl.BlockSpec((H, NCB, C, C), lambda n: (0, n, 0, 0))],
        out_shape=[jax.ShapeDtypeStruct((S, H * DK), F32),
                   jax.ShapeDtypeStruct((H, NC, DK, DK), F32),
                   jax.ShapeDtypeStruct((H, NC, C, C), F32)],
        scratch_shapes=[pltpu.VMEM((H, DK, DK), F32), pltpu.VMEM((NCB, H, C, DK), F32),
                        pltpu.VMEM((NCB, H, C, DK), BF16), pltpu.VMEM((NCB, H, C, DK), BF16),
                        pltpu.VMEM((NCB, H, C, DK), BF16), pltpu.VMEM((NCB, H, C, C), BF16)],
        compiler_params=_params("arbitrary"),
    )(qkv, ab, a_log, dt_bias)


def _gdn_chunk_bwd(qkv, ab, a_log, dt_bias, states, tinvs, do, *, name):
    S = qkv.shape[0]
    H, C, DK = GDN_HEADS, GDN_CHUNK, GDN_DK
    RB = min(_GDN_ROWS, S)
    NCB, NB, NC = RB // C, S // RB, S // C
    heads = range(H)

    def body(qkv_ref, ab_ref, alog_ref, dtb_ref, st_ref, t_ref, do_ref,
             dqkv_ref, dab_ref, dalog_ref, ddtb_ref, dstate, w_s, vn_s, qe_s, kf_s, att_s, dvn_s, dkf_s):
        nb = pl.program_id(0)

        @pl.when(nb == 0)
        def _():
            dstate[...] = jnp.zeros_like(dstate)
            dalog_ref[...] = jnp.zeros_like(dalog_ref)
            ddtb_ref[...] = jnp.zeros_like(ddtb_ref)

        ri, ci = _chunk_iotas()
        neg_a = [-jnp.exp(alog_ref[h]) for h in heads]

        def local(c):
            rows = pl.ds(c * C, C)
            a_pre = [ab_ref[h, c] + dtb_ref[h] for h in heads]
            g_row = [neg_a[h] * _softplus(a_pre[h]) for h in heads]
            beta_row = [_sigmoid(ab_ref[H + h, c]) for h in heads]
            L = _gdn_local_batch([qkv_ref[rows, h * _HM:(h + 1) * _HM] for h in heads], g_row, beta_row, ri, ci)
            return L, a_pre, g_row, beta_row

        e_last = [None] * NCB
        for c in range(NCB):
            L, _, _, _ = local(c)
            kbe = [L["kb"][h] * L["e_col"][h] for h in heads]
            u = [_dot(t_ref[h, c], L["vb"][h], exact=True) for h in heads]
            w = [_dot(t_ref[h, c], kbe[h], exact=True) for h in heads]
            vn = [u[h] - _dot(w[h], st_ref[h, c]) for h in heads]
            for h in heads:
                w_s[c, h] = w[h].astype(BF16)
                vn_s[c, h] = vn[h].astype(BF16)
                qe_s[c, h] = (L["q"][h] * L["e_col"][h]).astype(BF16)
                kf_s[c, h] = (L["k"][h] * L["f_col"][h]).astype(BF16)
                att_s[c, h] = L["att"][h].astype(BF16)
            e_last[c] = L["e_last"]

        dst = [dstate[h] for h in heads]
        de_last = [None] * NCB
        for c in reversed(range(NCB)):
            rows = pl.ds(c * C, C)
            dob = [do_ref[rows, h * DK:(h + 1) * DK].astype(BF16) for h in heads]
            dstb = [dst[h].astype(BF16) for h in heads]
            dvn = [_dot(att_s[c, h], dob[h], "tn") + _dot(kf_s[c, h], dstb[h]) for h in heads]
            dkf = [_dot(vn_s[c, h], dstb[h], "nt") for h in heads]
            de_last[c] = [jnp.sum(jnp.sum(dst[h] * st_ref[h, c], axis=1, keepdims=True), axis=0, keepdims=True)
                          for h in heads]
            new = [dst[h] * e_last[c][h] + _dot(qe_s[c, h], dob[h], "tn")
                   - _dot(w_s[c, h], dvn[h].astype(BF16), "tn") for h in heads]
            for h in heads:
                dvn_s[c, h] = dvn[h]
                dkf_s[c, h] = dkf[h]
            dst = new
        for h in heads:
            dstate[h] = dst[h]

        for c in range(NCB):
            rows = pl.ds(c * C, C)
            L, a_pre, g_row, beta_row = local(c)
            q, k, v, kb, vb = L["q"], L["k"], L["v"], L["kb"], L["vb"]
            e_col, f_col, decay, beta_col = L["e_col"], L["f_col"], L["decay"], L["beta_col"]
            eye, strict, tril = L["eye"], L["strict"], L["tril"]
            tinv = [t_ref[h, c] for h in heads]
            stb = [st_ref[h, c].astype(BF16) for h in heads]
            dov = [do_ref[rows, h * DK:(h + 1) * DK] for h in heads]
            dvn = [dvn_s[c, h] for h in heads]
            dkf = [dkf_s[c, h] for h in heads]
            kbe = [kb[h] * e_col[h] for h in heads]
            datt = [jnp.where(tril, _dot(dov[h], vn_s[c, h], "nt"), 0.0) for h in heads]
            dqe = [_dot(dov[h], stb[h], "nt") for h in heads]
            dw = [-_dot(dvn[h], stb[h], "nt") for h in heads]
            dt = [_dot(dvn[h], vb[h], "nt") + _dot(dw[h], kbe[h], "nt") for h in heads]
            dvb = [_dot(tinv[h], dvn[h], "tn", exact=True) for h in heads]
            dkbe = [_dot(tinv[h], dw[h], "tn", exact=True) for h in heads]
            tdt = [_dot(tinv[h], dt[h], "tn", exact=True) for h in heads]
            dlow = [-jnp.where(strict, _dot(tdt[h], tinv[h], "nt", exact=True), 0.0) for h in heads]
            dkk = [dlow[h] * decay[h] for h in heads]
            dqk = [datt[h] * decay[h] for h in heads]
            dkb = [_dot(dkk[h], k[h]) + dkbe[h] * e_col[h] for h in heads]
            dk = [_dot(dkk[h], kb[h], "tn") + _dot(dqk[h], q[h], "tn") + dkf[h] * f_col[h] + dkb[h] * beta_col[h]
                  for h in heads]
            dq = [_dot(dqk[h], k[h]) + dqe[h] * e_col[h] for h in heads]
            for h in heads:
                dqkv_ref[rows, h * _HM:h * _HM + DK] = dq[h]
                dqkv_ref[rows, h * _HM + DK:h * _HM + 2 * DK] = dk[h]
                dqkv_ref[rows, h * _HM + 2 * DK:(h + 1) * _HM] = dvb[h] * beta_col[h]

            dbeta_col = [jnp.sum(k[h] * dkb[h] + v[h] * dvb[h], axis=1, keepdims=True) for h in heads]
            pmat = [dlow[h] * L["low"][h] + datt[h] * L["att"][h] for h in heads]
            df_col = [jnp.sum(k[h] * dkf[h], axis=1, keepdims=True) * f_col[h] for h in heads]
            dgc_col = [jnp.sum(pmat[h], axis=1, keepdims=True)
                       + jnp.sum(q[h] * dqe[h] + kb[h] * dkbe[h], axis=1, keepdims=True) * e_col[h] - df_col[h]
                       for h in heads]
            dgc_row = [_to_row(dgc_col[h], eye) - jnp.sum(pmat[h], axis=0, keepdims=True) for h in heads]
            dg_last = [jnp.sum(df_col[h], axis=0, keepdims=True) + de_last[c][h] * L["e_last"][h] for h in heads]
            dgc_c = [_to_col(dgc_row[h], eye) for h in heads]
            dg_row = [jnp.sum(jnp.where(ri >= ci, dgc_c[h], 0.0), axis=0, keepdims=True) + dg_last[h] for h in heads]
            dbeta_row = [_to_row(dbeta_col[h], eye) for h in heads]
            for h in heads:
                da_row = dg_row[h] * neg_a[h] * _sigmoid(a_pre[h])
                dab_ref[h, c] = da_row
                dab_ref[H + h, c] = dbeta_row[h] * beta_row[h] * (1.0 - beta_row[h])
                dalog_ref[h] += jnp.sum(dg_row[h] * g_row[h], axis=1, keepdims=True)
                ddtb_ref[h] += jnp.sum(da_row, axis=1, keepdims=True)

    rev = lambda n: NB - 1 - n
    vec = pl.BlockSpec((H, 1, 1), lambda n: (0, 0, 0))
    gates = pl.BlockSpec((2 * H, NCB, 1, C), lambda n: (0, rev(n), 0, 0))
    wide = pl.BlockSpec((RB, H * _HM), lambda n: (rev(n), 0))
    item = lambda dt: pltpu.VMEM((NCB, H, C, DK), dt)
    return pl.pallas_call(
        body, name=name, grid=(NB,),
        in_specs=[wide, gates, vec, vec,
                  pl.BlockSpec((H, NCB, DK, DK), lambda n: (0, rev(n), 0, 0)),
                  pl.BlockSpec((H, NCB, C, C), lambda n: (0, rev(n), 0, 0)),
                  pl.BlockSpec((RB, H * DK), lambda n: (rev(n), 0))],
        out_specs=[wide, gates, vec, vec],
        out_shape=[jax.ShapeDtypeStruct((S, H * _HM), F32),
                   jax.ShapeDtypeStruct((2 * H, NC, 1, C), F32),
                   jax.ShapeDtypeStruct((H, 1, 1), F32),
                   jax.ShapeDtypeStruct((H, 1, 1), F32)],
        scratch_shapes=[pltpu.VMEM((H, DK, DK), F32), item(BF16), item(BF16), item(BF16), item(BF16),
                        pltpu.VMEM((NCB, H, C, C), BF16), item(F32), item(F32)],
        compiler_params=_params("arbitrary"),
    )(qkv, ab, a_log, dt_bias, states, tinvs, do)


def _gdn_outnorm_fwd(o, z, gain, *, name):
    S, HV = o.shape
    RB = min(512, S)

    def body(o_ref, z_ref, g_ref, y_ref):
        ov = o_ref[...]
        r = lax.rsqrt(jnp.mean(ov * ov, axis=-1, keepdims=True) + RMS_EPS)
        y_ref[...] = (ov * r * g_ref[...] * _silu(z_ref[...].astype(F32))).astype(BF16)

    blk = pl.BlockSpec((RB, GDN_DK), lambda i, h: (i, h))
    return pl.pallas_call(
        body, name=name, grid=(S // RB, HV // GDN_DK),
        in_specs=[blk, blk, pl.BlockSpec((1, GDN_DK), lambda i, h: (0, 0))], out_specs=blk,
        out_shape=jax.ShapeDtypeStruct((S, HV), BF16), compiler_params=_params("parallel", "parallel"),
    )(o, z, gain)


def _gdn_outnorm_bwd(dy, o, z, gain, *, name):
    S, HV = o.shape
    RB = min(512, S)

    def body(dy_ref, o_ref, z_ref, g_ref, do_ref, dz_ref, dg_ref):
        first = (pl.program_id(0) == 0) & (pl.program_id(1) == 0)
        ov = o_ref[...]
        zv = z_ref[...].astype(F32)
        dyv = dy_ref[...].astype(F32)
        r = lax.rsqrt(jnp.mean(ov * ov, axis=-1, keepdims=True) + RMS_EPS)
        n = ov * r
        sg = _sigmoid(zv)
        dng = dyv * (zv * sg)
        dn = dng * g_ref[...]
        do_ref[...] = r * (dn - n * jnp.mean(dn * n, axis=-1, keepdims=True))
        dz_ref[...] = (dyv * (n * g_ref[...]) * (sg * (1.0 + zv * (1.0 - sg)))).astype(BF16)
        part = jnp.sum(dng * n, axis=0, keepdims=True)

        @pl.when(first)
        def _():
            dg_ref[...] = part

        @pl.when(jnp.logical_not(first))
        def _():
            dg_ref[...] += part

    blk = pl.BlockSpec((RB, GDN_DK), lambda i, h: (i, h))
    vec = pl.BlockSpec((1, GDN_DK), lambda i, h: (0, 0))
    return pl.pallas_call(
        body, name=name, grid=(S // RB, HV // GDN_DK),
        in_specs=[blk, blk, blk, vec], out_specs=[blk, blk, vec],
        out_shape=[jax.ShapeDtypeStruct((S, HV), F32), jax.ShapeDtypeStruct((S, HV), BF16),
                   jax.ShapeDtypeStruct((1, GDN_DK), F32)],
        compiler_params=_params("arbitrary", "arbitrary"),
    )(dy, o, z, gain)


def _rms64(x, gain):
    r = lax.rsqrt(jnp.mean(x * x, axis=-1, keepdims=True) + RMS_EPS)
    xh = x * r
    return xh, r, xh * gain


def _rms64_bwd(dy, xh, r, gain):
    dxh = dy * gain
    return r * (dxh - xh * jnp.mean(dxh * xh, axis=-1, keepdims=True))


_HP = LANES // DSW_DH
_DSW_W = DSW_HEADS * DSW_DH
_DSW_ROWS = 1024
_DSW_BATCH = 8


def _dsw_geometry(S, g):
    d = DSW_GROUPS[g][1]
    slab = DSW_BLK * d
    tb = max(1, min(_DSW_ROWS, S) // slab)
    return d, slab, tb, S // (tb * slab)


def _block_rows(t, r, slab, d):
    return pl.ds(t * slab + r, DSW_BLK) if d == 1 else pl.ds(t * slab + r, DSW_BLK, stride=d)


def _head(x, h):
    return x[:, h * DSW_DH:(h + 1) * DSW_DH]


def _dsw_attn_fwd(q, k, v, bias, q_gain, k_gain, prev_out, *, g, name):
    S, WT = q.shape
    B = DSW_BLK
    d, slab, tb, n_tiles = _dsw_geometry(S, g)
    rt = tb * slab
    cb = g * (_DSW_W // LANES)
    batch_res = max(1, _DSW_BATCH // tb)

    def body(q_ref, kp_ref, kc_ref, vp_ref, vc_ref, bias_ref, qg_ref, kg_ref, *rest):
        o_ref, lse_ref = rest[-2:]
        i = pl.program_id(1)
        qg, kg = qg_ref[...] * DSW_DH ** -0.5, kg_ref[...]
        col = lax.broadcasted_iota(jnp.int32, (B, 2 * B), 1)
        for r0 in range(0, d, batch_res):
            res = range(r0, min(d, r0 + batch_res))
            heads = range(_HP)
            k_raw = {(r, -1): kp_ref[_block_rows(0, r, slab, d), :] for r in res}
            v_raw = {(r, -1): vp_ref[_block_rows(0, r, slab, d), :] for r in res}
            q_raw = {}
            for r in res:
                for t in range(tb):
                    rows = _block_rows(t, r, slab, d)
                    q_raw[r, t], k_raw[r, t], v_raw[r, t] = q_ref[rows, :], kc_ref[rows, :], vc_ref[rows, :]
            kn = {key: [_rms64(_head(x, h), kg)[2].astype(BF16) for h in heads] for key, x in k_raw.items()}
            vb = {key: [_head(x, h).astype(BF16) for h in heads] for key, x in v_raw.items()}
            qn = {key: [_rms64(_head(x, h), qg)[2] for h in heads] for key, x in q_raw.items()}
            items = [(r, t, h) for r in res for t in range(tb) for h in heads]
            s = {}
            for r, t, h in items:
                sv = _dot(qn[r, t][h], jnp.concatenate([kn[r, t - 1][h], kn[r, t][h]], axis=0), "nt") + bias_ref[h]
                s[r, t, h] = jnp.where((i == 0) & (col < B), NEG_BIG, sv) if t == 0 else sv
            m = {it: jnp.max(s[it], axis=-1, keepdims=True) for it in items}
            p = {it: jnp.exp(s[it] - m[it]) for it in items}
            l = {it: jnp.sum(p[it], axis=-1, keepdims=True) for it in items}
            o = {(r, t, h): _dot(p[r, t, h], jnp.concatenate([vb[r, t - 1][h], vb[r, t][h]], axis=0))
                 for r, t, h in items}
            for r in res:
                for t in range(tb):
                    rows = _block_rows(t, r, slab, d)
                    o_ref[rows, :] = jnp.concatenate([o[r, t, h] / l[r, t, h] for h in heads], axis=1)
                    lse_ref[rows, :] = jnp.concatenate(
                        [jnp.broadcast_to(m[r, t, h] + jnp.log(l[r, t, h]), (B, DSW_DH)) for h in heads], axis=1)

    cur = pl.BlockSpec((rt, LANES), lambda hp, i: (i, cb + hp))
    prev = pl.BlockSpec((slab, LANES), lambda hp, i: (jnp.maximum(i * tb - 1, 0), cb + hp))
    vec = pl.BlockSpec((1, DSW_DH), lambda hp, i: (0, 0))
    shp = jax.ShapeDtypeStruct((S, WT), F32)
    carried = [] if prev_out is None else list(prev_out)
    n_in = 8
    return pl.pallas_call(
        body, name=name, grid=(_DSW_W // LANES, n_tiles),
        in_specs=[cur, prev, cur, prev, cur, pl.BlockSpec((_HP, B, 2 * B), lambda hp, i: (hp, 0, 0)), vec, vec]
                 + [pl.BlockSpec(memory_space=pl.ANY)] * len(carried),
        out_specs=[cur, cur], out_shape=[shp, shp],
        input_output_aliases={n_in + j: j for j in range(len(carried))},
        compiler_params=_params("parallel", "parallel"),
    )(q, k, k, v, v, bias, q_gain, k_gain, *carried)


def _dsw_merge(o_g, lse_g, *, name):
    S = o_g.shape[0]
    W, G = _DSW_W, len(DSW_GROUPS)
    tr = min(512, S)

    def body(o_ref, l_ref, out_ref, lse_ref):
        ls = [l_ref[:, g * W:(g + 1) * W] for g in range(G)]
        m = ls[0]
        for g in range(1, G):
            m = jnp.maximum(m, ls[g])
        den = jnp.zeros_like(m)
        acc = jnp.zeros_like(m)
        for g in range(G):
            wg = jnp.exp(ls[g] - m)
            den = den + wg
            acc = acc + wg * o_ref[:, g * W:(g + 1) * W]
        out_ref[...] = acc / den
        lse_ref[...] = m + jnp.log(den)

    wide = pl.BlockSpec((tr, G * W), lambda i: (i, 0))
    blk = pl.BlockSpec((tr, W), lambda i: (i, 0))
    shp = jax.ShapeDtypeStruct((S, W), F32)
    return pl.pallas_call(
        body, name=name, grid=(S // tr,), in_specs=[wide, wide], out_specs=[blk, blk],
        out_shape=[shp, shp], compiler_params=_params("parallel"),
    )(o_g, lse_g)


def _dsw_attn_bwd(q, k, v, o, lse, do, bias, q_gain, k_gain, prev_out, *, g, name):
    S, WT = q.shape
    B = DSW_BLK
    d, slab, tb, n_tiles = _dsw_geometry(S, g)
    rt = tb * slab
    cb = g * (_DSW_W // LANES)
    n_slabs = S // slab
    scale = DSW_DH ** -0.5
    batch_res = max(1, _DSW_BATCH // tb)

    def body(q_ref, qx_ref, kp_ref, kc_ref, vp_ref, vc_ref, o_ref, ox_ref, l_ref, lx_ref, do_ref, dox_ref,
             bias_ref, qg_ref, kg_ref, *rest):
        dq_ref, dk_ref, dv_ref, db_ref, dqg_ref, dkg_ref = rest[-6:]
        hp, i = pl.program_id(0), pl.program_id(1)
        qg, kg = qg_ref[...] * scale, kg_ref[...]
        col = lax.broadcasted_iota(jnp.int32, (B, 2 * B), 1)
        has_next = i < n_tiles - 1

        @pl.when(i == 0)
        def _():
            db_ref[...] = jnp.zeros_like(db_ref)

        dqg_acc = jnp.zeros((1, DSW_DH), F32)
        dkg_acc = jnp.zeros((1, DSW_DH), F32)
        heads = range(_HP)
        for r0 in range(0, d, batch_res):
            res = range(r0, min(d, r0 + batch_res))
            q_raw, k_raw, v_raw, o_raw, l_raw, do_raw = {}, {}, {}, {}, {}, {}
            for r in res:
                first_rows = _block_rows(0, r, slab, d)
                k_raw[r, -1], v_raw[r, -1] = kp_ref[first_rows, :], vp_ref[first_rows, :]
                for t in range(tb):
                    rows = _block_rows(t, r, slab, d)
                    q_raw[r, t], o_raw[r, t], l_raw[r, t], do_raw[r, t] = (
                        q_ref[rows, :], o_ref[rows, :], l_ref[rows, :], do_ref[rows, :])
                    k_raw[r, t], v_raw[r, t] = kc_ref[rows, :], vc_ref[rows, :]
                q_raw[r, tb], o_raw[r, tb], l_raw[r, tb], do_raw[r, tb] = (
                    qx_ref[first_rows, :], ox_ref[first_rows, :], lx_ref[first_rows, :], dox_ref[first_rows, :])
            kk = {key: [_rms64(_head(x, h), kg) for h in heads] for key, x in k_raw.items()}
            qq = {key: [_rms64(_head(x, h), qg) for h in heads] for key, x in q_raw.items()}
            knb = {key: [kk[key][h][2].astype(BF16) for h in heads] for key in kk}
            qnb = {key: [qq[key][h][2].astype(BF16) for h in heads] for key in qq}
            vb = {key: [_head(x, h).astype(BF16) for h in heads] for key, x in v_raw.items()}
            dob = {key: [_head(x, h).astype(BF16) for h in heads] for key, x in do_raw.items()}
            delta = {key: [jnp.sum(_head(do_raw[key], h) * _head(o_raw[key], h), axis=-1, keepdims=True)
                           for h in heads] for key in q_raw}
            full = [(r, t, h) for r in res for t in range(tb) for h in heads]
            half = [(r, tb, h) for r in res for h in heads]
            s = {}
            for r, t, h in full:
                sv = _dot(qnb[r, t][h], jnp.concatenate([knb[r, t - 1][h], knb[r, t][h]], axis=0), "nt") + bias_ref[h]
                s[r, t, h] = jnp.where((i == 0) & (col < B), NEG_BIG, sv) if t == 0 else sv
            for r, t, h in half:
                s[r, t, h] = _dot(qnb[r, t][h], knb[r, t - 1][h], "nt") + bias_ref[h, :, 0:B]
            lse_of = lambda r, t, h: l_raw[r, t][:, h * DSW_DH:h * DSW_DH + 1]
            p = {(r, t, h): jnp.exp(s[r, t, h] - lse_of(r, t, h)) for r, t, h in full}
            for r, t, h in half:
                p[r, t, h] = jnp.where(has_next, jnp.exp(s[r, t, h] - lse_of(r, t, h)), 0.0)
            dp = {(r, t, h): _dot(dob[r, t][h], jnp.concatenate([vb[r, t - 1][h], vb[r, t][h]], axis=0), "nt")
                  for r, t, h in full}
            for r, t, h in half:
                dp[r, t, h] = _dot(dob[r, t][h], vb[r, t - 1][h], "nt")
            ds = {(r, t, h): p[r, t, h] * (dp[r, t, h] - delta[r, t][h]) for r, t, h in full + half}
            pb = {it: p[it].astype(BF16) for it in ds}
            dsb = {it: ds[it].astype(BF16) for it in ds}
            for h in heads:
                tot = None
                for r in res:
                    for t in range(tb):
                        tot = ds[r, t, h] if tot is None else tot + ds[r, t, h]
                db_ref[h] += tot
            dqn = {(r, t, h): _dot(dsb[r, t, h], jnp.concatenate([knb[r, t - 1][h], knb[r, t][h]], axis=0))
                   for r, t, h in full}
            prev_half = lambda x, r, t, h: x[r, t, h][:, :B] if t < tb else x[r, t, h]
            dkn = {(r, t, h): _dot(dsb[r, t, h][:, B:], qnb[r, t][h], "tn")
                   + _dot(prev_half(dsb, r, t + 1, h), qnb[r, t + 1][h], "tn") for r, t, h in full}
            dvv = {(r, t, h): _dot(pb[r, t, h][:, B:], dob[r, t][h], "tn")
                   + _dot(prev_half(pb, r, t + 1, h), dob[r, t + 1][h], "tn") for r, t, h in full}
            for r, t, h in full:
                dqg_acc = dqg_acc + jnp.sum(dqn[r, t, h] * qq[r, t][h][0], axis=0, keepdims=True)
                dkg_acc = dkg_acc + jnp.sum(dkn[r, t, h] * kk[r, t][h][0], axis=0, keepdims=True)
            for r in res:
                for t in range(tb):
                    rows = _block_rows(t, r, slab, d)
                    dq_ref[rows, :] = jnp.concatenate(
                        [_rms64_bwd(dqn[r, t, h], qq[r, t][h][0], qq[r, t][h][1], qg) for h in heads], axis=1)
                    dk_ref[rows, :] = jnp.concatenate(
                        [_rms64_bwd(dkn[r, t, h], kk[r, t][h][0], kk[r, t][h][1], kg) for h in heads], axis=1)
                    dv_ref[rows, :] = jnp.concatenate([dvv[r, t, h] for h in heads], axis=1)

        start = (hp == 0) & (i == 0)

        @pl.when(start)
        def _():
            dqg_ref[...] = dqg_acc * scale
            dkg_ref[...] = dkg_acc

        @pl.when(jnp.logical_not(start))
        def _():
            dqg_ref[...] += dqg_acc * scale
            dkg_ref[...] += dkg_acc

    def spec(rows, pick, base):
        return pl.BlockSpec((rows, LANES), lambda hp, i: (pick(i), base + hp))

    same = lambda i: i
    before = lambda i: jnp.maximum(i * tb - 1, 0)
    after = lambda i: jnp.minimum((i + 1) * tb, n_slabs - 1)
    cur, cur1 = spec(rt, same, cb), spec(rt, same, 0)
    vec = pl.BlockSpec((1, DSW_DH), lambda hp, i: (0, 0))
    bspec = pl.BlockSpec((_HP, B, 2 * B), lambda hp, i: (hp, 0, 0))
    shp = jax.ShapeDtypeStruct((S, WT), F32)
    vshp = jax.ShapeDtypeStruct((1, DSW_DH), F32)
    carried = [] if prev_out is None else list(prev_out)
    n_in = 15
    return pl.pallas_call(
        body, name=name, grid=(_DSW_W // LANES, n_tiles),
        in_specs=[cur, spec(slab, after, cb), spec(slab, before, cb), cur, spec(slab, before, cb), cur,
                  cur1, spec(slab, after, 0), cur1, spec(slab, after, 0), cur1, spec(slab, after, 0),
                  bspec, vec, vec] + [pl.BlockSpec(memory_space=pl.ANY)] * len(carried),
        out_specs=[cur, cur, cur, bspec, vec, vec],
        out_shape=[shp, shp, shp, jax.ShapeDtypeStruct(bias.shape, F32), vshp, vshp],
        input_output_aliases={n_in + j: j for j in range(len(carried))},
        compiler_params=_params("arbitrary", "arbitrary"),
    )(q, q, k, k, v, v, o, o, lse, lse, do, do, bias, q_gain, k_gain, *carried)


def _t5_bucket(dist):
    max_exact = REL_BUCKETS // 2
    scaled = jnp.log(jnp.maximum(dist, 1).astype(F32) / max_exact) / math.log(REL_MAX_DIST / max_exact)
    large = jnp.minimum(max_exact + (scaled * (REL_BUCKETS - max_exact)).astype(jnp.int32), REL_BUCKETS - 1)
    return jnp.where(dist < max_exact, dist, large)


def _dsw_band():
    dist = (jnp.arange(DSW_BLK)[:, None] + DSW_BLK) - jnp.arange(2 * DSW_BLK)[None, :]
    return dist, (dist >= 0) & (dist <= DSW_BLK)


def _dsw_bias(rel_bias):
    dist, band = _dsw_band()
    out = []
    for g, (_, d) in enumerate(DSW_GROUPS):
        hot = jax.nn.one_hot(_t5_bucket(jnp.maximum(dist, 0) * d), REL_BUCKETS, dtype=F32)
        tab = jnp.einsum("qkb,bh->hqk", hot, rel_bias[:, g * DSW_HEADS:(g + 1) * DSW_HEADS],
                         precision=lax.Precision.HIGHEST)
        out.append(jnp.where(band[None], tab, NEG_BIG))
    return jnp.stack(out)


def _dsw_bucket_onehot():
    dist, band = _dsw_band()
    out = []
    for _, d in DSW_GROUPS:
        hot = jax.nn.one_hot(_t5_bucket(jnp.maximum(dist, 0) * d), LANES, dtype=BF16)
        out.append(jnp.where(band[..., None], hot, 0).reshape(-1, LANES))
    return jnp.stack(out)


def _exchange(send, *, gather, name):
    R, C = send.shape[-2:]

    def body(src_ref, dst_ref, send_sems, recv_sems, local_sem):
        x, y, c = lax.axis_index("x"), lax.axis_index("y"), lax.axis_index("c")
        me = 4 * x + 2 * y + c
        mine = pltpu.make_async_copy(src_ref if gather else src_ref.at[me], dst_ref.at[me], local_sem)
        mine.start()
        copies = []
        for rel in range(1, N_DEV):
            px = 1 - x if rel & 4 else x
            py = 1 - y if rel & 2 else y
            pc = 1 - c if rel & 1 else c
            peer = 4 * px + 2 * py + pc
            cp = pltpu.make_async_remote_copy(
                src_ref=src_ref if gather else src_ref.at[peer], dst_ref=dst_ref.at[me],
                send_sem=send_sems.at[rel - 1], recv_sem=recv_sems.at[rel - 1],
                device_id=(px, py, pc), device_id_type=pl.DeviceIdType.MESH)
            cp.start()
            copies.append(cp)
        for cp in copies:
            cp.wait()
        mine.wait()

    return pl.pallas_call(
        body, name=name,
        in_specs=[pl.BlockSpec(memory_space=pl.ANY)], out_specs=pl.BlockSpec(memory_space=pl.ANY),
        out_shape=jax.ShapeDtypeStruct((N_DEV, R, C), send.dtype),
        scratch_shapes=[pltpu.SemaphoreType.DMA((N_DEV - 1,)), pltpu.SemaphoreType.DMA((N_DEV - 1,)),
                        pltpu.SemaphoreType.DMA(())],
    )(send)


def _gather_two_level(send, *, name):
    R, C = send.shape

    def body(src_ref, dst_ref, send_sems, recv_sems, local_sem):
        x, y, c = lax.axis_index("x"), lax.axis_index("y"), lax.axis_index("c")
        me, sibling = (x, y, c), (x, y, 1 - c)
        chips = [(1 - x, y), (x, 1 - y), (1 - x, 1 - y)]

        def slot(px, py, pc):
            return dst_ref.at[4 * px + 2 * py + pc]

        def copy(k, block, to, src=None):
            return pltpu.make_async_remote_copy(
                src_ref=slot(*block) if src is None else src, dst_ref=slot(*block),
                send_sem=send_sems.at[k], recv_sem=recv_sems.at[k],
                device_id=to, device_id_type=pl.DeviceIdType.MESH)

        mine = pltpu.make_async_copy(src_ref, slot(*me), local_sem)
        mine.start()
        first = [copy(0, me, sibling, src=src_ref)]
        first += [copy(1 + j, me, (*chip, c), src=src_ref) for j, chip in enumerate(chips)]
        for cp in first:
            cp.start()
        passed = [copy(4 + j, (*chip, c), sibling) for j, chip in enumerate(chips)]
        for j, chip in enumerate(chips):
            copy(1 + j, (*chip, c), me).wait_recv()
            passed[j].start()
        copy(0, sibling, me).wait_recv()
        for j, chip in enumerate(chips):
            copy(4 + j, (*chip, 1 - c), me).wait_recv()
        for cp in first + passed:
            cp.wait_send()
        mine.wait()

    return pl.pallas_call(
        body, name=name,
        in_specs=[pl.BlockSpec(memory_space=pl.ANY)], out_specs=pl.BlockSpec(memory_space=pl.ANY),
        out_shape=jax.ShapeDtypeStruct((N_DEV, R, C), send.dtype),
        scratch_shapes=[pltpu.SemaphoreType.DMA((N_DEV - 1,)), pltpu.SemaphoreType.DMA((N_DEV - 1,)),
                        pltpu.SemaphoreType.DMA(())],
    )(send)


_BIG = ("w_ffn_in", "w_ffn_out", "gdn_w_in", "gdn_conv", "gdn_w_out", "dsw_w_in", "dsw_w_out")
_SHARD_AXIS = {"w_ffn_in": 2, "w_ffn_out": 1, "gdn_w_in": 2, "gdn_conv": 2, "gdn_w_out": 1, "dsw_w_in": 2,
               "dsw_w_out": 2}
_SMALL = ("b_ada", "norm_mix", "norm_ffn", "gdn_a_log", "gdn_dt_bias", "gdn_out_norm", "dsw_q_norm",
          "dsw_k_norm", "rel_bias")
_ROW_ALIGN = 16
_BIG_ALIGN = 1024


def _ceil_to(n, m):
    return -(-n // m) * m


def _seg_rows(shape):
    return _ceil_to(_ceil_to(int(np.prod(shape)), LANES) // LANES, _ROW_ALIGN)


def _pack(arrs, total_align):
    lead = arrs[0][1]
    segs = []
    for a, nlead in arrs:
        assert nlead == lead
        bshape = a.shape[:nlead]
        n = int(np.prod(a.shape[nlead:]))
        rows = _seg_rows(a.shape[nlead:])
        flat = a.reshape(bshape + (n,))
        flat = jnp.pad(flat, [(0, 0)] * nlead + [(0, rows * LANES - n)])
        segs.append(flat.reshape(bshape + (rows, LANES)))
    buf = jnp.concatenate(segs, axis=lead)
    total = _ceil_to(buf.shape[lead], total_align)
    return jnp.pad(buf, [(0, 0)] * lead + [(0, total - buf.shape[lead]), (0, 0)])


def _unpack(buf, shapes, nlead):
    out, off = [], 0
    for shp in shapes:
        n, rows = int(np.prod(shp)), _seg_rows(shp)
        seg = lax.slice_in_dim(buf, off, off + rows, axis=nlead)
        seg = seg.reshape(buf.shape[:nlead] + (rows * LANES,))[..., :n]
        out.append(seg.reshape(buf.shape[:nlead] + tuple(shp)))
        off += rows
    return out


def _to_natural(g, axis):
    n, L, r, c = g.shape
    if axis == 2:
        return jnp.transpose(g, (1, 2, 0, 3)).reshape(L, r, n * c)
    return jnp.transpose(g, (1, 0, 2, 3)).reshape(L, n * r, c)


def _to_blocked(w, axis):
    L, R, C = w.shape
    if axis == 2:
        return jnp.transpose(w.reshape(L, R, N_DEV, C // N_DEV), (2, 0, 1, 3))
    return jnp.transpose(w.reshape(L, N_DEV, R // N_DEV, C), (1, 0, 2, 3))


def _hm(a):
    lead = a.shape[:-1]
    return jnp.swapaxes(a.reshape(lead + (3, GDN_HEADS, GDN_DK)), -3, -2).reshape(lead + (3 * GDN_HEADS * GDN_DK,))


def _un_hm(a):
    lead = a.shape[:-1]
    return jnp.swapaxes(a.reshape(lead + (GDN_HEADS, 3, GDN_DK)), -3, -2).reshape(lead + (3 * GDN_HEADS * GDN_DK,))


_TILES = (1536, 1408, 1024, 768, 704, 512, 384, 256, 128, 64, 32, 16, 8)


def _tile(n, cap):
    for t in _TILES:
        if t <= cap and n % t == 0:
            return t
    return n


def _mm_auto(a, b, mode, name, **kw):
    if mode == "tn":
        (K, M), N = a.shape, b.shape[1]
        tm, tn, tk = _tile(M, 1408), _tile(N, 512), _tile(K, 512)
    else:
        M, K = a.shape
        N = b.shape[1] if mode == "nn" else b.shape[0]
        tm, tn, tk = _tile(M, 512), _tile(N, 1536), _tile(K, 1408)
    return _mm(a, b, mode=mode, name=name, tm=tm, tn=tn, tk=tk, **kw)


def _row(v):
    return v.reshape(1, -1)


def _ffn_fwd(x, mod, gain, w_in, w_out, tag):
    sh, sc, gate = mod
    h = _norm_mod_fwd(x, gain, sc, sh, name=f"ffn_norm_{tag}")
    p = _mm_auto(h, w_in, "nn", f"ffn_in_{tag}", out_dtype=BF16)
    a = _swiglu_fwd(p, name=f"ffn_act_{tag}")
    y = _mm_auto(a, w_out, "nn", f"ffn_out_{tag}", out_scale=gate, resid=x)
    return y, (x, h, p, a)


def _ffn_bwd(dy, saved, mod, gain, w_in, w_out, tag):
    sh, sc, gate = mod
    x, h, p, a = saved
    gmat = _mm_auto(a, dy, "tn", f"ffn_out_g_{tag}")
    dw_out, dgate = _wout_grad(gmat, w_out, gate, name=f"ffn_out_dw_{tag}")
    da = _mm_auto(dy, w_out, "nt", f"ffn_out_dx_{tag}", a_scale=gate, out_dtype=BF16)
    dp = _swiglu_bwd(da, p, name=f"ffn_act_bwd_{tag}")
    dw_in = _mm_auto(h, dp, "tn", f"ffn_in_dw_{tag}")
    dh = _mm_auto(dp, w_in, "nt", f"ffn_in_dx_{tag}")
    dx, dsh, dsc, dgain = _norm_mod_bwd(dh, x, dy, gain, sc, name=f"ffn_norm_bwd_{tag}")
    return dx, dict(w_in=dw_in, w_out=dw_out, gain=dgain, mod=(dsh, dsc, dgate))


def _gdn_fwd(x, mod, gain, W):
    sh, sc, gate = mod
    S = x.shape[0]
    h = _norm_mod_fwd(x, gain, sc, sh, name="gdn_norm")
    pq = _mm_auto(h, W["gdn_qkv"], "nn", "gdn_in_qkv")
    z = _mm_auto(h, W["gdn_z"], "nn", "gdn_in_z")
    ab = _mm_auto(h, W["gdn_ab"], "nn", "gdn_in_ab")
    qkvn = _gdn_prep_fwd(pq, W["gdn_conv"], name="gdn_prep")
    ab4 = jnp.transpose(ab[:, :2 * GDN_HEADS]).reshape(2 * GDN_HEADS, S // GDN_CHUNK, 1, GDN_CHUNK)
    o, states, tinvs = _gdn_chunk_fwd(qkvn, ab4, W["gdn_a_log"], W["gdn_dt_bias"], name="gdn_chunk")
    o2 = _gdn_outnorm_fwd(o, z, W["gdn_out_norm"], name="gdn_outnorm")
    y = _mm_auto(o2, W["gdn_out"], "nn", "gdn_out", out_scale=gate, resid=x)
    return y, (x, h, pq, z, qkvn, ab4, o, states, tinvs, o2)


def _gdn_bwd(dy, saved, mod, gain, W):
    sh, sc, gate = mod
    x, h, pq, z, qkvn, ab4, o, states, tinvs, o2 = saved
    S = x.shape[0]
    gmat = _mm_auto(o2, dy, "tn", "gdn_out_g")
    dw_out, dgate = _wout_grad(gmat, W["gdn_out"], gate, name="gdn_out_dw")
    do2 = _mm_auto(dy, W["gdn_out"], "nt", "gdn_out_dx", a_scale=gate)
    do, dz, dout_norm = _gdn_outnorm_bwd(do2, o, z, W["gdn_out_norm"], name="gdn_outnorm_bwd")
    dqkvn, dab4, da_log, ddt_bias = _gdn_chunk_bwd(
        qkvn, ab4, W["gdn_a_log"], W["gdn_dt_bias"], states, tinvs, do, name="gdn_chunk_bwd")
    dc, dconv8 = _gdn_prep_bwd_pre(dqkvn, pq, W["gdn_conv"], name="gdn_prep_bwd")
    dpq = _gdn_conv_bwd_x(dc, W["gdn_conv"], name="gdn_conv_bwd")
    dab = jnp.transpose(dab4.reshape(2 * GDN_HEADS, S))
    dab = jnp.pad(dab, ((0, 0), (0, LANES - 2 * GDN_HEADS))).astype(BF16)
    dw_qkv = _mm_auto(h, dpq, "tn", "gdn_in_qkv_dw")
    dw_z = _mm_auto(h, dz, "tn", "gdn_in_z_dw")
    dw_ab = _mm_auto(h, dab, "tn", "gdn_in_ab_dw")
    dh = _mm_auto(dpq, W["gdn_qkv"], "nt", "gdn_in_qkv_dx")
    dh = _mm_auto(dz, W["gdn_z"], "nt", "gdn_in_z_dx", resid=dh)
    dh = _mm_auto(dab, W["gdn_ab"], "nt", "gdn_in_ab_dx", resid=dh)
    dx, dsh, dsc, dgain = _norm_mod_bwd(dh, x, dy, gain, sc, name="gdn_norm_bwd")
    dw_in = jnp.concatenate([_un_hm(dw_qkv), dw_z, dw_ab[:, :2 * GDN_HEADS]], axis=1)
    return dx, dict(gdn_w_in=dw_in, gdn_conv=_un_hm(dconv8[:GDN_CONV]), gdn_w_out=dw_out, gdn_out_norm=dout_norm,
                    gdn_a_log=da_log.reshape(1, GDN_HEADS), gdn_dt_bias=ddt_bias.reshape(1, GDN_HEADS),
                    gain=dgain, mod=(dsh, dsc, dgate))


def _dsw_fwd(x, mod, gain, W):
    sh, sc, gate = mod
    h = _norm_mod_fwd(x, gain, sc, sh, name="dsw_norm")
    q, k, v = (_mm_auto(h, W[n], "nn", f"dsw_in_{n[-1]}") for n in ("dsw_q", "dsw_k", "dsw_v"))
    outs = None
    for g in range(len(DSW_GROUPS)):
        outs = _dsw_attn_fwd(q, k, v, W["dsw_bias"][g], W["dsw_q_norm"], W["dsw_k_norm"], outs, g=g,
                             name=f"dsw_attn_{g}")
    o, lse = _dsw_merge(*outs, name="dsw_merge")
    y = _mm_auto(o, W["dsw_out"], "nn", "dsw_out", out_scale=gate, resid=x)
    return y, (x, h, q, k, v, o, lse)


def _dsw_bwd(dy, saved, mod, gain, W):
    sh, sc, gate = mod
    x, h, q, k, v, o, lse = saved
    gmat = _mm_auto(o, dy, "tn", "dsw_out_g")
    dw_out, dgate = _wout_grad(gmat, W["dsw_out"], gate, name="dsw_out_dw")
    do = _mm_auto(dy, W["dsw_out"], "nt", "dsw_out_dx", a_scale=gate)
    G = len(DSW_GROUPS)
    dqkv, dbias, dq_norm, dk_norm = None, [], 0.0, 0.0
    for g in range(G):
        *dqkv, db, dqg, dkg = _dsw_attn_bwd(q, k, v, o, lse, do, W["dsw_bias"][g], W["dsw_q_norm"],
                                            W["dsw_k_norm"], dqkv, g=g, name=f"dsw_attn_bwd_{g}")
        dbias.append(db)
        dq_norm, dk_norm = dq_norm + dqg, dk_norm + dkg
    dws, dh = [], None
    for n, d in zip(("dsw_q", "dsw_k", "dsw_v"), dqkv):
        dws.append(_mm_auto(h, d, "tn", f"dsw_in_{n[-1]}_dw"))
        dh = _mm_auto(d, W[n], "nt", f"dsw_in_{n[-1]}_dx", **({} if dh is None else {"resid": dh}))
    dx, dsh, dsc, dgain = _norm_mod_bwd(dh, x, dy, gain, sc, name="dsw_norm_bwd")
    hot = _dsw_bucket_onehot()
    drel = [_mm_auto(dbias[g].reshape(DSW_HEADS, -1), hot[g], "nn", f"dsw_rel_bias_{g}")[:, :REL_BUCKETS]
            for g in range(G)]
    return dx, dict(dsw_w_in=jnp.concatenate(dws, axis=1), dsw_w_out=dw_out, dsw_q_norm=dq_norm,
                    dsw_k_norm=dk_norm, rel_bias=jnp.transpose(jnp.concatenate(drel, axis=0)),
                    gain=dgain, mod=(dsh, dsc, dgate))


def _local_step(x, target, mod, W):
    mods = [[_row(mod[l, i]) for i in range(6)] for l in range(2)]
    nmix = [_row(W["norm_mix"][l]) for l in range(2)]
    nffn = [_row(W["norm_ffn"][l]) for l in range(2)]
    x1, s_gdn = _gdn_fwd(x, mods[0][:3], nmix[0], W)
    x2, s_f0 = _ffn_fwd(x1, mods[0][3:], nffn[0], W["w_ffn_in"][0], W["w_ffn_out"][0], "0")
    x3, s_dsw = _dsw_fwd(x2, mods[1][:3], nmix[1], W)
    x4, s_f1 = _ffn_fwd(x3, mods[1][3:], nffn[1], W["w_ffn_in"][1], W["w_ffn_out"][1], "1")
    dx4, sse = _loss_head(x4, target, name="loss_head")
    dx3, g_f1 = _ffn_bwd(dx4, s_f1, mods[1][3:], nffn[1], W["w_ffn_in"][1], W["w_ffn_out"][1], "1")
    dx2, g_dsw = _dsw_bwd(dx3, s_dsw, mods[1][:3], nmix[1], W)
    dx1, g_f0 = _ffn_bwd(dx2, s_f0, mods[0][3:], nffn[0], W["w_ffn_in"][0], W["w_ffn_out"][0], "0")
    dx0, g_gdn = _gdn_bwd(dx1, s_gdn, mods[0][:3], nmix[0], W)
    dmod = jnp.stack([jnp.concatenate(list(g_gdn["mod"]) + list(g_f0["mod"]), axis=0),
                      jnp.concatenate(list(g_dsw["mod"]) + list(g_f1["mod"]), axis=0)])
    grads = dict(
        w_ffn_in=jnp.stack([g_f0["w_in"], g_f1["w_in"]]), w_ffn_out=jnp.stack([g_f0["w_out"], g_f1["w_out"]]),
        norm_mix=jnp.concatenate([g_gdn["gain"], g_dsw["gain"]], axis=0),
        norm_ffn=jnp.concatenate([g_f0["gain"], g_f1["gain"]], axis=0),
        gdn_w_in=g_gdn["gdn_w_in"][None], gdn_conv=g_gdn["gdn_conv"][None], gdn_w_out=g_gdn["gdn_w_out"][None],
        gdn_out_norm=g_gdn["gdn_out_norm"], gdn_a_log=g_gdn["gdn_a_log"], gdn_dt_bias=g_gdn["gdn_dt_bias"],
        dsw_w_in=g_dsw["dsw_w_in"][None], dsw_w_out=g_dsw["dsw_w_out"][None],
        dsw_q_norm=g_dsw["dsw_q_norm"], dsw_k_norm=g_dsw["dsw_k_norm"], rel_bias=g_dsw["rel_bias"])
    return sse, dx0, grads, dmod


def _prepare_weights(full, small):
    gw = full["gdn_w_in"][0]
    hk3 = 3 * GDN_HEADS * GDN_DK
    di = full["dsw_w_in"][0]
    dq = di.shape[1] // 3
    return dict(
        w_ffn_in=full["w_ffn_in"], w_ffn_out=full["w_ffn_out"],
        gdn_qkv=_hm(gw[:, :hk3]), gdn_z=gw[:, hk3:hk3 + GDN_HEADS * GDN_DK],
        gdn_ab=jnp.pad(gw[:, hk3 + GDN_HEADS * GDN_DK:], ((0, 0), (0, LANES - 2 * GDN_HEADS))),
        gdn_conv=_hm(full["gdn_conv"][0]), gdn_out=full["gdn_w_out"][0],
        dsw_q=di[:, :dq], dsw_k=di[:, dq:2 * dq], dsw_v=di[:, 2 * dq:], dsw_out=full["dsw_w_out"][0],
        norm_mix=small["norm_mix"], norm_ffn=small["norm_ffn"],
        gdn_a_log=small["gdn_a_log"].reshape(GDN_HEADS, 1, 1), gdn_dt_bias=small["gdn_dt_bias"].reshape(GDN_HEADS, 1, 1),
        gdn_out_norm=small["gdn_out_norm"], dsw_q_norm=small["dsw_q_norm"], dsw_k_norm=small["dsw_k_norm"],
        dsw_bias=_dsw_bias(small["rel_bias"]))


_W_NAMES = ("w_ada", "b_ada", "norm_mix", "norm_ffn", "w_ffn_in", "w_ffn_out", "gdn_w_in", "gdn_conv",
            "gdn_a_log", "gdn_dt_bias", "gdn_out_norm", "gdn_w_out", "dsw_w_in", "dsw_q_norm", "dsw_k_norm",
            "dsw_w_out", "rel_bias")
_PAD_BATCH = 16


def _pad_rows(a, rows):
    return jnp.pad(a, ((0, rows - a.shape[0]), (0, 0)))


def kernel(x, c, w_ada, b_ada, norm_mix, norm_ffn, w_ffn_in, w_ffn_out, gdn_w_in, gdn_conv, gdn_a_log, gdn_dt_bias, gdn_out_norm, gdn_w_out, dsw_w_in, dsw_q_norm, dsw_k_norm, dsw_w_out, rel_bias, loss_target, m_w_ada, m_b_ada, m_norm_mix, m_norm_ffn, m_w_ffn_in, m_w_ffn_out, m_gdn_w_in, m_gdn_conv, m_gdn_a_log, m_gdn_dt_bias, m_gdn_out_norm, m_gdn_w_out, m_dsw_w_in, m_dsw_q_norm, m_dsw_k_norm, m_dsw_w_out, m_rel_bias, v_w_ada, v_b_ada, v_norm_mix, v_norm_ffn, v_w_ffn_in, v_w_ffn_out, v_gdn_w_in, v_gdn_conv, v_gdn_a_log, v_gdn_dt_bias, v_gdn_out_norm, v_gdn_w_out, v_dsw_w_in, v_dsw_q_norm, v_dsw_k_norm, v_dsw_w_out, v_rel_bias):
    w = dict(zip(_W_NAMES, (w_ada, b_ada, norm_mix, norm_ffn, w_ffn_in, w_ffn_out, gdn_w_in, gdn_conv, gdn_a_log,
                            gdn_dt_bias, gdn_out_norm, gdn_w_out, dsw_w_in, dsw_q_norm, dsw_k_norm, dsw_w_out,
                            rel_bias)))
    m = dict(zip(_W_NAMES, (m_w_ada, m_b_ada, m_norm_mix, m_norm_ffn, m_w_ffn_in, m_w_ffn_out, m_gdn_w_in,
                            m_gdn_conv, m_gdn_a_log, m_gdn_dt_bias, m_gdn_out_norm, m_gdn_w_out, m_dsw_w_in,
                            m_dsw_q_norm, m_dsw_k_norm, m_dsw_w_out, m_rel_bias)))
    v = dict(zip(_W_NAMES, (v_w_ada, v_b_ada, v_norm_mix, v_norm_ffn, v_w_ffn_in, v_w_ffn_out, v_gdn_w_in,
                            v_gdn_conv, v_gdn_a_log, v_gdn_dt_bias, v_gdn_out_norm, v_gdn_w_out, v_dsw_w_in,
                            v_dsw_q_norm, v_dsw_k_norm, v_dsw_w_out, v_rel_bias)))
    D = x.shape[-1]
    n_layers, _, ada_cols = w_ada.shape

    c_all = _exchange(c.reshape(D // LANES, LANES), gather=True, name="gather_cond").reshape(N_DEV, D)
    c_pad = _pad_rows(c_all, _PAD_BATCH)
    proj = [_mm(c_pad, w_ada[l], mode="nn", name=f"ada_proj_{l}", tm=_PAD_BATCH, tn=ada_cols, tk=D, a_silu=True)
            for l in range(n_layers)]
    mod_send = _pack([(jnp.stack([p[:N_DEV] for p in proj], axis=1), 1)], _ROW_ALIGN)
    mod_recv = _exchange(mod_send, gather=False, name="scatter_mod")
    mod = _unpack(mod_recv, [(n_layers, ada_cols)], 1)[0]
    mod = jnp.transpose(mod, (1, 0, 2)).reshape(n_layers, N_DEV * ada_cols) + b_ada
    mod = mod.reshape(n_layers, 6, D)

    conv_hi = gdn_conv.astype(BF16)
    conv_lo = (gdn_conv - conv_hi.astype(F32)).astype(BF16)
    w_send = _pack([(conv_hi if n == "gdn_conv" else w[n].astype(BF16), 0) for n in _BIG] + [(conv_lo, 0)],
                   _ROW_ALIGN)
    w_all = _gather_two_level(w_send, name="gather_weights")
    parts = _unpack(w_all, [w[n].shape for n in _BIG] + [gdn_conv.shape], 1)
    full = {n: _to_natural(parts[i], _SHARD_AXIS[n]) for i, n in enumerate(_BIG)}
    full["gdn_conv"] = full["gdn_conv"].astype(F32) + _to_natural(parts[-1], _SHARD_AXIS["gdn_conv"]).astype(F32)
    W = _prepare_weights(full, {n: w[n] for n in _SMALL})

    sse, grad_x, grads, dmod = _local_step(x[0], loss_target[0], mod, W)
    loss = lax.psum(0.5 * sse[0, 0] / D, ("x", "y", "c"))

    grads["b_ada"] = dmod.reshape(n_layers, 6 * D)
    big_send = _pack([(_to_blocked(grads[n], _SHARD_AXIS[n]), 1) for n in _BIG], _BIG_ALIGN)
    dmod_send = _pack([(jnp.transpose(dmod.reshape(n_layers, N_DEV, ada_cols), (1, 0, 2)), 1)], _ROW_ALIGN)
    small_send = _pack([(grads[n].reshape(w[n].shape), 0) for n in _SMALL], _ROW_ALIGN)
    g_send = jnp.concatenate(
        [big_send, dmod_send, jnp.broadcast_to(small_send[None], (N_DEV,) + small_send.shape)],
        axis=1).astype(BF16)
    g_recv = _exchange(g_send, gather=False, name="scatter_grads")
    big_rows, dmod_rows = big_send.shape[1], dmod_send.shape[1]

    out = {}
    packed = [_pack([(t[n], 0) for n in _BIG], _BIG_ALIGN) for t in (w, m, v)]
    res = _adamw(packed[0], g_recv, packed[1], packed[2], name="adamw_sharded")
    for kind, buf in zip(("grad", "delta", "new_m", "new_v"), res):
        for n, a in zip(_BIG, _unpack(buf, [w[n].shape for n in _BIG], 0)):
            out[kind, n] = a

    dmod_all = _unpack(lax.slice_in_dim(g_recv, big_rows, big_rows + dmod_rows, axis=1),
                       [(n_layers, ada_cols)], 1)[0]
    g_ada = jnp.stack([_mm(c_pad, _pad_rows(dmod_all[:, l], _PAD_BATCH), mode="tn", name=f"ada_dw_{l}",
                           tm=D, tn=ada_cols, tk=_PAD_BATCH, a_silu=True) for l in range(n_layers)])
    flat = lambda a: a.reshape(n_layers * D, ada_cols)
    res = _adamw(flat(w_ada), flat(g_ada)[None], flat(m_w_ada), flat(v_w_ada), name="adamw_ada")
    for kind, buf in zip(("grad", "delta", "new_m", "new_v"), res):
        out[kind, "w_ada"] = buf.reshape(w_ada.shape)

    small_parts = lax.slice_in_dim(g_recv, big_rows + dmod_rows, g_recv.shape[1], axis=1)
    packed = [_pack([(t[n], 0) for n in _SMALL], _ROW_ALIGN) for t in (w, m, v)]
    res = _adamw(packed[0], small_parts, packed[1], packed[2], name="adamw_replicated")
    for kind, buf in zip(("grad", "delta", "new_m", "new_v"), res):
        for n, a in zip(_SMALL, _unpack(buf, [w[n].shape for n in _SMALL], 0)):
            out[kind, n] = a

    return (loss, grad_x[None]) + tuple(out[kind, n] for kind in ("grad", "delta", "new_m", "new_v")
                                        for n in _W_NAMES)
```

```python
import functools
import math

import numpy as np
import jax
import jax.numpy as jnp
from jax import lax
from jax.experimental import pallas as pl
from jax.experimental.pallas import tpu as pltpu

F32 = jnp.float32
BF16 = jnp.bfloat16

N_DEV = 8
RMS_EPS = 1e-6
LANES = 128
V7X_VMEM_LIMIT = 48 * 1024 * 1024

GDN_HEADS = 8
GDN_DK = 128
GDN_CHUNK = 64
GDN_CONV = 4
DSW_GROUPS = ((128, 1), (512, 4), (2048, 16))
DSW_HEADS = 8
DSW_DH = 64
DSW_BLK = 128
REL_BUCKETS = 32
REL_MAX_DIST = 2048

ADAM_LR = 0.001
ADAM_B1 = 0.9
ADAM_B2 = 0.999
ADAM_EPS = 1e-08
ADAM_WD = 0.01
ADAM_STEP = 10

NEG_BIG = -1e30


def _params(*sem):
    return pltpu.CompilerParams(dimension_semantics=sem, vmem_limit_bytes=V7X_VMEM_LIMIT)


def _sigmoid(x):
    return 1.0 / (1.0 + jnp.exp(-x))


def _silu(x):
    return x * _sigmoid(x)


_DOT_DIMS = {
    "nn": (((1,), (0,)), ((), ())),
    "nt": (((1,), (1,)), ((), ())),
    "tn": (((0,), (0,)), ((), ())),
}


def _mm(a, b, *, mode, name, tm, tn, tk, out_dtype=F32, a_scale=None, out_scale=None, resid=None, a_silu=False):
    if mode == "nn":
        (M, K), N = a.shape, b.shape[1]
    elif mode == "nt":
        (M, K), N = a.shape, b.shape[0]
    else:
        (K, M), N = a.shape, b.shape[1]
    tm, tn, tk = min(tm, M), min(tn, N), min(tk, K)
    assert M % tm == 0 and N % tn == 0 and K % tk == 0, (name, M, N, K, tm, tn, tk)
    nk = K // tk

    def body(*refs):
        refs = list(refs)
        a_ref, b_ref = refs.pop(0), refs.pop(0)
        as_ref = refs.pop(0) if a_scale is not None else None
        os_ref = refs.pop(0) if out_scale is not None else None
        r_ref = refs.pop(0) if resid is not None else None
        o_ref = refs.pop(0)
        acc_ref = refs.pop(0) if nk > 1 else None

        av = a_ref[...]
        if a_silu:
            av = _silu(av.astype(F32))
        if as_ref is not None:
            av = av.astype(F32) * as_ref[...]
        part = lax.dot_general(av.astype(BF16), b_ref[...].astype(BF16), _DOT_DIMS[mode],
                               preferred_element_type=F32)

        def finish(r):
            if os_ref is not None:
                r = r * os_ref[...]
            if r_ref is not None:
                r = r + r_ref[...].astype(F32)
            o_ref[...] = r.astype(out_dtype)

        if nk == 1:
            finish(part)
        else:
            k = pl.program_id(2)

            @pl.when(k == 0)
            def _():
                acc_ref[...] = part

            @pl.when(k > 0)
            def _():
                acc_ref[...] += part

            @pl.when(k == nk - 1)
            def _():
                finish(acc_ref[...])

    if mode == "nn":
        a_spec = pl.BlockSpec((tm, tk), lambda i, j, k: (i, k))
        b_spec = pl.BlockSpec((tk, tn), lambda i, j, k: (k, j))
        as_spec = pl.BlockSpec((1, tk), lambda i, j, k: (0, k))
    elif mode == "nt":
        a_spec = pl.BlockSpec((tm, tk), lambda i, j, k: (i, k))
        b_spec = pl.BlockSpec((tn, tk), lambda i, j, k: (j, k))
        as_spec = pl.BlockSpec((1, tk), lambda i, j, k: (0, k))
    else:
        a_spec = pl.BlockSpec((tk, tm), lambda i, j, k: (k, i))
        b_spec = pl.BlockSpec((tk, tn), lambda i, j, k: (k, j))
        as_spec = None
    in_specs, args = [a_spec, b_spec], [a, b]
    if a_scale is not None:
        in_specs.append(as_spec)
        args.append(a_scale)
    if out_scale is not None:
        in_specs.append(pl.BlockSpec((1, tn), lambda i, j, k: (0, j)))
        args.append(out_scale)
    if resid is not None:
        in_specs.append(pl.BlockSpec((tm, tn), lambda i, j, k: (i, j)))
        args.append(resid)
    return pl.pallas_call(
        body, name=name, grid=(M // tm, N // tn, nk),
        in_specs=in_specs, out_specs=pl.BlockSpec((tm, tn), lambda i, j, k: (i, j)),
        out_shape=jax.ShapeDtypeStruct((M, N), out_dtype),
        scratch_shapes=[pltpu.VMEM((tm, tn), F32)] if nk > 1 else [],
        compiler_params=_params("parallel", "parallel", "arbitrary"),
    )(*args)


def _norm_mod_fwd(x, gain, sc, sh, *, name):
    S, D = x.shape
    tr = min(512, S)

    def body(x_ref, g_ref, sc_ref, sh_ref, h_ref):
        xv = x_ref[...]
        r = lax.rsqrt(jnp.mean(xv * xv, axis=-1, keepdims=True) + RMS_EPS)
        h_ref[...] = ((xv * r) * g_ref[...] * (1.0 + sc_ref[...]) + sh_ref[...]).astype(BF16)

    row = pl.BlockSpec((tr, D), lambda i: (i, 0))
    vec = pl.BlockSpec((1, D), lambda i: (0, 0))
    return pl.pallas_call(
        body, name=name, grid=(S // tr,), in_specs=[row, vec, vec, vec], out_specs=row,
        out_shape=jax.ShapeDtypeStruct((S, D), BF16), compiler_params=_params("parallel"),
    )(x, gain, sc, sh)


def _norm_mod_bwd(dh, x, dx_res, gain, sc, *, name):
    S, D = x.shape
    tr = min(256, S)
    n_steps = S // tr

    def body(dh_ref, x_ref, dxr_ref, g_ref, sc_ref, dx_ref, dsh_ref, dsc_ref, dgain_ref, acc_sh, acc_a):
        i = pl.program_id(0)
        xv = x_ref[...]
        r = lax.rsqrt(jnp.mean(xv * xv, axis=-1, keepdims=True) + RMS_EPS)
        n = xv * r
        dhv = dh_ref[...].astype(F32)
        dn = dhv * (g_ref[...] * (1.0 + sc_ref[...]))
        dx_ref[...] = dxr_ref[...] + r * (dn - n * jnp.mean(dn * n, axis=-1, keepdims=True))
        p_sh = jnp.sum(dhv, axis=0, keepdims=True)
        p_a = jnp.sum(dhv * n, axis=0, keepdims=True)

        @pl.when(i == 0)
        def _():
            acc_sh[...] = p_sh
            acc_a[...] = p_a

        @pl.when(i > 0)
        def _():
            acc_sh[...] += p_sh
            acc_a[...] += p_a

        @pl.when(i == n_steps - 1)
        def _():
            dsh_ref[...] = acc_sh[...]
            dsc_ref[...] = acc_a[...] * g_ref[...]
            dgain_ref[...] = acc_a[...] * (1.0 + sc_ref[...])

    row = pl.BlockSpec((tr, D), lambda i: (i, 0))
    vec = pl.BlockSpec((1, D), lambda i: (0, 0))
    vshape = jax.ShapeDtypeStruct((1, D), F32)
    return pl.pallas_call(
        body, name=name, grid=(n_steps,), in_specs=[row, row, row, vec, vec],
        out_specs=[row, vec, vec, vec],
        out_shape=[jax.ShapeDtypeStruct((S, D), F32), vshape, vshape, vshape],
        scratch_shapes=[pltpu.VMEM((1, D), F32), pltpu.VMEM((1, D), F32)],
        compiler_params=_params("arbitrary"),
    )(dh, x, dx_res, gain, sc)


def _wout_grad(gmat, w, gate, *, name):
    K, D = w.shape
    tr = min(256, K)
    n_steps = K // tr

    def body(g_ref, w_ref, gate_ref, dw_ref, dgate_ref, acc):
        i = pl.program_id(0)
        gv = g_ref[...]
        dw_ref[...] = gv * gate_ref[...]
        part = jnp.sum(gv * w_ref[...], axis=0, keepdims=True)

        @pl.when(i == 0)
        def _():
            acc[...] = part

        @pl.when(i > 0)
        def _():
            acc[...] += part

        @pl.when(i == n_steps - 1)
        def _():
            dgate_ref[...] = acc[...]

    row = pl.BlockSpec((tr, D), lambda i: (i, 0))
    vec = pl.BlockSpec((1, D), lambda i: (0, 0))
    return pl.pallas_call(
        body, name=name, grid=(n_steps,), in_specs=[row, row, vec], out_specs=[row, vec],
        out_shape=[jax.ShapeDtypeStruct((K, D), F32), jax.ShapeDtypeStruct((1, D), F32)],
        scratch_shapes=[pltpu.VMEM((1, D), F32)], compiler_params=_params("arbitrary"),
    )(gmat, w, gate)


def _swiglu_fwd(p, *, name):
    S, F2 = p.shape
    F = F2 // 2
    tr = min(256, S)

    def body(p_ref, a_ref):
        gate = p_ref[:, :F].astype(F32)
        up = p_ref[:, F:].astype(F32)
        a_ref[...] = (_silu(gate) * up).astype(BF16)

    return pl.pallas_call(
        body, name=name, grid=(S // tr,), in_specs=[pl.BlockSpec((tr, F2), lambda i: (i, 0))],
        out_specs=pl.BlockSpec((tr, F), lambda i: (i, 0)),
        out_shape=jax.ShapeDtypeStruct((S, F), BF16), compiler_params=_params("parallel"),
    )(p)


def _swiglu_bwd(da, p, *, name):
    S, F2 = p.shape
    F = F2 // 2
    tr = min(256, S)

    def body(da_ref, p_ref, dp_ref):
        gate = p_ref[:, :F].astype(F32)
        up = p_ref[:, F:].astype(F32)
        dav = da_ref[...].astype(F32)
        sg = _sigmoid(gate)
        dp_ref[:, :F] = (dav * up * (sg * (1.0 + gate * (1.0 - sg)))).astype(BF16)
        dp_ref[:, F:] = (dav * (gate * sg)).astype(BF16)

    return pl.pallas_call(
        body, name=name, grid=(S // tr,),
        in_specs=[pl.BlockSpec((tr, F), lambda i: (i, 0)), pl.BlockSpec((tr, F2), lambda i: (i, 0))],
        out_specs=pl.BlockSpec((tr, F2), lambda i: (i, 0)),
        out_shape=jax.ShapeDtypeStruct((S, F2), BF16), compiler_params=_params("parallel"),
    )(da, p)


def _loss_head(y, target, *, name):
    S, D = y.shape
    tr = min(512, S)
    n_steps = S // tr

    def body(y_ref, t_ref, dy_ref, sse_ref, acc):
        i = pl.program_id(0)
        e = y_ref[...] - t_ref[...]
        dy_ref[...] = e * (1.0 / D)
        part = jnp.sum(e * e, axis=0, keepdims=True)

        @pl.when(i == 0)
        def _():
            acc[...] = part

        @pl.when(i > 0)
        def _():
            acc[...] += part

        @pl.when(i == n_steps - 1)
        def _():
            sse_ref[...] = jnp.sum(acc[...], axis=1, keepdims=True)

    row = pl.BlockSpec((tr, D), lambda i: (i, 0))
    return pl.pallas_call(
        body, name=name, grid=(n_steps,), in_specs=[row, row],
        out_specs=[row, pl.BlockSpec((1, 1), lambda i: (0, 0))],
        out_shape=[jax.ShapeDtypeStruct((S, D), F32), jax.ShapeDtypeStruct((1, 1), F32)],
        scratch_shapes=[pltpu.VMEM((1, D), F32)], compiler_params=_params("arbitrary"),
    )(y, target)


def _adamw(w, g_parts, m, v, *, name):
    R, C = w.shape
    P = g_parts.shape[0]
    tr = _tile(R, max(8, 1024 * LANES // C))
    c1 = 1.0 / (1.0 - ADAM_B1 ** ADAM_STEP)
    c2 = 1.0 / (1.0 - ADAM_B2 ** ADAM_STEP)

    def body(w_ref, g_ref, m_ref, v_ref, go_ref, d_ref, mo_ref, vo_ref):
        g = g_ref[0].astype(F32)
        for q in range(1, P):
            g = g + g_ref[q].astype(F32)
        mn = ADAM_B1 * m_ref[...] + (1.0 - ADAM_B1) * g
        vn = ADAM_B2 * v_ref[...] + (1.0 - ADAM_B2) * (g * g)
        go_ref[...] = g
        mo_ref[...] = mn
        vo_ref[...] = vn
        d_ref[...] = -ADAM_LR * ((mn * c1) / (jnp.sqrt(vn * c2) + ADAM_EPS) + ADAM_WD * w_ref[...])

    row = pl.BlockSpec((tr, C), lambda i: (i, 0))
    shp = jax.ShapeDtypeStruct((R, C), F32)
    return pl.pallas_call(
        body, name=name, grid=(R // tr,),
        in_specs=[row, pl.BlockSpec((P, tr, C), lambda i: (0, i, 0)), row, row],
        out_specs=[row, row, row, row], out_shape=[shp, shp, shp, shp],
        compiler_params=_params("parallel"),
    )(w, g_parts, m, v)


_HALO = 16


def _conv_taps(buf, w_ref, rows, cols):
    acc = None
    for j in range(GDN_CONV):
        term = buf[pl.ds(_HALO - (GDN_CONV - 1) + j, rows), cols] * w_ref[j:j + 1, cols]
        acc = term if acc is None else acc + term
    return acc


def _fill_conv_buf(buf, halo_ref, x_ref, rows, first):
    buf[0:_HALO, :] = jnp.where(first, 0.0, halo_ref[...].astype(F32))
    buf[_HALO:_HALO + rows, :] = x_ref[...].astype(F32)


_HM = 3 * GDN_DK
_GDN_ROWS = 256
_PREP_HEADS = 4


def _l2n(seg):
    return lax.rsqrt(jnp.sum(seg * seg, axis=-1, keepdims=True) + RMS_EPS)


def _head_cols(hh):
    return slice(hh * _HM, (hh + 1) * _HM)


def _gdn_prep_fwd(x, conv_w, *, name):
    S, C3 = x.shape
    CB = _PREP_HEADS * _HM
    RB = min(256, S)

    def body(x_ref, halo_ref, w_ref, o_ref, buf):
        i = pl.program_id(0)
        _fill_conv_buf(buf, halo_ref, x_ref, RB, i == 0)
        for hh in range(_PREP_HEADS):
            c0 = hh * _HM
            y = _silu(_conv_taps(buf, w_ref, RB, _head_cols(hh)))
            q, k = y[:, :GDN_DK], y[:, GDN_DK:2 * GDN_DK]
            o_ref[:, c0:c0 + GDN_DK] = q * (_l2n(q) * GDN_DK ** -0.5)
            o_ref[:, c0 + GDN_DK:c0 + 2 * GDN_DK] = k * _l2n(k)
            o_ref[:, c0 + 2 * GDN_DK:c0 + _HM] = y[:, 2 * GDN_DK:]

    hb = RB // _HALO
    return pl.pallas_call(
        body, name=name, grid=(S // RB, C3 // CB),
        in_specs=[pl.BlockSpec((RB, CB), lambda i, j: (i, j)),
                  pl.BlockSpec((_HALO, CB), lambda i, j: (jnp.maximum(i * hb - 1, 0), j)),
                  pl.BlockSpec((GDN_CONV, CB), lambda i, j: (0, j))],
        out_specs=pl.BlockSpec((RB, CB), lambda i, j: (i, j)),
        out_shape=jax.ShapeDtypeStruct((S, C3), F32),
        scratch_shapes=[pltpu.VMEM((RB + _HALO, CB), F32)],
        compiler_params=_params("parallel", "parallel"),
    )(x, x, conv_w)


def _gdn_prep_bwd_pre(dn, x, conv_w, *, name):
    S, C3 = x.shape
    CB = _PREP_HEADS * _HM
    RB = min(256, S)
    n_steps = S // RB

    def body(dn_ref, x_ref, halo_ref, w_ref, dc_ref, dw_ref, buf):
        i = pl.program_id(1)
        _fill_conv_buf(buf, halo_ref, x_ref, RB, i == 0)
        head_parts = []
        for hh in range(_PREP_HEADS):
            c0, cols = hh * _HM, _head_cols(hh)
            acc = _conv_taps(buf, w_ref, RB, cols)
            sg = _sigmoid(acc)
            y = acc * sg
            dsilu = sg * (1.0 + acc * (1.0 - sg))
            for part, scale in ((0, GDN_DK ** -0.5), (1, 1.0)):
                sl = slice(part * GDN_DK, (part + 1) * GDN_DK)
                seg = y[:, sl]
                r = _l2n(seg)
                n = seg * r
                d = dn_ref[:, c0 + part * GDN_DK:c0 + (part + 1) * GDN_DK] * scale
                dc_ref[:, c0 + part * GDN_DK:c0 + (part + 1) * GDN_DK] = (
                    r * (d - n * jnp.sum(d * n, axis=-1, keepdims=True)) * dsilu[:, sl])
            dc_ref[:, c0 + 2 * GDN_DK:c0 + _HM] = dn_ref[:, c0 + 2 * GDN_DK:c0 + _HM] * dsilu[:, 2 * GDN_DK:]
            dc = dc_ref[:, cols]
            taps = [jnp.sum(dc * buf[pl.ds(_HALO - (GDN_CONV - 1) + t, RB), cols], axis=0, keepdims=True)
                    for t in range(GDN_CONV)]
            head_parts.append(jnp.concatenate(taps + [jnp.zeros((8 - GDN_CONV, _HM), F32)], axis=0))
        part = jnp.concatenate(head_parts, axis=1)

        @pl.when(i == 0)
        def _():
            dw_ref[...] = part

        @pl.when(i > 0)
        def _():
            dw_ref[...] += part

    hb = RB // _HALO
    return pl.pallas_call(
        body, name=name, grid=(C3 // CB, n_steps),
        in_specs=[pl.BlockSpec((RB, CB), lambda j, i: (i, j)),
                  pl.BlockSpec((RB, CB), lambda j, i: (i, j)),
                  pl.BlockSpec((_HALO, CB), lambda j, i: (jnp.maximum(i * hb - 1, 0), j)),
                  pl.BlockSpec((GDN_CONV, CB), lambda j, i: (0, j))],
        out_specs=[pl.BlockSpec((RB, CB), lambda j, i: (i, j)),
                   pl.BlockSpec((8, CB), lambda j, i: (0, j))],
        out_shape=[jax.ShapeDtypeStruct((S, C3), F32), jax.ShapeDtypeStruct((8, C3), F32)],
        scratch_shapes=[pltpu.VMEM((RB + _HALO, CB), F32)],
        compiler_params=_params("parallel", "arbitrary"),
    )(dn, x, x, conv_w)


def _gdn_conv_bwd_x(dc, conv_w, *, name):
    S, C3 = dc.shape
    CB = _PREP_HEADS * _HM
    RB = min(256, S)
    n_steps = S // RB

    def body(dc_ref, halo_ref, w_ref, dx_ref, buf):
        i = pl.program_id(0)
        buf[0:RB, :] = dc_ref[...]
        buf[RB:RB + _HALO, :] = jnp.where(i == n_steps - 1, 0.0, halo_ref[...])
        for hh in range(_PREP_HEADS):
            cols = _head_cols(hh)
            acc = None
            for j in range(GDN_CONV):
                term = buf[pl.ds(GDN_CONV - 1 - j, RB), cols] * w_ref[j:j + 1, cols]
                acc = term if acc is None else acc + term
            dx_ref[:, cols] = acc.astype(BF16)

    hb = RB // _HALO
    last = S // _HALO - 1
    return pl.pallas_call(
        body, name=name, grid=(n_steps, C3 // CB),
        in_specs=[pl.BlockSpec((RB, CB), lambda i, j: (i, j)),
                  pl.BlockSpec((_HALO, CB), lambda i, j: (jnp.minimum((i + 1) * hb, last), j)),
                  pl.BlockSpec((GDN_CONV, CB), lambda i, j: (0, j))],
        out_specs=pl.BlockSpec((RB, CB), lambda i, j: (i, j)),
        out_shape=jax.ShapeDtypeStruct((S, C3), BF16),
        scratch_shapes=[pltpu.VMEM((RB + _HALO, CB), F32)],
        compiler_params=_params("parallel", "parallel"),
    )(dc, dc, conv_w)


def _split_bf16(a):
    hi = a.astype(BF16)
    return hi, (a - hi.astype(F32)).astype(BF16)


def _dot(a, b, dims="nn", exact=False):
    def dot(p, q):
        return lax.dot_general(p, q, _DOT_DIMS[dims], preferred_element_type=F32)

    if exact:
        (ah, al), (bh, bl) = _split_bf16(a), _split_bf16(b)
        return dot(ah, bh) + (dot(ah, bl) + dot(al, bh))
    return dot(a.astype(BF16), b.astype(BF16))


def _softplus(x):
    return jnp.maximum(x, 0.0) + jnp.log(1.0 + jnp.exp(-jnp.abs(x)))


def _to_col(row, eye):
    return jnp.sum(jnp.where(eye, row, 0.0), axis=1, keepdims=True)


def _to_row(col, eye):
    return jnp.sum(jnp.where(eye, col, 0.0), axis=0, keepdims=True)


def _unit_lower_inverse(low, ri, ci):
    n = range(len(low))
    C = low[0].shape[0]
    eye = jnp.where(ri == ci, 1.0, 0.0)
    pair = (ri >> 1) == (ci >> 1)
    x = [eye - jnp.where(pair, low[j], 0.0) for j in n]
    m, sh = 2, 1
    while m < C:
        join = ((ri >> (sh + 1)) == (ci >> (sh + 1))) & (((ri >> sh) & 1) == 1) & (((ci >> sh) & 1) == 0)
        y = [_dot(x[j], jnp.where(join, low[j], 0.0)) for j in n]
        x = [x[j] - _dot(y[j], x[j]) for j in n]
        m, sh = 2 * m, sh + 1
    lx = [_dot(low[j], x[j], exact=True) for j in n]
    corr = [_dot(x[j], eye - x[j] - lx[j]) for j in n]
    return [x[j] + corr[j] for j in n]


def _gdn_local_batch(qkv, g_row, beta_row, ri, ci):
    n = range(len(qkv))
    eye, tril, strict = ri == ci, ri >= ci, ri > ci
    q = [qkv[j][:, :GDN_DK] for j in n]
    k = [qkv[j][:, GDN_DK:2 * GDN_DK] for j in n]
    v = [qkv[j][:, 2 * GDN_DK:] for j in n]
    g_col = [_to_col(g_row[j], eye) for j in n]
    beta_col = [_to_col(beta_row[j], eye) for j in n]
    gc_col = [jnp.sum(jnp.where(tril, g_row[j], 0.0), axis=1, keepdims=True) for j in n]
    gc_row = [jnp.sum(jnp.where(ri <= ci, g_col[j], 0.0), axis=0, keepdims=True) for j in n]
    g_last = [jnp.sum(g_row[j], axis=1, keepdims=True) for j in n]
    decay = [jnp.where(tril, jnp.exp(jnp.minimum(gc_col[j] - gc_row[j], 0.0)), 0.0) for j in n]
    e_col = [jnp.exp(gc_col[j]) for j in n]
    f_col = [jnp.exp(g_last[j] - gc_col[j]) for j in n]
    e_last = [jnp.exp(g_last[j]) for j in n]
    kb = [k[j] * beta_col[j] for j in n]
    vb = [v[j] * beta_col[j] for j in n]
    kk = [_dot(kb[j], k[j], "nt") for j in n]
    qk = [_dot(q[j], k[j], "nt") for j in n]
    low = [jnp.where(strict, kk[j] * decay[j], 0.0) for j in n]
    att = [qk[j] * decay[j] for j in n]
    return dict(q=q, k=k, v=v, beta_col=beta_col, decay=decay, e_col=e_col, f_col=f_col, e_last=e_last,
                kb=kb, vb=vb, low=low, att=att, eye=eye, strict=strict, tril=tril)


def _chunk_iotas():
    C = GDN_CHUNK
    return lax.broadcasted_iota(jnp.int32, (C, C), 0), lax.broadcasted_iota(jnp.int32, (C, C), 1)


def _gdn_chunk_fwd(qkv, ab, a_log, dt_bias, *, name):
    S = qkv.shape[0]
    H, C, DK = GDN_HEADS, GDN_CHUNK, GDN_DK
    RB = min(_GDN_ROWS, S)
    NCB, NB, NC = RB // C, S // RB, S // C
    heads = range(H)

    def body(qkv_ref, ab_ref, alog_ref, dtb_ref, o_ref, st_ref, t_ref, state, u_s, w_s, qe_s, kf_s, att_s):
        nb = pl.program_id(0)

        @pl.when(nb == 0)
        def _():
            state[...] = jnp.zeros_like(state)

        ri, ci = _chunk_iotas()
        neg_a = [-jnp.exp(alog_ref[h]) for h in heads]
        e_last = []
        for c in range(NCB):
            rows = pl.ds(c * C, C)
            g_row = [neg_a[h] * _softplus(ab_ref[h, c] + dtb_ref[h]) for h in heads]
            beta_row = [_sigmoid(ab_ref[H + h, c]) for h in heads]
            L = _gdn_local_batch([qkv_ref[rows, h * _HM:(h + 1) * _HM] for h in heads], g_row, beta_row, ri, ci)
            tinv = _unit_lower_inverse(L["low"], ri, ci)
            u = [_dot(tinv[h], L["vb"][h], exact=True) for h in heads]
            w = [_dot(tinv[h], L["kb"][h] * L["e_col"][h], exact=True) for h in heads]
            for h in heads:
                t_ref[h, c] = tinv[h]
                u_s[c, h] = u[h]
                w_s[c, h] = w[h].astype(BF16)
                qe_s[c, h] = (L["q"][h] * L["e_col"][h]).astype(BF16)
                kf_s[c, h] = (L["k"][h] * L["f_col"][h]).astype(BF16)
                att_s[c, h] = L["att"][h].astype(BF16)
            e_last.append(L["e_last"])
        st = [state[h] for h in heads]
        for c in range(NCB):
            rows = pl.ds(c * C, C)
            stb = [st[h].astype(BF16) for h in heads]
            vn = [u_s[c, h] - _dot(w_s[c, h], stb[h]) for h in heads]
            vnb = [vn[h].astype(BF16) for h in heads]
            out = [_dot(qe_s[c, h], stb[h]) + _dot(att_s[c, h], vnb[h]) for h in heads]
            new = [st[h] * e_last[c][h] + _dot(kf_s[c, h], vnb[h], "tn") for h in heads]
            for h in heads:
                o_ref[rows, h * DK:(h + 1) * DK] = out[h]
                st_ref[h, c] = st[h]
            st = new
        for h in heads:
            state[h] = st[h]

    return pl.pallas_call(
        body, name=name, grid=(NB,),
        in_specs=[pl.BlockSpec((RB, H * _HM), lambda n: (n, 0)),
                  pl.BlockSpec((2 * H, NCB, 1, C), lambda n: (0, n, 0, 0)),
                  pl.BlockSpec((H, 1, 1), lambda n: (0, 0, 0)),
                  pl.BlockSpec((H, 1, 1), lambda n: (0, 0, 0))],
        out_specs=[pl.BlockSpec((RB, H * DK), lambda n: (n, 0)),
                   pl.BlockSpec((H, NCB, DK, DK), lambda n: (0, n, 0, 0)),
                   pl.BlockSpec((H, NCB, C, C), lambda n: (0, n, 0, 0))],
        out_shape=[jax.ShapeDtypeStruct((S, H * DK), F32),
                   jax.ShapeDtypeStruct((H, NC, DK, DK), F32),
                   jax.ShapeDtypeStruct((H, NC, C, C), F32)],
        scratch_shapes=[pltpu.VMEM((H, DK, DK), F32), pltpu.VMEM((NCB, H, C, DK), F32),
                        pltpu.VMEM((NCB, H, C, DK), BF16), pltpu.VMEM((NCB, H, C, DK), BF16),
                        pltpu.VMEM((NCB, H, C, DK), BF16), pltpu.VMEM((NCB, H, C, C), BF16)],
        compiler_params=_params("arbitrary"),
    )(qkv, ab, a_log, dt_bias)


def _gdn_chunk_bwd(qkv, ab, a_log, dt_bias, states, tinvs, do, *, name):
    S = qkv.shape[0]
    H, C, DK = GDN_HEADS, GDN_CHUNK, GDN_DK
    RB = min(_GDN_ROWS, S)
    NCB, NB, NC = RB // C, S // RB, S // C
    heads = range(H)

    def body(qkv_ref, ab_ref, alog_ref, dtb_ref, st_ref, t_ref, do_ref,
             dqkv_ref, dab_ref, dalog_ref, ddtb_ref, dstate, w_s, vn_s, qe_s, kf_s, att_s, dvn_s, dkf_s):
        nb = pl.program_id(0)

        @pl.when(nb == 0)
        def _():
            dstate[...] = jnp.zeros_like(dstate)
            dalog_ref[...] = jnp.zeros_like(dalog_ref)
            ddtb_ref[...] = jnp.zeros_like(ddtb_ref)

        ri, ci = _chunk_iotas()
        neg_a = [-jnp.exp(alog_ref[h]) for h in heads]

        def local(c):
            rows = pl.ds(c * C, C)
            a_pre = [ab_ref[h, c] + dtb_ref[h] for h in heads]
            g_row = [neg_a[h] * _softplus(a_pre[h]) for h in heads]
            beta_row = [_sigmoid(ab_ref[H + h, c]) for h in heads]
            L = _gdn_local_batch([qkv_ref[rows, h * _HM:(h + 1) * _HM] for h in heads], g_row, beta_row, ri, ci)
            return L, a_pre, g_row, beta_row

        e_last = [None] * NCB
        for c in range(NCB):
            L, _, _, _ = local(c)
            kbe = [L["kb"][h] * L["e_col"][h] for h in heads]
            u = [_dot(t_ref[h, c], L["vb"][h], exact=True) for h in heads]
            w = [_dot(t_ref[h, c], kbe[h], exact=True) for h in heads]
            vn = [u[h] - _dot(w[h], st_ref[h, c]) for h in heads]
            for h in heads:
                w_s[c, h] = w[h].astype(BF16)
                vn_s[c, h] = vn[h].astype(BF16)
                qe_s[c, h] = (L["q"][h] * L["e_col"][h]).astype(BF16)
                kf_s[c, h] = (L["k"][h] * L["f_col"][h]).astype(BF16)
                att_s[c, h] = L["att"][h].astype(BF16)
            e_last[c] = L["e_last"]

        dst = [dstate[h] for h in heads]
        de_last = [None] * NCB
        for c in reversed(range(NCB)):
            rows = pl.ds(c * C, C)
            dob = [do_ref[rows, h * DK:(h + 1) * DK].astype(BF16) for h in heads]
            dstb = [dst[h].astype(BF16) for h in heads]
            dvn = [_dot(att_s[c, h], dob[h], "tn") + _dot(kf_s[c, h], dstb[h]) for h in heads]
            dkf = [_dot(vn_s[c, h], dstb[h], "nt") for h in heads]
            de_last[c] = [jnp.sum(jnp.sum(dst[h] * st_ref[h, c], axis=1, keepdims=True), axis=0, keepdims=True)
                          for h in heads]
            new = [dst[h] * e_last[c][h] + _dot(qe_s[c, h], dob[h], "tn")
                   - _dot(w_s[c, h], dvn[h].astype(BF16), "tn") for h in heads]
            for h in heads:
                dvn_s[c, h] = dvn[h]
                dkf_s[c, h] = dkf[h]
            dst = new
        for h in heads:
            dstate[h] = dst[h]

        for c in range(NCB):
            rows = pl.ds(c * C, C)
            L, a_pre, g_row, beta_row = local(c)
            q, k, v, kb, vb = L["q"], L["k"], L["v"], L["kb"], L["vb"]
            e_col, f_col, decay, beta_col = L["e_col"], L["f_col"], L["decay"], L["beta_col"]
            eye, strict, tril = L["eye"], L["strict"], L["tril"]
            tinv = [t_ref[h, c] for h in heads]
            stb = [st_ref[h, c].astype(BF16) for h in heads]
            dov = [do_ref[rows, h * DK:(h + 1) * DK] for h in heads]
            dvn = [dvn_s[c, h] for h in heads]
            dkf = [dkf_s[c, h] for h in heads]
            kbe = [kb[h] * e_col[h] for h in heads]
            datt = [jnp.where(tril, _dot(dov[h], vn_s[c, h], "nt"), 0.0) for h in heads]
            dqe = [_dot(dov[h], stb[h], "nt") for h in heads]
            dw = [-_dot(dvn[h], stb[h], "nt") for h in heads]
            dt = [_dot(dvn[h], vb[h], "nt") + _dot(dw[h], kbe[h], "nt") for h in heads]
            dvb = [_dot(tinv[h], dvn[h], "tn", exact=True) for h in heads]
            dkbe = [_dot(tinv[h], dw[h], "tn", exact=True) for h in heads]
            tdt = [_dot(tinv[h], dt[h], "tn", exact=True) for h in heads]
            dlow = [-jnp.where(strict, _dot(tdt[h], tinv[h], "nt", exact=True), 0.0) for h in heads]
            dkk = [dlow[h] * decay[h] for h in heads]
            dqk = [datt[h] * decay[h] for h in heads]
            dkb = [_dot(dkk[h], k[h]) + dkbe[h] * e_col[h] for h in heads]
            dk = [_dot(dkk[h], kb[h], "tn") + _dot(dqk[h], q[h], "tn") + dkf[h] * f_col[h] + dkb[h] * beta_col[h]
                  for h in heads]
            dq = [_dot(dqk[h], k[h]) + dqe[h] * e_col[h] for h in heads]
            for h in heads:
                dqkv_ref[rows, h * _HM:h * _HM + DK] = dq[h]
                dqkv_ref[rows, h * _HM + DK:h * _HM + 2 * DK] = dk[h]
                dqkv_ref[rows, h * _HM + 2 * DK:(h + 1) * _HM] = dvb[h] * beta_col[h]

            dbeta_col = [jnp.sum(k[h] * dkb[h] + v[h] * dvb[h], axis=1, keepdims=True) for h in heads]
            pmat = [dlow[h] * L["low"][h] + datt[h] * L["att"][h] for h in heads]
            df_col = [jnp.sum(k[h] * dkf[h], axis=1, keepdims=True) * f_col[h] for h in heads]
            dgc_col = [jnp.sum(pmat[h], axis=1, keepdims=True)
                       + jnp.sum(q[h] * dqe[h] + kb[h] * dkbe[h], axis=1, keepdims=True) * e_col[h] - df_col[h]
                       for h in heads]
            dgc_row = [_to_row(dgc_col[h], eye) - jnp.sum(pmat[h], axis=0, keepdims=True) for h in heads]
            dg_last = [jnp.sum(df_col[h], axis=0, keepdims=True) + de_last[c][h] * L["e_last"][h] for h in heads]
            dgc_c = [_to_col(dgc_row[h], eye) for h in heads]
            dg_row = [jnp.sum(jnp.where(ri >= ci, dgc_c[h], 0.0), axis=0, keepdims=True) + dg_last[h] for h in heads]
            dbeta_row = [_to_row(dbeta_col[h], eye) for h in heads]
            for h in heads:
                da_row = dg_row[h] * neg_a[h] * _sigmoid(a_pre[h])
                dab_ref[h, c] = da_row
                dab_ref[H + h, c] = dbeta_row[h] * beta_row[h] * (1.0 - beta_row[h])
                dalog_ref[h] += jnp.sum(dg_row[h] * g_row[h], axis=1, keepdims=True)
                ddtb_ref[h] += jnp.sum(da_row, axis=1, keepdims=True)

    rev = lambda n: NB - 1 - n
    vec = pl.BlockSpec((H, 1, 1), lambda n: (0, 0, 0))
    gates = pl.BlockSpec((2 * H, NCB, 1, C), lambda n: (0, rev(n), 0, 0))
    wide = pl.BlockSpec((RB, H * _HM), lambda n: (rev(n), 0))
    item = lambda dt: pltpu.VMEM((NCB, H, C, DK), dt)
    return pl.pallas_call(
        body, name=name, grid=(NB,),
        in_specs=[wide, gates, vec, vec,
                  pl.BlockSpec((H, NCB, DK, DK), lambda n: (0, rev(n), 0, 0)),
                  pl.BlockSpec((H, NCB, C, C), lambda n: (0, rev(n), 0, 0)),
                  pl.BlockSpec((RB, H * DK), lambda n: (rev(n), 0))],
        out_specs=[wide, gates, vec, vec],
        out_shape=[jax.ShapeDtypeStruct((S, H * _HM), F32),
                   jax.ShapeDtypeStruct((2 * H, NC, 1, C), F32),
                   jax.ShapeDtypeStruct((H, 1, 1), F32),
                   jax.ShapeDtypeStruct((H, 1, 1), F32)],
        scratch_shapes=[pltpu.VMEM((H, DK, DK), F32), item(BF16), item(BF16), item(BF16), item(BF16),
                        pltpu.VMEM((NCB, H, C, C), BF16), item(F32), item(F32)],
        compiler_params=_params("arbitrary"),
    )(qkv, ab, a_log, dt_bias, states, tinvs, do)


def _gdn_outnorm_fwd(o, z, gain, *, name):
    S, HV = o.shape
    RB = min(256, S)

    def body(o_ref, z_ref, g_ref, y_ref):
        for h in range(HV // GDN_DK):
            cols = slice(h * GDN_DK, (h + 1) * GDN_DK)
            ov = o_ref[:, cols]
            r = lax.rsqrt(jnp.mean(ov * ov, axis=-1, keepdims=True) + RMS_EPS)
            y_ref[:, cols] = (ov * r * g_ref[...] * _silu(z_ref[:, cols].astype(F32))).astype(BF16)

    blk = pl.BlockSpec((RB, HV), lambda i: (i, 0))
    return pl.pallas_call(
        body, name=name, grid=(S // RB,),
        in_specs=[blk, blk, pl.BlockSpec((1, GDN_DK), lambda i: (0, 0))], out_specs=blk,
        out_shape=jax.ShapeDtypeStruct((S, HV), BF16), compiler_params=_params("parallel"),
    )(o, z, gain)


def _gdn_outnorm_bwd(dy, o, z, gain, *, name):
    S, HV = o.shape
    RB = min(256, S)

    def body(dy_ref, o_ref, z_ref, g_ref, do_ref, dz_ref, dg_ref):
        part = None
        for h in range(HV // GDN_DK):
            cols = slice(h * GDN_DK, (h + 1) * GDN_DK)
            ov = o_ref[:, cols]
            zv = z_ref[:, cols].astype(F32)
            dyv = dy_ref[:, cols].astype(F32)
            r = lax.rsqrt(jnp.mean(ov * ov, axis=-1, keepdims=True) + RMS_EPS)
            n = ov * r
            sg = _sigmoid(zv)
            dng = dyv * (zv * sg)
            dn = dng * g_ref[...]
            do_ref[:, cols] = r * (dn - n * jnp.mean(dn * n, axis=-1, keepdims=True))
            dz_ref[:, cols] = (dyv * (n * g_ref[...]) * (sg * (1.0 + zv * (1.0 - sg)))).astype(BF16)
            p = jnp.sum(dng * n, axis=0, keepdims=True)
            part = p if part is None else part + p

        @pl.when(pl.program_id(0) == 0)
        def _():
            dg_ref[...] = part

        @pl.when(pl.program_id(0) > 0)
        def _():
            dg_ref[...] += part

    blk = pl.BlockSpec((RB, HV), lambda i: (i, 0))
    vec = pl.BlockSpec((1, GDN_DK), lambda i: (0, 0))
    return pl.pallas_call(
        body, name=name, grid=(S // RB,),
        in_specs=[blk, blk, blk, vec], out_specs=[blk, blk, vec],
        out_shape=[jax.ShapeDtypeStruct((S, HV), F32), jax.ShapeDtypeStruct((S, HV), BF16),
                   jax.ShapeDtypeStruct((1, GDN_DK), F32)],
        compiler_params=_params("arbitrary"),
    )(dy, o, z, gain)


def _rms64(x, gain):
    r = lax.rsqrt(jnp.mean(x * x, axis=-1, keepdims=True) + RMS_EPS)
    xh = x * r
    return xh, r, xh * gain


def _rms64_bwd(dy, xh, r, gain):
    dxh = dy * gain
    return r * (dxh - xh * jnp.mean(dxh * xh, axis=-1, keepdims=True))


_HP = LANES // DSW_DH
_DSW_W = DSW_HEADS * DSW_DH
_DSW_ROWS = 1024
_DSW_BATCH = 8


def _dsw_geometry(S, g):
    d = DSW_GROUPS[g][1]
    slab = DSW_BLK * d
    tb = max(1, min(_DSW_ROWS, S) // slab)
    return d, slab, tb, S // (tb * slab)


def _block_rows(t, r, slab, d):
    return pl.ds(t * slab + r, DSW_BLK) if d == 1 else pl.ds(t * slab + r, DSW_BLK, stride=d)


def _head(x, h):
    return x[:, h * DSW_DH:(h + 1) * DSW_DH]


def _dsw_attn_fwd(q, k, v, bias, q_gain, k_gain, prev_out, *, g, name):
    S, WT = q.shape
    B = DSW_BLK
    d, slab, tb, n_tiles = _dsw_geometry(S, g)
    rt = tb * slab
    cb = g * (_DSW_W // LANES)
    batch_res = max(1, _DSW_BATCH // tb)

    def body(q_ref, kp_ref, kc_ref, vp_ref, vc_ref, bias_ref, qg_ref, kg_ref, *rest):
        o_ref, lse_ref = rest[-2:]
        i = pl.program_id(1)
        qg, kg = qg_ref[...] * DSW_DH ** -0.5, kg_ref[...]
        col = lax.broadcasted_iota(jnp.int32, (B, 2 * B), 1)
        for r0 in range(0, d, batch_res):
            res = range(r0, min(d, r0 + batch_res))
            heads = range(_HP)
            k_raw = {(r, -1): kp_ref[_block_rows(0, r, slab, d), :] for r in res}
            v_raw = {(r, -1): vp_ref[_block_rows(0, r, slab, d), :] for r in res}
            q_raw = {}
            for r in res:
                for t in range(tb):
                    rows = _block_rows(t, r, slab, d)
                    q_raw[r, t], k_raw[r, t], v_raw[r, t] = q_ref[rows, :], kc_ref[rows, :], vc_ref[rows, :]
            kn = {key: [_rms64(_head(x, h), kg)[2].astype(BF16) for h in heads] for key, x in k_raw.items()}
            vb = {key: [_head(x, h).astype(BF16) for h in heads] for key, x in v_raw.items()}
            qn = {key: [_rms64(_head(x, h), qg)[2] for h in heads] for key, x in q_raw.items()}
            items = [(r, t, h) for r in res for t in range(tb) for h in heads]
            s = {}
            for r, t, h in items:
                sv = _dot(qn[r, t][h], jnp.concatenate([kn[r, t - 1][h], kn[r, t][h]], axis=0), "nt") + bias_ref[h]
                s[r, t, h] = jnp.where((i == 0) & (col < B), NEG_BIG, sv) if t == 0 else sv
            m = {it: jnp.max(s[it], axis=-1, keepdims=True) for it in items}
            p = {it: jnp.exp(s[it] - m[it]) for it in items}
            l = {it: jnp.sum(p[it], axis=-1, keepdims=True) for it in items}
            o = {(r, t, h): _dot(p[r, t, h], jnp.concatenate([vb[r, t - 1][h], vb[r, t][h]], axis=0))
                 for r, t, h in items}
            for r in res:
                for t in range(tb):
                    rows = _block_rows(t, r, slab, d)
                    o_ref[rows, :] = jnp.concatenate([o[r, t, h] / l[r, t, h] for h in heads], axis=1)
                    lse_ref[rows, :] = jnp.concatenate(
                        [jnp.broadcast_to(m[r, t, h] + jnp.log(l[r, t, h]), (B, DSW_DH)) for h in heads], axis=1)

    cur = pl.BlockSpec((rt, LANES), lambda hp, i: (i, cb + hp))
    prev = pl.BlockSpec((slab, LANES), lambda hp, i: (jnp.maximum(i * tb - 1, 0), cb + hp))
    vec = pl.BlockSpec((1, DSW_DH), lambda hp, i: (0, 0))
    shp = jax.ShapeDtypeStruct((S, WT), F32)
    carried = [] if prev_out is None else list(prev_out)
    n_in = 8
    return pl.pallas_call(
        body, name=name, grid=(_DSW_W // LANES, n_tiles),
        in_specs=[cur, prev, cur, prev, cur, pl.BlockSpec((_HP, B, 2 * B), lambda hp, i: (hp, 0, 0)), vec, vec]
                 + [pl.BlockSpec(memory_space=pl.ANY)] * len(carried),
        out_specs=[cur, cur], out_shape=[shp, shp],
        input_output_aliases={n_in + j: j for j in range(len(carried))},
        compiler_params=_params("parallel", "parallel"),
    )(q, k, k, v, v, bias, q_gain, k_gain, *carried)


def _dsw_merge(o_g, lse_g, *, name):
    S = o_g.shape[0]
    W, G = _DSW_W, len(DSW_GROUPS)
    tr = min(512, S)

    def body(o_ref, l_ref, out_ref, lse_ref):
        ls = [l_ref[:, g * W:(g + 1) * W] for g in range(G)]
        m = ls[0]
        for g in range(1, G):
            m = jnp.maximum(m, ls[g])
        den = jnp.zeros_like(m)
        acc = jnp.zeros_like(m)
        for g in range(G):
            wg = jnp.exp(ls[g] - m)
            den = den + wg
            acc = acc + wg * o_ref[:, g * W:(g + 1) * W]
        out_ref[...] = acc / den
        lse_ref[...] = m + jnp.log(den)

    wide = pl.BlockSpec((tr, G * W), lambda i: (i, 0))
    blk = pl.BlockSpec((tr, W), lambda i: (i, 0))
    shp = jax.ShapeDtypeStruct((S, W), F32)
    return pl.pallas_call(
        body, name=name, grid=(S // tr,), in_specs=[wide, wide], out_specs=[blk, blk],
        out_shape=[shp, shp], compiler_params=_params("parallel"),
    )(o_g, lse_g)


def _dsw_attn_bwd(q, k, v, o, lse, do, bias, q_gain, k_gain, prev_out, *, g, name):
    S, WT = q.shape
    B = DSW_BLK
    d, slab, tb, n_tiles = _dsw_geometry(S, g)
    rt = tb * slab
    cb = g * (_DSW_W // LANES)
    n_slabs = S // slab
    scale = DSW_DH ** -0.5
    batch_res = max(1, _DSW_BATCH // tb)

    def body(q_ref, qx_ref, kp_ref, kc_ref, vp_ref, vc_ref, o_ref, ox_ref, l_ref, lx_ref, do_ref, dox_ref,
             bias_ref, qg_ref, kg_ref, *rest):
        dq_ref, dk_ref, dv_ref, db_ref, dqg_ref, dkg_ref = rest[-6:]
        hp, i = pl.program_id(0), pl.program_id(1)
        qg, kg = qg_ref[...] * scale, kg_ref[...]
        col = lax.broadcasted_iota(jnp.int32, (B, 2 * B), 1)
        has_next = i < n_tiles - 1

        @pl.when(i == 0)
        def _():
            db_ref[...] = jnp.zeros_like(db_ref)

        dqg_acc = jnp.zeros((1, DSW_DH), F32)
        dkg_acc = jnp.zeros((1, DSW_DH), F32)
        heads = range(_HP)
        for r0 in range(0, d, batch_res):
            res = range(r0, min(d, r0 + batch_res))
            q_raw, k_raw, v_raw, o_raw, l_raw, do_raw = {}, {}, {}, {}, {}, {}
            for r in res:
                first_rows = _block_rows(0, r, slab, d)
                k_raw[r, -1], v_raw[r, -1] = kp_ref[first_rows, :], vp_ref[first_rows, :]
                for t in range(tb):
                    rows = _block_rows(t, r, slab, d)
                    q_raw[r, t], o_raw[r, t], l_raw[r, t], do_raw[r, t] = (
                        q_ref[rows, :], o_ref[rows, :], l_ref[rows, :], do_ref[rows, :])
                    k_raw[r, t], v_raw[r, t] = kc_ref[rows, :], vc_ref[rows, :]
                q_raw[r, tb], o_raw[r, tb], l_raw[r, tb], do_raw[r, tb] = (
                    qx_ref[first_rows, :], ox_ref[first_rows, :], lx_ref[first_rows, :], dox_ref[first_rows, :])
            kk = {key: [_rms64(_head(x, h), kg) for h in heads] for key, x in k_raw.items()}
            qq = {key: [_rms64(_head(x, h), qg) for h in heads] for key, x in q_raw.items()}
            knb = {key: [kk[key][h][2].astype(BF16) for h in heads] for key in kk}
            qnb = {key: [qq[key][h][2].astype(BF16) for h in heads] for key in qq}
            vb = {key: [_head(x, h).astype(BF16) for h in heads] for key, x in v_raw.items()}
            dob = {key: [_head(x, h).astype(BF16) for h in heads] for key, x in do_raw.items()}
            delta = {key: [jnp.sum(_head(do_raw[key], h) * _head(o_raw[key], h), axis=-1, keepdims=True)
                           for h in heads] for key in q_raw}
            full = [(r, t, h) for r in res for t in range(tb) for h in heads]
            half = [(r, tb, h) for r in res for h in heads]
            s = {}
            for r, t, h in full:
                sv = _dot(qnb[r, t][h], jnp.concatenate([knb[r, t - 1][h], knb[r, t][h]], axis=0), "nt") + bias_ref[h]
                s[r, t, h] = jnp.where((i == 0) & (col < B), NEG_BIG, sv) if t == 0 else sv
            for r, t, h in half:
                s[r, t, h] = _dot(qnb[r, t][h], knb[r, t - 1][h], "nt") + bias_ref[h, :, 0:B]
            lse_of = lambda r, t, h: l_raw[r, t][:, h * DSW_DH:h * DSW_DH + 1]
            p = {(r, t, h): jnp.exp(s[r, t, h] - lse_of(r, t, h)) for r, t, h in full}
            for r, t, h in half:
                p[r, t, h] = jnp.where(has_next, jnp.exp(s[r, t, h] - lse_of(r, t, h)), 0.0)
            dp = {(r, t, h): _dot(dob[r, t][h], jnp.concatenate([vb[r, t - 1][h], vb[r, t][h]], axis=0), "nt")
                  for r, t, h in full}
            for r, t, h in half:
                dp[r, t, h] = _dot(dob[r, t][h], vb[r, t - 1][h], "nt")
            ds = {(r, t, h): p[r, t, h] * (dp[r, t, h] - delta[r, t][h]) for r, t, h in full + half}
            pb = {it: p[it].astype(BF16) for it in ds}
            dsb = {it: ds[it].astype(BF16) for it in ds}
            for h in heads:
                tot = None
                for r in res:
                    for t in range(tb):
                        tot = ds[r, t, h] if tot is None else tot + ds[r, t, h]
                db_ref[h] += tot
            dqn = {(r, t, h): _dot(dsb[r, t, h], jnp.concatenate([knb[r, t - 1][h], knb[r, t][h]], axis=0))
                   for r, t, h in full}
            prev_half = lambda x, r, t, h: x[r, t, h][:, :B] if t < tb else x[r, t, h]
            dkn = {(r, t, h): _dot(dsb[r, t, h][:, B:], qnb[r, t][h], "tn")
                   + _dot(prev_half(dsb, r, t + 1, h), qnb[r, t + 1][h], "tn") for r, t, h in full}
            dvv = {(r, t, h): _dot(pb[r, t, h][:, B:], dob[r, t][h], "tn")
                   + _dot(prev_half(pb, r, t + 1, h), dob[r, t + 1][h], "tn") for r, t, h in full}
            for r, t, h in full:
                dqg_acc = dqg_acc + jnp.sum(dqn[r, t, h] * qq[r, t][h][0], axis=0, keepdims=True)
                dkg_acc = dkg_acc + jnp.sum(dkn[r, t, h] * kk[r, t][h][0], axis=0, keepdims=True)
            for r in res:
                for t in range(tb):
                    rows = _block_rows(t, r, slab, d)
                    dq_ref[rows, :] = jnp.concatenate(
                        [_rms64_bwd(dqn[r, t, h], qq[r, t][h][0], qq[r, t][h][1], qg) for h in heads], axis=1)
                    dk_ref[rows, :] = jnp.concatenate(
                        [_rms64_bwd(dkn[r, t, h], kk[r, t][h][0], kk[r, t][h][1], kg) for h in heads], axis=1)
                    dv_ref[rows, :] = jnp.concatenate([dvv[r, t, h] for h in heads], axis=1)

        start = (hp == 0) & (i == 0)

        @pl.when(start)
        def _():
            dqg_ref[...] = dqg_acc * scale
            dkg_ref[...] = dkg_acc

        @pl.when(jnp.logical_not(start))
        def _():
            dqg_ref[...] += dqg_acc * scale
            dkg_ref[...] += dkg_acc

    def spec(rows, pick, base):
        return pl.BlockSpec((rows, LANES), lambda hp, i: (pick(i), base + hp))

    same = lambda i: i
    before = lambda i: jnp.maximum(i * tb - 1, 0)
    after = lambda i: jnp.minimum((i + 1) * tb, n_slabs - 1)
    cur, cur1 = spec(rt, same, cb), spec(rt, same, 0)
    vec = pl.BlockSpec((1, DSW_DH), lambda hp, i: (0, 0))
    bspec = pl.BlockSpec((_HP, B, 2 * B), lambda hp, i: (hp, 0, 0))
    shp = jax.ShapeDtypeStruct((S, WT), F32)
    vshp = jax.ShapeDtypeStruct((1, DSW_DH), F32)
    carried = [] if prev_out is None else list(prev_out)
    n_in = 15
    return pl.pallas_call(
        body, name=name, grid=(_DSW_W // LANES, n_tiles),
        in_specs=[cur, spec(slab, after, cb), spec(slab, before, cb), cur, spec(slab, before, cb), cur,
                  cur1, spec(slab, after, 0), cur1, spec(slab, after, 0), cur1, spec(slab, after, 0),
                  bspec, vec, vec] + [pl.BlockSpec(memory_space=pl.ANY)] * len(carried),
        out_specs=[cur, cur, cur, bspec, vec, vec],
        out_shape=[shp, shp, shp, jax.ShapeDtypeStruct(bias.shape, F32), vshp, vshp],
        input_output_aliases={n_in + j: j for j in range(len(carried))},
        compiler_params=_params("arbitrary", "arbitrary"),
    )(q, q, k, k, v, v, o, o, lse, lse, do, do, bias, q_gain, k_gain, *carried)


def _t5_bucket(dist):
    max_exact = REL_BUCKETS // 2
    scaled = jnp.log(jnp.maximum(dist, 1).astype(F32) / max_exact) / math.log(REL_MAX_DIST / max_exact)
    large = jnp.minimum(max_exact + (scaled * (REL_BUCKETS - max_exact)).astype(jnp.int32), REL_BUCKETS - 1)
    return jnp.where(dist < max_exact, dist, large)


def _dsw_band():
    dist = (jnp.arange(DSW_BLK)[:, None] + DSW_BLK) - jnp.arange(2 * DSW_BLK)[None, :]
    return dist, (dist >= 0) & (dist <= DSW_BLK)


def _dsw_bias(rel_bias):
    dist, band = _dsw_band()
    out = []
    for g, (_, d) in enumerate(DSW_GROUPS):
        hot = jax.nn.one_hot(_t5_bucket(jnp.maximum(dist, 0) * d), REL_BUCKETS, dtype=F32)
        tab = jnp.einsum("qkb,bh->hqk", hot, rel_bias[:, g * DSW_HEADS:(g + 1) * DSW_HEADS],
                         precision=lax.Precision.HIGHEST)
        out.append(jnp.where(band[None], tab, NEG_BIG))
    return jnp.stack(out)


def _dsw_bucket_onehot():
    dist, band = _dsw_band()
    out = []
    for _, d in DSW_GROUPS:
        hot = jax.nn.one_hot(_t5_bucket(jnp.maximum(dist, 0) * d), LANES, dtype=BF16)
        out.append(jnp.where(band[..., None], hot, 0).reshape(-1, LANES))
    return jnp.stack(out)


def _exchange(send, *, gather, name):
    R, C = send.shape[-2:]

    def body(src_ref, dst_ref, send_sems, recv_sems, local_sem):
        x, y, c = lax.axis_index("x"), lax.axis_index("y"), lax.axis_index("c")
        me = 4 * x + 2 * y + c
        mine = pltpu.make_async_copy(src_ref if gather else src_ref.at[me], dst_ref.at[me], local_sem)
        mine.start()
        copies = []
        for rel in range(1, N_DEV):
            px = 1 - x if rel & 4 else x
            py = 1 - y if rel & 2 else y
            pc = 1 - c if rel & 1 else c
            peer = 4 * px + 2 * py + pc
            cp = pltpu.make_async_remote_copy(
                src_ref=src_ref if gather else src_ref.at[peer], dst_ref=dst_ref.at[me],
                send_sem=send_sems.at[rel - 1], recv_sem=recv_sems.at[rel - 1],
                device_id=(px, py, pc), device_id_type=pl.DeviceIdType.MESH)
            cp.start()
            copies.append(cp)
        for cp in copies:
            cp.wait()
        mine.wait()

    return pl.pallas_call(
        body, name=name,
        in_specs=[pl.BlockSpec(memory_space=pl.ANY)], out_specs=pl.BlockSpec(memory_space=pl.ANY),
        out_shape=jax.ShapeDtypeStruct((N_DEV, R, C), send.dtype),
        scratch_shapes=[pltpu.SemaphoreType.DMA((N_DEV - 1,)), pltpu.SemaphoreType.DMA((N_DEV - 1,)),
                        pltpu.SemaphoreType.DMA(())],
    )(send)


def _gather_two_level(send, *, name):
    R, C = send.shape

    def body(src_ref, dst_ref, send_sems, recv_sems, local_sem):
        x, y, c = lax.axis_index("x"), lax.axis_index("y"), lax.axis_index("c")
        me, sibling = (x, y, c), (x, y, 1 - c)
        chips = [(1 - x, y), (x, 1 - y), (1 - x, 1 - y)]

        def slot(px, py, pc):
            return dst_ref.at[4 * px + 2 * py + pc]

        def copy(k, block, to, src=None):
            return pltpu.make_async_remote_copy(
                src_ref=slot(*block) if src is None else src, dst_ref=slot(*block),
                send_sem=send_sems.at[k], recv_sem=recv_sems.at[k],
                device_id=to, device_id_type=pl.DeviceIdType.MESH)

        mine = pltpu.make_async_copy(src_ref, slot(*me), local_sem)
        mine.start()
        first = [copy(0, me, sibling, src=src_ref)]
        first += [copy(1 + j, me, (*chip, c), src=src_ref) for j, chip in enumerate(chips)]
        for cp in first:
            cp.start()
        passed = [copy(4 + j, (*chip, c), sibling) for j, chip in enumerate(chips)]
        for j, chip in enumerate(chips):
            copy(1 + j, (*chip, c), me).wait_recv()
            passed[j].start()
        copy(0, sibling, me).wait_recv()
        for j, chip in enumerate(chips):
            copy(4 + j, (*chip, 1 - c), me).wait_recv()
        for cp in first + passed:
            cp.wait_send()
        mine.wait()

    return pl.pallas_call(
        body, name=name,
        in_specs=[pl.BlockSpec(memory_space=pl.ANY)], out_specs=pl.BlockSpec(memory_space=pl.ANY),
        out_shape=jax.ShapeDtypeStruct((N_DEV, R, C), send.dtype),
        scratch_shapes=[pltpu.SemaphoreType.DMA((N_DEV - 1,)), pltpu.SemaphoreType.DMA((N_DEV - 1,)),
                        pltpu.SemaphoreType.DMA(())],
    )(send)


def _swap_with_sibling(send, *, name):
    def body(src_ref, dst_ref, send_sem, recv_sem):
        x, y, c = lax.axis_index("x"), lax.axis_index("y"), lax.axis_index("c")
        cp = pltpu.make_async_remote_copy(src_ref=src_ref, dst_ref=dst_ref, send_sem=send_sem, recv_sem=recv_sem,
                                          device_id=(x, y, 1 - c), device_id_type=pl.DeviceIdType.MESH)
        cp.start()
        cp.wait()

    return pl.pallas_call(
        body, name=name,
        in_specs=[pl.BlockSpec(memory_space=pl.ANY)], out_specs=pl.BlockSpec(memory_space=pl.ANY),
        out_shape=jax.ShapeDtypeStruct(send.shape, send.dtype),
        scratch_shapes=[pltpu.SemaphoreType.DMA(()), pltpu.SemaphoreType.DMA(())],
    )(send)


def _exchange_chips(send, *, name):
    n_chips, R, C = send.shape

    def body(src_ref, dst_ref, send_sems, recv_sems, local_sem):
        x, y, c = lax.axis_index("x"), lax.axis_index("y"), lax.axis_index("c")
        here = 2 * x + y
        mine = pltpu.make_async_copy(src_ref.at[here], dst_ref.at[here], local_sem)
        mine.start()
        copies = []
        for rel in range(1, n_chips):
            px = 1 - x if rel & 2 else x
            py = 1 - y if rel & 1 else y
            cp = pltpu.make_async_remote_copy(
                src_ref=src_ref.at[2 * px + py], dst_ref=dst_ref.at[here],
                send_sem=send_sems.at[rel - 1], recv_sem=recv_sems.at[rel - 1],
                device_id=(px, py, c), device_id_type=pl.DeviceIdType.MESH)
            cp.start()
            copies.append(cp)
        for cp in copies:
            cp.wait()
        mine.wait()

    return pl.pallas_call(
        body, name=name,
        in_specs=[pl.BlockSpec(memory_space=pl.ANY)], out_specs=pl.BlockSpec(memory_space=pl.ANY),
        out_shape=jax.ShapeDtypeStruct(send.shape, send.dtype),
        scratch_shapes=[pltpu.SemaphoreType.DMA((n_chips - 1,)), pltpu.SemaphoreType.DMA((n_chips - 1,)),
                        pltpu.SemaphoreType.DMA(())],
    )(send)


def _add_pair(a, b, *, name):
    n, R, C = a.shape
    tr = _tile(R, 1024)

    def body(a_ref, b_ref, o_ref):
        o_ref[...] = (a_ref[...].astype(F32) + b_ref[...].astype(F32)).astype(o_ref.dtype)

    blk = pl.BlockSpec((None, tr, C), lambda k, i: (k, i, 0))
    return pl.pallas_call(
        body, name=name, grid=(n, R // tr), in_specs=[blk, blk], out_specs=blk,
        out_shape=jax.ShapeDtypeStruct(a.shape, a.dtype), compiler_params=_params("parallel", "parallel"),
    )(a, b)


_BIG = ("w_ffn_in", "w_ffn_out", "gdn_w_in", "gdn_conv", "gdn_w_out", "dsw_w_in", "dsw_w_out")
_SHARD_AXIS = {"w_ffn_in": 2, "w_ffn_out": 1, "gdn_w_in": 2, "gdn_conv": 2, "gdn_w_out": 1, "dsw_w_in": 2,
               "dsw_w_out": 2}
_SMALL = ("b_ada", "norm_mix", "norm_ffn", "gdn_a_log", "gdn_dt_bias", "gdn_out_norm", "dsw_q_norm",
          "dsw_k_norm", "rel_bias")
_ROW_ALIGN = 16
_BIG_ALIGN = 1024


def _ceil_to(n, m):
    return -(-n // m) * m


def _seg_rows(shape):
    return _ceil_to(_ceil_to(int(np.prod(shape)), LANES) // LANES, _ROW_ALIGN)


def _pack(arrs, total_align):
    lead = arrs[0][1]
    segs = []
    for a, nlead in arrs:
        assert nlead == lead
        bshape = a.shape[:nlead]
        n = int(np.prod(a.shape[nlead:]))
        rows = _seg_rows(a.shape[nlead:])
        flat = a.reshape(bshape + (n,))
        flat = jnp.pad(flat, [(0, 0)] * nlead + [(0, rows * LANES - n)])
        segs.append(flat.reshape(bshape + (rows, LANES)))
    buf = jnp.concatenate(segs, axis=lead)
    total = _ceil_to(buf.shape[lead], total_align)
    return jnp.pad(buf, [(0, 0)] * lead + [(0, total - buf.shape[lead]), (0, 0)])


def _unpack(buf, shapes, nlead):
    out, off = [], 0
    for shp in shapes:
        n, rows = int(np.prod(shp)), _seg_rows(shp)
        seg = lax.slice_in_dim(buf, off, off + rows, axis=nlead)
        seg = seg.reshape(buf.shape[:nlead] + (rows * LANES,))[..., :n]
        out.append(seg.reshape(buf.shape[:nlead] + tuple(shp)))
        off += rows
    return out


def _to_natural(g, axis):
    n, L, r, c = g.shape
    if axis == 2:
        return jnp.transpose(g, (1, 2, 0, 3)).reshape(L, r, n * c)
    return jnp.transpose(g, (1, 0, 2, 3)).reshape(L, n * r, c)


def _to_blocked(w, axis):
    L, R, C = w.shape
    if axis == 2:
        return jnp.transpose(w.reshape(L, R, N_DEV, C // N_DEV), (2, 0, 1, 3))
    return jnp.transpose(w.reshape(L, N_DEV, R // N_DEV, C), (1, 0, 2, 3))


def _hm(a):
    lead = a.shape[:-1]
    return jnp.swapaxes(a.reshape(lead + (3, GDN_HEADS, GDN_DK)), -3, -2).reshape(lead + (3 * GDN_HEADS * GDN_DK,))


def _un_hm(a):
    lead = a.shape[:-1]
    return jnp.swapaxes(a.reshape(lead + (GDN_HEADS, 3, GDN_DK)), -3, -2).reshape(lead + (3 * GDN_HEADS * GDN_DK,))


_TILES = (1536, 1408, 1024, 768, 512, 384, 256, 128, 64, 32, 16, 8)


def _tile(n, cap):
    for t in _TILES:
        if t <= cap and n % t == 0:
            return t
    return n


def _mm_auto(a, b, mode, name, **kw):
    if mode == "tn":
        (K, M), N = a.shape, b.shape[1]
        tm, tn, tk = _tile(M, 1408), _tile(N, 1408), _tile(K, 1024)
    else:
        M, K = a.shape
        N = b.shape[1] if mode == "nn" else b.shape[0]
        tm, tn, tk = _tile(M, 512), _tile(N, 1536), _tile(K, 1408)
    return _mm(a, b, mode=mode, name=name, tm=tm, tn=tn, tk=tk, **kw)


def _row(v):
    return v.reshape(1, -1)


def _ffn_fwd(x, mod, gain, w_in, w_out, tag):
    sh, sc, gate = mod
    h = _norm_mod_fwd(x, gain, sc, sh, name=f"ffn_norm_{tag}")
    p = _mm_auto(h, w_in, "nn", f"ffn_in_{tag}", out_dtype=BF16)
    a = _swiglu_fwd(p, name=f"ffn_act_{tag}")
    y = _mm_auto(a, w_out, "nn", f"ffn_out_{tag}", out_scale=gate, resid=x)
    return y, (x, h, p, a)


def _ffn_bwd(dy, saved, mod, gain, w_in, w_out, tag):
    sh, sc, gate = mod
    x, h, p, a = saved
    gmat = _mm_auto(a, dy, "tn", f"ffn_out_g_{tag}")
    dw_out, dgate = _wout_grad(gmat, w_out, gate, name=f"ffn_out_dw_{tag}")
    da = _mm_auto(dy, w_out, "nt", f"ffn_out_dx_{tag}", a_scale=gate, out_dtype=BF16)
    dp = _swiglu_bwd(da, p, name=f"ffn_act_bwd_{tag}")
    dw_in = _mm_auto(h, dp, "tn", f"ffn_in_dw_{tag}")
    dh = _mm_auto(dp, w_in, "nt", f"ffn_in_dx_{tag}")
    dx, dsh, dsc, dgain = _norm_mod_bwd(dh, x, dy, gain, sc, name=f"ffn_norm_bwd_{tag}")
    return dx, dict(w_in=dw_in, w_out=dw_out, gain=dgain, mod=(dsh, dsc, dgate))


def _gdn_fwd(x, mod, gain, W):
    sh, sc, gate = mod
    S = x.shape[0]
    h = _norm_mod_fwd(x, gain, sc, sh, name="gdn_norm")
    pq = _mm_auto(h, W["gdn_qkv"], "nn", "gdn_in_qkv", out_dtype=BF16)
    z = _mm_auto(h, W["gdn_z"], "nn", "gdn_in_z", out_dtype=BF16)
    ab = _mm_auto(h, W["gdn_ab"], "nn", "gdn_in_ab")
    qkvn = _gdn_prep_fwd(pq, W["gdn_conv"], name="gdn_prep")
    ab4 = jnp.transpose(ab[:, :2 * GDN_HEADS]).reshape(2 * GDN_HEADS, S // GDN_CHUNK, 1, GDN_CHUNK)
    o, states, tinvs = _gdn_chunk_fwd(qkvn, ab4, W["gdn_a_log"], W["gdn_dt_bias"], name="gdn_chunk")
    o2 = _gdn_outnorm_fwd(o, z, W["gdn_out_norm"], name="gdn_outnorm")
    y = _mm_auto(o2, W["gdn_out"], "nn", "gdn_out", out_scale=gate, resid=x)
    return y, (x, h, pq, z, qkvn, ab4, o, states, tinvs, o2)


def _gdn_bwd(dy, saved, mod, gain, W):
    sh, sc, gate = mod
    x, h, pq, z, qkvn, ab4, o, states, tinvs, o2 = saved
    S = x.shape[0]
    gmat = _mm_auto(o2, dy, "tn", "gdn_out_g")
    dw_out, dgate = _wout_grad(gmat, W["gdn_out"], gate, name="gdn_out_dw")
    do2 = _mm_auto(dy, W["gdn_out"], "nt", "gdn_out_dx", a_scale=gate)
    do, dz, dout_norm = _gdn_outnorm_bwd(do2, o, z, W["gdn_out_norm"], name="gdn_outnorm_bwd")
    dqkvn, dab4, da_log, ddt_bias = _gdn_chunk_bwd(
        qkvn, ab4, W["gdn_a_log"], W["gdn_dt_bias"], states, tinvs, do, name="gdn_chunk_bwd")
    dc, dconv8 = _gdn_prep_bwd_pre(dqkvn, pq, W["gdn_conv"], name="gdn_prep_bwd")
    dpq = _gdn_conv_bwd_x(dc, W["gdn_conv"], name="gdn_conv_bwd")
    dab = jnp.transpose(dab4.reshape(2 * GDN_HEADS, S))
    dab = jnp.pad(dab, ((0, 0), (0, LANES - 2 * GDN_HEADS))).astype(BF16)
    dw_qkv = _mm_auto(h, dpq, "tn", "gdn_in_qkv_dw")
    dw_z = _mm_auto(h, dz, "tn", "gdn_in_z_dw")
    dw_ab = _mm_auto(h, dab, "tn", "gdn_in_ab_dw")
    dh = _mm_auto(dpq, W["gdn_qkv"], "nt", "gdn_in_qkv_dx")
    dh = _mm_auto(dz, W["gdn_z"], "nt", "gdn_in_z_dx", resid=dh)
    dh = _mm_auto(dab, W["gdn_ab"], "nt", "gdn_in_ab_dx", resid=dh)
    dx, dsh, dsc, dgain = _norm_mod_bwd(dh, x, dy, gain, sc, name="gdn_norm_bwd")
    dw_in = jnp.concatenate([_un_hm(dw_qkv), dw_z, dw_ab[:, :2 * GDN_HEADS]], axis=1)
    return dx, dict(gdn_w_in=dw_in, gdn_conv=_un_hm(dconv8[:GDN_CONV]), gdn_w_out=dw_out, gdn_out_norm=dout_norm,
                    gdn_a_log=da_log.reshape(1, GDN_HEADS), gdn_dt_bias=ddt_bias.reshape(1, GDN_HEADS),
                    gain=dgain, mod=(dsh, dsc, dgate))


def _dsw_fwd(x, mod, gain, W):
    sh, sc, gate = mod
    h = _norm_mod_fwd(x, gain, sc, sh, name="dsw_norm")
    q, k, v = (_mm_auto(h, W[n], "nn", f"dsw_in_{n[-1]}") for n in ("dsw_q", "dsw_k", "dsw_v"))
    outs = None
    for g in range(len(DSW_GROUPS)):
        outs = _dsw_attn_fwd(q, k, v, W["dsw_bias"][g], W["dsw_q_norm"], W["dsw_k_norm"], outs, g=g,
                             name=f"dsw_attn_{g}")
    o, lse = _dsw_merge(*outs, name="dsw_merge")
    y = _mm_auto(o, W["dsw_out"], "nn", "dsw_out", out_scale=gate, resid=x)
    return y, (x, h, q, k, v, o, lse)


def _dsw_bwd(dy, saved, mod, gain, W):
    sh, sc, gate = mod
    x, h, q, k, v, o, lse = saved
    gmat = _mm_auto(o, dy, "tn", "dsw_out_g")
    dw_out, dgate = _wout_grad(gmat, W["dsw_out"], gate, name="dsw_out_dw")
    do = _mm_auto(dy, W["dsw_out"], "nt", "dsw_out_dx", a_scale=gate)
    G = len(DSW_GROUPS)
    dqkv, dbias, dq_norm, dk_norm = None, [], 0.0, 0.0
    for g in range(G):
        *dqkv, db, dqg, dkg = _dsw_attn_bwd(q, k, v, o, lse, do, W["dsw_bias"][g], W["dsw_q_norm"],
                                            W["dsw_k_norm"], dqkv, g=g, name=f"dsw_attn_bwd_{g}")
        dbias.append(db)
        dq_norm, dk_norm = dq_norm + dqg, dk_norm + dkg
    dws, dh = [], None
    for n, d in zip(("dsw_q", "dsw_k", "dsw_v"), dqkv):
        dws.append(_mm_auto(h, d, "tn", f"dsw_in_{n[-1]}_dw"))
        dh = _mm_auto(d, W[n], "nt", f"dsw_in_{n[-1]}_dx", **({} if dh is None else {"resid": dh}))
    dx, dsh, dsc, dgain = _norm_mod_bwd(dh, x, dy, gain, sc, name="dsw_norm_bwd")
    hot = _dsw_bucket_onehot()
    drel = [_mm_auto(dbias[g].reshape(DSW_HEADS, -1), hot[g], "nn", f"dsw_rel_bias_{g}")[:, :REL_BUCKETS]
            for g in range(G)]
    return dx, dict(dsw_w_in=jnp.concatenate(dws, axis=1), dsw_w_out=dw_out, dsw_q_norm=dq_norm,
                    dsw_k_norm=dk_norm, rel_bias=jnp.transpose(jnp.concatenate(drel, axis=0)),
                    gain=dgain, mod=(dsh, dsc, dgate))


def _local_step(x, target, mod, W):
    mods = [[_row(mod[l, i]) for i in range(6)] for l in range(2)]
    nmix = [_row(W["norm_mix"][l]) for l in range(2)]
    nffn = [_row(W["norm_ffn"][l]) for l in range(2)]
    x1, s_gdn = _gdn_fwd(x, mods[0][:3], nmix[0], W)
    x2, s_f0 = _ffn_fwd(x1, mods[0][3:], nffn[0], W["w_ffn_in"][0], W["w_ffn_out"][0], "0")
    x3, s_dsw = _dsw_fwd(x2, mods[1][:3], nmix[1], W)
    x4, s_f1 = _ffn_fwd(x3, mods[1][3:], nffn[1], W["w_ffn_in"][1], W["w_ffn_out"][1], "1")
    dx4, sse = _loss_head(x4, target, name="loss_head")
    dx3, g_f1 = _ffn_bwd(dx4, s_f1, mods[1][3:], nffn[1], W["w_ffn_in"][1], W["w_ffn_out"][1], "1")
    dx2, g_dsw = _dsw_bwd(dx3, s_dsw, mods[1][:3], nmix[1], W)
    dx1, g_f0 = _ffn_bwd(dx2, s_f0, mods[0][3:], nffn[0], W["w_ffn_in"][0], W["w_ffn_out"][0], "0")
    dx0, g_gdn = _gdn_bwd(dx1, s_gdn, mods[0][:3], nmix[0], W)
    dmod = jnp.stack([jnp.concatenate(list(g_gdn["mod"]) + list(g_f0["mod"]), axis=0),
                      jnp.concatenate(list(g_dsw["mod"]) + list(g_f1["mod"]), axis=0)])
    grads = dict(
        w_ffn_in=jnp.stack([g_f0["w_in"], g_f1["w_in"]]), w_ffn_out=jnp.stack([g_f0["w_out"], g_f1["w_out"]]),
        norm_mix=jnp.concatenate([g_gdn["gain"], g_dsw["gain"]], axis=0),
        norm_ffn=jnp.concatenate([g_f0["gain"], g_f1["gain"]], axis=0),
        gdn_w_in=g_gdn["gdn_w_in"][None], gdn_conv=g_gdn["gdn_conv"][None], gdn_w_out=g_gdn["gdn_w_out"][None],
        gdn_out_norm=g_gdn["gdn_out_norm"], gdn_a_log=g_gdn["gdn_a_log"], gdn_dt_bias=g_gdn["gdn_dt_bias"],
        dsw_w_in=g_dsw["dsw_w_in"][None], dsw_w_out=g_dsw["dsw_w_out"][None],
        dsw_q_norm=g_dsw["dsw_q_norm"], dsw_k_norm=g_dsw["dsw_k_norm"], rel_bias=g_dsw["rel_bias"])
    return sse, dx0, grads, dmod


def _prepare_weights(full, small):
    gw = full["gdn_w_in"][0]
    hk3 = 3 * GDN_HEADS * GDN_DK
    di = full["dsw_w_in"][0]
    dq = di.shape[1] // 3
    return dict(
        w_ffn_in=full["w_ffn_in"], w_ffn_out=full["w_ffn_out"],
        gdn_qkv=_hm(gw[:, :hk3]), gdn_z=gw[:, hk3:hk3 + GDN_HEADS * GDN_DK],
        gdn_ab=jnp.pad(gw[:, hk3 + GDN_HEADS * GDN_DK:], ((0, 0), (0, LANES - 2 * GDN_HEADS))),
        gdn_conv=_hm(full["gdn_conv"][0]), gdn_out=full["gdn_w_out"][0],
        dsw_q=di[:, :dq], dsw_k=di[:, dq:2 * dq], dsw_v=di[:, 2 * dq:], dsw_out=full["dsw_w_out"][0],
        norm_mix=small["norm_mix"], norm_ffn=small["norm_ffn"],
        gdn_a_log=small["gdn_a_log"].reshape(GDN_HEADS, 1, 1), gdn_dt_bias=small["gdn_dt_bias"].reshape(GDN_HEADS, 1, 1),
        gdn_out_norm=small["gdn_out_norm"], dsw_q_norm=small["dsw_q_norm"], dsw_k_norm=small["dsw_k_norm"],
        dsw_bias=_dsw_bias(small["rel_bias"]))


_W_NAMES = ("w_ada", "b_ada", "norm_mix", "norm_ffn", "w_ffn_in", "w_ffn_out", "gdn_w_in", "gdn_conv",
            "gdn_a_log", "gdn_dt_bias", "gdn_out_norm", "gdn_w_out", "dsw_w_in", "dsw_q_norm", "dsw_k_norm",
            "dsw_w_out", "rel_bias")
_PAD_BATCH = 16


def _pad_rows(a, rows):
    return jnp.pad(a, ((0, rows - a.shape[0]), (0, 0)))


def kernel(x, c, w_ada, b_ada, norm_mix, norm_ffn, w_ffn_in, w_ffn_out, gdn_w_in, gdn_conv, gdn_a_log, gdn_dt_bias, gdn_out_norm, gdn_w_out, dsw_w_in, dsw_q_norm, dsw_k_norm, dsw_w_out, rel_bias, loss_target, m_w_ada, m_b_ada, m_norm_mix, m_norm_ffn, m_w_ffn_in, m_w_ffn_out, m_gdn_w_in, m_gdn_conv, m_gdn_a_log, m_gdn_dt_bias, m_gdn_out_norm, m_gdn_w_out, m_dsw_w_in, m_dsw_q_norm, m_dsw_k_norm, m_dsw_w_out, m_rel_bias, v_w_ada, v_b_ada, v_norm_mix, v_norm_ffn, v_w_ffn_in, v_w_ffn_out, v_gdn_w_in, v_gdn_conv, v_gdn_a_log, v_gdn_dt_bias, v_gdn_out_norm, v_gdn_w_out, v_dsw_w_in, v_dsw_q_norm, v_dsw_k_norm, v_dsw_w_out, v_rel_bias):
    w = dict(zip(_W_NAMES, (w_ada, b_ada, norm_mix, norm_ffn, w_ffn_in, w_ffn_out, gdn_w_in, gdn_conv, gdn_a_log,
                            gdn_dt_bias, gdn_out_norm, gdn_w_out, dsw_w_in, dsw_q_norm, dsw_k_norm, dsw_w_out,
                            rel_bias)))
    m = dict(zip(_W_NAMES, (m_w_ada, m_b_ada, m_norm_mix, m_norm_ffn, m_w_ffn_in, m_w_ffn_out, m_gdn_w_in,
                            m_gdn_conv, m_gdn_a_log, m_gdn_dt_bias, m_gdn_out_norm, m_gdn_w_out, m_dsw_w_in,
                            m_dsw_q_norm, m_dsw_k_norm, m_dsw_w_out, m_rel_bias)))
    v = dict(zip(_W_NAMES, (v_w_ada, v_b_ada, v_norm_mix, v_norm_ffn, v_w_ffn_in, v_w_ffn_out, v_gdn_w_in,
                            v_gdn_conv, v_gdn_a_log, v_gdn_dt_bias, v_gdn_out_norm, v_gdn_w_out, v_dsw_w_in,
                            v_dsw_q_norm, v_dsw_k_norm, v_dsw_w_out, v_rel_bias)))
    D = x.shape[-1]
    n_layers, _, ada_cols = w_ada.shape

    c_all = _exchange(c.reshape(D // LANES, LANES), gather=True, name="gather_cond").reshape(N_DEV, D)
    c_pad = _pad_rows(c_all, _PAD_BATCH)
    proj = [_mm(c_pad, w_ada[l], mode="nn", name=f"ada_proj_{l}", tm=_PAD_BATCH, tn=ada_cols, tk=D, a_silu=True)
            for l in range(n_layers)]
    mod_send = _pack([(jnp.stack([p[:N_DEV] for p in proj], axis=1), 1)], _ROW_ALIGN)
    mod_recv = _exchange(mod_send, gather=False, name="scatter_mod")
    mod = _unpack(mod_recv, [(n_layers, ada_cols)], 1)[0]
    mod = jnp.transpose(mod, (1, 0, 2)).reshape(n_layers, N_DEV * ada_cols) + b_ada
    mod = mod.reshape(n_layers, 6, D)

    conv_hi = gdn_conv.astype(BF16)
    conv_lo = (gdn_conv - conv_hi.astype(F32)).astype(BF16)
    w_send = _pack([(conv_hi if n == "gdn_conv" else w[n].astype(BF16), 0) for n in _BIG] + [(conv_lo, 0)],
                   _ROW_ALIGN)
    w_all = _gather_two_level(w_send, name="gather_weights")
    parts = _unpack(w_all, [w[n].shape for n in _BIG] + [gdn_conv.shape], 1)
    full = {n: _to_natural(parts[i], _SHARD_AXIS[n]) for i, n in enumerate(_BIG)}
    full["gdn_conv"] = full["gdn_conv"].astype(F32) + _to_natural(parts[-1], _SHARD_AXIS["gdn_conv"]).astype(F32)
    W = _prepare_weights(full, {n: w[n] for n in _SMALL})

    sse, grad_x, grads, dmod = _local_step(x[0], loss_target[0], mod, W)
    loss = lax.psum(0.5 * sse[0, 0] / D, ("x", "y", "c"))

    grads["b_ada"] = dmod.reshape(n_layers, 6 * D)
    my_c = lax.axis_index("c")
    big_send = _pack([(_to_blocked(grads[n], _SHARD_AXIS[n]), 1) for n in _BIG], _BIG_ALIGN).astype(BF16)
    by_core = big_send.reshape((N_DEV // 2, 2) + big_send.shape[1:])
    keep = lax.dynamic_index_in_dim(by_core, my_c, axis=1, keepdims=False)
    give = lax.dynamic_index_in_dim(by_core, 1 - my_c, axis=1, keepdims=False)
    pair = _add_pair(keep, _swap_with_sibling(give, name="swap_grads"), name="add_sibling_grads")
    g_recv = _exchange_chips(pair, name="scatter_grads")

    dmod_send = _pack([(jnp.transpose(dmod.reshape(n_layers, N_DEV, ada_cols), (1, 0, 2)), 1)], _ROW_ALIGN)
    small_send = _pack([(grads[n].reshape(w[n].shape), 0) for n in _SMALL], _ROW_ALIGN)
    s_recv = _exchange(jnp.concatenate(
        [dmod_send, jnp.broadcast_to(small_send[None], (N_DEV,) + small_send.shape)], axis=1),
        gather=False, name="scatter_small")
    dmod_rows = dmod_send.shape[1]

    out = {}
    packed = [_pack([(t[n], 0) for n in _BIG], _BIG_ALIGN) for t in (w, m, v)]
    res = _adamw(packed[0], g_recv, packed[1], packed[2], name="adamw_sharded")
    for kind, buf in zip(("grad", "delta", "new_m", "new_v"), res):
        for n, a in zip(_BIG, _unpack(buf, [w[n].shape for n in _BIG], 0)):
            out[kind, n] = a

    dmod_all = _unpack(lax.slice_in_dim(s_recv, 0, dmod_rows, axis=1), [(n_layers, ada_cols)], 1)[0]
    g_ada = jnp.stack([_mm(c_pad, _pad_rows(dmod_all[:, l], _PAD_BATCH), mode="tn", name=f"ada_dw_{l}",
                           tm=D, tn=ada_cols, tk=_PAD_BATCH, a_silu=True) for l in range(n_layers)])
    flat = lambda a: a.reshape(n_layers * D, ada_cols)
    res = _adamw(flat(w_ada), flat(g_ada)[None], flat(m_w_ada), flat(v_w_ada), name="adamw_ada")
    for kind, buf in zip(("grad", "delta", "new_m", "new_v"), res):
        out[kind, "w_ada"] = buf.reshape(w_ada.shape)

    small_parts = lax.slice_in_dim(s_recv, dmod_rows, s_recv.shape[1], axis=1)
    packed = [_pack([(t[n], 0) for n in _SMALL], _ROW_ALIGN) for t in (w, m, v)]
    res = _adamw(packed[0], small_parts, packed[1], packed[2], name="adamw_replicated")
    for kind, buf in zip(("grad", "delta", "new_m", "new_v"), res):
        for n, a in zip(_SMALL, _unpack(buf, [w[n].shape for n in _SMALL], 0)):
            out[kind, n] = a

    return (loss, grad_x[None]) + tuple(out[kind, n] for kind in ("grad", "delta", "new_m", "new_v")
                                        for n in _W_NAMES)
```

```python
import functools
import math

import numpy as np
import jax
import jax.numpy as jnp
from jax import lax
from jax.experimental import pallas as pl
from jax.experimental.pallas import tpu as pltpu

F32 = jnp.float32
BF16 = jnp.bfloat16

N_DEV = 8
RMS_EPS = 1e-6
LANES = 128
V7X_VMEM_LIMIT = 48 * 1024 * 1024

GDN_HEADS = 8
GDN_DK = 128
GDN_CHUNK = 64
GDN_CONV = 4
DSW_GROUPS = ((128, 1), (512, 4), (2048, 16))
DSW_HEADS = 8
DSW_DH = 64
DSW_BLK = 128
REL_BUCKETS = 32
REL_MAX_DIST = 2048

ADAM_LR = 0.001
ADAM_B1 = 0.9
ADAM_B2 = 0.999
ADAM_EPS = 1e-08
ADAM_WD = 0.01
ADAM_STEP = 10

NEG_BIG = -1e30


def _params(*sem):
    return pltpu.CompilerParams(dimension_semantics=sem, vmem_limit_bytes=V7X_VMEM_LIMIT)


def _sigmoid(x):
    return 1.0 / (1.0 + jnp.exp(-x))


def _silu(x):
    return x * _sigmoid(x)


_DOT_DIMS = {
    "nn": (((1,), (0,)), ((), ())),
    "nt": (((1,), (1,)), ((), ())),
    "tn": (((0,), (0,)), ((), ())),
}


def _mm(a, b, *, mode, name, tm, tn, tk, out_dtype=F32, a_scale=None, out_scale=None, resid=None, a_silu=False,
        b_k_off=0):
    if mode == "nn":
        (M, K), N = a.shape, b.shape[1]
    elif mode == "nt":
        (M, K), N = a.shape, b.shape[0]
    else:
        (K, M), N = a.shape, b.shape[1]
    tm, tn, tk = min(tm, M), min(tn, N), min(tk, K)
    assert M % tm == 0 and N % tn == 0 and K % tk == 0 and b_k_off % tk == 0, (name, M, N, K, tm, tn, tk)
    assert b_k_off == 0 or mode == "nt", name
    nk = K // tk

    def body(*refs):
        refs = list(refs)
        a_ref, b_ref = refs.pop(0), refs.pop(0)
        as_ref = refs.pop(0) if a_scale is not None else None
        os_ref = refs.pop(0) if out_scale is not None else None
        r_ref = refs.pop(0) if resid is not None else None
        o_ref = refs.pop(0)
        acc_ref = refs.pop(0) if nk > 1 else None

        av = a_ref[...]
        if a_silu:
            av = _silu(av.astype(F32))
        if as_ref is not None:
            av = av.astype(F32) * as_ref[...]
        part = lax.dot_general(av.astype(BF16), b_ref[...].astype(BF16), _DOT_DIMS[mode],
                               preferred_element_type=F32)

        def finish(r):
            if os_ref is not None:
                r = r * os_ref[...]
            if r_ref is not None:
                r = r + r_ref[...].astype(F32)
            o_ref[...] = r.astype(out_dtype)

        if nk == 1:
            finish(part)
        else:
            k = pl.program_id(2)

            @pl.when(k == 0)
            def _():
                acc_ref[...] = part

            @pl.when(k > 0)
            def _():
                acc_ref[...] += part

            @pl.when(k == nk - 1)
            def _():
                finish(acc_ref[...])

    if mode == "nn":
        a_spec = pl.BlockSpec((tm, tk), lambda i, j, k: (i, k))
        b_spec = pl.BlockSpec((tk, tn), lambda i, j, k: (k, j))
        as_spec = pl.BlockSpec((1, tk), lambda i, j, k: (0, k))
    elif mode == "nt":
        a_spec = pl.BlockSpec((tm, tk), lambda i, j, k: (i, k))
        b_spec = pl.BlockSpec((tn, tk), lambda i, j, k: (j, k + b_k_off // tk))
        as_spec = pl.BlockSpec((1, tk), lambda i, j, k: (0, k))
    else:
        a_spec = pl.BlockSpec((tk, tm), lambda i, j, k: (k, i))
        b_spec = pl.BlockSpec((tk, tn), lambda i, j, k: (k, j))
        as_spec = None
    in_specs, args = [a_spec, b_spec], [a, b]
    if a_scale is not None:
        in_specs.append(as_spec)
        args.append(a_scale)
    if out_scale is not None:
        in_specs.append(pl.BlockSpec((1, tn), lambda i, j, k: (0, j)))
        args.append(out_scale)
    if resid is not None:
        in_specs.append(pl.BlockSpec((tm, tn), lambda i, j, k: (i, j)))
        args.append(resid)
    return pl.pallas_call(
        body, name=name, grid=(M // tm, N // tn, nk),
        in_specs=in_specs, out_specs=pl.BlockSpec((tm, tn), lambda i, j, k: (i, j)),
        out_shape=jax.ShapeDtypeStruct((M, N), out_dtype),
        scratch_shapes=[pltpu.VMEM((tm, tn), F32)] if nk > 1 else [],
        compiler_params=_params("parallel", "parallel", "arbitrary"),
    )(*args)


def _norm_mod_fwd(x, gain, sc, sh, *, name):
    S, D = x.shape
    tr = min(512, S)

    def body(x_ref, g_ref, sc_ref, sh_ref, h_ref):
        xv = x_ref[...]
        r = lax.rsqrt(jnp.mean(xv * xv, axis=-1, keepdims=True) + RMS_EPS)
        h_ref[...] = ((xv * r) * g_ref[...] * (1.0 + sc_ref[...]) + sh_ref[...]).astype(BF16)

    row = pl.BlockSpec((tr, D), lambda i: (i, 0))
    vec = pl.BlockSpec((1, D), lambda i: (0, 0))
    return pl.pallas_call(
        body, name=name, grid=(S // tr,), in_specs=[row, vec, vec, vec], out_specs=row,
        out_shape=jax.ShapeDtypeStruct((S, D), BF16), compiler_params=_params("parallel"),
    )(x, gain, sc, sh)


def _norm_mod_bwd(dh, x, dx_res, gain, sc, *, name):
    S, D = x.shape
    tr = min(256, S)
    n_steps = S // tr

    def body(dh_ref, x_ref, dxr_ref, g_ref, sc_ref, dx_ref, dsh_ref, dsc_ref, dgain_ref, acc_sh, acc_a):
        i = pl.program_id(0)
        xv = x_ref[...]
        r = lax.rsqrt(jnp.mean(xv * xv, axis=-1, keepdims=True) + RMS_EPS)
        n = xv * r
        dhv = dh_ref[...].astype(F32)
        dn = dhv * (g_ref[...] * (1.0 + sc_ref[...]))
        dx_ref[...] = dxr_ref[...] + r * (dn - n * jnp.mean(dn * n, axis=-1, keepdims=True))
        p_sh = jnp.sum(dhv, axis=0, keepdims=True)
        p_a = jnp.sum(dhv * n, axis=0, keepdims=True)

        @pl.when(i == 0)
        def _():
            acc_sh[...] = p_sh
            acc_a[...] = p_a

        @pl.when(i > 0)
        def _():
            acc_sh[...] += p_sh
            acc_a[...] += p_a

        @pl.when(i == n_steps - 1)
        def _():
            dsh_ref[...] = acc_sh[...]
            dsc_ref[...] = acc_a[...] * g_ref[...]
            dgain_ref[...] = acc_a[...] * (1.0 + sc_ref[...])

    row = pl.BlockSpec((tr, D), lambda i: (i, 0))
    vec = pl.BlockSpec((1, D), lambda i: (0, 0))
    vshape = jax.ShapeDtypeStruct((1, D), F32)
    return pl.pallas_call(
        body, name=name, grid=(n_steps,), in_specs=[row, row, row, vec, vec],
        out_specs=[row, vec, vec, vec],
        out_shape=[jax.ShapeDtypeStruct((S, D), F32), vshape, vshape, vshape],
        scratch_shapes=[pltpu.VMEM((1, D), F32), pltpu.VMEM((1, D), F32)],
        compiler_params=_params("arbitrary"),
    )(dh, x, dx_res, gain, sc)


def _wout_grad(gmat, w, gate, *, name):
    K, D = w.shape
    tr = min(256, K)
    n_steps = K // tr

    def body(g_ref, w_ref, gate_ref, dw_ref, dgate_ref, acc):
        i = pl.program_id(0)
        gv = g_ref[...]
        dw_ref[...] = (gv * gate_ref[...]).astype(BF16)
        part = jnp.sum(gv * w_ref[...], axis=0, keepdims=True)

        @pl.when(i == 0)
        def _():
            acc[...] = part

        @pl.when(i > 0)
        def _():
            acc[...] += part

        @pl.when(i == n_steps - 1)
        def _():
            dgate_ref[...] = acc[...]

    row = pl.BlockSpec((tr, D), lambda i: (i, 0))
    vec = pl.BlockSpec((1, D), lambda i: (0, 0))
    return pl.pallas_call(
        body, name=name, grid=(n_steps,), in_specs=[row, row, vec], out_specs=[row, vec],
        out_shape=[jax.ShapeDtypeStruct((K, D), BF16), jax.ShapeDtypeStruct((1, D), F32)],
        scratch_shapes=[pltpu.VMEM((1, D), F32)], compiler_params=_params("arbitrary"),
    )(gmat, w, gate)


def _loss_head(y, target, *, name):
    S, D = y.shape
    tr = min(512, S)
    n_steps = S // tr

    def body(y_ref, t_ref, dy_ref, sse_ref, acc):
        i = pl.program_id(0)
        e = y_ref[...] - t_ref[...]
        dy_ref[...] = e * (1.0 / D)
        part = jnp.sum(e * e, axis=0, keepdims=True)

        @pl.when(i == 0)
        def _():
            acc[...] = part

        @pl.when(i > 0)
        def _():
            acc[...] += part

        @pl.when(i == n_steps - 1)
        def _():
            sse_ref[...] = jnp.sum(acc[...], axis=1, keepdims=True)

    row = pl.BlockSpec((tr, D), lambda i: (i, 0))
    return pl.pallas_call(
        body, name=name, grid=(n_steps,), in_specs=[row, row],
        out_specs=[row, pl.BlockSpec((1, 1), lambda i: (0, 0))],
        out_shape=[jax.ShapeDtypeStruct((S, D), F32), jax.ShapeDtypeStruct((1, 1), F32)],
        scratch_shapes=[pltpu.VMEM((1, D), F32)], compiler_params=_params("arbitrary"),
    )(y, target)


def _adamw(w, g_parts, m, v, *, name):
    R, C = w.shape
    P = g_parts.shape[0]
    tr = _tile(R, max(8, 1024 * LANES // C))
    c1 = 1.0 / (1.0 - ADAM_B1 ** ADAM_STEP)
    c2 = 1.0 / (1.0 - ADAM_B2 ** ADAM_STEP)

    def body(w_ref, g_ref, m_ref, v_ref, go_ref, d_ref, mo_ref, vo_ref):
        g = g_ref[0].astype(F32)
        for q in range(1, P):
            g = g + g_ref[q].astype(F32)
        mn = ADAM_B1 * m_ref[...] + (1.0 - ADAM_B1) * g
        vn = ADAM_B2 * v_ref[...] + (1.0 - ADAM_B2) * (g * g)
        go_ref[...] = g
        mo_ref[...] = mn
        vo_ref[...] = vn
        d_ref[...] = -ADAM_LR * ((mn * c1) / (jnp.sqrt(vn * c2) + ADAM_EPS) + ADAM_WD * w_ref[...])

    row = pl.BlockSpec((tr, C), lambda i: (i, 0))
    shp = jax.ShapeDtypeStruct((R, C), F32)
    return pl.pallas_call(
        body, name=name, grid=(R // tr,),
        in_specs=[row, pl.BlockSpec((P, tr, C), lambda i: (0, i, 0)), row, row],
        out_specs=[row, row, row, row], out_shape=[shp, shp, shp, shp],
        compiler_params=_params("parallel"),
    )(w, g_parts, m, v)


_HALO = 16


def _conv_taps(buf, w_ref, rows, cols):
    acc = None
    for j in range(GDN_CONV):
        term = buf[pl.ds(_HALO - (GDN_CONV - 1) + j, rows), cols] * w_ref[j:j + 1, cols]
        acc = term if acc is None else acc + term
    return acc


def _fill_conv_buf(buf, halo_ref, x_ref, rows, first):
    buf[0:_HALO, :] = jnp.where(first, 0.0, halo_ref[...].astype(F32))
    buf[_HALO:_HALO + rows, :] = x_ref[...].astype(F32)


_HM = 3 * GDN_DK
_GDN_ROWS = 256
_PREP_HEADS = 4


def _l2n(seg):
    return lax.rsqrt(jnp.sum(seg * seg, axis=-1, keepdims=True) + RMS_EPS)


def _head_cols(hh):
    return slice(hh * _HM, (hh + 1) * _HM)


def _gdn_prep_fwd(x, conv_w, *, name):
    S, C3 = x.shape
    CB = _PREP_HEADS * _HM
    RB = min(256, S)

    def body(x_ref, halo_ref, w_ref, o_ref, buf):
        i = pl.program_id(0)
        _fill_conv_buf(buf, halo_ref, x_ref, RB, i == 0)
        for hh in range(_PREP_HEADS):
            c0 = hh * _HM
            y = _silu(_conv_taps(buf, w_ref, RB, _head_cols(hh)))
            q, k = y[:, :GDN_DK], y[:, GDN_DK:2 * GDN_DK]
            o_ref[:, c0:c0 + GDN_DK] = q * (_l2n(q) * GDN_DK ** -0.5)
            o_ref[:, c0 + GDN_DK:c0 + 2 * GDN_DK] = k * _l2n(k)
            o_ref[:, c0 + 2 * GDN_DK:c0 + _HM] = y[:, 2 * GDN_DK:]

    hb = RB // _HALO
    return pl.pallas_call(
        body, name=name, grid=(S // RB, C3 // CB),
        in_specs=[pl.BlockSpec((RB, CB), lambda i, j: (i, j)),
                  pl.BlockSpec((_HALO, CB), lambda i, j: (jnp.maximum(i * hb - 1, 0), j)),
                  pl.BlockSpec((GDN_CONV, CB), lambda i, j: (0, j))],
        out_specs=pl.BlockSpec((RB, CB), lambda i, j: (i, j)),
        out_shape=jax.ShapeDtypeStruct((S, C3), F32),
        scratch_shapes=[pltpu.VMEM((RB + _HALO, CB), F32)],
        compiler_params=_params("parallel", "parallel"),
    )(x, x, conv_w)


def _gdn_prep_bwd_pre(dn, x, conv_w, *, name):
    S, C3 = x.shape
    CB = _PREP_HEADS * _HM
    RB = min(256, S)
    n_steps = S // RB

    def body(dn_ref, x_ref, halo_ref, w_ref, dc_ref, dw_ref, buf):
        i = pl.program_id(1)
        _fill_conv_buf(buf, halo_ref, x_ref, RB, i == 0)
        head_parts = []
        for hh in range(_PREP_HEADS):
            c0, cols = hh * _HM, _head_cols(hh)
            acc = _conv_taps(buf, w_ref, RB, cols)
            sg = _sigmoid(acc)
            y = acc * sg
            dsilu = sg * (1.0 + acc * (1.0 - sg))
            for part, scale in ((0, GDN_DK ** -0.5), (1, 1.0)):
                sl = slice(part * GDN_DK, (part + 1) * GDN_DK)
                seg = y[:, sl]
                r = _l2n(seg)
                n = seg * r
                d = dn_ref[:, c0 + part * GDN_DK:c0 + (part + 1) * GDN_DK] * scale
                dc_ref[:, c0 + part * GDN_DK:c0 + (part + 1) * GDN_DK] = (
                    r * (d - n * jnp.sum(d * n, axis=-1, keepdims=True)) * dsilu[:, sl])
            dc_ref[:, c0 + 2 * GDN_DK:c0 + _HM] = dn_ref[:, c0 + 2 * GDN_DK:c0 + _HM] * dsilu[:, 2 * GDN_DK:]
            dc = dc_ref[:, cols]
            taps = [jnp.sum(dc * buf[pl.ds(_HALO - (GDN_CONV - 1) + t, RB), cols], axis=0, keepdims=True)
                    for t in range(GDN_CONV)]
            head_parts.append(jnp.concatenate(taps + [jnp.zeros((8 - GDN_CONV, _HM), F32)], axis=0))
        part = jnp.concatenate(head_parts, axis=1)

        @pl.when(i == 0)
        def _():
            dw_ref[...] = part

        @pl.when(i > 0)
        def _():
            dw_ref[...] += part

    hb = RB // _HALO
    return pl.pallas_call(
        body, name=name, grid=(C3 // CB, n_steps),
        in_specs=[pl.BlockSpec((RB, CB), lambda j, i: (i, j)),
                  pl.BlockSpec((RB, CB), lambda j, i: (i, j)),
                  pl.BlockSpec((_HALO, CB), lambda j, i: (jnp.maximum(i * hb - 1, 0), j)),
                  pl.BlockSpec((GDN_CONV, CB), lambda j, i: (0, j))],
        out_specs=[pl.BlockSpec((RB, CB), lambda j, i: (i, j)),
                   pl.BlockSpec((8, CB), lambda j, i: (0, j))],
        out_shape=[jax.ShapeDtypeStruct((S, C3), F32), jax.ShapeDtypeStruct((8, C3), F32)],
        scratch_shapes=[pltpu.VMEM((RB + _HALO, CB), F32)],
        compiler_params=_params("parallel", "arbitrary"),
    )(dn, x, x, conv_w)


def _gdn_conv_bwd_x(dc, conv_w, *, name):
    S, C3 = dc.shape
    CB = _PREP_HEADS * _HM
    RB = min(256, S)
    n_steps = S // RB

    def body(dc_ref, halo_ref, w_ref, dx_ref, buf):
        i = pl.program_id(0)
        buf[0:RB, :] = dc_ref[...]
        buf[RB:RB + _HALO, :] = jnp.where(i == n_steps - 1, 0.0, halo_ref[...])
        for hh in range(_PREP_HEADS):
            cols = _head_cols(hh)
            acc = None
            for j in range(GDN_CONV):
                term = buf[pl.ds(GDN_CONV - 1 - j, RB), cols] * w_ref[j:j + 1, cols]
                acc = term if acc is None else acc + term
            dx_ref[:, cols] = acc.astype(BF16)

    hb = RB // _HALO
    last = S // _HALO - 1
    return pl.pallas_call(
        body, name=name, grid=(n_steps, C3 // CB),
        in_specs=[pl.BlockSpec((RB, CB), lambda i, j: (i, j)),
                  pl.BlockSpec((_HALO, CB), lambda i, j: (jnp.minimum((i + 1) * hb, last), j)),
                  pl.BlockSpec((GDN_CONV, CB), lambda i, j: (0, j))],
        out_specs=pl.BlockSpec((RB, CB), lambda i, j: (i, j)),
        out_shape=jax.ShapeDtypeStruct((S, C3), BF16),
        scratch_shapes=[pltpu.VMEM((RB + _HALO, CB), F32)],
        compiler_params=_params("parallel", "parallel"),
    )(dc, dc, conv_w)


def _split_bf16(a):
    hi = a.astype(BF16)
    return hi, (a - hi.astype(F32)).astype(BF16)


def _dot(a, b, dims="nn", exact=False):
    def dot(p, q):
        return lax.dot_general(p, q, _DOT_DIMS[dims], preferred_element_type=F32)

    if exact:
        (ah, al), (bh, bl) = _split_bf16(a), _split_bf16(b)
        return dot(ah, bh) + (dot(ah, bl) + dot(al, bh))
    return dot(a.astype(BF16), b.astype(BF16))


def _softplus(x):
    return jnp.maximum(x, 0.0) + jnp.log(1.0 + jnp.exp(-jnp.abs(x)))


def _to_col(row, eye):
    return jnp.sum(jnp.where(eye, row, 0.0), axis=1, keepdims=True)


def _to_row(col, eye):
    return jnp.sum(jnp.where(eye, col, 0.0), axis=0, keepdims=True)


def _unit_lower_inverse(low, ri, ci):
    n = range(len(low))
    C = low[0].shape[0]
    eye = jnp.where(ri == ci, 1.0, 0.0)
    pair = (ri >> 1) == (ci >> 1)
    x = [eye - jnp.where(pair, low[j], 0.0) for j in n]
    m, sh = 2, 1
    while m < C:
        join = ((ri >> (sh + 1)) == (ci >> (sh + 1))) & (((ri >> sh) & 1) == 1) & (((ci >> sh) & 1) == 0)
        y = [_dot(x[j], jnp.where(join, low[j], 0.0)) for j in n]
        x = [x[j] - _dot(y[j], x[j]) for j in n]
        m, sh = 2 * m, sh + 1
    lx = [_dot(low[j], x[j], exact=True) for j in n]
    corr = [_dot(x[j], eye - x[j] - lx[j]) for j in n]
    return [x[j] + corr[j] for j in n]


def _gdn_local_batch(qkv, g_row, beta_row, ri, ci):
    n = range(len(qkv))
    eye, tril, strict = ri == ci, ri >= ci, ri > ci
    q = [qkv[j][:, :GDN_DK] for j in n]
    k = [qkv[j][:, GDN_DK:2 * GDN_DK] for j in n]
    v = [qkv[j][:, 2 * GDN_DK:] for j in n]
    g_col = [_to_col(g_row[j], eye) for j in n]
    beta_col = [_to_col(beta_row[j], eye) for j in n]
    gc_col = [jnp.sum(jnp.where(tril, g_row[j], 0.0), axis=1, keepdims=True) for j in n]
    gc_row = [jnp.sum(jnp.where(ri <= ci, g_col[j], 0.0), axis=0, keepdims=True) for j in n]
    g_last = [jnp.sum(g_row[j], axis=1, keepdims=True) for j in n]
    decay = [jnp.where(tril, jnp.exp(jnp.minimum(gc_col[j] - gc_row[j], 0.0)), 0.0) for j in n]
    e_col = [jnp.exp(gc_col[j]) for j in n]
    f_col = [jnp.exp(g_last[j] - gc_col[j]) for j in n]
    e_last = [jnp.exp(g_last[j]) for j in n]
    kb = [k[j] * beta_col[j] for j in n]
    vb = [v[j] * beta_col[j] for j in n]
    kk = [_dot(kb[j], k[j], "nt") for j in n]
    qk = [_dot(q[j], k[j], "nt") for j in n]
    low = [jnp.where(strict, kk[j] * decay[j], 0.0) for j in n]
    att = [qk[j] * decay[j] for j in n]
    return dict(q=q, k=k, v=v, beta_col=beta_col, decay=decay, e_col=e_col, f_col=f_col, e_last=e_last,
                kb=kb, vb=vb, low=low, att=att, eye=eye, strict=strict, tril=tril)


def _chunk_iotas():
    C = GDN_CHUNK
    return lax.broadcasted_iota(jnp.int32, (C, C), 0), lax.broadcasted_iota(jnp.int32, (C, C), 1)


def _gdn_chunk_fwd(qkv, ab, a_log, dt_bias, *, name):
    S = qkv.shape[0]
    H, C, DK = GDN_HEADS, GDN_CHUNK, GDN_DK
    RB = min(_GDN_ROWS, S)
    NCB, NB, NC = RB // C, S // RB, S // C
    heads = range(H)

    def body(qkv_ref, ab_ref, alog_ref, dtb_ref, o_ref, st_ref, t_ref, state, u_s, w_s, qe_s, kf_s, att_s):
        nb = pl.program_id(0)

        @pl.when(nb == 0)
        def _():
            state[...] = jnp.zeros_like(state)

        ri, ci = _chunk_iotas()
        neg_a = [-jnp.exp(alog_ref[h]) for h in heads]
        e_last = []
        for c in range(NCB):
            rows = pl.ds(c * C, C)
            g_row = [neg_a[h] * _softplus(ab_ref[h, c] + dtb_ref[h]) for h in heads]
            beta_row = [_sigmoid(ab_ref[H + h, c]) for h in heads]
            L = _gdn_local_batch([qkv_ref[rows, h * _HM:(h + 1) * _HM] for h in heads], g_row, beta_row, ri, ci)
            tinv = _unit_lower_inverse(L["low"], ri, ci)
            u = [_dot(tinv[h], L["vb"][h], exact=True) for h in heads]
            w = [_dot(tinv[h], L["kb"][h] * L["e_col"][h], exact=True) for h in heads]
            for h in heads:
                t_ref[h, c] = tinv[h]
                u_s[c, h] = u[h]
                w_s[c, h] = w[h].astype(BF16)
                qe_s[c, h] = (L["q"][h] * L["e_col"][h]).astype(BF16)
                kf_s[c, h] = (L["k"][h] * L["f_col"][h]).astype(BF16)
                att_s[c, h] = L["att"][h].astype(BF16)
            e_last.append(L["e_last"])
        st = [state[h] for h in heads]
        for c in range(NCB):
            rows = pl.ds(c * C, C)
            stb = [st[h].astype(BF16) for h in heads]
            vn = [u_s[c, h] - _dot(w_s[c, h], stb[h]) for h in heads]
            vnb = [vn[h].astype(BF16) for h in heads]
            out = [_dot(qe_s[c, h], stb[h]) + _dot(att_s[c, h], vnb[h]) for h in heads]
            new = [st[h] * e_last[c][h] + _dot(kf_s[c, h], vnb[h], "tn") for h in heads]
            for h in heads:
                o_ref[rows, h * DK:(h + 1) * DK] = out[h]
                st_ref[h, c] = st[h]
            st = new
        for h in heads:
            state[h] = st[h]

    return pl.pallas_call(
        body, name=name, grid=(NB,),
        in_specs=[pl.BlockSpec((RB, H * _HM), lambda n: (n, 0)),
                  pl.BlockSpec((2 * H, NCB, 1, C), lambda n: (0, n, 0, 0)),
                  pl.BlockSpec((H, 1, 1), lambda n: (0, 0, 0)),
                  pl.BlockSpec((H, 1, 1), lambda n: (0, 0, 0))],
        out_specs=[pl.BlockSpec((RB, H * DK), lambda n: (n, 0)),
                   pl.BlockSpec((H, NCB, DK, DK), lambda n: (0, n, 0, 0)),
                   pl.BlockSpec((H, NCB, C, C), lambda n: (0, n, 0, 0))],
        out_shape=[jax.ShapeDtypeStruct((S, H * DK), F32),
                   jax.ShapeDtypeStruct((H, NC, DK, DK), F32),
                   jax.ShapeDtypeStruct((H, NC, C, C), F32)],
        scratch_shapes=[pltpu.VMEM((H, DK, DK), F32), pltpu.VMEM((NCB, H, C, DK), F32),
                        pltpu.VMEM((NCB, H, C, DK), BF16), pltpu.VMEM((NCB, H, C, DK), BF16),
                        pltpu.VMEM((NCB, H, C, DK), BF16), pltpu.VMEM((NCB, H, C, C), BF16)],
        compiler_params=_params("arbitrary"),
    )(qkv, ab, a_log, dt_bias)


def _gdn_chunk_bwd(qkv, ab, a_log, dt_bias, states, tinvs, do, *, name):
    S = qkv.shape[0]
    H, C, DK = GDN_HEADS, GDN_CHUNK, GDN_DK
    RB = min(_GDN_ROWS, S)
    NCB, NB, NC = RB // C, S // RB, S // C
    heads = range(H)

    def body(qkv_ref, ab_ref, alog_ref, dtb_ref, st_ref, t_ref, do_ref,
             dqkv_ref, dab_ref, dalog_ref, ddtb_ref, dstate, w_s, vn_s, qe_s, kf_s, att_s, dvn_s, dkf_s):
        nb = pl.program_id(0)

        @pl.when(nb == 0)
        def _():
            dstate[...] = jnp.zeros_like(dstate)
            dalog_ref[...] = jnp.zeros_like(dalog_ref)
            ddtb_ref[...] = jnp.zeros_like(ddtb_ref)

        ri, ci = _chunk_iotas()
        neg_a = [-jnp.exp(alog_ref[h]) for h in heads]

        def local(c):
            rows = pl.ds(c * C, C)
            a_pre = [ab_ref[h, c] + dtb_ref[h] for h in heads]
            g_row = [neg_a[h] * _softplus(a_pre[h]) for h in heads]
            beta_row = [_sigmoid(ab_ref[H + h, c]) for h in heads]
            L = _gdn_local_batch([qkv_ref[rows, h * _HM:(h + 1) * _HM] for h in heads], g_row, beta_row, ri, ci)
            return L, a_pre, g_row, beta_row

        e_last = [None] * NCB
        for c in range(NCB):
            L, _, _, _ = local(c)
            kbe = [L["kb"][h] * L["e_col"][h] for h in heads]
            u = [_dot(t_ref[h, c], L["vb"][h], exact=True) for h in heads]
            w = [_dot(t_ref[h, c], kbe[h], exact=True) for h in heads]
            vn = [u[h] - _dot(w[h], st_ref[h, c]) for h in heads]
            for h in heads:
                w_s[c, h] = w[h].astype(BF16)
                vn_s[c, h] = vn[h].astype(BF16)
                qe_s[c, h] = (L["q"][h] * L["e_col"][h]).astype(BF16)
                kf_s[c, h] = (L["k"][h] * L["f_col"][h]).astype(BF16)
                att_s[c, h] = L["att"][h].astype(BF16)
            e_last[c] = L["e_last"]

        dst = [dstate[h] for h in heads]
        de_last = [None] * NCB
        for c in reversed(range(NCB)):
            rows = pl.ds(c * C, C)
            dob = [do_ref[rows, h * DK:(h + 1) * DK].astype(BF16) for h in heads]
            dstb = [dst[h].astype(BF16) for h in heads]
            dvn = [_dot(att_s[c, h], dob[h], "tn") + _dot(kf_s[c, h], dstb[h]) for h in heads]
            dkf = [_dot(vn_s[c, h], dstb[h], "nt") for h in heads]
            de_last[c] = [jnp.sum(jnp.sum(dst[h] * st_ref[h, c], axis=1, keepdims=True), axis=0, keepdims=True)
                          for h in heads]
            new = [dst[h] * e_last[c][h] + _dot(qe_s[c, h], dob[h], "tn")
                   - _dot(w_s[c, h], dvn[h].astype(BF16), "tn") for h in heads]
            for h in heads:
                dvn_s[c, h] = dvn[h]
                dkf_s[c, h] = dkf[h]
            dst = new
        for h in heads:
            dstate[h] = dst[h]

        for c in range(NCB):
            rows = pl.ds(c * C, C)
            L, a_pre, g_row, beta_row = local(c)
            q, k, v, kb, vb = L["q"], L["k"], L["v"], L["kb"], L["vb"]
            e_col, f_col, decay, beta_col = L["e_col"], L["f_col"], L["decay"], L["beta_col"]
            eye, strict, tril = L["eye"], L["strict"], L["tril"]
            tinv = [t_ref[h, c] for h in heads]
            stb = [st_ref[h, c].astype(BF16) for h in heads]
            dov = [do_ref[rows, h * DK:(h + 1) * DK] for h in heads]
            dvn = [dvn_s[c, h] for h in heads]
            dkf = [dkf_s[c, h] for h in heads]
            kbe = [kb[h] * e_col[h] for h in heads]
            datt = [jnp.where(tril, _dot(dov[h], vn_s[c, h], "nt"), 0.0) for h in heads]
            dqe = [_dot(dov[h], stb[h], "nt") for h in heads]
            dw = [-_dot(dvn[h], stb[h], "nt") for h in heads]
            dt = [_dot(dvn[h], vb[h], "nt") + _dot(dw[h], kbe[h], "nt") for h in heads]
            dvb = [_dot(tinv[h], dvn[h], "tn", exact=True) for h in heads]
            dkbe = [_dot(tinv[h], dw[h], "tn", exact=True) for h in heads]
            tdt = [_dot(tinv[h], dt[h], "tn", exact=True) for h in heads]
            dlow = [-jnp.where(strict, _dot(tdt[h], tinv[h], "nt", exact=True), 0.0) for h in heads]
            dkk = [dlow[h] * decay[h] for h in heads]
            dqk = [datt[h] * decay[h] for h in heads]
            dkb = [_dot(dkk[h], k[h]) + dkbe[h] * e_col[h] for h in heads]
            dk = [_dot(dkk[h], kb[h], "tn") + _dot(dqk[h], q[h], "tn") + dkf[h] * f_col[h] + dkb[h] * beta_col[h]
                  for h in heads]
            dq = [_dot(dqk[h], k[h]) + dqe[h] * e_col[h] for h in heads]
            for h in heads:
                dqkv_ref[rows, h * _HM:h * _HM + DK] = dq[h]
                dqkv_ref[rows, h * _HM + DK:h * _HM + 2 * DK] = dk[h]
                dqkv_ref[rows, h * _HM + 2 * DK:(h + 1) * _HM] = dvb[h] * beta_col[h]

            dbeta_col = [jnp.sum(k[h] * dkb[h] + v[h] * dvb[h], axis=1, keepdims=True) for h in heads]
            pmat = [dlow[h] * L["low"][h] + datt[h] * L["att"][h] for h in heads]
            df_col = [jnp.sum(k[h] * dkf[h], axis=1, keepdims=True) * f_col[h] for h in heads]
            dgc_col = [jnp.sum(pmat[h], axis=1, keepdims=True)
                       + jnp.sum(q[h] * dqe[h] + kb[h] * dkbe[h], axis=1, keepdims=True) * e_col[h] - df_col[h]
                       for h in heads]
            dgc_row = [_to_row(dgc_col[h], eye) - jnp.sum(pmat[h], axis=0, keepdims=True) for h in heads]
            dg_last = [jnp.sum(df_col[h], axis=0, keepdims=True) + de_last[c][h] * L["e_last"][h] for h in heads]
            dgc_c = [_to_col(dgc_row[h], eye) for h in heads]
            dg_row = [jnp.sum(jnp.where(ri >= ci, dgc_c[h], 0.0), axis=0, keepdims=True) + dg_last[h] for h in heads]
            dbeta_row = [_to_row(dbeta_col[h], eye) for h in heads]
            for h in heads:
                da_row = dg_row[h] * neg_a[h] * _sigmoid(a_pre[h])
                dab_ref[h, c] = da_row
                dab_ref[H + h, c] = dbeta_row[h] * beta_row[h] * (1.0 - beta_row[h])
                dalog_ref[h] += jnp.sum(dg_row[h] * g_row[h], axis=1, keepdims=True)
                ddtb_ref[h] += jnp.sum(da_row, axis=1, keepdims=True)

    rev = lambda n: NB - 1 - n
    vec = pl.BlockSpec((H, 1, 1), lambda n: (0, 0, 0))
    gates = pl.BlockSpec((2 * H, NCB, 1, C), lambda n: (0, rev(n), 0, 0))
    wide = pl.BlockSpec((RB, H * _HM), lambda n: (rev(n), 0))
    item = lambda dt: pltpu.VMEM((NCB, H, C, DK), dt)
    return pl.pallas_call(
        body, name=name, grid=(NB,),
        in_specs=[wide, gates, vec, vec,
                  pl.BlockSpec((H, NCB, DK, DK), lambda n: (0, rev(n), 0, 0)),
                  pl.BlockSpec((H, NCB, C, C), lambda n: (0, rev(n), 0, 0)),
                  pl.BlockSpec((RB, H * DK), lambda n: (rev(n), 0))],
        out_specs=[wide, gates, vec, vec],
        out_shape=[jax.ShapeDtypeStruct((S, H * _HM), F32),
                   jax.ShapeDtypeStruct((2 * H, NC, 1, C), F32),
                   jax.ShapeDtypeStruct((H, 1, 1), F32),
                   jax.ShapeDtypeStruct((H, 1, 1), F32)],
        scratch_shapes=[pltpu.VMEM((H, DK, DK), F32), item(BF16), item(BF16), item(BF16), item(BF16),
                        pltpu.VMEM((NCB, H, C, C), BF16), item(F32), item(F32)],
        compiler_params=_params("arbitrary"),
    )(qkv, ab, a_log, dt_bias, states, tinvs, do)


def _gdn_outnorm_fwd(o, z, gain, *, name):
    S, HV = o.shape
    RB = min(256, S)

    def body(o_ref, z_ref, g_ref, y_ref):
        for h in range(HV // GDN_DK):
            cols = slice(h * GDN_DK, (h + 1) * GDN_DK)
            ov = o_ref[:, cols]
            r = lax.rsqrt(jnp.mean(ov * ov, axis=-1, keepdims=True) + RMS_EPS)
            y_ref[:, cols] = (ov * r * g_ref[...] * _silu(z_ref[:, cols].astype(F32))).astype(BF16)

    blk = pl.BlockSpec((RB, HV), lambda i: (i, 0))
    return pl.pallas_call(
        body, name=name, grid=(S // RB,),
        in_specs=[blk, blk, pl.BlockSpec((1, GDN_DK), lambda i: (0, 0))], out_specs=blk,
        out_shape=jax.ShapeDtypeStruct((S, HV), BF16), compiler_params=_params("parallel"),
    )(o, z, gain)


def _gdn_outnorm_bwd(dy, o, z, gain, *, name):
    S, HV = o.shape
    RB = min(256, S)

    def body(dy_ref, o_ref, z_ref, g_ref, do_ref, dz_ref, dg_ref):
        part = None
        for h in range(HV // GDN_DK):
            cols = slice(h * GDN_DK, (h + 1) * GDN_DK)
            ov = o_ref[:, cols]
            zv = z_ref[:, cols].astype(F32)
            dyv = dy_ref[:, cols].astype(F32)
            r = lax.rsqrt(jnp.mean(ov * ov, axis=-1, keepdims=True) + RMS_EPS)
            n = ov * r
            sg = _sigmoid(zv)
            dng = dyv * (zv * sg)
            dn = dng * g_ref[...]
            do_ref[:, cols] = r * (dn - n * jnp.mean(dn * n, axis=-1, keepdims=True))
            dz_ref[:, cols] = (dyv * (n * g_ref[...]) * (sg * (1.0 + zv * (1.0 - sg)))).astype(BF16)
            p = jnp.sum(dng * n, axis=0, keepdims=True)
            part = p if part is None else part + p

        @pl.when(pl.program_id(0) == 0)
        def _():
            dg_ref[...] = part

        @pl.when(pl.program_id(0) > 0)
        def _():
            dg_ref[...] += part

    blk = pl.BlockSpec((RB, HV), lambda i: (i, 0))
    vec = pl.BlockSpec((1, GDN_DK), lambda i: (0, 0))
    return pl.pallas_call(
        body, name=name, grid=(S // RB,),
        in_specs=[blk, blk, blk, vec], out_specs=[blk, blk, vec],
        out_shape=[jax.ShapeDtypeStruct((S, HV), F32), jax.ShapeDtypeStruct((S, HV), BF16),
                   jax.ShapeDtypeStruct((1, GDN_DK), F32)],
        compiler_params=_params("arbitrary"),
    )(dy, o, z, gain)


def _rms64(x, gain):
    r = lax.rsqrt(jnp.mean(x * x, axis=-1, keepdims=True) + RMS_EPS)
    xh = x * r
    return xh, r, xh * gain


def _rms64_bwd(dy, xh, r, gain):
    dxh = dy * gain
    return r * (dxh - xh * jnp.mean(dxh * xh, axis=-1, keepdims=True))


_HP = LANES // DSW_DH
_DSW_W = DSW_HEADS * DSW_DH
_DSW_ROWS = 1024
_DSW_BATCH = 8


def _dsw_geometry(S, g):
    d = DSW_GROUPS[g][1]
    slab = DSW_BLK * d
    tb = max(1, min(_DSW_ROWS, S) // slab)
    return d, slab, tb, S // (tb * slab)


def _block_rows(t, r, slab, d):
    return pl.ds(t * slab + r, DSW_BLK) if d == 1 else pl.ds(t * slab + r, DSW_BLK, stride=d)


def _head(x, h):
    return x[:, h * DSW_DH:(h + 1) * DSW_DH]


def _dsw_attn_fwd(q, k, v, bias, q_gain, k_gain, prev_out, *, g, name):
    S, WT = q.shape
    B = DSW_BLK
    d, slab, tb, n_tiles = _dsw_geometry(S, g)
    rt = tb * slab
    cb = g * (_DSW_W // LANES)
    batch_res = max(1, _DSW_BATCH // tb)

    def body(q_ref, kp_ref, kc_ref, vp_ref, vc_ref, bias_ref, qg_ref, kg_ref, *rest):
        o_ref, lse_ref = rest[-2:]
        i = pl.program_id(1)
        qg, kg = qg_ref[...] * DSW_DH ** -0.5, kg_ref[...]
        col = lax.broadcasted_iota(jnp.int32, (B, 2 * B), 1)
        for r0 in range(0, d, batch_res):
            res = range(r0, min(d, r0 + batch_res))
            heads = range(_HP)
            k_raw = {(r, -1): kp_ref[_block_rows(0, r, slab, d), :] for r in res}
            v_raw = {(r, -1): vp_ref[_block_rows(0, r, slab, d), :] for r in res}
            q_raw = {}
            for r in res:
                for t in range(tb):
                    rows = _block_rows(t, r, slab, d)
                    q_raw[r, t], k_raw[r, t], v_raw[r, t] = q_ref[rows, :], kc_ref[rows, :], vc_ref[rows, :]
            kn = {key: [_rms64(_head(x, h), kg)[2].astype(BF16) for h in heads] for key, x in k_raw.items()}
            vb = {key: [_head(x, h).astype(BF16) for h in heads] for key, x in v_raw.items()}
            qn = {key: [_rms64(_head(x, h), qg)[2] for h in heads] for key, x in q_raw.items()}
            items = [(r, t, h) for r in res for t in range(tb) for h in heads]
            s = {}
            for r, t, h in items:
                sv = _dot(qn[r, t][h], jnp.concatenate([kn[r, t - 1][h], kn[r, t][h]], axis=0), "nt") + bias_ref[h]
                s[r, t, h] = jnp.where((i == 0) & (col < B), NEG_BIG, sv) if t == 0 else sv
            m = {it: jnp.max(s[it], axis=-1, keepdims=True) for it in items}
            p = {it: jnp.exp(s[it] - m[it]) for it in items}
            l = {it: jnp.sum(p[it], axis=-1, keepdims=True) for it in items}
            o = {(r, t, h): _dot(p[r, t, h], jnp.concatenate([vb[r, t - 1][h], vb[r, t][h]], axis=0))
                 for r, t, h in items}
            for r in res:
                for t in range(tb):
                    rows = _block_rows(t, r, slab, d)
                    o_ref[rows, :] = jnp.concatenate([o[r, t, h] / l[r, t, h] for h in heads], axis=1)
                    lse_ref[rows, :] = jnp.concatenate(
                        [jnp.broadcast_to(m[r, t, h] + jnp.log(l[r, t, h]), (B, DSW_DH)) for h in heads], axis=1)

    cur = pl.BlockSpec((rt, LANES), lambda hp, i: (i, cb + hp))
    prev = pl.BlockSpec((slab, LANES), lambda hp, i: (jnp.maximum(i * tb - 1, 0), cb + hp))
    vec = pl.BlockSpec((1, DSW_DH), lambda hp, i: (0, 0))
    shp = jax.ShapeDtypeStruct((S, WT), F32)
    carried = [] if prev_out is None else list(prev_out)
    n_in = 8
    return pl.pallas_call(
        body, name=name, grid=(_DSW_W // LANES, n_tiles),
        in_specs=[cur, prev, cur, prev, cur, pl.BlockSpec((_HP, B, 2 * B), lambda hp, i: (hp, 0, 0)), vec, vec]
                 + [pl.BlockSpec(memory_space=pl.ANY)] * len(carried),
        out_specs=[cur, cur], out_shape=[shp, shp],
        input_output_aliases={n_in + j: j for j in range(len(carried))},
        compiler_params=_params("parallel", "parallel"),
    )(q, k, k, v, v, bias, q_gain, k_gain, *carried)


def _dsw_merge(o_g, lse_g, *, name):
    S = o_g.shape[0]
    W, G = _DSW_W, len(DSW_GROUPS)
    tr = min(512, S)

    def body(o_ref, l_ref, out_ref, lse_ref):
        ls = [l_ref[:, g * W:(g + 1) * W] for g in range(G)]
        m = ls[0]
        for g in range(1, G):
            m = jnp.maximum(m, ls[g])
        den = jnp.zeros_like(m)
        acc = jnp.zeros_like(m)
        for g in range(G):
            wg = jnp.exp(ls[g] - m)
            den = den + wg
            acc = acc + wg * o_ref[:, g * W:(g + 1) * W]
        out_ref[...] = acc / den
        lse_ref[...] = m + jnp.log(den)

    wide = pl.BlockSpec((tr, G * W), lambda i: (i, 0))
    blk = pl.BlockSpec((tr, W), lambda i: (i, 0))
    shp = jax.ShapeDtypeStruct((S, W), F32)
    return pl.pallas_call(
        body, name=name, grid=(S // tr,), in_specs=[wide, wide], out_specs=[blk, blk],
        out_shape=[shp, shp], compiler_params=_params("parallel"),
    )(o_g, lse_g)


def _dsw_attn_bwd(q, k, v, o, lse, do, bias, q_gain, k_gain, prev_out, *, g, name):
    S, WT = q.shape
    B = DSW_BLK
    d, slab, tb, n_tiles = _dsw_geometry(S, g)
    rt = tb * slab
    cb = g * (_DSW_W // LANES)
    n_slabs = S // slab
    scale = DSW_DH ** -0.5
    batch_res = max(1, _DSW_BATCH // tb)

    def body(q_ref, qx_ref, kp_ref, kc_ref, vp_ref, vc_ref, o_ref, ox_ref, l_ref, lx_ref, do_ref, dox_ref,
             bias_ref, qg_ref, kg_ref, *rest):
        dq_ref, dk_ref, dv_ref, db_ref, dqg_ref, dkg_ref = rest[-6:]
        hp, i = pl.program_id(0), pl.program_id(1)
        qg, kg = qg_ref[...] * scale, kg_ref[...]
        col = lax.broadcasted_iota(jnp.int32, (B, 2 * B), 1)
        has_next = i < n_tiles - 1

        @pl.when(i == 0)
        def _():
            db_ref[...] = jnp.zeros_like(db_ref)

        dqg_acc = jnp.zeros((1, DSW_DH), F32)
        dkg_acc = jnp.zeros((1, DSW_DH), F32)
        heads = range(_HP)
        for r0 in range(0, d, batch_res):
            res = range(r0, min(d, r0 + batch_res))
            q_raw, k_raw, v_raw, o_raw, l_raw, do_raw = {}, {}, {}, {}, {}, {}
            for r in res:
                first_rows = _block_rows(0, r, slab, d)
                k_raw[r, -1], v_raw[r, -1] = kp_ref[first_rows, :], vp_ref[first_rows, :]
                for t in range(tb):
                    rows = _block_rows(t, r, slab, d)
                    q_raw[r, t], o_raw[r, t], l_raw[r, t], do_raw[r, t] = (
                        q_ref[rows, :], o_ref[rows, :], l_ref[rows, :], do_ref[rows, :])
                    k_raw[r, t], v_raw[r, t] = kc_ref[rows, :], vc_ref[rows, :]
                q_raw[r, tb], o_raw[r, tb], l_raw[r, tb], do_raw[r, tb] = (
                    qx_ref[first_rows, :], ox_ref[first_rows, :], lx_ref[first_rows, :], dox_ref[first_rows, :])
            kk = {key: [_rms64(_head(x, h), kg) for h in heads] for key, x in k_raw.items()}
            qq = {key: [_rms64(_head(x, h), qg) for h in heads] for key, x in q_raw.items()}
            knb = {key: [kk[key][h][2].astype(BF16) for h in heads] for key in kk}
            qnb = {key: [qq[key][h][2].astype(BF16) for h in heads] for key in qq}
            vb = {key: [_head(x, h).astype(BF16) for h in heads] for key, x in v_raw.items()}
            dob = {key: [_head(x, h).astype(BF16) for h in heads] for key, x in do_raw.items()}
            delta = {key: [jnp.sum(_head(do_raw[key], h) * _head(o_raw[key], h), axis=-1, keepdims=True)
                           for h in heads] for key in q_raw}
            full = [(r, t, h) for r in res for t in range(tb) for h in heads]
            half = [(r, tb, h) for r in res for h in heads]
            s = {}
            for r, t, h in full:
                sv = _dot(qnb[r, t][h], jnp.concatenate([knb[r, t - 1][h], knb[r, t][h]], axis=0), "nt") + bias_ref[h]
                s[r, t, h] = jnp.where((i == 0) & (col < B), NEG_BIG, sv) if t == 0 else sv
            for r, t, h in half:
                s[r, t, h] = _dot(qnb[r, t][h], knb[r, t - 1][h], "nt") + bias_ref[h, :, 0:B]
            lse_of = lambda r, t, h: l_raw[r, t][:, h * DSW_DH:h * DSW_DH + 1]
            p = {(r, t, h): jnp.exp(s[r, t, h] - lse_of(r, t, h)) for r, t, h in full}
            for r, t, h in half:
                p[r, t, h] = jnp.where(has_next, jnp.exp(s[r, t, h] - lse_of(r, t, h)), 0.0)
            dp = {(r, t, h): _dot(dob[r, t][h], jnp.concatenate([vb[r, t - 1][h], vb[r, t][h]], axis=0), "nt")
                  for r, t, h in full}
            for r, t, h in half:
                dp[r, t, h] = _dot(dob[r, t][h], vb[r, t - 1][h], "nt")
            ds = {(r, t, h): p[r, t, h] * (dp[r, t, h] - delta[r, t][h]) for r, t, h in full + half}
            pb = {it: p[it].astype(BF16) for it in ds}
            dsb = {it: ds[it].astype(BF16) for it in ds}
            for h in heads:
                tot = None
                for r in res:
                    for t in range(tb):
                        tot = ds[r, t, h] if tot is None else tot + ds[r, t, h]
                db_ref[h] += tot
            dqn = {(r, t, h): _dot(dsb[r, t, h], jnp.concatenate([knb[r, t - 1][h], knb[r, t][h]], axis=0))
                   for r, t, h in full}
            prev_half = lambda x, r, t, h: x[r, t, h][:, :B] if t < tb else x[r, t, h]
            dkn = {(r, t, h): _dot(dsb[r, t, h][:, B:], qnb[r, t][h], "tn")
                   + _dot(prev_half(dsb, r, t + 1, h), qnb[r, t + 1][h], "tn") for r, t, h in full}
            dvv = {(r, t, h): _dot(pb[r, t, h][:, B:], dob[r, t][h], "tn")
                   + _dot(prev_half(pb, r, t + 1, h), dob[r, t + 1][h], "tn") for r, t, h in full}
            for r, t, h in full:
                dqg_acc = dqg_acc + jnp.sum(dqn[r, t, h] * qq[r, t][h][0], axis=0, keepdims=True)
                dkg_acc = dkg_acc + jnp.sum(dkn[r, t, h] * kk[r, t][h][0], axis=0, keepdims=True)
            for r in res:
                for t in range(tb):
                    rows = _block_rows(t, r, slab, d)
                    dq_ref[rows, :] = jnp.concatenate(
                        [_rms64_bwd(dqn[r, t, h], qq[r, t][h][0], qq[r, t][h][1], qg) for h in heads], axis=1)
                    dk_ref[rows, :] = jnp.concatenate(
                        [_rms64_bwd(dkn[r, t, h], kk[r, t][h][0], kk[r, t][h][1], kg) for h in heads], axis=1)
                    dv_ref[rows, :] = jnp.concatenate([dvv[r, t, h] for h in heads], axis=1)

        start = (hp == 0) & (i == 0)

        @pl.when(start)
        def _():
            dqg_ref[...] = dqg_acc * scale
            dkg_ref[...] = dkg_acc

        @pl.when(jnp.logical_not(start))
        def _():
            dqg_ref[...] += dqg_acc * scale
            dkg_ref[...] += dkg_acc

    def spec(rows, pick, base):
        return pl.BlockSpec((rows, LANES), lambda hp, i: (pick(i), base + hp))

    same = lambda i: i
    before = lambda i: jnp.maximum(i * tb - 1, 0)
    after = lambda i: jnp.minimum((i + 1) * tb, n_slabs - 1)
    cur, cur1 = spec(rt, same, cb), spec(rt, same, 0)
    vec = pl.BlockSpec((1, DSW_DH), lambda hp, i: (0, 0))
    bspec = pl.BlockSpec((_HP, B, 2 * B), lambda hp, i: (hp, 0, 0))
    shp = jax.ShapeDtypeStruct((S, WT), F32)
    vshp = jax.ShapeDtypeStruct((1, DSW_DH), F32)
    carried = [] if prev_out is None else list(prev_out)
    n_in = 15
    return pl.pallas_call(
        body, name=name, grid=(_DSW_W // LANES, n_tiles),
        in_specs=[cur, spec(slab, after, cb), spec(slab, before, cb), cur, spec(slab, before, cb), cur,
                  cur1, spec(slab, after, 0), cur1, spec(slab, after, 0), cur1, spec(slab, after, 0),
                  bspec, vec, vec] + [pl.BlockSpec(memory_space=pl.ANY)] * len(carried),
        out_specs=[cur, cur, cur, bspec, vec, vec],
        out_shape=[shp, shp, shp, jax.ShapeDtypeStruct(bias.shape, F32), vshp, vshp],
        input_output_aliases={n_in + j: j for j in range(len(carried))},
        compiler_params=_params("arbitrary", "arbitrary"),
    )(q, q, k, k, v, v, o, o, lse, lse, do, do, bias, q_gain, k_gain, *carried)


def _t5_bucket(dist):
    max_exact = REL_BUCKETS // 2
    scaled = jnp.log(jnp.maximum(dist, 1).astype(F32) / max_exact) / math.log(REL_MAX_DIST / max_exact)
    large = jnp.minimum(max_exact + (scaled * (REL_BUCKETS - max_exact)).astype(jnp.int32), REL_BUCKETS - 1)
    return jnp.where(dist < max_exact, dist, large)


def _dsw_band():
    dist = (jnp.arange(DSW_BLK)[:, None] + DSW_BLK) - jnp.arange(2 * DSW_BLK)[None, :]
    return dist, (dist >= 0) & (dist <= DSW_BLK)


def _dsw_bias(rel_bias):
    dist, band = _dsw_band()
    out = []
    for g, (_, d) in enumerate(DSW_GROUPS):
        hot = jax.nn.one_hot(_t5_bucket(jnp.maximum(dist, 0) * d), REL_BUCKETS, dtype=F32)
        tab = jnp.einsum("qkb,bh->hqk", hot, rel_bias[:, g * DSW_HEADS:(g + 1) * DSW_HEADS],
                         precision=lax.Precision.HIGHEST)
        out.append(jnp.where(band[None], tab, NEG_BIG))
    return jnp.stack(out)


def _dsw_bucket_onehot():
    dist, band = _dsw_band()
    out = []
    for _, d in DSW_GROUPS:
        hot = jax.nn.one_hot(_t5_bucket(jnp.maximum(dist, 0) * d), LANES, dtype=BF16)
        out.append(jnp.where(band[..., None], hot, 0).reshape(-1, LANES))
    return jnp.stack(out)


def _exchange(send, *, gather, name):
    R, C = send.shape[-2:]

    def body(src_ref, dst_ref, send_sems, recv_sems, local_sem):
        x, y, c = lax.axis_index("x"), lax.axis_index("y"), lax.axis_index("c")
        me = 4 * x + 2 * y + c
        mine = pltpu.make_async_copy(src_ref if gather else src_ref.at[me], dst_ref.at[me], local_sem)
        mine.start()
        copies = []
        for rel in range(1, N_DEV):
            px = 1 - x if rel & 4 else x
            py = 1 - y if rel & 2 else y
            pc = 1 - c if rel & 1 else c
            peer = 4 * px + 2 * py + pc
            cp = pltpu.make_async_remote_copy(
                src_ref=src_ref if gather else src_ref.at[peer], dst_ref=dst_ref.at[me],
                send_sem=send_sems.at[rel - 1], recv_sem=recv_sems.at[rel - 1],
                device_id=(px, py, pc), device_id_type=pl.DeviceIdType.MESH)
            cp.start()
            copies.append(cp)
        for cp in copies:
            cp.wait()
        mine.wait()

    return pl.pallas_call(
        body, name=name,
        in_specs=[pl.BlockSpec(memory_space=pl.ANY)], out_specs=pl.BlockSpec(memory_space=pl.ANY),
        out_shape=jax.ShapeDtypeStruct((N_DEV, R, C), send.dtype),
        scratch_shapes=[pltpu.SemaphoreType.DMA((N_DEV - 1,)), pltpu.SemaphoreType.DMA((N_DEV - 1,)),
                        pltpu.SemaphoreType.DMA(())],
    )(send)


def _gather_two_level(send, *, name):
    R, C = send.shape

    def body(src_ref, dst_ref, send_sems, recv_sems, local_sem):
        x, y, c = lax.axis_index("x"), lax.axis_index("y"), lax.axis_index("c")
        me, sibling = (x, y, c), (x, y, 1 - c)
        chips = [(1 - x, y), (x, 1 - y), (1 - x, 1 - y)]

        def slot(px, py, pc):
            return dst_ref.at[4 * px + 2 * py + pc]

        def copy(k, block, to, src=None):
            return pltpu.make_async_remote_copy(
                src_ref=slot(*block) if src is None else src, dst_ref=slot(*block),
                send_sem=send_sems.at[k], recv_sem=recv_sems.at[k],
                device_id=to, device_id_type=pl.DeviceIdType.MESH)

        mine = pltpu.make_async_copy(src_ref, slot(*me), local_sem)
        mine.start()
        first = [copy(0, me, sibling, src=src_ref)]
        first += [copy(1 + j, me, (*chip, c), src=src_ref) for j, chip in enumerate(chips)]
        for cp in first:
            cp.start()
        passed = [copy(4 + j, (*chip, c), sibling) for j, chip in enumerate(chips)]
        for j, chip in enumerate(chips):
            copy(1 + j, (*chip, c), me).wait_recv()
            passed[j].start()
        copy(0, sibling, me).wait_recv()
        for j, chip in enumerate(chips):
            copy(4 + j, (*chip, 1 - c), me).wait_recv()
        for cp in first + passed:
            cp.wait_send()
        mine.wait()

    return pl.pallas_call(
        body, name=name,
        in_specs=[pl.BlockSpec(memory_space=pl.ANY)], out_specs=pl.BlockSpec(memory_space=pl.ANY),
        out_shape=jax.ShapeDtypeStruct((N_DEV, R, C), send.dtype),
        scratch_shapes=[pltpu.SemaphoreType.DMA((N_DEV - 1,)), pltpu.SemaphoreType.DMA((N_DEV - 1,)),
                        pltpu.SemaphoreType.DMA(())],
    )(send)


def _swap_with_sibling(send, *, name):
    def body(src_ref, dst_ref, send_sem, recv_sem):
        x, y, c = lax.axis_index("x"), lax.axis_index("y"), lax.axis_index("c")
        cp = pltpu.make_async_remote_copy(src_ref=src_ref, dst_ref=dst_ref, send_sem=send_sem, recv_sem=recv_sem,
                                          device_id=(x, y, 1 - c), device_id_type=pl.DeviceIdType.MESH)
        cp.start()
        cp.wait()

    return pl.pallas_call(
        body, name=name,
        in_specs=[pl.BlockSpec(memory_space=pl.ANY)], out_specs=pl.BlockSpec(memory_space=pl.ANY),
        out_shape=jax.ShapeDtypeStruct(send.shape, send.dtype),
        scratch_shapes=[pltpu.SemaphoreType.DMA(()), pltpu.SemaphoreType.DMA(())],
    )(send)


def _exchange_chips(send, *, name):
    n_chips, R, C = send.shape

    def body(src_ref, dst_ref, send_sems, recv_sems, local_sem):
        x, y, c = lax.axis_index("x"), lax.axis_index("y"), lax.axis_index("c")
        here = 2 * x + y
        mine = pltpu.make_async_copy(src_ref.at[here], dst_ref.at[here], local_sem)
        mine.start()
        copies = []
        for rel in range(1, n_chips):
            px = 1 - x if rel & 2 else x
            py = 1 - y if rel & 1 else y
            cp = pltpu.make_async_remote_copy(
                src_ref=src_ref.at[2 * px + py], dst_ref=dst_ref.at[here],
                send_sem=send_sems.at[rel - 1], recv_sem=recv_sems.at[rel - 1],
                device_id=(px, py, c), device_id_type=pl.DeviceIdType.MESH)
            cp.start()
            copies.append(cp)
        for cp in copies:
            cp.wait()
        mine.wait()

    return pl.pallas_call(
        body, name=name,
        in_specs=[pl.BlockSpec(memory_space=pl.ANY)], out_specs=pl.BlockSpec(memory_space=pl.ANY),
        out_shape=jax.ShapeDtypeStruct(send.shape, send.dtype),
        scratch_shapes=[pltpu.SemaphoreType.DMA((n_chips - 1,)), pltpu.SemaphoreType.DMA((n_chips - 1,)),
                        pltpu.SemaphoreType.DMA(())],
    )(send)


def _add_pair(a, b, *, name):
    n, R, C = a.shape
    tr = _tile(R, 1024)

    def body(a_ref, b_ref, o_ref):
        o_ref[...] = (a_ref[...].astype(F32) + b_ref[...].astype(F32)).astype(o_ref.dtype)

    blk = pl.BlockSpec((None, tr, C), lambda k, i: (k, i, 0))
    return pl.pallas_call(
        body, name=name, grid=(n, R // tr), in_specs=[blk, blk], out_specs=blk,
        out_shape=jax.ShapeDtypeStruct(a.shape, a.dtype), compiler_params=_params("parallel", "parallel"),
    )(a, b)


_BIG = ("w_ffn_in", "w_ffn_out", "gdn_w_in", "gdn_conv", "gdn_w_out", "dsw_w_in", "dsw_w_out")
_SHARD_AXIS = {"w_ffn_in": 2, "w_ffn_out": 1, "gdn_w_in": 2, "gdn_conv": 2, "gdn_w_out": 1, "dsw_w_in": 2,
               "dsw_w_out": 2}
_SMALL = ("b_ada", "norm_mix", "norm_ffn", "gdn_a_log", "gdn_dt_bias", "gdn_out_norm", "dsw_q_norm",
          "dsw_k_norm", "rel_bias")
_ROW_ALIGN = 16
_BIG_ALIGN = 1024


def _ceil_to(n, m):
    return -(-n // m) * m


def _seg_rows(shape):
    return _ceil_to(_ceil_to(int(np.prod(shape)), LANES) // LANES, _ROW_ALIGN)


def _pack(arrs, total_align):
    lead = arrs[0][1]
    segs = []
    for a, nlead in arrs:
        assert nlead == lead
        bshape = a.shape[:nlead]
        n = int(np.prod(a.shape[nlead:]))
        rows = _seg_rows(a.shape[nlead:])
        flat = a.reshape(bshape + (n,))
        flat = jnp.pad(flat, [(0, 0)] * nlead + [(0, rows * LANES - n)])
        segs.append(flat.reshape(bshape + (rows, LANES)))
    buf = jnp.concatenate(segs, axis=lead)
    total = _ceil_to(buf.shape[lead], total_align)
    return jnp.pad(buf, [(0, 0)] * lead + [(0, total - buf.shape[lead]), (0, 0)])


def _unpack(buf, shapes, nlead):
    out, off = [], 0
    for shp in shapes:
        n, rows = int(np.prod(shp)), _seg_rows(shp)
        seg = lax.slice_in_dim(buf, off, off + rows, axis=nlead)
        seg = seg.reshape(buf.shape[:nlead] + (rows * LANES,))[..., :n]
        out.append(seg.reshape(buf.shape[:nlead] + tuple(shp)))
        off += rows
    return out


def _to_natural(g, axis):
    n, L, r, c = g.shape
    if axis == 2:
        return jnp.transpose(g, (1, 2, 0, 3)).reshape(L, r, n * c)
    return jnp.transpose(g, (1, 0, 2, 3)).reshape(L, n * r, c)


def _to_blocked(w, axis):
    L, R, C = w.shape
    if axis == 2:
        return jnp.transpose(w.reshape(L, R, N_DEV, C // N_DEV), (2, 0, 1, 3))
    return jnp.transpose(w.reshape(L, N_DEV, R // N_DEV, C), (1, 0, 2, 3))


def _hm(a):
    lead = a.shape[:-1]
    return jnp.swapaxes(a.reshape(lead + (3, GDN_HEADS, GDN_DK)), -3, -2).reshape(lead + (3 * GDN_HEADS * GDN_DK,))


def _un_hm(a):
    lead = a.shape[:-1]
    return jnp.swapaxes(a.reshape(lead + (GDN_HEADS, 3, GDN_DK)), -3, -2).reshape(lead + (3 * GDN_HEADS * GDN_DK,))


_TILES = (1536, 1408, 1024, 768, 512, 384, 256, 128, 64, 32, 16, 8)


def _tile(n, cap):
    for t in _TILES:
        if t <= cap and n % t == 0:
            return t
    return n


def _mm_auto(a, b, mode, name, **kw):
    if mode == "tn":
        (K, M), N = a.shape, b.shape[1]
        tm, tn, tk = _tile(M, 1408), _tile(N, 1408), _tile(K, 1024)
    else:
        M, K = a.shape
        N = b.shape[1] if mode == "nn" else b.shape[0]
        tm, tn, tk = _tile(M, 512), _tile(N, 1536), _tile(K, 1408)
    return _mm(a, b, mode=mode, name=name, tm=tm, tn=tn, tk=tk, **kw)


def _row(v):
    return v.reshape(1, -1)


def _ffn_in_act(h, w_in, *, name):
    S, D = h.shape
    F = w_in.shape[1] // 2
    tm, tn = _tile(S, 512), _tile(F, 1408)
    nj = F // tn

    def body(h_ref, wg_ref, wu_ref, g_ref, u_ref, a_ref):
        hv = h_ref[...]
        gate = jnp.dot(hv, wg_ref[...], preferred_element_type=F32)
        up = jnp.dot(hv, wu_ref[...], preferred_element_type=F32)
        g_ref[...] = gate.astype(BF16)
        u_ref[...] = up.astype(BF16)
        a_ref[...] = (_silu(gate) * up).astype(BF16)

    out = pl.BlockSpec((tm, tn), lambda i, j: (i, j))
    shp = jax.ShapeDtypeStruct((S, F), BF16)
    return pl.pallas_call(
        body, name=name, grid=(S // tm, nj),
        in_specs=[pl.BlockSpec((tm, D), lambda i, j: (i, 0)), pl.BlockSpec((D, tn), lambda i, j: (0, j)),
                  pl.BlockSpec((D, tn), lambda i, j: (0, j + nj))],
        out_specs=[out, out, out], out_shape=[shp, shp, shp],
        compiler_params=_params("parallel", "parallel"),
    )(h, w_in, w_in)


def _ffn_out_dx_act(dy, w_out, gate_vec, pg, pu, *, name):
    S, D = dy.shape
    F = w_out.shape[0]
    tm, tn = _tile(S, 512), _tile(F, 1408)

    def body(dy_ref, w_ref, gv_ref, pg_ref, pu_ref, dg_ref, du_ref):
        dyg = (dy_ref[...] * gv_ref[...]).astype(BF16)
        da = lax.dot_general(dyg, w_ref[...], _DOT_DIMS["nt"], preferred_element_type=F32)
        gate = pg_ref[...].astype(F32)
        up = pu_ref[...].astype(F32)
        sg = _sigmoid(gate)
        dg_ref[...] = (da * up * (sg * (1.0 + gate * (1.0 - sg)))).astype(BF16)
        du_ref[...] = (da * (gate * sg)).astype(BF16)

    blk = pl.BlockSpec((tm, tn), lambda i, j: (i, j))
    shp = jax.ShapeDtypeStruct((S, F), BF16)
    return pl.pallas_call(
        body, name=name, grid=(S // tm, F // tn),
        in_specs=[pl.BlockSpec((tm, D), lambda i, j: (i, 0)), pl.BlockSpec((tn, D), lambda i, j: (j, 0)),
                  pl.BlockSpec((1, D), lambda i, j: (0, 0)), blk, blk],
        out_specs=[blk, blk], out_shape=[shp, shp],
        compiler_params=_params("parallel", "parallel"),
    )(dy, w_out, gate_vec, pg, pu)


def _ffn_fwd(x, mod, gain, w_in, w_out, tag):
    sh, sc, gate = mod
    h = _norm_mod_fwd(x, gain, sc, sh, name=f"ffn_norm_{tag}")
    pg, pu, a = _ffn_in_act(h, w_in, name=f"ffn_in_{tag}")
    y = _mm_auto(a, w_out, "nn", f"ffn_out_{tag}", out_scale=gate, resid=x)
    return y, (x, h, pg, pu, a)


def _ffn_bwd(dy, saved, mod, gain, w_in, w_out, tag):
    sh, sc, gate = mod
    x, h, pg, pu, a = saved
    F = pg.shape[1]
    gmat = _mm_auto(a, dy, "tn", f"ffn_out_g_{tag}")
    dw_out, dgate = _wout_grad(gmat, w_out, gate, name=f"ffn_out_dw_{tag}")
    dpg, dpu = _ffn_out_dx_act(dy, w_out, gate, pg, pu, name=f"ffn_out_dx_{tag}")
    dw_in = jnp.concatenate([_mm_auto(h, dpg, "tn", f"ffn_in_dw_gate_{tag}", out_dtype=BF16),
                             _mm_auto(h, dpu, "tn", f"ffn_in_dw_up_{tag}", out_dtype=BF16)], axis=1)
    dh = _mm_auto(dpg, w_in, "nt", f"ffn_in_dx_gate_{tag}")
    dh = _mm_auto(dpu, w_in, "nt", f"ffn_in_dx_up_{tag}", b_k_off=F, resid=dh)
    dx, dsh, dsc, dgain = _norm_mod_bwd(dh, x, dy, gain, sc, name=f"ffn_norm_bwd_{tag}")
    return dx, dict(w_in=dw_in, w_out=dw_out, gain=dgain, mod=(dsh, dsc, dgate))


def _gdn_fwd(x, mod, gain, W):
    sh, sc, gate = mod
    S = x.shape[0]
    h = _norm_mod_fwd(x, gain, sc, sh, name="gdn_norm")
    pq = _mm_auto(h, W["gdn_qkv"], "nn", "gdn_in_qkv", out_dtype=BF16)
    z = _mm_auto(h, W["gdn_z"], "nn", "gdn_in_z", out_dtype=BF16)
    ab = _mm_auto(h, W["gdn_ab"], "nn", "gdn_in_ab")
    qkvn = _gdn_prep_fwd(pq, W["gdn_conv"], name="gdn_prep")
    ab4 = jnp.transpose(ab[:, :2 * GDN_HEADS]).reshape(2 * GDN_HEADS, S // GDN_CHUNK, 1, GDN_CHUNK)
    o, states, tinvs = _gdn_chunk_fwd(qkvn, ab4, W["gdn_a_log"], W["gdn_dt_bias"], name="gdn_chunk")
    o2 = _gdn_outnorm_fwd(o, z, W["gdn_out_norm"], name="gdn_outnorm")
    y = _mm_auto(o2, W["gdn_out"], "nn", "gdn_out", out_scale=gate, resid=x)
    return y, (x, h, pq, z, qkvn, ab4, o, states, tinvs, o2)


def _gdn_bwd(dy, saved, mod, gain, W):
    sh, sc, gate = mod
    x, h, pq, z, qkvn, ab4, o, states, tinvs, o2 = saved
    S = x.shape[0]
    gmat = _mm_auto(o2, dy, "tn", "gdn_out_g")
    dw_out, dgate = _wout_grad(gmat, W["gdn_out"], gate, name="gdn_out_dw")
    do2 = _mm_auto(dy, W["gdn_out"], "nt", "gdn_out_dx", a_scale=gate)
    do, dz, dout_norm = _gdn_outnorm_bwd(do2, o, z, W["gdn_out_norm"], name="gdn_outnorm_bwd")
    dqkvn, dab4, da_log, ddt_bias = _gdn_chunk_bwd(
        qkvn, ab4, W["gdn_a_log"], W["gdn_dt_bias"], states, tinvs, do, name="gdn_chunk_bwd")
    dc, dconv8 = _gdn_prep_bwd_pre(dqkvn, pq, W["gdn_conv"], name="gdn_prep_bwd")
    dpq = _gdn_conv_bwd_x(dc, W["gdn_conv"], name="gdn_conv_bwd")
    dab = jnp.transpose(dab4.reshape(2 * GDN_HEADS, S))
    dab = jnp.pad(dab, ((0, 0), (0, LANES - 2 * GDN_HEADS))).astype(BF16)
    dw_qkv = _mm_auto(h, dpq, "tn", "gdn_in_qkv_dw", out_dtype=BF16)
    dw_z = _mm_auto(h, dz, "tn", "gdn_in_z_dw", out_dtype=BF16)
    dw_ab = _mm_auto(h, dab, "tn", "gdn_in_ab_dw", out_dtype=BF16)
    dh = _mm_auto(dpq, W["gdn_qkv"], "nt", "gdn_in_qkv_dx")
    dh = _mm_auto(dz, W["gdn_z"], "nt", "gdn_in_z_dx", resid=dh)
    dh = _mm_auto(dab, W["gdn_ab"], "nt", "gdn_in_ab_dx", resid=dh)
    dx, dsh, dsc, dgain = _norm_mod_bwd(dh, x, dy, gain, sc, name="gdn_norm_bwd")
    dw_in = jnp.concatenate([_un_hm(dw_qkv), dw_z, dw_ab[:, :2 * GDN_HEADS]], axis=1)
    return dx, dict(gdn_w_in=dw_in, gdn_conv=_un_hm(dconv8[:GDN_CONV]), gdn_w_out=dw_out, gdn_out_norm=dout_norm,
                    gdn_a_log=da_log.reshape(1, GDN_HEADS), gdn_dt_bias=ddt_bias.reshape(1, GDN_HEADS),
                    gain=dgain, mod=(dsh, dsc, dgate))


def _dsw_fwd(x, mod, gain, W):
    sh, sc, gate = mod
    h = _norm_mod_fwd(x, gain, sc, sh, name="dsw_norm")
    q, k, v = (_mm_auto(h, W[n], "nn", f"dsw_in_{n[-1]}") for n in ("dsw_q", "dsw_k", "dsw_v"))
    outs = None
    for g in range(len(DSW_GROUPS)):
        outs = _dsw_attn_fwd(q, k, v, W["dsw_bias"][g], W["dsw_q_norm"], W["dsw_k_norm"], outs, g=g,
                             name=f"dsw_attn_{g}")
    o, lse = _dsw_merge(*outs, name="dsw_merge")
    y = _mm_auto(o, W["dsw_out"], "nn", "dsw_out", out_scale=gate, resid=x)
    return y, (x, h, q, k, v, o, lse)


def _dsw_bwd(dy, saved, mod, gain, W):
    sh, sc, gate = mod
    x, h, q, k, v, o, lse = saved
    gmat = _mm_auto(o, dy, "tn", "dsw_out_g")
    dw_out, dgate = _wout_grad(gmat, W["dsw_out"], gate, name="dsw_out_dw")
    do = _mm_auto(dy, W["dsw_out"], "nt", "dsw_out_dx", a_scale=gate)
    G = len(DSW_GROUPS)
    dqkv, dbias, dq_norm, dk_norm = None, [], 0.0, 0.0
    for g in range(G):
        *dqkv, db, dqg, dkg = _dsw_attn_bwd(q, k, v, o, lse, do, W["dsw_bias"][g], W["dsw_q_norm"],
                                            W["dsw_k_norm"], dqkv, g=g, name=f"dsw_attn_bwd_{g}")
        dbias.append(db)
        dq_norm, dk_norm = dq_norm + dqg, dk_norm + dkg
    dws, dh = [], None
    for n, d in zip(("dsw_q", "dsw_k", "dsw_v"), dqkv):
        dws.append(_mm_auto(h, d, "tn", f"dsw_in_{n[-1]}_dw", out_dtype=BF16))
        dh = _mm_auto(d, W[n], "nt", f"dsw_in_{n[-1]}_dx", **({} if dh is None else {"resid": dh}))
    dx, dsh, dsc, dgain = _norm_mod_bwd(dh, x, dy, gain, sc, name="dsw_norm_bwd")
    hot = _dsw_bucket_onehot()
    drel = [_mm_auto(dbias[g].reshape(DSW_HEADS, -1), hot[g], "nn", f"dsw_rel_bias_{g}")[:, :REL_BUCKETS]
            for g in range(G)]
    return dx, dict(dsw_w_in=jnp.concatenate(dws, axis=1), dsw_w_out=dw_out, dsw_q_norm=dq_norm,
                    dsw_k_norm=dk_norm, rel_bias=jnp.transpose(jnp.concatenate(drel, axis=0)),
                    gain=dgain, mod=(dsh, dsc, dgate))


def _local_step(x, target, mod, W):
    mods = [[_row(mod[l, i]) for i in range(6)] for l in range(2)]
    nmix = [_row(W["norm_mix"][l]) for l in range(2)]
    nffn = [_row(W["norm_ffn"][l]) for l in range(2)]
    x1, s_gdn = _gdn_fwd(x, mods[0][:3], nmix[0], W)
    x2, s_f0 = _ffn_fwd(x1, mods[0][3:], nffn[0], W["w_ffn_in"][0], W["w_ffn_out"][0], "0")
    x3, s_dsw = _dsw_fwd(x2, mods[1][:3], nmix[1], W)
    x4, s_f1 = _ffn_fwd(x3, mods[1][3:], nffn[1], W["w_ffn_in"][1], W["w_ffn_out"][1], "1")
    dx4, sse = _loss_head(x4, target, name="loss_head")
    dx3, g_f1 = _ffn_bwd(dx4, s_f1, mods[1][3:], nffn[1], W["w_ffn_in"][1], W["w_ffn_out"][1], "1")
    dx2, g_dsw = _dsw_bwd(dx3, s_dsw, mods[1][:3], nmix[1], W)
    dx1, g_f0 = _ffn_bwd(dx2, s_f0, mods[0][3:], nffn[0], W["w_ffn_in"][0], W["w_ffn_out"][0], "0")
    dx0, g_gdn = _gdn_bwd(dx1, s_gdn, mods[0][:3], nmix[0], W)
    dmod = jnp.stack([jnp.concatenate(list(g_gdn["mod"]) + list(g_f0["mod"]), axis=0),
                      jnp.concatenate(list(g_dsw["mod"]) + list(g_f1["mod"]), axis=0)])
    grads = dict(
        w_ffn_in=jnp.stack([g_f0["w_in"], g_f1["w_in"]]), w_ffn_out=jnp.stack([g_f0["w_out"], g_f1["w_out"]]),
        norm_mix=jnp.concatenate([g_gdn["gain"], g_dsw["gain"]], axis=0),
        norm_ffn=jnp.concatenate([g_f0["gain"], g_f1["gain"]], axis=0),
        gdn_w_in=g_gdn["gdn_w_in"][None], gdn_conv=g_gdn["gdn_conv"][None], gdn_w_out=g_gdn["gdn_w_out"][None],
        gdn_out_norm=g_gdn["gdn_out_norm"], gdn_a_log=g_gdn["gdn_a_log"], gdn_dt_bias=g_gdn["gdn_dt_bias"],
        dsw_w_in=g_dsw["dsw_w_in"][None], dsw_w_out=g_dsw["dsw_w_out"][None],
        dsw_q_norm=g_dsw["dsw_q_norm"], dsw_k_norm=g_dsw["dsw_k_norm"], rel_bias=g_dsw["rel_bias"])
    return sse, dx0, grads, dmod


def _prepare_weights(full, small):
    gw = full["gdn_w_in"][0]
    hk3 = 3 * GDN_HEADS * GDN_DK
    di = full["dsw_w_in"][0]
    dq = di.shape[1] // 3
    return dict(
        w_ffn_in=full["w_ffn_in"], w_ffn_out=full["w_ffn_out"],
        gdn_qkv=_hm(gw[:, :hk3]), gdn_z=gw[:, hk3:hk3 + GDN_HEADS * GDN_DK],
        gdn_ab=jnp.pad(gw[:, hk3 + GDN_HEADS * GDN_DK:], ((0, 0), (0, LANES - 2 * GDN_HEADS))),
        gdn_conv=_hm(full["gdn_conv"][0]), gdn_out=full["gdn_w_out"][0],
        dsw_q=di[:, :dq], dsw_k=di[:, dq:2 * dq], dsw_v=di[:, 2 * dq:], dsw_out=full["dsw_w_out"][0],
        norm_mix=small["norm_mix"], norm_ffn=small["norm_ffn"],
        gdn_a_log=small["gdn_a_log"].reshape(GDN_HEADS, 1, 1), gdn_dt_bias=small["gdn_dt_bias"].reshape(GDN_HEADS, 1, 1),
        gdn_out_norm=small["gdn_out_norm"], dsw_q_norm=small["dsw_q_norm"], dsw_k_norm=small["dsw_k_norm"],
        dsw_bias=_dsw_bias(small["rel_bias"]))


_W_NAMES = ("w_ada", "b_ada", "norm_mix", "norm_ffn", "w_ffn_in", "w_ffn_out", "gdn_w_in", "gdn_conv",
            "gdn_a_log", "gdn_dt_bias", "gdn_out_norm", "gdn_w_out", "dsw_w_in", "dsw_q_norm", "dsw_k_norm",
            "dsw_w_out", "rel_bias")
_PAD_BATCH = 16


def _pad_rows(a, rows):
    return jnp.pad(a, ((0, rows - a.shape[0]), (0, 0)))


def kernel(x, c, w_ada, b_ada, norm_mix, norm_ffn, w_ffn_in, w_ffn_out, gdn_w_in, gdn_conv, gdn_a_log, gdn_dt_bias, gdn_out_norm, gdn_w_out, dsw_w_in, dsw_q_norm, dsw_k_norm, dsw_w_out, rel_bias, loss_target, m_w_ada, m_b_ada, m_norm_mix, m_norm_ffn, m_w_ffn_in, m_w_ffn_out, m_gdn_w_in, m_gdn_conv, m_gdn_a_log, m_gdn_dt_bias, m_gdn_out_norm, m_gdn_w_out, m_dsw_w_in, m_dsw_q_norm, m_dsw_k_norm, m_dsw_w_out, m_rel_bias, v_w_ada, v_b_ada, v_norm_mix, v_norm_ffn, v_w_ffn_in, v_w_ffn_out, v_gdn_w_in, v_gdn_conv, v_gdn_a_log, v_gdn_dt_bias, v_gdn_out_norm, v_gdn_w_out, v_dsw_w_in, v_dsw_q_norm, v_dsw_k_norm, v_dsw_w_out, v_rel_bias):
    w = dict(zip(_W_NAMES, (w_ada, b_ada, norm_mix, norm_ffn, w_ffn_in, w_ffn_out, gdn_w_in, gdn_conv, gdn_a_log,
                            gdn_dt_bias, gdn_out_norm, gdn_w_out, dsw_w_in, dsw_q_norm, dsw_k_norm, dsw_w_out,
                            rel_bias)))
    m = dict(zip(_W_NAMES, (m_w_ada, m_b_ada, m_norm_mix, m_norm_ffn, m_w_ffn_in, m_w_ffn_out, m_gdn_w_in,
                            m_gdn_conv, m_gdn_a_log, m_gdn_dt_bias, m_gdn_out_norm, m_gdn_w_out, m_dsw_w_in,
                            m_dsw_q_norm, m_dsw_k_norm, m_dsw_w_out, m_rel_bias)))
    v = dict(zip(_W_NAMES, (v_w_ada, v_b_ada, v_norm_mix, v_norm_ffn, v_w_ffn_in, v_w_ffn_out, v_gdn_w_in,
                            v_gdn_conv, v_gdn_a_log, v_gdn_dt_bias, v_gdn_out_norm, v_gdn_w_out, v_dsw_w_in,
                            v_dsw_q_norm, v_dsw_k_norm, v_dsw_w_out, v_rel_bias)))
    D = x.shape[-1]
    n_layers, _, ada_cols = w_ada.shape

    c_all = _exchange(c.reshape(D // LANES, LANES), gather=True, name="gather_cond").reshape(N_DEV, D)
    c_pad = _pad_rows(c_all, _PAD_BATCH)
    proj = [_mm(c_pad, w_ada[l], mode="nn", name=f"ada_proj_{l}", tm=_PAD_BATCH, tn=ada_cols, tk=D, a_silu=True)
            for l in range(n_layers)]
    mod_send = _pack([(jnp.stack([p[:N_DEV] for p in proj], axis=1), 1)], _ROW_ALIGN)
    mod_recv = _exchange(mod_send, gather=False, name="scatter_mod")
    mod = _unpack(mod_recv, [(n_layers, ada_cols)], 1)[0]
    mod = jnp.transpose(mod, (1, 0, 2)).reshape(n_layers, N_DEV * ada_cols) + b_ada
    mod = mod.reshape(n_layers, 6, D)

    conv_hi = gdn_conv.astype(BF16)
    conv_lo = (gdn_conv - conv_hi.astype(F32)).astype(BF16)
    w_send = _pack([(conv_hi if n == "gdn_conv" else w[n].astype(BF16), 0) for n in _BIG] + [(conv_lo, 0)],
                   _ROW_ALIGN)
    w_all = _gather_two_level(w_send, name="gather_weights")
    parts = _unpack(w_all, [w[n].shape for n in _BIG] + [gdn_conv.shape], 1)
    full = {n: _to_natural(parts[i], _SHARD_AXIS[n]) for i, n in enumerate(_BIG)}
    full["gdn_conv"] = full["gdn_conv"].astype(F32) + _to_natural(parts[-1], _SHARD_AXIS["gdn_conv"]).astype(F32)
    W = _prepare_weights(full, {n: w[n] for n in _SMALL})

    sse, grad_x, grads, dmod = _local_step(x[0], loss_target[0], mod, W)
    loss = lax.psum(0.5 * sse[0, 0] / D, ("x", "y", "c"))

    grads["b_ada"] = dmod.reshape(n_layers, 6 * D)
    my_c = lax.axis_index("c")
    big_send = _pack([(_to_blocked(grads[n].astype(BF16), _SHARD_AXIS[n]), 1) for n in _BIG], _BIG_ALIGN)
    by_core = big_send.reshape((N_DEV // 2, 2) + big_send.shape[1:])
    keep = lax.dynamic_index_in_dim(by_core, my_c, axis=1, keepdims=False)
    give = lax.dynamic_index_in_dim(by_core, 1 - my_c, axis=1, keepdims=False)
    pair = _add_pair(keep, _swap_with_sibling(give, name="swap_grads"), name="add_sibling_grads")
    g_recv = _exchange_chips(pair, name="scatter_grads")

    dmod_send = _pack([(jnp.transpose(dmod.reshape(n_layers, N_DEV, ada_cols), (1, 0, 2)), 1)], _ROW_ALIGN)
    small_send = _pack([(grads[n].reshape(w[n].shape), 0) for n in _SMALL], _ROW_ALIGN)
    s_recv = _exchange(jnp.concatenate(
        [dmod_send, jnp.broadcast_to(small_send[None], (N_DEV,) + small_send.shape)], axis=1),
        gather=False, name="scatter_small")
    dmod_rows = dmod_send.shape[1]

    out = {}
    kinds = ("grad", "delta", "new_m", "new_v")
    for n, g4 in zip(_BIG, _unpack(g_recv, [w[n].shape for n in _BIG], 1)):
        rows2d = lambda a: a.reshape((-1, w[n].shape[-1]))
        res = _adamw(rows2d(w[n]), g4.reshape((g4.shape[0], -1, w[n].shape[-1])), rows2d(m[n]), rows2d(v[n]),
                     name=f"adamw_{n}")
        for kind, buf in zip(kinds, res):
            out[kind, n] = buf.reshape(w[n].shape)

    dmod_all = _unpack(lax.slice_in_dim(s_recv, 0, dmod_rows, axis=1), [(n_layers, ada_cols)], 1)[0]
    g_ada = jnp.stack([_mm(c_pad, _pad_rows(dmod_all[:, l], _PAD_BATCH), mode="tn", name=f"ada_dw_{l}",
                           tm=D, tn=ada_cols, tk=_PAD_BATCH, a_silu=True) for l in range(n_layers)])
    flat = lambda a: a.reshape(n_layers * D, ada_cols)
    res = _adamw(flat(w_ada), flat(g_ada)[None], flat(m_w_ada), flat(v_w_ada), name="adamw_ada")
    for kind, buf in zip(("grad", "delta", "new_m", "new_v"), res):
        out[kind, "w_ada"] = buf.reshape(w_ada.shape)

    small_parts = lax.slice_in_dim(s_recv, dmod_rows, s_recv.shape[1], axis=1)
    packed = [_pack([(t[n], 0) for n in _SMALL], _ROW_ALIGN) for t in (w, m, v)]
    res = _adamw(packed[0], small_parts, packed[1], packed[2], name="adamw_replicated")
    for kind, buf in zip(("grad", "delta", "new_m", "new_v"), res):
        for n, a in zip(_SMALL, _unpack(buf, [w[n].shape for n in _SMALL], 0)):
            out[kind, n] = a

    return (loss, grad_x[None]) + tuple(out[kind, n] for kind in ("grad", "delta", "new_m", "new_v")
                                        for n in _W_NAMES)
```

```python
import functools
import math

import numpy as np
import jax
import jax.numpy as jnp
from jax import lax
from jax.experimental import pallas as pl
from jax.experimental.pallas import tpu as pltpu

F32 = jnp.float32
BF16 = jnp.bfloat16

N_DEV = 8
RMS_EPS = 1e-6
LANES = 128
V7X_VMEM_LIMIT = 48 * 1024 * 1024

GDN_HEADS = 8
GDN_DK = 128
GDN_CHUNK = 64
GDN_CONV = 4
DSW_GROUPS = ((128, 1), (512, 4), (2048, 16))
DSW_HEADS = 8
DSW_DH = 64
DSW_BLK = 128
REL_BUCKETS = 32
REL_MAX_DIST = 2048

ADAM_LR = 0.001
ADAM_B1 = 0.9
ADAM_B2 = 0.999
ADAM_EPS = 1e-08
ADAM_WD = 0.01
ADAM_STEP = 10

NEG_BIG = -1e30


def _params(*sem):
    return pltpu.CompilerParams(dimension_semantics=sem, vmem_limit_bytes=V7X_VMEM_LIMIT)


def _sigmoid(x):
    return 1.0 / (1.0 + jnp.exp(-x))


def _silu(x):
    return x * _sigmoid(x)


_DOT_DIMS = {
    "nn": (((1,), (0,)), ((), ())),
    "nt": (((1,), (1,)), ((), ())),
    "tn": (((0,), (0,)), ((), ())),
}


def _mm(a, b, *, mode, name, tm, tn, tk, out_dtype=F32, a_scale=None, out_scale=None, resid=None, a_silu=False,
        b_k_off=0):
    if mode == "nn":
        (M, K), N = a.shape, b.shape[1]
    elif mode == "nt":
        (M, K), N = a.shape, b.shape[0]
    else:
        (K, M), N = a.shape, b.shape[1]
    tm, tn, tk = min(tm, M), min(tn, N), min(tk, K)
    assert M % tm == 0 and N % tn == 0 and K % tk == 0 and b_k_off % tk == 0, (name, M, N, K, tm, tn, tk)
    assert b_k_off == 0 or mode == "nt", name
    nk = K // tk

    def body(*refs):
        refs = list(refs)
        a_ref, b_ref = refs.pop(0), refs.pop(0)
        as_ref = refs.pop(0) if a_scale is not None else None
        os_ref = refs.pop(0) if out_scale is not None else None
        r_ref = refs.pop(0) if resid is not None else None
        o_ref = refs.pop(0)
        acc_ref = refs.pop(0) if nk > 1 else None

        av = a_ref[...]
        if a_silu:
            av = _silu(av.astype(F32))
        if as_ref is not None:
            av = av.astype(F32) * as_ref[...]
        part = lax.dot_general(av.astype(BF16), b_ref[...].astype(BF16), _DOT_DIMS[mode],
                               preferred_element_type=F32)

        def finish(r):
            if os_ref is not None:
                r = r * os_ref[...]
            if r_ref is not None:
                r = r + r_ref[...].astype(F32)
            o_ref[...] = r.astype(out_dtype)

        if nk == 1:
            finish(part)
        else:
            k = pl.program_id(2)

            @pl.when(k == 0)
            def _():
                acc_ref[...] = part

            @pl.when(k > 0)
            def _():
                acc_ref[...] += part

            @pl.when(k == nk - 1)
            def _():
                finish(acc_ref[...])

    if mode == "nn":
        a_spec = pl.BlockSpec((tm, tk), lambda i, j, k: (i, k))
        b_spec = pl.BlockSpec((tk, tn), lambda i, j, k: (k, j))
        as_spec = pl.BlockSpec((1, tk), lambda i, j, k: (0, k))
    elif mode == "nt":
        a_spec = pl.BlockSpec((tm, tk), lambda i, j, k: (i, k))
        b_spec = pl.BlockSpec((tn, tk), lambda i, j, k: (j, k + b_k_off // tk))
        as_spec = pl.BlockSpec((1, tk), lambda i, j, k: (0, k))
    else:
        a_spec = pl.BlockSpec((tk, tm), lambda i, j, k: (k, i))
        b_spec = pl.BlockSpec((tk, tn), lambda i, j, k: (k, j))
        as_spec = None
    in_specs, args = [a_spec, b_spec], [a, b]
    if a_scale is not None:
        in_specs.append(as_spec)
        args.append(a_scale)
    if out_scale is not None:
        in_specs.append(pl.BlockSpec((1, tn), lambda i, j, k: (0, j)))
        args.append(out_scale)
    if resid is not None:
        in_specs.append(pl.BlockSpec((tm, tn), lambda i, j, k: (i, j)))
        args.append(resid)
    return pl.pallas_call(
        body, name=name, grid=(M // tm, N // tn, nk),
        in_specs=in_specs, out_specs=pl.BlockSpec((tm, tn), lambda i, j, k: (i, j)),
        out_shape=jax.ShapeDtypeStruct((M, N), out_dtype),
        scratch_shapes=[pltpu.VMEM((tm, tn), F32)] if nk > 1 else [],
        compiler_params=_params("parallel", "parallel", "arbitrary"),
    )(*args)


def _norm_mod_fwd(x, gain, sc, sh, *, name):
    S, D = x.shape
    tr = min(512, S)

    def body(x_ref, g_ref, sc_ref, sh_ref, h_ref):
        xv = x_ref[...]
        r = lax.rsqrt(jnp.mean(xv * xv, axis=-1, keepdims=True) + RMS_EPS)
        h_ref[...] = ((xv * r) * g_ref[...] * (1.0 + sc_ref[...]) + sh_ref[...]).astype(BF16)

    row = pl.BlockSpec((tr, D), lambda i: (i, 0))
    vec = pl.BlockSpec((1, D), lambda i: (0, 0))
    return pl.pallas_call(
        body, name=name, grid=(S // tr,), in_specs=[row, vec, vec, vec], out_specs=row,
        out_shape=jax.ShapeDtypeStruct((S, D), BF16), compiler_params=_params("parallel"),
    )(x, gain, sc, sh)


def _norm_mod_bwd(dh, x, dx_res, gain, sc, *, name):
    S, D = x.shape
    tr = min(256, S)
    n_steps = S // tr

    def body(dh_ref, x_ref, dxr_ref, g_ref, sc_ref, dx_ref, dsh_ref, dsc_ref, dgain_ref, acc_sh, acc_a):
        i = pl.program_id(0)
        xv = x_ref[...]
        r = lax.rsqrt(jnp.mean(xv * xv, axis=-1, keepdims=True) + RMS_EPS)
        n = xv * r
        dhv = dh_ref[...].astype(F32)
        dn = dhv * (g_ref[...] * (1.0 + sc_ref[...]))
        dx_ref[...] = dxr_ref[...] + r * (dn - n * jnp.mean(dn * n, axis=-1, keepdims=True))
        p_sh = jnp.sum(dhv, axis=0, keepdims=True)
        p_a = jnp.sum(dhv * n, axis=0, keepdims=True)

        @pl.when(i == 0)
        def _():
            acc_sh[...] = p_sh
            acc_a[...] = p_a

        @pl.when(i > 0)
        def _():
            acc_sh[...] += p_sh
            acc_a[...] += p_a

        @pl.when(i == n_steps - 1)
        def _():
            dsh_ref[...] = acc_sh[...]
            dsc_ref[...] = acc_a[...] * g_ref[...]
            dgain_ref[...] = acc_a[...] * (1.0 + sc_ref[...])

    row = pl.BlockSpec((tr, D), lambda i: (i, 0))
    vec = pl.BlockSpec((1, D), lambda i: (0, 0))
    vshape = jax.ShapeDtypeStruct((1, D), F32)
    return pl.pallas_call(
        body, name=name, grid=(n_steps,), in_specs=[row, row, row, vec, vec],
        out_specs=[row, vec, vec, vec],
        out_shape=[jax.ShapeDtypeStruct((S, D), F32), vshape, vshape, vshape],
        scratch_shapes=[pltpu.VMEM((1, D), F32), pltpu.VMEM((1, D), F32)],
        compiler_params=_params("arbitrary"),
    )(dh, x, dx_res, gain, sc)


def _wout_grad(gmat, w, gate, *, name):
    K, D = w.shape
    tr = min(256, K)
    n_steps = K // tr

    def body(g_ref, w_ref, gate_ref, dw_ref, dgate_ref, acc):
        i = pl.program_id(0)
        gv = g_ref[...]
        dw_ref[...] = (gv * gate_ref[...]).astype(BF16)
        part = jnp.sum(gv * w_ref[...], axis=0, keepdims=True)

        @pl.when(i == 0)
        def _():
            acc[...] = part

        @pl.when(i > 0)
        def _():
            acc[...] += part

        @pl.when(i == n_steps - 1)
        def _():
            dgate_ref[...] = acc[...]

    row = pl.BlockSpec((tr, D), lambda i: (i, 0))
    vec = pl.BlockSpec((1, D), lambda i: (0, 0))
    return pl.pallas_call(
        body, name=name, grid=(n_steps,), in_specs=[row, row, vec], out_specs=[row, vec],
        out_shape=[jax.ShapeDtypeStruct((K, D), BF16), jax.ShapeDtypeStruct((1, D), F32)],
        scratch_shapes=[pltpu.VMEM((1, D), F32)], compiler_params=_params("arbitrary"),
    )(gmat, w, gate)


def _loss_head(y, target, *, name):
    S, D = y.shape
    tr = min(512, S)
    n_steps = S // tr

    def body(y_ref, t_ref, dy_ref, sse_ref, acc):
        i = pl.program_id(0)
        e = y_ref[...] - t_ref[...]
        dy_ref[...] = e * (1.0 / D)
        part = jnp.sum(e * e, axis=0, keepdims=True)

        @pl.when(i == 0)
        def _():
            acc[...] = part

        @pl.when(i > 0)
        def _():
            acc[...] += part

        @pl.when(i == n_steps - 1)
        def _():
            sse_ref[...] = jnp.sum(acc[...], axis=1, keepdims=True)

    row = pl.BlockSpec((tr, D), lambda i: (i, 0))
    return pl.pallas_call(
        body, name=name, grid=(n_steps,), in_specs=[row, row],
        out_specs=[row, pl.BlockSpec((1, 1), lambda i: (0, 0))],
        out_shape=[jax.ShapeDtypeStruct((S, D), F32), jax.ShapeDtypeStruct((1, 1), F32)],
        scratch_shapes=[pltpu.VMEM((1, D), F32)], compiler_params=_params("arbitrary"),
    )(y, target)


def _adamw(w, g_parts, m, v, *, name):
    R, C = w.shape
    P = g_parts.shape[0]
    tr = _tile(R, max(8, 1024 * LANES // C))
    c1 = 1.0 / (1.0 - ADAM_B1 ** ADAM_STEP)
    c2 = 1.0 / (1.0 - ADAM_B2 ** ADAM_STEP)

    def body(w_ref, g_ref, m_ref, v_ref, go_ref, d_ref, mo_ref, vo_ref):
        g = g_ref[0].astype(F32)
        for q in range(1, P):
            g = g + g_ref[q].astype(F32)
        mn = ADAM_B1 * m_ref[...] + (1.0 - ADAM_B1) * g
        vn = ADAM_B2 * v_ref[...] + (1.0 - ADAM_B2) * (g * g)
        go_ref[...] = g
        mo_ref[...] = mn
        vo_ref[...] = vn
        d_ref[...] = -ADAM_LR * ((mn * c1) / (jnp.sqrt(vn * c2) + ADAM_EPS) + ADAM_WD * w_ref[...])

    row = pl.BlockSpec((tr, C), lambda i: (i, 0))
    shp = jax.ShapeDtypeStruct((R, C), F32)
    return pl.pallas_call(
        body, name=name, grid=(R // tr,),
        in_specs=[row, pl.BlockSpec((P, tr, C), lambda i: (0, i, 0)), row, row],
        out_specs=[row, row, row, row], out_shape=[shp, shp, shp, shp],
        compiler_params=_params("parallel"),
    )(w, g_parts, m, v)


_HALO = 16


def _conv_taps(buf, w_ref, rows, cols):
    acc = None
    for j in range(GDN_CONV):
        term = buf[pl.ds(_HALO - (GDN_CONV - 1) + j, rows), cols] * w_ref[j:j + 1, cols]
        acc = term if acc is None else acc + term
    return acc


def _fill_conv_buf(buf, halo_ref, x_ref, rows, first):
    buf[0:_HALO, :] = jnp.where(first, 0.0, halo_ref[...].astype(F32))
    buf[_HALO:_HALO + rows, :] = x_ref[...].astype(F32)


_HM = 3 * GDN_DK
_GDN_ROWS = 256
_PREP_HEADS = 4


def _l2n(seg):
    return lax.rsqrt(jnp.sum(seg * seg, axis=-1, keepdims=True) + RMS_EPS)


def _head_cols(hh):
    return slice(hh * _HM, (hh + 1) * _HM)


def _gdn_prep_fwd(x, conv_w, *, name):
    S, C3 = x.shape
    CB = _PREP_HEADS * _HM
    RB = min(256, S)

    def body(x_ref, halo_ref, w_ref, o_ref, buf):
        i = pl.program_id(0)
        _fill_conv_buf(buf, halo_ref, x_ref, RB, i == 0)
        for hh in range(_PREP_HEADS):
            c0 = hh * _HM
            y = _silu(_conv_taps(buf, w_ref, RB, _head_cols(hh)))
            q, k = y[:, :GDN_DK], y[:, GDN_DK:2 * GDN_DK]
            o_ref[:, c0:c0 + GDN_DK] = q * (_l2n(q) * GDN_DK ** -0.5)
            o_ref[:, c0 + GDN_DK:c0 + 2 * GDN_DK] = k * _l2n(k)
            o_ref[:, c0 + 2 * GDN_DK:c0 + _HM] = y[:, 2 * GDN_DK:]

    hb = RB // _HALO
    return pl.pallas_call(
        body, name=name, grid=(S // RB, C3 // CB),
        in_specs=[pl.BlockSpec((RB, CB), lambda i, j: (i, j)),
                  pl.BlockSpec((_HALO, CB), lambda i, j: (jnp.maximum(i * hb - 1, 0), j)),
                  pl.BlockSpec((GDN_CONV, CB), lambda i, j: (0, j))],
        out_specs=pl.BlockSpec((RB, CB), lambda i, j: (i, j)),
        out_shape=jax.ShapeDtypeStruct((S, C3), F32),
        scratch_shapes=[pltpu.VMEM((RB + _HALO, CB), F32)],
        compiler_params=_params("parallel", "parallel"),
    )(x, x, conv_w)


def _gdn_prep_bwd_pre(dn, x, conv_w, *, name):
    S, C3 = x.shape
    CB = _PREP_HEADS * _HM
    RB = min(256, S)
    n_steps = S // RB

    def body(dn_ref, x_ref, halo_ref, w_ref, dc_ref, dw_ref, buf):
        i = pl.program_id(1)
        _fill_conv_buf(buf, halo_ref, x_ref, RB, i == 0)
        head_parts = []
        for hh in range(_PREP_HEADS):
            c0, cols = hh * _HM, _head_cols(hh)
            acc = _conv_taps(buf, w_ref, RB, cols)
            sg = _sigmoid(acc)
            y = acc * sg
            dsilu = sg * (1.0 + acc * (1.0 - sg))
            for part, scale in ((0, GDN_DK ** -0.5), (1, 1.0)):
                sl = slice(part * GDN_DK, (part + 1) * GDN_DK)
                seg = y[:, sl]
                r = _l2n(seg)
                n = seg * r
                d = dn_ref[:, c0 + part * GDN_DK:c0 + (part + 1) * GDN_DK] * scale
                dc_ref[:, c0 + part * GDN_DK:c0 + (part + 1) * GDN_DK] = (
                    r * (d - n * jnp.sum(d * n, axis=-1, keepdims=True)) * dsilu[:, sl])
            dc_ref[:, c0 + 2 * GDN_DK:c0 + _HM] = dn_ref[:, c0 + 2 * GDN_DK:c0 + _HM] * dsilu[:, 2 * GDN_DK:]
            dc = dc_ref[:, cols]
            taps = [jnp.sum(dc * buf[pl.ds(_HALO - (GDN_CONV - 1) + t, RB), cols], axis=0, keepdims=True)
                    for t in range(GDN_CONV)]
            head_parts.append(jnp.concatenate(taps + [jnp.zeros((8 - GDN_CONV, _HM), F32)], axis=0))
        part = jnp.concatenate(head_parts, axis=1)

        @pl.when(i == 0)
        def _():
            dw_ref[...] = part

        @pl.when(i > 0)
        def _():
            dw_ref[...] += part

    hb = RB // _HALO
    return pl.pallas_call(
        body, name=name, grid=(C3 // CB, n_steps),
        in_specs=[pl.BlockSpec((RB, CB), lambda j, i: (i, j)),
                  pl.BlockSpec((RB, CB), lambda j, i: (i, j)),
                  pl.BlockSpec((_HALO, CB), lambda j, i: (jnp.maximum(i * hb - 1, 0), j)),
                  pl.BlockSpec((GDN_CONV, CB), lambda j, i: (0, j))],
        out_specs=[pl.BlockSpec((RB, CB), lambda j, i: (i, j)),
                   pl.BlockSpec((8, CB), lambda j, i: (0, j))],
        out_shape=[jax.ShapeDtypeStruct((S, C3), F32), jax.ShapeDtypeStruct((8, C3), F32)],
        scratch_shapes=[pltpu.VMEM((RB + _HALO, CB), F32)],
        compiler_params=_params("parallel", "arbitrary"),
    )(dn, x, x, conv_w)


def _gdn_conv_bwd_x(dc, conv_w, *, name):
    S, C3 = dc.shape
    CB = _PREP_HEADS * _HM
    RB = min(256, S)
    n_steps = S // RB

    def body(dc_ref, halo_ref, w_ref, dx_ref, buf):
        i = pl.program_id(0)
        buf[0:RB, :] = dc_ref[...]
        buf[RB:RB + _HALO, :] = jnp.where(i == n_steps - 1, 0.0, halo_ref[...])
        for hh in range(_PREP_HEADS):
            cols = _head_cols(hh)
            acc = None
            for j in range(GDN_CONV):
                term = buf[pl.ds(GDN_CONV - 1 - j, RB), cols] * w_ref[j:j + 1, cols]
                acc = term if acc is None else acc + term
            dx_ref[:, cols] = acc.astype(BF16)

    hb = RB // _HALO
    last = S // _HALO - 1
    return pl.pallas_call(
        body, name=name, grid=(n_steps, C3 // CB),
        in_specs=[pl.BlockSpec((RB, CB), lambda i, j: (i, j)),
                  pl.BlockSpec((_HALO, CB), lambda i, j: (jnp.minimum((i + 1) * hb, last), j)),
                  pl.BlockSpec((GDN_CONV, CB), lambda i, j: (0, j))],
        out_specs=pl.BlockSpec((RB, CB), lambda i, j: (i, j)),
        out_shape=jax.ShapeDtypeStruct((S, C3), BF16),
        scratch_shapes=[pltpu.VMEM((RB + _HALO, CB), F32)],
        compiler_params=_params("parallel", "parallel"),
    )(dc, dc, conv_w)


def _split_bf16(a):
    hi = a.astype(BF16)
    return hi, (a - hi.astype(F32)).astype(BF16)


def _dot(a, b, dims="nn", exact=False):
    def dot(p, q):
        return lax.dot_general(p, q, _DOT_DIMS[dims], preferred_element_type=F32)

    if exact:
        (ah, al), (bh, bl) = _split_bf16(a), _split_bf16(b)
        return dot(ah, bh) + (dot(ah, bl) + dot(al, bh))
    return dot(a.astype(BF16), b.astype(BF16))


def _softplus(x):
    return jnp.maximum(x, 0.0) + jnp.log(1.0 + jnp.exp(-jnp.abs(x)))


def _to_col(row, eye):
    return jnp.sum(jnp.where(eye, row, 0.0), axis=1, keepdims=True)


def _to_row(col, eye):
    return jnp.sum(jnp.where(eye, col, 0.0), axis=0, keepdims=True)


def _unit_lower_inverse(low, ri, ci):
    n = range(len(low))
    C = low[0].shape[0]
    eye = jnp.where(ri == ci, 1.0, 0.0)
    pair = (ri >> 1) == (ci >> 1)
    x = [eye - jnp.where(pair, low[j], 0.0) for j in n]
    m, sh = 2, 1
    while m < C:
        join = ((ri >> (sh + 1)) == (ci >> (sh + 1))) & (((ri >> sh) & 1) == 1) & (((ci >> sh) & 1) == 0)
        y = [_dot(x[j], jnp.where(join, low[j], 0.0)) for j in n]
        x = [x[j] - _dot(y[j], x[j]) for j in n]
        m, sh = 2 * m, sh + 1
    lx = [_dot(low[j], x[j], exact=True) for j in n]
    corr = [_dot(x[j], eye - x[j] - lx[j]) for j in n]
    return [x[j] + corr[j] for j in n]


def _gdn_local_batch(qkv, g_row, beta_row, ri, ci):
    n = range(len(qkv))
    eye, tril, strict = ri == ci, ri >= ci, ri > ci
    q = [qkv[j][:, :GDN_DK] for j in n]
    k = [qkv[j][:, GDN_DK:2 * GDN_DK] for j in n]
    v = [qkv[j][:, 2 * GDN_DK:] for j in n]
    g_col = [_to_col(g_row[j], eye) for j in n]
    beta_col = [_to_col(beta_row[j], eye) for j in n]
    gc_col = [jnp.sum(jnp.where(tril, g_row[j], 0.0), axis=1, keepdims=True) for j in n]
    gc_row = [jnp.sum(jnp.where(ri <= ci, g_col[j], 0.0), axis=0, keepdims=True) for j in n]
    g_last = [jnp.sum(g_row[j], axis=1, keepdims=True) for j in n]
    decay = [jnp.where(tril, jnp.exp(jnp.minimum(gc_col[j] - gc_row[j], 0.0)), 0.0) for j in n]
    e_col = [jnp.exp(gc_col[j]) for j in n]
    f_col = [jnp.exp(g_last[j] - gc_col[j]) for j in n]
    e_last = [jnp.exp(g_last[j]) for j in n]
    kb = [k[j] * beta_col[j] for j in n]
    vb = [v[j] * beta_col[j] for j in n]
    kk = [_dot(kb[j], k[j], "nt") for j in n]
    qk = [_dot(q[j], k[j], "nt") for j in n]
    low = [jnp.where(strict, kk[j] * decay[j], 0.0) for j in n]
    att = [qk[j] * decay[j] for j in n]
    return dict(q=q, k=k, v=v, beta_col=beta_col, decay=decay, e_col=e_col, f_col=f_col, e_last=e_last,
                kb=kb, vb=vb, low=low, att=att, eye=eye, strict=strict, tril=tril)


def _chunk_iotas():
    C = GDN_CHUNK
    return lax.broadcasted_iota(jnp.int32, (C, C), 0), lax.broadcasted_iota(jnp.int32, (C, C), 1)


def _gdn_chunk_fwd(qkv, ab, a_log, dt_bias, *, name):
    S = qkv.shape[0]
    H, C, DK = GDN_HEADS, GDN_CHUNK, GDN_DK
    RB = min(_GDN_ROWS, S)
    NCB, NB, NC = RB // C, S // RB, S // C
    heads = range(H)

    def body(qkv_ref, ab_ref, alog_ref, dtb_ref, o_ref, st_ref, t_ref, state, u_s, w_s, qe_s, kf_s, att_s):
        nb = pl.program_id(0)

        @pl.when(nb == 0)
        def _():
            state[...] = jnp.zeros_like(state)

        ri, ci = _chunk_iotas()
        neg_a = [-jnp.exp(alog_ref[h]) for h in heads]
        e_last = []
        for c in range(NCB):
            rows = pl.ds(c * C, C)
            g_row = [neg_a[h] * _softplus(ab_ref[h, c] + dtb_ref[h]) for h in heads]
            beta_row = [_sigmoid(ab_ref[H + h, c]) for h in heads]
            L = _gdn_local_batch([qkv_ref[rows, h * _HM:(h + 1) * _HM] for h in heads], g_row, beta_row, ri, ci)
            tinv = _unit_lower_inverse(L["low"], ri, ci)
            u = [_dot(tinv[h], L["vb"][h], exact=True) for h in heads]
            w = [_dot(tinv[h], L["kb"][h] * L["e_col"][h], exact=True) for h in heads]
            for h in heads:
                t_ref[h, c] = tinv[h]
                u_s[c, h] = u[h]
                w_s[c, h] = w[h].astype(BF16)
                qe_s[c, h] = (L["q"][h] * L["e_col"][h]).astype(BF16)
                kf_s[c, h] = (L["k"][h] * L["f_col"][h]).astype(BF16)
                att_s[c, h] = L["att"][h].astype(BF16)
            e_last.append(L["e_last"])
        st = [state[h] for h in heads]
        for c in range(NCB):
            rows = pl.ds(c * C, C)
            stb = [st[h].astype(BF16) for h in heads]
            vn = [u_s[c, h] - _dot(w_s[c, h], stb[h]) for h in heads]
            vnb = [vn[h].astype(BF16) for h in heads]
            out = [_dot(qe_s[c, h], stb[h]) + _dot(att_s[c, h], vnb[h]) for h in heads]
            new = [st[h] * e_last[c][h] + _dot(kf_s[c, h], vnb[h], "tn") for h in heads]
            for h in heads:
                o_ref[rows, h * DK:(h + 1) * DK] = out[h]
                st_ref[h, c] = st[h]
            st = new
        for h in heads:
            state[h] = st[h]

    return pl.pallas_call(
        body, name=name, grid=(NB,),
        in_specs=[pl.BlockSpec((RB, H * _HM), lambda n: (n, 0)),
                  pl.BlockSpec((2 * H, NCB, 1, C), lambda n: (0, n, 0, 0)),
                  pl.BlockSpec((H, 1, 1), lambda n: (0, 0, 0)),
                  pl.BlockSpec((H, 1, 1), lambda n: (0, 0, 0))],
        out_specs=[pl.BlockSpec((RB, H * DK), lambda n: (n, 0)),
                   pl.BlockSpec((H, NCB, DK, DK), lambda n: (0, n, 0, 0)),
                   pl.BlockSpec((H, NCB, C, C), lambda n: (0, n, 0, 0))],
        out_shape=[jax.ShapeDtypeStruct((S, H * DK), F32),
                   jax.ShapeDtypeStruct((H, NC, DK, DK), F32),
                   jax.ShapeDtypeStruct((H, NC, C, C), F32)],
        scratch_shapes=[pltpu.VMEM((H, DK, DK), F32), pltpu.VMEM((NCB, H, C, DK), F32),
                        pltpu.VMEM((NCB, H, C, DK), BF16), pltpu.VMEM((NCB, H, C, DK), BF16),
                        pltpu.VMEM((NCB, H, C, DK), BF16), pltpu.VMEM((NCB, H, C, C), BF16)],
        compiler_params=_params("arbitrary"),
    )(qkv, ab, a_log, dt_bias)


def _chip_copies(src_ref, dst_ref, send_sems, recv_sems, local_sem):
    x, y, c = lax.axis_index("x"), lax.axis_index("y"), lax.axis_index("c")
    here = 2 * x + y
    mine = pltpu.make_async_copy(src_ref.at[here], dst_ref.at[here], local_sem)
    copies = []
    for rel in range(1, N_DEV // 2):
        px = 1 - x if rel & 2 else x
        py = 1 - y if rel & 1 else y
        copies.append(pltpu.make_async_remote_copy(
            src_ref=src_ref.at[2 * px + py], dst_ref=dst_ref.at[here],
            send_sem=send_sems.at[rel - 1], recv_sem=recv_sems.at[rel - 1],
            device_id=(px, py, c), device_id_type=pl.DeviceIdType.MESH))
    return mine, copies


def _gdn_chunk_bwd(qkv, ab, a_log, dt_bias, states, tinvs, do, *, name, riding=None):
    S = qkv.shape[0]
    H, C, DK = GDN_HEADS, GDN_CHUNK, GDN_DK
    RB = min(_GDN_ROWS, S)
    NCB, NB, NC = RB // C, S // RB, S // C
    heads = range(H)

    def body(qkv_ref, ab_ref, alog_ref, dtb_ref, st_ref, t_ref, do_ref, *rest):
        if riding is None:
            dqkv_ref, dab_ref, dalog_ref, ddtb_ref, dstate, w_s, vn_s, qe_s, kf_s, att_s, dvn_s, dkf_s = rest
        else:
            (ride_src, dqkv_ref, dab_ref, dalog_ref, ddtb_ref, ride_dst,
             dstate, w_s, vn_s, qe_s, kf_s, att_s, dvn_s, dkf_s, *ride_sems) = rest
        nb = pl.program_id(0)

        if riding is not None:
            @pl.when(nb == 0)
            def _():
                mine, copies = _chip_copies(ride_src, ride_dst, *ride_sems)
                mine.start()
                for cp in copies:
                    cp.start()

        @pl.when(nb == 0)
        def _():
            dstate[...] = jnp.zeros_like(dstate)
            dalog_ref[...] = jnp.zeros_like(dalog_ref)
            ddtb_ref[...] = jnp.zeros_like(ddtb_ref)

        ri, ci = _chunk_iotas()
        neg_a = [-jnp.exp(alog_ref[h]) for h in heads]

        def local(c):
            rows = pl.ds(c * C, C)
            a_pre = [ab_ref[h, c] + dtb_ref[h] for h in heads]
            g_row = [neg_a[h] * _softplus(a_pre[h]) for h in heads]
            beta_row = [_sigmoid(ab_ref[H + h, c]) for h in heads]
            L = _gdn_local_batch([qkv_ref[rows, h * _HM:(h + 1) * _HM] for h in heads], g_row, beta_row, ri, ci)
            return L, a_pre, g_row, beta_row

        e_last = [None] * NCB
        for c in range(NCB):
            L, _, _, _ = local(c)
            kbe = [L["kb"][h] * L["e_col"][h] for h in heads]
            u = [_dot(t_ref[h, c], L["vb"][h], exact=True) for h in heads]
            w = [_dot(t_ref[h, c], kbe[h], exact=True) for h in heads]
            vn = [u[h] - _dot(w[h], st_ref[h, c]) for h in heads]
            for h in heads:
                w_s[c, h] = w[h].astype(BF16)
                vn_s[c, h] = vn[h].astype(BF16)
                qe_s[c, h] = (L["q"][h] * L["e_col"][h]).astype(BF16)
                kf_s[c, h] = (L["k"][h] * L["f_col"][h]).astype(BF16)
                att_s[c, h] = L["att"][h].astype(BF16)
            e_last[c] = L["e_last"]

        dst = [dstate[h] for h in heads]
        de_last = [None] * NCB
        for c in reversed(range(NCB)):
            rows = pl.ds(c * C, C)
            dob = [do_ref[rows, h * DK:(h + 1) * DK].astype(BF16) for h in heads]
            dstb = [dst[h].astype(BF16) for h in heads]
            dvn = [_dot(att_s[c, h], dob[h], "tn") + _dot(kf_s[c, h], dstb[h]) for h in heads]
            dkf = [_dot(vn_s[c, h], dstb[h], "nt") for h in heads]
            de_last[c] = [jnp.sum(jnp.sum(dst[h] * st_ref[h, c], axis=1, keepdims=True), axis=0, keepdims=True)
                          for h in heads]
            new = [dst[h] * e_last[c][h] + _dot(qe_s[c, h], dob[h], "tn")
                   - _dot(w_s[c, h], dvn[h].astype(BF16), "tn") for h in heads]
            for h in heads:
                dvn_s[c, h] = dvn[h]
                dkf_s[c, h] = dkf[h]
            dst = new
        for h in heads:
            dstate[h] = dst[h]

        for c in range(NCB):
            rows = pl.ds(c * C, C)
            L, a_pre, g_row, beta_row = local(c)
            q, k, v, kb, vb = L["q"], L["k"], L["v"], L["kb"], L["vb"]
            e_col, f_col, decay, beta_col = L["e_col"], L["f_col"], L["decay"], L["beta_col"]
            eye, strict, tril = L["eye"], L["strict"], L["tril"]
            tinv = [t_ref[h, c] for h in heads]
            stb = [st_ref[h, c].astype(BF16) for h in heads]
            dov = [do_ref[rows, h * DK:(h + 1) * DK] for h in heads]
            dvn = [dvn_s[c, h] for h in heads]
            dkf = [dkf_s[c, h] for h in heads]
            kbe = [kb[h] * e_col[h] for h in heads]
            datt = [jnp.where(tril, _dot(dov[h], vn_s[c, h], "nt"), 0.0) for h in heads]
            dqe = [_dot(dov[h], stb[h], "nt") for h in heads]
            dw = [-_dot(dvn[h], stb[h], "nt") for h in heads]
            dt = [_dot(dvn[h], vb[h], "nt") + _dot(dw[h], kbe[h], "nt") for h in heads]
            dvb = [_dot(tinv[h], dvn[h], "tn", exact=True) for h in heads]
            dkbe = [_dot(tinv[h], dw[h], "tn", exact=True) for h in heads]
            tdt = [_dot(tinv[h], dt[h], "tn", exact=True) for h in heads]
            dlow = [-jnp.where(strict, _dot(tdt[h], tinv[h], "nt", exact=True), 0.0) for h in heads]
            dkk = [dlow[h] * decay[h] for h in heads]
            dqk = [datt[h] * decay[h] for h in heads]
            dkb = [_dot(dkk[h], k[h]) + dkbe[h] * e_col[h] for h in heads]
            dk = [_dot(dkk[h], kb[h], "tn") + _dot(dqk[h], q[h], "tn") + dkf[h] * f_col[h] + dkb[h] * beta_col[h]
                  for h in heads]
            dq = [_dot(dqk[h], k[h]) + dqe[h] * e_col[h] for h in heads]
            for h in heads:
                dqkv_ref[rows, h * _HM:h * _HM + DK] = dq[h]
                dqkv_ref[rows, h * _HM + DK:h * _HM + 2 * DK] = dk[h]
                dqkv_ref[rows, h * _HM + 2 * DK:(h + 1) * _HM] = dvb[h] * beta_col[h]

            dbeta_col = [jnp.sum(k[h] * dkb[h] + v[h] * dvb[h], axis=1, keepdims=True) for h in heads]
            pmat = [dlow[h] * L["low"][h] + datt[h] * L["att"][h] for h in heads]
            df_col = [jnp.sum(k[h] * dkf[h], axis=1, keepdims=True) * f_col[h] for h in heads]
            dgc_col = [jnp.sum(pmat[h], axis=1, keepdims=True)
                       + jnp.sum(q[h] * dqe[h] + kb[h] * dkbe[h], axis=1, keepdims=True) * e_col[h] - df_col[h]
                       for h in heads]
            dgc_row = [_to_row(dgc_col[h], eye) - jnp.sum(pmat[h], axis=0, keepdims=True) for h in heads]
            dg_last = [jnp.sum(df_col[h], axis=0, keepdims=True) + de_last[c][h] * L["e_last"][h] for h in heads]
            dgc_c = [_to_col(dgc_row[h], eye) for h in heads]
            dg_row = [jnp.sum(jnp.where(ri >= ci, dgc_c[h], 0.0), axis=0, keepdims=True) + dg_last[h] for h in heads]
            dbeta_row = [_to_row(dbeta_col[h], eye) for h in heads]
            for h in heads:
                da_row = dg_row[h] * neg_a[h] * _sigmoid(a_pre[h])
                dab_ref[h, c] = da_row
                dab_ref[H + h, c] = dbeta_row[h] * beta_row[h] * (1.0 - beta_row[h])
                dalog_ref[h] += jnp.sum(dg_row[h] * g_row[h], axis=1, keepdims=True)
                ddtb_ref[h] += jnp.sum(da_row, axis=1, keepdims=True)

        if riding is not None:
            @pl.when(nb == NB - 1)
            def _():
                mine, copies = _chip_copies(ride_src, ride_dst, *ride_sems)
                for cp in copies:
                    cp.wait()
                mine.wait()

    rev = lambda n: NB - 1 - n
    vec = pl.BlockSpec((H, 1, 1), lambda n: (0, 0, 0))
    gates = pl.BlockSpec((2 * H, NCB, 1, C), lambda n: (0, rev(n), 0, 0))
    wide = pl.BlockSpec((RB, H * _HM), lambda n: (rev(n), 0))
    item = lambda dt: pltpu.VMEM((NCB, H, C, DK), dt)
    ride_args, ride_specs, ride_out, ride_scratch = [], [], [], []
    if riding is not None:
        n_peers = riding.shape[0] - 1
        ride_args, ride_specs = [riding], [pl.BlockSpec(memory_space=pl.ANY)]
        ride_out = [jax.ShapeDtypeStruct(riding.shape, riding.dtype)]
        ride_scratch = [pltpu.SemaphoreType.DMA((n_peers,)), pltpu.SemaphoreType.DMA((n_peers,)),
                        pltpu.SemaphoreType.DMA(())]
    return pl.pallas_call(
        body, name=name, grid=(NB,),
        in_specs=[wide, gates, vec, vec,
                  pl.BlockSpec((H, NCB, DK, DK), lambda n: (0, rev(n), 0, 0)),
                  pl.BlockSpec((H, NCB, C, C), lambda n: (0, rev(n), 0, 0)),
                  pl.BlockSpec((RB, H * DK), lambda n: (rev(n), 0))] + ride_specs,
        out_specs=[wide, gates, vec, vec] + ride_specs,
        out_shape=[jax.ShapeDtypeStruct((S, H * _HM), F32),
                   jax.ShapeDtypeStruct((2 * H, NC, 1, C), F32),
                   jax.ShapeDtypeStruct((H, 1, 1), F32),
                   jax.ShapeDtypeStruct((H, 1, 1), F32)] + ride_out,
        scratch_shapes=[pltpu.VMEM((H, DK, DK), F32), item(BF16), item(BF16), item(BF16), item(BF16),
                        pltpu.VMEM((NCB, H, C, C), BF16), item(F32), item(F32)] + ride_scratch,
        compiler_params=_params("arbitrary"),
    )(qkv, ab, a_log, dt_bias, states, tinvs, do, *ride_args)


def _gdn_outnorm_fwd(o, z, gain, *, name):
    S, HV = o.shape
    RB = min(256, S)

    def body(o_ref, z_ref, g_ref, y_ref):
        for h in range(HV // GDN_DK):
            cols = slice(h * GDN_DK, (h + 1) * GDN_DK)
            ov = o_ref[:, cols]
            r = lax.rsqrt(jnp.mean(ov * ov, axis=-1, keepdims=True) + RMS_EPS)
            y_ref[:, cols] = (ov * r * g_ref[...] * _silu(z_ref[:, cols].astype(F32))).astype(BF16)

    blk = pl.BlockSpec((RB, HV), lambda i: (i, 0))
    return pl.pallas_call(
        body, name=name, grid=(S // RB,),
        in_specs=[blk, blk, pl.BlockSpec((1, GDN_DK), lambda i: (0, 0))], out_specs=blk,
        out_shape=jax.ShapeDtypeStruct((S, HV), BF16), compiler_params=_params("parallel"),
    )(o, z, gain)


def _gdn_outnorm_bwd(dy, o, z, gain, *, name):
    S, HV = o.shape
    RB = min(256, S)

    def body(dy_ref, o_ref, z_ref, g_ref, do_ref, dz_ref, dg_ref):
        part = None
        for h in range(HV // GDN_DK):
            cols = slice(h * GDN_DK, (h + 1) * GDN_DK)
            ov = o_ref[:, cols]
            zv = z_ref[:, cols].astype(F32)
            dyv = dy_ref[:, cols].astype(F32)
            r = lax.rsqrt(jnp.mean(ov * ov, axis=-1, keepdims=True) + RMS_EPS)
            n = ov * r
            sg = _sigmoid(zv)
            dng = dyv * (zv * sg)
            dn = dng * g_ref[...]
            do_ref[:, cols] = r * (dn - n * jnp.mean(dn * n, axis=-1, keepdims=True))
            dz_ref[:, cols] = (dyv * (n * g_ref[...]) * (sg * (1.0 + zv * (1.0 - sg)))).astype(BF16)
            p = jnp.sum(dng * n, axis=0, keepdims=True)
            part = p if part is None else part + p

        @pl.when(pl.program_id(0) == 0)
        def _():
            dg_ref[...] = part

        @pl.when(pl.program_id(0) > 0)
        def _():
            dg_ref[...] += part

    blk = pl.BlockSpec((RB, HV), lambda i: (i, 0))
    vec = pl.BlockSpec((1, GDN_DK), lambda i: (0, 0))
    return pl.pallas_call(
        body, name=name, grid=(S // RB,),
        in_specs=[blk, blk, blk, vec], out_specs=[blk, blk, vec],
        out_shape=[jax.ShapeDtypeStruct((S, HV), F32), jax.ShapeDtypeStruct((S, HV), BF16),
                   jax.ShapeDtypeStruct((1, GDN_DK), F32)],
        compiler_params=_params("arbitrary"),
    )(dy, o, z, gain)


def _rms64(x, gain):
    r = lax.rsqrt(jnp.mean(x * x, axis=-1, keepdims=True) + RMS_EPS)
    xh = x * r
    return xh, r, xh * gain


def _rms64_bwd(dy, xh, r, gain):
    dxh = dy * gain
    return r * (dxh - xh * jnp.mean(dxh * xh, axis=-1, keepdims=True))


_HP = LANES // DSW_DH
_DSW_W = DSW_HEADS * DSW_DH
_DSW_ROWS = 1024
_DSW_BATCH = 8


def _dsw_geometry(S, g):
    d = DSW_GROUPS[g][1]
    slab = DSW_BLK * d
    tb = max(1, min(_DSW_ROWS, S) // slab)
    return d, slab, tb, S // (tb * slab)


def _block_rows(t, r, slab, d):
    return pl.ds(t * slab + r, DSW_BLK) if d == 1 else pl.ds(t * slab + r, DSW_BLK, stride=d)


def _head(x, h):
    return x[:, h * DSW_DH:(h + 1) * DSW_DH]


def _dsw_attn_fwd(q, k, v, bias, q_gain, k_gain, prev_out, *, g, name):
    S, WT = q.shape
    B = DSW_BLK
    d, slab, tb, n_tiles = _dsw_geometry(S, g)
    rt = tb * slab
    cb = g * (_DSW_W // LANES)
    batch_res = max(1, _DSW_BATCH // tb)

    def body(q_ref, kp_ref, kc_ref, vp_ref, vc_ref, bias_ref, qg_ref, kg_ref, *rest):
        o_ref, lse_ref = rest[-2:]
        i = pl.program_id(1)
        qg, kg = qg_ref[...] * DSW_DH ** -0.5, kg_ref[...]
        col = lax.broadcasted_iota(jnp.int32, (B, 2 * B), 1)
        for r0 in range(0, d, batch_res):
            res = range(r0, min(d, r0 + batch_res))
            heads = range(_HP)
            k_raw = {(r, -1): kp_ref[_block_rows(0, r, slab, d), :] for r in res}
            v_raw = {(r, -1): vp_ref[_block_rows(0, r, slab, d), :] for r in res}
            q_raw = {}
            for r in res:
                for t in range(tb):
                    rows = _block_rows(t, r, slab, d)
                    q_raw[r, t], k_raw[r, t], v_raw[r, t] = q_ref[rows, :], kc_ref[rows, :], vc_ref[rows, :]
            kn = {key: [_rms64(_head(x, h), kg)[2].astype(BF16) for h in heads] for key, x in k_raw.items()}
            vb = {key: [_head(x, h).astype(BF16) for h in heads] for key, x in v_raw.items()}
            qn = {key: [_rms64(_head(x, h), qg)[2] for h in heads] for key, x in q_raw.items()}
            items = [(r, t, h) for r in res for t in range(tb) for h in heads]
            s = {}
            for r, t, h in items:
                sv = _dot(qn[r, t][h], jnp.concatenate([kn[r, t - 1][h], kn[r, t][h]], axis=0), "nt") + bias_ref[h]
                s[r, t, h] = jnp.where((i == 0) & (col < B), NEG_BIG, sv) if t == 0 else sv
            m = {it: jnp.max(s[it], axis=-1, keepdims=True) for it in items}
            p = {it: jnp.exp(s[it] - m[it]) for it in items}
            l = {it: jnp.sum(p[it], axis=-1, keepdims=True) for it in items}
            o = {(r, t, h): _dot(p[r, t, h], jnp.concatenate([vb[r, t - 1][h], vb[r, t][h]], axis=0))
                 for r, t, h in items}
            for r in res:
                for t in range(tb):
                    rows = _block_rows(t, r, slab, d)
                    o_ref[rows, :] = jnp.concatenate([o[r, t, h] / l[r, t, h] for h in heads], axis=1)
                    lse_ref[rows, :] = jnp.concatenate(
                        [jnp.broadcast_to(m[r, t, h] + jnp.log(l[r, t, h]), (B, DSW_DH)) for h in heads], axis=1)

    cur = pl.BlockSpec((rt, LANES), lambda hp, i: (i, cb + hp))
    prev = pl.BlockSpec((slab, LANES), lambda hp, i: (jnp.maximum(i * tb - 1, 0), cb + hp))
    vec = pl.BlockSpec((1, DSW_DH), lambda hp, i: (0, 0))
    shp = jax.ShapeDtypeStruct((S, WT), F32)
    carried = [] if prev_out is None else list(prev_out)
    n_in = 8
    return pl.pallas_call(
        body, name=name, grid=(_DSW_W // LANES, n_tiles),
        in_specs=[cur, prev, cur, prev, cur, pl.BlockSpec((_HP, B, 2 * B), lambda hp, i: (hp, 0, 0)), vec, vec]
                 + [pl.BlockSpec(memory_space=pl.ANY)] * len(carried),
        out_specs=[cur, cur], out_shape=[shp, shp],
        input_output_aliases={n_in + j: j for j in range(len(carried))},
        compiler_params=_params("parallel", "parallel"),
    )(q, k, k, v, v, bias, q_gain, k_gain, *carried)


def _dsw_merge(o_g, lse_g, *, name):
    S = o_g.shape[0]
    W, G = _DSW_W, len(DSW_GROUPS)
    tr = min(512, S)

    def body(o_ref, l_ref, out_ref, lse_ref):
        ls = [l_ref[:, g * W:(g + 1) * W] for g in range(G)]
        m = ls[0]
        for g in range(1, G):
            m = jnp.maximum(m, ls[g])
        den = jnp.zeros_like(m)
        acc = jnp.zeros_like(m)
        for g in range(G):
            wg = jnp.exp(ls[g] - m)
            den = den + wg
            acc = acc + wg * o_ref[:, g * W:(g + 1) * W]
        out_ref[...] = acc / den
        lse_ref[...] = m + jnp.log(den)

    wide = pl.BlockSpec((tr, G * W), lambda i: (i, 0))
    blk = pl.BlockSpec((tr, W), lambda i: (i, 0))
    shp = jax.ShapeDtypeStruct((S, W), F32)
    return pl.pallas_call(
        body, name=name, grid=(S // tr,), in_specs=[wide, wide], out_specs=[blk, blk],
        out_shape=[shp, shp], compiler_params=_params("parallel"),
    )(o_g, lse_g)


def _dsw_attn_bwd(q, k, v, o, lse, do, bias, q_gain, k_gain, prev_out, *, g, name):
    S, WT = q.shape
    B = DSW_BLK
    d, slab, tb, n_tiles = _dsw_geometry(S, g)
    rt = tb * slab
    cb = g * (_DSW_W // LANES)
    n_slabs = S // slab
    scale = DSW_DH ** -0.5
    batch_res = max(1, _DSW_BATCH // tb)

    def body(q_ref, qx_ref, kp_ref, kc_ref, vp_ref, vc_ref, o_ref, ox_ref, l_ref, lx_ref, do_ref, dox_ref,
             bias_ref, qg_ref, kg_ref, *rest):
        dq_ref, dk_ref, dv_ref, db_ref, dqg_ref, dkg_ref = rest[-6:]
        hp, i = pl.program_id(0), pl.program_id(1)
        qg, kg = qg_ref[...] * scale, kg_ref[...]
        col = lax.broadcasted_iota(jnp.int32, (B, 2 * B), 1)
        has_next = i < n_tiles - 1

        @pl.when(i == 0)
        def _():
            db_ref[...] = jnp.zeros_like(db_ref)

        dqg_acc = jnp.zeros((1, DSW_DH), F32)
        dkg_acc = jnp.zeros((1, DSW_DH), F32)
        heads = range(_HP)
        for r0 in range(0, d, batch_res):
            res = range(r0, min(d, r0 + batch_res))
            q_raw, k_raw, v_raw, o_raw, l_raw, do_raw = {}, {}, {}, {}, {}, {}
            for r in res:
                first_rows = _block_rows(0, r, slab, d)
                k_raw[r, -1], v_raw[r, -1] = kp_ref[first_rows, :], vp_ref[first_rows, :]
                for t in range(tb):
                    rows = _block_rows(t, r, slab, d)
                    q_raw[r, t], o_raw[r, t], l_raw[r, t], do_raw[r, t] = (
                        q_ref[rows, :], o_ref[rows, :], l_ref[rows, :], do_ref[rows, :])
                    k_raw[r, t], v_raw[r, t] = kc_ref[rows, :], vc_ref[rows, :]
                q_raw[r, tb], o_raw[r, tb], l_raw[r, tb], do_raw[r, tb] = (
                    qx_ref[first_rows, :], ox_ref[first_rows, :], lx_ref[first_rows, :], dox_ref[first_rows, :])
            kk = {key: [_rms64(_head(x, h), kg) for h in heads] for key, x in k_raw.items()}
            qq = {key: [_rms64(_head(x, h), qg) for h in heads] for key, x in q_raw.items()}
            knb = {key: [kk[key][h][2].astype(BF16) for h in heads] for key in kk}
            qnb = {key: [qq[key][h][2].astype(BF16) for h in heads] for key in qq}
            vb = {key: [_head(x, h).astype(BF16) for h in heads] for key, x in v_raw.items()}
            dob = {key: [_head(x, h).astype(BF16) for h in heads] for key, x in do_raw.items()}
            delta = {key: [jnp.sum(_head(do_raw[key], h) * _head(o_raw[key], h), axis=-1, keepdims=True)
                           for h in heads] for key in q_raw}
            full = [(r, t, h) for r in res for t in range(tb) for h in heads]
            half = [(r, tb, h) for r in res for h in heads]
            s = {}
            for r, t, h in full:
                sv = _dot(qnb[r, t][h], jnp.concatenate([knb[r, t - 1][h], knb[r, t][h]], axis=0), "nt") + bias_ref[h]
                s[r, t, h] = jnp.where((i == 0) & (col < B), NEG_BIG, sv) if t == 0 else sv
            for r, t, h in half:
                s[r, t, h] = _dot(qnb[r, t][h], knb[r, t - 1][h], "nt") + bias_ref[h, :, 0:B]
            lse_of = lambda r, t, h: l_raw[r, t][:, h * DSW_DH:h * DSW_DH + 1]
            p = {(r, t, h): jnp.exp(s[r, t, h] - lse_of(r, t, h)) for r, t, h in full}
            for r, t, h in half:
                p[r, t, h] = jnp.where(has_next, jnp.exp(s[r, t, h] - lse_of(r, t, h)), 0.0)
            dp = {(r, t, h): _dot(dob[r, t][h], jnp.concatenate([vb[r, t - 1][h], vb[r, t][h]], axis=0), "nt")
                  for r, t, h in full}
            for r, t, h in half:
                dp[r, t, h] = _dot(dob[r, t][h], vb[r, t - 1][h], "nt")
            ds = {(r, t, h): p[r, t, h] * (dp[r, t, h] - delta[r, t][h]) for r, t, h in full + half}
            pb = {it: p[it].astype(BF16) for it in ds}
            dsb = {it: ds[it].astype(BF16) for it in ds}
            for h in heads:
                tot = None
                for r in res:
                    for t in range(tb):
                        tot = ds[r, t, h] if tot is None else tot + ds[r, t, h]
                db_ref[h] += tot
            dqn = {(r, t, h): _dot(dsb[r, t, h], jnp.concatenate([knb[r, t - 1][h], knb[r, t][h]], axis=0))
                   for r, t, h in full}
            prev_half = lambda x, r, t, h: x[r, t, h][:, :B] if t < tb else x[r, t, h]
            dkn = {(r, t, h): _dot(dsb[r, t, h][:, B:], qnb[r, t][h], "tn")
                   + _dot(prev_half(dsb, r, t + 1, h), qnb[r, t + 1][h], "tn") for r, t, h in full}
            dvv = {(r, t, h): _dot(pb[r, t, h][:, B:], dob[r, t][h], "tn")
                   + _dot(prev_half(pb, r, t + 1, h), dob[r, t + 1][h], "tn") for r, t, h in full}
            for r, t, h in full:
                dqg_acc = dqg_acc + jnp.sum(dqn[r, t, h] * qq[r, t][h][0], axis=0, keepdims=True)
                dkg_acc = dkg_acc + jnp.sum(dkn[r, t, h] * kk[r, t][h][0], axis=0, keepdims=True)
            for r in res:
                for t in range(tb):
                    rows = _block_rows(t, r, slab, d)
                    dq_ref[rows, :] = jnp.concatenate(
                        [_rms64_bwd(dqn[r, t, h], qq[r, t][h][0], qq[r, t][h][1], qg) for h in heads], axis=1)
                    dk_ref[rows, :] = jnp.concatenate(
                        [_rms64_bwd(dkn[r, t, h], kk[r, t][h][0], kk[r, t][h][1], kg) for h in heads], axis=1)
                    dv_ref[rows, :] = jnp.concatenate([dvv[r, t, h] for h in heads], axis=1)

        start = (hp == 0) & (i == 0)

        @pl.when(start)
        def _():
            dqg_ref[...] = dqg_acc * scale
            dkg_ref[...] = dkg_acc

        @pl.when(jnp.logical_not(start))
        def _():
            dqg_ref[...] += dqg_acc * scale
            dkg_ref[...] += dkg_acc

    def spec(rows, pick, base):
        return pl.BlockSpec((rows, LANES), lambda hp, i: (pick(i), base + hp))

    same = lambda i: i
    before = lambda i: jnp.maximum(i * tb - 1, 0)
    after = lambda i: jnp.minimum((i + 1) * tb, n_slabs - 1)
    cur, cur1 = spec(rt, same, cb), spec(rt, same, 0)
    vec = pl.BlockSpec((1, DSW_DH), lambda hp, i: (0, 0))
    bspec = pl.BlockSpec((_HP, B, 2 * B), lambda hp, i: (hp, 0, 0))
    shp = jax.ShapeDtypeStruct((S, WT), F32)
    vshp = jax.ShapeDtypeStruct((1, DSW_DH), F32)
    carried = [] if prev_out is None else list(prev_out)
    n_in = 15
    return pl.pallas_call(
        body, name=name, grid=(_DSW_W // LANES, n_tiles),
        in_specs=[cur, spec(slab, after, cb), spec(slab, before, cb), cur, spec(slab, before, cb), cur,
                  cur1, spec(slab, after, 0), cur1, spec(slab, after, 0), cur1, spec(slab, after, 0),
                  bspec, vec, vec] + [pl.BlockSpec(memory_space=pl.ANY)] * len(carried),
        out_specs=[cur, cur, cur, bspec, vec, vec],
        out_shape=[shp, shp, shp, jax.ShapeDtypeStruct(bias.shape, F32), vshp, vshp],
        input_output_aliases={n_in + j: j for j in range(len(carried))},
        compiler_params=_params("arbitrary", "arbitrary"),
    )(q, q, k, k, v, v, o, o, lse, lse, do, do, bias, q_gain, k_gain, *carried)


def _t5_bucket(dist):
    max_exact = REL_BUCKETS // 2
    scaled = jnp.log(jnp.maximum(dist, 1).astype(F32) / max_exact) / math.log(REL_MAX_DIST / max_exact)
    large = jnp.minimum(max_exact + (scaled * (REL_BUCKETS - max_exact)).astype(jnp.int32), REL_BUCKETS - 1)
    return jnp.where(dist < max_exact, dist, large)


def _dsw_band():
    dist = (jnp.arange(DSW_BLK)[:, None] + DSW_BLK) - jnp.arange(2 * DSW_BLK)[None, :]
    return dist, (dist >= 0) & (dist <= DSW_BLK)


def _dsw_bias(rel_bias):
    dist, band = _dsw_band()
    out = []
    for g, (_, d) in enumerate(DSW_GROUPS):
        hot = jax.nn.one_hot(_t5_bucket(jnp.maximum(dist, 0) * d), REL_BUCKETS, dtype=F32)
        tab = jnp.einsum("qkb,bh->hqk", hot, rel_bias[:, g * DSW_HEADS:(g + 1) * DSW_HEADS],
                         precision=lax.Precision.HIGHEST)
        out.append(jnp.where(band[None], tab, NEG_BIG))
    return jnp.stack(out)


def _dsw_bucket_onehot():
    dist, band = _dsw_band()
    out = []
    for _, d in DSW_GROUPS:
        hot = jax.nn.one_hot(_t5_bucket(jnp.maximum(dist, 0) * d), LANES, dtype=BF16)
        out.append(jnp.where(band[..., None], hot, 0).reshape(-1, LANES))
    return jnp.stack(out)


def _exchange(send, *, gather, name):
    R, C = send.shape[-2:]

    def body(src_ref, dst_ref, send_sems, recv_sems, local_sem):
        x, y, c = lax.axis_index("x"), lax.axis_index("y"), lax.axis_index("c")
        me = 4 * x + 2 * y + c
        mine = pltpu.make_async_copy(src_ref if gather else src_ref.at[me], dst_ref.at[me], local_sem)
        mine.start()
        copies = []
        for rel in range(1, N_DEV):
            px = 1 - x if rel & 4 else x
            py = 1 - y if rel & 2 else y
            pc = 1 - c if rel & 1 else c
            peer = 4 * px + 2 * py + pc
            cp = pltpu.make_async_remote_copy(
                src_ref=src_ref if gather else src_ref.at[peer], dst_ref=dst_ref.at[me],
                send_sem=send_sems.at[rel - 1], recv_sem=recv_sems.at[rel - 1],
                device_id=(px, py, pc), device_id_type=pl.DeviceIdType.MESH)
            cp.start()
            copies.append(cp)
        for cp in copies:
            cp.wait()
        mine.wait()

    return pl.pallas_call(
        body, name=name,
        in_specs=[pl.BlockSpec(memory_space=pl.ANY)], out_specs=pl.BlockSpec(memory_space=pl.ANY),
        out_shape=jax.ShapeDtypeStruct((N_DEV, R, C), send.dtype),
        scratch_shapes=[pltpu.SemaphoreType.DMA((N_DEV - 1,)), pltpu.SemaphoreType.DMA((N_DEV - 1,)),
                        pltpu.SemaphoreType.DMA(())],
    )(send)


def _gather_two_level(send, *, name):
    R, C = send.shape

    def body(src_ref, dst_ref, send_sems, recv_sems, local_sem):
        x, y, c = lax.axis_index("x"), lax.axis_index("y"), lax.axis_index("c")
        me, sibling = (x, y, c), (x, y, 1 - c)
        chips = [(1 - x, y), (x, 1 - y), (1 - x, 1 - y)]

        def slot(px, py, pc):
            return dst_ref.at[4 * px + 2 * py + pc]

        def copy(k, block, to, src=None):
            return pltpu.make_async_remote_copy(
                src_ref=slot(*block) if src is None else src, dst_ref=slot(*block),
                send_sem=send_sems.at[k], recv_sem=recv_sems.at[k],
                device_id=to, device_id_type=pl.DeviceIdType.MESH)

        mine = pltpu.make_async_copy(src_ref, slot(*me), local_sem)
        mine.start()
        first = [copy(0, me, sibling, src=src_ref)]
        first += [copy(1 + j, me, (*chip, c), src=src_ref) for j, chip in enumerate(chips)]
        for cp in first:
            cp.start()
        passed = [copy(4 + j, (*chip, c), sibling) for j, chip in enumerate(chips)]
        for j, chip in enumerate(chips):
            copy(1 + j, (*chip, c), me).wait_recv()
            passed[j].start()
        copy(0, sibling, me).wait_recv()
        for j, chip in enumerate(chips):
            copy(4 + j, (*chip, 1 - c), me).wait_recv()
        for cp in first + passed:
            cp.wait_send()
        mine.wait()

    return pl.pallas_call(
        body, name=name,
        in_specs=[pl.BlockSpec(memory_space=pl.ANY)], out_specs=pl.BlockSpec(memory_space=pl.ANY),
        out_shape=jax.ShapeDtypeStruct((N_DEV, R, C), send.dtype),
        scratch_shapes=[pltpu.SemaphoreType.DMA((N_DEV - 1,)), pltpu.SemaphoreType.DMA((N_DEV - 1,)),
                        pltpu.SemaphoreType.DMA(())],
    )(send)


def _swap_with_sibling(send, *, name):
    def body(src_ref, dst_ref, send_sem, recv_sem):
        x, y, c = lax.axis_index("x"), lax.axis_index("y"), lax.axis_index("c")
        cp = pltpu.make_async_remote_copy(src_ref=src_ref, dst_ref=dst_ref, send_sem=send_sem, recv_sem=recv_sem,
                                          device_id=(x, y, 1 - c), device_id_type=pl.DeviceIdType.MESH)
        cp.start()
        cp.wait()

    return pl.pallas_call(
        body, name=name,
        in_specs=[pl.BlockSpec(memory_space=pl.ANY)], out_specs=pl.BlockSpec(memory_space=pl.ANY),
        out_shape=jax.ShapeDtypeStruct(send.shape, send.dtype),
        scratch_shapes=[pltpu.SemaphoreType.DMA(()), pltpu.SemaphoreType.DMA(())],
    )(send)


def _exchange_chips(send, *, name):
    n_chips, R, C = send.shape

    def body(src_ref, dst_ref, send_sems, recv_sems, local_sem):
        mine, copies = _chip_copies(src_ref, dst_ref, send_sems, recv_sems, local_sem)
        mine.start()
        for cp in copies:
            cp.start()
        for cp in copies:
            cp.wait()
        mine.wait()

    return pl.pallas_call(
        body, name=name,
        in_specs=[pl.BlockSpec(memory_space=pl.ANY)], out_specs=pl.BlockSpec(memory_space=pl.ANY),
        out_shape=jax.ShapeDtypeStruct(send.shape, send.dtype),
        scratch_shapes=[pltpu.SemaphoreType.DMA((n_chips - 1,)), pltpu.SemaphoreType.DMA((n_chips - 1,)),
                        pltpu.SemaphoreType.DMA(())],
    )(send)


def _add_pair(a, b, *, name):
    n, R, C = a.shape
    tr = _tile(R, 1024)

    def body(a_ref, b_ref, o_ref):
        o_ref[...] = (a_ref[...].astype(F32) + b_ref[...].astype(F32)).astype(o_ref.dtype)

    blk = pl.BlockSpec((None, tr, C), lambda k, i: (k, i, 0))
    return pl.pallas_call(
        body, name=name, grid=(n, R // tr), in_specs=[blk, blk], out_specs=blk,
        out_shape=jax.ShapeDtypeStruct(a.shape, a.dtype), compiler_params=_params("parallel", "parallel"),
    )(a, b)


_BIG = ("w_ffn_in", "w_ffn_out", "gdn_w_in", "gdn_conv", "gdn_w_out", "dsw_w_in", "dsw_w_out")
_LATE = ("gdn_w_in", "gdn_conv", "gdn_w_out")
_EARLY = tuple(n for n in _BIG if n not in _LATE)
_SHARD_AXIS = {"w_ffn_in": 2, "w_ffn_out": 1, "gdn_w_in": 2, "gdn_conv": 2, "gdn_w_out": 1, "dsw_w_in": 2,
               "dsw_w_out": 2}
_SMALL = ("b_ada", "norm_mix", "norm_ffn", "gdn_a_log", "gdn_dt_bias", "gdn_out_norm", "dsw_q_norm",
          "dsw_k_norm", "rel_bias")
_ROW_ALIGN = 16
_BIG_ALIGN = 1024


def _ceil_to(n, m):
    return -(-n // m) * m


def _seg_rows(shape):
    return _ceil_to(_ceil_to(int(np.prod(shape)), LANES) // LANES, _ROW_ALIGN)


def _pack(arrs, total_align):
    lead = arrs[0][1]
    segs = []
    for a, nlead in arrs:
        assert nlead == lead
        bshape = a.shape[:nlead]
        n = int(np.prod(a.shape[nlead:]))
        rows = _seg_rows(a.shape[nlead:])
        flat = a.reshape(bshape + (n,))
        flat = jnp.pad(flat, [(0, 0)] * nlead + [(0, rows * LANES - n)])
        segs.append(flat.reshape(bshape + (rows, LANES)))
    buf = jnp.concatenate(segs, axis=lead)
    total = _ceil_to(buf.shape[lead], total_align)
    return jnp.pad(buf, [(0, 0)] * lead + [(0, total - buf.shape[lead]), (0, 0)])


def _unpack(buf, shapes, nlead):
    out, off = [], 0
    for shp in shapes:
        n, rows = int(np.prod(shp)), _seg_rows(shp)
        seg = lax.slice_in_dim(buf, off, off + rows, axis=nlead)
        seg = seg.reshape(buf.shape[:nlead] + (rows * LANES,))[..., :n]
        out.append(seg.reshape(buf.shape[:nlead] + tuple(shp)))
        off += rows
    return out


def _to_natural(g, axis):
    n, L, r, c = g.shape
    if axis == 2:
        return jnp.transpose(g, (1, 2, 0, 3)).reshape(L, r, n * c)
    return jnp.transpose(g, (1, 0, 2, 3)).reshape(L, n * r, c)


def _to_blocked(w, axis):
    L, R, C = w.shape
    if axis == 2:
        return jnp.transpose(w.reshape(L, R, N_DEV, C // N_DEV), (2, 0, 1, 3))
    return jnp.transpose(w.reshape(L, N_DEV, R // N_DEV, C), (1, 0, 2, 3))


def _hm(a):
    lead = a.shape[:-1]
    return jnp.swapaxes(a.reshape(lead + (3, GDN_HEADS, GDN_DK)), -3, -2).reshape(lead + (3 * GDN_HEADS * GDN_DK,))


def _un_hm(a):
    lead = a.shape[:-1]
    return jnp.swapaxes(a.reshape(lead + (GDN_HEADS, 3, GDN_DK)), -3, -2).reshape(lead + (3 * GDN_HEADS * GDN_DK,))


_TILES = (1536, 1408, 1024, 768, 512, 384, 256, 128, 64, 32, 16, 8)


def _tile(n, cap):
    for t in _TILES:
        if t <= cap and n % t == 0:
            return t
    return n


def _mm_auto(a, b, mode, name, **kw):
    if mode == "tn":
        (K, M), N = a.shape, b.shape[1]
        tm, tn, tk = _tile(M, 1408), _tile(N, 1408), _tile(K, 1024)
    else:
        M, K = a.shape
        N = b.shape[1] if mode == "nn" else b.shape[0]
        tm, tn, tk = _tile(M, 512), _tile(N, 1536), _tile(K, 1408)
    return _mm(a, b, mode=mode, name=name, tm=tm, tn=tn, tk=tk, **kw)


def _row(v):
    return v.reshape(1, -1)


def _ffn_in_act(h, w_in, *, name):
    S, D = h.shape
    F = w_in.shape[1] // 2
    tm, tn = _tile(S, 512), _tile(F, 1408)
    nj = F // tn

    def body(h_ref, wg_ref, wu_ref, g_ref, u_ref, a_ref):
        hv = h_ref[...]
        gate = jnp.dot(hv, wg_ref[...], preferred_element_type=F32)
        up = jnp.dot(hv, wu_ref[...], preferred_element_type=F32)
        g_ref[...] = gate.astype(BF16)
        u_ref[...] = up.astype(BF16)
        a_ref[...] = (_silu(gate) * up).astype(BF16)

    out = pl.BlockSpec((tm, tn), lambda i, j: (i, j))
    shp = jax.ShapeDtypeStruct((S, F), BF16)
    return pl.pallas_call(
        body, name=name, grid=(S // tm, nj),
        in_specs=[pl.BlockSpec((tm, D), lambda i, j: (i, 0)), pl.BlockSpec((D, tn), lambda i, j: (0, j)),
                  pl.BlockSpec((D, tn), lambda i, j: (0, j + nj))],
        out_specs=[out, out, out], out_shape=[shp, shp, shp],
        compiler_params=_params("parallel", "parallel"),
    )(h, w_in, w_in)


def _ffn_out_dx_act(dy, w_out, gate_vec, pg, pu, *, name):
    S, D = dy.shape
    F = w_out.shape[0]
    tm, tn = _tile(S, 512), _tile(F, 1408)

    def body(dy_ref, w_ref, gv_ref, pg_ref, pu_ref, dg_ref, du_ref):
        dyg = (dy_ref[...] * gv_ref[...]).astype(BF16)
        da = lax.dot_general(dyg, w_ref[...], _DOT_DIMS["nt"], preferred_element_type=F32)
        gate = pg_ref[...].astype(F32)
        up = pu_ref[...].astype(F32)
        sg = _sigmoid(gate)
        dg_ref[...] = (da * up * (sg * (1.0 + gate * (1.0 - sg)))).astype(BF16)
        du_ref[...] = (da * (gate * sg)).astype(BF16)

    blk = pl.BlockSpec((tm, tn), lambda i, j: (i, j))
    shp = jax.ShapeDtypeStruct((S, F), BF16)
    return pl.pallas_call(
        body, name=name, grid=(S // tm, F // tn),
        in_specs=[pl.BlockSpec((tm, D), lambda i, j: (i, 0)), pl.BlockSpec((tn, D), lambda i, j: (j, 0)),
                  pl.BlockSpec((1, D), lambda i, j: (0, 0)), blk, blk],
        out_specs=[blk, blk], out_shape=[shp, shp],
        compiler_params=_params("parallel", "parallel"),
    )(dy, w_out, gate_vec, pg, pu)


def _ffn_fwd(x, mod, gain, w_in, w_out, tag):
    sh, sc, gate = mod
    h = _norm_mod_fwd(x, gain, sc, sh, name=f"ffn_norm_{tag}")
    pg, pu, a = _ffn_in_act(h, w_in, name=f"ffn_in_{tag}")
    y = _mm_auto(a, w_out, "nn", f"ffn_out_{tag}", out_scale=gate, resid=x)
    return y, (x, h, pg, pu, a)


def _ffn_bwd(dy, saved, mod, gain, w_in, w_out, tag):
    sh, sc, gate = mod
    x, h, pg, pu, a = saved
    F = pg.shape[1]
    gmat = _mm_auto(a, dy, "tn", f"ffn_out_g_{tag}")
    dw_out, dgate = _wout_grad(gmat, w_out, gate, name=f"ffn_out_dw_{tag}")
    dpg, dpu = _ffn_out_dx_act(dy, w_out, gate, pg, pu, name=f"ffn_out_dx_{tag}")
    dw_in = jnp.concatenate([_mm_auto(h, dpg, "tn", f"ffn_in_dw_gate_{tag}", out_dtype=BF16),
                             _mm_auto(h, dpu, "tn", f"ffn_in_dw_up_{tag}", out_dtype=BF16)], axis=1)
    dh = _mm_auto(dpg, w_in, "nt", f"ffn_in_dx_gate_{tag}")
    dh = _mm_auto(dpu, w_in, "nt", f"ffn_in_dx_up_{tag}", b_k_off=F, resid=dh)
    dx, dsh, dsc, dgain = _norm_mod_bwd(dh, x, dy, gain, sc, name=f"ffn_norm_bwd_{tag}")
    return dx, dict(w_in=dw_in, w_out=dw_out, gain=dgain, mod=(dsh, dsc, dgate))


def _gdn_fwd(x, mod, gain, W):
    sh, sc, gate = mod
    S = x.shape[0]
    h = _norm_mod_fwd(x, gain, sc, sh, name="gdn_norm")
    pq = _mm_auto(h, W["gdn_qkv"], "nn", "gdn_in_qkv", out_dtype=BF16)
    z = _mm_auto(h, W["gdn_z"], "nn", "gdn_in_z", out_dtype=BF16)
    ab = _mm_auto(h, W["gdn_ab"], "nn", "gdn_in_ab")
    qkvn = _gdn_prep_fwd(pq, W["gdn_conv"], name="gdn_prep")
    ab4 = jnp.transpose(ab[:, :2 * GDN_HEADS]).reshape(2 * GDN_HEADS, S // GDN_CHUNK, 1, GDN_CHUNK)
    o, states, tinvs = _gdn_chunk_fwd(qkvn, ab4, W["gdn_a_log"], W["gdn_dt_bias"], name="gdn_chunk")
    o2 = _gdn_outnorm_fwd(o, z, W["gdn_out_norm"], name="gdn_outnorm")
    y = _mm_auto(o2, W["gdn_out"], "nn", "gdn_out", out_scale=gate, resid=x)
    return y, (x, h, pq, z, qkvn, ab4, o, states, tinvs, o2)


def _gdn_bwd(dy, saved, mod, gain, W, riding=None):
    sh, sc, gate = mod
    x, h, pq, z, qkvn, ab4, o, states, tinvs, o2 = saved
    S = x.shape[0]
    gmat = _mm_auto(o2, dy, "tn", "gdn_out_g")
    dw_out, dgate = _wout_grad(gmat, W["gdn_out"], gate, name="gdn_out_dw")
    do2 = _mm_auto(dy, W["gdn_out"], "nt", "gdn_out_dx", a_scale=gate)
    do, dz, dout_norm = _gdn_outnorm_bwd(do2, o, z, W["gdn_out_norm"], name="gdn_outnorm_bwd")
    dqkvn, dab4, da_log, ddt_bias, *rode = _gdn_chunk_bwd(
        qkvn, ab4, W["gdn_a_log"], W["gdn_dt_bias"], states, tinvs, do, name="gdn_chunk_bwd", riding=riding)
    dc, dconv8 = _gdn_prep_bwd_pre(dqkvn, pq, W["gdn_conv"], name="gdn_prep_bwd")
    dpq = _gdn_conv_bwd_x(dc, W["gdn_conv"], name="gdn_conv_bwd")
    dab = jnp.transpose(dab4.reshape(2 * GDN_HEADS, S))
    dab = jnp.pad(dab, ((0, 0), (0, LANES - 2 * GDN_HEADS))).astype(BF16)
    dw_qkv = _mm_auto(h, dpq, "tn", "gdn_in_qkv_dw", out_dtype=BF16)
    dw_z = _mm_auto(h, dz, "tn", "gdn_in_z_dw", out_dtype=BF16)
    dw_ab = _mm_auto(h, dab, "tn", "gdn_in_ab_dw", out_dtype=BF16)
    dh = _mm_auto(dpq, W["gdn_qkv"], "nt", "gdn_in_qkv_dx")
    dh = _mm_auto(dz, W["gdn_z"], "nt", "gdn_in_z_dx", resid=dh)
    dh = _mm_auto(dab, W["gdn_ab"], "nt", "gdn_in_ab_dx", resid=dh)
    dx, dsh, dsc, dgain = _norm_mod_bwd(dh, x, dy, gain, sc, name="gdn_norm_bwd")
    dw_in = jnp.concatenate([_un_hm(dw_qkv), dw_z, dw_ab[:, :2 * GDN_HEADS]], axis=1)
    return dx, dict(gdn_w_in=dw_in, gdn_conv=_un_hm(dconv8[:GDN_CONV]), gdn_w_out=dw_out, gdn_out_norm=dout_norm,
                    gdn_a_log=da_log.reshape(1, GDN_HEADS), gdn_dt_bias=ddt_bias.reshape(1, GDN_HEADS),
                    gain=dgain, mod=(dsh, dsc, dgate)), (rode[0] if rode else None)


def _dsw_fwd(x, mod, gain, W):
    sh, sc, gate = mod
    h = _norm_mod_fwd(x, gain, sc, sh, name="dsw_norm")
    q, k, v = (_mm_auto(h, W[n], "nn", f"dsw_in_{n[-1]}") for n in ("dsw_q", "dsw_k", "dsw_v"))
    outs = None
    for g in range(len(DSW_GROUPS)):
        outs = _dsw_attn_fwd(q, k, v, W["dsw_bias"][g], W["dsw_q_norm"], W["dsw_k_norm"], outs, g=g,
                             name=f"dsw_attn_{g}")
    o, lse = _dsw_merge(*outs, name="dsw_merge")
    y = _mm_auto(o, W["dsw_out"], "nn", "dsw_out", out_scale=gate, resid=x)
    return y, (x, h, q, k, v, o, lse)


def _dsw_bwd(dy, saved, mod, gain, W):
    sh, sc, gate = mod
    x, h, q, k, v, o, lse = saved
    gmat = _mm_auto(o, dy, "tn", "dsw_out_g")
    dw_out, dgate = _wout_grad(gmat, W["dsw_out"], gate, name="dsw_out_dw")
    do = _mm_auto(dy, W["dsw_out"], "nt", "dsw_out_dx", a_scale=gate)
    G = len(DSW_GROUPS)
    dqkv, dbias, dq_norm, dk_norm = None, [], 0.0, 0.0
    for g in range(G):
        *dqkv, db, dqg, dkg = _dsw_attn_bwd(q, k, v, o, lse, do, W["dsw_bias"][g], W["dsw_q_norm"],
                                            W["dsw_k_norm"], dqkv, g=g, name=f"dsw_attn_bwd_{g}")
        dbias.append(db)
        dq_norm, dk_norm = dq_norm + dqg, dk_norm + dkg
    dws, dh = [], None
    for n, d in zip(("dsw_q", "dsw_k", "dsw_v"), dqkv):
        dws.append(_mm_auto(h, d, "tn", f"dsw_in_{n[-1]}_dw", out_dtype=BF16))
        dh = _mm_auto(d, W[n], "nt", f"dsw_in_{n[-1]}_dx", **({} if dh is None else {"resid": dh}))
    dx, dsh, dsc, dgain = _norm_mod_bwd(dh, x, dy, gain, sc, name="dsw_norm_bwd")
    hot = _dsw_bucket_onehot()
    drel = [_mm_auto(dbias[g].reshape(DSW_HEADS, -1), hot[g], "nn", f"dsw_rel_bias_{g}")[:, :REL_BUCKETS]
            for g in range(G)]
    return dx, dict(dsw_w_in=jnp.concatenate(dws, axis=1), dsw_w_out=dw_out, dsw_q_norm=dq_norm,
                    dsw_k_norm=dk_norm, rel_bias=jnp.transpose(jnp.concatenate(drel, axis=0)),
                    gain=dgain, mod=(dsh, dsc, dgate))


def _local_step(x, target, mod, W, early_pairs=None):
    mods = [[_row(mod[l, i]) for i in range(6)] for l in range(2)]
    nmix = [_row(W["norm_mix"][l]) for l in range(2)]
    nffn = [_row(W["norm_ffn"][l]) for l in range(2)]
    x1, s_gdn = _gdn_fwd(x, mods[0][:3], nmix[0], W)
    x2, s_f0 = _ffn_fwd(x1, mods[0][3:], nffn[0], W["w_ffn_in"][0], W["w_ffn_out"][0], "0")
    x3, s_dsw = _dsw_fwd(x2, mods[1][:3], nmix[1], W)
    x4, s_f1 = _ffn_fwd(x3, mods[1][3:], nffn[1], W["w_ffn_in"][1], W["w_ffn_out"][1], "1")
    dx4, sse = _loss_head(x4, target, name="loss_head")
    dx3, g_f1 = _ffn_bwd(dx4, s_f1, mods[1][3:], nffn[1], W["w_ffn_in"][1], W["w_ffn_out"][1], "1")
    dx2, g_dsw = _dsw_bwd(dx3, s_dsw, mods[1][:3], nmix[1], W)
    dx1, g_f0 = _ffn_bwd(dx2, s_f0, mods[0][3:], nffn[0], W["w_ffn_in"][0], W["w_ffn_out"][0], "0")
    grads = dict(
        w_ffn_in=jnp.stack([g_f0["w_in"], g_f1["w_in"]]), w_ffn_out=jnp.stack([g_f0["w_out"], g_f1["w_out"]]),
        dsw_w_in=g_dsw["dsw_w_in"][None], dsw_w_out=g_dsw["dsw_w_out"][None])
    riding = None if early_pairs is None else early_pairs(grads)
    dx0, g_gdn, rode = _gdn_bwd(dx1, s_gdn, mods[0][:3], nmix[0], W, riding)
    dmod = jnp.stack([jnp.concatenate(list(g_gdn["mod"]) + list(g_f0["mod"]), axis=0),
                      jnp.concatenate(list(g_dsw["mod"]) + list(g_f1["mod"]), axis=0)])
    grads.update(
        norm_mix=jnp.concatenate([g_gdn["gain"], g_dsw["gain"]], axis=0),
        norm_ffn=jnp.concatenate([g_f0["gain"], g_f1["gain"]], axis=0),
        gdn_w_in=g_gdn["gdn_w_in"][None], gdn_conv=g_gdn["gdn_conv"][None], gdn_w_out=g_gdn["gdn_w_out"][None],
        gdn_out_norm=g_gdn["gdn_out_norm"], gdn_a_log=g_gdn["gdn_a_log"], gdn_dt_bias=g_gdn["gdn_dt_bias"],
        dsw_q_norm=g_dsw["dsw_q_norm"], dsw_k_norm=g_dsw["dsw_k_norm"], rel_bias=g_dsw["rel_bias"])
    return sse, dx0, grads, dmod, rode


def _prepare_weights(full, small):
    gw = full["gdn_w_in"][0]
    hk3 = 3 * GDN_HEADS * GDN_DK
    di = full["dsw_w_in"][0]
    dq = di.shape[1] // 3
    return dict(
        w_ffn_in=full["w_ffn_in"], w_ffn_out=full["w_ffn_out"],
        gdn_qkv=_hm(gw[:, :hk3]), gdn_z=gw[:, hk3:hk3 + GDN_HEADS * GDN_DK],
        gdn_ab=jnp.pad(gw[:, hk3 + GDN_HEADS * GDN_DK:], ((0, 0), (0, LANES - 2 * GDN_HEADS))),
        gdn_conv=_hm(full["gdn_conv"][0]), gdn_out=full["gdn_w_out"][0],
        dsw_q=di[:, :dq], dsw_k=di[:, dq:2 * dq], dsw_v=di[:, 2 * dq:], dsw_out=full["dsw_w_out"][0],
        norm_mix=small["norm_mix"], norm_ffn=small["norm_ffn"],
        gdn_a_log=small["gdn_a_log"].reshape(GDN_HEADS, 1, 1), gdn_dt_bias=small["gdn_dt_bias"].reshape(GDN_HEADS, 1, 1),
        gdn_out_norm=small["gdn_out_norm"], dsw_q_norm=small["dsw_q_norm"], dsw_k_norm=small["dsw_k_norm"],
        dsw_bias=_dsw_bias(small["rel_bias"]))


_W_NAMES = ("w_ada", "b_ada", "norm_mix", "norm_ffn", "w_ffn_in", "w_ffn_out", "gdn_w_in", "gdn_conv",
            "gdn_a_log", "gdn_dt_bias", "gdn_out_norm", "gdn_w_out", "dsw_w_in", "dsw_q_norm", "dsw_k_norm",
            "dsw_w_out", "rel_bias")
_PAD_BATCH = 16


def _pad_rows(a, rows):
    return jnp.pad(a, ((0, rows - a.shape[0]), (0, 0)))


def kernel(x, c, w_ada, b_ada, norm_mix, norm_ffn, w_ffn_in, w_ffn_out, gdn_w_in, gdn_conv, gdn_a_log, gdn_dt_bias, gdn_out_norm, gdn_w_out, dsw_w_in, dsw_q_norm, dsw_k_norm, dsw_w_out, rel_bias, loss_target, m_w_ada, m_b_ada, m_norm_mix, m_norm_ffn, m_w_ffn_in, m_w_ffn_out, m_gdn_w_in, m_gdn_conv, m_gdn_a_log, m_gdn_dt_bias, m_gdn_out_norm, m_gdn_w_out, m_dsw_w_in, m_dsw_q_norm, m_dsw_k_norm, m_dsw_w_out, m_rel_bias, v_w_ada, v_b_ada, v_norm_mix, v_norm_ffn, v_w_ffn_in, v_w_ffn_out, v_gdn_w_in, v_gdn_conv, v_gdn_a_log, v_gdn_dt_bias, v_gdn_out_norm, v_gdn_w_out, v_dsw_w_in, v_dsw_q_norm, v_dsw_k_norm, v_dsw_w_out, v_rel_bias):
    w = dict(zip(_W_NAMES, (w_ada, b_ada, norm_mix, norm_ffn, w_ffn_in, w_ffn_out, gdn_w_in, gdn_conv, gdn_a_log,
                            gdn_dt_bias, gdn_out_norm, gdn_w_out, dsw_w_in, dsw_q_norm, dsw_k_norm, dsw_w_out,
                            rel_bias)))
    m = dict(zip(_W_NAMES, (m_w_ada, m_b_ada, m_norm_mix, m_norm_ffn, m_w_ffn_in, m_w_ffn_out, m_gdn_w_in,
                            m_gdn_conv, m_gdn_a_log, m_gdn_dt_bias, m_gdn_out_norm, m_gdn_w_out, m_dsw_w_in,
                            m_dsw_q_norm, m_dsw_k_norm, m_dsw_w_out, m_rel_bias)))
    v = dict(zip(_W_NAMES, (v_w_ada, v_b_ada, v_norm_mix, v_norm_ffn, v_w_ffn_in, v_w_ffn_out, v_gdn_w_in,
                            v_gdn_conv, v_gdn_a_log, v_gdn_dt_bias, v_gdn_out_norm, v_gdn_w_out, v_dsw_w_in,
                            v_dsw_q_norm, v_dsw_k_norm, v_dsw_w_out, v_rel_bias)))
    D = x.shape[-1]
    n_layers, _, ada_cols = w_ada.shape

    c_all = _exchange(c.reshape(D // LANES, LANES), gather=True, name="gather_cond").reshape(N_DEV, D)
    c_pad = _pad_rows(c_all, _PAD_BATCH)
    proj = [_mm(c_pad, w_ada[l], mode="nn", name=f"ada_proj_{l}", tm=_PAD_BATCH, tn=ada_cols, tk=D, a_silu=True)
            for l in range(n_layers)]
    mod_send = _pack([(jnp.stack([p[:N_DEV] for p in proj], axis=1), 1)], _ROW_ALIGN)
    mod_recv = _exchange(mod_send, gather=False, name="scatter_mod")
    mod = _unpack(mod_recv, [(n_layers, ada_cols)], 1)[0]
    mod = jnp.transpose(mod, (1, 0, 2)).reshape(n_layers, N_DEV * ada_cols) + b_ada
    mod = mod.reshape(n_layers, 6, D)

    conv_hi = gdn_conv.astype(BF16)
    conv_lo = (gdn_conv - conv_hi.astype(F32)).astype(BF16)
    w_send = _pack([(conv_hi if n == "gdn_conv" else w[n].astype(BF16), 0) for n in _BIG] + [(conv_lo, 0)],
                   _ROW_ALIGN)
    w_all = _gather_two_level(w_send, name="gather_weights")
    parts = _unpack(w_all, [w[n].shape for n in _BIG] + [gdn_conv.shape], 1)
    full = {n: _to_natural(parts[i], _SHARD_AXIS[n]) for i, n in enumerate(_BIG)}
    full["gdn_conv"] = full["gdn_conv"].astype(F32) + _to_natural(parts[-1], _SHARD_AXIS["gdn_conv"]).astype(F32)
    W = _prepare_weights(full, {n: w[n] for n in _SMALL})

    my_c = lax.axis_index("c")

    def pair_sums(g, names, tag):
        send = _pack([(_to_blocked(g[n].astype(BF16), _SHARD_AXIS[n]), 1) for n in names], _BIG_ALIGN)
        by_core = send.reshape((N_DEV // 2, 2) + send.shape[1:])
        keep = lax.dynamic_index_in_dim(by_core, my_c, axis=1, keepdims=False)
        give = lax.dynamic_index_in_dim(by_core, 1 - my_c, axis=1, keepdims=False)
        return _add_pair(keep, _swap_with_sibling(give, name=f"swap_grads_{tag}"), name=f"add_sibling_grads_{tag}")

    sse, grad_x, grads, dmod, early_recv = _local_step(
        x[0], loss_target[0], mod, W, early_pairs=lambda g: pair_sums(g, _EARLY, "early"))
    loss = lax.psum(0.5 * sse[0, 0] / D, ("x", "y", "c"))
    grads["b_ada"] = dmod.reshape(n_layers, 6 * D)
    late_recv = _exchange_chips(pair_sums(grads, _LATE, "late"), name="scatter_grads_late")
    g_parts = dict(zip(_EARLY, _unpack(early_recv, [w[n].shape for n in _EARLY], 1)))
    g_parts.update(zip(_LATE, _unpack(late_recv, [w[n].shape for n in _LATE], 1)))

    dmod_send = _pack([(jnp.transpose(dmod.reshape(n_layers, N_DEV, ada_cols), (1, 0, 2)), 1)], _ROW_ALIGN)
    small_send = _pack([(grads[n].reshape(w[n].shape), 0) for n in _SMALL], _ROW_ALIGN)
    s_recv = _exchange(jnp.concatenate(
        [dmod_send, jnp.broadcast_to(small_send[None], (N_DEV,) + small_send.shape)], axis=1),
        gather=False, name="scatter_small")
    dmod_rows = dmod_send.shape[1]

    out = {}
    kinds = ("grad", "delta", "new_m", "new_v")
    for n in _BIG:
        g4 = g_parts[n]
        rows2d = lambda a: a.reshape((-1, w[n].shape[-1]))
        res = _adamw(rows2d(w[n]), g4.reshape((g4.shape[0], -1, w[n].shape[-1])), rows2d(m[n]), rows2d(v[n]),
                     name=f"adamw_{n}")
        for kind, buf in zip(kinds, res):
            out[kind, n] = buf.reshape(w[n].shape)

    dmod_all = _unpack(lax.slice_in_dim(s_recv, 0, dmod_rows, axis=1), [(n_layers, ada_cols)], 1)[0]
    g_ada = jnp.stack([_mm(c_pad, _pad_rows(dmod_all[:, l], _PAD_BATCH), mode="tn", name=f"ada_dw_{l}",
                           tm=D, tn=ada_cols, tk=_PAD_BATCH, a_silu=True) for l in range(n_layers)])
    flat = lambda a: a.reshape(n_layers * D, ada_cols)
    res = _adamw(flat(w_ada), flat(g_ada)[None], flat(m_w_ada), flat(v_w_ada), name="adamw_ada")
    for kind, buf in zip(("grad", "delta", "new_m", "new_v"), res):
        out[kind, "w_ada"] = buf.reshape(w_ada.shape)

    small_parts = lax.slice_in_dim(s_recv, dmod_rows, s_recv.shape[1], axis=1)
    packed = [_pack([(t[n], 0) for n in _SMALL], _ROW_ALIGN) for t in (w, m, v)]
    res = _adamw(packed[0], small_parts, packed[1], packed[2], name="adamw_replicated")
    for kind, buf in zip(("grad", "delta", "new_m", "new_v"), res):
        for n, a in zip(_SMALL, _unpack(buf, [w[n].shape for n in _SMALL], 0)):
            out[kind, n] = a

    return (loss, grad_x[None]) + tuple(out[kind, n] for kind in ("grad", "delta", "new_m", "new_v")
                                        for n in _W_NAMES)
```

```python
import functools
import math

import numpy as np
import jax
import jax.numpy as jnp
from jax import lax
from jax.experimental import pallas as pl
from jax.experimental.pallas import tpu as pltpu

F32 = jnp.float32
BF16 = jnp.bfloat16

N_DEV = 8
RMS_EPS = 1e-6
LANES = 128
V7X_VMEM_LIMIT = 48 * 1024 * 1024

GDN_HEADS = 8
GDN_DK = 128
GDN_CHUNK = 64
GDN_CONV = 4
DSW_GROUPS = ((128, 1), (512, 4), (2048, 16))
DSW_HEADS = 8
DSW_DH = 64
DSW_BLK = 128
REL_BUCKETS = 32
REL_MAX_DIST = 2048

ADAM_LR = 0.001
ADAM_B1 = 0.9
ADAM_B2 = 0.999
ADAM_EPS = 1e-08
ADAM_WD = 0.01
ADAM_STEP = 10

NEG_BIG = -1e30


def _params(*sem):
    return pltpu.CompilerParams(dimension_semantics=sem, vmem_limit_bytes=V7X_VMEM_LIMIT)


def _sigmoid(x):
    return 1.0 / (1.0 + jnp.exp(-x))


def _silu(x):
    return x * _sigmoid(x)


_DOT_DIMS = {
    "nn": (((1,), (0,)), ((), ())),
    "nt": (((1,), (1,)), ((), ())),
    "tn": (((0,), (0,)), ((), ())),
}


def _mm(a, b, *, mode, name, tm, tn, tk, out_dtype=F32, a_scale=None, out_scale=None, resid=None, a_silu=False,
        b_k_off=0):
    if mode == "nn":
        (M, K), N = a.shape, b.shape[1]
    elif mode == "nt":
        (M, K), N = a.shape, b.shape[0]
    else:
        (K, M), N = a.shape, b.shape[1]
    tm, tn, tk = min(tm, M), min(tn, N), min(tk, K)
    assert M % tm == 0 and N % tn == 0 and K % tk == 0 and b_k_off % tk == 0, (name, M, N, K, tm, tn, tk)
    assert b_k_off == 0 or mode == "nt", name
    nk = K // tk

    def body(*refs):
        refs = list(refs)
        a_ref, b_ref = refs.pop(0), refs.pop(0)
        as_ref = refs.pop(0) if a_scale is not None else None
        os_ref = refs.pop(0) if out_scale is not None else None
        r_ref = refs.pop(0) if resid is not None else None
        o_ref = refs.pop(0)
        acc_ref = refs.pop(0) if nk > 1 else None

        av = a_ref[...]
        if a_silu:
            av = _silu(av.astype(F32))
        if as_ref is not None:
            av = av.astype(F32) * as_ref[...]
        part = lax.dot_general(av.astype(BF16), b_ref[...].astype(BF16), _DOT_DIMS[mode],
                               preferred_element_type=F32)

        def finish(r):
            if os_ref is not None:
                r = r * os_ref[...]
            if r_ref is not None:
                r = r + r_ref[...].astype(F32)
            o_ref[...] = r.astype(out_dtype)

        if nk == 1:
            finish(part)
        else:
            k = pl.program_id(2)

            @pl.when(k == 0)
            def _():
                acc_ref[...] = part

            @pl.when(k > 0)
            def _():
                acc_ref[...] += part

            @pl.when(k == nk - 1)
            def _():
                finish(acc_ref[...])

    if mode == "nn":
        a_spec = pl.BlockSpec((tm, tk), lambda i, j, k: (i, k))
        b_spec = pl.BlockSpec((tk, tn), lambda i, j, k: (k, j))
        as_spec = pl.BlockSpec((1, tk), lambda i, j, k: (0, k))
    elif mode == "nt":
        a_spec = pl.BlockSpec((tm, tk), lambda i, j, k: (i, k))
        b_spec = pl.BlockSpec((tn, tk), lambda i, j, k: (j, k + b_k_off // tk))
        as_spec = pl.BlockSpec((1, tk), lambda i, j, k: (0, k))
    else:
        a_spec = pl.BlockSpec((tk, tm), lambda i, j, k: (k, i))
        b_spec = pl.BlockSpec((tk, tn), lambda i, j, k: (k, j))
        as_spec = None
    in_specs, args = [a_spec, b_spec], [a, b]
    if a_scale is not None:
        in_specs.append(as_spec)
        args.append(a_scale)
    if out_scale is not None:
        in_specs.append(pl.BlockSpec((1, tn), lambda i, j, k: (0, j)))
        args.append(out_scale)
    if resid is not None:
        in_specs.append(pl.BlockSpec((tm, tn), lambda i, j, k: (i, j)))
        args.append(resid)
    return pl.pallas_call(
        body, name=name, grid=(M // tm, N // tn, nk),
        in_specs=in_specs, out_specs=pl.BlockSpec((tm, tn), lambda i, j, k: (i, j)),
        out_shape=jax.ShapeDtypeStruct((M, N), out_dtype),
        scratch_shapes=[pltpu.VMEM((tm, tn), F32)] if nk > 1 else [],
        compiler_params=_params("parallel", "parallel", "arbitrary"),
    )(*args)


def _norm_mod_fwd(x, gain, sc, sh, *, name):
    S, D = x.shape
    tr = min(512, S)

    def body(x_ref, g_ref, sc_ref, sh_ref, h_ref):
        xv = x_ref[...]
        r = lax.rsqrt(jnp.mean(xv * xv, axis=-1, keepdims=True) + RMS_EPS)
        h_ref[...] = ((xv * r) * g_ref[...] * (1.0 + sc_ref[...]) + sh_ref[...]).astype(BF16)

    row = pl.BlockSpec((tr, D), lambda i: (i, 0))
    vec = pl.BlockSpec((1, D), lambda i: (0, 0))
    return pl.pallas_call(
        body, name=name, grid=(S // tr,), in_specs=[row, vec, vec, vec], out_specs=row,
        out_shape=jax.ShapeDtypeStruct((S, D), BF16), compiler_params=_params("parallel"),
    )(x, gain, sc, sh)


def _norm_mod_bwd(dh, x, dx_res, gain, sc, *, name):
    S, D = x.shape
    tr = min(256, S)
    n_steps = S // tr

    def body(dh_ref, x_ref, dxr_ref, g_ref, sc_ref, dx_ref, dsh_ref, dsc_ref, dgain_ref, acc_sh, acc_a):
        i = pl.program_id(0)
        xv = x_ref[...]
        r = lax.rsqrt(jnp.mean(xv * xv, axis=-1, keepdims=True) + RMS_EPS)
        n = xv * r
        dhv = dh_ref[...].astype(F32)
        dn = dhv * (g_ref[...] * (1.0 + sc_ref[...]))
        dx_ref[...] = dxr_ref[...] + r * (dn - n * jnp.mean(dn * n, axis=-1, keepdims=True))
        p_sh = jnp.sum(dhv, axis=0, keepdims=True)
        p_a = jnp.sum(dhv * n, axis=0, keepdims=True)

        @pl.when(i == 0)
        def _():
            acc_sh[...] = p_sh
            acc_a[...] = p_a

        @pl.when(i > 0)
        def _():
            acc_sh[...] += p_sh
            acc_a[...] += p_a

        @pl.when(i == n_steps - 1)
        def _():
            dsh_ref[...] = acc_sh[...]
            dsc_ref[...] = acc_a[...] * g_ref[...]
            dgain_ref[...] = acc_a[...] * (1.0 + sc_ref[...])

    row = pl.BlockSpec((tr, D), lambda i: (i, 0))
    vec = pl.BlockSpec((1, D), lambda i: (0, 0))
    vshape = jax.ShapeDtypeStruct((1, D), F32)
    return pl.pallas_call(
        body, name=name, grid=(n_steps,), in_specs=[row, row, row, vec, vec],
        out_specs=[row, vec, vec, vec],
        out_shape=[jax.ShapeDtypeStruct((S, D), F32), vshape, vshape, vshape],
        scratch_shapes=[pltpu.VMEM((1, D), F32), pltpu.VMEM((1, D), F32)],
        compiler_params=_params("arbitrary"),
    )(dh, x, dx_res, gain, sc)


def _wout_grad(gmat, w, gate, *, name):
    K, D = w.shape
    tr = min(256, K)
    n_steps = K // tr

    def body(g_ref, w_ref, gate_ref, dw_ref, dgate_ref, acc):
        i = pl.program_id(0)
        gv = g_ref[...]
        dw_ref[...] = (gv * gate_ref[...]).astype(BF16)
        part = jnp.sum(gv * w_ref[...], axis=0, keepdims=True)

        @pl.when(i == 0)
        def _():
            acc[...] = part

        @pl.when(i > 0)
        def _():
            acc[...] += part

        @pl.when(i == n_steps - 1)
        def _():
            dgate_ref[...] = acc[...]

    row = pl.BlockSpec((tr, D), lambda i: (i, 0))
    vec = pl.BlockSpec((1, D), lambda i: (0, 0))
    return pl.pallas_call(
        body, name=name, grid=(n_steps,), in_specs=[row, row, vec], out_specs=[row, vec],
        out_shape=[jax.ShapeDtypeStruct((K, D), BF16), jax.ShapeDtypeStruct((1, D), F32)],
        scratch_shapes=[pltpu.VMEM((1, D), F32)], compiler_params=_params("arbitrary"),
    )(gmat, w, gate)


def _loss_head(y, target, *, name):
    S, D = y.shape
    tr = min(512, S)
    n_steps = S // tr

    def body(y_ref, t_ref, dy_ref, sse_ref, acc):
        i = pl.program_id(0)
        e = y_ref[...] - t_ref[...]
        dy_ref[...] = e * (1.0 / D)
        part = jnp.sum(e * e, axis=0, keepdims=True)

        @pl.when(i == 0)
        def _():
            acc[...] = part

        @pl.when(i > 0)
        def _():
            acc[...] += part

        @pl.when(i == n_steps - 1)
        def _():
            sse_ref[...] = jnp.sum(acc[...], axis=1, keepdims=True)

    row = pl.BlockSpec((tr, D), lambda i: (i, 0))
    return pl.pallas_call(
        body, name=name, grid=(n_steps,), in_specs=[row, row],
        out_specs=[row, pl.BlockSpec((1, 1), lambda i: (0, 0))],
        out_shape=[jax.ShapeDtypeStruct((S, D), F32), jax.ShapeDtypeStruct((1, 1), F32)],
        scratch_shapes=[pltpu.VMEM((1, D), F32)], compiler_params=_params("arbitrary"),
    )(y, target)


def _adamw(w, g_parts, m, v, *, name):
    R, C = w.shape
    P = g_parts.shape[0]
    tr = _tile(R, max(8, 1024 * LANES // C))
    c1 = 1.0 / (1.0 - ADAM_B1 ** ADAM_STEP)
    c2 = 1.0 / (1.0 - ADAM_B2 ** ADAM_STEP)

    def body(w_ref, g_ref, m_ref, v_ref, go_ref, d_ref, mo_ref, vo_ref):
        g = g_ref[0].astype(F32)
        for q in range(1, P):
            g = g + g_ref[q].astype(F32)
        mn = ADAM_B1 * m_ref[...] + (1.0 - ADAM_B1) * g
        vn = ADAM_B2 * v_ref[...] + (1.0 - ADAM_B2) * (g * g)
        go_ref[...] = g
        mo_ref[...] = mn
        vo_ref[...] = vn
        d_ref[...] = -ADAM_LR * ((mn * c1) / (jnp.sqrt(vn * c2) + ADAM_EPS) + ADAM_WD * w_ref[...])

    row = pl.BlockSpec((tr, C), lambda i: (i, 0))
    shp = jax.ShapeDtypeStruct((R, C), F32)
    return pl.pallas_call(
        body, name=name, grid=(R // tr,),
        in_specs=[row, pl.BlockSpec((P, tr, C), lambda i: (0, i, 0)), row, row],
        out_specs=[row, row, row, row], out_shape=[shp, shp, shp, shp],
        compiler_params=_params("parallel"),
    )(w, g_parts, m, v)


_HALO = 16


def _conv_taps(buf, w_ref, rows, cols):
    acc = None
    for j in range(GDN_CONV):
        term = buf[pl.ds(_HALO - (GDN_CONV - 1) + j, rows), cols] * w_ref[j:j + 1, cols]
        acc = term if acc is None else acc + term
    return acc


def _fill_conv_buf(buf, halo_ref, x_ref, rows, first):
    buf[0:_HALO, :] = jnp.where(first, 0.0, halo_ref[...].astype(F32))
    buf[_HALO:_HALO + rows, :] = x_ref[...].astype(F32)


_HM = 3 * GDN_DK
_GDN_ROWS = 256
_PREP_HEADS = 4


def _l2n(seg):
    return lax.rsqrt(jnp.sum(seg * seg, axis=-1, keepdims=True) + RMS_EPS)


def _head_cols(hh):
    return slice(hh * _HM, (hh + 1) * _HM)


def _gdn_prep_fwd(x, conv_w, *, name):
    S, C3 = x.shape
    CB = _PREP_HEADS * _HM
    RB = min(256, S)

    def body(x_ref, halo_ref, w_ref, o_ref, buf):
        i = pl.program_id(0)
        _fill_conv_buf(buf, halo_ref, x_ref, RB, i == 0)
        for hh in range(_PREP_HEADS):
            c0 = hh * _HM
            y = _silu(_conv_taps(buf, w_ref, RB, _head_cols(hh)))
            q, k = y[:, :GDN_DK], y[:, GDN_DK:2 * GDN_DK]
            o_ref[:, c0:c0 + GDN_DK] = q * (_l2n(q) * GDN_DK ** -0.5)
            o_ref[:, c0 + GDN_DK:c0 + 2 * GDN_DK] = k * _l2n(k)
            o_ref[:, c0 + 2 * GDN_DK:c0 + _HM] = y[:, 2 * GDN_DK:]

    hb = RB // _HALO
    return pl.pallas_call(
        body, name=name, grid=(S // RB, C3 // CB),
        in_specs=[pl.BlockSpec((RB, CB), lambda i, j: (i, j)),
                  pl.BlockSpec((_HALO, CB), lambda i, j: (jnp.maximum(i * hb - 1, 0), j)),
                  pl.BlockSpec((GDN_CONV, CB), lambda i, j: (0, j))],
        out_specs=pl.BlockSpec((RB, CB), lambda i, j: (i, j)),
        out_shape=jax.ShapeDtypeStruct((S, C3), F32),
        scratch_shapes=[pltpu.VMEM((RB + _HALO, CB), F32)],
        compiler_params=_params("parallel", "parallel"),
    )(x, x, conv_w)


def _gdn_prep_bwd_pre(dn, x, conv_w, *, name):
    S, C3 = x.shape
    CB = _PREP_HEADS * _HM
    RB = min(256, S)
    n_steps = S // RB

    def body(dn_ref, x_ref, halo_ref, w_ref, dc_ref, dw_ref, buf):
        i = pl.program_id(1)
        _fill_conv_buf(buf, halo_ref, x_ref, RB, i == 0)
        head_parts = []
        for hh in range(_PREP_HEADS):
            c0, cols = hh * _HM, _head_cols(hh)
            acc = _conv_taps(buf, w_ref, RB, cols)
            sg = _sigmoid(acc)
            y = acc * sg
            dsilu = sg * (1.0 + acc * (1.0 - sg))
            for part, scale in ((0, GDN_DK ** -0.5), (1, 1.0)):
                sl = slice(part * GDN_DK, (part + 1) * GDN_DK)
                seg = y[:, sl]
                r = _l2n(seg)
                n = seg * r
                d = dn_ref[:, c0 + part * GDN_DK:c0 + (part + 1) * GDN_DK] * scale
                dc_ref[:, c0 + part * GDN_DK:c0 + (part + 1) * GDN_DK] = (
                    r * (d - n * jnp.sum(d * n, axis=-1, keepdims=True)) * dsilu[:, sl])
            dc_ref[:, c0 + 2 * GDN_DK:c0 + _HM] = dn_ref[:, c0 + 2 * GDN_DK:c0 + _HM] * dsilu[:, 2 * GDN_DK:]
            dc = dc_ref[:, cols]
            taps = [jnp.sum(dc * buf[pl.ds(_HALO - (GDN_CONV - 1) + t, RB), cols], axis=0, keepdims=True)
                    for t in range(GDN_CONV)]
            head_parts.append(jnp.concatenate(taps + [jnp.zeros((8 - GDN_CONV, _HM), F32)], axis=0))
        part = jnp.concatenate(head_parts, axis=1)

        @pl.when(i == 0)
        def _():
            dw_ref[...] = part

        @pl.when(i > 0)
        def _():
            dw_ref[...] += part

    hb = RB // _HALO
    return pl.pallas_call(
        body, name=name, grid=(C3 // CB, n_steps),
        in_specs=[pl.BlockSpec((RB, CB), lambda j, i: (i, j)),
                  pl.BlockSpec((RB, CB), lambda j, i: (i, j)),
                  pl.BlockSpec((_HALO, CB), lambda j, i: (jnp.maximum(i * hb - 1, 0), j)),
                  pl.BlockSpec((GDN_CONV, CB), lambda j, i: (0, j))],
        out_specs=[pl.BlockSpec((RB, CB), lambda j, i: (i, j)),
                   pl.BlockSpec((8, CB), lambda j, i: (0, j))],
        out_shape=[jax.ShapeDtypeStruct((S, C3), F32), jax.ShapeDtypeStruct((8, C3), F32)],
        scratch_shapes=[pltpu.VMEM((RB + _HALO, CB), F32)],
        compiler_params=_params("parallel", "arbitrary"),
    )(dn, x, x, conv_w)


def _gdn_conv_bwd_x(dc, conv_w, *, name):
    S, C3 = dc.shape
    CB = _PREP_HEADS * _HM
    RB = min(256, S)
    n_steps = S // RB

    def body(dc_ref, halo_ref, w_ref, dx_ref, buf):
        i = pl.program_id(0)
        buf[0:RB, :] = dc_ref[...]
        buf[RB:RB + _HALO, :] = jnp.where(i == n_steps - 1, 0.0, halo_ref[...])
        for hh in range(_PREP_HEADS):
            cols = _head_cols(hh)
            acc = None
            for j in range(GDN_CONV):
                term = buf[pl.ds(GDN_CONV - 1 - j, RB), cols] * w_ref[j:j + 1, cols]
                acc = term if acc is None else acc + term
            dx_ref[:, cols] = acc.astype(BF16)

    hb = RB // _HALO
    last = S // _HALO - 1
    return pl.pallas_call(
        body, name=name, grid=(n_steps, C3 // CB),
        in_specs=[pl.BlockSpec((RB, CB), lambda i, j: (i, j)),
                  pl.BlockSpec((_HALO, CB), lambda i, j: (jnp.minimum((i + 1) * hb, last), j)),
                  pl.BlockSpec((GDN_CONV, CB), lambda i, j: (0, j))],
        out_specs=pl.BlockSpec((RB, CB), lambda i, j: (i, j)),
        out_shape=jax.ShapeDtypeStruct((S, C3), BF16),
        scratch_shapes=[pltpu.VMEM((RB + _HALO, CB), F32)],
        compiler_params=_params("parallel", "parallel"),
    )(dc, dc, conv_w)


def _split_bf16(a):
    hi = a.astype(BF16)
    return hi, (a - hi.astype(F32)).astype(BF16)


def _dot(a, b, dims="nn", exact=False):
    def dot(p, q):
        return lax.dot_general(p, q, _DOT_DIMS[dims], preferred_element_type=F32)

    if exact:
        (ah, al), (bh, bl) = _split_bf16(a), _split_bf16(b)
        return dot(ah, bh) + (dot(ah, bl) + dot(al, bh))
    return dot(a.astype(BF16), b.astype(BF16))


def _softplus(x):
    return jnp.maximum(x, 0.0) + jnp.log(1.0 + jnp.exp(-jnp.abs(x)))


def _to_col(row, eye):
    return jnp.sum(jnp.where(eye, row, 0.0), axis=1, keepdims=True)


def _to_row(col, eye):
    return jnp.sum(jnp.where(eye, col, 0.0), axis=0, keepdims=True)


def _unit_lower_inverse(low, ri, ci):
    n = range(len(low))
    C = low[0].shape[0]
    eye = jnp.where(ri == ci, 1.0, 0.0)
    pair = (ri >> 1) == (ci >> 1)
    x = [eye - jnp.where(pair, low[j], 0.0) for j in n]
    m, sh = 2, 1
    while m < C:
        join = ((ri >> (sh + 1)) == (ci >> (sh + 1))) & (((ri >> sh) & 1) == 1) & (((ci >> sh) & 1) == 0)
        y = [_dot(x[j], jnp.where(join, low[j], 0.0)) for j in n]
        x = [x[j] - _dot(y[j], x[j]) for j in n]
        m, sh = 2 * m, sh + 1
    lx = [_dot(low[j], x[j], exact=True) for j in n]
    corr = [_dot(x[j], eye - x[j] - lx[j]) for j in n]
    return [x[j] + corr[j] for j in n]


def _gdn_local_batch(qkv, g_row, beta_row, ri, ci):
    n = range(len(qkv))
    eye, tril, strict = ri == ci, ri >= ci, ri > ci
    q = [qkv[j][:, :GDN_DK] for j in n]
    k = [qkv[j][:, GDN_DK:2 * GDN_DK] for j in n]
    v = [qkv[j][:, 2 * GDN_DK:] for j in n]
    g_col = [_to_col(g_row[j], eye) for j in n]
    beta_col = [_to_col(beta_row[j], eye) for j in n]
    gc_col = [jnp.sum(jnp.where(tril, g_row[j], 0.0), axis=1, keepdims=True) for j in n]
    gc_row = [jnp.sum(jnp.where(ri <= ci, g_col[j], 0.0), axis=0, keepdims=True) for j in n]
    g_last = [jnp.sum(g_row[j], axis=1, keepdims=True) for j in n]
    decay = [jnp.where(tril, jnp.exp(jnp.minimum(gc_col[j] - gc_row[j], 0.0)), 0.0) for j in n]
    e_col = [jnp.exp(gc_col[j]) for j in n]
    f_col = [jnp.exp(g_last[j] - gc_col[j]) for j in n]
    e_last = [jnp.exp(g_last[j]) for j in n]
    kb = [k[j] * beta_col[j] for j in n]
    vb = [v[j] * beta_col[j] for j in n]
    kk = [_dot(kb[j], k[j], "nt") for j in n]
    qk = [_dot(q[j], k[j], "nt") for j in n]
    low = [jnp.where(strict, kk[j] * decay[j], 0.0) for j in n]
    att = [qk[j] * decay[j] for j in n]
    return dict(q=q, k=k, v=v, beta_col=beta_col, decay=decay, e_col=e_col, f_col=f_col, e_last=e_last,
                kb=kb, vb=vb, low=low, att=att, eye=eye, strict=strict, tril=tril)


def _chunk_iotas():
    C = GDN_CHUNK
    return lax.broadcasted_iota(jnp.int32, (C, C), 0), lax.broadcasted_iota(jnp.int32, (C, C), 1)


def _gdn_chunk_fwd(qkv, ab, a_log, dt_bias, *, name, riding=None):
    S = qkv.shape[0]
    H, C, DK = GDN_HEADS, GDN_CHUNK, GDN_DK
    RB = min(_GDN_ROWS, S)
    NCB, NB, NC = RB // C, S // RB, S // C
    heads = range(H)

    def body(qkv_ref, ab_ref, alog_ref, dtb_ref, *rest):
        if riding is None:
            o_ref, st_ref, t_ref, state, u_s, w_s, qe_s, kf_s, att_s = rest
        else:
            ride_src, o_ref, st_ref, t_ref, ride_dst, state, u_s, w_s, qe_s, kf_s, att_s, *ride_sems = rest
        nb = pl.program_id(0)
        if riding is not None:
            finish_ride = _ride(nb == 0, nb == NB - 1, ride_src, ride_dst, ride_sems, True)

        @pl.when(nb == 0)
        def _():
            state[...] = jnp.zeros_like(state)

        ri, ci = _chunk_iotas()
        neg_a = [-jnp.exp(alog_ref[h]) for h in heads]
        e_last = []
        for c in range(NCB):
            rows = pl.ds(c * C, C)
            g_row = [neg_a[h] * _softplus(ab_ref[h, c] + dtb_ref[h]) for h in heads]
            beta_row = [_sigmoid(ab_ref[H + h, c]) for h in heads]
            L = _gdn_local_batch([qkv_ref[rows, h * _HM:(h + 1) * _HM] for h in heads], g_row, beta_row, ri, ci)
            tinv = _unit_lower_inverse(L["low"], ri, ci)
            u = [_dot(tinv[h], L["vb"][h], exact=True) for h in heads]
            w = [_dot(tinv[h], L["kb"][h] * L["e_col"][h], exact=True) for h in heads]
            for h in heads:
                t_ref[h, c] = tinv[h]
                u_s[c, h] = u[h]
                w_s[c, h] = w[h].astype(BF16)
                qe_s[c, h] = (L["q"][h] * L["e_col"][h]).astype(BF16)
                kf_s[c, h] = (L["k"][h] * L["f_col"][h]).astype(BF16)
                att_s[c, h] = L["att"][h].astype(BF16)
            e_last.append(L["e_last"])
        st = [state[h] for h in heads]
        for c in range(NCB):
            rows = pl.ds(c * C, C)
            stb = [st[h].astype(BF16) for h in heads]
            vn = [u_s[c, h] - _dot(w_s[c, h], stb[h]) for h in heads]
            vnb = [vn[h].astype(BF16) for h in heads]
            out = [_dot(qe_s[c, h], stb[h]) + _dot(att_s[c, h], vnb[h]) for h in heads]
            new = [st[h] * e_last[c][h] + _dot(kf_s[c, h], vnb[h], "tn") for h in heads]
            for h in heads:
                o_ref[rows, h * DK:(h + 1) * DK] = out[h]
                st_ref[h, c] = st[h]
            st = new
        for h in heads:
            state[h] = st[h]
        if riding is not None:
            finish_ride()

    ride_args, ride_specs, ride_out, ride_scratch = _riding(riding, True)
    return pl.pallas_call(
        body, name=name, grid=(NB,),
        in_specs=[pl.BlockSpec((RB, H * _HM), lambda n: (n, 0)),
                  pl.BlockSpec((2 * H, NCB, 1, C), lambda n: (0, n, 0, 0)),
                  pl.BlockSpec((H, 1, 1), lambda n: (0, 0, 0)),
                  pl.BlockSpec((H, 1, 1), lambda n: (0, 0, 0))] + ride_specs,
        out_specs=[pl.BlockSpec((RB, H * DK), lambda n: (n, 0)),
                   pl.BlockSpec((H, NCB, DK, DK), lambda n: (0, n, 0, 0)),
                   pl.BlockSpec((H, NCB, C, C), lambda n: (0, n, 0, 0))] + ride_specs,
        out_shape=[jax.ShapeDtypeStruct((S, H * DK), F32),
                   jax.ShapeDtypeStruct((H, NC, DK, DK), F32),
                   jax.ShapeDtypeStruct((H, NC, C, C), F32)] + ride_out,
        scratch_shapes=[pltpu.VMEM((H, DK, DK), F32), pltpu.VMEM((NCB, H, C, DK), F32),
                        pltpu.VMEM((NCB, H, C, DK), BF16), pltpu.VMEM((NCB, H, C, DK), BF16),
                        pltpu.VMEM((NCB, H, C, DK), BF16), pltpu.VMEM((NCB, H, C, C), BF16)] + ride_scratch,
        compiler_params=_params("arbitrary"),
    )(qkv, ab, a_log, dt_bias, *ride_args)


def _chip_copies(src_ref, dst_ref, send_sems, recv_sems, local_sem, gather=False):
    x, y, c = lax.axis_index("x"), lax.axis_index("y"), lax.axis_index("c")
    here = 2 * x + y
    landing = dst_ref.at[here, c] if gather else dst_ref.at[here]
    mine = pltpu.make_async_copy(src_ref if gather else src_ref.at[here], landing, local_sem)
    copies = []
    for rel in range(1, N_DEV // 2):
        px = 1 - x if rel & 2 else x
        py = 1 - y if rel & 1 else y
        copies.append(pltpu.make_async_remote_copy(
            src_ref=src_ref if gather else src_ref.at[2 * px + py], dst_ref=landing,
            send_sem=send_sems.at[rel - 1], recv_sem=recv_sems.at[rel - 1],
            device_id=(px, py, c), device_id_type=pl.DeviceIdType.MESH))
    return mine, copies


def _riding(riding, gather):
    if riding is None:
        return [], [], [], []
    shape = (N_DEV // 2, 2) + riding.shape if gather else riding.shape
    n_peers = N_DEV // 2 - 1
    return ([riding], [pl.BlockSpec(memory_space=pl.ANY)], [jax.ShapeDtypeStruct(shape, riding.dtype)],
            [pltpu.SemaphoreType.DMA((n_peers,)), pltpu.SemaphoreType.DMA((n_peers,)), pltpu.SemaphoreType.DMA(())])


def _ride(first, last, src, dst, sems, gather):
    @pl.when(first)
    def _():
        mine, copies = _chip_copies(src, dst, *sems, gather=gather)
        mine.start()
        for cp in copies:
            cp.start()

    def finish():
        @pl.when(last)
        def _():
            mine, copies = _chip_copies(src, dst, *sems, gather=gather)
            for cp in copies:
                cp.wait()
            mine.wait()

    return finish


def _gdn_chunk_bwd(qkv, ab, a_log, dt_bias, states, tinvs, do, *, name, riding=None):
    S = qkv.shape[0]
    H, C, DK = GDN_HEADS, GDN_CHUNK, GDN_DK
    RB = min(_GDN_ROWS, S)
    NCB, NB, NC = RB // C, S // RB, S // C
    heads = range(H)

    def body(qkv_ref, ab_ref, alog_ref, dtb_ref, st_ref, t_ref, do_ref, *rest):
        if riding is None:
            dqkv_ref, dab_ref, dalog_ref, ddtb_ref, dstate, w_s, vn_s, qe_s, kf_s, att_s, dvn_s, dkf_s = rest
        else:
            (ride_src, dqkv_ref, dab_ref, dalog_ref, ddtb_ref, ride_dst,
             dstate, w_s, vn_s, qe_s, kf_s, att_s, dvn_s, dkf_s, *ride_sems) = rest
        nb = pl.program_id(0)
        if riding is not None:
            finish_ride = _ride(nb == 0, nb == NB - 1, ride_src, ride_dst, ride_sems, False)

        @pl.when(nb == 0)
        def _():
            dstate[...] = jnp.zeros_like(dstate)
            dalog_ref[...] = jnp.zeros_like(dalog_ref)
            ddtb_ref[...] = jnp.zeros_like(ddtb_ref)

        ri, ci = _chunk_iotas()
        neg_a = [-jnp.exp(alog_ref[h]) for h in heads]

        def local(c):
            rows = pl.ds(c * C, C)
            a_pre = [ab_ref[h, c] + dtb_ref[h] for h in heads]
            g_row = [neg_a[h] * _softplus(a_pre[h]) for h in heads]
            beta_row = [_sigmoid(ab_ref[H + h, c]) for h in heads]
            L = _gdn_local_batch([qkv_ref[rows, h * _HM:(h + 1) * _HM] for h in heads], g_row, beta_row, ri, ci)
            return L, a_pre, g_row, beta_row

        e_last = [None] * NCB
        for c in range(NCB):
            L, _, _, _ = local(c)
            kbe = [L["kb"][h] * L["e_col"][h] for h in heads]
            u = [_dot(t_ref[h, c], L["vb"][h], exact=True) for h in heads]
            w = [_dot(t_ref[h, c], kbe[h], exact=True) for h in heads]
            vn = [u[h] - _dot(w[h], st_ref[h, c]) for h in heads]
            for h in heads:
                w_s[c, h] = w[h].astype(BF16)
                vn_s[c, h] = vn[h].astype(BF16)
                qe_s[c, h] = (L["q"][h] * L["e_col"][h]).astype(BF16)
                kf_s[c, h] = (L["k"][h] * L["f_col"][h]).astype(BF16)
                att_s[c, h] = L["att"][h].astype(BF16)
            e_last[c] = L["e_last"]

        dst = [dstate[h] for h in heads]
        de_last = [None] * NCB
        for c in reversed(range(NCB)):
            rows = pl.ds(c * C, C)
            dob = [do_ref[rows, h * DK:(h + 1) * DK].astype(BF16) for h in heads]
            dstb = [dst[h].astype(BF16) for h in heads]
            dvn = [_dot(att_s[c, h], dob[h], "tn") + _dot(kf_s[c, h], dstb[h]) for h in heads]
            dkf = [_dot(vn_s[c, h], dstb[h], "nt") for h in heads]
            de_last[c] = [jnp.sum(jnp.sum(dst[h] * st_ref[h, c], axis=1, keepdims=True), axis=0, keepdims=True)
                          for h in heads]
            new = [dst[h] * e_last[c][h] + _dot(qe_s[c, h], dob[h], "tn")
                   - _dot(w_s[c, h], dvn[h].astype(BF16), "tn") for h in heads]
            for h in heads:
                dvn_s[c, h] = dvn[h]
                dkf_s[c, h] = dkf[h]
            dst = new
        for h in heads:
            dstate[h] = dst[h]

        for c in range(NCB):
            rows = pl.ds(c * C, C)
            L, a_pre, g_row, beta_row = local(c)
            q, k, v, kb, vb = L["q"], L["k"], L["v"], L["kb"], L["vb"]
            e_col, f_col, decay, beta_col = L["e_col"], L["f_col"], L["decay"], L["beta_col"]
            eye, strict, tril = L["eye"], L["strict"], L["tril"]
            tinv = [t_ref[h, c] for h in heads]
            stb = [st_ref[h, c].astype(BF16) for h in heads]
            dov = [do_ref[rows, h * DK:(h + 1) * DK] for h in heads]
            dvn = [dvn_s[c, h] for h in heads]
            dkf = [dkf_s[c, h] for h in heads]
            kbe = [kb[h] * e_col[h] for h in heads]
            datt = [jnp.where(tril, _dot(dov[h], vn_s[c, h], "nt"), 0.0) for h in heads]
            dqe = [_dot(dov[h], stb[h], "nt") for h in heads]
            dw = [-_dot(dvn[h], stb[h], "nt") for h in heads]
            dt = [_dot(dvn[h], vb[h], "nt") + _dot(dw[h], kbe[h], "nt") for h in heads]
            dvb = [_dot(tinv[h], dvn[h], "tn", exact=True) for h in heads]
            dkbe = [_dot(tinv[h], dw[h], "tn", exact=True) for h in heads]
            tdt = [_dot(tinv[h], dt[h], "tn", exact=True) for h in heads]
            dlow = [-jnp.where(strict, _dot(tdt[h], tinv[h], "nt", exact=True), 0.0) for h in heads]
            dkk = [dlow[h] * decay[h] for h in heads]
            dqk = [datt[h] * decay[h] for h in heads]
            dkb = [_dot(dkk[h], k[h]) + dkbe[h] * e_col[h] for h in heads]
            dk = [_dot(dkk[h], kb[h], "tn") + _dot(dqk[h], q[h], "tn") + dkf[h] * f_col[h] + dkb[h] * beta_col[h]
                  for h in heads]
            dq = [_dot(dqk[h], k[h]) + dqe[h] * e_col[h] for h in heads]
            for h in heads:
                dqkv_ref[rows, h * _HM:h * _HM + DK] = dq[h]
                dqkv_ref[rows, h * _HM + DK:h * _HM + 2 * DK] = dk[h]
                dqkv_ref[rows, h * _HM + 2 * DK:(h + 1) * _HM] = dvb[h] * beta_col[h]

            dbeta_col = [jnp.sum(k[h] * dkb[h] + v[h] * dvb[h], axis=1, keepdims=True) for h in heads]
            pmat = [dlow[h] * L["low"][h] + datt[h] * L["att"][h] for h in heads]
            df_col = [jnp.sum(k[h] * dkf[h], axis=1, keepdims=True) * f_col[h] for h in heads]
            dgc_col = [jnp.sum(pmat[h], axis=1, keepdims=True)
                       + jnp.sum(q[h] * dqe[h] + kb[h] * dkbe[h], axis=1, keepdims=True) * e_col[h] - df_col[h]
                       for h in heads]
            dgc_row = [_to_row(dgc_col[h], eye) - jnp.sum(pmat[h], axis=0, keepdims=True) for h in heads]
            dg_last = [jnp.sum(df_col[h], axis=0, keepdims=True) + de_last[c][h] * L["e_last"][h] for h in heads]
            dgc_c = [_to_col(dgc_row[h], eye) for h in heads]
            dg_row = [jnp.sum(jnp.where(ri >= ci, dgc_c[h], 0.0), axis=0, keepdims=True) + dg_last[h] for h in heads]
            dbeta_row = [_to_row(dbeta_col[h], eye) for h in heads]
            for h in heads:
                da_row = dg_row[h] * neg_a[h] * _sigmoid(a_pre[h])
                dab_ref[h, c] = da_row
                dab_ref[H + h, c] = dbeta_row[h] * beta_row[h] * (1.0 - beta_row[h])
                dalog_ref[h] += jnp.sum(dg_row[h] * g_row[h], axis=1, keepdims=True)
                ddtb_ref[h] += jnp.sum(da_row, axis=1, keepdims=True)

        if riding is not None:
            finish_ride()

    rev = lambda n: NB - 1 - n
    vec = pl.BlockSpec((H, 1, 1), lambda n: (0, 0, 0))
    gates = pl.BlockSpec((2 * H, NCB, 1, C), lambda n: (0, rev(n), 0, 0))
    wide = pl.BlockSpec((RB, H * _HM), lambda n: (rev(n), 0))
    item = lambda dt: pltpu.VMEM((NCB, H, C, DK), dt)
    ride_args, ride_specs, ride_out, ride_scratch = _riding(riding, False)
    return pl.pallas_call(
        body, name=name, grid=(NB,),
        in_specs=[wide, gates, vec, vec,
                  pl.BlockSpec((H, NCB, DK, DK), lambda n: (0, rev(n), 0, 0)),
                  pl.BlockSpec((H, NCB, C, C), lambda n: (0, rev(n), 0, 0)),
                  pl.BlockSpec((RB, H * DK), lambda n: (rev(n), 0))] + ride_specs,
        out_specs=[wide, gates, vec, vec] + ride_specs,
        out_shape=[jax.ShapeDtypeStruct((S, H * _HM), F32),
                   jax.ShapeDtypeStruct((2 * H, NC, 1, C), F32),
                   jax.ShapeDtypeStruct((H, 1, 1), F32),
                   jax.ShapeDtypeStruct((H, 1, 1), F32)] + ride_out,
        scratch_shapes=[pltpu.VMEM((H, DK, DK), F32), item(BF16), item(BF16), item(BF16), item(BF16),
                        pltpu.VMEM((NCB, H, C, C), BF16), item(F32), item(F32)] + ride_scratch,
        compiler_params=_params("arbitrary"),
    )(qkv, ab, a_log, dt_bias, states, tinvs, do, *ride_args)


def _gdn_outnorm_fwd(o, z, gain, *, name):
    S, HV = o.shape
    RB = min(256, S)

    def body(o_ref, z_ref, g_ref, y_ref):
        for h in range(HV // GDN_DK):
            cols = slice(h * GDN_DK, (h + 1) * GDN_DK)
            ov = o_ref[:, cols]
            r = lax.rsqrt(jnp.mean(ov * ov, axis=-1, keepdims=True) + RMS_EPS)
            y_ref[:, cols] = (ov * r * g_ref[...] * _silu(z_ref[:, cols].astype(F32))).astype(BF16)

    blk = pl.BlockSpec((RB, HV), lambda i: (i, 0))
    return pl.pallas_call(
        body, name=name, grid=(S // RB,),
        in_specs=[blk, blk, pl.BlockSpec((1, GDN_DK), lambda i: (0, 0))], out_specs=blk,
        out_shape=jax.ShapeDtypeStruct((S, HV), BF16), compiler_params=_params("parallel"),
    )(o, z, gain)


def _gdn_outnorm_bwd(dy, o, z, gain, *, name):
    S, HV = o.shape
    RB = min(256, S)

    def body(dy_ref, o_ref, z_ref, g_ref, do_ref, dz_ref, dg_ref):
        part = None
        for h in range(HV // GDN_DK):
            cols = slice(h * GDN_DK, (h + 1) * GDN_DK)
            ov = o_ref[:, cols]
            zv = z_ref[:, cols].astype(F32)
            dyv = dy_ref[:, cols].astype(F32)
            r = lax.rsqrt(jnp.mean(ov * ov, axis=-1, keepdims=True) + RMS_EPS)
            n = ov * r
            sg = _sigmoid(zv)
            dng = dyv * (zv * sg)
            dn = dng * g_ref[...]
            do_ref[:, cols] = r * (dn - n * jnp.mean(dn * n, axis=-1, keepdims=True))
            dz_ref[:, cols] = (dyv * (n * g_ref[...]) * (sg * (1.0 + zv * (1.0 - sg)))).astype(BF16)
            p = jnp.sum(dng * n, axis=0, keepdims=True)
            part = p if part is None else part + p

        @pl.when(pl.program_id(0) == 0)
        def _():
            dg_ref[...] = part

        @pl.when(pl.program_id(0) > 0)
        def _():
            dg_ref[...] += part

    blk = pl.BlockSpec((RB, HV), lambda i: (i, 0))
    vec = pl.BlockSpec((1, GDN_DK), lambda i: (0, 0))
    return pl.pallas_call(
        body, name=name, grid=(S // RB,),
        in_specs=[blk, blk, blk, vec], out_specs=[blk, blk, vec],
        out_shape=[jax.ShapeDtypeStruct((S, HV), F32), jax.ShapeDtypeStruct((S, HV), BF16),
                   jax.ShapeDtypeStruct((1, GDN_DK), F32)],
        compiler_params=_params("arbitrary"),
    )(dy, o, z, gain)


def _rms64(x, gain):
    r = lax.rsqrt(jnp.mean(x * x, axis=-1, keepdims=True) + RMS_EPS)
    xh = x * r
    return xh, r, xh * gain


def _rms64_bwd(dy, xh, r, gain):
    dxh = dy * gain
    return r * (dxh - xh * jnp.mean(dxh * xh, axis=-1, keepdims=True))


_HP = LANES // DSW_DH
_DSW_W = DSW_HEADS * DSW_DH
_DSW_ROWS = 1024
_DSW_BATCH = 8


def _dsw_geometry(S, g):
    d = DSW_GROUPS[g][1]
    slab = DSW_BLK * d
    tb = max(1, min(_DSW_ROWS, S) // slab)
    return d, slab, tb, S // (tb * slab)


def _block_rows(t, r, slab, d):
    return pl.ds(t * slab + r, DSW_BLK) if d == 1 else pl.ds(t * slab + r, DSW_BLK, stride=d)


def _head(x, h):
    return x[:, h * DSW_DH:(h + 1) * DSW_DH]


def _dsw_attn_fwd(q, k, v, bias, q_gain, k_gain, prev_out, *, g, name):
    S, WT = q.shape
    B = DSW_BLK
    d, slab, tb, n_tiles = _dsw_geometry(S, g)
    rt = tb * slab
    cb = g * (_DSW_W // LANES)
    batch_res = max(1, _DSW_BATCH // tb)

    def body(q_ref, kp_ref, kc_ref, vp_ref, vc_ref, bias_ref, qg_ref, kg_ref, *rest):
        o_ref, lse_ref = rest[-2:]
        i = pl.program_id(1)
        qg, kg = qg_ref[...] * DSW_DH ** -0.5, kg_ref[...]
        col = lax.broadcasted_iota(jnp.int32, (B, 2 * B), 1)
        for r0 in range(0, d, batch_res):
            res = range(r0, min(d, r0 + batch_res))
            heads = range(_HP)
            k_raw = {(r, -1): kp_ref[_block_rows(0, r, slab, d), :] for r in res}
            v_raw = {(r, -1): vp_ref[_block_rows(0, r, slab, d), :] for r in res}
            q_raw = {}
            for r in res:
                for t in range(tb):
                    rows = _block_rows(t, r, slab, d)
                    q_raw[r, t], k_raw[r, t], v_raw[r, t] = q_ref[rows, :], kc_ref[rows, :], vc_ref[rows, :]
            kn = {key: [_rms64(_head(x, h), kg)[2].astype(BF16) for h in heads] for key, x in k_raw.items()}
            vb = {key: [_head(x, h).astype(BF16) for h in heads] for key, x in v_raw.items()}
            qn = {key: [_rms64(_head(x, h), qg)[2] for h in heads] for key, x in q_raw.items()}
            items = [(r, t, h) for r in res for t in range(tb) for h in heads]
            s = {}
            for r, t, h in items:
                sv = _dot(qn[r, t][h], jnp.concatenate([kn[r, t - 1][h], kn[r, t][h]], axis=0), "nt") + bias_ref[h]
                s[r, t, h] = jnp.where((i == 0) & (col < B), NEG_BIG, sv) if t == 0 else sv
            m = {it: jnp.max(s[it], axis=-1, keepdims=True) for it in items}
            p = {it: jnp.exp(s[it] - m[it]) for it in items}
            l = {it: jnp.sum(p[it], axis=-1, keepdims=True) for it in items}
            o = {(r, t, h): _dot(p[r, t, h], jnp.concatenate([vb[r, t - 1][h], vb[r, t][h]], axis=0))
                 for r, t, h in items}
            for r in res:
                for t in range(tb):
                    rows = _block_rows(t, r, slab, d)
                    o_ref[rows, :] = jnp.concatenate([o[r, t, h] / l[r, t, h] for h in heads], axis=1)
                    lse_ref[rows, :] = jnp.concatenate(
                        [jnp.broadcast_to(m[r, t, h] + jnp.log(l[r, t, h]), (B, DSW_DH)) for h in heads], axis=1)

    cur = pl.BlockSpec((rt, LANES), lambda hp, i: (i, cb + hp))
    prev = pl.BlockSpec((slab, LANES), lambda hp, i: (jnp.maximum(i * tb - 1, 0), cb + hp))
    vec = pl.BlockSpec((1, DSW_DH), lambda hp, i: (0, 0))
    shp = jax.ShapeDtypeStruct((S, WT), F32)
    carried = [] if prev_out is None else list(prev_out)
    n_in = 8
    return pl.pallas_call(
        body, name=name, grid=(_DSW_W // LANES, n_tiles),
        in_specs=[cur, prev, cur, prev, cur, pl.BlockSpec((_HP, B, 2 * B), lambda hp, i: (hp, 0, 0)), vec, vec]
                 + [pl.BlockSpec(memory_space=pl.ANY)] * len(carried),
        out_specs=[cur, cur], out_shape=[shp, shp],
        input_output_aliases={n_in + j: j for j in range(len(carried))},
        compiler_params=_params("parallel", "parallel"),
    )(q, k, k, v, v, bias, q_gain, k_gain, *carried)


def _dsw_merge(o_g, lse_g, *, name):
    S = o_g.shape[0]
    W, G = _DSW_W, len(DSW_GROUPS)
    tr = min(512, S)

    def body(o_ref, l_ref, out_ref, lse_ref):
        ls = [l_ref[:, g * W:(g + 1) * W] for g in range(G)]
        m = ls[0]
        for g in range(1, G):
            m = jnp.maximum(m, ls[g])
        den = jnp.zeros_like(m)
        acc = jnp.zeros_like(m)
        for g in range(G):
            wg = jnp.exp(ls[g] - m)
            den = den + wg
            acc = acc + wg * o_ref[:, g * W:(g + 1) * W]
        out_ref[...] = acc / den
        lse_ref[...] = m + jnp.log(den)

    wide = pl.BlockSpec((tr, G * W), lambda i: (i, 0))
    blk = pl.BlockSpec((tr, W), lambda i: (i, 0))
    shp = jax.ShapeDtypeStruct((S, W), F32)
    return pl.pallas_call(
        body, name=name, grid=(S // tr,), in_specs=[wide, wide], out_specs=[blk, blk],
        out_shape=[shp, shp], compiler_params=_params("parallel"),
    )(o_g, lse_g)


def _dsw_attn_bwd(q, k, v, o, lse, do, bias, q_gain, k_gain, prev_out, *, g, name):
    S, WT = q.shape
    B = DSW_BLK
    d, slab, tb, n_tiles = _dsw_geometry(S, g)
    rt = tb * slab
    cb = g * (_DSW_W // LANES)
    n_slabs = S // slab
    scale = DSW_DH ** -0.5
    batch_res = max(1, _DSW_BATCH // tb)

    def body(q_ref, qx_ref, kp_ref, kc_ref, vp_ref, vc_ref, o_ref, ox_ref, l_ref, lx_ref, do_ref, dox_ref,
             bias_ref, qg_ref, kg_ref, *rest):
        dq_ref, dk_ref, dv_ref, db_ref, dqg_ref, dkg_ref = rest[-6:]
        hp, i = pl.program_id(0), pl.program_id(1)
        qg, kg = qg_ref[...] * scale, kg_ref[...]
        col = lax.broadcasted_iota(jnp.int32, (B, 2 * B), 1)
        has_next = i < n_tiles - 1

        @pl.when(i == 0)
        def _():
            db_ref[...] = jnp.zeros_like(db_ref)

        dqg_acc = jnp.zeros((1, DSW_DH), F32)
        dkg_acc = jnp.zeros((1, DSW_DH), F32)
        heads = range(_HP)
        for r0 in range(0, d, batch_res):
            res = range(r0, min(d, r0 + batch_res))
            q_raw, k_raw, v_raw, o_raw, l_raw, do_raw = {}, {}, {}, {}, {}, {}
            for r in res:
                first_rows = _block_rows(0, r, slab, d)
                k_raw[r, -1], v_raw[r, -1] = kp_ref[first_rows, :], vp_ref[first_rows, :]
                for t in range(tb):
                    rows = _block_rows(t, r, slab, d)
                    q_raw[r, t], o_raw[r, t], l_raw[r, t], do_raw[r, t] = (
                        q_ref[rows, :], o_ref[rows, :], l_ref[rows, :], do_ref[rows, :])
                    k_raw[r, t], v_raw[r, t] = kc_ref[rows, :], vc_ref[rows, :]
                q_raw[r, tb], o_raw[r, tb], l_raw[r, tb], do_raw[r, tb] = (
                    qx_ref[first_rows, :], ox_ref[first_rows, :], lx_ref[first_rows, :], dox_ref[first_rows, :])
            kk = {key: [_rms64(_head(x, h), kg) for h in heads] for key, x in k_raw.items()}
            qq = {key: [_rms64(_head(x, h), qg) for h in heads] for key, x in q_raw.items()}
            knb = {key: [kk[key][h][2].astype(BF16) for h in heads] for key in kk}
            qnb = {key: [qq[key][h][2].astype(BF16) for h in heads] for key in qq}
            vb = {key: [_head(x, h).astype(BF16) for h in heads] for key, x in v_raw.items()}
            dob = {key: [_head(x, h).astype(BF16) for h in heads] for key, x in do_raw.items()}
            delta = {key: [jnp.sum(_head(do_raw[key], h) * _head(o_raw[key], h), axis=-1, keepdims=True)
                           for h in heads] for key in q_raw}
            full = [(r, t, h) for r in res for t in range(tb) for h in heads]
            half = [(r, tb, h) for r in res for h in heads]
            s = {}
            for r, t, h in full:
                sv = _dot(qnb[r, t][h], jnp.concatenate([knb[r, t - 1][h], knb[r, t][h]], axis=0), "nt") + bias_ref[h]
                s[r, t, h] = jnp.where((i == 0) & (col < B), NEG_BIG, sv) if t == 0 else sv
            for r, t, h in half:
                s[r, t, h] = _dot(qnb[r, t][h], knb[r, t - 1][h], "nt") + bias_ref[h, :, 0:B]
            lse_of = lambda r, t, h: l_raw[r, t][:, h * DSW_DH:h * DSW_DH + 1]
            p = {(r, t, h): jnp.exp(s[r, t, h] - lse_of(r, t, h)) for r, t, h in full}
            for r, t, h in half:
                p[r, t, h] = jnp.where(has_next, jnp.exp(s[r, t, h] - lse_of(r, t, h)), 0.0)
            dp = {(r, t, h): _dot(dob[r, t][h], jnp.concatenate([vb[r, t - 1][h], vb[r, t][h]], axis=0), "nt")
                  for r, t, h in full}
            for r, t, h in half:
                dp[r, t, h] = _dot(dob[r, t][h], vb[r, t - 1][h], "nt")
            ds = {(r, t, h): p[r, t, h] * (dp[r, t, h] - delta[r, t][h]) for r, t, h in full + half}
            pb = {it: p[it].astype(BF16) for it in ds}
            dsb = {it: ds[it].astype(BF16) for it in ds}
            for h in heads:
                tot = None
                for r in res:
                    for t in range(tb):
                        tot = ds[r, t, h] if tot is None else tot + ds[r, t, h]
                db_ref[h] += tot
            dqn = {(r, t, h): _dot(dsb[r, t, h], jnp.concatenate([knb[r, t - 1][h], knb[r, t][h]], axis=0))
                   for r, t, h in full}
            prev_half = lambda x, r, t, h: x[r, t, h][:, :B] if t < tb else x[r, t, h]
            dkn = {(r, t, h): _dot(dsb[r, t, h][:, B:], qnb[r, t][h], "tn")
                   + _dot(prev_half(dsb, r, t + 1, h), qnb[r, t + 1][h], "tn") for r, t, h in full}
            dvv = {(r, t, h): _dot(pb[r, t, h][:, B:], dob[r, t][h], "tn")
                   + _dot(prev_half(pb, r, t + 1, h), dob[r, t + 1][h], "tn") for r, t, h in full}
            for r, t, h in full:
                dqg_acc = dqg_acc + jnp.sum(dqn[r, t, h] * qq[r, t][h][0], axis=0, keepdims=True)
                dkg_acc = dkg_acc + jnp.sum(dkn[r, t, h] * kk[r, t][h][0], axis=0, keepdims=True)
            for r in res:
                for t in range(tb):
                    rows = _block_rows(t, r, slab, d)
                    dq_ref[rows, :] = jnp.concatenate(
                        [_rms64_bwd(dqn[r, t, h], qq[r, t][h][0], qq[r, t][h][1], qg) for h in heads], axis=1)
                    dk_ref[rows, :] = jnp.concatenate(
                        [_rms64_bwd(dkn[r, t, h], kk[r, t][h][0], kk[r, t][h][1], kg) for h in heads], axis=1)
                    dv_ref[rows, :] = jnp.concatenate([dvv[r, t, h] for h in heads], axis=1)

        start = (hp == 0) & (i == 0)

        @pl.when(start)
        def _():
            dqg_ref[...] = dqg_acc * scale
            dkg_ref[...] = dkg_acc

        @pl.when(jnp.logical_not(start))
        def _():
            dqg_ref[...] += dqg_acc * scale
            dkg_ref[...] += dkg_acc

    def spec(rows, pick, base):
        return pl.BlockSpec((rows, LANES), lambda hp, i: (pick(i), base + hp))

    same = lambda i: i
    before = lambda i: jnp.maximum(i * tb - 1, 0)
    after = lambda i: jnp.minimum((i + 1) * tb, n_slabs - 1)
    cur, cur1 = spec(rt, same, cb), spec(rt, same, 0)
    vec = pl.BlockSpec((1, DSW_DH), lambda hp, i: (0, 0))
    bspec = pl.BlockSpec((_HP, B, 2 * B), lambda hp, i: (hp, 0, 0))
    shp = jax.ShapeDtypeStruct((S, WT), F32)
    vshp = jax.ShapeDtypeStruct((1, DSW_DH), F32)
    carried = [] if prev_out is None else list(prev_out)
    n_in = 15
    return pl.pallas_call(
        body, name=name, grid=(_DSW_W // LANES, n_tiles),
        in_specs=[cur, spec(slab, after, cb), spec(slab, before, cb), cur, spec(slab, before, cb), cur,
                  cur1, spec(slab, after, 0), cur1, spec(slab, after, 0), cur1, spec(slab, after, 0),
                  bspec, vec, vec] + [pl.BlockSpec(memory_space=pl.ANY)] * len(carried),
        out_specs=[cur, cur, cur, bspec, vec, vec],
        out_shape=[shp, shp, shp, jax.ShapeDtypeStruct(bias.shape, F32), vshp, vshp],
        input_output_aliases={n_in + j: j for j in range(len(carried))},
        compiler_params=_params("arbitrary", "arbitrary"),
    )(q, q, k, k, v, v, o, o, lse, lse, do, do, bias, q_gain, k_gain, *carried)


def _t5_bucket(dist):
    max_exact = REL_BUCKETS // 2
    scaled = jnp.log(jnp.maximum(dist, 1).astype(F32) / max_exact) / math.log(REL_MAX_DIST / max_exact)
    large = jnp.minimum(max_exact + (scaled * (REL_BUCKETS - max_exact)).astype(jnp.int32), REL_BUCKETS - 1)
    return jnp.where(dist < max_exact, dist, large)


def _dsw_band():
    dist = (jnp.arange(DSW_BLK)[:, None] + DSW_BLK) - jnp.arange(2 * DSW_BLK)[None, :]
    return dist, (dist >= 0) & (dist <= DSW_BLK)


def _dsw_bias(rel_bias):
    dist, band = _dsw_band()
    out = []
    for g, (_, d) in enumerate(DSW_GROUPS):
        hot = jax.nn.one_hot(_t5_bucket(jnp.maximum(dist, 0) * d), REL_BUCKETS, dtype=F32)
        tab = jnp.einsum("qkb,bh->hqk", hot, rel_bias[:, g * DSW_HEADS:(g + 1) * DSW_HEADS],
                         precision=lax.Precision.HIGHEST)
        out.append(jnp.where(band[None], tab, NEG_BIG))
    return jnp.stack(out)


def _dsw_bucket_onehot():
    dist, band = _dsw_band()
    out = []
    for _, d in DSW_GROUPS:
        hot = jax.nn.one_hot(_t5_bucket(jnp.maximum(dist, 0) * d), LANES, dtype=BF16)
        out.append(jnp.where(band[..., None], hot, 0).reshape(-1, LANES))
    return jnp.stack(out)


def _exchange(send, *, gather, name):
    R, C = send.shape[-2:]

    def body(src_ref, dst_ref, send_sems, recv_sems, local_sem):
        x, y, c = lax.axis_index("x"), lax.axis_index("y"), lax.axis_index("c")
        me = 4 * x + 2 * y + c
        mine = pltpu.make_async_copy(src_ref if gather else src_ref.at[me], dst_ref.at[me], local_sem)
        mine.start()
        copies = []
        for rel in range(1, N_DEV):
            px = 1 - x if rel & 4 else x
            py = 1 - y if rel & 2 else y
            pc = 1 - c if rel & 1 else c
            peer = 4 * px + 2 * py + pc
            cp = pltpu.make_async_remote_copy(
                src_ref=src_ref if gather else src_ref.at[peer], dst_ref=dst_ref.at[me],
                send_sem=send_sems.at[rel - 1], recv_sem=recv_sems.at[rel - 1],
                device_id=(px, py, pc), device_id_type=pl.DeviceIdType.MESH)
            cp.start()
            copies.append(cp)
        for cp in copies:
            cp.wait()
        mine.wait()

    return pl.pallas_call(
        body, name=name,
        in_specs=[pl.BlockSpec(memory_space=pl.ANY)], out_specs=pl.BlockSpec(memory_space=pl.ANY),
        out_shape=jax.ShapeDtypeStruct((N_DEV, R, C), send.dtype),
        scratch_shapes=[pltpu.SemaphoreType.DMA((N_DEV - 1,)), pltpu.SemaphoreType.DMA((N_DEV - 1,)),
                        pltpu.SemaphoreType.DMA(())],
    )(send)


def _gather_two_level(send, *, name):
    R, C = send.shape

    def body(src_ref, dst_ref, send_sems, recv_sems, local_sem):
        x, y, c = lax.axis_index("x"), lax.axis_index("y"), lax.axis_index("c")
        me, sibling = (x, y, c), (x, y, 1 - c)
        chips = [(1 - x, y), (x, 1 - y), (1 - x, 1 - y)]

        def slot(px, py, pc):
            return dst_ref.at[4 * px + 2 * py + pc]

        def copy(k, block, to, src=None):
            return pltpu.make_async_remote_copy(
                src_ref=slot(*block) if src is None else src, dst_ref=slot(*block),
                send_sem=send_sems.at[k], recv_sem=recv_sems.at[k],
                device_id=to, device_id_type=pl.DeviceIdType.MESH)

        mine = pltpu.make_async_copy(src_ref, slot(*me), local_sem)
        mine.start()
        first = [copy(0, me, sibling, src=src_ref)]
        first += [copy(1 + j, me, (*chip, c), src=src_ref) for j, chip in enumerate(chips)]
        for cp in first:
            cp.start()
        passed = [copy(4 + j, (*chip, c), sibling) for j, chip in enumerate(chips)]
        for j, chip in enumerate(chips):
            copy(1 + j, (*chip, c), me).wait_recv()
            passed[j].start()
        copy(0, sibling, me).wait_recv()
        for j, chip in enumerate(chips):
            copy(4 + j, (*chip, 1 - c), me).wait_recv()
        for cp in first + passed:
            cp.wait_send()
        mine.wait()

    return pl.pallas_call(
        body, name=name,
        in_specs=[pl.BlockSpec(memory_space=pl.ANY)], out_specs=pl.BlockSpec(memory_space=pl.ANY),
        out_shape=jax.ShapeDtypeStruct((N_DEV, R, C), send.dtype),
        scratch_shapes=[pltpu.SemaphoreType.DMA((N_DEV - 1,)), pltpu.SemaphoreType.DMA((N_DEV - 1,)),
                        pltpu.SemaphoreType.DMA(())],
    )(send)


def _swap_with_sibling(send, *, name):
    def body(src_ref, dst_ref, send_sem, recv_sem):
        x, y, c = lax.axis_index("x"), lax.axis_index("y"), lax.axis_index("c")
        cp = pltpu.make_async_remote_copy(src_ref=src_ref, dst_ref=dst_ref, send_sem=send_sem, recv_sem=recv_sem,
                                          device_id=(x, y, 1 - c), device_id_type=pl.DeviceIdType.MESH)
        cp.start()
        cp.wait()

    return pl.pallas_call(
        body, name=name,
        in_specs=[pl.BlockSpec(memory_space=pl.ANY)], out_specs=pl.BlockSpec(memory_space=pl.ANY),
        out_shape=jax.ShapeDtypeStruct(send.shape, send.dtype),
        scratch_shapes=[pltpu.SemaphoreType.DMA(()), pltpu.SemaphoreType.DMA(())],
    )(send)


def _fill_from_sibling(buf, *, name):
    n_chips = buf.shape[0]

    def body(in_ref, out_ref, send_sems, recv_sems):
        x, y, c = lax.axis_index("x"), lax.axis_index("y"), lax.axis_index("c")
        copies = [pltpu.make_async_remote_copy(
            src_ref=in_ref.at[q, c], dst_ref=out_ref.at[q, c], send_sem=send_sems.at[q], recv_sem=recv_sems.at[q],
            device_id=(x, y, 1 - c), device_id_type=pl.DeviceIdType.MESH) for q in range(n_chips)]
        for cp in copies:
            cp.start()
        for cp in copies:
            cp.wait()

    return pl.pallas_call(
        body, name=name,
        in_specs=[pl.BlockSpec(memory_space=pl.ANY)], out_specs=pl.BlockSpec(memory_space=pl.ANY),
        out_shape=jax.ShapeDtypeStruct(buf.shape, buf.dtype), input_output_aliases={0: 0},
        scratch_shapes=[pltpu.SemaphoreType.DMA((n_chips,)), pltpu.SemaphoreType.DMA((n_chips,))],
    )(buf)


def _exchange_chips(send, *, name):
    n_chips, R, C = send.shape

    def body(src_ref, dst_ref, send_sems, recv_sems, local_sem):
        mine, copies = _chip_copies(src_ref, dst_ref, send_sems, recv_sems, local_sem)
        mine.start()
        for cp in copies:
            cp.start()
        for cp in copies:
            cp.wait()
        mine.wait()

    return pl.pallas_call(
        body, name=name,
        in_specs=[pl.BlockSpec(memory_space=pl.ANY)], out_specs=pl.BlockSpec(memory_space=pl.ANY),
        out_shape=jax.ShapeDtypeStruct(send.shape, send.dtype),
        scratch_shapes=[pltpu.SemaphoreType.DMA((n_chips - 1,)), pltpu.SemaphoreType.DMA((n_chips - 1,)),
                        pltpu.SemaphoreType.DMA(())],
    )(send)


def _add_pair(a, b, *, name):
    n, R, C = a.shape
    tr = _tile(R, 1024)

    def body(a_ref, b_ref, o_ref):
        o_ref[...] = (a_ref[...].astype(F32) + b_ref[...].astype(F32)).astype(o_ref.dtype)

    blk = pl.BlockSpec((None, tr, C), lambda k, i: (k, i, 0))
    return pl.pallas_call(
        body, name=name, grid=(n, R // tr), in_specs=[blk, blk], out_specs=blk,
        out_shape=jax.ShapeDtypeStruct(a.shape, a.dtype), compiler_params=_params("parallel", "parallel"),
    )(a, b)


_BIG = ("w_ffn_in", "w_ffn_out", "gdn_w_in", "gdn_conv", "gdn_w_out", "dsw_w_in", "dsw_w_out")
_LATE = ("gdn_w_in", "gdn_conv", "gdn_w_out")
_EARLY = tuple(n for n in _BIG if n not in _LATE)
_SHARD_AXIS = {"w_ffn_in": 2, "w_ffn_out": 1, "gdn_w_in": 2, "gdn_conv": 2, "gdn_w_out": 1, "dsw_w_in": 2,
               "dsw_w_out": 2}
_SMALL = ("b_ada", "norm_mix", "norm_ffn", "gdn_a_log", "gdn_dt_bias", "gdn_out_norm", "dsw_q_norm",
          "dsw_k_norm", "rel_bias")
_ROW_ALIGN = 16
_BIG_ALIGN = 1024


def _ceil_to(n, m):
    return -(-n // m) * m


def _seg_rows(shape):
    return _ceil_to(_ceil_to(int(np.prod(shape)), LANES) // LANES, _ROW_ALIGN)


def _pack(arrs, total_align):
    lead = arrs[0][1]
    segs = []
    for a, nlead in arrs:
        assert nlead == lead
        bshape = a.shape[:nlead]
        n = int(np.prod(a.shape[nlead:]))
        rows = _seg_rows(a.shape[nlead:])
        flat = a.reshape(bshape + (n,))
        flat = jnp.pad(flat, [(0, 0)] * nlead + [(0, rows * LANES - n)])
        segs.append(flat.reshape(bshape + (rows, LANES)))
    buf = jnp.concatenate(segs, axis=lead)
    total = _ceil_to(buf.shape[lead], total_align)
    return jnp.pad(buf, [(0, 0)] * lead + [(0, total - buf.shape[lead]), (0, 0)])


def _unpack(buf, shapes, nlead):
    out, off = [], 0
    for shp in shapes:
        n, rows = int(np.prod(shp)), _seg_rows(shp)
        seg = lax.slice_in_dim(buf, off, off + rows, axis=nlead)
        seg = seg.reshape(buf.shape[:nlead] + (rows * LANES,))[..., :n]
        out.append(seg.reshape(buf.shape[:nlead] + tuple(shp)))
        off += rows
    return out


def _to_natural(g, axis):
    n, L, r, c = g.shape
    if axis == 2:
        return jnp.transpose(g, (1, 2, 0, 3)).reshape(L, r, n * c)
    return jnp.transpose(g, (1, 0, 2, 3)).reshape(L, n * r, c)


def _to_blocked(w, axis):
    L, R, C = w.shape
    if axis == 2:
        return jnp.transpose(w.reshape(L, R, N_DEV, C // N_DEV), (2, 0, 1, 3))
    return jnp.transpose(w.reshape(L, N_DEV, R // N_DEV, C), (1, 0, 2, 3))


def _hm(a):
    lead = a.shape[:-1]
    return jnp.swapaxes(a.reshape(lead + (3, GDN_HEADS, GDN_DK)), -3, -2).reshape(lead + (3 * GDN_HEADS * GDN_DK,))


def _un_hm(a):
    lead = a.shape[:-1]
    return jnp.swapaxes(a.reshape(lead + (GDN_HEADS, 3, GDN_DK)), -3, -2).reshape(lead + (3 * GDN_HEADS * GDN_DK,))


_TILES = (1536, 1408, 1024, 768, 512, 384, 256, 128, 64, 32, 16, 8)


def _tile(n, cap):
    for t in _TILES:
        if t <= cap and n % t == 0:
            return t
    return n


def _mm_auto(a, b, mode, name, **kw):
    if mode == "tn":
        (K, M), N = a.shape, b.shape[1]
        tm, tn, tk = _tile(M, 1408), _tile(N, 1408), _tile(K, 1024)
    else:
        M, K = a.shape
        N = b.shape[1] if mode == "nn" else b.shape[0]
        tm, tn, tk = _tile(M, 512), _tile(N, 1536), _tile(K, 1408)
    return _mm(a, b, mode=mode, name=name, tm=tm, tn=tn, tk=tk, **kw)


def _row(v):
    return v.reshape(1, -1)


def _ffn_in_act(h, w_in, *, name):
    S, D = h.shape
    F = w_in.shape[1] // 2
    tm, tn = _tile(S, 512), _tile(F, 1408)
    nj = F // tn

    def body(h_ref, wg_ref, wu_ref, g_ref, u_ref, a_ref):
        hv = h_ref[...]
        gate = jnp.dot(hv, wg_ref[...], preferred_element_type=F32)
        up = jnp.dot(hv, wu_ref[...], preferred_element_type=F32)
        g_ref[...] = gate.astype(BF16)
        u_ref[...] = up.astype(BF16)
        a_ref[...] = (_silu(gate) * up).astype(BF16)

    out = pl.BlockSpec((tm, tn), lambda i, j: (i, j))
    shp = jax.ShapeDtypeStruct((S, F), BF16)
    return pl.pallas_call(
        body, name=name, grid=(S // tm, nj),
        in_specs=[pl.BlockSpec((tm, D), lambda i, j: (i, 0)), pl.BlockSpec((D, tn), lambda i, j: (0, j)),
                  pl.BlockSpec((D, tn), lambda i, j: (0, j + nj))],
        out_specs=[out, out, out], out_shape=[shp, shp, shp],
        compiler_params=_params("parallel", "parallel"),
    )(h, w_in, w_in)


def _ffn_out_dx_act(dy, w_out, gate_vec, pg, pu, *, name):
    S, D = dy.shape
    F = w_out.shape[0]
    tm, tn = _tile(S, 512), _tile(F, 1408)

    def body(dy_ref, w_ref, gv_ref, pg_ref, pu_ref, dg_ref, du_ref):
        dyg = (dy_ref[...] * gv_ref[...]).astype(BF16)
        da = lax.dot_general(dyg, w_ref[...], _DOT_DIMS["nt"], preferred_element_type=F32)
        gate = pg_ref[...].astype(F32)
        up = pu_ref[...].astype(F32)
        sg = _sigmoid(gate)
        dg_ref[...] = (da * up * (sg * (1.0 + gate * (1.0 - sg)))).astype(BF16)
        du_ref[...] = (da * (gate * sg)).astype(BF16)

    blk = pl.BlockSpec((tm, tn), lambda i, j: (i, j))
    shp = jax.ShapeDtypeStruct((S, F), BF16)
    return pl.pallas_call(
        body, name=name, grid=(S // tm, F // tn),
        in_specs=[pl.BlockSpec((tm, D), lambda i, j: (i, 0)), pl.BlockSpec((tn, D), lambda i, j: (j, 0)),
                  pl.BlockSpec((1, D), lambda i, j: (0, 0)), blk, blk],
        out_specs=[blk, blk], out_shape=[shp, shp],
        compiler_params=_params("parallel", "parallel"),
    )(dy, w_out, gate_vec, pg, pu)


def _ffn_fwd(x, mod, gain, w_in, w_out, tag):
    sh, sc, gate = mod
    h = _norm_mod_fwd(x, gain, sc, sh, name=f"ffn_norm_{tag}")
    pg, pu, a = _ffn_in_act(h, w_in, name=f"ffn_in_{tag}")
    y = _mm_auto(a, w_out, "nn", f"ffn_out_{tag}", out_scale=gate, resid=x)
    return y, (x, h, pg, pu, a)


def _ffn_bwd(dy, saved, mod, gain, w_in, w_out, tag):
    sh, sc, gate = mod
    x, h, pg, pu, a = saved
    F = pg.shape[1]
    gmat = _mm_auto(a, dy, "tn", f"ffn_out_g_{tag}")
    dw_out, dgate = _wout_grad(gmat, w_out, gate, name=f"ffn_out_dw_{tag}")
    dpg, dpu = _ffn_out_dx_act(dy, w_out, gate, pg, pu, name=f"ffn_out_dx_{tag}")
    dw_in = jnp.concatenate([_mm_auto(h, dpg, "tn", f"ffn_in_dw_gate_{tag}", out_dtype=BF16),
                             _mm_auto(h, dpu, "tn", f"ffn_in_dw_up_{tag}", out_dtype=BF16)], axis=1)
    dh = _mm_auto(dpg, w_in, "nt", f"ffn_in_dx_gate_{tag}")
    dh = _mm_auto(dpu, w_in, "nt", f"ffn_in_dx_up_{tag}", b_k_off=F, resid=dh)
    dx, dsh, dsc, dgain = _norm_mod_bwd(dh, x, dy, gain, sc, name=f"ffn_norm_bwd_{tag}")
    return dx, dict(w_in=dw_in, w_out=dw_out, gain=dgain, mod=(dsh, dsc, dgate))


def _gdn_fwd(x, mod, gain, W, riding=None):
    sh, sc, gate = mod
    S = x.shape[0]
    h = _norm_mod_fwd(x, gain, sc, sh, name="gdn_norm")
    pq = _mm_auto(h, W["gdn_qkv"], "nn", "gdn_in_qkv", out_dtype=BF16)
    z = _mm_auto(h, W["gdn_z"], "nn", "gdn_in_z", out_dtype=BF16)
    ab = _mm_auto(h, W["gdn_ab"], "nn", "gdn_in_ab")
    qkvn = _gdn_prep_fwd(pq, W["gdn_conv"], name="gdn_prep")
    ab4 = jnp.transpose(ab[:, :2 * GDN_HEADS]).reshape(2 * GDN_HEADS, S // GDN_CHUNK, 1, GDN_CHUNK)
    o, states, tinvs, *rode = _gdn_chunk_fwd(qkvn, ab4, W["gdn_a_log"], W["gdn_dt_bias"], name="gdn_chunk",
                                             riding=riding)
    o2 = _gdn_outnorm_fwd(o, z, W["gdn_out_norm"], name="gdn_outnorm")
    y = _mm_auto(o2, W["gdn_out"], "nn", "gdn_out", out_scale=gate, resid=x)
    return y, (x, h, pq, z, qkvn, ab4, o, states, tinvs, o2), (rode[0] if rode else None)


def _gdn_bwd(dy, saved, mod, gain, W, riding=None):
    sh, sc, gate = mod
    x, h, pq, z, qkvn, ab4, o, states, tinvs, o2 = saved
    S = x.shape[0]
    gmat = _mm_auto(o2, dy, "tn", "gdn_out_g")
    dw_out, dgate = _wout_grad(gmat, W["gdn_out"], gate, name="gdn_out_dw")
    do2 = _mm_auto(dy, W["gdn_out"], "nt", "gdn_out_dx", a_scale=gate)
    do, dz, dout_norm = _gdn_outnorm_bwd(do2, o, z, W["gdn_out_norm"], name="gdn_outnorm_bwd")
    dqkvn, dab4, da_log, ddt_bias, *rode = _gdn_chunk_bwd(
        qkvn, ab4, W["gdn_a_log"], W["gdn_dt_bias"], states, tinvs, do, name="gdn_chunk_bwd", riding=riding)
    dc, dconv8 = _gdn_prep_bwd_pre(dqkvn, pq, W["gdn_conv"], name="gdn_prep_bwd")
    dpq = _gdn_conv_bwd_x(dc, W["gdn_conv"], name="gdn_conv_bwd")
    dab = jnp.transpose(dab4.reshape(2 * GDN_HEADS, S))
    dab = jnp.pad(dab, ((0, 0), (0, LANES - 2 * GDN_HEADS))).astype(BF16)
    dw_qkv = _mm_auto(h, dpq, "tn", "gdn_in_qkv_dw", out_dtype=BF16)
    dw_z = _mm_auto(h, dz, "tn", "gdn_in_z_dw", out_dtype=BF16)
    dw_ab = _mm_auto(h, dab, "tn", "gdn_in_ab_dw", out_dtype=BF16)
    dh = _mm_auto(dpq, W["gdn_qkv"], "nt", "gdn_in_qkv_dx")
    dh = _mm_auto(dz, W["gdn_z"], "nt", "gdn_in_z_dx", resid=dh)
    dh = _mm_auto(dab, W["gdn_ab"], "nt", "gdn_in_ab_dx", resid=dh)
    dx, dsh, dsc, dgain = _norm_mod_bwd(dh, x, dy, gain, sc, name="gdn_norm_bwd")
    dw_in = jnp.concatenate([_un_hm(dw_qkv), dw_z, dw_ab[:, :2 * GDN_HEADS]], axis=1)
    return dx, dict(gdn_w_in=dw_in, gdn_conv=_un_hm(dconv8[:GDN_CONV]), gdn_w_out=dw_out, gdn_out_norm=dout_norm,
                    gdn_a_log=da_log.reshape(1, GDN_HEADS), gdn_dt_bias=ddt_bias.reshape(1, GDN_HEADS),
                    gain=dgain, mod=(dsh, dsc, dgate)), (rode[0] if rode else None)


def _dsw_fwd(x, mod, gain, W):
    sh, sc, gate = mod
    h = _norm_mod_fwd(x, gain, sc, sh, name="dsw_norm")
    q, k, v = (_mm_auto(h, W[n], "nn", f"dsw_in_{n[-1]}") for n in ("dsw_q", "dsw_k", "dsw_v"))
    outs = None
    for g in range(len(DSW_GROUPS)):
        outs = _dsw_attn_fwd(q, k, v, W["dsw_bias"][g], W["dsw_q_norm"], W["dsw_k_norm"], outs, g=g,
                             name=f"dsw_attn_{g}")
    o, lse = _dsw_merge(*outs, name="dsw_merge")
    y = _mm_auto(o, W["dsw_out"], "nn", "dsw_out", out_scale=gate, resid=x)
    return y, (x, h, q, k, v, o, lse)


def _dsw_bwd(dy, saved, mod, gain, W):
    sh, sc, gate = mod
    x, h, q, k, v, o, lse = saved
    gmat = _mm_auto(o, dy, "tn", "dsw_out_g")
    dw_out, dgate = _wout_grad(gmat, W["dsw_out"], gate, name="dsw_out_dw")
    do = _mm_auto(dy, W["dsw_out"], "nt", "dsw_out_dx", a_scale=gate)
    G = len(DSW_GROUPS)
    dqkv, dbias, dq_norm, dk_norm = None, [], 0.0, 0.0
    for g in range(G):
        *dqkv, db, dqg, dkg = _dsw_attn_bwd(q, k, v, o, lse, do, W["dsw_bias"][g], W["dsw_q_norm"],
                                            W["dsw_k_norm"], dqkv, g=g, name=f"dsw_attn_bwd_{g}")
        dbias.append(db)
        dq_norm, dk_norm = dq_norm + dqg, dk_norm + dkg
    dws, dh = [], None
    for n, d in zip(("dsw_q", "dsw_k", "dsw_v"), dqkv):
        dws.append(_mm_auto(h, d, "tn", f"dsw_in_{n[-1]}_dw", out_dtype=BF16))
        dh = _mm_auto(d, W[n], "nt", f"dsw_in_{n[-1]}_dx", **({} if dh is None else {"resid": dh}))
    dx, dsh, dsc, dgain = _norm_mod_bwd(dh, x, dy, gain, sc, name="dsw_norm_bwd")
    hot = _dsw_bucket_onehot()
    drel = [_mm_auto(dbias[g].reshape(DSW_HEADS, -1), hot[g], "nn", f"dsw_rel_bias_{g}")[:, :REL_BUCKETS]
            for g in range(G)]
    return dx, dict(dsw_w_in=jnp.concatenate(dws, axis=1), dsw_w_out=dw_out, dsw_q_norm=dq_norm,
                    dsw_k_norm=dk_norm, rel_bias=jnp.transpose(jnp.concatenate(drel, axis=0)),
                    gain=dgain, mod=(dsh, dsc, dgate))


def _local_step(x, target, mod, W, late_weights=None, early_pairs=None):
    mods = [[_row(mod[l, i]) for i in range(6)] for l in range(2)]
    nmix = [_row(W["norm_mix"][l]) for l in range(2)]
    nffn = [_row(W["norm_ffn"][l]) for l in range(2)]
    x1, s_gdn, arrived = _gdn_fwd(x, mods[0][:3], nmix[0], W, None if late_weights is None else late_weights[0])
    if late_weights is not None:
        W = {**W, **late_weights[1](arrived)}
    x2, s_f0 = _ffn_fwd(x1, mods[0][3:], nffn[0], W["w_ffn_in"][0], W["w_ffn_out"][0], "0")
    x3, s_dsw = _dsw_fwd(x2, mods[1][:3], nmix[1], W)
    x4, s_f1 = _ffn_fwd(x3, mods[1][3:], nffn[1], W["w_ffn_in"][1], W["w_ffn_out"][1], "1")
    dx4, sse = _loss_head(x4, target, name="loss_head")
    dx3, g_f1 = _ffn_bwd(dx4, s_f1, mods[1][3:], nffn[1], W["w_ffn_in"][1], W["w_ffn_out"][1], "1")
    dx2, g_dsw = _dsw_bwd(dx3, s_dsw, mods[1][:3], nmix[1], W)
    dx1, g_f0 = _ffn_bwd(dx2, s_f0, mods[0][3:], nffn[0], W["w_ffn_in"][0], W["w_ffn_out"][0], "0")
    grads = dict(
        w_ffn_in=jnp.stack([g_f0["w_in"], g_f1["w_in"]]), w_ffn_out=jnp.stack([g_f0["w_out"], g_f1["w_out"]]),
        dsw_w_in=g_dsw["dsw_w_in"][None], dsw_w_out=g_dsw["dsw_w_out"][None])
    riding = None if early_pairs is None else early_pairs(grads)
    dx0, g_gdn, rode = _gdn_bwd(dx1, s_gdn, mods[0][:3], nmix[0], W, riding)
    dmod = jnp.stack([jnp.concatenate(list(g_gdn["mod"]) + list(g_f0["mod"]), axis=0),
                      jnp.concatenate(list(g_dsw["mod"]) + list(g_f1["mod"]), axis=0)])
    grads.update(
        norm_mix=jnp.concatenate([g_gdn["gain"], g_dsw["gain"]], axis=0),
        norm_ffn=jnp.concatenate([g_f0["gain"], g_f1["gain"]], axis=0),
        gdn_w_in=g_gdn["gdn_w_in"][None], gdn_conv=g_gdn["gdn_conv"][None], gdn_w_out=g_gdn["gdn_w_out"][None],
        gdn_out_norm=g_gdn["gdn_out_norm"], gdn_a_log=g_gdn["gdn_a_log"], gdn_dt_bias=g_gdn["gdn_dt_bias"],
        dsw_q_norm=g_dsw["dsw_q_norm"], dsw_k_norm=g_dsw["dsw_k_norm"], rel_bias=g_dsw["rel_bias"])
    return sse, dx0, grads, dmod, rode


def _prepare_first(full, small):
    gw = full["gdn_w_in"][0]
    hk3 = 3 * GDN_HEADS * GDN_DK
    return dict(
        gdn_qkv=_hm(gw[:, :hk3]), gdn_z=gw[:, hk3:hk3 + GDN_HEADS * GDN_DK],
        gdn_ab=jnp.pad(gw[:, hk3 + GDN_HEADS * GDN_DK:], ((0, 0), (0, LANES - 2 * GDN_HEADS))),
        gdn_conv=_hm(full["gdn_conv"][0]), gdn_out=full["gdn_w_out"][0],
        norm_mix=small["norm_mix"], norm_ffn=small["norm_ffn"],
        gdn_a_log=small["gdn_a_log"].reshape(GDN_HEADS, 1, 1), gdn_dt_bias=small["gdn_dt_bias"].reshape(GDN_HEADS, 1, 1),
        gdn_out_norm=small["gdn_out_norm"], dsw_q_norm=small["dsw_q_norm"], dsw_k_norm=small["dsw_k_norm"],
        dsw_bias=_dsw_bias(small["rel_bias"]))


def _prepare_rest(full):
    di = full["dsw_w_in"][0]
    dq = di.shape[1] // 3
    return dict(w_ffn_in=full["w_ffn_in"], w_ffn_out=full["w_ffn_out"],
                dsw_q=di[:, :dq], dsw_k=di[:, dq:2 * dq], dsw_v=di[:, 2 * dq:], dsw_out=full["dsw_w_out"][0])


def _prepare_weights(full, small):
    return {**_prepare_first(full, small), **_prepare_rest(full)}


_W_NAMES = ("w_ada", "b_ada", "norm_mix", "norm_ffn", "w_ffn_in", "w_ffn_out", "gdn_w_in", "gdn_conv",
            "gdn_a_log", "gdn_dt_bias", "gdn_out_norm", "gdn_w_out", "dsw_w_in", "dsw_q_norm", "dsw_k_norm",
            "dsw_w_out", "rel_bias")
_PAD_BATCH = 16


def _pad_rows(a, rows):
    return jnp.pad(a, ((0, rows - a.shape[0]), (0, 0)))


def kernel(x, c, w_ada, b_ada, norm_mix, norm_ffn, w_ffn_in, w_ffn_out, gdn_w_in, gdn_conv, gdn_a_log, gdn_dt_bias, gdn_out_norm, gdn_w_out, dsw_w_in, dsw_q_norm, dsw_k_norm, dsw_w_out, rel_bias, loss_target, m_w_ada, m_b_ada, m_norm_mix, m_norm_ffn, m_w_ffn_in, m_w_ffn_out, m_gdn_w_in, m_gdn_conv, m_gdn_a_log, m_gdn_dt_bias, m_gdn_out_norm, m_gdn_w_out, m_dsw_w_in, m_dsw_q_norm, m_dsw_k_norm, m_dsw_w_out, m_rel_bias, v_w_ada, v_b_ada, v_norm_mix, v_norm_ffn, v_w_ffn_in, v_w_ffn_out, v_gdn_w_in, v_gdn_conv, v_gdn_a_log, v_gdn_dt_bias, v_gdn_out_norm, v_gdn_w_out, v_dsw_w_in, v_dsw_q_norm, v_dsw_k_norm, v_dsw_w_out, v_rel_bias):
    w = dict(zip(_W_NAMES, (w_ada, b_ada, norm_mix, norm_ffn, w_ffn_in, w_ffn_out, gdn_w_in, gdn_conv, gdn_a_log,
                            gdn_dt_bias, gdn_out_norm, gdn_w_out, dsw_w_in, dsw_q_norm, dsw_k_norm, dsw_w_out,
                            rel_bias)))
    m = dict(zip(_W_NAMES, (m_w_ada, m_b_ada, m_norm_mix, m_norm_ffn, m_w_ffn_in, m_w_ffn_out, m_gdn_w_in,
                            m_gdn_conv, m_gdn_a_log, m_gdn_dt_bias, m_gdn_out_norm, m_gdn_w_out, m_dsw_w_in,
                            m_dsw_q_norm, m_dsw_k_norm, m_dsw_w_out, m_rel_bias)))
    v = dict(zip(_W_NAMES, (v_w_ada, v_b_ada, v_norm_mix, v_norm_ffn, v_w_ffn_in, v_w_ffn_out, v_gdn_w_in,
                            v_gdn_conv, v_gdn_a_log, v_gdn_dt_bias, v_gdn_out_norm, v_gdn_w_out, v_dsw_w_in,
                            v_dsw_q_norm, v_dsw_k_norm, v_dsw_w_out, v_rel_bias)))
    D = x.shape[-1]
    n_layers, _, ada_cols = w_ada.shape

    c_all = _exchange(c.reshape(D // LANES, LANES), gather=True, name="gather_cond").reshape(N_DEV, D)
    c_pad = _pad_rows(c_all, _PAD_BATCH)
    proj = [_mm(c_pad, w_ada[l], mode="nn", name=f"ada_proj_{l}", tm=_PAD_BATCH, tn=ada_cols, tk=D, a_silu=True)
            for l in range(n_layers)]
    mod_send = _pack([(jnp.stack([p[:N_DEV] for p in proj], axis=1), 1)], _ROW_ALIGN)
    mod_recv = _exchange(mod_send, gather=False, name="scatter_mod")
    mod = _unpack(mod_recv, [(n_layers, ada_cols)], 1)[0]
    mod = jnp.transpose(mod, (1, 0, 2)).reshape(n_layers, N_DEV * ada_cols) + b_ada
    mod = mod.reshape(n_layers, 6, D)

    conv_hi = gdn_conv.astype(BF16)
    conv_lo = (gdn_conv - conv_hi.astype(F32)).astype(BF16)
    first_send = _pack([(conv_hi if n == "gdn_conv" else w[n].astype(BF16), 0) for n in _LATE] + [(conv_lo, 0)],
                       _ROW_ALIGN)
    parts = _unpack(_gather_two_level(first_send, name="gather_weights_first"),
                    [w[n].shape for n in _LATE] + [gdn_conv.shape], 1)
    full = {n: _to_natural(parts[i], _SHARD_AXIS[n]) for i, n in enumerate(_LATE)}
    full["gdn_conv"] = full["gdn_conv"].astype(F32) + _to_natural(parts[-1], _SHARD_AXIS["gdn_conv"]).astype(F32)
    W = _prepare_first(full, {n: w[n] for n in _SMALL})
    rest_send = _pack([(w[n].astype(BF16), 0) for n in _EARLY], _ROW_ALIGN)

    def rest_weights(arrived):
        w_all = _fill_from_sibling(arrived, name="swap_weights").reshape((N_DEV,) + rest_send.shape)
        blocks = _unpack(w_all, [w[n].shape for n in _EARLY], 1)
        return _prepare_rest({n: _to_natural(blocks[i], _SHARD_AXIS[n]) for i, n in enumerate(_EARLY)})

    my_c = lax.axis_index("c")

    def pair_sums(g, names, tag):
        send = _pack([(_to_blocked(g[n].astype(BF16), _SHARD_AXIS[n]), 1) for n in names], _BIG_ALIGN)
        by_core = send.reshape((N_DEV // 2, 2) + send.shape[1:])
        keep = lax.dynamic_index_in_dim(by_core, my_c, axis=1, keepdims=False)
        give = lax.dynamic_index_in_dim(by_core, 1 - my_c, axis=1, keepdims=False)
        return _add_pair(keep, _swap_with_sibling(give, name=f"swap_grads_{tag}"), name=f"add_sibling_grads_{tag}")

    sse, grad_x, grads, dmod, early_recv = _local_step(
        x[0], loss_target[0], mod, W, late_weights=(rest_send, rest_weights),
        early_pairs=lambda g: pair_sums(g, _EARLY, "early"))
    loss = lax.psum(0.5 * sse[0, 0] / D, ("x", "y", "c"))
    grads["b_ada"] = dmod.reshape(n_layers, 6 * D)
    late_recv = _exchange_chips(pair_sums(grads, _LATE, "late"), name="scatter_grads_late")
    g_parts = dict(zip(_EARLY, _unpack(early_recv, [w[n].shape for n in _EARLY], 1)))
    g_parts.update(zip(_LATE, _unpack(late_recv, [w[n].shape for n in _LATE], 1)))

    dmod_send = _pack([(jnp.transpose(dmod.reshape(n_layers, N_DEV, ada_cols), (1, 0, 2)), 1)], _ROW_ALIGN)
    small_send = _pack([(grads[n].reshape(w[n].shape), 0) for n in _SMALL], _ROW_ALIGN)
    s_recv = _exchange(jnp.concatenate(
        [dmod_send, jnp.broadcast_to(small_send[None], (N_DEV,) + small_send.shape)], axis=1),
        gather=False, name="scatter_small")
    dmod_rows = dmod_send.shape[1]

    out = {}
    kinds = ("grad", "delta", "new_m", "new_v")
    for n in _BIG:
        g4 = g_parts[n]
        rows2d = lambda a: a.reshape((-1, w[n].shape[-1]))
        res = _adamw(rows2d(w[n]), g4.reshape((g4.shape[0], -1, w[n].shape[-1])), rows2d(m[n]), rows2d(v[n]),
                     name=f"adamw_{n}")
        for kind, buf in zip(kinds, res):
            out[kind, n] = buf.reshape(w[n].shape)

    dmod_all = _unpack(lax.slice_in_dim(s_recv, 0, dmod_rows, axis=1), [(n_layers, ada_cols)], 1)[0]
    g_ada = jnp.stack([_mm(c_pad, _pad_rows(dmod_all[:, l], _PAD_BATCH), mode="tn", name=f"ada_dw_{l}",
                           tm=D, tn=ada_cols, tk=_PAD_BATCH, a_silu=True) for l in range(n_layers)])
    flat = lambda a: a.reshape(n_layers * D, ada_cols)
    res = _adamw(flat(w_ada), flat(g_ada)[None], flat(m_w_ada), flat(v_w_ada), name="adamw_ada")
    for kind, buf in zip(("grad", "delta", "new_m", "new_v"), res):
        out[kind, "w_ada"] = buf.reshape(w_ada.shape)

    small_parts = lax.slice_in_dim(s_recv, dmod_rows, s_recv.shape[1], axis=1)
    packed = [_pack([(t[n], 0) for n in _SMALL], _ROW_ALIGN) for t in (w, m, v)]
    res = _adamw(packed[0], small_parts, packed[1], packed[2], name="adamw_replicated")
    for kind, buf in zip(("grad", "delta", "new_m", "new_v"), res):
        for n, a in zip(_SMALL, _unpack(buf, [w[n].shape for n in _SMALL], 0)):
            out[kind, n] = a

    return (loss, grad_x[None]) + tuple(out[kind, n] for kind in ("grad", "delta", "new_m", "new_v")
                                        for n in _W_NAMES)
```

```python
import functools
import math

import numpy as np
import jax
import jax.numpy as jnp
from jax import lax
from jax.experimental import pallas as pl
from jax.experimental.pallas import tpu as pltpu

F32 = jnp.float32
BF16 = jnp.bfloat16

N_DEV = 8
RMS_EPS = 1e-6
LANES = 128
V7X_VMEM_LIMIT = 48 * 1024 * 1024

GDN_HEADS = 8
GDN_DK = 128
GDN_CHUNK = 64
GDN_CONV = 4
DSW_GROUPS = ((128, 1), (512, 4), (2048, 16))
DSW_HEADS = 8
DSW_DH = 64
DSW_BLK = 128
REL_BUCKETS = 32
REL_MAX_DIST = 2048

ADAM_LR = 0.001
ADAM_B1 = 0.9
ADAM_B2 = 0.999
ADAM_EPS = 1e-08
ADAM_WD = 0.01
ADAM_STEP = 10

NEG_BIG = -1e30


def _params(*sem):
    return pltpu.CompilerParams(dimension_semantics=sem, vmem_limit_bytes=V7X_VMEM_LIMIT)


def _sigmoid(x):
    return 1.0 / (1.0 + jnp.exp(-x))


def _silu(x):
    return x * _sigmoid(x)


_DOT_DIMS = {
    "nn": (((1,), (0,)), ((), ())),
    "nt": (((1,), (1,)), ((), ())),
    "tn": (((0,), (0,)), ((), ())),
}


def _mm(a, b, *, mode, name, tm, tn, tk, out_dtype=F32, a_scale=None, out_scale=None, resid=None, a_silu=False,
        b_k_off=0):
    if mode == "nn":
        (M, K), N = a.shape, b.shape[1]
    elif mode == "nt":
        (M, K), N = a.shape, b.shape[0]
    else:
        (K, M), N = a.shape, b.shape[1]
    tm, tn, tk = min(tm, M), min(tn, N), min(tk, K)
    assert M % tm == 0 and N % tn == 0 and K % tk == 0 and b_k_off % tk == 0, (name, M, N, K, tm, tn, tk)
    assert b_k_off == 0 or mode == "nt", name
    nk = K // tk

    def body(*refs):
        refs = list(refs)
        a_ref, b_ref = refs.pop(0), refs.pop(0)
        as_ref = refs.pop(0) if a_scale is not None else None
        os_ref = refs.pop(0) if out_scale is not None else None
        r_ref = refs.pop(0) if resid is not None else None
        o_ref = refs.pop(0)
        acc_ref = refs.pop(0) if nk > 1 else None

        av = a_ref[...]
        if a_silu:
            av = _silu(av.astype(F32))
        if as_ref is not None:
            av = av.astype(F32) * as_ref[...]
        part = lax.dot_general(av.astype(BF16), b_ref[...].astype(BF16), _DOT_DIMS[mode],
                               preferred_element_type=F32)

        def finish(r):
            if os_ref is not None:
                r = r * os_ref[...]
            if r_ref is not None:
                r = r + r_ref[...].astype(F32)
            o_ref[...] = r.astype(out_dtype)

        if nk == 1:
            finish(part)
        else:
            k = pl.program_id(2)

            @pl.when(k == 0)
            def _():
                acc_ref[...] = part

            @pl.when(k > 0)
            def _():
                acc_ref[...] += part

            @pl.when(k == nk - 1)
            def _():
                finish(acc_ref[...])

    if mode == "nn":
        a_spec = pl.BlockSpec((tm, tk), lambda i, j, k: (i, k))
        b_spec = pl.BlockSpec((tk, tn), lambda i, j, k: (k, j))
        as_spec = pl.BlockSpec((1, tk), lambda i, j, k: (0, k))
    elif mode == "nt":
        a_spec = pl.BlockSpec((tm, tk), lambda i, j, k: (i, k))
        b_spec = pl.BlockSpec((tn, tk), lambda i, j, k: (j, k + b_k_off // tk))
        as_spec = pl.BlockSpec((1, tk), lambda i, j, k: (0, k))
    else:
        a_spec = pl.BlockSpec((tk, tm), lambda i, j, k: (k, i))
        b_spec = pl.BlockSpec((tk, tn), lambda i, j, k: (k, j))
        as_spec = None
    in_specs, args = [a_spec, b_spec], [a, b]
    if a_scale is not None:
        in_specs.append(as_spec)
        args.append(a_scale)
    if out_scale is not None:
        in_specs.append(pl.BlockSpec((1, tn), lambda i, j, k: (0, j)))
        args.append(out_scale)
    if resid is not None:
        in_specs.append(pl.BlockSpec((tm, tn), lambda i, j, k: (i, j)))
        args.append(resid)
    return pl.pallas_call(
        body, name=name, grid=(M // tm, N // tn, nk),
        in_specs=in_specs, out_specs=pl.BlockSpec((tm, tn), lambda i, j, k: (i, j)),
        out_shape=jax.ShapeDtypeStruct((M, N), out_dtype),
        scratch_shapes=[pltpu.VMEM((tm, tn), F32)] if nk > 1 else [],
        compiler_params=_params("parallel", "parallel", "arbitrary"),
    )(*args)


def _norm_mod_fwd(x, gain, sc, sh, *, name):
    S, D = x.shape
    tr = min(512, S)

    def body(x_ref, g_ref, sc_ref, sh_ref, h_ref):
        xv = x_ref[...]
        r = lax.rsqrt(jnp.mean(xv * xv, axis=-1, keepdims=True) + RMS_EPS)
        h_ref[...] = ((xv * r) * g_ref[...] * (1.0 + sc_ref[...]) + sh_ref[...]).astype(BF16)

    row = pl.BlockSpec((tr, D), lambda i: (i, 0))
    vec = pl.BlockSpec((1, D), lambda i: (0, 0))
    return pl.pallas_call(
        body, name=name, grid=(S // tr,), in_specs=[row, vec, vec, vec], out_specs=row,
        out_shape=jax.ShapeDtypeStruct((S, D), BF16), compiler_params=_params("parallel"),
    )(x, gain, sc, sh)


def _norm_mod_bwd(dh, x, dx_res, gain, sc, *, name):
    S, D = x.shape
    tr = min(256, S)
    n_steps = S // tr

    def body(dh_ref, x_ref, dxr_ref, g_ref, sc_ref, dx_ref, dsh_ref, dsc_ref, dgain_ref, acc_sh, acc_a):
        i = pl.program_id(0)
        xv = x_ref[...]
        r = lax.rsqrt(jnp.mean(xv * xv, axis=-1, keepdims=True) + RMS_EPS)
        n = xv * r
        dhv = dh_ref[...].astype(F32)
        dn = dhv * (g_ref[...] * (1.0 + sc_ref[...]))
        dx_ref[...] = dxr_ref[...] + r * (dn - n * jnp.mean(dn * n, axis=-1, keepdims=True))
        p_sh = jnp.sum(dhv, axis=0, keepdims=True)
        p_a = jnp.sum(dhv * n, axis=0, keepdims=True)

        @pl.when(i == 0)
        def _():
            acc_sh[...] = p_sh
            acc_a[...] = p_a

        @pl.when(i > 0)
        def _():
            acc_sh[...] += p_sh
            acc_a[...] += p_a

        @pl.when(i == n_steps - 1)
        def _():
            dsh_ref[...] = acc_sh[...]
            dsc_ref[...] = acc_a[...] * g_ref[...]
            dgain_ref[...] = acc_a[...] * (1.0 + sc_ref[...])

    row = pl.BlockSpec((tr, D), lambda i: (i, 0))
    vec = pl.BlockSpec((1, D), lambda i: (0, 0))
    vshape = jax.ShapeDtypeStruct((1, D), F32)
    return pl.pallas_call(
        body, name=name, grid=(n_steps,), in_specs=[row, row, row, vec, vec],
        out_specs=[row, vec, vec, vec],
        out_shape=[jax.ShapeDtypeStruct((S, D), F32), vshape, vshape, vshape],
        scratch_shapes=[pltpu.VMEM((1, D), F32), pltpu.VMEM((1, D), F32)],
        compiler_params=_params("arbitrary"),
    )(dh, x, dx_res, gain, sc)


def _wout_grad(gmat, w, gate, *, name):
    K, D = w.shape
    tr = min(256, K)
    n_steps = K // tr

    def body(g_ref, w_ref, gate_ref, dw_ref, dgate_ref, acc):
        i = pl.program_id(0)
        gv = g_ref[...]
        dw_ref[...] = (gv * gate_ref[...]).astype(BF16)
        part = jnp.sum(gv * w_ref[...], axis=0, keepdims=True)

        @pl.when(i == 0)
        def _():
            acc[...] = part

        @pl.when(i > 0)
        def _():
            acc[...] += part

        @pl.when(i == n_steps - 1)
        def _():
            dgate_ref[...] = acc[...]

    row = pl.BlockSpec((tr, D), lambda i: (i, 0))
    vec = pl.BlockSpec((1, D), lambda i: (0, 0))
    return pl.pallas_call(
        body, name=name, grid=(n_steps,), in_specs=[row, row, vec], out_specs=[row, vec],
        out_shape=[jax.ShapeDtypeStruct((K, D), BF16), jax.ShapeDtypeStruct((1, D), F32)],
        scratch_shapes=[pltpu.VMEM((1, D), F32)], compiler_params=_params("arbitrary"),
    )(gmat, w, gate)


def _loss_head(y, target, *, name):
    S, D = y.shape
    tr = min(512, S)
    n_steps = S // tr

    def body(y_ref, t_ref, dy_ref, sse_ref, acc):
        i = pl.program_id(0)
        e = y_ref[...] - t_ref[...]
        dy_ref[...] = e * (1.0 / D)
        part = jnp.sum(e * e, axis=0, keepdims=True)

        @pl.when(i == 0)
        def _():
            acc[...] = part

        @pl.when(i > 0)
        def _():
            acc[...] += part

        @pl.when(i == n_steps - 1)
        def _():
            sse_ref[...] = jnp.sum(acc[...], axis=1, keepdims=True)

    row = pl.BlockSpec((tr, D), lambda i: (i, 0))
    return pl.pallas_call(
        body, name=name, grid=(n_steps,), in_specs=[row, row],
        out_specs=[row, pl.BlockSpec((1, 1), lambda i: (0, 0))],
        out_shape=[jax.ShapeDtypeStruct((S, D), F32), jax.ShapeDtypeStruct((1, 1), F32)],
        scratch_shapes=[pltpu.VMEM((1, D), F32)], compiler_params=_params("arbitrary"),
    )(y, target)


def _adamw(w, g_parts, m, v, *, name):
    R, C = w.shape
    P = g_parts.shape[0]
    tr = _tile(R, max(8, 1024 * LANES // C))
    c1 = 1.0 / (1.0 - ADAM_B1 ** ADAM_STEP)
    c2 = 1.0 / (1.0 - ADAM_B2 ** ADAM_STEP)

    def body(w_ref, g_ref, m_ref, v_ref, go_ref, d_ref, mo_ref, vo_ref):
        g = g_ref[0].astype(F32)
        for q in range(1, P):
            g = g + g_ref[q].astype(F32)
        mn = ADAM_B1 * m_ref[...] + (1.0 - ADAM_B1) * g
        vn = ADAM_B2 * v_ref[...] + (1.0 - ADAM_B2) * (g * g)
        go_ref[...] = g
        mo_ref[...] = mn
        vo_ref[...] = vn
        d_ref[...] = -ADAM_LR * ((mn * c1) / (jnp.sqrt(vn * c2) + ADAM_EPS) + ADAM_WD * w_ref[...])

    row = pl.BlockSpec((tr, C), lambda i: (i, 0))
    shp = jax.ShapeDtypeStruct((R, C), F32)
    return pl.pallas_call(
        body, name=name, grid=(R // tr,),
        in_specs=[row, pl.BlockSpec((P, tr, C), lambda i: (0, i, 0)), row, row],
        out_specs=[row, row, row, row], out_shape=[shp, shp, shp, shp],
        compiler_params=_params("parallel"),
    )(w, g_parts, m, v)


_HALO = 16


def _conv_taps(buf, w_ref, rows, cols):
    acc = None
    for j in range(GDN_CONV):
        term = buf[pl.ds(_HALO - (GDN_CONV - 1) + j, rows), cols] * w_ref[j:j + 1, cols]
        acc = term if acc is None else acc + term
    return acc


def _fill_conv_buf(buf, halo_ref, x_ref, rows, first):
    buf[0:_HALO, :] = jnp.where(first, 0.0, halo_ref[...].astype(F32))
    buf[_HALO:_HALO + rows, :] = x_ref[...].astype(F32)


_HM = 3 * GDN_DK
_GDN_ROWS = 256
_PREP_HEADS = 4


def _l2n(seg):
    return lax.rsqrt(jnp.sum(seg * seg, axis=-1, keepdims=True) + RMS_EPS)


def _head_cols(hh):
    return slice(hh * _HM, (hh + 1) * _HM)


def _gdn_prep_fwd(x, conv_w, *, name):
    S, C3 = x.shape
    CB = _PREP_HEADS * _HM
    RB = min(256, S)

    def body(x_ref, halo_ref, w_ref, o_ref, buf):
        i = pl.program_id(0)
        _fill_conv_buf(buf, halo_ref, x_ref, RB, i == 0)
        for hh in range(_PREP_HEADS):
            c0 = hh * _HM
            y = _silu(_conv_taps(buf, w_ref, RB, _head_cols(hh)))
            q, k = y[:, :GDN_DK], y[:, GDN_DK:2 * GDN_DK]
            o_ref[:, c0:c0 + GDN_DK] = q * (_l2n(q) * GDN_DK ** -0.5)
            o_ref[:, c0 + GDN_DK:c0 + 2 * GDN_DK] = k * _l2n(k)
            o_ref[:, c0 + 2 * GDN_DK:c0 + _HM] = y[:, 2 * GDN_DK:]

    hb = RB // _HALO
    return pl.pallas_call(
        body, name=name, grid=(S // RB, C3 // CB),
        in_specs=[pl.BlockSpec((RB, CB), lambda i, j: (i, j)),
                  pl.BlockSpec((_HALO, CB), lambda i, j: (jnp.maximum(i * hb - 1, 0), j)),
                  pl.BlockSpec((GDN_CONV, CB), lambda i, j: (0, j))],
        out_specs=pl.BlockSpec((RB, CB), lambda i, j: (i, j)),
        out_shape=jax.ShapeDtypeStruct((S, C3), F32),
        scratch_shapes=[pltpu.VMEM((RB + _HALO, CB), F32)],
        compiler_params=_params("parallel", "parallel"),
    )(x, x, conv_w)


def _gdn_prep_bwd_pre(dn, x, conv_w, *, name):
    S, C3 = x.shape
    CB = _PREP_HEADS * _HM
    RB = min(256, S)
    n_steps = S // RB

    def body(dn_ref, x_ref, halo_ref, w_ref, dc_ref, dw_ref, buf):
        i = pl.program_id(1)
        _fill_conv_buf(buf, halo_ref, x_ref, RB, i == 0)
        head_parts = []
        for hh in range(_PREP_HEADS):
            c0, cols = hh * _HM, _head_cols(hh)
            acc = _conv_taps(buf, w_ref, RB, cols)
            sg = _sigmoid(acc)
            y = acc * sg
            dsilu = sg * (1.0 + acc * (1.0 - sg))
            for part, scale in ((0, GDN_DK ** -0.5), (1, 1.0)):
                sl = slice(part * GDN_DK, (part + 1) * GDN_DK)
                seg = y[:, sl]
                r = _l2n(seg)
                n = seg * r
                d = dn_ref[:, c0 + part * GDN_DK:c0 + (part + 1) * GDN_DK] * scale
                dc_ref[:, c0 + part * GDN_DK:c0 + (part + 1) * GDN_DK] = (
                    r * (d - n * jnp.sum(d * n, axis=-1, keepdims=True)) * dsilu[:, sl])
            dc_ref[:, c0 + 2 * GDN_DK:c0 + _HM] = dn_ref[:, c0 + 2 * GDN_DK:c0 + _HM] * dsilu[:, 2 * GDN_DK:]
            dc = dc_ref[:, cols]
            taps = [jnp.sum(dc * buf[pl.ds(_HALO - (GDN_CONV - 1) + t, RB), cols], axis=0, keepdims=True)
                    for t in range(GDN_CONV)]
            head_parts.append(jnp.concatenate(taps + [jnp.zeros((8 - GDN_CONV, _HM), F32)], axis=0))
        part = jnp.concatenate(head_parts, axis=1)

        @pl.when(i == 0)
        def _():
            dw_ref[...] = part

        @pl.when(i > 0)
        def _():
            dw_ref[...] += part

    hb = RB // _HALO
    return pl.pallas_call(
        body, name=name, grid=(C3 // CB, n_steps),
        in_specs=[pl.BlockSpec((RB, CB), lambda j, i: (i, j)),
                  pl.BlockSpec((RB, CB), lambda j, i: (i, j)),
                  pl.BlockSpec((_HALO, CB), lambda j, i: (jnp.maximum(i * hb - 1, 0), j)),
                  pl.BlockSpec((GDN_CONV, CB), lambda j, i: (0, j))],
        out_specs=[pl.BlockSpec((RB, CB), lambda j, i: (i, j)),
                   pl.BlockSpec((8, CB), lambda j, i: (0, j))],
        out_shape=[jax.ShapeDtypeStruct((S, C3), F32), jax.ShapeDtypeStruct((8, C3), F32)],
        scratch_shapes=[pltpu.VMEM((RB + _HALO, CB), F32)],
        compiler_params=_params("parallel", "arbitrary"),
    )(dn, x, x, conv_w)


def _gdn_conv_bwd_x(dc, conv_w, *, name):
    S, C3 = dc.shape
    CB = _PREP_HEADS * _HM
    RB = min(256, S)
    n_steps = S // RB

    def body(dc_ref, halo_ref, w_ref, dx_ref, buf):
        i = pl.program_id(0)
        buf[0:RB, :] = dc_ref[...]
        buf[RB:RB + _HALO, :] = jnp.where(i == n_steps - 1, 0.0, halo_ref[...])
        for hh in range(_PREP_HEADS):
            cols = _head_cols(hh)
            acc = None
            for j in range(GDN_CONV):
                term = buf[pl.ds(GDN_CONV - 1 - j, RB), cols] * w_ref[j:j + 1, cols]
                acc = term if acc is None else acc + term
            dx_ref[:, cols] = acc.astype(BF16)

    hb = RB // _HALO
    last = S // _HALO - 1
    return pl.pallas_call(
        body, name=name, grid=(n_steps, C3 // CB),
        in_specs=[pl.BlockSpec((RB, CB), lambda i, j: (i, j)),
                  pl.BlockSpec((_HALO, CB), lambda i, j: (jnp.minimum((i + 1) * hb, last), j)),
                  pl.BlockSpec((GDN_CONV, CB), lambda i, j: (0, j))],
        out_specs=pl.BlockSpec((RB, CB), lambda i, j: (i, j)),
        out_shape=jax.ShapeDtypeStruct((S, C3), BF16),
        scratch_shapes=[pltpu.VMEM((RB + _HALO, CB), F32)],
        compiler_params=_params("parallel", "parallel"),
    )(dc, dc, conv_w)


def _split_bf16(a):
    hi = a.astype(BF16)
    return hi, (a - hi.astype(F32)).astype(BF16)


def _dot(a, b, dims="nn", exact=False):
    def dot(p, q):
        return lax.dot_general(p, q, _DOT_DIMS[dims], preferred_element_type=F32)

    if exact:
        (ah, al), (bh, bl) = _split_bf16(a), _split_bf16(b)
        return dot(ah, bh) + (dot(ah, bl) + dot(al, bh))
    return dot(a.astype(BF16), b.astype(BF16))


def _softplus(x):
    return jnp.maximum(x, 0.0) + jnp.log(1.0 + jnp.exp(-jnp.abs(x)))


def _to_col(row, eye):
    return jnp.sum(jnp.where(eye, row, 0.0), axis=1, keepdims=True)


def _to_row(col, eye):
    return jnp.sum(jnp.where(eye, col, 0.0), axis=0, keepdims=True)


def _unit_lower_inverse(low, ri, ci):
    n = range(len(low))
    C = low[0].shape[0]
    eye = jnp.where(ri == ci, 1.0, 0.0)
    pair = (ri >> 1) == (ci >> 1)
    x = [eye - jnp.where(pair, low[j], 0.0) for j in n]
    m, sh = 2, 1
    while m < C:
        join = ((ri >> (sh + 1)) == (ci >> (sh + 1))) & (((ri >> sh) & 1) == 1) & (((ci >> sh) & 1) == 0)
        y = [_dot(x[j], jnp.where(join, low[j], 0.0)) for j in n]
        x = [x[j] - _dot(y[j], x[j]) for j in n]
        m, sh = 2 * m, sh + 1
    lx = [_dot(low[j], x[j], exact=True) for j in n]
    corr = [_dot(x[j], eye - x[j] - lx[j]) for j in n]
    return [x[j] + corr[j] for j in n]


def _gdn_local_batch(qkv, g_row, beta_row, ri, ci):
    n = range(len(qkv))
    eye, tril, strict = ri == ci, ri >= ci, ri > ci
    q = [qkv[j][:, :GDN_DK] for j in n]
    k = [qkv[j][:, GDN_DK:2 * GDN_DK] for j in n]
    v = [qkv[j][:, 2 * GDN_DK:] for j in n]
    g_col = [_to_col(g_row[j], eye) for j in n]
    beta_col = [_to_col(beta_row[j], eye) for j in n]
    gc_col = [jnp.sum(jnp.where(tril, g_row[j], 0.0), axis=1, keepdims=True) for j in n]
    gc_row = [jnp.sum(jnp.where(ri <= ci, g_col[j], 0.0), axis=0, keepdims=True) for j in n]
    g_last = [jnp.sum(g_row[j], axis=1, keepdims=True) for j in n]
    decay = [jnp.where(tril, jnp.exp(jnp.minimum(gc_col[j] - gc_row[j], 0.0)), 0.0) for j in n]
    e_col = [jnp.exp(gc_col[j]) for j in n]
    f_col = [jnp.exp(g_last[j] - gc_col[j]) for j in n]
    e_last = [jnp.exp(g_last[j]) for j in n]
    kb = [k[j] * beta_col[j] for j in n]
    vb = [v[j] * beta_col[j] for j in n]
    kk = [_dot(kb[j], k[j], "nt") for j in n]
    qk = [_dot(q[j], k[j], "nt") for j in n]
    low = [jnp.where(strict, kk[j] * decay[j], 0.0) for j in n]
    att = [qk[j] * decay[j] for j in n]
    return dict(q=q, k=k, v=v, beta_col=beta_col, decay=decay, e_col=e_col, f_col=f_col, e_last=e_last,
                kb=kb, vb=vb, low=low, att=att, eye=eye, strict=strict, tril=tril)


def _chunk_iotas():
    C = GDN_CHUNK
    return lax.broadcasted_iota(jnp.int32, (C, C), 0), lax.broadcasted_iota(jnp.int32, (C, C), 1)


def _gdn_chunk_fwd(qkv, ab, a_log, dt_bias, *, name, riding=None):
    S = qkv.shape[0]
    H, C, DK = GDN_HEADS, GDN_CHUNK, GDN_DK
    RB = min(_GDN_ROWS, S)
    NCB, NB, NC = RB // C, S // RB, S // C
    heads = range(H)

    def body(qkv_ref, ab_ref, alog_ref, dtb_ref, *rest):
        if riding is None:
            o_ref, st_ref, t_ref, state, u_s, w_s, qe_s, kf_s, att_s = rest
        else:
            ride_src, o_ref, st_ref, t_ref, ride_dst, state, u_s, w_s, qe_s, kf_s, att_s, *ride_sems = rest
        nb = pl.program_id(0)
        if riding is not None:
            finish_ride = _ride(nb == 0, nb == NB - 1, ride_src, ride_dst, ride_sems, True)

        @pl.when(nb == 0)
        def _():
            state[...] = jnp.zeros_like(state)

        ri, ci = _chunk_iotas()
        neg_a = [-jnp.exp(alog_ref[h]) for h in heads]
        e_last = []
        for c in range(NCB):
            rows = pl.ds(c * C, C)
            g_row = [neg_a[h] * _softplus(ab_ref[h, c] + dtb_ref[h]) for h in heads]
            beta_row = [_sigmoid(ab_ref[H + h, c]) for h in heads]
            L = _gdn_local_batch([qkv_ref[rows, h * _HM:(h + 1) * _HM] for h in heads], g_row, beta_row, ri, ci)
            tinv = _unit_lower_inverse(L["low"], ri, ci)
            u = [_dot(tinv[h], L["vb"][h], exact=True) for h in heads]
            w = [_dot(tinv[h], L["kb"][h] * L["e_col"][h], exact=True) for h in heads]
            for h in heads:
                t_ref[h, c] = tinv[h]
                u_s[c, h] = u[h]
                w_s[c, h] = w[h].astype(BF16)
                qe_s[c, h] = (L["q"][h] * L["e_col"][h]).astype(BF16)
                kf_s[c, h] = (L["k"][h] * L["f_col"][h]).astype(BF16)
                att_s[c, h] = L["att"][h].astype(BF16)
            e_last.append(L["e_last"])
        st = [state[h] for h in heads]
        for c in range(NCB):
            rows = pl.ds(c * C, C)
            stb = [st[h].astype(BF16) for h in heads]
            vn = [u_s[c, h] - _dot(w_s[c, h], stb[h]) for h in heads]
            vnb = [vn[h].astype(BF16) for h in heads]
            out = [_dot(qe_s[c, h], stb[h]) + _dot(att_s[c, h], vnb[h]) for h in heads]
            new = [st[h] * e_last[c][h] + _dot(kf_s[c, h], vnb[h], "tn") for h in heads]
            for h in heads:
                o_ref[rows, h * DK:(h + 1) * DK] = out[h]
                st_ref[h, c] = st[h]
            st = new
        for h in heads:
            state[h] = st[h]
        if riding is not None:
            finish_ride()

    ride_args, ride_specs, ride_out, ride_scratch = _riding(riding, True)
    return pl.pallas_call(
        body, name=name, grid=(NB,),
        in_specs=[pl.BlockSpec((RB, H * _HM), lambda n: (n, 0)),
                  pl.BlockSpec((2 * H, NCB, 1, C), lambda n: (0, n, 0, 0)),
                  pl.BlockSpec((H, 1, 1), lambda n: (0, 0, 0)),
                  pl.BlockSpec((H, 1, 1), lambda n: (0, 0, 0))] + ride_specs,
        out_specs=[pl.BlockSpec((RB, H * DK), lambda n: (n, 0)),
                   pl.BlockSpec((H, NCB, DK, DK), lambda n: (0, n, 0, 0)),
                   pl.BlockSpec((H, NCB, C, C), lambda n: (0, n, 0, 0))] + ride_specs,
        out_shape=[jax.ShapeDtypeStruct((S, H * DK), F32),
                   jax.ShapeDtypeStruct((H, NC, DK, DK), F32),
                   jax.ShapeDtypeStruct((H, NC, C, C), F32)] + ride_out,
        scratch_shapes=[pltpu.VMEM((H, DK, DK), F32), pltpu.VMEM((NCB, H, C, DK), F32),
                        pltpu.VMEM((NCB, H, C, DK), BF16), pltpu.VMEM((NCB, H, C, DK), BF16),
                        pltpu.VMEM((NCB, H, C, DK), BF16), pltpu.VMEM((NCB, H, C, C), BF16)] + ride_scratch,
        compiler_params=_params("arbitrary"),
    )(qkv, ab, a_log, dt_bias, *ride_args)


def _chip_copies(src_ref, dst_ref, send_sems, recv_sems, local_sem, gather=False):
    x, y, c = lax.axis_index("x"), lax.axis_index("y"), lax.axis_index("c")
    here = 2 * x + y
    landing = dst_ref.at[here, c] if gather else dst_ref.at[here]
    mine = pltpu.make_async_copy(src_ref if gather else src_ref.at[here], landing, local_sem)
    copies = []
    for rel in range(1, N_DEV // 2):
        px = 1 - x if rel & 2 else x
        py = 1 - y if rel & 1 else y
        copies.append(pltpu.make_async_remote_copy(
            src_ref=src_ref if gather else src_ref.at[2 * px + py], dst_ref=landing,
            send_sem=send_sems.at[rel - 1], recv_sem=recv_sems.at[rel - 1],
            device_id=(px, py, c), device_id_type=pl.DeviceIdType.MESH))
    return mine, copies


def _riding(riding, gather):
    if riding is None:
        return [], [], [], []
    shape = (N_DEV // 2, 2) + riding.shape if gather else riding.shape
    n_peers = N_DEV // 2 - 1
    return ([riding], [pl.BlockSpec(memory_space=pl.ANY)], [jax.ShapeDtypeStruct(shape, riding.dtype)],
            [pltpu.SemaphoreType.DMA((n_peers,)), pltpu.SemaphoreType.DMA((n_peers,)), pltpu.SemaphoreType.DMA(())])


def _ride(first, last, src, dst, sems, gather):
    @pl.when(first)
    def _():
        mine, copies = _chip_copies(src, dst, *sems, gather=gather)
        mine.start()
        for cp in copies:
            cp.start()

    def finish():
        @pl.when(last)
        def _():
            mine, copies = _chip_copies(src, dst, *sems, gather=gather)
            for cp in copies:
                cp.wait()
            mine.wait()

    return finish


def _gdn_chunk_bwd(qkv, ab, a_log, dt_bias, states, tinvs, do, *, name, riding=None):
    S = qkv.shape[0]
    H, C, DK = GDN_HEADS, GDN_CHUNK, GDN_DK
    RB = min(_GDN_ROWS, S)
    NCB, NB, NC = RB // C, S // RB, S // C
    heads = range(H)

    def body(qkv_ref, ab_ref, alog_ref, dtb_ref, st_ref, t_ref, do_ref, *rest):
        if riding is None:
            dqkv_ref, dab_ref, dalog_ref, ddtb_ref, dstate, w_s, vn_s, qe_s, kf_s, att_s, dvn_s, dkf_s = rest
        else:
            (ride_src, dqkv_ref, dab_ref, dalog_ref, ddtb_ref, ride_dst,
             dstate, w_s, vn_s, qe_s, kf_s, att_s, dvn_s, dkf_s, *ride_sems) = rest
        nb = pl.program_id(0)
        if riding is not None:
            finish_ride = _ride(nb == 0, nb == NB - 1, ride_src, ride_dst, ride_sems, False)

        @pl.when(nb == 0)
        def _():
            dstate[...] = jnp.zeros_like(dstate)
            dalog_ref[...] = jnp.zeros_like(dalog_ref)
            ddtb_ref[...] = jnp.zeros_like(ddtb_ref)

        ri, ci = _chunk_iotas()
        neg_a = [-jnp.exp(alog_ref[h]) for h in heads]

        def local(c):
            rows = pl.ds(c * C, C)
            a_pre = [ab_ref[h, c] + dtb_ref[h] for h in heads]
            g_row = [neg_a[h] * _softplus(a_pre[h]) for h in heads]
            beta_row = [_sigmoid(ab_ref[H + h, c]) for h in heads]
            L = _gdn_local_batch([qkv_ref[rows, h * _HM:(h + 1) * _HM] for h in heads], g_row, beta_row, ri, ci)
            return L, a_pre, g_row, beta_row

        e_last = [None] * NCB
        for c in range(NCB):
            L, _, _, _ = local(c)
            kbe = [L["kb"][h] * L["e_col"][h] for h in heads]
            u = [_dot(t_ref[h, c], L["vb"][h], exact=True) for h in heads]
            w = [_dot(t_ref[h, c], kbe[h], exact=True) for h in heads]
            vn = [u[h] - _dot(w[h], st_ref[h, c]) for h in heads]
            for h in heads:
                w_s[c, h] = w[h].astype(BF16)
                vn_s[c, h] = vn[h].astype(BF16)
                qe_s[c, h] = (L["q"][h] * L["e_col"][h]).astype(BF16)
                kf_s[c, h] = (L["k"][h] * L["f_col"][h]).astype(BF16)
                att_s[c, h] = L["att"][h].astype(BF16)
            e_last[c] = L["e_last"]

        dst = [dstate[h] for h in heads]
        de_last = [None] * NCB
        for c in reversed(range(NCB)):
            rows = pl.ds(c * C, C)
            dob = [do_ref[rows, h * DK:(h + 1) * DK].astype(BF16) for h in heads]
            dstb = [dst[h].astype(BF16) for h in heads]
            dvn = [_dot(att_s[c, h], dob[h], "tn") + _dot(kf_s[c, h], dstb[h]) for h in heads]
            dkf = [_dot(vn_s[c, h], dstb[h], "nt") for h in heads]
            de_last[c] = [jnp.sum(jnp.sum(dst[h] * st_ref[h, c], axis=1, keepdims=True), axis=0, keepdims=True)
                          for h in heads]
            new = [dst[h] * e_last[c][h] + _dot(qe_s[c, h], dob[h], "tn")
                   - _dot(w_s[c, h], dvn[h].astype(BF16), "tn") for h in heads]
            for h in heads:
                dvn_s[c, h] = dvn[h]
                dkf_s[c, h] = dkf[h]
            dst = new
        for h in heads:
            dstate[h] = dst[h]

        for c in range(NCB):
            rows = pl.ds(c * C, C)
            L, a_pre, g_row, beta_row = local(c)
            q, k, v, kb, vb = L["q"], L["k"], L["v"], L["kb"], L["vb"]
            e_col, f_col, decay, beta_col = L["e_col"], L["f_col"], L["decay"], L["beta_col"]
            eye, strict, tril = L["eye"], L["strict"], L["tril"]
            tinv = [t_ref[h, c] for h in heads]
            stb = [st_ref[h, c].astype(BF16) for h in heads]
            dov = [do_ref[rows, h * DK:(h + 1) * DK] for h in heads]
            dvn = [dvn_s[c, h] for h in heads]
            dkf = [dkf_s[c, h] for h in heads]
            kbe = [kb[h] * e_col[h] for h in heads]
            datt = [jnp.where(tril, _dot(dov[h], vn_s[c, h], "nt"), 0.0) for h in heads]
            dqe = [_dot(dov[h], stb[h], "nt") for h in heads]
            dw = [-_dot(dvn[h], stb[h], "nt") for h in heads]
            dt = [_dot(dvn[h], vb[h], "nt") + _dot(dw[h], kbe[h], "nt") for h in heads]
            dvb = [_dot(tinv[h], dvn[h], "tn", exact=True) for h in heads]
            dkbe = [_dot(tinv[h], dw[h], "tn", exact=True) for h in heads]
            tdt = [_dot(tinv[h], dt[h], "tn", exact=True) for h in heads]
            dlow = [-jnp.where(strict, _dot(tdt[h], tinv[h], "nt", exact=True), 0.0) for h in heads]
            dkk = [dlow[h] * decay[h] for h in heads]
            dqk = [datt[h] * decay[h] for h in heads]
            dkb = [_dot(dkk[h], k[h]) + dkbe[h] * e_col[h] for h in heads]
            dk = [_dot(dkk[h], kb[h], "tn") + _dot(dqk[h], q[h], "tn") + dkf[h] * f_col[h] + dkb[h] * beta_col[h]
                  for h in heads]
            dq = [_dot(dqk[h], k[h]) + dqe[h] * e_col[h] for h in heads]
            for h in heads:
                dqkv_ref[rows, h * _HM:h * _HM + DK] = dq[h]
                dqkv_ref[rows, h * _HM + DK:h * _HM + 2 * DK] = dk[h]
                dqkv_ref[rows, h * _HM + 2 * DK:(h + 1) * _HM] = dvb[h] * beta_col[h]

            dbeta_col = [jnp.sum(k[h] * dkb[h] + v[h] * dvb[h], axis=1, keepdims=True) for h in heads]
            pmat = [dlow[h] * L["low"][h] + datt[h] * L["att"][h] for h in heads]
            df_col = [jnp.sum(k[h] * dkf[h], axis=1, keepdims=True) * f_col[h] for h in heads]
            dgc_col = [jnp.sum(pmat[h], axis=1, keepdims=True)
                       + jnp.sum(q[h] * dqe[h] + kb[h] * dkbe[h], axis=1, keepdims=True) * e_col[h] - df_col[h]
                       for h in heads]
            dgc_row = [_to_row(dgc_col[h], eye) - jnp.sum(pmat[h], axis=0, keepdims=True) for h in heads]
            dg_last = [jnp.sum(df_col[h], axis=0, keepdims=True) + de_last[c][h] * L["e_last"][h] for h in heads]
            dgc_c = [_to_col(dgc_row[h], eye) for h in heads]
            dg_row = [jnp.sum(jnp.where(ri >= ci, dgc_c[h], 0.0), axis=0, keepdims=True) + dg_last[h] for h in heads]
            dbeta_row = [_to_row(dbeta_col[h], eye) for h in heads]
            for h in heads:
                da_row = dg_row[h] * neg_a[h] * _sigmoid(a_pre[h])
                dab_ref[h, c] = da_row
                dab_ref[H + h, c] = dbeta_row[h] * beta_row[h] * (1.0 - beta_row[h])
                dalog_ref[h] += jnp.sum(dg_row[h] * g_row[h], axis=1, keepdims=True)
                ddtb_ref[h] += jnp.sum(da_row, axis=1, keepdims=True)

        if riding is not None:
            finish_ride()

    rev = lambda n: NB - 1 - n
    vec = pl.BlockSpec((H, 1, 1), lambda n: (0, 0, 0))
    gates = pl.BlockSpec((2 * H, NCB, 1, C), lambda n: (0, rev(n), 0, 0))
    wide = pl.BlockSpec((RB, H * _HM), lambda n: (rev(n), 0))
    item = lambda dt: pltpu.VMEM((NCB, H, C, DK), dt)
    ride_args, ride_specs, ride_out, ride_scratch = _riding(riding, False)
    return pl.pallas_call(
        body, name=name, grid=(NB,),
        in_specs=[wide, gates, vec, vec,
                  pl.BlockSpec((H, NCB, DK, DK), lambda n: (0, rev(n), 0, 0)),
                  pl.BlockSpec((H, NCB, C, C), lambda n: (0, rev(n), 0, 0)),
                  pl.BlockSpec((RB, H * DK), lambda n: (rev(n), 0))] + ride_specs,
        out_specs=[wide, gates, vec, vec] + ride_specs,
        out_shape=[jax.ShapeDtypeStruct((S, H * _HM), F32),
                   jax.ShapeDtypeStruct((2 * H, NC, 1, C), F32),
                   jax.ShapeDtypeStruct((H, 1, 1), F32),
                   jax.ShapeDtypeStruct((H, 1, 1), F32)] + ride_out,
        scratch_shapes=[pltpu.VMEM((H, DK, DK), F32), item(BF16), item(BF16), item(BF16), item(BF16),
                        pltpu.VMEM((NCB, H, C, C), BF16), item(F32), item(F32)] + ride_scratch,
        compiler_params=_params("arbitrary"),
    )(qkv, ab, a_log, dt_bias, states, tinvs, do, *ride_args)


def _gdn_outnorm_fwd(o, z, gain, *, name):
    S, HV = o.shape
    RB = min(256, S)

    def body(o_ref, z_ref, g_ref, y_ref):
        for h in range(HV // GDN_DK):
            cols = slice(h * GDN_DK, (h + 1) * GDN_DK)
            ov = o_ref[:, cols]
            r = lax.rsqrt(jnp.mean(ov * ov, axis=-1, keepdims=True) + RMS_EPS)
            y_ref[:, cols] = (ov * r * g_ref[...] * _silu(z_ref[:, cols].astype(F32))).astype(BF16)

    blk = pl.BlockSpec((RB, HV), lambda i: (i, 0))
    return pl.pallas_call(
        body, name=name, grid=(S // RB,),
        in_specs=[blk, blk, pl.BlockSpec((1, GDN_DK), lambda i: (0, 0))], out_specs=blk,
        out_shape=jax.ShapeDtypeStruct((S, HV), BF16), compiler_params=_params("parallel"),
    )(o, z, gain)


def _gdn_outnorm_bwd(dy, o, z, gain, *, name):
    S, HV = o.shape
    RB = min(256, S)

    def body(dy_ref, o_ref, z_ref, g_ref, do_ref, dz_ref, dg_ref):
        part = None
        for h in range(HV // GDN_DK):
            cols = slice(h * GDN_DK, (h + 1) * GDN_DK)
            ov = o_ref[:, cols]
            zv = z_ref[:, cols].astype(F32)
            dyv = dy_ref[:, cols].astype(F32)
            r = lax.rsqrt(jnp.mean(ov * ov, axis=-1, keepdims=True) + RMS_EPS)
            n = ov * r
            sg = _sigmoid(zv)
            dng = dyv * (zv * sg)
            dn = dng * g_ref[...]
            do_ref[:, cols] = r * (dn - n * jnp.mean(dn * n, axis=-1, keepdims=True))
            dz_ref[:, cols] = (dyv * (n * g_ref[...]) * (sg * (1.0 + zv * (1.0 - sg)))).astype(BF16)
            p = jnp.sum(dng * n, axis=0, keepdims=True)
            part = p if part is None else part + p

        @pl.when(pl.program_id(0) == 0)
        def _():
            dg_ref[...] = part

        @pl.when(pl.program_id(0) > 0)
        def _():
            dg_ref[...] += part

    blk = pl.BlockSpec((RB, HV), lambda i: (i, 0))
    vec = pl.BlockSpec((1, GDN_DK), lambda i: (0, 0))
    return pl.pallas_call(
        body, name=name, grid=(S // RB,),
        in_specs=[blk, blk, blk, vec], out_specs=[blk, blk, vec],
        out_shape=[jax.ShapeDtypeStruct((S, HV), F32), jax.ShapeDtypeStruct((S, HV), BF16),
                   jax.ShapeDtypeStruct((1, GDN_DK), F32)],
        compiler_params=_params("arbitrary"),
    )(dy, o, z, gain)


def _head_mask():
    return lax.broadcasted_iota(jnp.int32, (DSW_BLK, LANES), 1) < DSW_DH


def _per_head_sum(t, first):
    s0 = jnp.sum(jnp.where(first, t, 0.0), axis=-1, keepdims=True)
    s1 = jnp.sum(jnp.where(first, 0.0, t), axis=-1, keepdims=True)
    return jnp.where(first, s0, s1)


def _rms2(x, gain, first):
    r = lax.rsqrt(_per_head_sum(x * x, first) * (1.0 / DSW_DH) + RMS_EPS)
    xh = x * r
    return xh, r, xh * gain


def _rms2_bwd(dy, xh, r, gain, first):
    dxh = dy * gain
    return r * (dxh - xh * (_per_head_sum(dxh * xh, first) * (1.0 / DSW_DH)))


def _split_heads(x, first):
    return [jnp.where(first, x, 0.0).astype(BF16), jnp.where(first, 0.0, x).astype(BF16)]


_HP = LANES // DSW_DH
_DSW_W = DSW_HEADS * DSW_DH
_DSW_ROWS = 1024
_DSW_BATCH = 8


def _dsw_geometry(S, g):
    d = DSW_GROUPS[g][1]
    slab = DSW_BLK * d
    tb = max(1, min(_DSW_ROWS, S) // slab)
    return d, slab, tb, S // (tb * slab)


def _block_rows(t, r, slab, d):
    return pl.ds(t * slab + r, DSW_BLK) if d == 1 else pl.ds(t * slab + r, DSW_BLK, stride=d)


def _dsw_attn_fwd(q, k, v, bias, q_gain, k_gain, prev_out, *, g, name):
    S, WT = q.shape
    B = DSW_BLK
    d, slab, tb, n_tiles = _dsw_geometry(S, g)
    rt = tb * slab
    cb = g * (_DSW_W // LANES)
    batch_res = max(1, _DSW_BATCH // tb)

    def body(q_ref, kp_ref, kc_ref, vp_ref, vc_ref, bias_ref, qg_ref, kg_ref, *rest):
        o_ref, lse_ref = rest[-2:]
        i = pl.program_id(1)
        qg, kg = qg_ref[...] * DSW_DH ** -0.5, kg_ref[...]
        col = lax.broadcasted_iota(jnp.int32, (B, 2 * B), 1)
        first = _head_mask()
        heads = range(_HP)
        for r0 in range(0, d, batch_res):
            res = range(r0, min(d, r0 + batch_res))
            k_raw = {(r, -1): kp_ref[_block_rows(0, r, slab, d), :] for r in res}
            v_raw = {(r, -1): vp_ref[_block_rows(0, r, slab, d), :] for r in res}
            q_raw = {}
            for r in res:
                for t in range(tb):
                    rows = _block_rows(t, r, slab, d)
                    q_raw[r, t], k_raw[r, t], v_raw[r, t] = q_ref[rows, :], kc_ref[rows, :], vc_ref[rows, :]
            kn = {key: _rms2(x, kg, first)[2].astype(BF16) for key, x in k_raw.items()}
            vb = {key: x.astype(BF16) for key, x in v_raw.items()}
            qn = {key: _split_heads(_rms2(x, qg, first)[2], first) for key, x in q_raw.items()}
            items = [(r, t, h) for r in res for t in range(tb) for h in heads]
            s = {}
            for r, t, h in items:
                sv = _dot(qn[r, t][h], jnp.concatenate([kn[r, t - 1], kn[r, t]], axis=0), "nt") + bias_ref[h]
                s[r, t, h] = jnp.where((i == 0) & (col < B), NEG_BIG, sv) if t == 0 else sv
            m = {it: jnp.max(s[it], axis=-1, keepdims=True) for it in items}
            p = {it: jnp.exp(s[it] - m[it]) for it in items}
            l = {it: jnp.sum(p[it], axis=-1, keepdims=True) for it in items}
            o = {(r, t, h): _dot(p[r, t, h], jnp.concatenate([vb[r, t - 1], vb[r, t]], axis=0)) for r, t, h in items}
            for r in res:
                for t in range(tb):
                    rows = _block_rows(t, r, slab, d)
                    o_ref[rows, :] = jnp.where(first, o[r, t, 0] / l[r, t, 0], o[r, t, 1] / l[r, t, 1])
                    lse_ref[rows, :] = jnp.where(first, m[r, t, 0] + jnp.log(l[r, t, 0]),
                                                 m[r, t, 1] + jnp.log(l[r, t, 1]))

    cur = pl.BlockSpec((rt, LANES), lambda hp, i: (i, cb + hp))
    prev = pl.BlockSpec((slab, LANES), lambda hp, i: (jnp.maximum(i * tb - 1, 0), cb + hp))
    vec = pl.BlockSpec((1, LANES), lambda hp, i: (0, 0))
    shp = jax.ShapeDtypeStruct((S, WT), F32)
    carried = [] if prev_out is None else list(prev_out)
    n_in = 8
    return pl.pallas_call(
        body, name=name, grid=(_DSW_W // LANES, n_tiles),
        in_specs=[cur, prev, cur, prev, cur, pl.BlockSpec((_HP, B, 2 * B), lambda hp, i: (hp, 0, 0)), vec, vec]
                 + [pl.BlockSpec(memory_space=pl.ANY)] * len(carried),
        out_specs=[cur, cur], out_shape=[shp, shp],
        input_output_aliases={n_in + j: j for j in range(len(carried))},
        compiler_params=_params("parallel", "parallel"),
    )(q, k, k, v, v, bias, jnp.tile(q_gain, (1, _HP)), jnp.tile(k_gain, (1, _HP)), *carried)


def _dsw_merge(o_g, lse_g, *, name):
    S = o_g.shape[0]
    W, G = _DSW_W, len(DSW_GROUPS)
    tr = min(512, S)

    def body(o_ref, l_ref, out_ref, lse_ref):
        ls = [l_ref[:, g * W:(g + 1) * W] for g in range(G)]
        m = ls[0]
        for g in range(1, G):
            m = jnp.maximum(m, ls[g])
        den = jnp.zeros_like(m)
        acc = jnp.zeros_like(m)
        for g in range(G):
            wg = jnp.exp(ls[g] - m)
            den = den + wg
            acc = acc + wg * o_ref[:, g * W:(g + 1) * W]
        out_ref[...] = acc / den
        lse_ref[...] = m + jnp.log(den)

    wide = pl.BlockSpec((tr, G * W), lambda i: (i, 0))
    blk = pl.BlockSpec((tr, W), lambda i: (i, 0))
    shp = jax.ShapeDtypeStruct((S, W), F32)
    return pl.pallas_call(
        body, name=name, grid=(S // tr,), in_specs=[wide, wide], out_specs=[blk, blk],
        out_shape=[shp, shp], compiler_params=_params("parallel"),
    )(o_g, lse_g)


def _dsw_attn_bwd(q, k, v, o, lse, do, bias, q_gain, k_gain, prev_out, *, g, name):
    S, WT = q.shape
    B = DSW_BLK
    d, slab, tb, n_tiles = _dsw_geometry(S, g)
    rt = tb * slab
    cb = g * (_DSW_W // LANES)
    n_slabs = S // slab
    scale = DSW_DH ** -0.5
    batch_res = max(1, _DSW_BATCH // tb)

    def body(q_ref, qx_ref, kp_ref, kc_ref, vp_ref, vc_ref, o_ref, ox_ref, l_ref, lx_ref, do_ref, dox_ref,
             bias_ref, qg_ref, kg_ref, *rest):
        dq_ref, dk_ref, dv_ref, db_ref, dqg_ref, dkg_ref = rest[-6:]
        hp, i = pl.program_id(0), pl.program_id(1)
        qg, kg = qg_ref[...] * scale, kg_ref[...]
        col = lax.broadcasted_iota(jnp.int32, (B, 2 * B), 1)
        has_next = i < n_tiles - 1

        @pl.when(i == 0)
        def _():
            db_ref[...] = jnp.zeros_like(db_ref)

        dqg_acc = jnp.zeros((1, LANES), F32)
        dkg_acc = jnp.zeros((1, LANES), F32)
        first = _head_mask()
        heads = range(_HP)
        for r0 in range(0, d, batch_res):
            res = range(r0, min(d, r0 + batch_res))
            q_raw, k_raw, v_raw, o_raw, l_raw, do_raw = {}, {}, {}, {}, {}, {}
            for r in res:
                first_rows = _block_rows(0, r, slab, d)
                k_raw[r, -1], v_raw[r, -1] = kp_ref[first_rows, :], vp_ref[first_rows, :]
                for t in range(tb):
                    rows = _block_rows(t, r, slab, d)
                    q_raw[r, t], o_raw[r, t], l_raw[r, t], do_raw[r, t] = (
                        q_ref[rows, :], o_ref[rows, :], l_ref[rows, :], do_ref[rows, :])
                    k_raw[r, t], v_raw[r, t] = kc_ref[rows, :], vc_ref[rows, :]
                q_raw[r, tb], o_raw[r, tb], l_raw[r, tb], do_raw[r, tb] = (
                    qx_ref[first_rows, :], ox_ref[first_rows, :], lx_ref[first_rows, :], dox_ref[first_rows, :])
            kk = {key: _rms2(x, kg, first) for key, x in k_raw.items()}
            qq = {key: _rms2(x, qg, first) for key, x in q_raw.items()}
            knb = {key: kk[key][2].astype(BF16) for key in kk}
            qnb = {key: _split_heads(qq[key][2], first) for key in qq}
            vb = {key: x.astype(BF16) for key, x in v_raw.items()}
            dob = {key: _split_heads(x, first) for key, x in do_raw.items()}
            delta = {key: _per_head_sum(do_raw[key] * o_raw[key], first) for key in q_raw}
            pick = lambda x, h: x[:, h * DSW_DH:h * DSW_DH + 1]
            full = [(r, t, h) for r in res for t in range(tb) for h in heads]
            half = [(r, tb, h) for r in res for h in heads]
            s = {}
            for r, t, h in full:
                sv = _dot(qnb[r, t][h], jnp.concatenate([knb[r, t - 1], knb[r, t]], axis=0), "nt") + bias_ref[h]
                s[r, t, h] = jnp.where((i == 0) & (col < B), NEG_BIG, sv) if t == 0 else sv
            for r, t, h in half:
                s[r, t, h] = _dot(qnb[r, t][h], knb[r, t - 1], "nt") + bias_ref[h, :, 0:B]
            p = {(r, t, h): jnp.exp(s[r, t, h] - pick(l_raw[r, t], h)) for r, t, h in full}
            for r, t, h in half:
                p[r, t, h] = jnp.where(has_next, jnp.exp(s[r, t, h] - pick(l_raw[r, t], h)), 0.0)
            dp = {(r, t, h): _dot(dob[r, t][h], jnp.concatenate([vb[r, t - 1], vb[r, t]], axis=0), "nt")
                  for r, t, h in full}
            for r, t, h in half:
                dp[r, t, h] = _dot(dob[r, t][h], vb[r, t - 1], "nt")
            ds = {(r, t, h): p[r, t, h] * (dp[r, t, h] - pick(delta[r, t], h)) for r, t, h in full + half}
            pb = {it: p[it].astype(BF16) for it in ds}
            dsb = {it: ds[it].astype(BF16) for it in ds}
            for h in heads:
                tot = None
                for r in res:
                    for t in range(tb):
                        tot = ds[r, t, h] if tot is None else tot + ds[r, t, h]
                db_ref[h] += tot
            blocks = [(r, t) for r in res for t in range(tb)]
            keys2 = {(r, t): jnp.concatenate([knb[r, t - 1], knb[r, t]], axis=0) for r, t in blocks}
            dqn = {(r, t): jnp.where(first, _dot(dsb[r, t, 0], keys2[r, t]), _dot(dsb[r, t, 1], keys2[r, t]))
                   for r, t in blocks}
            prev_half = lambda x, r, t, h: x[r, t, h][:, :B] if t < tb else x[r, t, h]
            dkn = {(r, t): sum(_dot(dsb[r, t, h][:, B:], qnb[r, t][h], "tn")
                               + _dot(prev_half(dsb, r, t + 1, h), qnb[r, t + 1][h], "tn") for h in heads)
                   for r, t in blocks}
            dvv = {(r, t): sum(_dot(pb[r, t, h][:, B:], dob[r, t][h], "tn")
                               + _dot(prev_half(pb, r, t + 1, h), dob[r, t + 1][h], "tn") for h in heads)
                   for r, t in blocks}
            for r, t in blocks:
                dqg_acc = dqg_acc + jnp.sum(dqn[r, t] * qq[r, t][0], axis=0, keepdims=True)
                dkg_acc = dkg_acc + jnp.sum(dkn[r, t] * kk[r, t][0], axis=0, keepdims=True)
            for r, t in blocks:
                rows = _block_rows(t, r, slab, d)
                dq_ref[rows, :] = _rms2_bwd(dqn[r, t], qq[r, t][0], qq[r, t][1], qg, first)
                dk_ref[rows, :] = _rms2_bwd(dkn[r, t], kk[r, t][0], kk[r, t][1], kg, first)
                dv_ref[rows, :] = dvv[r, t]

        start = (hp == 0) & (i == 0)
        fold = lambda a: a[:, :DSW_DH] + a[:, DSW_DH:]

        @pl.when(start)
        def _():
            dqg_ref[...] = fold(dqg_acc) * scale
            dkg_ref[...] = fold(dkg_acc)

        @pl.when(jnp.logical_not(start))
        def _():
            dqg_ref[...] += fold(dqg_acc) * scale
            dkg_ref[...] += fold(dkg_acc)

    def spec(rows, pick, base):
        return pl.BlockSpec((rows, LANES), lambda hp, i: (pick(i), base + hp))

    same = lambda i: i
    before = lambda i: jnp.maximum(i * tb - 1, 0)
    after = lambda i: jnp.minimum((i + 1) * tb, n_slabs - 1)
    cur, cur1 = spec(rt, same, cb), spec(rt, same, 0)
    vec = pl.BlockSpec((1, DSW_DH), lambda hp, i: (0, 0))
    vec2 = pl.BlockSpec((1, LANES), lambda hp, i: (0, 0))
    bspec = pl.BlockSpec((_HP, B, 2 * B), lambda hp, i: (hp, 0, 0))
    shp = jax.ShapeDtypeStruct((S, WT), F32)
    vshp = jax.ShapeDtypeStruct((1, DSW_DH), F32)
    carried = [] if prev_out is None else list(prev_out)
    n_in = 15
    return pl.pallas_call(
        body, name=name, grid=(_DSW_W // LANES, n_tiles),
        in_specs=[cur, spec(slab, after, cb), spec(slab, before, cb), cur, spec(slab, before, cb), cur,
                  cur1, spec(slab, after, 0), cur1, spec(slab, after, 0), cur1, spec(slab, after, 0),
                  bspec, vec2, vec2] + [pl.BlockSpec(memory_space=pl.ANY)] * len(carried),
        out_specs=[cur, cur, cur, bspec, vec, vec],
        out_shape=[shp, shp, shp, jax.ShapeDtypeStruct(bias.shape, F32), vshp, vshp],
        input_output_aliases={n_in + j: j for j in range(len(carried))},
        compiler_params=_params("arbitrary", "arbitrary"),
    )(q, q, k, k, v, v, o, o, lse, lse, do, do, bias, jnp.tile(q_gain, (1, _HP)), jnp.tile(k_gain, (1, _HP)),
      *carried)


def _t5_bucket(dist):
    max_exact = REL_BUCKETS // 2
    scaled = jnp.log(jnp.maximum(dist, 1).astype(F32) / max_exact) / math.log(REL_MAX_DIST / max_exact)
    large = jnp.minimum(max_exact + (scaled * (REL_BUCKETS - max_exact)).astype(jnp.int32), REL_BUCKETS - 1)
    return jnp.where(dist < max_exact, dist, large)


def _dsw_band():
    dist = (jnp.arange(DSW_BLK)[:, None] + DSW_BLK) - jnp.arange(2 * DSW_BLK)[None, :]
    return dist, (dist >= 0) & (dist <= DSW_BLK)


def _dsw_bias(rel_bias):
    dist, band = _dsw_band()
    out = []
    for g, (_, d) in enumerate(DSW_GROUPS):
        hot = jax.nn.one_hot(_t5_bucket(jnp.maximum(dist, 0) * d), REL_BUCKETS, dtype=F32)
        tab = jnp.einsum("qkb,bh->hqk", hot, rel_bias[:, g * DSW_HEADS:(g + 1) * DSW_HEADS],
                         precision=lax.Precision.HIGHEST)
        out.append(jnp.where(band[None], tab, NEG_BIG))
    return jnp.stack(out)


def _dsw_bucket_onehot():
    dist, band = _dsw_band()
    out = []
    for _, d in DSW_GROUPS:
        hot = jax.nn.one_hot(_t5_bucket(jnp.maximum(dist, 0) * d), LANES, dtype=BF16)
        out.append(jnp.where(band[..., None], hot, 0).reshape(-1, LANES))
    return jnp.stack(out)


def _exchange(send, *, gather, name):
    R, C = send.shape[-2:]

    def body(src_ref, dst_ref, send_sems, recv_sems, local_sem):
        x, y, c = lax.axis_index("x"), lax.axis_index("y"), lax.axis_index("c")
        me = 4 * x + 2 * y + c
        mine = pltpu.make_async_copy(src_ref if gather else src_ref.at[me], dst_ref.at[me], local_sem)
        mine.start()
        copies = []
        for rel in range(1, N_DEV):
            px = 1 - x if rel & 4 else x
            py = 1 - y if rel & 2 else y
            pc = 1 - c if rel & 1 else c
            peer = 4 * px + 2 * py + pc
            cp = pltpu.make_async_remote_copy(
                src_ref=src_ref if gather else src_ref.at[peer], dst_ref=dst_ref.at[me],
                send_sem=send_sems.at[rel - 1], recv_sem=recv_sems.at[rel - 1],
                device_id=(px, py, pc), device_id_type=pl.DeviceIdType.MESH)
            cp.start()
            copies.append(cp)
        for cp in copies:
            cp.wait()
        mine.wait()

    return pl.pallas_call(
        body, name=name,
        in_specs=[pl.BlockSpec(memory_space=pl.ANY)], out_specs=pl.BlockSpec(memory_space=pl.ANY),
        out_shape=jax.ShapeDtypeStruct((N_DEV, R, C), send.dtype),
        scratch_shapes=[pltpu.SemaphoreType.DMA((N_DEV - 1,)), pltpu.SemaphoreType.DMA((N_DEV - 1,)),
                        pltpu.SemaphoreType.DMA(())],
    )(send)


def _gather_two_level(send, *, name):
    R, C = send.shape

    def body(src_ref, dst_ref, send_sems, recv_sems, local_sem):
        x, y, c = lax.axis_index("x"), lax.axis_index("y"), lax.axis_index("c")
        me, sibling = (x, y, c), (x, y, 1 - c)
        chips = [(1 - x, y), (x, 1 - y), (1 - x, 1 - y)]

        def slot(px, py, pc):
            return dst_ref.at[4 * px + 2 * py + pc]

        def copy(k, block, to, src=None):
            return pltpu.make_async_remote_copy(
                src_ref=slot(*block) if src is None else src, dst_ref=slot(*block),
                send_sem=send_sems.at[k], recv_sem=recv_sems.at[k],
                device_id=to, device_id_type=pl.DeviceIdType.MESH)

        mine = pltpu.make_async_copy(src_ref, slot(*me), local_sem)
        mine.start()
        first = [copy(0, me, sibling, src=src_ref)]
        first += [copy(1 + j, me, (*chip, c), src=src_ref) for j, chip in enumerate(chips)]
        for cp in first:
            cp.start()
        passed = [copy(4 + j, (*chip, c), sibling) for j, chip in enumerate(chips)]
        for j, chip in enumerate(chips):
            copy(1 + j, (*chip, c), me).wait_recv()
            passed[j].start()
        copy(0, sibling, me).wait_recv()
        for j, chip in enumerate(chips):
            copy(4 + j, (*chip, 1 - c), me).wait_recv()
        for cp in first + passed:
            cp.wait_send()
        mine.wait()

    return pl.pallas_call(
        body, name=name,
        in_specs=[pl.BlockSpec(memory_space=pl.ANY)], out_specs=pl.BlockSpec(memory_space=pl.ANY),
        out_shape=jax.ShapeDtypeStruct((N_DEV, R, C), send.dtype),
        scratch_shapes=[pltpu.SemaphoreType.DMA((N_DEV - 1,)), pltpu.SemaphoreType.DMA((N_DEV - 1,)),
                        pltpu.SemaphoreType.DMA(())],
    )(send)


def _swap_with_sibling(send, *, name):
    def body(src_ref, dst_ref, send_sem, recv_sem):
        x, y, c = lax.axis_index("x"), lax.axis_index("y"), lax.axis_index("c")
        cp = pltpu.make_async_remote_copy(src_ref=src_ref, dst_ref=dst_ref, send_sem=send_sem, recv_sem=recv_sem,
                                          device_id=(x, y, 1 - c), device_id_type=pl.DeviceIdType.MESH)
        cp.start()
        cp.wait()

    return pl.pallas_call(
        body, name=name,
        in_specs=[pl.BlockSpec(memory_space=pl.ANY)], out_specs=pl.BlockSpec(memory_space=pl.ANY),
        out_shape=jax.ShapeDtypeStruct(send.shape, send.dtype),
        scratch_shapes=[pltpu.SemaphoreType.DMA(()), pltpu.SemaphoreType.DMA(())],
    )(send)


def _fill_from_sibling(buf, *, name):
    n_chips = buf.shape[0]

    def body(in_ref, out_ref, send_sems, recv_sems):
        x, y, c = lax.axis_index("x"), lax.axis_index("y"), lax.axis_index("c")
        copies = [pltpu.make_async_remote_copy(
            src_ref=in_ref.at[q, c], dst_ref=out_ref.at[q, c], send_sem=send_sems.at[q], recv_sem=recv_sems.at[q],
            device_id=(x, y, 1 - c), device_id_type=pl.DeviceIdType.MESH) for q in range(n_chips)]
        for cp in copies:
            cp.start()
        for cp in copies:
            cp.wait()

    return pl.pallas_call(
        body, name=name,
        in_specs=[pl.BlockSpec(memory_space=pl.ANY)], out_specs=pl.BlockSpec(memory_space=pl.ANY),
        out_shape=jax.ShapeDtypeStruct(buf.shape, buf.dtype), input_output_aliases={0: 0},
        scratch_shapes=[pltpu.SemaphoreType.DMA((n_chips,)), pltpu.SemaphoreType.DMA((n_chips,))],
    )(buf)


def _exchange_chips(send, *, name):
    n_chips, R, C = send.shape

    def body(src_ref, dst_ref, send_sems, recv_sems, local_sem):
        mine, copies = _chip_copies(src_ref, dst_ref, send_sems, recv_sems, local_sem)
        mine.start()
        for cp in copies:
            cp.start()
        for cp in copies:
            cp.wait()
        mine.wait()

    return pl.pallas_call(
        body, name=name,
        in_specs=[pl.BlockSpec(memory_space=pl.ANY)], out_specs=pl.BlockSpec(memory_space=pl.ANY),
        out_shape=jax.ShapeDtypeStruct(send.shape, send.dtype),
        scratch_shapes=[pltpu.SemaphoreType.DMA((n_chips - 1,)), pltpu.SemaphoreType.DMA((n_chips - 1,)),
                        pltpu.SemaphoreType.DMA(())],
    )(send)


def _add_pair(a, b, *, name):
    n, R, C = a.shape
    tr = _tile(R, 1024)

    def body(a_ref, b_ref, o_ref):
        o_ref[...] = (a_ref[...].astype(F32) + b_ref[...].astype(F32)).astype(o_ref.dtype)

    blk = pl.BlockSpec((None, tr, C), lambda k, i: (k, i, 0))
    return pl.pallas_call(
        body, name=name, grid=(n, R // tr), in_specs=[blk, blk], out_specs=blk,
        out_shape=jax.ShapeDtypeStruct(a.shape, a.dtype), compiler_params=_params("parallel", "parallel"),
    )(a, b)


_BIG = ("w_ffn_in", "w_ffn_out", "gdn_w_in", "gdn_conv", "gdn_w_out", "dsw_w_in", "dsw_w_out")
_LATE = ("gdn_w_in", "gdn_conv", "gdn_w_out")
_EARLY = tuple(n for n in _BIG if n not in _LATE)
_SHARD_AXIS = {"w_ffn_in": 2, "w_ffn_out": 1, "gdn_w_in": 2, "gdn_conv": 2, "gdn_w_out": 1, "dsw_w_in": 2,
               "dsw_w_out": 2}
_SMALL = ("b_ada", "norm_mix", "norm_ffn", "gdn_a_log", "gdn_dt_bias", "gdn_out_norm", "dsw_q_norm",
          "dsw_k_norm", "rel_bias")
_ROW_ALIGN = 16
_BIG_ALIGN = 1024


def _ceil_to(n, m):
    return -(-n // m) * m


def _seg_rows(shape):
    return _ceil_to(_ceil_to(int(np.prod(shape)), LANES) // LANES, _ROW_ALIGN)


def _pack(arrs, total_align):
    lead = arrs[0][1]
    segs = []
    for a, nlead in arrs:
        assert nlead == lead
        bshape = a.shape[:nlead]
        n = int(np.prod(a.shape[nlead:]))
        rows = _seg_rows(a.shape[nlead:])
        flat = a.reshape(bshape + (n,))
        flat = jnp.pad(flat, [(0, 0)] * nlead + [(0, rows * LANES - n)])
        segs.append(flat.reshape(bshape + (rows, LANES)))
    buf = jnp.concatenate(segs, axis=lead)
    total = _ceil_to(buf.shape[lead], total_align)
    return jnp.pad(buf, [(0, 0)] * lead + [(0, total - buf.shape[lead]), (0, 0)])


def _unpack(buf, shapes, nlead):
    out, off = [], 0
    for shp in shapes:
        n, rows = int(np.prod(shp)), _seg_rows(shp)
        seg = lax.slice_in_dim(buf, off, off + rows, axis=nlead)
        seg = seg.reshape(buf.shape[:nlead] + (rows * LANES,))[..., :n]
        out.append(seg.reshape(buf.shape[:nlead] + tuple(shp)))
        off += rows
    return out


def _to_natural(g, axis):
    n, L, r, c = g.shape
    if axis == 2:
        return jnp.transpose(g, (1, 2, 0, 3)).reshape(L, r, n * c)
    return jnp.transpose(g, (1, 0, 2, 3)).reshape(L, n * r, c)


def _to_blocked(w, axis):
    L, R, C = w.shape
    if axis == 2:
        return jnp.transpose(w.reshape(L, R, N_DEV, C // N_DEV), (2, 0, 1, 3))
    return jnp.transpose(w.reshape(L, N_DEV, R // N_DEV, C), (1, 0, 2, 3))


def _hm(a):
    lead = a.shape[:-1]
    return jnp.swapaxes(a.reshape(lead + (3, GDN_HEADS, GDN_DK)), -3, -2).reshape(lead + (3 * GDN_HEADS * GDN_DK,))


def _un_hm(a):
    lead = a.shape[:-1]
    return jnp.swapaxes(a.reshape(lead + (GDN_HEADS, 3, GDN_DK)), -3, -2).reshape(lead + (3 * GDN_HEADS * GDN_DK,))


_TILES = (1536, 1408, 1024, 768, 512, 384, 256, 128, 64, 32, 16, 8)


def _tile(n, cap):
    for t in _TILES:
        if t <= cap and n % t == 0:
            return t
    return n


def _mm_auto(a, b, mode, name, **kw):
    if mode == "tn":
        (K, M), N = a.shape, b.shape[1]
        tm, tn, tk = _tile(M, 1408), _tile(N, 1408), _tile(K, 1024)
    else:
        M, K = a.shape
        N = b.shape[1] if mode == "nn" else b.shape[0]
        tm, tn, tk = _tile(M, 512), _tile(N, 1536), _tile(K, 1408)
    return _mm(a, b, mode=mode, name=name, tm=tm, tn=tn, tk=tk, **kw)


def _row(v):
    return v.reshape(1, -1)


def _ffn_in_act(h, w_in, *, name):
    S, D = h.shape
    F = w_in.shape[1] // 2
    tm, tn = _tile(S, 512), _tile(F, 1408)
    nj = F // tn

    def body(h_ref, wg_ref, wu_ref, g_ref, u_ref, a_ref):
        hv = h_ref[...]
        gate = jnp.dot(hv, wg_ref[...], preferred_element_type=F32)
        up = jnp.dot(hv, wu_ref[...], preferred_element_type=F32)
        g_ref[...] = gate.astype(BF16)
        u_ref[...] = up.astype(BF16)
        a_ref[...] = (_silu(gate) * up).astype(BF16)

    out = pl.BlockSpec((tm, tn), lambda i, j: (i, j))
    shp = jax.ShapeDtypeStruct((S, F), BF16)
    return pl.pallas_call(
        body, name=name, grid=(S // tm, nj),
        in_specs=[pl.BlockSpec((tm, D), lambda i, j: (i, 0)), pl.BlockSpec((D, tn), lambda i, j: (0, j)),
                  pl.BlockSpec((D, tn), lambda i, j: (0, j + nj))],
        out_specs=[out, out, out], out_shape=[shp, shp, shp],
        compiler_params=_params("parallel", "parallel"),
    )(h, w_in, w_in)


def _ffn_out_dx_act(dy, w_out, gate_vec, pg, pu, *, name):
    S, D = dy.shape
    F = w_out.shape[0]
    tm, tn = _tile(S, 512), _tile(F, 1408)

    def body(dy_ref, w_ref, gv_ref, pg_ref, pu_ref, dg_ref, du_ref):
        dyg = (dy_ref[...] * gv_ref[...]).astype(BF16)
        da = lax.dot_general(dyg, w_ref[...], _DOT_DIMS["nt"], preferred_element_type=F32)
        gate = pg_ref[...].astype(F32)
        up = pu_ref[...].astype(F32)
        sg = _sigmoid(gate)
        dg_ref[...] = (da * up * (sg * (1.0 + gate * (1.0 - sg)))).astype(BF16)
        du_ref[...] = (da * (gate * sg)).astype(BF16)

    blk = pl.BlockSpec((tm, tn), lambda i, j: (i, j))
    shp = jax.ShapeDtypeStruct((S, F), BF16)
    return pl.pallas_call(
        body, name=name, grid=(S // tm, F // tn),
        in_specs=[pl.BlockSpec((tm, D), lambda i, j: (i, 0)), pl.BlockSpec((tn, D), lambda i, j: (j, 0)),
                  pl.BlockSpec((1, D), lambda i, j: (0, 0)), blk, blk],
        out_specs=[blk, blk], out_shape=[shp, shp],
        compiler_params=_params("parallel", "parallel"),
    )(dy, w_out, gate_vec, pg, pu)


def _ffn_fwd(x, mod, gain, w_in, w_out, tag):
    sh, sc, gate = mod
    h = _norm_mod_fwd(x, gain, sc, sh, name=f"ffn_norm_{tag}")
    pg, pu, a = _ffn_in_act(h, w_in, name=f"ffn_in_{tag}")
    y = _mm_auto(a, w_out, "nn", f"ffn_out_{tag}", out_scale=gate, resid=x)
    return y, (x, h, pg, pu, a)


def _ffn_bwd(dy, saved, mod, gain, w_in, w_out, tag):
    sh, sc, gate = mod
    x, h, pg, pu, a = saved
    F = pg.shape[1]
    gmat = _mm_auto(a, dy, "tn", f"ffn_out_g_{tag}")
    dw_out, dgate = _wout_grad(gmat, w_out, gate, name=f"ffn_out_dw_{tag}")
    dpg, dpu = _ffn_out_dx_act(dy, w_out, gate, pg, pu, name=f"ffn_out_dx_{tag}")
    dw_in = jnp.concatenate([_mm_auto(h, dpg, "tn", f"ffn_in_dw_gate_{tag}", out_dtype=BF16),
                             _mm_auto(h, dpu, "tn", f"ffn_in_dw_up_{tag}", out_dtype=BF16)], axis=1)
    dh = _mm_auto(dpg, w_in, "nt", f"ffn_in_dx_gate_{tag}")
    dh = _mm_auto(dpu, w_in, "nt", f"ffn_in_dx_up_{tag}", b_k_off=F, resid=dh)
    dx, dsh, dsc, dgain = _norm_mod_bwd(dh, x, dy, gain, sc, name=f"ffn_norm_bwd_{tag}")
    return dx, dict(w_in=dw_in, w_out=dw_out, gain=dgain, mod=(dsh, dsc, dgate))


def _gdn_fwd(x, mod, gain, W, riding=None):
    sh, sc, gate = mod
    S = x.shape[0]
    h = _norm_mod_fwd(x, gain, sc, sh, name="gdn_norm")
    pq = _mm_auto(h, W["gdn_qkv"], "nn", "gdn_in_qkv", out_dtype=BF16)
    z = _mm_auto(h, W["gdn_z"], "nn", "gdn_in_z", out_dtype=BF16)
    ab = _mm_auto(h, W["gdn_ab"], "nn", "gdn_in_ab")
    qkvn = _gdn_prep_fwd(pq, W["gdn_conv"], name="gdn_prep")
    ab4 = jnp.transpose(ab[:, :2 * GDN_HEADS]).reshape(2 * GDN_HEADS, S // GDN_CHUNK, 1, GDN_CHUNK)
    o, states, tinvs, *rode = _gdn_chunk_fwd(qkvn, ab4, W["gdn_a_log"], W["gdn_dt_bias"], name="gdn_chunk",
                                             riding=riding)
    o2 = _gdn_outnorm_fwd(o, z, W["gdn_out_norm"], name="gdn_outnorm")
    y = _mm_auto(o2, W["gdn_out"], "nn", "gdn_out", out_scale=gate, resid=x)
    return y, (x, h, pq, z, qkvn, ab4, o, states, tinvs, o2), (rode[0] if rode else None)


def _gdn_bwd(dy, saved, mod, gain, W, riding=None):
    sh, sc, gate = mod
    x, h, pq, z, qkvn, ab4, o, states, tinvs, o2 = saved
    S = x.shape[0]
    gmat = _mm_auto(o2, dy, "tn", "gdn_out_g")
    dw_out, dgate = _wout_grad(gmat, W["gdn_out"], gate, name="gdn_out_dw")
    do2 = _mm_auto(dy, W["gdn_out"], "nt", "gdn_out_dx", a_scale=gate)
    do, dz, dout_norm = _gdn_outnorm_bwd(do2, o, z, W["gdn_out_norm"], name="gdn_outnorm_bwd")
    dqkvn, dab4, da_log, ddt_bias, *rode = _gdn_chunk_bwd(
        qkvn, ab4, W["gdn_a_log"], W["gdn_dt_bias"], states, tinvs, do, name="gdn_chunk_bwd", riding=riding)
    dc, dconv8 = _gdn_prep_bwd_pre(dqkvn, pq, W["gdn_conv"], name="gdn_prep_bwd")
    dpq = _gdn_conv_bwd_x(dc, W["gdn_conv"], name="gdn_conv_bwd")
    dab = jnp.transpose(dab4.reshape(2 * GDN_HEADS, S))
    dab = jnp.pad(dab, ((0, 0), (0, LANES - 2 * GDN_HEADS))).astype(BF16)
    dw_qkv = _mm_auto(h, dpq, "tn", "gdn_in_qkv_dw", out_dtype=BF16)
    dw_z = _mm_auto(h, dz, "tn", "gdn_in_z_dw", out_dtype=BF16)
    dw_ab = _mm_auto(h, dab, "tn", "gdn_in_ab_dw", out_dtype=BF16)
    dh = _mm_auto(dpq, W["gdn_qkv"], "nt", "gdn_in_qkv_dx")
    dh = _mm_auto(dz, W["gdn_z"], "nt", "gdn_in_z_dx", resid=dh)
    dh = _mm_auto(dab, W["gdn_ab"], "nt", "gdn_in_ab_dx", resid=dh)
    dx, dsh, dsc, dgain = _norm_mod_bwd(dh, x, dy, gain, sc, name="gdn_norm_bwd")
    dw_in = jnp.concatenate([_un_hm(dw_qkv), dw_z, dw_ab[:, :2 * GDN_HEADS]], axis=1)
    return dx, dict(gdn_w_in=dw_in, gdn_conv=_un_hm(dconv8[:GDN_CONV]), gdn_w_out=dw_out, gdn_out_norm=dout_norm,
                    gdn_a_log=da_log.reshape(1, GDN_HEADS), gdn_dt_bias=ddt_bias.reshape(1, GDN_HEADS),
                    gain=dgain, mod=(dsh, dsc, dgate)), (rode[0] if rode else None)


def _dsw_fwd(x, mod, gain, W):
    sh, sc, gate = mod
    h = _norm_mod_fwd(x, gain, sc, sh, name="dsw_norm")
    q, k, v = (_mm_auto(h, W[n], "nn", f"dsw_in_{n[-1]}") for n in ("dsw_q", "dsw_k", "dsw_v"))
    outs = None
    for g in range(len(DSW_GROUPS)):
        outs = _dsw_attn_fwd(q, k, v, W["dsw_bias"][g], W["dsw_q_norm"], W["dsw_k_norm"], outs, g=g,
                             name=f"dsw_attn_{g}")
    o, lse = _dsw_merge(*outs, name="dsw_merge")
    y = _mm_auto(o, W["dsw_out"], "nn", "dsw_out", out_scale=gate, resid=x)
    return y, (x, h, q, k, v, o, lse)


def _dsw_bwd(dy, saved, mod, gain, W):
    sh, sc, gate = mod
    x, h, q, k, v, o, lse = saved
    gmat = _mm_auto(o, dy, "tn", "dsw_out_g")
    dw_out, dgate = _wout_grad(gmat, W["dsw_out"], gate, name="dsw_out_dw")
    do = _mm_auto(dy, W["dsw_out"], "nt", "dsw_out_dx", a_scale=gate)
    G = len(DSW_GROUPS)
    dqkv, dbias, dq_norm, dk_norm = None, [], 0.0, 0.0
    for g in range(G):
        *dqkv, db, dqg, dkg = _dsw_attn_bwd(q, k, v, o, lse, do, W["dsw_bias"][g], W["dsw_q_norm"],
                                            W["dsw_k_norm"], dqkv, g=g, name=f"dsw_attn_bwd_{g}")
        dbias.append(db)
        dq_norm, dk_norm = dq_norm + dqg, dk_norm + dkg
    dws, dh = [], None
    for n, d in zip(("dsw_q", "dsw_k", "dsw_v"), dqkv):
        dws.append(_mm_auto(h, d, "tn", f"dsw_in_{n[-1]}_dw", out_dtype=BF16))
        dh = _mm_auto(d, W[n], "nt", f"dsw_in_{n[-1]}_dx", **({} if dh is None else {"resid": dh}))
    dx, dsh, dsc, dgain = _norm_mod_bwd(dh, x, dy, gain, sc, name="dsw_norm_bwd")
    hot = _dsw_bucket_onehot()
    drel = [_mm_auto(dbias[g].reshape(DSW_HEADS, -1), hot[g], "nn", f"dsw_rel_bias_{g}")[:, :REL_BUCKETS]
            for g in range(G)]
    return dx, dict(dsw_w_in=jnp.concatenate(dws, axis=1), dsw_w_out=dw_out, dsw_q_norm=dq_norm,
                    dsw_k_norm=dk_norm, rel_bias=jnp.transpose(jnp.concatenate(drel, axis=0)),
                    gain=dgain, mod=(dsh, dsc, dgate))


def _local_step(x, target, mod, W, late_weights=None, early_pairs=None):
    mods = [[_row(mod[l, i]) for i in range(6)] for l in range(2)]
    nmix = [_row(W["norm_mix"][l]) for l in range(2)]
    nffn = [_row(W["norm_ffn"][l]) for l in range(2)]
    x1, s_gdn, arrived = _gdn_fwd(x, mods[0][:3], nmix[0], W, None if late_weights is None else late_weights[0])
    if late_weights is not None:
        W = {**W, **late_weights[1](arrived)}
    x2, s_f0 = _ffn_fwd(x1, mods[0][3:], nffn[0], W["w_ffn_in"][0], W["w_ffn_out"][0], "0")
    x3, s_dsw = _dsw_fwd(x2, mods[1][:3], nmix[1], W)
    x4, s_f1 = _ffn_fwd(x3, mods[1][3:], nffn[1], W["w_ffn_in"][1], W["w_ffn_out"][1], "1")
    dx4, sse = _loss_head(x4, target, name="loss_head")
    dx3, g_f1 = _ffn_bwd(dx4, s_f1, mods[1][3:], nffn[1], W["w_ffn_in"][1], W["w_ffn_out"][1], "1")
    dx2, g_dsw = _dsw_bwd(dx3, s_dsw, mods[1][:3], nmix[1], W)
    dx1, g_f0 = _ffn_bwd(dx2, s_f0, mods[0][3:], nffn[0], W["w_ffn_in"][0], W["w_ffn_out"][0], "0")
    grads = dict(
        w_ffn_in=jnp.stack([g_f0["w_in"], g_f1["w_in"]]), w_ffn_out=jnp.stack([g_f0["w_out"], g_f1["w_out"]]),
        dsw_w_in=g_dsw["dsw_w_in"][None], dsw_w_out=g_dsw["dsw_w_out"][None])
    riding = None if early_pairs is None else early_pairs(grads)
    dx0, g_gdn, rode = _gdn_bwd(dx1, s_gdn, mods[0][:3], nmix[0], W, riding)
    dmod = jnp.stack([jnp.concatenate(list(g_gdn["mod"]) + list(g_f0["mod"]), axis=0),
                      jnp.concatenate(list(g_dsw["mod"]) + list(g_f1["mod"]), axis=0)])
    grads.update(
        norm_mix=jnp.concatenate([g_gdn["gain"], g_dsw["gain"]], axis=0),
        norm_ffn=jnp.concatenate([g_f0["gain"], g_f1["gain"]], axis=0),
        gdn_w_in=g_gdn["gdn_w_in"][None], gdn_conv=g_gdn["gdn_conv"][None], gdn_w_out=g_gdn["gdn_w_out"][None],
        gdn_out_norm=g_gdn["gdn_out_norm"], gdn_a_log=g_gdn["gdn_a_log"], gdn_dt_bias=g_gdn["gdn_dt_bias"],
        dsw_q_norm=g_dsw["dsw_q_norm"], dsw_k_norm=g_dsw["dsw_k_norm"], rel_bias=g_dsw["rel_bias"])
    return sse, dx0, grads, dmod, rode


def _prepare_first(full, small):
    gw = full["gdn_w_in"][0]
    hk3 = 3 * GDN_HEADS * GDN_DK
    return dict(
        gdn_qkv=_hm(gw[:, :hk3]), gdn_z=gw[:, hk3:hk3 + GDN_HEADS * GDN_DK],
        gdn_ab=jnp.pad(gw[:, hk3 + GDN_HEADS * GDN_DK:], ((0, 0), (0, LANES - 2 * GDN_HEADS))),
        gdn_conv=_hm(full["gdn_conv"][0]), gdn_out=full["gdn_w_out"][0],
        norm_mix=small["norm_mix"], norm_ffn=small["norm_ffn"],
        gdn_a_log=small["gdn_a_log"].reshape(GDN_HEADS, 1, 1), gdn_dt_bias=small["gdn_dt_bias"].reshape(GDN_HEADS, 1, 1),
        gdn_out_norm=small["gdn_out_norm"], dsw_q_norm=small["dsw_q_norm"], dsw_k_norm=small["dsw_k_norm"],
        dsw_bias=_dsw_bias(small["rel_bias"]))


def _prepare_rest(full):
    di = full["dsw_w_in"][0]
    dq = di.shape[1] // 3
    return dict(w_ffn_in=full["w_ffn_in"], w_ffn_out=full["w_ffn_out"],
                dsw_q=di[:, :dq], dsw_k=di[:, dq:2 * dq], dsw_v=di[:, 2 * dq:], dsw_out=full["dsw_w_out"][0])


def _prepare_weights(full, small):
    return {**_prepare_first(full, small), **_prepare_rest(full)}


_W_NAMES = ("w_ada", "b_ada", "norm_mix", "norm_ffn", "w_ffn_in", "w_ffn_out", "gdn_w_in", "gdn_conv",
            "gdn_a_log", "gdn_dt_bias", "gdn_out_norm", "gdn_w_out", "dsw_w_in", "dsw_q_norm", "dsw_k_norm",
            "dsw_w_out", "rel_bias")
_PAD_BATCH = 16


def _pad_rows(a, rows):
    return jnp.pad(a, ((0, rows - a.shape[0]), (0, 0)))


def kernel(x, c, w_ada, b_ada, norm_mix, norm_ffn, w_ffn_in, w_ffn_out, gdn_w_in, gdn_conv, gdn_a_log, gdn_dt_bias, gdn_out_norm, gdn_w_out, dsw_w_in, dsw_q_norm, dsw_k_norm, dsw_w_out, rel_bias, loss_target, m_w_ada, m_b_ada, m_norm_mix, m_norm_ffn, m_w_ffn_in, m_w_ffn_out, m_gdn_w_in, m_gdn_conv, m_gdn_a_log, m_gdn_dt_bias, m_gdn_out_norm, m_gdn_w_out, m_dsw_w_in, m_dsw_q_norm, m_dsw_k_norm, m_dsw_w_out, m_rel_bias, v_w_ada, v_b_ada, v_norm_mix, v_norm_ffn, v_w_ffn_in, v_w_ffn_out, v_gdn_w_in, v_gdn_conv, v_gdn_a_log, v_gdn_dt_bias, v_gdn_out_norm, v_gdn_w_out, v_dsw_w_in, v_dsw_q_norm, v_dsw_k_norm, v_dsw_w_out, v_rel_bias):
    w = dict(zip(_W_NAMES, (w_ada, b_ada, norm_mix, norm_ffn, w_ffn_in, w_ffn_out, gdn_w_in, gdn_conv, gdn_a_log,
                            gdn_dt_bias, gdn_out_norm, gdn_w_out, dsw_w_in, dsw_q_norm, dsw_k_norm, dsw_w_out,
                            rel_bias)))
    m = dict(zip(_W_NAMES, (m_w_ada, m_b_ada, m_norm_mix, m_norm_ffn, m_w_ffn_in, m_w_ffn_out, m_gdn_w_in,
                            m_gdn_conv, m_gdn_a_log, m_gdn_dt_bias, m_gdn_out_norm, m_gdn_w_out, m_dsw_w_in,
                            m_dsw_q_norm, m_dsw_k_norm, m_dsw_w_out, m_rel_bias)))
    v = dict(zip(_W_NAMES, (v_w_ada, v_b_ada, v_norm_mix, v_norm_ffn, v_w_ffn_in, v_w_ffn_out, v_gdn_w_in,
                            v_gdn_conv, v_gdn_a_log, v_gdn_dt_bias, v_gdn_out_norm, v_gdn_w_out, v_dsw_w_in,
                            v_dsw_q_norm, v_dsw_k_norm, v_dsw_w_out, v_rel_bias)))
    D = x.shape[-1]
    n_layers, _, ada_cols = w_ada.shape

    c_all = _exchange(c.reshape(D // LANES, LANES), gather=True, name="gather_cond").reshape(N_DEV, D)
    c_pad = _pad_rows(c_all, _PAD_BATCH)
    proj = [_mm(c_pad, w_ada[l], mode="nn", name=f"ada_proj_{l}", tm=_PAD_BATCH, tn=ada_cols, tk=D, a_silu=True)
            for l in range(n_layers)]
    mod_send = _pack([(jnp.stack([p[:N_DEV] for p in proj], axis=1), 1)], _ROW_ALIGN)
    mod_recv = _exchange(mod_send, gather=False, name="scatter_mod")
    mod = _unpack(mod_recv, [(n_layers, ada_cols)], 1)[0]
    mod = jnp.transpose(mod, (1, 0, 2)).reshape(n_layers, N_DEV * ada_cols) + b_ada
    mod = mod.reshape(n_layers, 6, D)

    conv_hi = gdn_conv.astype(BF16)
    conv_lo = (gdn_conv - conv_hi.astype(F32)).astype(BF16)
    first_send = _pack([(conv_hi if n == "gdn_conv" else w[n].astype(BF16), 0) for n in _LATE] + [(conv_lo, 0)],
                       _ROW_ALIGN)
    parts = _unpack(_gather_two_level(first_send, name="gather_weights_first"),
                    [w[n].shape for n in _LATE] + [gdn_conv.shape], 1)
    full = {n: _to_natural(parts[i], _SHARD_AXIS[n]) for i, n in enumerate(_LATE)}
    full["gdn_conv"] = full["gdn_conv"].astype(F32) + _to_natural(parts[-1], _SHARD_AXIS["gdn_conv"]).astype(F32)
    W = _prepare_first(full, {n: w[n] for n in _SMALL})
    rest_send = _pack([(w[n].astype(BF16), 0) for n in _EARLY], _ROW_ALIGN)

    def rest_weights(arrived):
        w_all = _fill_from_sibling(arrived, name="swap_weights").reshape((N_DEV,) + rest_send.shape)
        blocks = _unpack(w_all, [w[n].shape for n in _EARLY], 1)
        return _prepare_rest({n: _to_natural(blocks[i], _SHARD_AXIS[n]) for i, n in enumerate(_EARLY)})

    my_c = lax.axis_index("c")

    def pair_sums(g, names, tag):
        send = _pack([(_to_blocked(g[n].astype(BF16), _SHARD_AXIS[n]), 1) for n in names], _BIG_ALIGN)
        by_core = send.reshape((N_DEV // 2, 2) + send.shape[1:])
        keep = lax.dynamic_index_in_dim(by_core, my_c, axis=1, keepdims=False)
        give = lax.dynamic_index_in_dim(by_core, 1 - my_c, axis=1, keepdims=False)
        return _add_pair(keep, _swap_with_sibling(give, name=f"swap_grads_{tag}"), name=f"add_sibling_grads_{tag}")

    sse, grad_x, grads, dmod, early_recv = _local_step(
        x[0], loss_target[0], mod, W, late_weights=(rest_send, rest_weights),
        early_pairs=lambda g: pair_sums(g, _EARLY, "early"))
    loss = lax.psum(0.5 * sse[0, 0] / D, ("x", "y", "c"))
    grads["b_ada"] = dmod.reshape(n_layers, 6 * D)
    late_recv = _exchange_chips(pair_sums(grads, _LATE, "late"), name="scatter_grads_late")
    g_parts = dict(zip(_EARLY, _unpack(early_recv, [w[n].shape for n in _EARLY], 1)))
    g_parts.update(zip(_LATE, _unpack(late_recv, [w[n].shape for n in _LATE], 1)))

    dmod_send = _pack([(jnp.transpose(dmod.reshape(n_layers, N_DEV, ada_cols), (1, 0, 2)), 1)], _ROW_ALIGN)
    small_send = _pack([(grads[n].reshape(w[n].shape), 0) for n in _SMALL], _ROW_ALIGN)
    s_recv = _exchange(jnp.concatenate(
        [dmod_send, jnp.broadcast_to(small_send[None], (N_DEV,) + small_send.shape)], axis=1),
        gather=False, name="scatter_small")
    dmod_rows = dmod_send.shape[1]

    out = {}
    kinds = ("grad", "delta", "new_m", "new_v")
    for n in _BIG:
        g4 = g_parts[n]
        rows2d = lambda a: a.reshape((-1, w[n].shape[-1]))
        res = _adamw(rows2d(w[n]), g4.reshape((g4.shape[0], -1, w[n].shape[-1])), rows2d(m[n]), rows2d(v[n]),
                     name=f"adamw_{n}")
        for kind, buf in zip(kinds, res):
            out[kind, n] = buf.reshape(w[n].shape)

    dmod_all = _unpack(lax.slice_in_dim(s_recv, 0, dmod_rows, axis=1), [(n_layers, ada_cols)], 1)[0]
    g_ada = jnp.stack([_mm(c_pad, _pad_rows(dmod_all[:, l], _PAD_BATCH), mode="tn", name=f"ada_dw_{l}",
                           tm=D, tn=ada_cols, tk=_PAD_BATCH, a_silu=True) for l in range(n_layers)])
    flat = lambda a: a.reshape(n_layers * D, ada_cols)
    res = _adamw(flat(w_ada), flat(g_ada)[None], flat(m_w_ada), flat(v_w_ada), name="adamw_ada")
    for kind, buf in zip(("grad", "delta", "new_m", "new_v"), res):
        out[kind, "w_ada"] = buf.reshape(w_ada.shape)

    small_parts = lax.slice_in_dim(s_recv, dmod_rows, s_recv.shape[1], axis=1)
    packed = [_pack([(t[n], 0) for n in _SMALL], _ROW_ALIGN) for t in (w, m, v)]
    res = _adamw(packed[0], small_parts, packed[1], packed[2], name="adamw_replicated")
    for kind, buf in zip(("grad", "delta", "new_m", "new_v"), res):
        for n, a in zip(_SMALL, _unpack(buf, [w[n].shape for n in _SMALL], 0)):
            out[kind, n] = a

    return (loss, grad_x[None]) + tuple(out[kind, n] for kind in ("grad", "delta", "new_m", "new_v")
                                        for n in _W_NAMES)
```

```python
import functools
import math

import numpy as np
import jax
import jax.numpy as jnp
from jax import lax
from jax.experimental import pallas as pl
from jax.experimental.pallas import tpu as pltpu

F32 = jnp.float32
BF16 = jnp.bfloat16

N_DEV = 8
RMS_EPS = 1e-6
LANES = 128
V7X_VMEM_LIMIT = 48 * 1024 * 1024

GDN_HEADS = 8
GDN_DK = 128
GDN_CHUNK = 64
GDN_CONV = 4
DSW_GROUPS = ((128, 1), (512, 4), (2048, 16))
DSW_HEADS = 8
DSW_DH = 64
DSW_BLK = 128
REL_BUCKETS = 32
REL_MAX_DIST = 2048

ADAM_LR = 0.001
ADAM_B1 = 0.9
ADAM_B2 = 0.999
ADAM_EPS = 1e-08
ADAM_WD = 0.01
ADAM_STEP = 10

NEG_BIG = -1e30


def _params(*sem):
    return pltpu.CompilerParams(dimension_semantics=sem, vmem_limit_bytes=V7X_VMEM_LIMIT)


def _sigmoid(x):
    return 1.0 / (1.0 + jnp.exp(-x))


def _silu(x):
    return x * _sigmoid(x)


_DOT_DIMS = {
    "nn": (((1,), (0,)), ((), ())),
    "nt": (((1,), (1,)), ((), ())),
    "tn": (((0,), (0,)), ((), ())),
}


def _mm(a, b, *, mode, name, tm, tn, tk, out_dtype=F32, a_scale=None, out_scale=None, resid=None, a_silu=False):
    if mode == "nn":
        (M, K), N = a.shape, b.shape[1]
    elif mode == "nt":
        (M, K), N = a.shape, b.shape[0]
    else:
        (K, M), N = a.shape, b.shape[1]
    tm, tn, tk = min(tm, M), min(tn, N), min(tk, K)
    assert M % tm == 0 and N % tn == 0 and K % tk == 0, (name, M, N, K, tm, tn, tk)
    nk = K // tk

    def body(*refs):
        refs = list(refs)
        a_ref, b_ref = refs.pop(0), refs.pop(0)
        as_ref = refs.pop(0) if a_scale is not None else None
        os_ref = refs.pop(0) if out_scale is not None else None
        r_ref = refs.pop(0) if resid is not None else None
        o_ref = refs.pop(0)
        acc_ref = refs.pop(0) if nk > 1 else None

        av = a_ref[...]
        if a_silu:
            av = _silu(av.astype(F32))
        if as_ref is not None:
            av = av.astype(F32) * as_ref[...]
        part = lax.dot_general(av.astype(BF16), b_ref[...].astype(BF16), _DOT_DIMS[mode],
                               preferred_element_type=F32)

        def finish(r):
            if os_ref is not None:
                r = r * os_ref[...]
            if r_ref is not None:
                r = r + r_ref[...].astype(F32)
            o_ref[...] = r.astype(out_dtype)

        if nk == 1:
            finish(part)
        else:
            k = pl.program_id(2)

            @pl.when(k == 0)
            def _():
                acc_ref[...] = part

            @pl.when(k > 0)
            def _():
                acc_ref[...] += part

            @pl.when(k == nk - 1)
            def _():
                finish(acc_ref[...])

    if mode == "nn":
        a_spec = pl.BlockSpec((tm, tk), lambda i, j, k: (i, k))
        b_spec = pl.BlockSpec((tk, tn), lambda i, j, k: (k, j))
        as_spec = pl.BlockSpec((1, tk), lambda i, j, k: (0, k))
    elif mode == "nt":
        a_spec = pl.BlockSpec((tm, tk), lambda i, j, k: (i, k))
        b_spec = pl.BlockSpec((tn, tk), lambda i, j, k: (j, k))
        as_spec = pl.BlockSpec((1, tk), lambda i, j, k: (0, k))
    else:
        a_spec = pl.BlockSpec((tk, tm), lambda i, j, k: (k, i))
        b_spec = pl.BlockSpec((tk, tn), lambda i, j, k: (k, j))
        as_spec = None
    in_specs, args = [a_spec, b_spec], [a, b]
    if a_scale is not None:
        in_specs.append(as_spec)
        args.append(a_scale)
    if out_scale is not None:
        in_specs.append(pl.BlockSpec((1, tn), lambda i, j, k: (0, j)))
        args.append(out_scale)
    if resid is not None:
        in_specs.append(pl.BlockSpec((tm, tn), lambda i, j, k: (i, j)))
        args.append(resid)
    return pl.pallas_call(
        body, name=name, grid=(M // tm, N // tn, nk),
        in_specs=in_specs, out_specs=pl.BlockSpec((tm, tn), lambda i, j, k: (i, j)),
        out_shape=jax.ShapeDtypeStruct((M, N), out_dtype),
        scratch_shapes=[pltpu.VMEM((tm, tn), F32)] if nk > 1 else [],
        compiler_params=_params("parallel", "parallel", "arbitrary"),
    )(*args)


def _mm_sum_nt(pairs, *, name, tm=512, tn=1024):
    M, N = pairs[0][0].shape[0], pairs[0][1].shape[0]
    tm, tn = _tile(M, tm), _tile(N, tn)
    spans, start = [], 0
    for a, b, tk, off in pairs:
        K = a.shape[1]
        assert a.shape[0] == M and b.shape[0] == N and K % tk == 0 and off % tk == 0, name
        spans.append((start, K // tk, tk, off // tk))
        start += K // tk
    total = start

    def body(*refs):
        o_ref, acc_ref = refs[-2:]
        k = pl.program_id(2)

        @pl.when(k == 0)
        def _():
            acc_ref[...] = jnp.zeros_like(acc_ref)

        for p, (s0, nk, _, _) in enumerate(spans):
            a_ref, b_ref = refs[2 * p], refs[2 * p + 1]

            @pl.when((k >= s0) & (k < s0 + nk))
            def _():
                acc_ref[...] += lax.dot_general(a_ref[...].astype(BF16), b_ref[...].astype(BF16), _DOT_DIMS["nt"],
                                                preferred_element_type=F32)

        @pl.when(k == total - 1)
        def _():
            o_ref[...] = acc_ref[...]

    def spec(rows, tk, s0, nk, koff, axis):
        def index(i, j, k):
            return ((i, j)[axis], jnp.clip(k - s0, 0, nk - 1) + koff)
        return pl.BlockSpec((rows, tk), index)

    in_specs, args = [], []
    for (a, b, _, _), (s0, nk, tk, koff) in zip(pairs, spans):
        in_specs += [spec(tm, tk, s0, nk, 0, 0), spec(tn, tk, s0, nk, koff, 1)]
        args += [a, b]
    return pl.pallas_call(
        body, name=name, grid=(M // tm, N // tn, total), in_specs=in_specs,
        out_specs=pl.BlockSpec((tm, tn), lambda i, j, k: (i, j)),
        out_shape=jax.ShapeDtypeStruct((M, N), F32), scratch_shapes=[pltpu.VMEM((tm, tn), F32)],
        compiler_params=_params("parallel", "parallel", "arbitrary"),
    )(*args)


def _norm_mod_fwd(x, gain, sc, sh, *, name):
    S, D = x.shape
    tr = min(512, S)

    def body(x_ref, g_ref, sc_ref, sh_ref, h_ref):
        xv = x_ref[...]
        r = lax.rsqrt(jnp.mean(xv * xv, axis=-1, keepdims=True) + RMS_EPS)
        h_ref[...] = ((xv * r) * g_ref[...] * (1.0 + sc_ref[...]) + sh_ref[...]).astype(BF16)

    row = pl.BlockSpec((tr, D), lambda i: (i, 0))
    vec = pl.BlockSpec((1, D), lambda i: (0, 0))
    return pl.pallas_call(
        body, name=name, grid=(S // tr,), in_specs=[row, vec, vec, vec], out_specs=row,
        out_shape=jax.ShapeDtypeStruct((S, D), BF16), compiler_params=_params("parallel"),
    )(x, gain, sc, sh)


def _norm_mod_bwd(dh, x, dx_res, gain, sc, *, name):
    S, D = x.shape
    tr = min(256, S)
    n_steps = S // tr

    def body(dh_ref, x_ref, dxr_ref, g_ref, sc_ref, dx_ref, dsh_ref, dsc_ref, dgain_ref, acc_sh, acc_a):
        i = pl.program_id(0)
        xv = x_ref[...]
        r = lax.rsqrt(jnp.mean(xv * xv, axis=-1, keepdims=True) + RMS_EPS)
        n = xv * r
        dhv = dh_ref[...].astype(F32)
        dn = dhv * (g_ref[...] * (1.0 + sc_ref[...]))
        dx_ref[...] = dxr_ref[...] + r * (dn - n * jnp.mean(dn * n, axis=-1, keepdims=True))
        p_sh = jnp.sum(dhv, axis=0, keepdims=True)
        p_a = jnp.sum(dhv * n, axis=0, keepdims=True)

        @pl.when(i == 0)
        def _():
            acc_sh[...] = p_sh
            acc_a[...] = p_a

        @pl.when(i > 0)
        def _():
            acc_sh[...] += p_sh
            acc_a[...] += p_a

        @pl.when(i == n_steps - 1)
        def _():
            dsh_ref[...] = acc_sh[...]
            dsc_ref[...] = acc_a[...] * g_ref[...]
            dgain_ref[...] = acc_a[...] * (1.0 + sc_ref[...])

    row = pl.BlockSpec((tr, D), lambda i: (i, 0))
    vec = pl.BlockSpec((1, D), lambda i: (0, 0))
    vshape = jax.ShapeDtypeStruct((1, D), F32)
    return pl.pallas_call(
        body, name=name, grid=(n_steps,), in_specs=[row, row, row, vec, vec],
        out_specs=[row, vec, vec, vec],
        out_shape=[jax.ShapeDtypeStruct((S, D), F32), vshape, vshape, vshape],
        scratch_shapes=[pltpu.VMEM((1, D), F32), pltpu.VMEM((1, D), F32)],
        compiler_params=_params("arbitrary"),
    )(dh, x, dx_res, gain, sc)


def _wout_grad(gmat, w, gate, *, name):
    K, D = w.shape
    tr = min(256, K)
    n_steps = K // tr

    def body(g_ref, w_ref, gate_ref, dw_ref, dgate_ref, acc):
        i = pl.program_id(0)
        gv = g_ref[...]
        dw_ref[...] = (gv * gate_ref[...]).astype(BF16)
        part = jnp.sum(gv * w_ref[...], axis=0, keepdims=True)

        @pl.when(i == 0)
        def _():
            acc[...] = part

        @pl.when(i > 0)
        def _():
            acc[...] += part

        @pl.when(i == n_steps - 1)
        def _():
            dgate_ref[...] = acc[...]

    row = pl.BlockSpec((tr, D), lambda i: (i, 0))
    vec = pl.BlockSpec((1, D), lambda i: (0, 0))
    return pl.pallas_call(
        body, name=name, grid=(n_steps,), in_specs=[row, row, vec], out_specs=[row, vec],
        out_shape=[jax.ShapeDtypeStruct((K, D), BF16), jax.ShapeDtypeStruct((1, D), F32)],
        scratch_shapes=[pltpu.VMEM((1, D), F32)], compiler_params=_params("arbitrary"),
    )(gmat, w, gate)


def _loss_head(y, target, *, name):
    S, D = y.shape
    tr = min(512, S)
    n_steps = S // tr

    def body(y_ref, t_ref, dy_ref, sse_ref, acc):
        i = pl.program_id(0)
        e = y_ref[...] - t_ref[...]
        dy_ref[...] = e * (1.0 / D)
        part = jnp.sum(e * e, axis=0, keepdims=True)

        @pl.when(i == 0)
        def _():
            acc[...] = part

        @pl.when(i > 0)
        def _():
            acc[...] += part

        @pl.when(i == n_steps - 1)
        def _():
            sse_ref[...] = jnp.sum(acc[...], axis=1, keepdims=True)

    row = pl.BlockSpec((tr, D), lambda i: (i, 0))
    return pl.pallas_call(
        body, name=name, grid=(n_steps,), in_specs=[row, row],
        out_specs=[row, pl.BlockSpec((1, 1), lambda i: (0, 0))],
        out_shape=[jax.ShapeDtypeStruct((S, D), F32), jax.ShapeDtypeStruct((1, 1), F32)],
        scratch_shapes=[pltpu.VMEM((1, D), F32)], compiler_params=_params("arbitrary"),
    )(y, target)


def _adamw(w, g_parts, m, v, *, name):
    R, C = w.shape
    P = g_parts.shape[0]
    tr = _tile(R, max(8, 1024 * LANES // C))
    c1 = 1.0 / (1.0 - ADAM_B1 ** ADAM_STEP)
    c2 = 1.0 / (1.0 - ADAM_B2 ** ADAM_STEP)

    def body(w_ref, g_ref, m_ref, v_ref, go_ref, d_ref, mo_ref, vo_ref):
        g = g_ref[0].astype(F32)
        for q in range(1, P):
            g = g + g_ref[q].astype(F32)
        mn = ADAM_B1 * m_ref[...] + (1.0 - ADAM_B1) * g
        vn = ADAM_B2 * v_ref[...] + (1.0 - ADAM_B2) * (g * g)
        go_ref[...] = g
        mo_ref[...] = mn
        vo_ref[...] = vn
        d_ref[...] = -ADAM_LR * ((mn * c1) / (jnp.sqrt(vn * c2) + ADAM_EPS) + ADAM_WD * w_ref[...])

    row = pl.BlockSpec((tr, C), lambda i: (i, 0))
    shp = jax.ShapeDtypeStruct((R, C), F32)
    return pl.pallas_call(
        body, name=name, grid=(R // tr,),
        in_specs=[row, pl.BlockSpec((P, tr, C), lambda i: (0, i, 0)), row, row],
        out_specs=[row, row, row, row], out_shape=[shp, shp, shp, shp],
        compiler_params=_params("parallel"),
    )(w, g_parts, m, v)


_HALO = 16


def _conv_taps(buf, w_ref, rows, cols):
    acc = None
    for j in range(GDN_CONV):
        term = buf[pl.ds(_HALO - (GDN_CONV - 1) + j, rows), cols] * w_ref[j:j + 1, cols]
        acc = term if acc is None else acc + term
    return acc


def _fill_conv_buf(buf, halo_ref, x_ref, rows, first):
    buf[0:_HALO, :] = jnp.where(first, 0.0, halo_ref[...].astype(F32))
    buf[_HALO:_HALO + rows, :] = x_ref[...].astype(F32)


_HM = 3 * GDN_DK
_GDN_ROWS = 256
_PREP_HEADS = 4


def _l2n(seg):
    return lax.rsqrt(jnp.sum(seg * seg, axis=-1, keepdims=True) + RMS_EPS)


def _head_cols(hh):
    return slice(hh * _HM, (hh + 1) * _HM)


def _gdn_prep_fwd(x, conv_w, *, name):
    S, C3 = x.shape
    CB = _PREP_HEADS * _HM
    RB = min(256, S)

    def body(x_ref, halo_ref, w_ref, o_ref, buf):
        i = pl.program_id(0)
        _fill_conv_buf(buf, halo_ref, x_ref, RB, i == 0)
        for hh in range(_PREP_HEADS):
            c0 = hh * _HM
            y = _silu(_conv_taps(buf, w_ref, RB, _head_cols(hh)))
            q, k = y[:, :GDN_DK], y[:, GDN_DK:2 * GDN_DK]
            o_ref[:, c0:c0 + GDN_DK] = q * (_l2n(q) * GDN_DK ** -0.5)
            o_ref[:, c0 + GDN_DK:c0 + 2 * GDN_DK] = k * _l2n(k)
            o_ref[:, c0 + 2 * GDN_DK:c0 + _HM] = y[:, 2 * GDN_DK:]

    hb = RB // _HALO
    return pl.pallas_call(
        body, name=name, grid=(S // RB, C3 // CB),
        in_specs=[pl.BlockSpec((RB, CB), lambda i, j: (i, j)),
                  pl.BlockSpec((_HALO, CB), lambda i, j: (jnp.maximum(i * hb - 1, 0), j)),
                  pl.BlockSpec((GDN_CONV, CB), lambda i, j: (0, j))],
        out_specs=pl.BlockSpec((RB, CB), lambda i, j: (i, j)),
        out_shape=jax.ShapeDtypeStruct((S, C3), F32),
        scratch_shapes=[pltpu.VMEM((RB + _HALO, CB), F32)],
        compiler_params=_params("parallel", "parallel"),
    )(x, x, conv_w)


def _gdn_prep_bwd_pre(dn, x, conv_w, *, name):
    S, C3 = x.shape
    CB = _PREP_HEADS * _HM
    RB = min(256, S)
    n_steps = S // RB

    def body(dn_ref, x_ref, halo_ref, w_ref, dc_ref, dw_ref, buf):
        i = pl.program_id(1)
        _fill_conv_buf(buf, halo_ref, x_ref, RB, i == 0)
        head_parts = []
        for hh in range(_PREP_HEADS):
            c0, cols = hh * _HM, _head_cols(hh)
            acc = _conv_taps(buf, w_ref, RB, cols)
            sg = _sigmoid(acc)
            y = acc * sg
            dsilu = sg * (1.0 + acc * (1.0 - sg))
            for part, scale in ((0, GDN_DK ** -0.5), (1, 1.0)):
                sl = slice(part * GDN_DK, (part + 1) * GDN_DK)
                seg = y[:, sl]
                r = _l2n(seg)
                n = seg * r
                d = dn_ref[:, c0 + part * GDN_DK:c0 + (part + 1) * GDN_DK] * scale
                dc_ref[:, c0 + part * GDN_DK:c0 + (part + 1) * GDN_DK] = (
                    r * (d - n * jnp.sum(d * n, axis=-1, keepdims=True)) * dsilu[:, sl])
            dc_ref[:, c0 + 2 * GDN_DK:c0 + _HM] = dn_ref[:, c0 + 2 * GDN_DK:c0 + _HM] * dsilu[:, 2 * GDN_DK:]
            dc = dc_ref[:, cols]
            taps = [jnp.sum(dc * buf[pl.ds(_HALO - (GDN_CONV - 1) + t, RB), cols], axis=0, keepdims=True)
                    for t in range(GDN_CONV)]
            head_parts.append(jnp.concatenate(taps + [jnp.zeros((8 - GDN_CONV, _HM), F32)], axis=0))
        part = jnp.concatenate(head_parts, axis=1)

        @pl.when(i == 0)
        def _():
            dw_ref[...] = part

        @pl.when(i > 0)
        def _():
            dw_ref[...] += part

    hb = RB // _HALO
    return pl.pallas_call(
        body, name=name, grid=(C3 // CB, n_steps),
        in_specs=[pl.BlockSpec((RB, CB), lambda j, i: (i, j)),
                  pl.BlockSpec((RB, CB), lambda j, i: (i, j)),
                  pl.BlockSpec((_HALO, CB), lambda j, i: (jnp.maximum(i * hb - 1, 0), j)),
                  pl.BlockSpec((GDN_CONV, CB), lambda j, i: (0, j))],
        out_specs=[pl.BlockSpec((RB, CB), lambda j, i: (i, j)),
                   pl.BlockSpec((8, CB), lambda j, i: (0, j))],
        out_shape=[jax.ShapeDtypeStruct((S, C3), F32), jax.ShapeDtypeStruct((8, C3), F32)],
        scratch_shapes=[pltpu.VMEM((RB + _HALO, CB), F32)],
        compiler_params=_params("parallel", "arbitrary"),
    )(dn, x, x, conv_w)


def _gdn_conv_bwd_x(dc, conv_w, *, name):
    S, C3 = dc.shape
    CB = _PREP_HEADS * _HM
    RB = min(256, S)
    n_steps = S // RB

    def body(dc_ref, halo_ref, w_ref, dx_ref, buf):
        i = pl.program_id(0)
        buf[0:RB, :] = dc_ref[...]
        buf[RB:RB + _HALO, :] = jnp.where(i == n_steps - 1, 0.0, halo_ref[...])
        for hh in range(_PREP_HEADS):
            cols = _head_cols(hh)
            acc = None
            for j in range(GDN_CONV):
                term = buf[pl.ds(GDN_CONV - 1 - j, RB), cols] * w_ref[j:j + 1, cols]
                acc = term if acc is None else acc + term
            dx_ref[:, cols] = acc.astype(BF16)

    hb = RB // _HALO
    last = S // _HALO - 1
    return pl.pallas_call(
        body, name=name, grid=(n_steps, C3 // CB),
        in_specs=[pl.BlockSpec((RB, CB), lambda i, j: (i, j)),
                  pl.BlockSpec((_HALO, CB), lambda i, j: (jnp.minimum((i + 1) * hb, last), j)),
                  pl.BlockSpec((GDN_CONV, CB), lambda i, j: (0, j))],
        out_specs=pl.BlockSpec((RB, CB), lambda i, j: (i, j)),
        out_shape=jax.ShapeDtypeStruct((S, C3), BF16),
        scratch_shapes=[pltpu.VMEM((RB + _HALO, CB), F32)],
        compiler_params=_params("parallel", "parallel"),
    )(dc, dc, conv_w)


def _split_bf16(a):
    hi = a.astype(BF16)
    return hi, (a - hi.astype(F32)).astype(BF16)


def _dot(a, b, dims="nn", exact=False):
    def dot(p, q):
        return lax.dot_general(p, q, _DOT_DIMS[dims], preferred_element_type=F32)

    if exact:
        (ah, al), (bh, bl) = _split_bf16(a), _split_bf16(b)
        return dot(ah, bh) + (dot(ah, bl) + dot(al, bh))
    return dot(a.astype(BF16), b.astype(BF16))


def _softplus(x):
    return jnp.maximum(x, 0.0) + jnp.log(1.0 + jnp.exp(-jnp.abs(x)))


def _to_col(row, eye):
    return jnp.sum(jnp.where(eye, row, 0.0), axis=1, keepdims=True)


def _to_row(col, eye):
    return jnp.sum(jnp.where(eye, col, 0.0), axis=0, keepdims=True)


def _unit_lower_inverse(low, ri, ci):
    n = range(len(low))
    C = low[0].shape[0]
    eye = jnp.where(ri == ci, 1.0, 0.0)
    pair = (ri >> 1) == (ci >> 1)
    x = [eye - jnp.where(pair, low[j], 0.0) for j in n]
    m, sh = 2, 1
    while m < C:
        join = ((ri >> (sh + 1)) == (ci >> (sh + 1))) & (((ri >> sh) & 1) == 1) & (((ci >> sh) & 1) == 0)
        y = [_dot(x[j], jnp.where(join, low[j], 0.0)) for j in n]
        x = [x[j] - _dot(y[j], x[j]) for j in n]
        m, sh = 2 * m, sh + 1
    lx = [_dot(low[j], x[j], exact=True) for j in n]
    corr = [_dot(x[j], eye - x[j] - lx[j]) for j in n]
    return [x[j] + corr[j] for j in n]


def _gdn_local_batch(qkv, g_row, beta_row, ri, ci):
    n = range(len(qkv))
    eye, tril, strict = ri == ci, ri >= ci, ri > ci
    q = [qkv[j][:, :GDN_DK] for j in n]
    k = [qkv[j][:, GDN_DK:2 * GDN_DK] for j in n]
    v = [qkv[j][:, 2 * GDN_DK:] for j in n]
    g_col = [_to_col(g_row[j], eye) for j in n]
    beta_col = [_to_col(beta_row[j], eye) for j in n]
    gc_col = [jnp.sum(jnp.where(tril, g_row[j], 0.0), axis=1, keepdims=True) for j in n]
    gc_row = [jnp.sum(jnp.where(ri <= ci, g_col[j], 0.0), axis=0, keepdims=True) for j in n]
    g_last = [jnp.sum(g_row[j], axis=1, keepdims=True) for j in n]
    decay = [jnp.where(tril, jnp.exp(jnp.minimum(gc_col[j] - gc_row[j], 0.0)), 0.0) for j in n]
    e_col = [jnp.exp(gc_col[j]) for j in n]
    f_col = [jnp.exp(g_last[j] - gc_col[j]) for j in n]
    e_last = [jnp.exp(g_last[j]) for j in n]
    kb = [k[j] * beta_col[j] for j in n]
    vb = [v[j] * beta_col[j] for j in n]
    kk = [_dot(kb[j], k[j], "nt") for j in n]
    qk = [_dot(q[j], k[j], "nt") for j in n]
    low = [jnp.where(strict, kk[j] * decay[j], 0.0) for j in n]
    att = [qk[j] * decay[j] for j in n]
    return dict(q=q, k=k, v=v, beta_col=beta_col, decay=decay, e_col=e_col, f_col=f_col, e_last=e_last,
                kb=kb, vb=vb, low=low, att=att, eye=eye, strict=strict, tril=tril)


def _chunk_iotas():
    C = GDN_CHUNK
    return lax.broadcasted_iota(jnp.int32, (C, C), 0), lax.broadcasted_iota(jnp.int32, (C, C), 1)


def _gdn_chunk_fwd(qkv, ab, a_log, dt_bias, *, name, riding=None):
    S = qkv.shape[0]
    H, C, DK = GDN_HEADS, GDN_CHUNK, GDN_DK
    RB = min(_GDN_ROWS, S)
    NCB, NB, NC = RB // C, S // RB, S // C
    heads = range(H)

    def body(qkv_ref, ab_ref, alog_ref, dtb_ref, *rest):
        if riding is None:
            o_ref, st_ref, t_ref, state, u_s, w_s, qe_s, kf_s, att_s = rest
        else:
            ride_src, o_ref, st_ref, t_ref, ride_dst, state, u_s, w_s, qe_s, kf_s, att_s, *ride_sems = rest
        nb = pl.program_id(0)
        if riding is not None:
            finish_ride = _ride(nb == 0, nb == NB - 1, ride_src, ride_dst, ride_sems, True)

        @pl.when(nb == 0)
        def _():
            state[...] = jnp.zeros_like(state)

        ri, ci = _chunk_iotas()
        neg_a = [-jnp.exp(alog_ref[h]) for h in heads]
        e_last = []
        for c in range(NCB):
            rows = pl.ds(c * C, C)
            g_row = [neg_a[h] * _softplus(ab_ref[h, c] + dtb_ref[h]) for h in heads]
            beta_row = [_sigmoid(ab_ref[H + h, c]) for h in heads]
            L = _gdn_local_batch([qkv_ref[rows, h * _HM:(h + 1) * _HM] for h in heads], g_row, beta_row, ri, ci)
            tinv = _unit_lower_inverse(L["low"], ri, ci)
            u = [_dot(tinv[h], L["vb"][h], exact=True) for h in heads]
            w = [_dot(tinv[h], L["kb"][h] * L["e_col"][h], exact=True) for h in heads]
            for h in heads:
                t_ref[h, c] = tinv[h]
                u_s[c, h] = u[h]
                w_s[c, h] = w[h].astype(BF16)
                qe_s[c, h] = (L["q"][h] * L["e_col"][h]).astype(BF16)
                kf_s[c, h] = (L["k"][h] * L["f_col"][h]).astype(BF16)
                att_s[c, h] = L["att"][h].astype(BF16)
            e_last.append(L["e_last"])
        st = [state[h] for h in heads]
        for c in range(NCB):
            rows = pl.ds(c * C, C)
            stb = [st[h].astype(BF16) for h in heads]
            vn = [u_s[c, h] - _dot(w_s[c, h], stb[h]) for h in heads]
            vnb = [vn[h].astype(BF16) for h in heads]
            out = [_dot(qe_s[c, h], stb[h]) + _dot(att_s[c, h], vnb[h]) for h in heads]
            new = [st[h] * e_last[c][h] + _dot(kf_s[c, h], vnb[h], "tn") for h in heads]
            for h in heads:
                o_ref[rows, h * DK:(h + 1) * DK] = out[h]
                st_ref[h, c] = st[h]
            st = new
        for h in heads:
            state[h] = st[h]
        if riding is not None:
            finish_ride()

    ride_args, ride_specs, ride_out, ride_scratch = _riding(riding, True)
    return pl.pallas_call(
        body, name=name, grid=(NB,),
        in_specs=[pl.BlockSpec((RB, H * _HM), lambda n: (n, 0)),
                  pl.BlockSpec((2 * H, NCB, 1, C), lambda n: (0, n, 0, 0)),
                  pl.BlockSpec((H, 1, 1), lambda n: (0, 0, 0)),
                  pl.BlockSpec((H, 1, 1), lambda n: (0, 0, 0))] + ride_specs,
        out_specs=[pl.BlockSpec((RB, H * DK), lambda n: (n, 0)),
                   pl.BlockSpec((H, NCB, DK, DK), lambda n: (0, n, 0, 0)),
                   pl.BlockSpec((H, NCB, C, C), lambda n: (0, n, 0, 0))] + ride_specs,
        out_shape=[jax.ShapeDtypeStruct((S, H * DK), F32),
                   jax.ShapeDtypeStruct((H, NC, DK, DK), F32),
                   jax.ShapeDtypeStruct((H, NC, C, C), F32)] + ride_out,
        scratch_shapes=[pltpu.VMEM((H, DK, DK), F32), pltpu.VMEM((NCB, H, C, DK), F32),
                        pltpu.VMEM((NCB, H, C, DK), BF16), pltpu.VMEM((NCB, H, C, DK), BF16),
                        pltpu.VMEM((NCB, H, C, DK), BF16), pltpu.VMEM((NCB, H, C, C), BF16)] + ride_scratch,
        compiler_params=_params("arbitrary"),
    )(qkv, ab, a_log, dt_bias, *ride_args)


def _chip_copies(src_ref, dst_ref, send_sems, recv_sems, local_sem, gather=False):
    x, y, c = lax.axis_index("x"), lax.axis_index("y"), lax.axis_index("c")
    here = 2 * x + y
    landing = dst_ref.at[here, c] if gather else dst_ref.at[here]
    mine = pltpu.make_async_copy(src_ref if gather else src_ref.at[here], landing, local_sem)
    copies = []
    for rel in range(1, N_DEV // 2):
        px = 1 - x if rel & 2 else x
        py = 1 - y if rel & 1 else y
        copies.append(pltpu.make_async_remote_copy(
            src_ref=src_ref if gather else src_ref.at[2 * px + py], dst_ref=landing,
            send_sem=send_sems.at[rel - 1], recv_sem=recv_sems.at[rel - 1],
            device_id=(px, py, c), device_id_type=pl.DeviceIdType.MESH))
    return mine, copies


def _riding(riding, gather):
    if riding is None:
        return [], [], [], []
    shape = (N_DEV // 2, 2) + riding.shape if gather else riding.shape
    n_peers = N_DEV // 2 - 1
    return ([riding], [pl.BlockSpec(memory_space=pl.ANY)], [jax.ShapeDtypeStruct(shape, riding.dtype)],
            [pltpu.SemaphoreType.DMA((n_peers,)), pltpu.SemaphoreType.DMA((n_peers,)), pltpu.SemaphoreType.DMA(())])


def _ride(first, last, src, dst, sems, gather):
    @pl.when(first)
    def _():
        mine, copies = _chip_copies(src, dst, *sems, gather=gather)
        mine.start()
        for cp in copies:
            cp.start()

    def finish():
        @pl.when(last)
        def _():
            mine, copies = _chip_copies(src, dst, *sems, gather=gather)
            for cp in copies:
                cp.wait()
            mine.wait()

    return finish


def _gdn_chunk_bwd(qkv, ab, a_log, dt_bias, states, tinvs, do, *, name, riding=None):
    S = qkv.shape[0]
    H, C, DK = GDN_HEADS, GDN_CHUNK, GDN_DK
    RB = min(_GDN_ROWS, S)
    NCB, NB, NC = RB // C, S // RB, S // C
    heads = range(H)

    def body(qkv_ref, ab_ref, alog_ref, dtb_ref, st_ref, t_ref, do_ref, *rest):
        if riding is None:
            dqkv_ref, dab_ref, dalog_ref, ddtb_ref, dstate, w_s, vn_s, qe_s, kf_s, att_s, dvn_s, dkf_s = rest
        else:
            (ride_src, dqkv_ref, dab_ref, dalog_ref, ddtb_ref, ride_dst,
             dstate, w_s, vn_s, qe_s, kf_s, att_s, dvn_s, dkf_s, *ride_sems) = rest
        nb = pl.program_id(0)
        if riding is not None:
            finish_ride = _ride(nb == 0, nb == NB - 1, ride_src, ride_dst, ride_sems, False)

        @pl.when(nb == 0)
        def _():
            dstate[...] = jnp.zeros_like(dstate)
            dalog_ref[...] = jnp.zeros_like(dalog_ref)
            ddtb_ref[...] = jnp.zeros_like(ddtb_ref)

        ri, ci = _chunk_iotas()
        neg_a = [-jnp.exp(alog_ref[h]) for h in heads]

        def local(c):
            rows = pl.ds(c * C, C)
            a_pre = [ab_ref[h, c] + dtb_ref[h] for h in heads]
            g_row = [neg_a[h] * _softplus(a_pre[h]) for h in heads]
            beta_row = [_sigmoid(ab_ref[H + h, c]) for h in heads]
            L = _gdn_local_batch([qkv_ref[rows, h * _HM:(h + 1) * _HM] for h in heads], g_row, beta_row, ri, ci)
            return L, a_pre, g_row, beta_row

        e_last = [None] * NCB
        for c in range(NCB):
            L, _, _, _ = local(c)
            kbe = [L["kb"][h] * L["e_col"][h] for h in heads]
            u = [_dot(t_ref[h, c], L["vb"][h], exact=True) for h in heads]
            w = [_dot(t_ref[h, c], kbe[h], exact=True) for h in heads]
            vn = [u[h] - _dot(w[h], st_ref[h, c]) for h in heads]
            for h in heads:
                w_s[c, h] = w[h].astype(BF16)
                vn_s[c, h] = vn[h].astype(BF16)
                qe_s[c, h] = (L["q"][h] * L["e_col"][h]).astype(BF16)
                kf_s[c, h] = (L["k"][h] * L["f_col"][h]).astype(BF16)
                att_s[c, h] = L["att"][h].astype(BF16)
            e_last[c] = L["e_last"]

        dst = [dstate[h] for h in heads]
        de_last = [None] * NCB
        for c in reversed(range(NCB)):
            rows = pl.ds(c * C, C)
            dob = [do_ref[rows, h * DK:(h + 1) * DK].astype(BF16) for h in heads]
            dstb = [dst[h].astype(BF16) for h in heads]
            dvn = [_dot(att_s[c, h], dob[h], "tn") + _dot(kf_s[c, h], dstb[h]) for h in heads]
            dkf = [_dot(vn_s[c, h], dstb[h], "nt") for h in heads]
            de_last[c] = [jnp.sum(jnp.sum(dst[h] * st_ref[h, c], axis=1, keepdims=True), axis=0, keepdims=True)
                          for h in heads]
            new = [dst[h] * e_last[c][h] + _dot(qe_s[c, h], dob[h], "tn")
                   - _dot(w_s[c, h], dvn[h].astype(BF16), "tn") for h in heads]
            for h in heads:
                dvn_s[c, h] = dvn[h]
                dkf_s[c, h] = dkf[h]
            dst = new
        for h in heads:
            dstate[h] = dst[h]

        for c in range(NCB):
            rows = pl.ds(c * C, C)
            L, a_pre, g_row, beta_row = local(c)
            q, k, v, kb, vb = L["q"], L["k"], L["v"], L["kb"], L["vb"]
            e_col, f_col, decay, beta_col = L["e_col"], L["f_col"], L["decay"], L["beta_col"]
            eye, strict, tril = L["eye"], L["strict"], L["tril"]
            tinv = [t_ref[h, c] for h in heads]
            stb = [st_ref[h, c].astype(BF16) for h in heads]
            dov = [do_ref[rows, h * DK:(h + 1) * DK] for h in heads]
            dvn = [dvn_s[c, h] for h in heads]
            dkf = [dkf_s[c, h] for h in heads]
            kbe = [kb[h] * e_col[h] for h in heads]
            datt = [jnp.where(tril, _dot(dov[h], vn_s[c, h], "nt"), 0.0) for h in heads]
            dqe = [_dot(dov[h], stb[h], "nt") for h in heads]
            dw = [-_dot(dvn[h], stb[h], "nt") for h in heads]
            dt = [_dot(dvn[h], vb[h], "nt") + _dot(dw[h], kbe[h], "nt") for h in heads]
            dvb = [_dot(tinv[h], dvn[h], "tn", exact=True) for h in heads]
            dkbe = [_dot(tinv[h], dw[h], "tn", exact=True) for h in heads]
            tdt = [_dot(tinv[h], dt[h], "tn", exact=True) for h in heads]
            dlow = [-jnp.where(strict, _dot(tdt[h], tinv[h], "nt", exact=True), 0.0) for h in heads]
            dkk = [dlow[h] * decay[h] for h in heads]
            dqk = [datt[h] * decay[h] for h in heads]
            dkb = [_dot(dkk[h], k[h]) + dkbe[h] * e_col[h] for h in heads]
            dk = [_dot(dkk[h], kb[h], "tn") + _dot(dqk[h], q[h], "tn") + dkf[h] * f_col[h] + dkb[h] * beta_col[h]
                  for h in heads]
            dq = [_dot(dqk[h], k[h]) + dqe[h] * e_col[h] for h in heads]
            for h in heads:
                dqkv_ref[rows, h * _HM:h * _HM + DK] = dq[h]
                dqkv_ref[rows, h * _HM + DK:h * _HM + 2 * DK] = dk[h]
                dqkv_ref[rows, h * _HM + 2 * DK:(h + 1) * _HM] = dvb[h] * beta_col[h]

            dbeta_col = [jnp.sum(k[h] * dkb[h] + v[h] * dvb[h], axis=1, keepdims=True) for h in heads]
            pmat = [dlow[h] * L["low"][h] + datt[h] * L["att"][h] for h in heads]
            df_col = [jnp.sum(k[h] * dkf[h], axis=1, keepdims=True) * f_col[h] for h in heads]
            dgc_col = [jnp.sum(pmat[h], axis=1, keepdims=True)
                       + jnp.sum(q[h] * dqe[h] + kb[h] * dkbe[h], axis=1, keepdims=True) * e_col[h] - df_col[h]
                       for h in heads]
            dgc_row = [_to_row(dgc_col[h], eye) - jnp.sum(pmat[h], axis=0, keepdims=True) for h in heads]
            dg_last = [jnp.sum(df_col[h], axis=0, keepdims=True) + de_last[c][h] * L["e_last"][h] for h in heads]
            dgc_c = [_to_col(dgc_row[h], eye) for h in heads]
            dg_row = [jnp.sum(jnp.where(ri >= ci, dgc_c[h], 0.0), axis=0, keepdims=True) + dg_last[h] for h in heads]
            dbeta_row = [_to_row(dbeta_col[h], eye) for h in heads]
            for h in heads:
                da_row = dg_row[h] * neg_a[h] * _sigmoid(a_pre[h])
                dab_ref[h, c] = da_row
                dab_ref[H + h, c] = dbeta_row[h] * beta_row[h] * (1.0 - beta_row[h])
                dalog_ref[h] += jnp.sum(dg_row[h] * g_row[h], axis=1, keepdims=True)
                ddtb_ref[h] += jnp.sum(da_row, axis=1, keepdims=True)

        if riding is not None:
            finish_ride()

    rev = lambda n: NB - 1 - n
    vec = pl.BlockSpec((H, 1, 1), lambda n: (0, 0, 0))
    gates = pl.BlockSpec((2 * H, NCB, 1, C), lambda n: (0, rev(n), 0, 0))
    wide = pl.BlockSpec((RB, H * _HM), lambda n: (rev(n), 0))
    item = lambda dt: pltpu.VMEM((NCB, H, C, DK), dt)
    ride_args, ride_specs, ride_out, ride_scratch = _riding(riding, False)
    return pl.pallas_call(
        body, name=name, grid=(NB,),
        in_specs=[wide, gates, vec, vec,
                  pl.BlockSpec((H, NCB, DK, DK), lambda n: (0, rev(n), 0, 0)),
                  pl.BlockSpec((H, NCB, C, C), lambda n: (0, rev(n), 0, 0)),
                  pl.BlockSpec((RB, H * DK), lambda n: (rev(n), 0))] + ride_specs,
        out_specs=[wide, gates, vec, vec] + ride_specs,
        out_shape=[jax.ShapeDtypeStruct((S, H * _HM), F32),
                   jax.ShapeDtypeStruct((2 * H, NC, 1, C), F32),
                   jax.ShapeDtypeStruct((H, 1, 1), F32),
                   jax.ShapeDtypeStruct((H, 1, 1), F32)] + ride_out,
        scratch_shapes=[pltpu.VMEM((H, DK, DK), F32), item(BF16), item(BF16), item(BF16), item(BF16),
                        pltpu.VMEM((NCB, H, C, C), BF16), item(F32), item(F32)] + ride_scratch,
        compiler_params=_params("arbitrary"),
    )(qkv, ab, a_log, dt_bias, states, tinvs, do, *ride_args)


def _gdn_outnorm_fwd(o, z, gain, *, name):
    S, HV = o.shape
    RB = min(256, S)

    def body(o_ref, z_ref, g_ref, y_ref):
        for h in range(HV // GDN_DK):
            cols = slice(h * GDN_DK, (h + 1) * GDN_DK)
            ov = o_ref[:, cols]
            r = lax.rsqrt(jnp.mean(ov * ov, axis=-1, keepdims=True) + RMS_EPS)
            y_ref[:, cols] = (ov * r * g_ref[...] * _silu(z_ref[:, cols].astype(F32))).astype(BF16)

    blk = pl.BlockSpec((RB, HV), lambda i: (i, 0))
    return pl.pallas_call(
        body, name=name, grid=(S // RB,),
        in_specs=[blk, blk, pl.BlockSpec((1, GDN_DK), lambda i: (0, 0))], out_specs=blk,
        out_shape=jax.ShapeDtypeStruct((S, HV), BF16), compiler_params=_params("parallel"),
    )(o, z, gain)


def _gdn_outnorm_bwd(dy, o, z, gain, *, name):
    S, HV = o.shape
    RB = min(256, S)

    def body(dy_ref, o_ref, z_ref, g_ref, do_ref, dz_ref, dg_ref):
        part = None
        for h in range(HV // GDN_DK):
            cols = slice(h * GDN_DK, (h + 1) * GDN_DK)
            ov = o_ref[:, cols]
            zv = z_ref[:, cols].astype(F32)
            dyv = dy_ref[:, cols].astype(F32)
            r = lax.rsqrt(jnp.mean(ov * ov, axis=-1, keepdims=True) + RMS_EPS)
            n = ov * r
            sg = _sigmoid(zv)
            dng = dyv * (zv * sg)
            dn = dng * g_ref[...]
            do_ref[:, cols] = r * (dn - n * jnp.mean(dn * n, axis=-1, keepdims=True))
            dz_ref[:, cols] = (dyv * (n * g_ref[...]) * (sg * (1.0 + zv * (1.0 - sg)))).astype(BF16)
            p = jnp.sum(dng * n, axis=0, keepdims=True)
            part = p if part is None else part + p

        @pl.when(pl.program_id(0) == 0)
        def _():
            dg_ref[...] = part

        @pl.when(pl.program_id(0) > 0)
        def _():
            dg_ref[...] += part

    blk = pl.BlockSpec((RB, HV), lambda i: (i, 0))
    vec = pl.BlockSpec((1, GDN_DK), lambda i: (0, 0))
    return pl.pallas_call(
        body, name=name, grid=(S // RB,),
        in_specs=[blk, blk, blk, vec], out_specs=[blk, blk, vec],
        out_shape=[jax.ShapeDtypeStruct((S, HV), F32), jax.ShapeDtypeStruct((S, HV), BF16),
                   jax.ShapeDtypeStruct((1, GDN_DK), F32)],
        compiler_params=_params("arbitrary"),
    )(dy, o, z, gain)


def _head_mask():
    return lax.broadcasted_iota(jnp.int32, (DSW_BLK, LANES), 1) < DSW_DH


def _per_head_sum(t, first):
    s0 = jnp.sum(jnp.where(first, t, 0.0), axis=-1, keepdims=True)
    s1 = jnp.sum(jnp.where(first, 0.0, t), axis=-1, keepdims=True)
    return jnp.where(first, s0, s1)


def _rms2(x, gain, first):
    r = lax.rsqrt(_per_head_sum(x * x, first) * (1.0 / DSW_DH) + RMS_EPS)
    xh = x * r
    return xh, r, xh * gain


def _rms2_bwd(dy, xh, r, gain, first):
    dxh = dy * gain
    return r * (dxh - xh * (_per_head_sum(dxh * xh, first) * (1.0 / DSW_DH)))


def _split_heads(x, first):
    return [jnp.where(first, x, 0.0).astype(BF16), jnp.where(first, 0.0, x).astype(BF16)]


_HP = LANES // DSW_DH
_DSW_W = DSW_HEADS * DSW_DH
_DSW_ROWS = 1024
_DSW_BATCH = 8


def _dsw_geometry(S, g):
    d = DSW_GROUPS[g][1]
    slab = DSW_BLK * d
    tb = max(1, min(_DSW_ROWS, S) // slab)
    return d, slab, tb, S // (tb * slab)


def _block_rows(t, r, slab, d):
    return pl.ds(t * slab + r, DSW_BLK) if d == 1 else pl.ds(t * slab + r, DSW_BLK, stride=d)


def _dsw_attn_fwd(q, k, v, bias, q_gain, k_gain, prev_out, *, g, name):
    S, WT = q.shape
    B = DSW_BLK
    d, slab, tb, n_tiles = _dsw_geometry(S, g)
    rt = tb * slab
    cb = g * (_DSW_W // LANES)
    batch_res = max(1, _DSW_BATCH // tb)

    def body(q_ref, kp_ref, kc_ref, vp_ref, vc_ref, bias_ref, qg_ref, kg_ref, *rest):
        o_ref, lse_ref = rest[-2:]
        i = pl.program_id(1)
        qg, kg = qg_ref[...] * DSW_DH ** -0.5, kg_ref[...]
        col = lax.broadcasted_iota(jnp.int32, (B, 2 * B), 1)
        first = _head_mask()
        heads = range(_HP)
        for r0 in range(0, d, batch_res):
            res = range(r0, min(d, r0 + batch_res))
            k_raw = {(r, -1): kp_ref[_block_rows(0, r, slab, d), :] for r in res}
            v_raw = {(r, -1): vp_ref[_block_rows(0, r, slab, d), :] for r in res}
            q_raw = {}
            for r in res:
                for t in range(tb):
                    rows = _block_rows(t, r, slab, d)
                    q_raw[r, t], k_raw[r, t], v_raw[r, t] = q_ref[rows, :], kc_ref[rows, :], vc_ref[rows, :]
            kn = {key: _rms2(x, kg, first)[2].astype(BF16) for key, x in k_raw.items()}
            vb = {key: x.astype(BF16) for key, x in v_raw.items()}
            qn = {key: _split_heads(_rms2(x, qg, first)[2], first) for key, x in q_raw.items()}
            items = [(r, t, h) for r in res for t in range(tb) for h in heads]
            s = {}
            for r, t, h in items:
                sv = _dot(qn[r, t][h], jnp.concatenate([kn[r, t - 1], kn[r, t]], axis=0), "nt") + bias_ref[h]
                s[r, t, h] = jnp.where((i == 0) & (col < B), NEG_BIG, sv) if t == 0 else sv
            m = {it: jnp.max(s[it], axis=-1, keepdims=True) for it in items}
            p = {it: jnp.exp(s[it] - m[it]) for it in items}
            l = {it: jnp.sum(p[it], axis=-1, keepdims=True) for it in items}
            o = {(r, t, h): _dot(p[r, t, h], jnp.concatenate([vb[r, t - 1], vb[r, t]], axis=0)) for r, t, h in items}
            for r in res:
                for t in range(tb):
                    rows = _block_rows(t, r, slab, d)
                    o_ref[rows, :] = jnp.where(first, o[r, t, 0] / l[r, t, 0], o[r, t, 1] / l[r, t, 1])
                    lse_ref[rows, :] = jnp.where(first, m[r, t, 0] + jnp.log(l[r, t, 0]),
                                                 m[r, t, 1] + jnp.log(l[r, t, 1]))

    cur = pl.BlockSpec((rt, LANES), lambda hp, i: (i, cb + hp))
    prev = pl.BlockSpec((slab, LANES), lambda hp, i: (jnp.maximum(i * tb - 1, 0), cb + hp))
    vec = pl.BlockSpec((1, LANES), lambda hp, i: (0, 0))
    shp = jax.ShapeDtypeStruct((S, WT), F32)
    carried = [] if prev_out is None else list(prev_out)
    n_in = 8
    return pl.pallas_call(
        body, name=name, grid=(_DSW_W // LANES, n_tiles),
        in_specs=[cur, prev, cur, prev, cur, pl.BlockSpec((_HP, B, 2 * B), lambda hp, i: (hp, 0, 0)), vec, vec]
                 + [pl.BlockSpec(memory_space=pl.ANY)] * len(carried),
        out_specs=[cur, cur], out_shape=[shp, shp],
        input_output_aliases={n_in + j: j for j in range(len(carried))},
        compiler_params=_params("parallel", "parallel"),
    )(q, k, k, v, v, bias, jnp.tile(q_gain, (1, _HP)), jnp.tile(k_gain, (1, _HP)), *carried)


def _dsw_merge(o_g, lse_g, *, name):
    S = o_g.shape[0]
    W, G = _DSW_W, len(DSW_GROUPS)
    tr = min(512, S)

    def body(o_ref, l_ref, out_ref, lse_ref):
        ls = [l_ref[:, g * W:(g + 1) * W] for g in range(G)]
        m = ls[0]
        for g in range(1, G):
            m = jnp.maximum(m, ls[g])
        den = jnp.zeros_like(m)
        acc = jnp.zeros_like(m)
        for g in range(G):
            wg = jnp.exp(ls[g] - m)
            den = den + wg
            acc = acc + wg * o_ref[:, g * W:(g + 1) * W]
        out_ref[...] = acc / den
        lse_ref[...] = m + jnp.log(den)

    wide = pl.BlockSpec((tr, G * W), lambda i: (i, 0))
    blk = pl.BlockSpec((tr, W), lambda i: (i, 0))
    shp = jax.ShapeDtypeStruct((S, W), F32)
    return pl.pallas_call(
        body, name=name, grid=(S // tr,), in_specs=[wide, wide], out_specs=[blk, blk],
        out_shape=[shp, shp], compiler_params=_params("parallel"),
    )(o_g, lse_g)


def _dsw_attn_bwd(q, k, v, o, lse, do, bias, q_gain, k_gain, prev_out, *, g, name):
    S, WT = q.shape
    B = DSW_BLK
    d, slab, tb, n_tiles = _dsw_geometry(S, g)
    rt = tb * slab
    cb = g * (_DSW_W // LANES)
    n_slabs = S // slab
    scale = DSW_DH ** -0.5
    batch_res = max(1, _DSW_BATCH // tb)

    def body(q_ref, qx_ref, kp_ref, kc_ref, vp_ref, vc_ref, o_ref, ox_ref, l_ref, lx_ref, do_ref, dox_ref,
             bias_ref, qg_ref, kg_ref, *rest):
        dq_ref, dk_ref, dv_ref, db_ref, dqg_ref, dkg_ref = rest[-6:]
        hp, i = pl.program_id(0), pl.program_id(1)
        qg, kg = qg_ref[...] * scale, kg_ref[...]
        col = lax.broadcasted_iota(jnp.int32, (B, 2 * B), 1)
        has_next = i < n_tiles - 1

        @pl.when(i == 0)
        def _():
            db_ref[...] = jnp.zeros_like(db_ref)

        dqg_acc = jnp.zeros((1, LANES), F32)
        dkg_acc = jnp.zeros((1, LANES), F32)
        first = _head_mask()
        heads = range(_HP)
        for r0 in range(0, d, batch_res):
            res = range(r0, min(d, r0 + batch_res))
            q_raw, k_raw, v_raw, o_raw, l_raw, do_raw = {}, {}, {}, {}, {}, {}
            for r in res:
                first_rows = _block_rows(0, r, slab, d)
                k_raw[r, -1], v_raw[r, -1] = kp_ref[first_rows, :], vp_ref[first_rows, :]
                for t in range(tb):
                    rows = _block_rows(t, r, slab, d)
                    q_raw[r, t], o_raw[r, t], l_raw[r, t], do_raw[r, t] = (
                        q_ref[rows, :], o_ref[rows, :], l_ref[rows, :], do_ref[rows, :])
                    k_raw[r, t], v_raw[r, t] = kc_ref[rows, :], vc_ref[rows, :]
                q_raw[r, tb], o_raw[r, tb], l_raw[r, tb], do_raw[r, tb] = (
                    qx_ref[first_rows, :], ox_ref[first_rows, :], lx_ref[first_rows, :], dox_ref[first_rows, :])
            kk = {key: _rms2(x, kg, first) for key, x in k_raw.items()}
            qq = {key: _rms2(x, qg, first) for key, x in q_raw.items()}
            knb = {key: kk[key][2].astype(BF16) for key in kk}
            qnb = {key: _split_heads(qq[key][2], first) for key in qq}
            vb = {key: x.astype(BF16) for key, x in v_raw.items()}
            dob = {key: _split_heads(x, first) for key, x in do_raw.items()}
            delta = {key: _per_head_sum(do_raw[key] * o_raw[key], first) for key in q_raw}
            pick = lambda x, h: x[:, h * DSW_DH:h * DSW_DH + 1]
            full = [(r, t, h) for r in res for t in range(tb) for h in heads]
            half = [(r, tb, h) for r in res for h in heads]
            s = {}
            for r, t, h in full:
                sv = _dot(qnb[r, t][h], jnp.concatenate([knb[r, t - 1], knb[r, t]], axis=0), "nt") + bias_ref[h]
                s[r, t, h] = jnp.where((i == 0) & (col < B), NEG_BIG, sv) if t == 0 else sv
            for r, t, h in half:
                s[r, t, h] = _dot(qnb[r, t][h], knb[r, t - 1], "nt") + bias_ref[h, :, 0:B]
            p = {(r, t, h): jnp.exp(s[r, t, h] - pick(l_raw[r, t], h)) for r, t, h in full}
            for r, t, h in half:
                p[r, t, h] = jnp.where(has_next, jnp.exp(s[r, t, h] - pick(l_raw[r, t], h)), 0.0)
            dp = {(r, t, h): _dot(dob[r, t][h], jnp.concatenate([vb[r, t - 1], vb[r, t]], axis=0), "nt")
                  for r, t, h in full}
            for r, t, h in half:
                dp[r, t, h] = _dot(dob[r, t][h], vb[r, t - 1], "nt")
            ds = {(r, t, h): p[r, t, h] * (dp[r, t, h] - pick(delta[r, t], h)) for r, t, h in full + half}
            pb = {it: p[it].astype(BF16) for it in ds}
            dsb = {it: ds[it].astype(BF16) for it in ds}
            for h in heads:
                tot = None
                for r in res:
                    for t in range(tb):
                        tot = ds[r, t, h] if tot is None else tot + ds[r, t, h]
                db_ref[h] += tot
            blocks = [(r, t) for r in res for t in range(tb)]
            keys2 = {(r, t): jnp.concatenate([knb[r, t - 1], knb[r, t]], axis=0) for r, t in blocks}
            dqn = {(r, t): jnp.where(first, _dot(dsb[r, t, 0], keys2[r, t]), _dot(dsb[r, t, 1], keys2[r, t]))
                   for r, t in blocks}
            prev_half = lambda x, r, t, h: x[r, t, h][:, :B] if t < tb else x[r, t, h]
            dkn = {(r, t): sum(_dot(dsb[r, t, h][:, B:], qnb[r, t][h], "tn")
                               + _dot(prev_half(dsb, r, t + 1, h), qnb[r, t + 1][h], "tn") for h in heads)
                   for r, t in blocks}
            dvv = {(r, t): sum(_dot(pb[r, t, h][:, B:], dob[r, t][h], "tn")
                               + _dot(prev_half(pb, r, t + 1, h), dob[r, t + 1][h], "tn") for h in heads)
                   for r, t in blocks}
            for r, t in blocks:
                dqg_acc = dqg_acc + jnp.sum(dqn[r, t] * qq[r, t][0], axis=0, keepdims=True)
                dkg_acc = dkg_acc + jnp.sum(dkn[r, t] * kk[r, t][0], axis=0, keepdims=True)
            for r, t in blocks:
                rows = _block_rows(t, r, slab, d)
                dq_ref[rows, :] = _rms2_bwd(dqn[r, t], qq[r, t][0], qq[r, t][1], qg, first)
                dk_ref[rows, :] = _rms2_bwd(dkn[r, t], kk[r, t][0], kk[r, t][1], kg, first)
                dv_ref[rows, :] = dvv[r, t]

        start = (hp == 0) & (i == 0)
        fold = lambda a: a[:, :DSW_DH] + a[:, DSW_DH:]

        @pl.when(start)
        def _():
            dqg_ref[...] = fold(dqg_acc) * scale
            dkg_ref[...] = fold(dkg_acc)

        @pl.when(jnp.logical_not(start))
        def _():
            dqg_ref[...] += fold(dqg_acc) * scale
            dkg_ref[...] += fold(dkg_acc)

    def spec(rows, pick, base):
        return pl.BlockSpec((rows, LANES), lambda hp, i: (pick(i), base + hp))

    same = lambda i: i
    before = lambda i: jnp.maximum(i * tb - 1, 0)
    after = lambda i: jnp.minimum((i + 1) * tb, n_slabs - 1)
    cur, cur1 = spec(rt, same, cb), spec(rt, same, 0)
    vec = pl.BlockSpec((1, DSW_DH), lambda hp, i: (0, 0))
    vec2 = pl.BlockSpec((1, LANES), lambda hp, i: (0, 0))
    bspec = pl.BlockSpec((_HP, B, 2 * B), lambda hp, i: (hp, 0, 0))
    shp = jax.ShapeDtypeStruct((S, WT), F32)
    vshp = jax.ShapeDtypeStruct((1, DSW_DH), F32)
    carried = [] if prev_out is None else list(prev_out)
    n_in = 15
    return pl.pallas_call(
        body, name=name, grid=(_DSW_W // LANES, n_tiles),
        in_specs=[cur, spec(slab, after, cb), spec(slab, before, cb), cur, spec(slab, before, cb), cur,
                  cur1, spec(slab, after, 0), cur1, spec(slab, after, 0), cur1, spec(slab, after, 0),
                  bspec, vec2, vec2] + [pl.BlockSpec(memory_space=pl.ANY)] * len(carried),
        out_specs=[cur, cur, cur, bspec, vec, vec],
        out_shape=[shp, shp, shp, jax.ShapeDtypeStruct(bias.shape, F32), vshp, vshp],
        input_output_aliases={n_in + j: j for j in range(len(carried))},
        compiler_params=_params("arbitrary", "arbitrary"),
    )(q, q, k, k, v, v, o, o, lse, lse, do, do, bias, jnp.tile(q_gain, (1, _HP)), jnp.tile(k_gain, (1, _HP)),
      *carried)


def _t5_bucket(dist):
    max_exact = REL_BUCKETS // 2
    scaled = jnp.log(jnp.maximum(dist, 1).astype(F32) / max_exact) / math.log(REL_MAX_DIST / max_exact)
    large = jnp.minimum(max_exact + (scaled * (REL_BUCKETS - max_exact)).astype(jnp.int32), REL_BUCKETS - 1)
    return jnp.where(dist < max_exact, dist, large)


def _dsw_band():
    dist = (jnp.arange(DSW_BLK)[:, None] + DSW_BLK) - jnp.arange(2 * DSW_BLK)[None, :]
    return dist, (dist >= 0) & (dist <= DSW_BLK)


def _dsw_bias(rel_bias):
    dist, band = _dsw_band()
    out = []
    for g, (_, d) in enumerate(DSW_GROUPS):
        hot = jax.nn.one_hot(_t5_bucket(jnp.maximum(dist, 0) * d), REL_BUCKETS, dtype=F32)
        tab = jnp.einsum("qkb,bh->hqk", hot, rel_bias[:, g * DSW_HEADS:(g + 1) * DSW_HEADS],
                         precision=lax.Precision.HIGHEST)
        out.append(jnp.where(band[None], tab, NEG_BIG))
    return jnp.stack(out)


def _dsw_bucket_onehot():
    dist, band = _dsw_band()
    out = []
    for _, d in DSW_GROUPS:
        hot = jax.nn.one_hot(_t5_bucket(jnp.maximum(dist, 0) * d), LANES, dtype=BF16)
        out.append(jnp.where(band[..., None], hot, 0).reshape(-1, LANES))
    return jnp.stack(out)


def _exchange(send, *, gather, name):
    R, C = send.shape[-2:]

    def body(src_ref, dst_ref, send_sems, recv_sems, local_sem):
        x, y, c = lax.axis_index("x"), lax.axis_index("y"), lax.axis_index("c")
        me = 4 * x + 2 * y + c
        mine = pltpu.make_async_copy(src_ref if gather else src_ref.at[me], dst_ref.at[me], local_sem)
        mine.start()
        copies = []
        for rel in range(1, N_DEV):
            px = 1 - x if rel & 4 else x
            py = 1 - y if rel & 2 else y
            pc = 1 - c if rel & 1 else c
            peer = 4 * px + 2 * py + pc
            cp = pltpu.make_async_remote_copy(
                src_ref=src_ref if gather else src_ref.at[peer], dst_ref=dst_ref.at[me],
                send_sem=send_sems.at[rel - 1], recv_sem=recv_sems.at[rel - 1],
                device_id=(px, py, pc), device_id_type=pl.DeviceIdType.MESH)
            cp.start()
            copies.append(cp)
        for cp in copies:
            cp.wait()
        mine.wait()

    return pl.pallas_call(
        body, name=name,
        in_specs=[pl.BlockSpec(memory_space=pl.ANY)], out_specs=pl.BlockSpec(memory_space=pl.ANY),
        out_shape=jax.ShapeDtypeStruct((N_DEV, R, C), send.dtype),
        scratch_shapes=[pltpu.SemaphoreType.DMA((N_DEV - 1,)), pltpu.SemaphoreType.DMA((N_DEV - 1,)),
                        pltpu.SemaphoreType.DMA(())],
    )(send)


def _gather_two_level(send, *, name):
    R, C = send.shape

    def body(src_ref, dst_ref, send_sems, recv_sems, local_sem):
        x, y, c = lax.axis_index("x"), lax.axis_index("y"), lax.axis_index("c")
        me, sibling = (x, y, c), (x, y, 1 - c)
        chips = [(1 - x, y), (x, 1 - y), (1 - x, 1 - y)]

        def slot(px, py, pc):
            return dst_ref.at[4 * px + 2 * py + pc]

        def copy(k, block, to, src=None):
            return pltpu.make_async_remote_copy(
                src_ref=slot(*block) if src is None else src, dst_ref=slot(*block),
                send_sem=send_sems.at[k], recv_sem=recv_sems.at[k],
                device_id=to, device_id_type=pl.DeviceIdType.MESH)

        mine = pltpu.make_async_copy(src_ref, slot(*me), local_sem)
        mine.start()
        first = [copy(0, me, sibling, src=src_ref)]
        first += [copy(1 + j, me, (*chip, c), src=src_ref) for j, chip in enumerate(chips)]
        for cp in first:
            cp.start()
        passed = [copy(4 + j, (*chip, c), sibling) for j, chip in enumerate(chips)]
        for j, chip in enumerate(chips):
            copy(1 + j, (*chip, c), me).wait_recv()
            passed[j].start()
        copy(0, sibling, me).wait_recv()
        for j, chip in enumerate(chips):
            copy(4 + j, (*chip, 1 - c), me).wait_recv()
        for cp in first + passed:
            cp.wait_send()
        mine.wait()

    return pl.pallas_call(
        body, name=name,
        in_specs=[pl.BlockSpec(memory_space=pl.ANY)], out_specs=pl.BlockSpec(memory_space=pl.ANY),
        out_shape=jax.ShapeDtypeStruct((N_DEV, R, C), send.dtype),
        scratch_shapes=[pltpu.SemaphoreType.DMA((N_DEV - 1,)), pltpu.SemaphoreType.DMA((N_DEV - 1,)),
                        pltpu.SemaphoreType.DMA(())],
    )(send)


def _swap_with_sibling(send, *, name):
    def body(src_ref, dst_ref, send_sem, recv_sem):
        x, y, c = lax.axis_index("x"), lax.axis_index("y"), lax.axis_index("c")
        cp = pltpu.make_async_remote_copy(src_ref=src_ref, dst_ref=dst_ref, send_sem=send_sem, recv_sem=recv_sem,
                                          device_id=(x, y, 1 - c), device_id_type=pl.DeviceIdType.MESH)
        cp.start()
        cp.wait()

    return pl.pallas_call(
        body, name=name,
        in_specs=[pl.BlockSpec(memory_space=pl.ANY)], out_specs=pl.BlockSpec(memory_space=pl.ANY),
        out_shape=jax.ShapeDtypeStruct(send.shape, send.dtype),
        scratch_shapes=[pltpu.SemaphoreType.DMA(()), pltpu.SemaphoreType.DMA(())],
    )(send)


def _fill_from_sibling(buf, *, name):
    n_chips = buf.shape[0]

    def body(in_ref, out_ref, send_sems, recv_sems):
        x, y, c = lax.axis_index("x"), lax.axis_index("y"), lax.axis_index("c")
        copies = [pltpu.make_async_remote_copy(
            src_ref=in_ref.at[q, c], dst_ref=out_ref.at[q, c], send_sem=send_sems.at[q], recv_sem=recv_sems.at[q],
            device_id=(x, y, 1 - c), device_id_type=pl.DeviceIdType.MESH) for q in range(n_chips)]
        for cp in copies:
            cp.start()
        for cp in copies:
            cp.wait()

    return pl.pallas_call(
        body, name=name,
        in_specs=[pl.BlockSpec(memory_space=pl.ANY)], out_specs=pl.BlockSpec(memory_space=pl.ANY),
        out_shape=jax.ShapeDtypeStruct(buf.shape, buf.dtype), input_output_aliases={0: 0},
        scratch_shapes=[pltpu.SemaphoreType.DMA((n_chips,)), pltpu.SemaphoreType.DMA((n_chips,))],
    )(buf)


def _exchange_chips(send, *, name):
    n_chips, R, C = send.shape

    def body(src_ref, dst_ref, send_sems, recv_sems, local_sem):
        mine, copies = _chip_copies(src_ref, dst_ref, send_sems, recv_sems, local_sem)
        mine.start()
        for cp in copies:
            cp.start()
        for cp in copies:
            cp.wait()
        mine.wait()

    return pl.pallas_call(
        body, name=name,
        in_specs=[pl.BlockSpec(memory_space=pl.ANY)], out_specs=pl.BlockSpec(memory_space=pl.ANY),
        out_shape=jax.ShapeDtypeStruct(send.shape, send.dtype),
        scratch_shapes=[pltpu.SemaphoreType.DMA((n_chips - 1,)), pltpu.SemaphoreType.DMA((n_chips - 1,)),
                        pltpu.SemaphoreType.DMA(())],
    )(send)


def _add_pair(a, b, *, name):
    n, R, C = a.shape
    tr = _tile(R, 1024)

    def body(a_ref, b_ref, o_ref):
        o_ref[...] = (a_ref[...].astype(F32) + b_ref[...].astype(F32)).astype(o_ref.dtype)

    blk = pl.BlockSpec((None, tr, C), lambda k, i: (k, i, 0))
    return pl.pallas_call(
        body, name=name, grid=(n, R // tr), in_specs=[blk, blk], out_specs=blk,
        out_shape=jax.ShapeDtypeStruct(a.shape, a.dtype), compiler_params=_params("parallel", "parallel"),
    )(a, b)


_BIG = ("w_ffn_in", "w_ffn_out", "gdn_w_in", "gdn_conv", "gdn_w_out", "dsw_w_in", "dsw_w_out")
_LATE = ("gdn_w_in", "gdn_conv", "gdn_w_out")
_EARLY = tuple(n for n in _BIG if n not in _LATE)
_SHARD_AXIS = {"w_ffn_in": 2, "w_ffn_out": 1, "gdn_w_in": 2, "gdn_conv": 2, "gdn_w_out": 1, "dsw_w_in": 2,
               "dsw_w_out": 2}
_SMALL = ("b_ada", "norm_mix", "norm_ffn", "gdn_a_log", "gdn_dt_bias", "gdn_out_norm", "dsw_q_norm",
          "dsw_k_norm", "rel_bias")
_ROW_ALIGN = 16
_BIG_ALIGN = 1024


def _ceil_to(n, m):
    return -(-n // m) * m


def _seg_rows(shape):
    return _ceil_to(_ceil_to(int(np.prod(shape)), LANES) // LANES, _ROW_ALIGN)


def _pack(arrs, total_align):
    lead = arrs[0][1]
    segs = []
    for a, nlead in arrs:
        assert nlead == lead
        bshape = a.shape[:nlead]
        n = int(np.prod(a.shape[nlead:]))
        rows = _seg_rows(a.shape[nlead:])
        flat = a.reshape(bshape + (n,))
        flat = jnp.pad(flat, [(0, 0)] * nlead + [(0, rows * LANES - n)])
        segs.append(flat.reshape(bshape + (rows, LANES)))
    buf = jnp.concatenate(segs, axis=lead)
    total = _ceil_to(buf.shape[lead], total_align)
    return jnp.pad(buf, [(0, 0)] * lead + [(0, total - buf.shape[lead]), (0, 0)])


def _unpack(buf, shapes, nlead):
    out, off = [], 0
    for shp in shapes:
        n, rows = int(np.prod(shp)), _seg_rows(shp)
        seg = lax.slice_in_dim(buf, off, off + rows, axis=nlead)
        seg = seg.reshape(buf.shape[:nlead] + (rows * LANES,))[..., :n]
        out.append(seg.reshape(buf.shape[:nlead] + tuple(shp)))
        off += rows
    return out


def _to_natural(g, axis):
    n, L, r, c = g.shape
    if axis == 2:
        return jnp.transpose(g, (1, 2, 0, 3)).reshape(L, r, n * c)
    return jnp.transpose(g, (1, 0, 2, 3)).reshape(L, n * r, c)


def _to_blocked(w, axis):
    L, R, C = w.shape
    if axis == 2:
        return jnp.transpose(w.reshape(L, R, N_DEV, C // N_DEV), (2, 0, 1, 3))
    return jnp.transpose(w.reshape(L, N_DEV, R // N_DEV, C), (1, 0, 2, 3))


def _hm(a):
    lead = a.shape[:-1]
    return jnp.swapaxes(a.reshape(lead + (3, GDN_HEADS, GDN_DK)), -3, -2).reshape(lead + (3 * GDN_HEADS * GDN_DK,))


def _un_hm(a):
    lead = a.shape[:-1]
    return jnp.swapaxes(a.reshape(lead + (GDN_HEADS, 3, GDN_DK)), -3, -2).reshape(lead + (3 * GDN_HEADS * GDN_DK,))


_TILES = (1536, 1408, 1024, 768, 512, 384, 256, 128, 64, 32, 16, 8)


def _tile(n, cap):
    for t in _TILES:
        if t <= cap and n % t == 0:
            return t
    return n


def _mm_auto(a, b, mode, name, **kw):
    if mode == "tn":
        (K, M), N = a.shape, b.shape[1]
        tm, tn, tk = _tile(M, 1408), _tile(N, 1408), _tile(K, 1024)
    else:
        M, K = a.shape
        N = b.shape[1] if mode == "nn" else b.shape[0]
        tm, tn, tk = _tile(M, 512), _tile(N, 1536), _tile(K, 1408)
    return _mm(a, b, mode=mode, name=name, tm=tm, tn=tn, tk=tk, **kw)


def _row(v):
    return v.reshape(1, -1)


def _ffn_in_act(h, w_in, *, name):
    S, D = h.shape
    F = w_in.shape[1] // 2
    tm, tn = _tile(S, 512), _tile(F, 1408)
    nj = F // tn

    def body(h_ref, wg_ref, wu_ref, g_ref, u_ref, a_ref):
        hv = h_ref[...]
        gate = jnp.dot(hv, wg_ref[...], preferred_element_type=F32)
        up = jnp.dot(hv, wu_ref[...], preferred_element_type=F32)
        g_ref[...] = gate.astype(BF16)
        u_ref[...] = up.astype(BF16)
        a_ref[...] = (_silu(gate) * up).astype(BF16)

    out = pl.BlockSpec((tm, tn), lambda i, j: (i, j))
    shp = jax.ShapeDtypeStruct((S, F), BF16)
    return pl.pallas_call(
        body, name=name, grid=(S // tm, nj),
        in_specs=[pl.BlockSpec((tm, D), lambda i, j: (i, 0)), pl.BlockSpec((D, tn), lambda i, j: (0, j)),
                  pl.BlockSpec((D, tn), lambda i, j: (0, j + nj))],
        out_specs=[out, out, out], out_shape=[shp, shp, shp],
        compiler_params=_params("parallel", "parallel"),
    )(h, w_in, w_in)


def _ffn_out_dx_act(dy, w_out, gate_vec, pg, pu, *, name):
    S, D = dy.shape
    F = w_out.shape[0]
    tm, tn = _tile(S, 512), _tile(F, 1408)

    def body(dy_ref, w_ref, gv_ref, pg_ref, pu_ref, dg_ref, du_ref):
        dyg = (dy_ref[...] * gv_ref[...]).astype(BF16)
        da = lax.dot_general(dyg, w_ref[...], _DOT_DIMS["nt"], preferred_element_type=F32)
        gate = pg_ref[...].astype(F32)
        up = pu_ref[...].astype(F32)
        sg = _sigmoid(gate)
        dg_ref[...] = (da * up * (sg * (1.0 + gate * (1.0 - sg)))).astype(BF16)
        du_ref[...] = (da * (gate * sg)).astype(BF16)

    blk = pl.BlockSpec((tm, tn), lambda i, j: (i, j))
    shp = jax.ShapeDtypeStruct((S, F), BF16)
    return pl.pallas_call(
        body, name=name, grid=(S // tm, F // tn),
        in_specs=[pl.BlockSpec((tm, D), lambda i, j: (i, 0)), pl.BlockSpec((tn, D), lambda i, j: (j, 0)),
                  pl.BlockSpec((1, D), lambda i, j: (0, 0)), blk, blk],
        out_specs=[blk, blk], out_shape=[shp, shp],
        compiler_params=_params("parallel", "parallel"),
    )(dy, w_out, gate_vec, pg, pu)


def _ffn_fwd(x, mod, gain, w_in, w_out, tag):
    sh, sc, gate = mod
    h = _norm_mod_fwd(x, gain, sc, sh, name=f"ffn_norm_{tag}")
    pg, pu, a = _ffn_in_act(h, w_in, name=f"ffn_in_{tag}")
    y = _mm_auto(a, w_out, "nn", f"ffn_out_{tag}", out_scale=gate, resid=x)
    return y, (x, h, pg, pu, a)


def _ffn_bwd(dy, saved, mod, gain, w_in, w_out, tag):
    sh, sc, gate = mod
    x, h, pg, pu, a = saved
    F = pg.shape[1]
    gmat = _mm_auto(a, dy, "tn", f"ffn_out_g_{tag}")
    dw_out, dgate = _wout_grad(gmat, w_out, gate, name=f"ffn_out_dw_{tag}")
    dpg, dpu = _ffn_out_dx_act(dy, w_out, gate, pg, pu, name=f"ffn_out_dx_{tag}")
    dw_in = jnp.concatenate([_mm_auto(h, dpg, "tn", f"ffn_in_dw_gate_{tag}", out_dtype=BF16),
                             _mm_auto(h, dpu, "tn", f"ffn_in_dw_up_{tag}", out_dtype=BF16)], axis=1)
    tk = _tile(F, 1408)
    dh = _mm_sum_nt([(dpg, w_in, tk, 0), (dpu, w_in, tk, F)], name=f"ffn_in_dx_{tag}")
    dx, dsh, dsc, dgain = _norm_mod_bwd(dh, x, dy, gain, sc, name=f"ffn_norm_bwd_{tag}")
    return dx, dict(w_in=dw_in, w_out=dw_out, gain=dgain, mod=(dsh, dsc, dgate))


def _gdn_fwd(x, mod, gain, W, riding=None):
    sh, sc, gate = mod
    S = x.shape[0]
    h = _norm_mod_fwd(x, gain, sc, sh, name="gdn_norm")
    pq = _mm_auto(h, W["gdn_qkv"], "nn", "gdn_in_qkv", out_dtype=BF16)
    z = _mm_auto(h, W["gdn_z"], "nn", "gdn_in_z", out_dtype=BF16)
    ab = _mm_auto(h, W["gdn_ab"], "nn", "gdn_in_ab")
    qkvn = _gdn_prep_fwd(pq, W["gdn_conv"], name="gdn_prep")
    ab4 = jnp.transpose(ab[:, :2 * GDN_HEADS]).reshape(2 * GDN_HEADS, S // GDN_CHUNK, 1, GDN_CHUNK)
    o, states, tinvs, *rode = _gdn_chunk_fwd(qkvn, ab4, W["gdn_a_log"], W["gdn_dt_bias"], name="gdn_chunk",
                                             riding=riding)
    o2 = _gdn_outnorm_fwd(o, z, W["gdn_out_norm"], name="gdn_outnorm")
    y = _mm_auto(o2, W["gdn_out"], "nn", "gdn_out", out_scale=gate, resid=x)
    return y, (x, h, pq, z, qkvn, ab4, o, states, tinvs, o2), (rode[0] if rode else None)


def _gdn_bwd(dy, saved, mod, gain, W, riding=None):
    sh, sc, gate = mod
    x, h, pq, z, qkvn, ab4, o, states, tinvs, o2 = saved
    S = x.shape[0]
    gmat = _mm_auto(o2, dy, "tn", "gdn_out_g")
    dw_out, dgate = _wout_grad(gmat, W["gdn_out"], gate, name="gdn_out_dw")
    do2 = _mm_auto(dy, W["gdn_out"], "nt", "gdn_out_dx", a_scale=gate)
    do, dz, dout_norm = _gdn_outnorm_bwd(do2, o, z, W["gdn_out_norm"], name="gdn_outnorm_bwd")
    dqkvn, dab4, da_log, ddt_bias, *rode = _gdn_chunk_bwd(
        qkvn, ab4, W["gdn_a_log"], W["gdn_dt_bias"], states, tinvs, do, name="gdn_chunk_bwd", riding=riding)
    dc, dconv8 = _gdn_prep_bwd_pre(dqkvn, pq, W["gdn_conv"], name="gdn_prep_bwd")
    dpq = _gdn_conv_bwd_x(dc, W["gdn_conv"], name="gdn_conv_bwd")
    dab = jnp.transpose(dab4.reshape(2 * GDN_HEADS, S))
    dab = jnp.pad(dab, ((0, 0), (0, LANES - 2 * GDN_HEADS))).astype(BF16)
    dw_qkv = _mm_auto(h, dpq, "tn", "gdn_in_qkv_dw", out_dtype=BF16)
    dw_z = _mm_auto(h, dz, "tn", "gdn_in_z_dw", out_dtype=BF16)
    dw_ab = _mm_auto(h, dab, "tn", "gdn_in_ab_dw", out_dtype=BF16)
    dh = _mm_sum_nt([(dpq, W["gdn_qkv"], 1024, 0), (dz, W["gdn_z"], 1024, 0), (dab, W["gdn_ab"], LANES, 0)],
                    name="gdn_in_dx")
    dx, dsh, dsc, dgain = _norm_mod_bwd(dh, x, dy, gain, sc, name="gdn_norm_bwd")
    dw_in = jnp.concatenate([_un_hm(dw_qkv), dw_z, dw_ab[:, :2 * GDN_HEADS]], axis=1)
    return dx, dict(gdn_w_in=dw_in, gdn_conv=_un_hm(dconv8[:GDN_CONV]), gdn_w_out=dw_out, gdn_out_norm=dout_norm,
                    gdn_a_log=da_log.reshape(1, GDN_HEADS), gdn_dt_bias=ddt_bias.reshape(1, GDN_HEADS),
                    gain=dgain, mod=(dsh, dsc, dgate)), (rode[0] if rode else None)


def _dsw_fwd(x, mod, gain, W):
    sh, sc, gate = mod
    h = _norm_mod_fwd(x, gain, sc, sh, name="dsw_norm")
    q, k, v = (_mm_auto(h, W[n], "nn", f"dsw_in_{n[-1]}") for n in ("dsw_q", "dsw_k", "dsw_v"))
    outs = None
    for g in range(len(DSW_GROUPS)):
        outs = _dsw_attn_fwd(q, k, v, W["dsw_bias"][g], W["dsw_q_norm"], W["dsw_k_norm"], outs, g=g,
                             name=f"dsw_attn_{g}")
    o, lse = _dsw_merge(*outs, name="dsw_merge")
    y = _mm_auto(o, W["dsw_out"], "nn", "dsw_out", out_scale=gate, resid=x)
    return y, (x, h, q, k, v, o, lse)


def _dsw_bwd(dy, saved, mod, gain, W):
    sh, sc, gate = mod
    x, h, q, k, v, o, lse = saved
    gmat = _mm_auto(o, dy, "tn", "dsw_out_g")
    dw_out, dgate = _wout_grad(gmat, W["dsw_out"], gate, name="dsw_out_dw")
    do = _mm_auto(dy, W["dsw_out"], "nt", "dsw_out_dx", a_scale=gate)
    G = len(DSW_GROUPS)
    dqkv, dbias, dq_norm, dk_norm = None, [], 0.0, 0.0
    for g in range(G):
        *dqkv, db, dqg, dkg = _dsw_attn_bwd(q, k, v, o, lse, do, W["dsw_bias"][g], W["dsw_q_norm"],
                                            W["dsw_k_norm"], dqkv, g=g, name=f"dsw_attn_bwd_{g}")
        dbias.append(db)
        dq_norm, dk_norm = dq_norm + dqg, dk_norm + dkg
    names = ("dsw_q", "dsw_k", "dsw_v")
    dws = [_mm_auto(h, d, "tn", f"dsw_in_{n[-1]}_dw", out_dtype=BF16) for n, d in zip(names, dqkv)]
    dh = _mm_sum_nt([(d, W[n], _tile(d.shape[1], 1024), 0) for n, d in zip(names, dqkv)], name="dsw_in_dx")
    dx, dsh, dsc, dgain = _norm_mod_bwd(dh, x, dy, gain, sc, name="dsw_norm_bwd")
    hot = _dsw_bucket_onehot()
    drel = [_mm(dbias[g].reshape(DSW_HEADS, -1), hot[g], mode="nn", name=f"dsw_rel_bias_{g}", tm=DSW_HEADS,
                tn=LANES, tk=8192)[:, :REL_BUCKETS] for g in range(G)]
    return dx, dict(dsw_w_in=jnp.concatenate(dws, axis=1), dsw_w_out=dw_out, dsw_q_norm=dq_norm,
                    dsw_k_norm=dk_norm, rel_bias=jnp.transpose(jnp.concatenate(drel, axis=0)),
                    gain=dgain, mod=(dsh, dsc, dgate))


def _local_step(x, target, mod, W, late_weights=None, early_pairs=None):
    mods = [[_row(mod[l, i]) for i in range(6)] for l in range(2)]
    nmix = [_row(W["norm_mix"][l]) for l in range(2)]
    nffn = [_row(W["norm_ffn"][l]) for l in range(2)]
    x1, s_gdn, arrived = _gdn_fwd(x, mods[0][:3], nmix[0], W, None if late_weights is None else late_weights[0])
    if late_weights is not None:
        W = {**W, **late_weights[1](arrived)}
    x2, s_f0 = _ffn_fwd(x1, mods[0][3:], nffn[0], W["w_ffn_in"][0], W["w_ffn_out"][0], "0")
    x3, s_dsw = _dsw_fwd(x2, mods[1][:3], nmix[1], W)
    x4, s_f1 = _ffn_fwd(x3, mods[1][3:], nffn[1], W["w_ffn_in"][1], W["w_ffn_out"][1], "1")
    dx4, sse = _loss_head(x4, target, name="loss_head")
    dx3, g_f1 = _ffn_bwd(dx4, s_f1, mods[1][3:], nffn[1], W["w_ffn_in"][1], W["w_ffn_out"][1], "1")
    dx2, g_dsw = _dsw_bwd(dx3, s_dsw, mods[1][:3], nmix[1], W)
    dx1, g_f0 = _ffn_bwd(dx2, s_f0, mods[0][3:], nffn[0], W["w_ffn_in"][0], W["w_ffn_out"][0], "0")
    grads = dict(
        w_ffn_in=jnp.stack([g_f0["w_in"], g_f1["w_in"]]), w_ffn_out=jnp.stack([g_f0["w_out"], g_f1["w_out"]]),
        dsw_w_in=g_dsw["dsw_w_in"][None], dsw_w_out=g_dsw["dsw_w_out"][None])
    riding = None if early_pairs is None else early_pairs(grads)
    dx0, g_gdn, rode = _gdn_bwd(dx1, s_gdn, mods[0][:3], nmix[0], W, riding)
    dmod = jnp.stack([jnp.concatenate(list(g_gdn["mod"]) + list(g_f0["mod"]), axis=0),
                      jnp.concatenate(list(g_dsw["mod"]) + list(g_f1["mod"]), axis=0)])
    grads.update(
        norm_mix=jnp.concatenate([g_gdn["gain"], g_dsw["gain"]], axis=0),
        norm_ffn=jnp.concatenate([g_f0["gain"], g_f1["gain"]], axis=0),
        gdn_w_in=g_gdn["gdn_w_in"][None], gdn_conv=g_gdn["gdn_conv"][None], gdn_w_out=g_gdn["gdn_w_out"][None],
        gdn_out_norm=g_gdn["gdn_out_norm"], gdn_a_log=g_gdn["gdn_a_log"], gdn_dt_bias=g_gdn["gdn_dt_bias"],
        dsw_q_norm=g_dsw["dsw_q_norm"], dsw_k_norm=g_dsw["dsw_k_norm"], rel_bias=g_dsw["rel_bias"])
    return sse, dx0, grads, dmod, rode


def _prepare_first(full, small):
    gw = full["gdn_w_in"][0]
    hk3 = 3 * GDN_HEADS * GDN_DK
    return dict(
        gdn_qkv=_hm(gw[:, :hk3]), gdn_z=gw[:, hk3:hk3 + GDN_HEADS * GDN_DK],
        gdn_ab=jnp.pad(gw[:, hk3 + GDN_HEADS * GDN_DK:], ((0, 0), (0, LANES - 2 * GDN_HEADS))),
        gdn_conv=_hm(full["gdn_conv"][0]), gdn_out=full["gdn_w_out"][0],
        norm_mix=small["norm_mix"], norm_ffn=small["norm_ffn"],
        gdn_a_log=small["gdn_a_log"].reshape(GDN_HEADS, 1, 1), gdn_dt_bias=small["gdn_dt_bias"].reshape(GDN_HEADS, 1, 1),
        gdn_out_norm=small["gdn_out_norm"], dsw_q_norm=small["dsw_q_norm"], dsw_k_norm=small["dsw_k_norm"],
        dsw_bias=_dsw_bias(small["rel_bias"]))


def _prepare_rest(full):
    di = full["dsw_w_in"][0]
    dq = di.shape[1] // 3
    return dict(w_ffn_in=full["w_ffn_in"], w_ffn_out=full["w_ffn_out"],
                dsw_q=di[:, :dq], dsw_k=di[:, dq:2 * dq], dsw_v=di[:, 2 * dq:], dsw_out=full["dsw_w_out"][0])


def _prepare_weights(full, small):
    return {**_prepare_first(full, small), **_prepare_rest(full)}


_W_NAMES = ("w_ada", "b_ada", "norm_mix", "norm_ffn", "w_ffn_in", "w_ffn_out", "gdn_w_in", "gdn_conv",
            "gdn_a_log", "gdn_dt_bias", "gdn_out_norm", "gdn_w_out", "dsw_w_in", "dsw_q_norm", "dsw_k_norm",
            "dsw_w_out", "rel_bias")
_PAD_BATCH = 16


def _pad_rows(a, rows):
    return jnp.pad(a, ((0, rows - a.shape[0]), (0, 0)))


def kernel(x, c, w_ada, b_ada, norm_mix, norm_ffn, w_ffn_in, w_ffn_out, gdn_w_in, gdn_conv, gdn_a_log, gdn_dt_bias, gdn_out_norm, gdn_w_out, dsw_w_in, dsw_q_norm, dsw_k_norm, dsw_w_out, rel_bias, loss_target, m_w_ada, m_b_ada, m_norm_mix, m_norm_ffn, m_w_ffn_in, m_w_ffn_out, m_gdn_w_in, m_gdn_conv, m_gdn_a_log, m_gdn_dt_bias, m_gdn_out_norm, m_gdn_w_out, m_dsw_w_in, m_dsw_q_norm, m_dsw_k_norm, m_dsw_w_out, m_rel_bias, v_w_ada, v_b_ada, v_norm_mix, v_norm_ffn, v_w_ffn_in, v_w_ffn_out, v_gdn_w_in, v_gdn_conv, v_gdn_a_log, v_gdn_dt_bias, v_gdn_out_norm, v_gdn_w_out, v_dsw_w_in, v_dsw_q_norm, v_dsw_k_norm, v_dsw_w_out, v_rel_bias):
    w = dict(zip(_W_NAMES, (w_ada, b_ada, norm_mix, norm_ffn, w_ffn_in, w_ffn_out, gdn_w_in, gdn_conv, gdn_a_log,
                            gdn_dt_bias, gdn_out_norm, gdn_w_out, dsw_w_in, dsw_q_norm, dsw_k_norm, dsw_w_out,
                            rel_bias)))
    m = dict(zip(_W_NAMES, (m_w_ada, m_b_ada, m_norm_mix, m_norm_ffn, m_w_ffn_in, m_w_ffn_out, m_gdn_w_in,
                            m_gdn_conv, m_gdn_a_log, m_gdn_dt_bias, m_gdn_out_norm, m_gdn_w_out, m_dsw_w_in,
                            m_dsw_q_norm, m_dsw_k_norm, m_dsw_w_out, m_rel_bias)))
    v = dict(zip(_W_NAMES, (v_w_ada, v_b_ada, v_norm_mix, v_norm_ffn, v_w_ffn_in, v_w_ffn_out, v_gdn_w_in,
                            v_gdn_conv, v_gdn_a_log, v_gdn_dt_bias, v_gdn_out_norm, v_gdn_w_out, v_dsw_w_in,
                            v_dsw_q_norm, v_dsw_k_norm, v_dsw_w_out, v_rel_bias)))
    D = x.shape[-1]
    n_layers, _, ada_cols = w_ada.shape

    c_all = _exchange(c.reshape(D // LANES, LANES), gather=True, name="gather_cond").reshape(N_DEV, D)
    c_pad = _pad_rows(c_all, _PAD_BATCH)
    proj = [_mm(c_pad, w_ada[l], mode="nn", name=f"ada_proj_{l}", tm=_PAD_BATCH, tn=ada_cols, tk=D, a_silu=True)
            for l in range(n_layers)]
    mod_send = _pack([(jnp.stack([p[:N_DEV] for p in proj], axis=1), 1)], _ROW_ALIGN)
    mod_recv = _exchange(mod_send, gather=False, name="scatter_mod")
    mod = _unpack(mod_recv, [(n_layers, ada_cols)], 1)[0]
    mod = jnp.transpose(mod, (1, 0, 2)).reshape(n_layers, N_DEV * ada_cols) + b_ada
    mod = mod.reshape(n_layers, 6, D)

    conv_hi = gdn_conv.astype(BF16)
    conv_lo = (gdn_conv - conv_hi.astype(F32)).astype(BF16)
    first_send = _pack([(conv_hi if n == "gdn_conv" else w[n].astype(BF16), 0) for n in _LATE] + [(conv_lo, 0)],
                       _ROW_ALIGN)
    parts = _unpack(_gather_two_level(first_send, name="gather_weights_first"),
                    [w[n].shape for n in _LATE] + [gdn_conv.shape], 1)
    full = {n: _to_natural(parts[i], _SHARD_AXIS[n]) for i, n in enumerate(_LATE)}
    full["gdn_conv"] = full["gdn_conv"].astype(F32) + _to_natural(parts[-1], _SHARD_AXIS["gdn_conv"]).astype(F32)
    W = _prepare_first(full, {n: w[n] for n in _SMALL})
    rest_send = _pack([(w[n].astype(BF16), 0) for n in _EARLY], _ROW_ALIGN)

    def rest_weights(arrived):
        w_all = _fill_from_sibling(arrived, name="swap_weights").reshape((N_DEV,) + rest_send.shape)
        blocks = _unpack(w_all, [w[n].shape for n in _EARLY], 1)
        return _prepare_rest({n: _to_natural(blocks[i], _SHARD_AXIS[n]) for i, n in enumerate(_EARLY)})

    my_c = lax.axis_index("c")

    def pair_sums(g, names, tag):
        send = _pack([(_to_blocked(g[n].astype(BF16), _SHARD_AXIS[n]), 1) for n in names], _BIG_ALIGN)
        by_core = send.reshape((N_DEV // 2, 2) + send.shape[1:])
        keep = lax.dynamic_index_in_dim(by_core, my_c, axis=1, keepdims=False)
        give = lax.dynamic_index_in_dim(by_core, 1 - my_c, axis=1, keepdims=False)
        return _add_pair(keep, _swap_with_sibling(give, name=f"swap_grads_{tag}"), name=f"add_sibling_grads_{tag}")

    sse, grad_x, grads, dmod, early_recv = _local_step(
        x[0], loss_target[0], mod, W, late_weights=(rest_send, rest_weights),
        early_pairs=lambda g: pair_sums(g, _EARLY, "early"))
    loss = lax.psum(0.5 * sse[0, 0] / D, ("x", "y", "c"))
    grads["b_ada"] = dmod.reshape(n_layers, 6 * D)
    late_recv = _exchange_chips(pair_sums(grads, _LATE, "late"), name="scatter_grads_late")
    g_parts = dict(zip(_EARLY, _unpack(early_recv, [w[n].shape for n in _EARLY], 1)))
    g_parts.update(zip(_LATE, _unpack(late_recv, [w[n].shape for n in _LATE], 1)))

    dmod_send = _pack([(jnp.transpose(dmod.reshape(n_layers, N_DEV, ada_cols), (1, 0, 2)), 1)], _ROW_ALIGN)
    small_send = _pack([(grads[n].reshape(w[n].shape), 0) for n in _SMALL], _ROW_ALIGN)
    s_recv = _exchange(jnp.concatenate(
        [dmod_send, jnp.broadcast_to(small_send[None], (N_DEV,) + small_send.shape)], axis=1),
        gather=False, name="scatter_small")
    dmod_rows = dmod_send.shape[1]

    out = {}
    kinds = ("grad", "delta", "new_m", "new_v")
    for n in _BIG:
        g4 = g_parts[n]
        rows2d = lambda a: a.reshape((-1, w[n].shape[-1]))
        res = _adamw(rows2d(w[n]), g4.reshape((g4.shape[0], -1, w[n].shape[-1])), rows2d(m[n]), rows2d(v[n]),
                     name=f"adamw_{n}")
        for kind, buf in zip(kinds, res):
            out[kind, n] = buf.reshape(w[n].shape)

    dmod_all = _unpack(lax.slice_in_dim(s_recv, 0, dmod_rows, axis=1), [(n_layers, ada_cols)], 1)[0]
    g_ada = jnp.stack([_mm(c_pad, _pad_rows(dmod_all[:, l], _PAD_BATCH), mode="tn", name=f"ada_dw_{l}",
                           tm=D, tn=ada_cols, tk=_PAD_BATCH, a_silu=True) for l in range(n_layers)])
    flat = lambda a: a.reshape(n_layers * D, ada_cols)
    res = _adamw(flat(w_ada), flat(g_ada)[None], flat(m_w_ada), flat(v_w_ada), name="adamw_ada")
    for kind, buf in zip(("grad", "delta", "new_m", "new_v"), res):
        out[kind, "w_ada"] = buf.reshape(w_ada.shape)

    small_parts = lax.slice_in_dim(s_recv, dmod_rows, s_recv.shape[1], axis=1)
    packed = [_pack([(t[n], 0) for n in _SMALL], _ROW_ALIGN) for t in (w, m, v)]
    res = _adamw(packed[0], small_parts, packed[1], packed[2], name="adamw_replicated")
    for kind, buf in zip(("grad", "delta", "new_m", "new_v"), res):
        for n, a in zip(_SMALL, _unpack(buf, [w[n].shape for n in _SMALL], 0)):
            out[kind, n] = a

    return (loss, grad_x[None]) + tuple(out[kind, n] for kind in ("grad", "delta", "new_m", "new_v")
                                        for n in _W_NAMES)
```

```python
import functools
import math

import numpy as np
import jax
import jax.numpy as jnp
from jax import lax
from jax.experimental import pallas as pl
from jax.experimental.pallas import tpu as pltpu

F32 = jnp.float32
BF16 = jnp.bfloat16

N_DEV = 8
RMS_EPS = 1e-6
LANES = 128
V7X_VMEM_LIMIT = 48 * 1024 * 1024

GDN_HEADS = 8
GDN_DK = 128
GDN_CHUNK = 64
GDN_CONV = 4
DSW_GROUPS = ((128, 1), (512, 4), (2048, 16))
DSW_HEADS = 8
DSW_DH = 64
DSW_BLK = 128
REL_BUCKETS = 32
REL_MAX_DIST = 2048

ADAM_LR = 0.001
ADAM_B1 = 0.9
ADAM_B2 = 0.999
ADAM_EPS = 1e-08
ADAM_WD = 0.01
ADAM_STEP = 10

NEG_BIG = -1e30


def _params(*sem):
    return pltpu.CompilerParams(dimension_semantics=sem, vmem_limit_bytes=V7X_VMEM_LIMIT)


def _sigmoid(x):
    return 1.0 / (1.0 + jnp.exp(-x))


def _silu(x):
    return x * _sigmoid(x)


_DOT_DIMS = {
    "nn": (((1,), (0,)), ((), ())),
    "nt": (((1,), (1,)), ((), ())),
    "tn": (((0,), (0,)), ((), ())),
}


def _mm(a, b, *, mode, name, tm, tn, tk, out_dtype=F32, a_scale=None, out_scale=None, resid=None, a_silu=False):
    if mode == "nn":
        (M, K), N = a.shape, b.shape[1]
    elif mode == "nt":
        (M, K), N = a.shape, b.shape[0]
    else:
        (K, M), N = a.shape, b.shape[1]
    tm, tn, tk = min(tm, M), min(tn, N), min(tk, K)
    assert M % tm == 0 and N % tn == 0 and K % tk == 0, (name, M, N, K, tm, tn, tk)
    nk = K // tk

    def body(*refs):
        refs = list(refs)
        a_ref, b_ref = refs.pop(0), refs.pop(0)
        as_ref = refs.pop(0) if a_scale is not None else None
        os_ref = refs.pop(0) if out_scale is not None else None
        r_ref = refs.pop(0) if resid is not None else None
        o_ref = refs.pop(0)
        acc_ref = refs.pop(0) if nk > 1 else None

        av = a_ref[...]
        if a_silu:
            av = _silu(av.astype(F32))
        if as_ref is not None:
            av = av.astype(F32) * as_ref[...]
        part = lax.dot_general(av.astype(BF16), b_ref[...].astype(BF16), _DOT_DIMS[mode],
                               preferred_element_type=F32)

        def finish(r):
            if os_ref is not None:
                r = r * os_ref[...]
            if r_ref is not None:
                r = r + r_ref[...].astype(F32)
            o_ref[...] = r.astype(out_dtype)

        if nk == 1:
            finish(part)
        else:
            k = pl.program_id(2)

            @pl.when(k == 0)
            def _():
                acc_ref[...] = part

            @pl.when(k > 0)
            def _():
                acc_ref[...] += part

            @pl.when(k == nk - 1)
            def _():
                finish(acc_ref[...])

    if mode == "nn":
        a_spec = pl.BlockSpec((tm, tk), lambda i, j, k: (i, k))
        b_spec = pl.BlockSpec((tk, tn), lambda i, j, k: (k, j))
        as_spec = pl.BlockSpec((1, tk), lambda i, j, k: (0, k))
    elif mode == "nt":
        a_spec = pl.BlockSpec((tm, tk), lambda i, j, k: (i, k))
        b_spec = pl.BlockSpec((tn, tk), lambda i, j, k: (j, k))
        as_spec = pl.BlockSpec((1, tk), lambda i, j, k: (0, k))
    else:
        a_spec = pl.BlockSpec((tk, tm), lambda i, j, k: (k, i))
        b_spec = pl.BlockSpec((tk, tn), lambda i, j, k: (k, j))
        as_spec = None
    in_specs, args = [a_spec, b_spec], [a, b]
    if a_scale is not None:
        in_specs.append(as_spec)
        args.append(a_scale)
    if out_scale is not None:
        in_specs.append(pl.BlockSpec((1, tn), lambda i, j, k: (0, j)))
        args.append(out_scale)
    if resid is not None:
        in_specs.append(pl.BlockSpec((tm, tn), lambda i, j, k: (i, j)))
        args.append(resid)
    return pl.pallas_call(
        body, name=name, grid=(M // tm, N // tn, nk),
        in_specs=in_specs, out_specs=pl.BlockSpec((tm, tn), lambda i, j, k: (i, j)),
        out_shape=jax.ShapeDtypeStruct((M, N), out_dtype),
        scratch_shapes=[pltpu.VMEM((tm, tn), F32)] if nk > 1 else [],
        compiler_params=_params("parallel", "parallel", "arbitrary"),
    )(*args)


def _mm_sum_nt(pairs, *, name, tm=512, tn=1024):
    M, N = pairs[0][0].shape[0], pairs[0][1].shape[0]
    tm, tn = _tile(M, tm), _tile(N, tn)
    spans, start = [], 0
    for a, b, tk, off in pairs:
        K = a.shape[1]
        assert a.shape[0] == M and b.shape[0] == N and K % tk == 0 and off % tk == 0, name
        spans.append((start, K // tk, tk, off // tk))
        start += K // tk
    total = start

    def body(*refs):
        o_ref, acc_ref = refs[-2:]
        k = pl.program_id(2)

        @pl.when(k == 0)
        def _():
            acc_ref[...] = jnp.zeros_like(acc_ref)

        for p, (s0, nk, _, _) in enumerate(spans):
            a_ref, b_ref = refs[2 * p], refs[2 * p + 1]

            @pl.when((k >= s0) & (k < s0 + nk))
            def _():
                acc_ref[...] += lax.dot_general(a_ref[...].astype(BF16), b_ref[...].astype(BF16), _DOT_DIMS["nt"],
                                                preferred_element_type=F32)

        @pl.when(k == total - 1)
        def _():
            o_ref[...] = acc_ref[...]

    def spec(rows, tk, s0, nk, koff, axis):
        def index(i, j, k):
            return ((i, j)[axis], jnp.clip(k - s0, 0, nk - 1) + koff)
        return pl.BlockSpec((rows, tk), index)

    in_specs, args = [], []
    for (a, b, _, _), (s0, nk, tk, koff) in zip(pairs, spans):
        in_specs += [spec(tm, tk, s0, nk, 0, 0), spec(tn, tk, s0, nk, koff, 1)]
        args += [a, b]
    return pl.pallas_call(
        body, name=name, grid=(M // tm, N // tn, total), in_specs=in_specs,
        out_specs=pl.BlockSpec((tm, tn), lambda i, j, k: (i, j)),
        out_shape=jax.ShapeDtypeStruct((M, N), F32), scratch_shapes=[pltpu.VMEM((tm, tn), F32)],
        compiler_params=_params("parallel", "parallel", "arbitrary"),
    )(*args)


def _norm_mod_fwd(x, gain, sc, sh, *, name):
    S, D = x.shape
    tr = min(512, S)

    def body(x_ref, g_ref, sc_ref, sh_ref, h_ref):
        xv = x_ref[...]
        r = lax.rsqrt(jnp.mean(xv * xv, axis=-1, keepdims=True) + RMS_EPS)
        h_ref[...] = ((xv * r) * g_ref[...] * (1.0 + sc_ref[...]) + sh_ref[...]).astype(BF16)

    row = pl.BlockSpec((tr, D), lambda i: (i, 0))
    vec = pl.BlockSpec((1, D), lambda i: (0, 0))
    return pl.pallas_call(
        body, name=name, grid=(S // tr,), in_specs=[row, vec, vec, vec], out_specs=row,
        out_shape=jax.ShapeDtypeStruct((S, D), BF16), compiler_params=_params("parallel"),
    )(x, gain, sc, sh)


def _norm_mod_bwd(dh, x, dx_res, gain, sc, *, name):
    S, D = x.shape
    tr = min(256, S)
    n_steps = S // tr

    def body(dh_ref, x_ref, dxr_ref, g_ref, sc_ref, dx_ref, dsh_ref, dsc_ref, dgain_ref, acc_sh, acc_a):
        i = pl.program_id(0)
        xv = x_ref[...]
        r = lax.rsqrt(jnp.mean(xv * xv, axis=-1, keepdims=True) + RMS_EPS)
        n = xv * r
        dhv = dh_ref[...].astype(F32)
        dn = dhv * (g_ref[...] * (1.0 + sc_ref[...]))
        dx_ref[...] = dxr_ref[...] + r * (dn - n * jnp.mean(dn * n, axis=-1, keepdims=True))
        p_sh = jnp.sum(dhv, axis=0, keepdims=True)
        p_a = jnp.sum(dhv * n, axis=0, keepdims=True)

        @pl.when(i == 0)
        def _():
            acc_sh[...] = p_sh
            acc_a[...] = p_a

        @pl.when(i > 0)
        def _():
            acc_sh[...] += p_sh
            acc_a[...] += p_a

        @pl.when(i == n_steps - 1)
        def _():
            dsh_ref[...] = acc_sh[...]
            dsc_ref[...] = acc_a[...] * g_ref[...]
            dgain_ref[...] = acc_a[...] * (1.0 + sc_ref[...])

    row = pl.BlockSpec((tr, D), lambda i: (i, 0))
    vec = pl.BlockSpec((1, D), lambda i: (0, 0))
    vshape = jax.ShapeDtypeStruct((1, D), F32)
    return pl.pallas_call(
        body, name=name, grid=(n_steps,), in_specs=[row, row, row, vec, vec],
        out_specs=[row, vec, vec, vec],
        out_shape=[jax.ShapeDtypeStruct((S, D), F32), vshape, vshape, vshape],
        scratch_shapes=[pltpu.VMEM((1, D), F32), pltpu.VMEM((1, D), F32)],
        compiler_params=_params("arbitrary"),
    )(dh, x, dx_res, gain, sc)


def _wout_grad(gmat, w, gate, *, name):
    K, D = w.shape
    tr = min(256, K)
    n_steps = K // tr

    def body(g_ref, w_ref, gate_ref, dw_ref, dgate_ref, acc):
        i = pl.program_id(0)
        gv = g_ref[...]
        dw_ref[...] = (gv * gate_ref[...]).astype(BF16)
        part = jnp.sum(gv * w_ref[...], axis=0, keepdims=True)

        @pl.when(i == 0)
        def _():
            acc[...] = part

        @pl.when(i > 0)
        def _():
            acc[...] += part

        @pl.when(i == n_steps - 1)
        def _():
            dgate_ref[...] = acc[...]

    row = pl.BlockSpec((tr, D), lambda i: (i, 0))
    vec = pl.BlockSpec((1, D), lambda i: (0, 0))
    return pl.pallas_call(
        body, name=name, grid=(n_steps,), in_specs=[row, row, vec], out_specs=[row, vec],
        out_shape=[jax.ShapeDtypeStruct((K, D), BF16), jax.ShapeDtypeStruct((1, D), F32)],
        scratch_shapes=[pltpu.VMEM((1, D), F32)], compiler_params=_params("arbitrary"),
    )(gmat, w, gate)


def _loss_head(y, target, *, name):
    S, D = y.shape
    tr = min(512, S)
    n_steps = S // tr

    def body(y_ref, t_ref, dy_ref, sse_ref, acc):
        i = pl.program_id(0)
        e = y_ref[...] - t_ref[...]
        dy_ref[...] = e * (1.0 / D)
        part = jnp.sum(e * e, axis=0, keepdims=True)

        @pl.when(i == 0)
        def _():
            acc[...] = part

        @pl.when(i > 0)
        def _():
            acc[...] += part

        @pl.when(i == n_steps - 1)
        def _():
            sse_ref[...] = jnp.sum(acc[...], axis=1, keepdims=True)

    row = pl.BlockSpec((tr, D), lambda i: (i, 0))
    return pl.pallas_call(
        body, name=name, grid=(n_steps,), in_specs=[row, row],
        out_specs=[row, pl.BlockSpec((1, 1), lambda i: (0, 0))],
        out_shape=[jax.ShapeDtypeStruct((S, D), F32), jax.ShapeDtypeStruct((1, 1), F32)],
        scratch_shapes=[pltpu.VMEM((1, D), F32)], compiler_params=_params("arbitrary"),
    )(y, target)


def _adamw(w, g_parts, m, v, *, name):
    R, C = w.shape
    P = g_parts.shape[0]
    tr = _tile(R, max(8, 1024 * LANES // C))
    c1 = 1.0 / (1.0 - ADAM_B1 ** ADAM_STEP)
    c2 = 1.0 / (1.0 - ADAM_B2 ** ADAM_STEP)

    def body(w_ref, g_ref, m_ref, v_ref, go_ref, d_ref, mo_ref, vo_ref):
        g = g_ref[0].astype(F32)
        for q in range(1, P):
            g = g + g_ref[q].astype(F32)
        mn = ADAM_B1 * m_ref[...] + (1.0 - ADAM_B1) * g
        vn = ADAM_B2 * v_ref[...] + (1.0 - ADAM_B2) * (g * g)
        go_ref[...] = g
        mo_ref[...] = mn
        vo_ref[...] = vn
        d_ref[...] = -ADAM_LR * ((mn * c1) / (jnp.sqrt(vn * c2) + ADAM_EPS) + ADAM_WD * w_ref[...])

    row = pl.BlockSpec((tr, C), lambda i: (i, 0))
    shp = jax.ShapeDtypeStruct((R, C), F32)
    return pl.pallas_call(
        body, name=name, grid=(R // tr,),
        in_specs=[row, pl.BlockSpec((P, tr, C), lambda i: (0, i, 0)), row, row],
        out_specs=[row, row, row, row], out_shape=[shp, shp, shp, shp],
        compiler_params=_params("parallel"),
    )(w, g_parts, m, v)


_HALO = 16


def _conv_taps(buf, w_ref, rows, cols):
    acc = None
    for j in range(GDN_CONV):
        term = buf[pl.ds(_HALO - (GDN_CONV - 1) + j, rows), cols] * w_ref[j:j + 1, cols]
        acc = term if acc is None else acc + term
    return acc


def _fill_conv_buf(buf, halo_ref, x_ref, rows, first):
    buf[0:_HALO, :] = jnp.where(first, 0.0, halo_ref[...].astype(F32))
    buf[_HALO:_HALO + rows, :] = x_ref[...].astype(F32)


_HM = 3 * GDN_DK
_GDN_ROWS = 256
_PREP_HEADS = 4


def _l2n(seg):
    return lax.rsqrt(jnp.sum(seg * seg, axis=-1, keepdims=True) + RMS_EPS)


def _head_cols(hh):
    return slice(hh * _HM, (hh + 1) * _HM)


def _gdn_prep_fwd(x, conv_w, *, name):
    S, C3 = x.shape
    CB = _PREP_HEADS * _HM
    RB = min(256, S)

    def body(x_ref, halo_ref, w_ref, o_ref, buf):
        i = pl.program_id(0)
        _fill_conv_buf(buf, halo_ref, x_ref, RB, i == 0)
        for hh in range(_PREP_HEADS):
            c0 = hh * _HM
            y = _silu(_conv_taps(buf, w_ref, RB, _head_cols(hh)))
            q, k = y[:, :GDN_DK], y[:, GDN_DK:2 * GDN_DK]
            o_ref[:, c0:c0 + GDN_DK] = q * (_l2n(q) * GDN_DK ** -0.5)
            o_ref[:, c0 + GDN_DK:c0 + 2 * GDN_DK] = k * _l2n(k)
            o_ref[:, c0 + 2 * GDN_DK:c0 + _HM] = y[:, 2 * GDN_DK:]

    hb = RB // _HALO
    return pl.pallas_call(
        body, name=name, grid=(S // RB, C3 // CB),
        in_specs=[pl.BlockSpec((RB, CB), lambda i, j: (i, j)),
                  pl.BlockSpec((_HALO, CB), lambda i, j: (jnp.maximum(i * hb - 1, 0), j)),
                  pl.BlockSpec((GDN_CONV, CB), lambda i, j: (0, j))],
        out_specs=pl.BlockSpec((RB, CB), lambda i, j: (i, j)),
        out_shape=jax.ShapeDtypeStruct((S, C3), F32),
        scratch_shapes=[pltpu.VMEM((RB + _HALO, CB), F32)],
        compiler_params=_params("parallel", "parallel"),
    )(x, x, conv_w)


def _gdn_prep_bwd_pre(dn, x, conv_w, *, name):
    S, C3 = x.shape
    CB = _PREP_HEADS * _HM
    RB = min(256, S)
    n_steps = S // RB

    def body(dn_ref, x_ref, halo_ref, w_ref, dc_ref, dw_ref, buf):
        i = pl.program_id(1)
        _fill_conv_buf(buf, halo_ref, x_ref, RB, i == 0)
        head_parts = []
        for hh in range(_PREP_HEADS):
            c0, cols = hh * _HM, _head_cols(hh)
            acc = _conv_taps(buf, w_ref, RB, cols)
            sg = _sigmoid(acc)
            y = acc * sg
            dsilu = sg * (1.0 + acc * (1.0 - sg))
            for part, scale in ((0, GDN_DK ** -0.5), (1, 1.0)):
                sl = slice(part * GDN_DK, (part + 1) * GDN_DK)
                seg = y[:, sl]
                r = _l2n(seg)
                n = seg * r
                d = dn_ref[:, c0 + part * GDN_DK:c0 + (part + 1) * GDN_DK] * scale
                dc_ref[:, c0 + part * GDN_DK:c0 + (part + 1) * GDN_DK] = (
                    r * (d - n * jnp.sum(d * n, axis=-1, keepdims=True)) * dsilu[:, sl])
            dc_ref[:, c0 + 2 * GDN_DK:c0 + _HM] = dn_ref[:, c0 + 2 * GDN_DK:c0 + _HM] * dsilu[:, 2 * GDN_DK:]
            dc = dc_ref[:, cols]
            taps = [jnp.sum(dc * buf[pl.ds(_HALO - (GDN_CONV - 1) + t, RB), cols], axis=0, keepdims=True)
                    for t in range(GDN_CONV)]
            head_parts.append(jnp.concatenate(taps + [jnp.zeros((8 - GDN_CONV, _HM), F32)], axis=0))
        part = jnp.concatenate(head_parts, axis=1)

        @pl.when(i == 0)
        def _():
            dw_ref[...] = part

        @pl.when(i > 0)
        def _():
            dw_ref[...] += part

    hb = RB // _HALO
    return pl.pallas_call(
        body, name=name, grid=(C3 // CB, n_steps),
        in_specs=[pl.BlockSpec((RB, CB), lambda j, i: (i, j)),
                  pl.BlockSpec((RB, CB), lambda j, i: (i, j)),
                  pl.BlockSpec((_HALO, CB), lambda j, i: (jnp.maximum(i * hb - 1, 0), j)),
                  pl.BlockSpec((GDN_CONV, CB), lambda j, i: (0, j))],
        out_specs=[pl.BlockSpec((RB, CB), lambda j, i: (i, j)),
                   pl.BlockSpec((8, CB), lambda j, i: (0, j))],
        out_shape=[jax.ShapeDtypeStruct((S, C3), F32), jax.ShapeDtypeStruct((8, C3), F32)],
        scratch_shapes=[pltpu.VMEM((RB + _HALO, CB), F32)],
        compiler_params=_params("parallel", "arbitrary"),
    )(dn, x, x, conv_w)


def _gdn_conv_bwd_x(dc, conv_w, *, name):
    S, C3 = dc.shape
    CB = _PREP_HEADS * _HM
    RB = min(256, S)
    n_steps = S // RB

    def body(dc_ref, halo_ref, w_ref, dx_ref, buf):
        i = pl.program_id(0)
        buf[0:RB, :] = dc_ref[...]
        buf[RB:RB + _HALO, :] = jnp.where(i == n_steps - 1, 0.0, halo_ref[...])
        for hh in range(_PREP_HEADS):
            cols = _head_cols(hh)
            acc = None
            for j in range(GDN_CONV):
                term = buf[pl.ds(GDN_CONV - 1 - j, RB), cols] * w_ref[j:j + 1, cols]
                acc = term if acc is None else acc + term
            dx_ref[:, cols] = acc.astype(BF16)

    hb = RB // _HALO
    last = S // _HALO - 1
    return pl.pallas_call(
        body, name=name, grid=(n_steps, C3 // CB),
        in_specs=[pl.BlockSpec((RB, CB), lambda i, j: (i, j)),
                  pl.BlockSpec((_HALO, CB), lambda i, j: (jnp.minimum((i + 1) * hb, last), j)),
                  pl.BlockSpec((GDN_CONV, CB), lambda i, j: (0, j))],
        out_specs=pl.BlockSpec((RB, CB), lambda i, j: (i, j)),
        out_shape=jax.ShapeDtypeStruct((S, C3), BF16),
        scratch_shapes=[pltpu.VMEM((RB + _HALO, CB), F32)],
        compiler_params=_params("parallel", "parallel"),
    )(dc, dc, conv_w)


def _split_bf16(a):
    hi = a.astype(BF16)
    return hi, (a - hi.astype(F32)).astype(BF16)


def _dot(a, b, dims="nn", exact=False):
    def dot(p, q):
        return lax.dot_general(p, q, _DOT_DIMS[dims], preferred_element_type=F32)

    if exact:
        (ah, al), (bh, bl) = _split_bf16(a), _split_bf16(b)
        return dot(ah, bh) + (dot(ah, bl) + dot(al, bh))
    return dot(a.astype(BF16), b.astype(BF16))


def _softplus(x):
    return jnp.maximum(x, 0.0) + jnp.log(1.0 + jnp.exp(-jnp.abs(x)))


def _to_col(row, eye):
    return jnp.sum(jnp.where(eye, row, 0.0), axis=1, keepdims=True)


def _to_row(col, eye):
    return jnp.sum(jnp.where(eye, col, 0.0), axis=0, keepdims=True)


def _unit_lower_inverse(low, ri, ci):
    n = range(len(low))
    C = low[0].shape[0]
    eye = jnp.where(ri == ci, 1.0, 0.0)
    pair = (ri >> 1) == (ci >> 1)
    x = [eye - jnp.where(pair, low[j], 0.0) for j in n]
    m, sh = 2, 1
    while m < C:
        join = ((ri >> (sh + 1)) == (ci >> (sh + 1))) & (((ri >> sh) & 1) == 1) & (((ci >> sh) & 1) == 0)
        y = [_dot(x[j], jnp.where(join, low[j], 0.0)) for j in n]
        x = [x[j] - _dot(y[j], x[j]) for j in n]
        m, sh = 2 * m, sh + 1
    lx = [_dot(low[j], x[j], exact=True) for j in n]
    corr = [_dot(x[j], eye - x[j] - lx[j]) for j in n]
    return [x[j] + corr[j] for j in n]


def _gdn_local_batch(qkv, g_row, beta_row, ri, ci):
    n = range(len(qkv))
    eye, tril, strict = ri == ci, ri >= ci, ri > ci
    q = [qkv[j][:, :GDN_DK] for j in n]
    k = [qkv[j][:, GDN_DK:2 * GDN_DK] for j in n]
    v = [qkv[j][:, 2 * GDN_DK:] for j in n]
    g_col = [_to_col(g_row[j], eye) for j in n]
    beta_col = [_to_col(beta_row[j], eye) for j in n]
    gc_col = [jnp.sum(jnp.where(tril, g_row[j], 0.0), axis=1, keepdims=True) for j in n]
    gc_row = [jnp.sum(jnp.where(ri <= ci, g_col[j], 0.0), axis=0, keepdims=True) for j in n]
    g_last = [jnp.sum(g_row[j], axis=1, keepdims=True) for j in n]
    decay = [jnp.where(tril, jnp.exp(jnp.minimum(gc_col[j] - gc_row[j], 0.0)), 0.0) for j in n]
    e_col = [jnp.exp(gc_col[j]) for j in n]
    f_col = [jnp.exp(g_last[j] - gc_col[j]) for j in n]
    e_last = [jnp.exp(g_last[j]) for j in n]
    kb = [k[j] * beta_col[j] for j in n]
    vb = [v[j] * beta_col[j] for j in n]
    kk = [_dot(kb[j], k[j], "nt") for j in n]
    qk = [_dot(q[j], k[j], "nt") for j in n]
    low = [jnp.where(strict, kk[j] * decay[j], 0.0) for j in n]
    att = [qk[j] * decay[j] for j in n]
    return dict(q=q, k=k, v=v, beta_col=beta_col, decay=decay, e_col=e_col, f_col=f_col, e_last=e_last,
                kb=kb, vb=vb, low=low, att=att, eye=eye, strict=strict, tril=tril)


def _chunk_iotas():
    C = GDN_CHUNK
    return lax.broadcasted_iota(jnp.int32, (C, C), 0), lax.broadcasted_iota(jnp.int32, (C, C), 1)


def _gdn_chunk_fwd(qkv, ab, a_log, dt_bias, *, name, riding=None):
    S = qkv.shape[0]
    H, C, DK = GDN_HEADS, GDN_CHUNK, GDN_DK
    RB = min(_GDN_ROWS, S)
    NCB, NB, NC = RB // C, S // RB, S // C
    heads = range(H)

    def body(qkv_ref, ab_ref, alog_ref, dtb_ref, *rest):
        n_ride = 0 if riding is None else len(riding)
        ride_srcs, rest = rest[:n_ride], rest[n_ride:]
        (o_ref, st_ref, t_ref), rest = rest[:3], rest[3:]
        ride_dsts, rest = rest[:n_ride], rest[n_ride:]
        state, u_s, w_s, qe_s, kf_s, att_s, *ride_sems = rest
        nb = pl.program_id(0)
        if riding is not None:
            finish_ride = _ride(nb == 0, nb == NB - 1, ride_srcs, ride_dsts, ride_sems, True)

        @pl.when(nb == 0)
        def _():
            state[...] = jnp.zeros_like(state)

        ri, ci = _chunk_iotas()
        neg_a = [-jnp.exp(alog_ref[h]) for h in heads]
        e_last = []
        for c in range(NCB):
            rows = pl.ds(c * C, C)
            g_row = [neg_a[h] * _softplus(ab_ref[h, c] + dtb_ref[h]) for h in heads]
            beta_row = [_sigmoid(ab_ref[H + h, c]) for h in heads]
            L = _gdn_local_batch([qkv_ref[rows, h * _HM:(h + 1) * _HM] for h in heads], g_row, beta_row, ri, ci)
            tinv = _unit_lower_inverse(L["low"], ri, ci)
            u = [_dot(tinv[h], L["vb"][h], exact=True) for h in heads]
            w = [_dot(tinv[h], L["kb"][h] * L["e_col"][h], exact=True) for h in heads]
            for h in heads:
                t_ref[h, c] = tinv[h]
                u_s[c, h] = u[h]
                w_s[c, h] = w[h].astype(BF16)
                qe_s[c, h] = (L["q"][h] * L["e_col"][h]).astype(BF16)
                kf_s[c, h] = (L["k"][h] * L["f_col"][h]).astype(BF16)
                att_s[c, h] = L["att"][h].astype(BF16)
            e_last.append(L["e_last"])
        st = [state[h] for h in heads]
        for c in range(NCB):
            rows = pl.ds(c * C, C)
            stb = [st[h].astype(BF16) for h in heads]
            vn = [u_s[c, h] - _dot(w_s[c, h], stb[h]) for h in heads]
            vnb = [vn[h].astype(BF16) for h in heads]
            out = [_dot(qe_s[c, h], stb[h]) + _dot(att_s[c, h], vnb[h]) for h in heads]
            new = [st[h] * e_last[c][h] + _dot(kf_s[c, h], vnb[h], "tn") for h in heads]
            for h in heads:
                o_ref[rows, h * DK:(h + 1) * DK] = out[h]
                st_ref[h, c] = st[h]
            st = new
        for h in heads:
            state[h] = st[h]
        if riding is not None:
            finish_ride()

    ride_args, ride_specs, ride_out, ride_scratch = _riding(riding, True)
    return pl.pallas_call(
        body, name=name, grid=(NB,),
        in_specs=[pl.BlockSpec((RB, H * _HM), lambda n: (n, 0)),
                  pl.BlockSpec((2 * H, NCB, 1, C), lambda n: (0, n, 0, 0)),
                  pl.BlockSpec((H, 1, 1), lambda n: (0, 0, 0)),
                  pl.BlockSpec((H, 1, 1), lambda n: (0, 0, 0))] + ride_specs,
        out_specs=[pl.BlockSpec((RB, H * DK), lambda n: (n, 0)),
                   pl.BlockSpec((H, NCB, DK, DK), lambda n: (0, n, 0, 0)),
                   pl.BlockSpec((H, NCB, C, C), lambda n: (0, n, 0, 0))] + ride_specs,
        out_shape=[jax.ShapeDtypeStruct((S, H * DK), F32),
                   jax.ShapeDtypeStruct((H, NC, DK, DK), F32),
                   jax.ShapeDtypeStruct((H, NC, C, C), F32)] + ride_out,
        scratch_shapes=[pltpu.VMEM((H, DK, DK), F32), pltpu.VMEM((NCB, H, C, DK), F32),
                        pltpu.VMEM((NCB, H, C, DK), BF16), pltpu.VMEM((NCB, H, C, DK), BF16),
                        pltpu.VMEM((NCB, H, C, DK), BF16), pltpu.VMEM((NCB, H, C, C), BF16)] + ride_scratch,
        compiler_params=_params("arbitrary"),
    )(qkv, ab, a_log, dt_bias, *ride_args)


_CHIP_PEERS = N_DEV // 2 - 1


def _chip_copies(src_refs, dst_refs, send_sems, recv_sems, local_sems, gather=False):
    x, y, c = lax.axis_index("x"), lax.axis_index("y"), lax.axis_index("c")
    here = 2 * x + y
    copies = []
    for a, (src_ref, dst_ref) in enumerate(zip(src_refs, dst_refs)):
        landing = dst_ref.at[here, c] if gather else dst_ref.at[here]
        copies.append(pltpu.make_async_copy(src_ref if gather else src_ref.at[here], landing, local_sems.at[a]))
        for rel in range(1, N_DEV // 2):
            px = 1 - x if rel & 2 else x
            py = 1 - y if rel & 1 else y
            k = a * _CHIP_PEERS + rel - 1
            copies.append(pltpu.make_async_remote_copy(
                src_ref=src_ref if gather else src_ref.at[2 * px + py], dst_ref=landing,
                send_sem=send_sems.at[k], recv_sem=recv_sems.at[k],
                device_id=(px, py, c), device_id_type=pl.DeviceIdType.MESH))
    return copies


def _chip_sems(n):
    return [pltpu.SemaphoreType.DMA((n * _CHIP_PEERS,)), pltpu.SemaphoreType.DMA((n * _CHIP_PEERS,)),
            pltpu.SemaphoreType.DMA((n,))]


def _riding(riding, gather):
    if riding is None:
        return [], [], [], []
    shapes = [jax.ShapeDtypeStruct(((N_DEV // 2, 2) + r.shape) if gather else r.shape, r.dtype) for r in riding]
    return list(riding), [pl.BlockSpec(memory_space=pl.ANY)] * len(riding), shapes, _chip_sems(len(riding))


def _ride(first, last, srcs, dsts, sems, gather):
    @pl.when(first)
    def _():
        for cp in _chip_copies(srcs, dsts, *sems, gather=gather):
            cp.start()

    def finish():
        @pl.when(last)
        def _():
            for cp in _chip_copies(srcs, dsts, *sems, gather=gather):
                cp.wait()

    return finish


def _gdn_chunk_bwd(qkv, ab, a_log, dt_bias, states, tinvs, do, *, name, riding=None):
    S = qkv.shape[0]
    H, C, DK = GDN_HEADS, GDN_CHUNK, GDN_DK
    RB = min(_GDN_ROWS, S)
    NCB, NB, NC = RB // C, S // RB, S // C
    heads = range(H)

    def body(qkv_ref, ab_ref, alog_ref, dtb_ref, st_ref, t_ref, do_ref, *rest):
        n_ride = 0 if riding is None else len(riding)
        ride_srcs, rest = rest[:n_ride], rest[n_ride:]
        (dqkv_ref, dab_ref, dalog_ref, ddtb_ref), rest = rest[:4], rest[4:]
        ride_dsts, rest = rest[:n_ride], rest[n_ride:]
        dstate, w_s, vn_s, qe_s, kf_s, att_s, dvn_s, dkf_s, *ride_sems = rest
        nb = pl.program_id(0)
        if riding is not None:
            finish_ride = _ride(nb == 0, nb == NB - 1, ride_srcs, ride_dsts, ride_sems, False)

        @pl.when(nb == 0)
        def _():
            dstate[...] = jnp.zeros_like(dstate)
            dalog_ref[...] = jnp.zeros_like(dalog_ref)
            ddtb_ref[...] = jnp.zeros_like(ddtb_ref)

        ri, ci = _chunk_iotas()
        neg_a = [-jnp.exp(alog_ref[h]) for h in heads]

        def local(c):
            rows = pl.ds(c * C, C)
            a_pre = [ab_ref[h, c] + dtb_ref[h] for h in heads]
            g_row = [neg_a[h] * _softplus(a_pre[h]) for h in heads]
            beta_row = [_sigmoid(ab_ref[H + h, c]) for h in heads]
            L = _gdn_local_batch([qkv_ref[rows, h * _HM:(h + 1) * _HM] for h in heads], g_row, beta_row, ri, ci)
            return L, a_pre, g_row, beta_row

        e_last = [None] * NCB
        for c in range(NCB):
            L, _, _, _ = local(c)
            kbe = [L["kb"][h] * L["e_col"][h] for h in heads]
            u = [_dot(t_ref[h, c], L["vb"][h], exact=True) for h in heads]
            w = [_dot(t_ref[h, c], kbe[h], exact=True) for h in heads]
            vn = [u[h] - _dot(w[h], st_ref[h, c]) for h in heads]
            for h in heads:
                w_s[c, h] = w[h].astype(BF16)
                vn_s[c, h] = vn[h].astype(BF16)
                qe_s[c, h] = (L["q"][h] * L["e_col"][h]).astype(BF16)
                kf_s[c, h] = (L["k"][h] * L["f_col"][h]).astype(BF16)
                att_s[c, h] = L["att"][h].astype(BF16)
            e_last[c] = L["e_last"]

        dst = [dstate[h] for h in heads]
        de_last = [None] * NCB
        for c in reversed(range(NCB)):
            rows = pl.ds(c * C, C)
            dob = [do_ref[rows, h * DK:(h + 1) * DK].astype(BF16) for h in heads]
            dstb = [dst[h].astype(BF16) for h in heads]
            dvn = [_dot(att_s[c, h], dob[h], "tn") + _dot(kf_s[c, h], dstb[h]) for h in heads]
            dkf = [_dot(vn_s[c, h], dstb[h], "nt") for h in heads]
            de_last[c] = [jnp.sum(jnp.sum(dst[h] * st_ref[h, c], axis=1, keepdims=True), axis=0, keepdims=True)
                          for h in heads]
            new = [dst[h] * e_last[c][h] + _dot(qe_s[c, h], dob[h], "tn")
                   - _dot(w_s[c, h], dvn[h].astype(BF16), "tn") for h in heads]
            for h in heads:
                dvn_s[c, h] = dvn[h]
                dkf_s[c, h] = dkf[h]
            dst = new
        for h in heads:
            dstate[h] = dst[h]

        for c in range(NCB):
            rows = pl.ds(c * C, C)
            L, a_pre, g_row, beta_row = local(c)
            q, k, v, kb, vb = L["q"], L["k"], L["v"], L["kb"], L["vb"]
            e_col, f_col, decay, beta_col = L["e_col"], L["f_col"], L["decay"], L["beta_col"]
            eye, strict, tril = L["eye"], L["strict"], L["tril"]
            tinv = [t_ref[h, c] for h in heads]
            stb = [st_ref[h, c].astype(BF16) for h in heads]
            dov = [do_ref[rows, h * DK:(h + 1) * DK] for h in heads]
            dvn = [dvn_s[c, h] for h in heads]
            dkf = [dkf_s[c, h] for h in heads]
            kbe = [kb[h] * e_col[h] for h in heads]
            datt = [jnp.where(tril, _dot(dov[h], vn_s[c, h], "nt"), 0.0) for h in heads]
            dqe = [_dot(dov[h], stb[h], "nt") for h in heads]
            dw = [-_dot(dvn[h], stb[h], "nt") for h in heads]
            dt = [_dot(dvn[h], vb[h], "nt") + _dot(dw[h], kbe[h], "nt") for h in heads]
            dvb = [_dot(tinv[h], dvn[h], "tn", exact=True) for h in heads]
            dkbe = [_dot(tinv[h], dw[h], "tn", exact=True) for h in heads]
            tdt = [_dot(tinv[h], dt[h], "tn", exact=True) for h in heads]
            dlow = [-jnp.where(strict, _dot(tdt[h], tinv[h], "nt", exact=True), 0.0) for h in heads]
            dkk = [dlow[h] * decay[h] for h in heads]
            dqk = [datt[h] * decay[h] for h in heads]
            dkb = [_dot(dkk[h], k[h]) + dkbe[h] * e_col[h] for h in heads]
            dk = [_dot(dkk[h], kb[h], "tn") + _dot(dqk[h], q[h], "tn") + dkf[h] * f_col[h] + dkb[h] * beta_col[h]
                  for h in heads]
            dq = [_dot(dqk[h], k[h]) + dqe[h] * e_col[h] for h in heads]
            for h in heads:
                dqkv_ref[rows, h * _HM:h * _HM + DK] = dq[h]
                dqkv_ref[rows, h * _HM + DK:h * _HM + 2 * DK] = dk[h]
                dqkv_ref[rows, h * _HM + 2 * DK:(h + 1) * _HM] = dvb[h] * beta_col[h]

            dbeta_col = [jnp.sum(k[h] * dkb[h] + v[h] * dvb[h], axis=1, keepdims=True) for h in heads]
            pmat = [dlow[h] * L["low"][h] + datt[h] * L["att"][h] for h in heads]
            df_col = [jnp.sum(k[h] * dkf[h], axis=1, keepdims=True) * f_col[h] for h in heads]
            dgc_col = [jnp.sum(pmat[h], axis=1, keepdims=True)
                       + jnp.sum(q[h] * dqe[h] + kb[h] * dkbe[h], axis=1, keepdims=True) * e_col[h] - df_col[h]
                       for h in heads]
            dgc_row = [_to_row(dgc_col[h], eye) - jnp.sum(pmat[h], axis=0, keepdims=True) for h in heads]
            dg_last = [jnp.sum(df_col[h], axis=0, keepdims=True) + de_last[c][h] * L["e_last"][h] for h in heads]
            dgc_c = [_to_col(dgc_row[h], eye) for h in heads]
            dg_row = [jnp.sum(jnp.where(ri >= ci, dgc_c[h], 0.0), axis=0, keepdims=True) + dg_last[h] for h in heads]
            dbeta_row = [_to_row(dbeta_col[h], eye) for h in heads]
            for h in heads:
                da_row = dg_row[h] * neg_a[h] * _sigmoid(a_pre[h])
                dab_ref[h, c] = da_row
                dab_ref[H + h, c] = dbeta_row[h] * beta_row[h] * (1.0 - beta_row[h])
                dalog_ref[h] += jnp.sum(dg_row[h] * g_row[h], axis=1, keepdims=True)
                ddtb_ref[h] += jnp.sum(da_row, axis=1, keepdims=True)

        if riding is not None:
            finish_ride()

    rev = lambda n: NB - 1 - n
    vec = pl.BlockSpec((H, 1, 1), lambda n: (0, 0, 0))
    gates = pl.BlockSpec((2 * H, NCB, 1, C), lambda n: (0, rev(n), 0, 0))
    wide = pl.BlockSpec((RB, H * _HM), lambda n: (rev(n), 0))
    item = lambda dt: pltpu.VMEM((NCB, H, C, DK), dt)
    ride_args, ride_specs, ride_out, ride_scratch = _riding(riding, False)
    return pl.pallas_call(
        body, name=name, grid=(NB,),
        in_specs=[wide, gates, vec, vec,
                  pl.BlockSpec((H, NCB, DK, DK), lambda n: (0, rev(n), 0, 0)),
                  pl.BlockSpec((H, NCB, C, C), lambda n: (0, rev(n), 0, 0)),
                  pl.BlockSpec((RB, H * DK), lambda n: (rev(n), 0))] + ride_specs,
        out_specs=[wide, gates, vec, vec] + ride_specs,
        out_shape=[jax.ShapeDtypeStruct((S, H * _HM), F32),
                   jax.ShapeDtypeStruct((2 * H, NC, 1, C), F32),
                   jax.ShapeDtypeStruct((H, 1, 1), F32),
                   jax.ShapeDtypeStruct((H, 1, 1), F32)] + ride_out,
        scratch_shapes=[pltpu.VMEM((H, DK, DK), F32), item(BF16), item(BF16), item(BF16), item(BF16),
                        pltpu.VMEM((NCB, H, C, C), BF16), item(F32), item(F32)] + ride_scratch,
        compiler_params=_params("arbitrary"),
    )(qkv, ab, a_log, dt_bias, states, tinvs, do, *ride_args)


def _gdn_outnorm_fwd(o, z, gain, *, name):
    S, HV = o.shape
    RB = min(256, S)

    def body(o_ref, z_ref, g_ref, y_ref):
        for h in range(HV // GDN_DK):
            cols = slice(h * GDN_DK, (h + 1) * GDN_DK)
            ov = o_ref[:, cols]
            r = lax.rsqrt(jnp.mean(ov * ov, axis=-1, keepdims=True) + RMS_EPS)
            y_ref[:, cols] = (ov * r * g_ref[...] * _silu(z_ref[:, cols].astype(F32))).astype(BF16)

    blk = pl.BlockSpec((RB, HV), lambda i: (i, 0))
    return pl.pallas_call(
        body, name=name, grid=(S // RB,),
        in_specs=[blk, blk, pl.BlockSpec((1, GDN_DK), lambda i: (0, 0))], out_specs=blk,
        out_shape=jax.ShapeDtypeStruct((S, HV), BF16), compiler_params=_params("parallel"),
    )(o, z, gain)


def _gdn_outnorm_bwd(dy, o, z, gain, *, name):
    S, HV = o.shape
    RB = min(256, S)

    def body(dy_ref, o_ref, z_ref, g_ref, do_ref, dz_ref, dg_ref):
        part = None
        for h in range(HV // GDN_DK):
            cols = slice(h * GDN_DK, (h + 1) * GDN_DK)
            ov = o_ref[:, cols]
            zv = z_ref[:, cols].astype(F32)
            dyv = dy_ref[:, cols].astype(F32)
            r = lax.rsqrt(jnp.mean(ov * ov, axis=-1, keepdims=True) + RMS_EPS)
            n = ov * r
            sg = _sigmoid(zv)
            dng = dyv * (zv * sg)
            dn = dng * g_ref[...]
            do_ref[:, cols] = r * (dn - n * jnp.mean(dn * n, axis=-1, keepdims=True))
            dz_ref[:, cols] = (dyv * (n * g_ref[...]) * (sg * (1.0 + zv * (1.0 - sg)))).astype(BF16)
            p = jnp.sum(dng * n, axis=0, keepdims=True)
            part = p if part is None else part + p

        @pl.when(pl.program_id(0) == 0)
        def _():
            dg_ref[...] = part

        @pl.when(pl.program_id(0) > 0)
        def _():
            dg_ref[...] += part

    blk = pl.BlockSpec((RB, HV), lambda i: (i, 0))
    vec = pl.BlockSpec((1, GDN_DK), lambda i: (0, 0))
    return pl.pallas_call(
        body, name=name, grid=(S // RB,),
        in_specs=[blk, blk, blk, vec], out_specs=[blk, blk, vec],
        out_shape=[jax.ShapeDtypeStruct((S, HV), F32), jax.ShapeDtypeStruct((S, HV), BF16),
                   jax.ShapeDtypeStruct((1, GDN_DK), F32)],
        compiler_params=_params("arbitrary"),
    )(dy, o, z, gain)


def _head_mask():
    return lax.broadcasted_iota(jnp.int32, (DSW_BLK, LANES), 1) < DSW_DH


def _per_head_sum(t, first):
    s0 = jnp.sum(jnp.where(first, t, 0.0), axis=-1, keepdims=True)
    s1 = jnp.sum(jnp.where(first, 0.0, t), axis=-1, keepdims=True)
    return jnp.where(first, s0, s1)


def _rms2(x, gain, first):
    r = lax.rsqrt(_per_head_sum(x * x, first) * (1.0 / DSW_DH) + RMS_EPS)
    xh = x * r
    return xh, r, xh * gain


def _rms2_bwd(dy, xh, r, gain, first):
    dxh = dy * gain
    return r * (dxh - xh * (_per_head_sum(dxh * xh, first) * (1.0 / DSW_DH)))


def _split_heads(x, first):
    return [jnp.where(first, x, 0.0).astype(BF16), jnp.where(first, 0.0, x).astype(BF16)]


_HP = LANES // DSW_DH
_DSW_W = DSW_HEADS * DSW_DH
_DSW_ROWS = 1024
_DSW_BATCH = 8


def _dsw_geometry(S, g):
    d = DSW_GROUPS[g][1]
    slab = DSW_BLK * d
    tb = max(1, min(_DSW_ROWS, S) // slab)
    return d, slab, tb, S // (tb * slab)


def _block_rows(t, r, slab, d):
    return pl.ds(t * slab + r, DSW_BLK) if d == 1 else pl.ds(t * slab + r, DSW_BLK, stride=d)


def _dsw_attn_fwd(q, k, v, bias, q_gain, k_gain, prev_out, *, g, name):
    S, WT = q.shape
    B = DSW_BLK
    d, slab, tb, n_tiles = _dsw_geometry(S, g)
    rt = tb * slab
    cb = g * (_DSW_W // LANES)
    batch_res = max(1, _DSW_BATCH // tb)

    def body(q_ref, kp_ref, kc_ref, vp_ref, vc_ref, bias_ref, qg_ref, kg_ref, *rest):
        o_ref, lse_ref = rest[-2:]
        i = pl.program_id(1)
        qg, kg = qg_ref[...] * DSW_DH ** -0.5, kg_ref[...]
        col = lax.broadcasted_iota(jnp.int32, (B, 2 * B), 1)
        first = _head_mask()
        heads = range(_HP)
        for r0 in range(0, d, batch_res):
            res = range(r0, min(d, r0 + batch_res))
            k_raw = {(r, -1): kp_ref[_block_rows(0, r, slab, d), :] for r in res}
            v_raw = {(r, -1): vp_ref[_block_rows(0, r, slab, d), :] for r in res}
            q_raw = {}
            for r in res:
                for t in range(tb):
                    rows = _block_rows(t, r, slab, d)
                    q_raw[r, t], k_raw[r, t], v_raw[r, t] = q_ref[rows, :], kc_ref[rows, :], vc_ref[rows, :]
            kn = {key: _rms2(x, kg, first)[2].astype(BF16) for key, x in k_raw.items()}
            vb = {key: x.astype(BF16) for key, x in v_raw.items()}
            qn = {key: _split_heads(_rms2(x, qg, first)[2], first) for key, x in q_raw.items()}
            items = [(r, t, h) for r in res for t in range(tb) for h in heads]
            s = {}
            for r, t, h in items:
                sv = _dot(qn[r, t][h], jnp.concatenate([kn[r, t - 1], kn[r, t]], axis=0), "nt") + bias_ref[h]
                s[r, t, h] = jnp.where((i == 0) & (col < B), NEG_BIG, sv) if t == 0 else sv
            m = {it: jnp.max(s[it], axis=-1, keepdims=True) for it in items}
            p = {it: jnp.exp(s[it] - m[it]) for it in items}
            l = {it: jnp.sum(p[it], axis=-1, keepdims=True) for it in items}
            o = {(r, t, h): _dot(p[r, t, h], jnp.concatenate([vb[r, t - 1], vb[r, t]], axis=0)) for r, t, h in items}
            for r in res:
                for t in range(tb):
                    rows = _block_rows(t, r, slab, d)
                    o_ref[rows, :] = jnp.where(first, o[r, t, 0] / l[r, t, 0], o[r, t, 1] / l[r, t, 1])
                    lse_ref[rows, :] = jnp.where(first, m[r, t, 0] + jnp.log(l[r, t, 0]),
                                                 m[r, t, 1] + jnp.log(l[r, t, 1]))

    cur = pl.BlockSpec((rt, LANES), lambda hp, i: (i, cb + hp))
    prev = pl.BlockSpec((slab, LANES), lambda hp, i: (jnp.maximum(i * tb - 1, 0), cb + hp))
    vec = pl.BlockSpec((1, LANES), lambda hp, i: (0, 0))
    shp = jax.ShapeDtypeStruct((S, WT), F32)
    carried = [] if prev_out is None else list(prev_out)
    n_in = 8
    return pl.pallas_call(
        body, name=name, grid=(_DSW_W // LANES, n_tiles),
        in_specs=[cur, prev, cur, prev, cur, pl.BlockSpec((_HP, B, 2 * B), lambda hp, i: (hp, 0, 0)), vec, vec]
                 + [pl.BlockSpec(memory_space=pl.ANY)] * len(carried),
        out_specs=[cur, cur], out_shape=[shp, shp],
        input_output_aliases={n_in + j: j for j in range(len(carried))},
        compiler_params=_params("parallel", "parallel"),
    )(q, k, k, v, v, bias, jnp.tile(q_gain, (1, _HP)), jnp.tile(k_gain, (1, _HP)), *carried)


def _dsw_merge(o_g, lse_g, *, name):
    S = o_g.shape[0]
    W, G = _DSW_W, len(DSW_GROUPS)
    tr = min(512, S)

    def body(o_ref, l_ref, out_ref, lse_ref):
        ls = [l_ref[:, g * W:(g + 1) * W] for g in range(G)]
        m = ls[0]
        for g in range(1, G):
            m = jnp.maximum(m, ls[g])
        den = jnp.zeros_like(m)
        acc = jnp.zeros_like(m)
        for g in range(G):
            wg = jnp.exp(ls[g] - m)
            den = den + wg
            acc = acc + wg * o_ref[:, g * W:(g + 1) * W]
        out_ref[...] = acc / den
        lse_ref[...] = m + jnp.log(den)

    wide = pl.BlockSpec((tr, G * W), lambda i: (i, 0))
    blk = pl.BlockSpec((tr, W), lambda i: (i, 0))
    shp = jax.ShapeDtypeStruct((S, W), F32)
    return pl.pallas_call(
        body, name=name, grid=(S // tr,), in_specs=[wide, wide], out_specs=[blk, blk],
        out_shape=[shp, shp], compiler_params=_params("parallel"),
    )(o_g, lse_g)


def _dsw_attn_bwd(q, k, v, o, lse, do, bias, q_gain, k_gain, prev_out, *, g, name):
    S, WT = q.shape
    B = DSW_BLK
    d, slab, tb, n_tiles = _dsw_geometry(S, g)
    rt = tb * slab
    cb = g * (_DSW_W // LANES)
    n_slabs = S // slab
    scale = DSW_DH ** -0.5
    batch_res = max(1, _DSW_BATCH // tb)

    def body(q_ref, qx_ref, kp_ref, kc_ref, vp_ref, vc_ref, o_ref, ox_ref, l_ref, lx_ref, do_ref, dox_ref,
             bias_ref, qg_ref, kg_ref, *rest):
        dq_ref, dk_ref, dv_ref, db_ref, dqg_ref, dkg_ref = rest[-6:]
        hp, i = pl.program_id(0), pl.program_id(1)
        qg, kg = qg_ref[...] * scale, kg_ref[...]
        col = lax.broadcasted_iota(jnp.int32, (B, 2 * B), 1)
        has_next = i < n_tiles - 1

        @pl.when(i == 0)
        def _():
            db_ref[...] = jnp.zeros_like(db_ref)

        dqg_acc = jnp.zeros((1, LANES), F32)
        dkg_acc = jnp.zeros((1, LANES), F32)
        first = _head_mask()
        heads = range(_HP)
        for r0 in range(0, d, batch_res):
            res = range(r0, min(d, r0 + batch_res))
            q_raw, k_raw, v_raw, o_raw, l_raw, do_raw = {}, {}, {}, {}, {}, {}
            for r in res:
                first_rows = _block_rows(0, r, slab, d)
                k_raw[r, -1], v_raw[r, -1] = kp_ref[first_rows, :], vp_ref[first_rows, :]
                for t in range(tb):
                    rows = _block_rows(t, r, slab, d)
                    q_raw[r, t], o_raw[r, t], l_raw[r, t], do_raw[r, t] = (
                        q_ref[rows, :], o_ref[rows, :], l_ref[rows, :], do_ref[rows, :])
                    k_raw[r, t], v_raw[r, t] = kc_ref[rows, :], vc_ref[rows, :]
                q_raw[r, tb], o_raw[r, tb], l_raw[r, tb], do_raw[r, tb] = (
                    qx_ref[first_rows, :], ox_ref[first_rows, :], lx_ref[first_rows, :], dox_ref[first_rows, :])
            kk = {key: _rms2(x, kg, first) for key, x in k_raw.items()}
            qq = {key: _rms2(x, qg, first) for key, x in q_raw.items()}
            knb = {key: kk[key][2].astype(BF16) for key in kk}
            qnb = {key: _split_heads(qq[key][2], first) for key in qq}
            vb = {key: x.astype(BF16) for key, x in v_raw.items()}
            dob = {key: _split_heads(x, first) for key, x in do_raw.items()}
            delta = {key: _per_head_sum(do_raw[key] * o_raw[key], first) for key in q_raw}
            pick = lambda x, h: x[:, h * DSW_DH:h * DSW_DH + 1]
            full = [(r, t, h) for r in res for t in range(tb) for h in heads]
            half = [(r, tb, h) for r in res for h in heads]
            s = {}
            for r, t, h in full:
                sv = _dot(qnb[r, t][h], jnp.concatenate([knb[r, t - 1], knb[r, t]], axis=0), "nt") + bias_ref[h]
                s[r, t, h] = jnp.where((i == 0) & (col < B), NEG_BIG, sv) if t == 0 else sv
            for r, t, h in half:
                s[r, t, h] = _dot(qnb[r, t][h], knb[r, t - 1], "nt") + bias_ref[h, :, 0:B]
            p = {(r, t, h): jnp.exp(s[r, t, h] - pick(l_raw[r, t], h)) for r, t, h in full}
            for r, t, h in half:
                p[r, t, h] = jnp.where(has_next, jnp.exp(s[r, t, h] - pick(l_raw[r, t], h)), 0.0)
            dp = {(r, t, h): _dot(dob[r, t][h], jnp.concatenate([vb[r, t - 1], vb[r, t]], axis=0), "nt")
                  for r, t, h in full}
            for r, t, h in half:
                dp[r, t, h] = _dot(dob[r, t][h], vb[r, t - 1], "nt")
            ds = {(r, t, h): p[r, t, h] * (dp[r, t, h] - pick(delta[r, t], h)) for r, t, h in full + half}
            pb = {it: p[it].astype(BF16) for it in ds}
            dsb = {it: ds[it].astype(BF16) for it in ds}
            for h in heads:
                tot = None
                for r in res:
                    for t in range(tb):
                        tot = ds[r, t, h] if tot is None else tot + ds[r, t, h]
                db_ref[h] += tot
            blocks = [(r, t) for r in res for t in range(tb)]
            keys2 = {(r, t): jnp.concatenate([knb[r, t - 1], knb[r, t]], axis=0) for r, t in blocks}
            dqn = {(r, t): jnp.where(first, _dot(dsb[r, t, 0], keys2[r, t]), _dot(dsb[r, t, 1], keys2[r, t]))
                   for r, t in blocks}
            prev_half = lambda x, r, t, h: x[r, t, h][:, :B] if t < tb else x[r, t, h]
            dkn = {(r, t): sum(_dot(dsb[r, t, h][:, B:], qnb[r, t][h], "tn")
                               + _dot(prev_half(dsb, r, t + 1, h), qnb[r, t + 1][h], "tn") for h in heads)
                   for r, t in blocks}
            dvv = {(r, t): sum(_dot(pb[r, t, h][:, B:], dob[r, t][h], "tn")
                               + _dot(prev_half(pb, r, t + 1, h), dob[r, t + 1][h], "tn") for h in heads)
                   for r, t in blocks}
            for r, t in blocks:
                dqg_acc = dqg_acc + jnp.sum(dqn[r, t] * qq[r, t][0], axis=0, keepdims=True)
                dkg_acc = dkg_acc + jnp.sum(dkn[r, t] * kk[r, t][0], axis=0, keepdims=True)
            for r, t in blocks:
                rows = _block_rows(t, r, slab, d)
                dq_ref[rows, :] = _rms2_bwd(dqn[r, t], qq[r, t][0], qq[r, t][1], qg, first)
                dk_ref[rows, :] = _rms2_bwd(dkn[r, t], kk[r, t][0], kk[r, t][1], kg, first)
                dv_ref[rows, :] = dvv[r, t]

        start = (hp == 0) & (i == 0)
        fold = lambda a: a[:, :DSW_DH] + a[:, DSW_DH:]

        @pl.when(start)
        def _():
            dqg_ref[...] = fold(dqg_acc) * scale
            dkg_ref[...] = fold(dkg_acc)

        @pl.when(jnp.logical_not(start))
        def _():
            dqg_ref[...] += fold(dqg_acc) * scale
            dkg_ref[...] += fold(dkg_acc)

    def spec(rows, pick, base):
        return pl.BlockSpec((rows, LANES), lambda hp, i: (pick(i), base + hp))

    same = lambda i: i
    before = lambda i: jnp.maximum(i * tb - 1, 0)
    after = lambda i: jnp.minimum((i + 1) * tb, n_slabs - 1)
    cur, cur1 = spec(rt, same, cb), spec(rt, same, 0)
    vec = pl.BlockSpec((1, DSW_DH), lambda hp, i: (0, 0))
    vec2 = pl.BlockSpec((1, LANES), lambda hp, i: (0, 0))
    bspec = pl.BlockSpec((_HP, B, 2 * B), lambda hp, i: (hp, 0, 0))
    shp = jax.ShapeDtypeStruct((S, WT), F32)
    vshp = jax.ShapeDtypeStruct((1, DSW_DH), F32)
    carried = [] if prev_out is None else list(prev_out)
    n_in = 15
    return pl.pallas_call(
        body, name=name, grid=(_DSW_W // LANES, n_tiles),
        in_specs=[cur, spec(slab, after, cb), spec(slab, before, cb), cur, spec(slab, before, cb), cur,
                  cur1, spec(slab, after, 0), cur1, spec(slab, after, 0), cur1, spec(slab, after, 0),
                  bspec, vec2, vec2] + [pl.BlockSpec(memory_space=pl.ANY)] * len(carried),
        out_specs=[cur, cur, cur, bspec, vec, vec],
        out_shape=[shp, shp, shp, jax.ShapeDtypeStruct(bias.shape, F32), vshp, vshp],
        input_output_aliases={n_in + j: j for j in range(len(carried))},
        compiler_params=_params("arbitrary", "arbitrary"),
    )(q, q, k, k, v, v, o, o, lse, lse, do, do, bias, jnp.tile(q_gain, (1, _HP)), jnp.tile(k_gain, (1, _HP)),
      *carried)


def _t5_bucket(dist):
    max_exact = REL_BUCKETS // 2
    scaled = jnp.log(jnp.maximum(dist, 1).astype(F32) / max_exact) / math.log(REL_MAX_DIST / max_exact)
    large = jnp.minimum(max_exact + (scaled * (REL_BUCKETS - max_exact)).astype(jnp.int32), REL_BUCKETS - 1)
    return jnp.where(dist < max_exact, dist, large)


def _dsw_band():
    dist = (jnp.arange(DSW_BLK)[:, None] + DSW_BLK) - jnp.arange(2 * DSW_BLK)[None, :]
    return dist, (dist >= 0) & (dist <= DSW_BLK)


def _dsw_bias(rel_bias):
    dist, band = _dsw_band()
    out = []
    for g, (_, d) in enumerate(DSW_GROUPS):
        hot = jax.nn.one_hot(_t5_bucket(jnp.maximum(dist, 0) * d), REL_BUCKETS, dtype=F32)
        tab = jnp.einsum("qkb,bh->hqk", hot, rel_bias[:, g * DSW_HEADS:(g + 1) * DSW_HEADS],
                         precision=lax.Precision.HIGHEST)
        out.append(jnp.where(band[None], tab, NEG_BIG))
    return jnp.stack(out)


def _dsw_bucket_onehot():
    dist, band = _dsw_band()
    out = []
    for _, d in DSW_GROUPS:
        hot = jax.nn.one_hot(_t5_bucket(jnp.maximum(dist, 0) * d), LANES, dtype=BF16)
        out.append(jnp.where(band[..., None], hot, 0).reshape(-1, LANES))
    return jnp.stack(out)


def _exchange(send, *, gather, name):
    R, C = send.shape[-2:]

    def body(src_ref, dst_ref, send_sems, recv_sems, local_sem):
        x, y, c = lax.axis_index("x"), lax.axis_index("y"), lax.axis_index("c")
        me = 4 * x + 2 * y + c
        mine = pltpu.make_async_copy(src_ref if gather else src_ref.at[me], dst_ref.at[me], local_sem)
        mine.start()
        copies = []
        for rel in range(1, N_DEV):
            px = 1 - x if rel & 4 else x
            py = 1 - y if rel & 2 else y
            pc = 1 - c if rel & 1 else c
            peer = 4 * px + 2 * py + pc
            cp = pltpu.make_async_remote_copy(
                src_ref=src_ref if gather else src_ref.at[peer], dst_ref=dst_ref.at[me],
                send_sem=send_sems.at[rel - 1], recv_sem=recv_sems.at[rel - 1],
                device_id=(px, py, pc), device_id_type=pl.DeviceIdType.MESH)
            cp.start()
            copies.append(cp)
        for cp in copies:
            cp.wait()
        mine.wait()

    return pl.pallas_call(
        body, name=name,
        in_specs=[pl.BlockSpec(memory_space=pl.ANY)], out_specs=pl.BlockSpec(memory_space=pl.ANY),
        out_shape=jax.ShapeDtypeStruct((N_DEV, R, C), send.dtype),
        scratch_shapes=[pltpu.SemaphoreType.DMA((N_DEV - 1,)), pltpu.SemaphoreType.DMA((N_DEV - 1,)),
                        pltpu.SemaphoreType.DMA(())],
    )(send)


def _gather_two_level(send, *, name):
    R, C = send.shape

    def body(src_ref, dst_ref, send_sems, recv_sems, local_sem):
        x, y, c = lax.axis_index("x"), lax.axis_index("y"), lax.axis_index("c")
        me, sibling = (x, y, c), (x, y, 1 - c)
        chips = [(1 - x, y), (x, 1 - y), (1 - x, 1 - y)]

        def slot(px, py, pc):
            return dst_ref.at[4 * px + 2 * py + pc]

        def copy(k, block, to, src=None):
            return pltpu.make_async_remote_copy(
                src_ref=slot(*block) if src is None else src, dst_ref=slot(*block),
                send_sem=send_sems.at[k], recv_sem=recv_sems.at[k],
                device_id=to, device_id_type=pl.DeviceIdType.MESH)

        mine = pltpu.make_async_copy(src_ref, slot(*me), local_sem)
        mine.start()
        first = [copy(0, me, sibling, src=src_ref)]
        first += [copy(1 + j, me, (*chip, c), src=src_ref) for j, chip in enumerate(chips)]
        for cp in first:
            cp.start()
        passed = [copy(4 + j, (*chip, c), sibling) for j, chip in enumerate(chips)]
        for j, chip in enumerate(chips):
            copy(1 + j, (*chip, c), me).wait_recv()
            passed[j].start()
        copy(0, sibling, me).wait_recv()
        for j, chip in enumerate(chips):
            copy(4 + j, (*chip, 1 - c), me).wait_recv()
        for cp in first + passed:
            cp.wait_send()
        mine.wait()

    return pl.pallas_call(
        body, name=name,
        in_specs=[pl.BlockSpec(memory_space=pl.ANY)], out_specs=pl.BlockSpec(memory_space=pl.ANY),
        out_shape=jax.ShapeDtypeStruct((N_DEV, R, C), send.dtype),
        scratch_shapes=[pltpu.SemaphoreType.DMA((N_DEV - 1,)), pltpu.SemaphoreType.DMA((N_DEV - 1,)),
                        pltpu.SemaphoreType.DMA(())],
    )(send)


_ANY = pl.BlockSpec(memory_space=pl.ANY)


def _swap_with_sibling(sends, *, name):
    n = len(sends)

    def body(*refs):
        x, y, c = lax.axis_index("x"), lax.axis_index("y"), lax.axis_index("c")
        send_sems, recv_sems = refs[2 * n:]
        copies = [pltpu.make_async_remote_copy(
            src_ref=refs[a], dst_ref=refs[n + a], send_sem=send_sems.at[a], recv_sem=recv_sems.at[a],
            device_id=(x, y, 1 - c), device_id_type=pl.DeviceIdType.MESH) for a in range(n)]
        for cp in copies:
            cp.start()
        for cp in copies:
            cp.wait()

    return pl.pallas_call(
        body, name=name, in_specs=[_ANY] * n, out_specs=[_ANY] * n,
        out_shape=[jax.ShapeDtypeStruct(s.shape, s.dtype) for s in sends],
        scratch_shapes=[pltpu.SemaphoreType.DMA((n,)), pltpu.SemaphoreType.DMA((n,))],
    )(*sends)


def _fill_from_sibling(bufs, *, name):
    n, n_chips = len(bufs), bufs[0].shape[0]

    def body(*refs):
        x, y, c = lax.axis_index("x"), lax.axis_index("y"), lax.axis_index("c")
        send_sems, recv_sems = refs[2 * n:]
        copies = [pltpu.make_async_remote_copy(
            src_ref=refs[a].at[q, c], dst_ref=refs[n + a].at[q, c],
            send_sem=send_sems.at[a * n_chips + q], recv_sem=recv_sems.at[a * n_chips + q],
            device_id=(x, y, 1 - c), device_id_type=pl.DeviceIdType.MESH) for a in range(n) for q in range(n_chips)]
        for cp in copies:
            cp.start()
        for cp in copies:
            cp.wait()

    return pl.pallas_call(
        body, name=name, in_specs=[_ANY] * n, out_specs=[_ANY] * n,
        out_shape=[jax.ShapeDtypeStruct(b.shape, b.dtype) for b in bufs],
        input_output_aliases={a: a for a in range(n)},
        scratch_shapes=[pltpu.SemaphoreType.DMA((n * n_chips,)), pltpu.SemaphoreType.DMA((n * n_chips,))],
    )(*bufs)


def _exchange_chips(send, *, name):
    def body(src_ref, dst_ref, *sems):
        copies = _chip_copies([src_ref], [dst_ref], *sems)
        for cp in copies:
            cp.start()
        for cp in copies:
            cp.wait()

    return pl.pallas_call(
        body, name=name, in_specs=[_ANY], out_specs=_ANY,
        out_shape=jax.ShapeDtypeStruct(send.shape, send.dtype), scratch_shapes=_chip_sems(1),
    )(send)


def _add_pair(a, b, *, name):
    n, R, C = a.shape
    tr = _tile(R, max(8, 1024 * LANES // C))

    def body(a_ref, b_ref, o_ref):
        o_ref[...] = (a_ref[...].astype(F32) + b_ref[...].astype(F32)).astype(o_ref.dtype)

    blk = pl.BlockSpec((None, tr, C), lambda k, i: (k, i, 0))
    return pl.pallas_call(
        body, name=name, grid=(n, R // tr), in_specs=[blk, blk], out_specs=blk,
        out_shape=jax.ShapeDtypeStruct(a.shape, a.dtype), compiler_params=_params("parallel", "parallel"),
    )(a, b)


_BIG = ("w_ffn_in", "w_ffn_out", "gdn_w_in", "gdn_conv", "gdn_w_out", "dsw_w_in", "dsw_w_out")
_LATE = ("gdn_w_in", "gdn_conv", "gdn_w_out")
_EARLY = tuple(n for n in _BIG if n not in _LATE)
_NATIVE = ("w_ffn_in", "w_ffn_out")
_SHARD_AXIS = {"w_ffn_in": 2, "w_ffn_out": 1, "gdn_w_in": 2, "gdn_conv": 2, "gdn_w_out": 1, "dsw_w_in": 2,
               "dsw_w_out": 2}
_SMALL = ("b_ada", "norm_mix", "norm_ffn", "gdn_a_log", "gdn_dt_bias", "gdn_out_norm", "dsw_q_norm",
          "dsw_k_norm", "rel_bias")
_ROW_ALIGN = 16
_BIG_ALIGN = 1024


def _ceil_to(n, m):
    return -(-n // m) * m


def _seg_rows(shape):
    return _ceil_to(_ceil_to(int(np.prod(shape)), LANES) // LANES, _ROW_ALIGN)


def _pack(arrs, total_align):
    lead = arrs[0][1]
    segs = []
    for a, nlead in arrs:
        assert nlead == lead
        bshape = a.shape[:nlead]
        n = int(np.prod(a.shape[nlead:]))
        rows = _seg_rows(a.shape[nlead:])
        flat = a.reshape(bshape + (n,))
        flat = jnp.pad(flat, [(0, 0)] * nlead + [(0, rows * LANES - n)])
        segs.append(flat.reshape(bshape + (rows, LANES)))
    buf = jnp.concatenate(segs, axis=lead)
    total = _ceil_to(buf.shape[lead], total_align)
    return jnp.pad(buf, [(0, 0)] * lead + [(0, total - buf.shape[lead]), (0, 0)])


def _unpack(buf, shapes, nlead):
    out, off = [], 0
    for shp in shapes:
        n, rows = int(np.prod(shp)), _seg_rows(shp)
        seg = lax.slice_in_dim(buf, off, off + rows, axis=nlead)
        seg = seg.reshape(buf.shape[:nlead] + (rows * LANES,))[..., :n]
        out.append(seg.reshape(buf.shape[:nlead] + tuple(shp)))
        off += rows
    return out


def _to_natural(g, axis):
    n, L, r, c = g.shape
    if axis == 2:
        return jnp.transpose(g, (1, 2, 0, 3)).reshape(L, r, n * c)
    return jnp.transpose(g, (1, 0, 2, 3)).reshape(L, n * r, c)


def _to_blocked(w, axis):
    L, R, C = w.shape
    if axis == 2:
        return jnp.transpose(w.reshape(L, R, N_DEV, C // N_DEV), (2, 0, 1, 3))
    return jnp.transpose(w.reshape(L, N_DEV, R // N_DEV, C), (1, 0, 2, 3))


def _hm(a):
    lead = a.shape[:-1]
    return jnp.swapaxes(a.reshape(lead + (3, GDN_HEADS, GDN_DK)), -3, -2).reshape(lead + (3 * GDN_HEADS * GDN_DK,))


def _un_hm(a):
    lead = a.shape[:-1]
    return jnp.swapaxes(a.reshape(lead + (GDN_HEADS, 3, GDN_DK)), -3, -2).reshape(lead + (3 * GDN_HEADS * GDN_DK,))


_TILES = (1536, 1408, 1024, 768, 512, 384, 256, 128, 64, 32, 16, 8)


def _tile(n, cap):
    for t in _TILES:
        if t <= cap and n % t == 0:
            return t
    return n


def _mm_auto(a, b, mode, name, **kw):
    if mode == "tn":
        (K, M), N = a.shape, b.shape[1]
        tm, tn, tk = _tile(M, 1408), _tile(N, 1408), _tile(K, 1024)
    else:
        M, K = a.shape
        N = b.shape[1] if mode == "nn" else b.shape[0]
        tm, tn, tk = _tile(M, 512), _tile(N, 1536), _tile(K, 1408)
    return _mm(a, b, mode=mode, name=name, tm=tm, tn=tn, tk=tk, **kw)


def _row(v):
    return v.reshape(1, -1)


def _ffn_in_act(h, w_in, *, name):
    S, D = h.shape
    F = w_in.shape[1] // 2
    tm, tn = _tile(S, 512), _tile(F, 1408)
    nj = F // tn

    def body(h_ref, wg_ref, wu_ref, g_ref, u_ref, a_ref):
        hv = h_ref[...]
        gate = jnp.dot(hv, wg_ref[...], preferred_element_type=F32)
        up = jnp.dot(hv, wu_ref[...], preferred_element_type=F32)
        g_ref[...] = gate.astype(BF16)
        u_ref[...] = up.astype(BF16)
        a_ref[...] = (_silu(gate) * up).astype(BF16)

    out = pl.BlockSpec((tm, tn), lambda i, j: (i, j))
    shp = jax.ShapeDtypeStruct((S, F), BF16)
    return pl.pallas_call(
        body, name=name, grid=(S // tm, nj),
        in_specs=[pl.BlockSpec((tm, D), lambda i, j: (i, 0)), pl.BlockSpec((D, tn), lambda i, j: (0, j)),
                  pl.BlockSpec((D, tn), lambda i, j: (0, j + nj))],
        out_specs=[out, out, out], out_shape=[shp, shp, shp],
        compiler_params=_params("parallel", "parallel"),
    )(h, w_in, w_in)


def _ffn_out_dx_act(dy, w_out, gate_vec, pg, pu, *, name):
    S, D = dy.shape
    F = w_out.shape[0]
    tm, tn = _tile(S, 512), _tile(F, 1408)

    def body(dy_ref, w_ref, gv_ref, pg_ref, pu_ref, dg_ref, du_ref):
        dyg = (dy_ref[...] * gv_ref[...]).astype(BF16)
        da = lax.dot_general(dyg, w_ref[...], _DOT_DIMS["nt"], preferred_element_type=F32)
        gate = pg_ref[...].astype(F32)
        up = pu_ref[...].astype(F32)
        sg = _sigmoid(gate)
        dg_ref[...] = (da * up * (sg * (1.0 + gate * (1.0 - sg)))).astype(BF16)
        du_ref[...] = (da * (gate * sg)).astype(BF16)

    blk = pl.BlockSpec((tm, tn), lambda i, j: (i, j))
    shp = jax.ShapeDtypeStruct((S, F), BF16)
    return pl.pallas_call(
        body, name=name, grid=(S // tm, F // tn),
        in_specs=[pl.BlockSpec((tm, D), lambda i, j: (i, 0)), pl.BlockSpec((tn, D), lambda i, j: (j, 0)),
                  pl.BlockSpec((1, D), lambda i, j: (0, 0)), blk, blk],
        out_specs=[blk, blk], out_shape=[shp, shp],
        compiler_params=_params("parallel", "parallel"),
    )(dy, w_out, gate_vec, pg, pu)


def _ffn_fwd(x, mod, gain, w_in, w_out, tag):
    sh, sc, gate = mod
    h = _norm_mod_fwd(x, gain, sc, sh, name=f"ffn_norm_{tag}")
    pg, pu, a = _ffn_in_act(h, w_in, name=f"ffn_in_{tag}")
    y = _mm_auto(a, w_out, "nn", f"ffn_out_{tag}", out_scale=gate, resid=x)
    return y, (x, h, pg, pu, a)


def _ffn_bwd(dy, saved, mod, gain, w_in, w_out, tag):
    sh, sc, gate = mod
    x, h, pg, pu, a = saved
    F = pg.shape[1]
    gmat = _mm_auto(a, dy, "tn", f"ffn_out_g_{tag}")
    dw_out, dgate = _wout_grad(gmat, w_out, gate, name=f"ffn_out_dw_{tag}")
    dpg, dpu = _ffn_out_dx_act(dy, w_out, gate, pg, pu, name=f"ffn_out_dx_{tag}")
    dw_in = jnp.concatenate([_mm_auto(h, dpg, "tn", f"ffn_in_dw_gate_{tag}", out_dtype=BF16),
                             _mm_auto(h, dpu, "tn", f"ffn_in_dw_up_{tag}", out_dtype=BF16)], axis=1)
    tk = _tile(F, 1408)
    dh = _mm_sum_nt([(dpg, w_in, tk, 0), (dpu, w_in, tk, F)], name=f"ffn_in_dx_{tag}")
    dx, dsh, dsc, dgain = _norm_mod_bwd(dh, x, dy, gain, sc, name=f"ffn_norm_bwd_{tag}")
    return dx, dict(w_in=dw_in, w_out=dw_out, gain=dgain, mod=(dsh, dsc, dgate))


def _gdn_fwd(x, mod, gain, W, riding=None):
    sh, sc, gate = mod
    S = x.shape[0]
    h = _norm_mod_fwd(x, gain, sc, sh, name="gdn_norm")
    pq = _mm_auto(h, W["gdn_qkv"], "nn", "gdn_in_qkv", out_dtype=BF16)
    z = _mm_auto(h, W["gdn_z"], "nn", "gdn_in_z", out_dtype=BF16)
    ab = _mm_auto(h, W["gdn_ab"], "nn", "gdn_in_ab")
    qkvn = _gdn_prep_fwd(pq, W["gdn_conv"], name="gdn_prep")
    ab4 = jnp.transpose(ab[:, :2 * GDN_HEADS]).reshape(2 * GDN_HEADS, S // GDN_CHUNK, 1, GDN_CHUNK)
    o, states, tinvs, *rode = _gdn_chunk_fwd(qkvn, ab4, W["gdn_a_log"], W["gdn_dt_bias"], name="gdn_chunk",
                                             riding=riding)
    o2 = _gdn_outnorm_fwd(o, z, W["gdn_out_norm"], name="gdn_outnorm")
    y = _mm_auto(o2, W["gdn_out"], "nn", "gdn_out", out_scale=gate, resid=x)
    return y, (x, h, pq, z, qkvn, ab4, o, states, tinvs, o2), (tuple(rode) if rode else None)


def _gdn_bwd(dy, saved, mod, gain, W, riding=None):
    sh, sc, gate = mod
    x, h, pq, z, qkvn, ab4, o, states, tinvs, o2 = saved
    S = x.shape[0]
    gmat = _mm_auto(o2, dy, "tn", "gdn_out_g")
    dw_out, dgate = _wout_grad(gmat, W["gdn_out"], gate, name="gdn_out_dw")
    do2 = _mm_auto(dy, W["gdn_out"], "nt", "gdn_out_dx", a_scale=gate)
    do, dz, dout_norm = _gdn_outnorm_bwd(do2, o, z, W["gdn_out_norm"], name="gdn_outnorm_bwd")
    dqkvn, dab4, da_log, ddt_bias, *rode = _gdn_chunk_bwd(
        qkvn, ab4, W["gdn_a_log"], W["gdn_dt_bias"], states, tinvs, do, name="gdn_chunk_bwd", riding=riding)
    dc, dconv8 = _gdn_prep_bwd_pre(dqkvn, pq, W["gdn_conv"], name="gdn_prep_bwd")
    dpq = _gdn_conv_bwd_x(dc, W["gdn_conv"], name="gdn_conv_bwd")
    dab = jnp.transpose(dab4.reshape(2 * GDN_HEADS, S))
    dab = jnp.pad(dab, ((0, 0), (0, LANES - 2 * GDN_HEADS))).astype(BF16)
    dw_qkv = _mm_auto(h, dpq, "tn", "gdn_in_qkv_dw", out_dtype=BF16)
    dw_z = _mm_auto(h, dz, "tn", "gdn_in_z_dw", out_dtype=BF16)
    dw_ab = _mm_auto(h, dab, "tn", "gdn_in_ab_dw", out_dtype=BF16)
    dh = _mm_sum_nt([(dpq, W["gdn_qkv"], 1024, 0), (dz, W["gdn_z"], 1024, 0), (dab, W["gdn_ab"], LANES, 0)],
                    name="gdn_in_dx")
    dx, dsh, dsc, dgain = _norm_mod_bwd(dh, x, dy, gain, sc, name="gdn_norm_bwd")
    dw_in = jnp.concatenate([_un_hm(dw_qkv), dw_z, dw_ab[:, :2 * GDN_HEADS]], axis=1)
    return dx, dict(gdn_w_in=dw_in, gdn_conv=_un_hm(dconv8[:GDN_CONV]), gdn_w_out=dw_out, gdn_out_norm=dout_norm,
                    gdn_a_log=da_log.reshape(1, GDN_HEADS), gdn_dt_bias=ddt_bias.reshape(1, GDN_HEADS),
                    gain=dgain, mod=(dsh, dsc, dgate)), (tuple(rode) if rode else None)


def _dsw_fwd(x, mod, gain, W):
    sh, sc, gate = mod
    h = _norm_mod_fwd(x, gain, sc, sh, name="dsw_norm")
    q, k, v = (_mm_auto(h, W[n], "nn", f"dsw_in_{n[-1]}") for n in ("dsw_q", "dsw_k", "dsw_v"))
    outs = None
    for g in range(len(DSW_GROUPS)):
        outs = _dsw_attn_fwd(q, k, v, W["dsw_bias"][g], W["dsw_q_norm"], W["dsw_k_norm"], outs, g=g,
                             name=f"dsw_attn_{g}")
    o, lse = _dsw_merge(*outs, name="dsw_merge")
    y = _mm_auto(o, W["dsw_out"], "nn", "dsw_out", out_scale=gate, resid=x)
    return y, (x, h, q, k, v, o, lse)


def _dsw_bwd(dy, saved, mod, gain, W):
    sh, sc, gate = mod
    x, h, q, k, v, o, lse = saved
    gmat = _mm_auto(o, dy, "tn", "dsw_out_g")
    dw_out, dgate = _wout_grad(gmat, W["dsw_out"], gate, name="dsw_out_dw")
    do = _mm_auto(dy, W["dsw_out"], "nt", "dsw_out_dx", a_scale=gate)
    G = len(DSW_GROUPS)
    dqkv, dbias, dq_norm, dk_norm = None, [], 0.0, 0.0
    for g in range(G):
        *dqkv, db, dqg, dkg = _dsw_attn_bwd(q, k, v, o, lse, do, W["dsw_bias"][g], W["dsw_q_norm"],
                                            W["dsw_k_norm"], dqkv, g=g, name=f"dsw_attn_bwd_{g}")
        dbias.append(db)
        dq_norm, dk_norm = dq_norm + dqg, dk_norm + dkg
    names = ("dsw_q", "dsw_k", "dsw_v")
    dws = [_mm_auto(h, d, "tn", f"dsw_in_{n[-1]}_dw", out_dtype=BF16) for n, d in zip(names, dqkv)]
    dh = _mm_sum_nt([(d, W[n], _tile(d.shape[1], 1024), 0) for n, d in zip(names, dqkv)], name="dsw_in_dx")
    dx, dsh, dsc, dgain = _norm_mod_bwd(dh, x, dy, gain, sc, name="dsw_norm_bwd")
    hot = _dsw_bucket_onehot()
    drel = [_mm(dbias[g].reshape(DSW_HEADS, -1), hot[g], mode="nn", name=f"dsw_rel_bias_{g}", tm=DSW_HEADS,
                tn=LANES, tk=8192)[:, :REL_BUCKETS] for g in range(G)]
    return dx, dict(dsw_w_in=jnp.concatenate(dws, axis=1), dsw_w_out=dw_out, dsw_q_norm=dq_norm,
                    dsw_k_norm=dk_norm, rel_bias=jnp.transpose(jnp.concatenate(drel, axis=0)),
                    gain=dgain, mod=(dsh, dsc, dgate))


def _local_step(x, target, mod, W, late_weights=None, early_pairs=None):
    mods = [[_row(mod[l, i]) for i in range(6)] for l in range(2)]
    nmix = [_row(W["norm_mix"][l]) for l in range(2)]
    nffn = [_row(W["norm_ffn"][l]) for l in range(2)]
    x1, s_gdn, arrived = _gdn_fwd(x, mods[0][:3], nmix[0], W, None if late_weights is None else late_weights[0])
    if late_weights is not None:
        W = {**W, **late_weights[1](arrived)}
    x2, s_f0 = _ffn_fwd(x1, mods[0][3:], nffn[0], W["w_ffn_in"][0], W["w_ffn_out"][0], "0")
    x3, s_dsw = _dsw_fwd(x2, mods[1][:3], nmix[1], W)
    x4, s_f1 = _ffn_fwd(x3, mods[1][3:], nffn[1], W["w_ffn_in"][1], W["w_ffn_out"][1], "1")
    dx4, sse = _loss_head(x4, target, name="loss_head")
    dx3, g_f1 = _ffn_bwd(dx4, s_f1, mods[1][3:], nffn[1], W["w_ffn_in"][1], W["w_ffn_out"][1], "1")
    dx2, g_dsw = _dsw_bwd(dx3, s_dsw, mods[1][:3], nmix[1], W)
    dx1, g_f0 = _ffn_bwd(dx2, s_f0, mods[0][3:], nffn[0], W["w_ffn_in"][0], W["w_ffn_out"][0], "0")
    grads = dict(
        w_ffn_in=jnp.stack([g_f0["w_in"], g_f1["w_in"]]), w_ffn_out=jnp.stack([g_f0["w_out"], g_f1["w_out"]]),
        dsw_w_in=g_dsw["dsw_w_in"][None], dsw_w_out=g_dsw["dsw_w_out"][None])
    riding = None if early_pairs is None else early_pairs(grads)
    dx0, g_gdn, rode = _gdn_bwd(dx1, s_gdn, mods[0][:3], nmix[0], W, riding)
    dmod = jnp.stack([jnp.concatenate(list(g_gdn["mod"]) + list(g_f0["mod"]), axis=0),
                      jnp.concatenate(list(g_dsw["mod"]) + list(g_f1["mod"]), axis=0)])
    grads.update(
        norm_mix=jnp.concatenate([g_gdn["gain"], g_dsw["gain"]], axis=0),
        norm_ffn=jnp.concatenate([g_f0["gain"], g_f1["gain"]], axis=0),
        gdn_w_in=g_gdn["gdn_w_in"][None], gdn_conv=g_gdn["gdn_conv"][None], gdn_w_out=g_gdn["gdn_w_out"][None],
        gdn_out_norm=g_gdn["gdn_out_norm"], gdn_a_log=g_gdn["gdn_a_log"], gdn_dt_bias=g_gdn["gdn_dt_bias"],
        dsw_q_norm=g_dsw["dsw_q_norm"], dsw_k_norm=g_dsw["dsw_k_norm"], rel_bias=g_dsw["rel_bias"])
    return sse, dx0, grads, dmod, rode


def _prepare_first(full, small):
    gw = full["gdn_w_in"][0]
    hk3 = 3 * GDN_HEADS * GDN_DK
    return dict(
        gdn_qkv=_hm(gw[:, :hk3]), gdn_z=gw[:, hk3:hk3 + GDN_HEADS * GDN_DK],
        gdn_ab=jnp.pad(gw[:, hk3 + GDN_HEADS * GDN_DK:], ((0, 0), (0, LANES - 2 * GDN_HEADS))),
        gdn_conv=_hm(full["gdn_conv"][0]), gdn_out=full["gdn_w_out"][0],
        norm_mix=small["norm_mix"], norm_ffn=small["norm_ffn"],
        gdn_a_log=small["gdn_a_log"].reshape(GDN_HEADS, 1, 1), gdn_dt_bias=small["gdn_dt_bias"].reshape(GDN_HEADS, 1, 1),
        gdn_out_norm=small["gdn_out_norm"], dsw_q_norm=small["dsw_q_norm"], dsw_k_norm=small["dsw_k_norm"],
        dsw_bias=_dsw_bias(small["rel_bias"]))


def _prepare_rest(full):
    di = full["dsw_w_in"][0]
    dq = di.shape[1] // 3
    return dict(w_ffn_in=full["w_ffn_in"], w_ffn_out=full["w_ffn_out"],
                dsw_q=di[:, :dq], dsw_k=di[:, dq:2 * dq], dsw_v=di[:, 2 * dq:], dsw_out=full["dsw_w_out"][0])


def _prepare_weights(full, small):
    return {**_prepare_first(full, small), **_prepare_rest(full)}


_W_NAMES = ("w_ada", "b_ada", "norm_mix", "norm_ffn", "w_ffn_in", "w_ffn_out", "gdn_w_in", "gdn_conv",
            "gdn_a_log", "gdn_dt_bias", "gdn_out_norm", "gdn_w_out", "dsw_w_in", "dsw_q_norm", "dsw_k_norm",
            "dsw_w_out", "rel_bias")
_PAD_BATCH = 16


def _pad_rows(a, rows):
    return jnp.pad(a, ((0, rows - a.shape[0]), (0, 0)))


def kernel(x, c, w_ada, b_ada, norm_mix, norm_ffn, w_ffn_in, w_ffn_out, gdn_w_in, gdn_conv, gdn_a_log, gdn_dt_bias, gdn_out_norm, gdn_w_out, dsw_w_in, dsw_q_norm, dsw_k_norm, dsw_w_out, rel_bias, loss_target, m_w_ada, m_b_ada, m_norm_mix, m_norm_ffn, m_w_ffn_in, m_w_ffn_out, m_gdn_w_in, m_gdn_conv, m_gdn_a_log, m_gdn_dt_bias, m_gdn_out_norm, m_gdn_w_out, m_dsw_w_in, m_dsw_q_norm, m_dsw_k_norm, m_dsw_w_out, m_rel_bias, v_w_ada, v_b_ada, v_norm_mix, v_norm_ffn, v_w_ffn_in, v_w_ffn_out, v_gdn_w_in, v_gdn_conv, v_gdn_a_log, v_gdn_dt_bias, v_gdn_out_norm, v_gdn_w_out, v_dsw_w_in, v_dsw_q_norm, v_dsw_k_norm, v_dsw_w_out, v_rel_bias):
    w = dict(zip(_W_NAMES, (w_ada, b_ada, norm_mix, norm_ffn, w_ffn_in, w_ffn_out, gdn_w_in, gdn_conv, gdn_a_log,
                            gdn_dt_bias, gdn_out_norm, gdn_w_out, dsw_w_in, dsw_q_norm, dsw_k_norm, dsw_w_out,
                            rel_bias)))
    m = dict(zip(_W_NAMES, (m_w_ada, m_b_ada, m_norm_mix, m_norm_ffn, m_w_ffn_in, m_w_ffn_out, m_gdn_w_in,
                            m_gdn_conv, m_gdn_a_log, m_gdn_dt_bias, m_gdn_out_norm, m_gdn_w_out, m_dsw_w_in,
                            m_dsw_q_norm, m_dsw_k_norm, m_dsw_w_out, m_rel_bias)))
    v = dict(zip(_W_NAMES, (v_w_ada, v_b_ada, v_norm_mix, v_norm_ffn, v_w_ffn_in, v_w_ffn_out, v_gdn_w_in,
                            v_gdn_conv, v_gdn_a_log, v_gdn_dt_bias, v_gdn_out_norm, v_gdn_w_out, v_dsw_w_in,
                            v_dsw_q_norm, v_dsw_k_norm, v_dsw_w_out, v_rel_bias)))
    D = x.shape[-1]
    n_layers, _, ada_cols = w_ada.shape

    c_all = _exchange(c.reshape(D // LANES, LANES), gather=True, name="gather_cond").reshape(N_DEV, D)
    c_pad = _pad_rows(c_all, _PAD_BATCH)
    proj = [_mm(c_pad, w_ada[l], mode="nn", name=f"ada_proj_{l}", tm=_PAD_BATCH, tn=ada_cols, tk=D, a_silu=True)
            for l in range(n_layers)]
    mod_send = _pack([(jnp.stack([p[:N_DEV] for p in proj], axis=1), 1)], _ROW_ALIGN)
    mod_recv = _exchange(mod_send, gather=False, name="scatter_mod")
    mod = _unpack(mod_recv, [(n_layers, ada_cols)], 1)[0]
    mod = jnp.transpose(mod, (1, 0, 2)).reshape(n_layers, N_DEV * ada_cols) + b_ada
    mod = mod.reshape(n_layers, 6, D)

    conv_hi = gdn_conv.astype(BF16)
    conv_lo = (gdn_conv - conv_hi.astype(F32)).astype(BF16)
    first_send = _pack([(conv_hi if n == "gdn_conv" else w[n].astype(BF16), 0) for n in _LATE] + [(conv_lo, 0)],
                       _ROW_ALIGN)
    parts = _unpack(_gather_two_level(first_send, name="gather_weights_first"),
                    [w[n].shape for n in _LATE] + [gdn_conv.shape], 1)
    full = {n: _to_natural(parts[i], _SHARD_AXIS[n]) for i, n in enumerate(_LATE)}
    full["gdn_conv"] = full["gdn_conv"].astype(F32) + _to_natural(parts[-1], _SHARD_AXIS["gdn_conv"]).astype(F32)
    W = _prepare_first(full, {n: w[n] for n in _SMALL})
    packed_early = tuple(n for n in _EARLY if n not in _NATIVE)
    rest_send = (_pack([(w[n].astype(BF16), 0) for n in packed_early], _ROW_ALIGN),
                 ) + tuple(w[n].astype(BF16) for n in _NATIVE)

    def rest_weights(arrived):
        filled = _fill_from_sibling(arrived, name="swap_weights")
        by_dev = [a.reshape((N_DEV,) + a.shape[2:]) for a in filled]
        blocks = dict(zip(packed_early, _unpack(by_dev[0], [w[n].shape for n in packed_early], 1)))
        blocks.update(zip(_NATIVE, by_dev[1:]))
        return _prepare_rest({n: _to_natural(blocks[n], _SHARD_AXIS[n]) for n in _EARLY})

    my_c = lax.axis_index("c")

    def pair_sums(g, packed, native, tag):
        sends = [_pack([(_to_blocked(g[n].astype(BF16), _SHARD_AXIS[n]), 1) for n in packed], _BIG_ALIGN)]
        sends += [_to_blocked(g[n].astype(BF16), _SHARD_AXIS[n]).reshape((N_DEV, -1, w[n].shape[-1]))
                  for n in native]
        by_core = [s.reshape((N_DEV // 2, 2) + s.shape[1:]) for s in sends]
        keep = [lax.dynamic_index_in_dim(s, my_c, axis=1, keepdims=False) for s in by_core]
        give = [lax.dynamic_index_in_dim(s, 1 - my_c, axis=1, keepdims=False) for s in by_core]
        got = _swap_with_sibling(give, name=f"swap_grads_{tag}")
        return tuple(_add_pair(k, t, name=f"add_sibling_grads_{tag}_{j}") for j, (k, t) in enumerate(zip(keep, got)))

    sse, grad_x, grads, dmod, early_recv = _local_step(
        x[0], loss_target[0], mod, W, late_weights=(rest_send, rest_weights),
        early_pairs=lambda g: pair_sums(g, packed_early, _NATIVE, "early"))
    loss = lax.psum(0.5 * sse[0, 0] / D, ("x", "y", "c"))
    grads["b_ada"] = dmod.reshape(n_layers, 6 * D)
    late_recv = _exchange_chips(pair_sums(grads, _LATE, (), "late")[0], name="scatter_grads_late")
    g_parts = dict(zip(packed_early, _unpack(early_recv[0], [w[n].shape for n in packed_early], 1)))
    g_parts.update(zip(_NATIVE, early_recv[1:]))
    g_parts.update(zip(_LATE, _unpack(late_recv, [w[n].shape for n in _LATE], 1)))

    dmod_send = _pack([(jnp.transpose(dmod.reshape(n_layers, N_DEV, ada_cols), (1, 0, 2)), 1)], _ROW_ALIGN)
    small_send = _pack([(grads[n].reshape(w[n].shape), 0) for n in _SMALL], _ROW_ALIGN)
    s_recv = _exchange(jnp.concatenate(
        [dmod_send, jnp.broadcast_to(small_send[None], (N_DEV,) + small_send.shape)], axis=1),
        gather=False, name="scatter_small")
    dmod_rows = dmod_send.shape[1]

    out = {}
    kinds = ("grad", "delta", "new_m", "new_v")
    for n in _BIG:
        g4 = g_parts[n]
        rows2d = lambda a: a.reshape((-1, w[n].shape[-1]))
        res = _adamw(rows2d(w[n]), g4.reshape((g4.shape[0], -1, w[n].shape[-1])), rows2d(m[n]), rows2d(v[n]),
                     name=f"adamw_{n}")
        for kind, buf in zip(kinds, res):
            out[kind, n] = buf.reshape(w[n].shape)

    dmod_all = _unpack(lax.slice_in_dim(s_recv, 0, dmod_rows, axis=1), [(n_layers, ada_cols)], 1)[0]
    g_ada = jnp.stack([_mm(c_pad, _pad_rows(dmod_all[:, l], _PAD_BATCH), mode="tn", name=f"ada_dw_{l}",
                           tm=D, tn=ada_cols, tk=_PAD_BATCH, a_silu=True) for l in range(n_layers)])
    flat = lambda a: a.reshape(n_layers * D, ada_cols)
    res = _adamw(flat(w_ada), flat(g_ada)[None], flat(m_w_ada), flat(v_w_ada), name="adamw_ada")
    for kind, buf in zip(("grad", "delta", "new_m", "new_v"), res):
        out[kind, "w_ada"] = buf.reshape(w_ada.shape)

    small_parts = lax.slice_in_dim(s_recv, dmod_rows, s_recv.shape[1], axis=1)
    packed = [_pack([(t[n], 0) for n in _SMALL], _ROW_ALIGN) for t in (w, m, v)]
    res = _adamw(packed[0], small_parts, packed[1], packed[2], name="adamw_replicated")
    for kind, buf in zip(("grad", "delta", "new_m", "new_v"), res):
        for n, a in zip(_SMALL, _unpack(buf, [w[n].shape for n in _SMALL], 0)):
            out[kind, n] = a

    return (loss, grad_x[None]) + tuple(out[kind, n] for kind in ("grad", "delta", "new_m", "new_v")
                                        for n in _W_NAMES)
```

```python
import functools
import math

import numpy as np
import jax
import jax.numpy as jnp
from jax import lax
from jax.experimental import pallas as pl
from jax.experimental.pallas import tpu as pltpu

F32 = jnp.float32
BF16 = jnp.bfloat16

N_DEV = 8
RMS_EPS = 1e-6
LANES = 128
V7X_VMEM_LIMIT = 48 * 1024 * 1024

GDN_HEADS = 8
GDN_DK = 128
GDN_CHUNK = 64
GDN_CONV = 4
DSW_GROUPS = ((128, 1), (512, 4), (2048, 16))
DSW_HEADS = 8
DSW_DH = 64
DSW_BLK = 128
REL_BUCKETS = 32
REL_MAX_DIST = 2048

ADAM_LR = 0.001
ADAM_B1 = 0.9
ADAM_B2 = 0.999
ADAM_EPS = 1e-08
ADAM_WD = 0.01
ADAM_STEP = 10

NEG_BIG = -1e30


def _params(*sem):
    return pltpu.CompilerParams(dimension_semantics=sem, vmem_limit_bytes=V7X_VMEM_LIMIT)


def _sigmoid(x):
    return 1.0 / (1.0 + jnp.exp(-x))


def _silu(x):
    return x * _sigmoid(x)


_DOT_DIMS = {
    "nn": (((1,), (0,)), ((), ())),
    "nt": (((1,), (1,)), ((), ())),
    "tn": (((0,), (0,)), ((), ())),
}


def _mm(a, b, *, mode, name, tm, tn, tk, out_dtype=F32, a_scale=None, out_scale=None, resid=None, a_silu=False):
    if mode == "nn":
        (M, K), N = a.shape, b.shape[1]
    elif mode == "nt":
        (M, K), N = a.shape, b.shape[0]
    else:
        (K, M), N = a.shape, b.shape[1]
    tm, tn, tk = min(tm, M), min(tn, N), min(tk, K)
    assert M % tm == 0 and N % tn == 0 and K % tk == 0, (name, M, N, K, tm, tn, tk)
    nk = K // tk

    def body(*refs):
        refs = list(refs)
        a_ref, b_ref = refs.pop(0), refs.pop(0)
        as_ref = refs.pop(0) if a_scale is not None else None
        os_ref = refs.pop(0) if out_scale is not None else None
        r_ref = refs.pop(0) if resid is not None else None
        o_ref = refs.pop(0)
        acc_ref = refs.pop(0) if nk > 1 else None

        av = a_ref[...]
        if a_silu:
            av = _silu(av.astype(F32))
        if as_ref is not None:
            av = av.astype(F32) * as_ref[...]
        part = lax.dot_general(av.astype(BF16), b_ref[...].astype(BF16), _DOT_DIMS[mode],
                               preferred_element_type=F32)

        def finish(r):
            if os_ref is not None:
                r = r * os_ref[...]
            if r_ref is not None:
                r = r + r_ref[...].astype(F32)
            o_ref[...] = r.astype(out_dtype)

        if nk == 1:
            finish(part)
        else:
            k = pl.program_id(2)

            @pl.when(k == 0)
            def _():
                acc_ref[...] = part

            @pl.when(k > 0)
            def _():
                acc_ref[...] += part

            @pl.when(k == nk - 1)
            def _():
                finish(acc_ref[...])

    if mode == "nn":
        a_spec = pl.BlockSpec((tm, tk), lambda i, j, k: (i, k))
        b_spec = pl.BlockSpec((tk, tn), lambda i, j, k: (k, j))
        as_spec = pl.BlockSpec((1, tk), lambda i, j, k: (0, k))
    elif mode == "nt":
        a_spec = pl.BlockSpec((tm, tk), lambda i, j, k: (i, k))
        b_spec = pl.BlockSpec((tn, tk), lambda i, j, k: (j, k))
        as_spec = pl.BlockSpec((1, tk), lambda i, j, k: (0, k))
    else:
        a_spec = pl.BlockSpec((tk, tm), lambda i, j, k: (k, i))
        b_spec = pl.BlockSpec((tk, tn), lambda i, j, k: (k, j))
        as_spec = None
    in_specs, args = [a_spec, b_spec], [a, b]
    if a_scale is not None:
        in_specs.append(as_spec)
        args.append(a_scale)
    if out_scale is not None:
        in_specs.append(pl.BlockSpec((1, tn), lambda i, j, k: (0, j)))
        args.append(out_scale)
    if resid is not None:
        in_specs.append(pl.BlockSpec((tm, tn), lambda i, j, k: (i, j)))
        args.append(resid)
    return pl.pallas_call(
        body, name=name, grid=(M // tm, N // tn, nk),
        in_specs=in_specs, out_specs=pl.BlockSpec((tm, tn), lambda i, j, k: (i, j)),
        out_shape=jax.ShapeDtypeStruct((M, N), out_dtype),
        scratch_shapes=[pltpu.VMEM((tm, tn), F32)] if nk > 1 else [],
        compiler_params=_params("parallel", "parallel", "arbitrary"),
    )(*args)


def _mm_sum_nt(pairs, *, name, tm=512, tn=1024):
    M, N = pairs[0][0].shape[0], pairs[0][1].shape[0]
    tm, tn = _tile(M, tm), _tile(N, tn)
    spans, start = [], 0
    for a, b, tk, off in pairs:
        K = a.shape[1]
        assert a.shape[0] == M and b.shape[0] == N and K % tk == 0 and off % tk == 0, name
        spans.append((start, K // tk, tk, off // tk))
        start += K // tk
    total = start

    def body(*refs):
        o_ref, acc_ref = refs[-2:]
        k = pl.program_id(2)

        @pl.when(k == 0)
        def _():
            acc_ref[...] = jnp.zeros_like(acc_ref)

        for p, (s0, nk, _, _) in enumerate(spans):
            a_ref, b_ref = refs[2 * p], refs[2 * p + 1]

            @pl.when((k >= s0) & (k < s0 + nk))
            def _():
                acc_ref[...] += lax.dot_general(a_ref[...].astype(BF16), b_ref[...].astype(BF16), _DOT_DIMS["nt"],
                                                preferred_element_type=F32)

        @pl.when(k == total - 1)
        def _():
            o_ref[...] = acc_ref[...]

    def spec(rows, tk, s0, nk, koff, axis):
        def index(i, j, k):
            return ((i, j)[axis], jnp.clip(k - s0, 0, nk - 1) + koff)
        return pl.BlockSpec((rows, tk), index)

    in_specs, args = [], []
    for (a, b, _, _), (s0, nk, tk, koff) in zip(pairs, spans):
        in_specs += [spec(tm, tk, s0, nk, 0, 0), spec(tn, tk, s0, nk, koff, 1)]
        args += [a, b]
    return pl.pallas_call(
        body, name=name, grid=(M // tm, N // tn, total), in_specs=in_specs,
        out_specs=pl.BlockSpec((tm, tn), lambda i, j, k: (i, j)),
        out_shape=jax.ShapeDtypeStruct((M, N), F32), scratch_shapes=[pltpu.VMEM((tm, tn), F32)],
        compiler_params=_params("parallel", "parallel", "arbitrary"),
    )(*args)


def _norm_mod_fwd(x, gain, sc, sh, *, name):
    S, D = x.shape
    tr = min(512, S)

    def body(x_ref, g_ref, sc_ref, sh_ref, h_ref):
        xv = x_ref[...]
        r = lax.rsqrt(jnp.mean(xv * xv, axis=-1, keepdims=True) + RMS_EPS)
        h_ref[...] = ((xv * r) * g_ref[...] * (1.0 + sc_ref[...]) + sh_ref[...]).astype(BF16)

    row = pl.BlockSpec((tr, D), lambda i: (i, 0))
    vec = pl.BlockSpec((1, D), lambda i: (0, 0))
    return pl.pallas_call(
        body, name=name, grid=(S // tr,), in_specs=[row, vec, vec, vec], out_specs=row,
        out_shape=jax.ShapeDtypeStruct((S, D), BF16), compiler_params=_params("parallel"),
    )(x, gain, sc, sh)


def _norm_mod_bwd(dh, x, dx_res, gain, sc, *, name):
    S, D = x.shape
    tr = min(256, S)
    n_steps = S // tr

    def body(dh_ref, x_ref, dxr_ref, g_ref, sc_ref, dx_ref, dsh_ref, dsc_ref, dgain_ref, acc_sh, acc_a):
        i = pl.program_id(0)
        xv = x_ref[...]
        r = lax.rsqrt(jnp.mean(xv * xv, axis=-1, keepdims=True) + RMS_EPS)
        n = xv * r
        dhv = dh_ref[...].astype(F32)
        dn = dhv * (g_ref[...] * (1.0 + sc_ref[...]))
        dx_ref[...] = dxr_ref[...] + r * (dn - n * jnp.mean(dn * n, axis=-1, keepdims=True))
        p_sh = jnp.sum(dhv, axis=0, keepdims=True)
        p_a = jnp.sum(dhv * n, axis=0, keepdims=True)

        @pl.when(i == 0)
        def _():
            acc_sh[...] = p_sh
            acc_a[...] = p_a

        @pl.when(i > 0)
        def _():
            acc_sh[...] += p_sh
            acc_a[...] += p_a

        @pl.when(i == n_steps - 1)
        def _():
            dsh_ref[...] = acc_sh[...]
            dsc_ref[...] = acc_a[...] * g_ref[...]
            dgain_ref[...] = acc_a[...] * (1.0 + sc_ref[...])

    row = pl.BlockSpec((tr, D), lambda i: (i, 0))
    vec = pl.BlockSpec((1, D), lambda i: (0, 0))
    vshape = jax.ShapeDtypeStruct((1, D), F32)
    return pl.pallas_call(
        body, name=name, grid=(n_steps,), in_specs=[row, row, row, vec, vec],
        out_specs=[row, vec, vec, vec],
        out_shape=[jax.ShapeDtypeStruct((S, D), F32), vshape, vshape, vshape],
        scratch_shapes=[pltpu.VMEM((1, D), F32), pltpu.VMEM((1, D), F32)],
        compiler_params=_params("arbitrary"),
    )(dh, x, dx_res, gain, sc)


def _wout_grad(gmat, w, gate, *, name):
    K, D = w.shape
    tr = min(256, K)
    n_steps = K // tr

    def body(g_ref, w_ref, gate_ref, dw_ref, dgate_ref, acc):
        i = pl.program_id(0)
        gv = g_ref[...]
        dw_ref[...] = (gv * gate_ref[...]).astype(BF16)
        part = jnp.sum(gv * w_ref[...], axis=0, keepdims=True)

        @pl.when(i == 0)
        def _():
            acc[...] = part

        @pl.when(i > 0)
        def _():
            acc[...] += part

        @pl.when(i == n_steps - 1)
        def _():
            dgate_ref[...] = acc[...]

    row = pl.BlockSpec((tr, D), lambda i: (i, 0))
    vec = pl.BlockSpec((1, D), lambda i: (0, 0))
    return pl.pallas_call(
        body, name=name, grid=(n_steps,), in_specs=[row, row, vec], out_specs=[row, vec],
        out_shape=[jax.ShapeDtypeStruct((K, D), BF16), jax.ShapeDtypeStruct((1, D), F32)],
        scratch_shapes=[pltpu.VMEM((1, D), F32)], compiler_params=_params("arbitrary"),
    )(gmat, w, gate)


def _loss_head(y, target, *, name):
    S, D = y.shape
    tr = min(512, S)
    n_steps = S // tr

    def body(y_ref, t_ref, dy_ref, sse_ref, acc):
        i = pl.program_id(0)
        e = y_ref[...] - t_ref[...]
        dy_ref[...] = e * (1.0 / D)
        part = jnp.sum(e * e, axis=0, keepdims=True)

        @pl.when(i == 0)
        def _():
            acc[...] = part

        @pl.when(i > 0)
        def _():
            acc[...] += part

        @pl.when(i == n_steps - 1)
        def _():
            sse_ref[...] = jnp.sum(acc[...], axis=1, keepdims=True)

    row = pl.BlockSpec((tr, D), lambda i: (i, 0))
    return pl.pallas_call(
        body, name=name, grid=(n_steps,), in_specs=[row, row],
        out_specs=[row, pl.BlockSpec((1, 1), lambda i: (0, 0))],
        out_shape=[jax.ShapeDtypeStruct((S, D), F32), jax.ShapeDtypeStruct((1, 1), F32)],
        scratch_shapes=[pltpu.VMEM((1, D), F32)], compiler_params=_params("arbitrary"),
    )(y, target)


def _adamw(w, g_parts, m, v, *, name):
    R, C = w.shape
    P = g_parts.shape[0]
    tr = _tile(R, max(8, 1024 * LANES // C))
    c1 = 1.0 / (1.0 - ADAM_B1 ** ADAM_STEP)
    c2 = 1.0 / (1.0 - ADAM_B2 ** ADAM_STEP)

    def body(w_ref, g_ref, m_ref, v_ref, go_ref, d_ref, mo_ref, vo_ref):
        g = g_ref[0].astype(F32)
        for q in range(1, P):
            g = g + g_ref[q].astype(F32)
        mn = ADAM_B1 * m_ref[...] + (1.0 - ADAM_B1) * g
        vn = ADAM_B2 * v_ref[...] + (1.0 - ADAM_B2) * (g * g)
        go_ref[...] = g
        mo_ref[...] = mn
        vo_ref[...] = vn
        d_ref[...] = -ADAM_LR * ((mn * c1) / (jnp.sqrt(vn * c2) + ADAM_EPS) + ADAM_WD * w_ref[...])

    row = pl.BlockSpec((tr, C), lambda i: (i, 0))
    shp = jax.ShapeDtypeStruct((R, C), F32)
    return pl.pallas_call(
        body, name=name, grid=(R // tr,),
        in_specs=[row, pl.BlockSpec((P, tr, C), lambda i: (0, i, 0)), row, row],
        out_specs=[row, row, row, row], out_shape=[shp, shp, shp, shp],
        compiler_params=_params("parallel"),
    )(w, g_parts, m, v)


_HALO = 16


def _conv_taps(buf, w_ref, rows, cols):
    acc = None
    for j in range(GDN_CONV):
        term = buf[pl.ds(_HALO - (GDN_CONV - 1) + j, rows), cols] * w_ref[j:j + 1, cols]
        acc = term if acc is None else acc + term
    return acc


def _fill_conv_buf(buf, halo_ref, x_ref, rows, first):
    buf[0:_HALO, :] = jnp.where(first, 0.0, halo_ref[...].astype(F32))
    buf[_HALO:_HALO + rows, :] = x_ref[...].astype(F32)


_HM = 3 * GDN_DK
_GDN_ROWS = 256
_PREP_HEADS = 4


def _l2n(seg):
    return lax.rsqrt(jnp.sum(seg * seg, axis=-1, keepdims=True) + RMS_EPS)


def _head_cols(hh):
    return slice(hh * _HM, (hh + 1) * _HM)


def _gdn_prep_fwd(x, conv_w, *, name):
    S, C3 = x.shape
    CB = _PREP_HEADS * _HM
    RB = min(256, S)

    def body(x_ref, halo_ref, w_ref, o_ref, buf):
        i = pl.program_id(0)
        _fill_conv_buf(buf, halo_ref, x_ref, RB, i == 0)
        for hh in range(_PREP_HEADS):
            c0 = hh * _HM
            y = _silu(_conv_taps(buf, w_ref, RB, _head_cols(hh)))
            q, k = y[:, :GDN_DK], y[:, GDN_DK:2 * GDN_DK]
            o_ref[:, c0:c0 + GDN_DK] = q * (_l2n(q) * GDN_DK ** -0.5)
            o_ref[:, c0 + GDN_DK:c0 + 2 * GDN_DK] = k * _l2n(k)
            o_ref[:, c0 + 2 * GDN_DK:c0 + _HM] = y[:, 2 * GDN_DK:]

    hb = RB // _HALO
    return pl.pallas_call(
        body, name=name, grid=(S // RB, C3 // CB),
        in_specs=[pl.BlockSpec((RB, CB), lambda i, j: (i, j)),
                  pl.BlockSpec((_HALO, CB), lambda i, j: (jnp.maximum(i * hb - 1, 0), j)),
                  pl.BlockSpec((GDN_CONV, CB), lambda i, j: (0, j))],
        out_specs=pl.BlockSpec((RB, CB), lambda i, j: (i, j)),
        out_shape=jax.ShapeDtypeStruct((S, C3), F32),
        scratch_shapes=[pltpu.VMEM((RB + _HALO, CB), F32)],
        compiler_params=_params("parallel", "parallel"),
    )(x, x, conv_w)


def _gdn_prep_bwd_pre(dn, x, conv_w, *, name):
    S, C3 = x.shape
    CB = _PREP_HEADS * _HM
    RB = min(256, S)
    n_steps = S // RB

    def body(dn_ref, x_ref, halo_ref, w_ref, dc_ref, dw_ref, buf):
        i = pl.program_id(1)
        _fill_conv_buf(buf, halo_ref, x_ref, RB, i == 0)
        head_parts = []
        for hh in range(_PREP_HEADS):
            c0, cols = hh * _HM, _head_cols(hh)
            acc = _conv_taps(buf, w_ref, RB, cols)
            sg = _sigmoid(acc)
            y = acc * sg
            dsilu = sg * (1.0 + acc * (1.0 - sg))
            for part, scale in ((0, GDN_DK ** -0.5), (1, 1.0)):
                sl = slice(part * GDN_DK, (part + 1) * GDN_DK)
                seg = y[:, sl]
                r = _l2n(seg)
                n = seg * r
                d = dn_ref[:, c0 + part * GDN_DK:c0 + (part + 1) * GDN_DK] * scale
                dc_ref[:, c0 + part * GDN_DK:c0 + (part + 1) * GDN_DK] = (
                    r * (d - n * jnp.sum(d * n, axis=-1, keepdims=True)) * dsilu[:, sl])
            dc_ref[:, c0 + 2 * GDN_DK:c0 + _HM] = dn_ref[:, c0 + 2 * GDN_DK:c0 + _HM] * dsilu[:, 2 * GDN_DK:]
            dc = dc_ref[:, cols]
            taps = [jnp.sum(dc * buf[pl.ds(_HALO - (GDN_CONV - 1) + t, RB), cols], axis=0, keepdims=True)
                    for t in range(GDN_CONV)]
            head_parts.append(jnp.concatenate(taps + [jnp.zeros((8 - GDN_CONV, _HM), F32)], axis=0))
        part = jnp.concatenate(head_parts, axis=1)

        @pl.when(i == 0)
        def _():
            dw_ref[...] = part

        @pl.when(i > 0)
        def _():
            dw_ref[...] += part

    hb = RB // _HALO
    return pl.pallas_call(
        body, name=name, grid=(C3 // CB, n_steps),
        in_specs=[pl.BlockSpec((RB, CB), lambda j, i: (i, j)),
                  pl.BlockSpec((RB, CB), lambda j, i: (i, j)),
                  pl.BlockSpec((_HALO, CB), lambda j, i: (jnp.maximum(i * hb - 1, 0), j)),
                  pl.BlockSpec((GDN_CONV, CB), lambda j, i: (0, j))],
        out_specs=[pl.BlockSpec((RB, CB), lambda j, i: (i, j)),
                   pl.BlockSpec((8, CB), lambda j, i: (0, j))],
        out_shape=[jax.ShapeDtypeStruct((S, C3), F32), jax.ShapeDtypeStruct((8, C3), F32)],
        scratch_shapes=[pltpu.VMEM((RB + _HALO, CB), F32)],
        compiler_params=_params("parallel", "arbitrary"),
    )(dn, x, x, conv_w)


def _gdn_conv_bwd_x(dc, conv_w, *, name):
    S, C3 = dc.shape
    CB = _PREP_HEADS * _HM
    RB = min(256, S)
    n_steps = S // RB

    def body(dc_ref, halo_ref, w_ref, dx_ref, buf):
        i = pl.program_id(0)
        buf[0:RB, :] = dc_ref[...]
        buf[RB:RB + _HALO, :] = jnp.where(i == n_steps - 1, 0.0, halo_ref[...])
        for hh in range(_PREP_HEADS):
            cols = _head_cols(hh)
            acc = None
            for j in range(GDN_CONV):
                term = buf[pl.ds(GDN_CONV - 1 - j, RB), cols] * w_ref[j:j + 1, cols]
                acc = term if acc is None else acc + term
            dx_ref[:, cols] = acc.astype(BF16)

    hb = RB // _HALO
    last = S // _HALO - 1
    return pl.pallas_call(
        body, name=name, grid=(n_steps, C3 // CB),
        in_specs=[pl.BlockSpec((RB, CB), lambda i, j: (i, j)),
                  pl.BlockSpec((_HALO, CB), lambda i, j: (jnp.minimum((i + 1) * hb, last), j)),
                  pl.BlockSpec((GDN_CONV, CB), lambda i, j: (0, j))],
        out_specs=pl.BlockSpec((RB, CB), lambda i, j: (i, j)),
        out_shape=jax.ShapeDtypeStruct((S, C3), BF16),
        scratch_shapes=[pltpu.VMEM((RB + _HALO, CB), F32)],
        compiler_params=_params("parallel", "parallel"),
    )(dc, dc, conv_w)


def _split_bf16(a):
    hi = a.astype(BF16)
    return hi, (a - hi.astype(F32)).astype(BF16)


def _dot(a, b, dims="nn", exact=False):
    def dot(p, q):
        return lax.dot_general(p, q, _DOT_DIMS[dims], preferred_element_type=F32)

    if exact:
        (ah, al), (bh, bl) = _split_bf16(a), _split_bf16(b)
        return dot(ah, bh) + (dot(ah, bl) + dot(al, bh))
    return dot(a.astype(BF16), b.astype(BF16))


def _softplus(x):
    return jnp.maximum(x, 0.0) + jnp.log(1.0 + jnp.exp(-jnp.abs(x)))


def _to_col(row, eye):
    return jnp.sum(jnp.where(eye, row, 0.0), axis=1, keepdims=True)


def _to_row(col, eye):
    return jnp.sum(jnp.where(eye, col, 0.0), axis=0, keepdims=True)


def _unit_lower_inverse(low, ri, ci):
    n = range(len(low))
    C = low[0].shape[0]
    eye = jnp.where(ri == ci, 1.0, 0.0)
    pair = (ri >> 1) == (ci >> 1)
    x = [eye - jnp.where(pair, low[j], 0.0) for j in n]
    m, sh = 2, 1
    while m < C:
        join = ((ri >> (sh + 1)) == (ci >> (sh + 1))) & (((ri >> sh) & 1) == 1) & (((ci >> sh) & 1) == 0)
        y = [_dot(x[j], jnp.where(join, low[j], 0.0)) for j in n]
        x = [x[j] - _dot(y[j], x[j]) for j in n]
        m, sh = 2 * m, sh + 1
    lx = [_dot(low[j], x[j], exact=True) for j in n]
    corr = [_dot(x[j], eye - x[j] - lx[j]) for j in n]
    return [x[j] + corr[j] for j in n]


def _gdn_local_batch(qkv, g_row, beta_row, ri, ci):
    n = range(len(qkv))
    eye, tril, strict = ri == ci, ri >= ci, ri > ci
    q = [qkv[j][:, :GDN_DK] for j in n]
    k = [qkv[j][:, GDN_DK:2 * GDN_DK] for j in n]
    v = [qkv[j][:, 2 * GDN_DK:] for j in n]
    g_col = [_to_col(g_row[j], eye) for j in n]
    beta_col = [_to_col(beta_row[j], eye) for j in n]
    gc_col = [jnp.sum(jnp.where(tril, g_row[j], 0.0), axis=1, keepdims=True) for j in n]
    gc_row = [jnp.sum(jnp.where(ri <= ci, g_col[j], 0.0), axis=0, keepdims=True) for j in n]
    g_last = [jnp.sum(g_row[j], axis=1, keepdims=True) for j in n]
    decay = [jnp.where(tril, jnp.exp(jnp.minimum(gc_col[j] - gc_row[j], 0.0)), 0.0) for j in n]
    e_col = [jnp.exp(gc_col[j]) for j in n]
    f_col = [jnp.exp(g_last[j] - gc_col[j]) for j in n]
    e_last = [jnp.exp(g_last[j]) for j in n]
    kb = [k[j] * beta_col[j] for j in n]
    vb = [v[j] * beta_col[j] for j in n]
    kk = [_dot(kb[j], k[j], "nt") for j in n]
    qk = [_dot(q[j], k[j], "nt") for j in n]
    low = [jnp.where(strict, kk[j] * decay[j], 0.0) for j in n]
    att = [qk[j] * decay[j] for j in n]
    return dict(q=q, k=k, v=v, beta_col=beta_col, decay=decay, e_col=e_col, f_col=f_col, e_last=e_last,
                kb=kb, vb=vb, low=low, att=att, eye=eye, strict=strict, tril=tril)


def _chunk_iotas():
    C = GDN_CHUNK
    return lax.broadcasted_iota(jnp.int32, (C, C), 0), lax.broadcasted_iota(jnp.int32, (C, C), 1)


def _gdn_chunk_fwd(qkv, ab, a_log, dt_bias, *, name, riding=None):
    S = qkv.shape[0]
    H, C, DK = GDN_HEADS, GDN_CHUNK, GDN_DK
    RB = min(_GDN_ROWS, S)
    NCB, NB, NC = RB // C, S // RB, S // C
    heads = range(H)

    def body(qkv_ref, ab_ref, alog_ref, dtb_ref, *rest):
        n_ride = 0 if riding is None else len(riding)
        ride_srcs, rest = rest[:n_ride], rest[n_ride:]
        (o_ref, st_ref, t_ref), rest = rest[:3], rest[3:]
        ride_dsts, rest = rest[:n_ride], rest[n_ride:]
        state, u_s, w_s, qe_s, kf_s, att_s, *ride_sems = rest
        nb = pl.program_id(0)
        if riding is not None:
            finish_ride = _ride(nb == 0, nb == NB - 1, ride_srcs, ride_dsts, ride_sems, True)

        @pl.when(nb == 0)
        def _():
            state[...] = jnp.zeros_like(state)

        ri, ci = _chunk_iotas()
        neg_a = [-jnp.exp(alog_ref[h]) for h in heads]
        e_last = []
        for c in range(NCB):
            rows = pl.ds(c * C, C)
            g_row = [neg_a[h] * _softplus(ab_ref[h, c] + dtb_ref[h]) for h in heads]
            beta_row = [_sigmoid(ab_ref[H + h, c]) for h in heads]
            L = _gdn_local_batch([qkv_ref[rows, h * _HM:(h + 1) * _HM] for h in heads], g_row, beta_row, ri, ci)
            tinv = _unit_lower_inverse(L["low"], ri, ci)
            u = [_dot(tinv[h], L["vb"][h], exact=True) for h in heads]
            w = [_dot(tinv[h], L["kb"][h] * L["e_col"][h], exact=True) for h in heads]
            for h in heads:
                t_ref[h, c] = tinv[h]
                u_s[c, h] = u[h]
                w_s[c, h] = w[h].astype(BF16)
                qe_s[c, h] = (L["q"][h] * L["e_col"][h]).astype(BF16)
                kf_s[c, h] = (L["k"][h] * L["f_col"][h]).astype(BF16)
                att_s[c, h] = L["att"][h].astype(BF16)
            e_last.append(L["e_last"])
        st = [state[h] for h in heads]
        for c in range(NCB):
            rows = pl.ds(c * C, C)
            stb = [st[h].astype(BF16) for h in heads]
            vn = [u_s[c, h] - _dot(w_s[c, h], stb[h]) for h in heads]
            vnb = [vn[h].astype(BF16) for h in heads]
            out = [_dot(qe_s[c, h], stb[h]) + _dot(att_s[c, h], vnb[h]) for h in heads]
            new = [st[h] * e_last[c][h] + _dot(kf_s[c, h], vnb[h], "tn") for h in heads]
            for h in heads:
                o_ref[rows, h * DK:(h + 1) * DK] = out[h]
                st_ref[h, c] = st[h]
            st = new
        for h in heads:
            state[h] = st[h]
        if riding is not None:
            finish_ride()

    ride_args, ride_specs, ride_out, ride_scratch = _riding(riding, True)
    return pl.pallas_call(
        body, name=name, grid=(NB,),
        in_specs=[pl.BlockSpec((RB, H * _HM), lambda n: (n, 0)),
                  pl.BlockSpec((2 * H, NCB, 1, C), lambda n: (0, n, 0, 0)),
                  pl.BlockSpec((H, 1, 1), lambda n: (0, 0, 0)),
                  pl.BlockSpec((H, 1, 1), lambda n: (0, 0, 0))] + ride_specs,
        out_specs=[pl.BlockSpec((RB, H * DK), lambda n: (n, 0)),
                   pl.BlockSpec((H, NCB, DK, DK), lambda n: (0, n, 0, 0)),
                   pl.BlockSpec((H, NCB, C, C), lambda n: (0, n, 0, 0))] + ride_specs,
        out_shape=[jax.ShapeDtypeStruct((S, H * DK), F32),
                   jax.ShapeDtypeStruct((H, NC, DK, DK), F32),
                   jax.ShapeDtypeStruct((H, NC, C, C), F32)] + ride_out,
        scratch_shapes=[pltpu.VMEM((H, DK, DK), F32), pltpu.VMEM((NCB, H, C, DK), F32),
                        pltpu.VMEM((NCB, H, C, DK), BF16), pltpu.VMEM((NCB, H, C, DK), BF16),
                        pltpu.VMEM((NCB, H, C, DK), BF16), pltpu.VMEM((NCB, H, C, C), BF16)] + ride_scratch,
        compiler_params=_params("arbitrary"),
    )(qkv, ab, a_log, dt_bias, *ride_args)


_CHIP_PEERS = N_DEV // 2 - 1


def _chip_copies(src_refs, dst_refs, send_sems, recv_sems, local_sems, gather=False):
    x, y, c = lax.axis_index("x"), lax.axis_index("y"), lax.axis_index("c")
    here = 2 * x + y
    copies = []
    for a, (src_ref, dst_ref) in enumerate(zip(src_refs, dst_refs)):
        landing = dst_ref.at[here, c] if gather else dst_ref.at[here]
        copies.append(pltpu.make_async_copy(src_ref if gather else src_ref.at[here], landing, local_sems.at[a]))
        for rel in range(1, N_DEV // 2):
            px = 1 - x if rel & 2 else x
            py = 1 - y if rel & 1 else y
            k = a * _CHIP_PEERS + rel - 1
            copies.append(pltpu.make_async_remote_copy(
                src_ref=src_ref if gather else src_ref.at[2 * px + py], dst_ref=landing,
                send_sem=send_sems.at[k], recv_sem=recv_sems.at[k],
                device_id=(px, py, c), device_id_type=pl.DeviceIdType.MESH))
    return copies


def _chip_sems(n):
    return [pltpu.SemaphoreType.DMA((n * _CHIP_PEERS,)), pltpu.SemaphoreType.DMA((n * _CHIP_PEERS,)),
            pltpu.SemaphoreType.DMA((n,))]


def _riding(riding, gather):
    if riding is None:
        return [], [], [], []
    shapes = [jax.ShapeDtypeStruct(((N_DEV // 2, 2) + r.shape) if gather else r.shape, r.dtype) for r in riding]
    return list(riding), [pl.BlockSpec(memory_space=pl.ANY)] * len(riding), shapes, _chip_sems(len(riding))


def _ride(first, last, srcs, dsts, sems, gather):
    @pl.when(first)
    def _():
        for cp in _chip_copies(srcs, dsts, *sems, gather=gather):
            cp.start()

    def finish():
        @pl.when(last)
        def _():
            for cp in _chip_copies(srcs, dsts, *sems, gather=gather):
                cp.wait()

    return finish


def _gdn_chunk_bwd(qkv, ab, a_log, dt_bias, states, tinvs, do, *, name, riding=None):
    S = qkv.shape[0]
    H, C, DK = GDN_HEADS, GDN_CHUNK, GDN_DK
    RB = min(_GDN_ROWS, S)
    NCB, NB, NC = RB // C, S // RB, S // C
    heads = range(H)

    def body(qkv_ref, ab_ref, alog_ref, dtb_ref, st_ref, t_ref, do_ref, *rest):
        n_ride = 0 if riding is None else len(riding)
        ride_srcs, rest = rest[:n_ride], rest[n_ride:]
        (dqkv_ref, dab_ref, dalog_ref, ddtb_ref), rest = rest[:4], rest[4:]
        ride_dsts, rest = rest[:n_ride], rest[n_ride:]
        dstate, w_s, vn_s, qe_s, kf_s, att_s, dvn_s, dkf_s, *ride_sems = rest
        nb = pl.program_id(0)
        if riding is not None:
            finish_ride = _ride(nb == 0, nb == NB - 1, ride_srcs, ride_dsts, ride_sems, False)

        @pl.when(nb == 0)
        def _():
            dstate[...] = jnp.zeros_like(dstate)
            dalog_ref[...] = jnp.zeros_like(dalog_ref)
            ddtb_ref[...] = jnp.zeros_like(ddtb_ref)

        ri, ci = _chunk_iotas()
        neg_a = [-jnp.exp(alog_ref[h]) for h in heads]

        def local(c):
            rows = pl.ds(c * C, C)
            a_pre = [ab_ref[h, c] + dtb_ref[h] for h in heads]
            g_row = [neg_a[h] * _softplus(a_pre[h]) for h in heads]
            beta_row = [_sigmoid(ab_ref[H + h, c]) for h in heads]
            L = _gdn_local_batch([qkv_ref[rows, h * _HM:(h + 1) * _HM] for h in heads], g_row, beta_row, ri, ci)
            return L, a_pre, g_row, beta_row

        e_last = [None] * NCB
        for c in range(NCB):
            L, _, _, _ = local(c)
            kbe = [L["kb"][h] * L["e_col"][h] for h in heads]
            u = [_dot(t_ref[h, c], L["vb"][h], exact=True) for h in heads]
            w = [_dot(t_ref[h, c], kbe[h], exact=True) for h in heads]
            vn = [u[h] - _dot(w[h], st_ref[h, c]) for h in heads]
            for h in heads:
                w_s[c, h] = w[h].astype(BF16)
                vn_s[c, h] = vn[h].astype(BF16)
                qe_s[c, h] = (L["q"][h] * L["e_col"][h]).astype(BF16)
                kf_s[c, h] = (L["k"][h] * L["f_col"][h]).astype(BF16)
                att_s[c, h] = L["att"][h].astype(BF16)
            e_last[c] = L["e_last"]

        dst = [dstate[h] for h in heads]
        de_last = [None] * NCB
        for c in reversed(range(NCB)):
            rows = pl.ds(c * C, C)
            dob = [do_ref[rows, h * DK:(h + 1) * DK].astype(BF16) for h in heads]
            dstb = [dst[h].astype(BF16) for h in heads]
            dvn = [_dot(att_s[c, h], dob[h], "tn") + _dot(kf_s[c, h], dstb[h]) for h in heads]
            dkf = [_dot(vn_s[c, h], dstb[h], "nt") for h in heads]
            de_last[c] = [jnp.sum(jnp.sum(dst[h] * st_ref[h, c], axis=1, keepdims=True), axis=0, keepdims=True)
                          for h in heads]
            new = [dst[h] * e_last[c][h] + _dot(qe_s[c, h], dob[h], "tn")
                   - _dot(w_s[c, h], dvn[h].astype(BF16), "tn") for h in heads]
            for h in heads:
                dvn_s[c, h] = dvn[h]
                dkf_s[c, h] = dkf[h]
            dst = new
        for h in heads:
            dstate[h] = dst[h]

        for c in range(NCB):
            rows = pl.ds(c * C, C)
            L, a_pre, g_row, beta_row = local(c)
            q, k, v, kb, vb = L["q"], L["k"], L["v"], L["kb"], L["vb"]
            e_col, f_col, decay, beta_col = L["e_col"], L["f_col"], L["decay"], L["beta_col"]
            eye, strict, tril = L["eye"], L["strict"], L["tril"]
            tinv = [t_ref[h, c] for h in heads]
            stb = [st_ref[h, c].astype(BF16) for h in heads]
            dov = [do_ref[rows, h * DK:(h + 1) * DK] for h in heads]
            dvn = [dvn_s[c, h] for h in heads]
            dkf = [dkf_s[c, h] for h in heads]
            kbe = [kb[h] * e_col[h] for h in heads]
            datt = [jnp.where(tril, _dot(dov[h], vn_s[c, h], "nt"), 0.0) for h in heads]
            dqe = [_dot(dov[h], stb[h], "nt") for h in heads]
            dw = [-_dot(dvn[h], stb[h], "nt") for h in heads]
            dt = [_dot(dvn[h], vb[h], "nt") + _dot(dw[h], kbe[h], "nt") for h in heads]
            dvb = [_dot(tinv[h], dvn[h], "tn", exact=True) for h in heads]
            dkbe = [_dot(tinv[h], dw[h], "tn", exact=True) for h in heads]
            tdt = [_dot(tinv[h], dt[h], "tn", exact=True) for h in heads]
            dlow = [-jnp.where(strict, _dot(tdt[h], tinv[h], "nt", exact=True), 0.0) for h in heads]
            dkk = [dlow[h] * decay[h] for h in heads]
            dqk = [datt[h] * decay[h] for h in heads]
            dkb = [_dot(dkk[h], k[h]) + dkbe[h] * e_col[h] for h in heads]
            dk = [_dot(dkk[h], kb[h], "tn") + _dot(dqk[h], q[h], "tn") + dkf[h] * f_col[h] + dkb[h] * beta_col[h]
                  for h in heads]
            dq = [_dot(dqk[h], k[h]) + dqe[h] * e_col[h] for h in heads]
            for h in heads:
                dqkv_ref[rows, h * _HM:h * _HM + DK] = dq[h]
                dqkv_ref[rows, h * _HM + DK:h * _HM + 2 * DK] = dk[h]
                dqkv_ref[rows, h * _HM + 2 * DK:(h + 1) * _HM] = dvb[h] * beta_col[h]

            dbeta_col = [jnp.sum(k[h] * dkb[h] + v[h] * dvb[h], axis=1, keepdims=True) for h in heads]
            pmat = [dlow[h] * L["low"][h] + datt[h] * L["att"][h] for h in heads]
            df_col = [jnp.sum(k[h] * dkf[h], axis=1, keepdims=True) * f_col[h] for h in heads]
            dgc_col = [jnp.sum(pmat[h], axis=1, keepdims=True)
                       + jnp.sum(q[h] * dqe[h] + kb[h] * dkbe[h], axis=1, keepdims=True) * e_col[h] - df_col[h]
                       for h in heads]
            dgc_row = [_to_row(dgc_col[h], eye) - jnp.sum(pmat[h], axis=0, keepdims=True) for h in heads]
            dg_last = [jnp.sum(df_col[h], axis=0, keepdims=True) + de_last[c][h] * L["e_last"][h] for h in heads]
            dgc_c = [_to_col(dgc_row[h], eye) for h in heads]
            dg_row = [jnp.sum(jnp.where(ri >= ci, dgc_c[h], 0.0), axis=0, keepdims=True) + dg_last[h] for h in heads]
            dbeta_row = [_to_row(dbeta_col[h], eye) for h in heads]
            for h in heads:
                da_row = dg_row[h] * neg_a[h] * _sigmoid(a_pre[h])
                dab_ref[h, c] = da_row
                dab_ref[H + h, c] = dbeta_row[h] * beta_row[h] * (1.0 - beta_row[h])
                dalog_ref[h] += jnp.sum(dg_row[h] * g_row[h], axis=1, keepdims=True)
                ddtb_ref[h] += jnp.sum(da_row, axis=1, keepdims=True)

        if riding is not None:
            finish_ride()

    rev = lambda n: NB - 1 - n
    vec = pl.BlockSpec((H, 1, 1), lambda n: (0, 0, 0))
    gates = pl.BlockSpec((2 * H, NCB, 1, C), lambda n: (0, rev(n), 0, 0))
    wide = pl.BlockSpec((RB, H * _HM), lambda n: (rev(n), 0))
    item = lambda dt: pltpu.VMEM((NCB, H, C, DK), dt)
    ride_args, ride_specs, ride_out, ride_scratch = _riding(riding, False)
    return pl.pallas_call(
        body, name=name, grid=(NB,),
        in_specs=[wide, gates, vec, vec,
                  pl.BlockSpec((H, NCB, DK, DK), lambda n: (0, rev(n), 0, 0)),
                  pl.BlockSpec((H, NCB, C, C), lambda n: (0, rev(n), 0, 0)),
                  pl.BlockSpec((RB, H * DK), lambda n: (rev(n), 0))] + ride_specs,
        out_specs=[wide, gates, vec, vec] + ride_specs,
        out_shape=[jax.ShapeDtypeStruct((S, H * _HM), F32),
                   jax.ShapeDtypeStruct((2 * H, NC, 1, C), F32),
                   jax.ShapeDtypeStruct((H, 1, 1), F32),
                   jax.ShapeDtypeStruct((H, 1, 1), F32)] + ride_out,
        scratch_shapes=[pltpu.VMEM((H, DK, DK), F32), item(BF16), item(BF16), item(BF16), item(BF16),
                        pltpu.VMEM((NCB, H, C, C), BF16), item(F32), item(F32)] + ride_scratch,
        compiler_params=_params("arbitrary"),
    )(qkv, ab, a_log, dt_bias, states, tinvs, do, *ride_args)


def _gdn_outnorm_fwd(o, z, gain, *, name):
    S, HV = o.shape
    RB = min(256, S)

    def body(o_ref, z_ref, g_ref, y_ref):
        for h in range(HV // GDN_DK):
            cols = slice(h * GDN_DK, (h + 1) * GDN_DK)
            ov = o_ref[:, cols]
            r = lax.rsqrt(jnp.mean(ov * ov, axis=-1, keepdims=True) + RMS_EPS)
            y_ref[:, cols] = (ov * r * g_ref[...] * _silu(z_ref[:, cols].astype(F32))).astype(BF16)

    blk = pl.BlockSpec((RB, HV), lambda i: (i, 0))
    return pl.pallas_call(
        body, name=name, grid=(S // RB,),
        in_specs=[blk, blk, pl.BlockSpec((1, GDN_DK), lambda i: (0, 0))], out_specs=blk,
        out_shape=jax.ShapeDtypeStruct((S, HV), BF16), compiler_params=_params("parallel"),
    )(o, z, gain)


def _gdn_outnorm_bwd(dy, o, z, gain, *, name):
    S, HV = o.shape
    RB = min(256, S)

    def body(dy_ref, o_ref, z_ref, g_ref, do_ref, dz_ref, dg_ref):
        part = None
        for h in range(HV // GDN_DK):
            cols = slice(h * GDN_DK, (h + 1) * GDN_DK)
            ov = o_ref[:, cols]
            zv = z_ref[:, cols].astype(F32)
            dyv = dy_ref[:, cols].astype(F32)
            r = lax.rsqrt(jnp.mean(ov * ov, axis=-1, keepdims=True) + RMS_EPS)
            n = ov * r
            sg = _sigmoid(zv)
            dng = dyv * (zv * sg)
            dn = dng * g_ref[...]
            do_ref[:, cols] = r * (dn - n * jnp.mean(dn * n, axis=-1, keepdims=True))
            dz_ref[:, cols] = (dyv * (n * g_ref[...]) * (sg * (1.0 + zv * (1.0 - sg)))).astype(BF16)
            p = jnp.sum(dng * n, axis=0, keepdims=True)
            part = p if part is None else part + p

        @pl.when(pl.program_id(0) == 0)
        def _():
            dg_ref[...] = part

        @pl.when(pl.program_id(0) > 0)
        def _():
            dg_ref[...] += part

    blk = pl.BlockSpec((RB, HV), lambda i: (i, 0))
    vec = pl.BlockSpec((1, GDN_DK), lambda i: (0, 0))
    return pl.pallas_call(
        body, name=name, grid=(S // RB,),
        in_specs=[blk, blk, blk, vec], out_specs=[blk, blk, vec],
        out_shape=[jax.ShapeDtypeStruct((S, HV), F32), jax.ShapeDtypeStruct((S, HV), BF16),
                   jax.ShapeDtypeStruct((1, GDN_DK), F32)],
        compiler_params=_params("arbitrary"),
    )(dy, o, z, gain)


def _head_mask():
    return lax.broadcasted_iota(jnp.int32, (DSW_BLK, LANES), 1) < DSW_DH


def _per_head_sum(t, first):
    s0 = jnp.sum(jnp.where(first, t, 0.0), axis=-1, keepdims=True)
    s1 = jnp.sum(jnp.where(first, 0.0, t), axis=-1, keepdims=True)
    return jnp.where(first, s0, s1)


def _rms2(x, gain, first):
    r = lax.rsqrt(_per_head_sum(x * x, first) * (1.0 / DSW_DH) + RMS_EPS)
    xh = x * r
    return xh, r, xh * gain


def _rms2_bwd(dy, xh, r, gain, first):
    dxh = dy * gain
    return r * (dxh - xh * (_per_head_sum(dxh * xh, first) * (1.0 / DSW_DH)))


def _split_heads(x, first):
    return [jnp.where(first, x, 0.0).astype(BF16), jnp.where(first, 0.0, x).astype(BF16)]


_HP = LANES // DSW_DH
_DSW_W = DSW_HEADS * DSW_DH
_DSW_ROWS = 1024
_DSW_BATCH = 8


def _dsw_geometry(S, g):
    d = DSW_GROUPS[g][1]
    slab = DSW_BLK * d
    tb = max(1, min(_DSW_ROWS, S) // slab)
    return d, slab, tb, S // (tb * slab)


def _block_rows(t, r, slab, d):
    return pl.ds(t * slab + r, DSW_BLK) if d == 1 else pl.ds(t * slab + r, DSW_BLK, stride=d)


def _dsw_attn_fwd(q, k, v, bias, q_gain, k_gain, prev_out, *, g, name):
    S, WT = q.shape
    B = DSW_BLK
    d, slab, tb, n_tiles = _dsw_geometry(S, g)
    rt = tb * slab
    cb = g * (_DSW_W // LANES)
    batch_res = max(1, _DSW_BATCH // tb)

    def body(q_ref, kp_ref, kc_ref, vp_ref, vc_ref, bias_ref, qg_ref, kg_ref, *rest):
        o_ref, lse_ref = rest[-2:]
        i = pl.program_id(1)
        qg, kg = qg_ref[...] * DSW_DH ** -0.5, kg_ref[...]
        col = lax.broadcasted_iota(jnp.int32, (B, 2 * B), 1)
        first = _head_mask()
        heads = range(_HP)
        for r0 in range(0, d, batch_res):
            res = range(r0, min(d, r0 + batch_res))
            k_raw = {(r, -1): kp_ref[_block_rows(0, r, slab, d), :] for r in res}
            v_raw = {(r, -1): vp_ref[_block_rows(0, r, slab, d), :] for r in res}
            q_raw = {}
            for r in res:
                for t in range(tb):
                    rows = _block_rows(t, r, slab, d)
                    q_raw[r, t], k_raw[r, t], v_raw[r, t] = q_ref[rows, :], kc_ref[rows, :], vc_ref[rows, :]
            kn = {key: _rms2(x, kg, first)[2].astype(BF16) for key, x in k_raw.items()}
            vb = {key: x.astype(BF16) for key, x in v_raw.items()}
            qn = {key: _split_heads(_rms2(x, qg, first)[2], first) for key, x in q_raw.items()}
            items = [(r, t, h) for r in res for t in range(tb) for h in heads]
            s = {}
            for r, t, h in items:
                sv = _dot(qn[r, t][h], jnp.concatenate([kn[r, t - 1], kn[r, t]], axis=0), "nt") + bias_ref[h]
                s[r, t, h] = jnp.where((i == 0) & (col < B), NEG_BIG, sv) if t == 0 else sv
            m = {it: jnp.max(s[it], axis=-1, keepdims=True) for it in items}
            p = {it: jnp.exp(s[it] - m[it]) for it in items}
            l = {it: jnp.sum(p[it], axis=-1, keepdims=True) for it in items}
            o = {(r, t, h): _dot(p[r, t, h], jnp.concatenate([vb[r, t - 1], vb[r, t]], axis=0)) for r, t, h in items}
            for r in res:
                for t in range(tb):
                    rows = _block_rows(t, r, slab, d)
                    o_ref[rows, :] = jnp.where(first, o[r, t, 0] / l[r, t, 0], o[r, t, 1] / l[r, t, 1])
                    lse_ref[rows, :] = jnp.where(first, m[r, t, 0] + jnp.log(l[r, t, 0]),
                                                 m[r, t, 1] + jnp.log(l[r, t, 1]))

    cur = pl.BlockSpec((rt, LANES), lambda hp, i: (i, cb + hp))
    prev = pl.BlockSpec((slab, LANES), lambda hp, i: (jnp.maximum(i * tb - 1, 0), cb + hp))
    vec = pl.BlockSpec((1, LANES), lambda hp, i: (0, 0))
    shp = jax.ShapeDtypeStruct((S, WT), F32)
    carried = [] if prev_out is None else list(prev_out)
    n_in = 8
    return pl.pallas_call(
        body, name=name, grid=(_DSW_W // LANES, n_tiles),
        in_specs=[cur, prev, cur, prev, cur, pl.BlockSpec((_HP, B, 2 * B), lambda hp, i: (hp, 0, 0)), vec, vec]
                 + [pl.BlockSpec(memory_space=pl.ANY)] * len(carried),
        out_specs=[cur, cur], out_shape=[shp, shp],
        input_output_aliases={n_in + j: j for j in range(len(carried))},
        compiler_params=_params("parallel", "parallel"),
    )(q, k, k, v, v, bias, jnp.tile(q_gain, (1, _HP)), jnp.tile(k_gain, (1, _HP)), *carried)


def _dsw_merge(o_g, lse_g, *, name):
    S = o_g.shape[0]
    W, G = _DSW_W, len(DSW_GROUPS)
    tr = min(512, S)

    def body(o_ref, l_ref, out_ref, lse_ref):
        ls = [l_ref[:, g * W:(g + 1) * W] for g in range(G)]
        m = ls[0]
        for g in range(1, G):
            m = jnp.maximum(m, ls[g])
        den = jnp.zeros_like(m)
        acc = jnp.zeros_like(m)
        for g in range(G):
            wg = jnp.exp(ls[g] - m)
            den = den + wg
            acc = acc + wg * o_ref[:, g * W:(g + 1) * W]
        out_ref[...] = acc / den
        lse_ref[...] = m + jnp.log(den)

    wide = pl.BlockSpec((tr, G * W), lambda i: (i, 0))
    blk = pl.BlockSpec((tr, W), lambda i: (i, 0))
    shp = jax.ShapeDtypeStruct((S, W), F32)
    return pl.pallas_call(
        body, name=name, grid=(S // tr,), in_specs=[wide, wide], out_specs=[blk, blk],
        out_shape=[shp, shp], compiler_params=_params("parallel"),
    )(o_g, lse_g)


def _dsw_attn_bwd(q, k, v, o, lse, do, bias, q_gain, k_gain, prev_out, *, g, name):
    S, WT = q.shape
    B = DSW_BLK
    d, slab, tb, n_tiles = _dsw_geometry(S, g)
    rt = tb * slab
    cb = g * (_DSW_W // LANES)
    n_slabs = S // slab
    scale = DSW_DH ** -0.5
    batch_res = max(1, _DSW_BATCH // tb)

    def body(q_ref, qx_ref, kp_ref, kc_ref, vp_ref, vc_ref, o_ref, ox_ref, l_ref, lx_ref, do_ref, dox_ref,
             bias_ref, qg_ref, kg_ref, *rest):
        dq_ref, dk_ref, dv_ref, db_ref, dqg_ref, dkg_ref = rest[-6:]
        hp, i = pl.program_id(0), pl.program_id(1)
        qg, kg = qg_ref[...] * scale, kg_ref[...]
        col = lax.broadcasted_iota(jnp.int32, (B, 2 * B), 1)
        has_next = i < n_tiles - 1

        @pl.when(i == 0)
        def _():
            db_ref[...] = jnp.zeros_like(db_ref)

        dqg_acc = jnp.zeros((1, LANES), F32)
        dkg_acc = jnp.zeros((1, LANES), F32)
        first = _head_mask()
        heads = range(_HP)
        for r0 in range(0, d, batch_res):
            res = range(r0, min(d, r0 + batch_res))
            q_raw, k_raw, v_raw, o_raw, l_raw, do_raw = {}, {}, {}, {}, {}, {}
            for r in res:
                first_rows = _block_rows(0, r, slab, d)
                k_raw[r, -1], v_raw[r, -1] = kp_ref[first_rows, :], vp_ref[first_rows, :]
                for t in range(tb):
                    rows = _block_rows(t, r, slab, d)
                    q_raw[r, t], o_raw[r, t], l_raw[r, t], do_raw[r, t] = (
                        q_ref[rows, :], o_ref[rows, :], l_ref[rows, :], do_ref[rows, :])
                    k_raw[r, t], v_raw[r, t] = kc_ref[rows, :], vc_ref[rows, :]
                q_raw[r, tb], o_raw[r, tb], l_raw[r, tb], do_raw[r, tb] = (
                    qx_ref[first_rows, :], ox_ref[first_rows, :], lx_ref[first_rows, :], dox_ref[first_rows, :])
            kk = {key: _rms2(x, kg, first) for key, x in k_raw.items()}
            qq = {key: _rms2(x, qg, first) for key, x in q_raw.items()}
            knb = {key: kk[key][2].astype(BF16) for key in kk}
            qnb = {key: _split_heads(qq[key][2], first) for key in qq}
            vb = {key: x.astype(BF16) for key, x in v_raw.items()}
            dob = {key: _split_heads(x, first) for key, x in do_raw.items()}
            delta = {key: _per_head_sum(do_raw[key] * o_raw[key], first) for key in q_raw}
            pick = lambda x, h: x[:, h * DSW_DH:h * DSW_DH + 1]
            full = [(r, t, h) for r in res for t in range(tb) for h in heads]
            half = [(r, tb, h) for r in res for h in heads]
            s = {}
            for r, t, h in full:
                sv = _dot(qnb[r, t][h], jnp.concatenate([knb[r, t - 1], knb[r, t]], axis=0), "nt") + bias_ref[h]
                s[r, t, h] = jnp.where((i == 0) & (col < B), NEG_BIG, sv) if t == 0 else sv
            for r, t, h in half:
                s[r, t, h] = _dot(qnb[r, t][h], knb[r, t - 1], "nt") + bias_ref[h, :, 0:B]
            p = {(r, t, h): jnp.exp(s[r, t, h] - pick(l_raw[r, t], h)) for r, t, h in full}
            for r, t, h in half:
                p[r, t, h] = jnp.where(has_next, jnp.exp(s[r, t, h] - pick(l_raw[r, t], h)), 0.0)
            dp = {(r, t, h): _dot(dob[r, t][h], jnp.concatenate([vb[r, t - 1], vb[r, t]], axis=0), "nt")
                  for r, t, h in full}
            for r, t, h in half:
                dp[r, t, h] = _dot(dob[r, t][h], vb[r, t - 1], "nt")
            ds = {(r, t, h): p[r, t, h] * (dp[r, t, h] - pick(delta[r, t], h)) for r, t, h in full + half}
            pb = {it: p[it].astype(BF16) for it in ds}
            dsb = {it: ds[it].astype(BF16) for it in ds}
            for h in heads:
                tot = None
                for r in res:
                    for t in range(tb):
                        tot = ds[r, t, h] if tot is None else tot + ds[r, t, h]
                db_ref[h] += tot
            blocks = [(r, t) for r in res for t in range(tb)]
            keys2 = {(r, t): jnp.concatenate([knb[r, t - 1], knb[r, t]], axis=0) for r, t in blocks}
            dqn = {(r, t): jnp.where(first, _dot(dsb[r, t, 0], keys2[r, t]), _dot(dsb[r, t, 1], keys2[r, t]))
                   for r, t in blocks}
            prev_half = lambda x, r, t, h: x[r, t, h][:, :B] if t < tb else x[r, t, h]
            dkn = {(r, t): sum(_dot(dsb[r, t, h][:, B:], qnb[r, t][h], "tn")
                               + _dot(prev_half(dsb, r, t + 1, h), qnb[r, t + 1][h], "tn") for h in heads)
                   for r, t in blocks}
            dvv = {(r, t): sum(_dot(pb[r, t, h][:, B:], dob[r, t][h], "tn")
                               + _dot(prev_half(pb, r, t + 1, h), dob[r, t + 1][h], "tn") for h in heads)
                   for r, t in blocks}
            for r, t in blocks:
                dqg_acc = dqg_acc + jnp.sum(dqn[r, t] * qq[r, t][0], axis=0, keepdims=True)
                dkg_acc = dkg_acc + jnp.sum(dkn[r, t] * kk[r, t][0], axis=0, keepdims=True)
            for r, t in blocks:
                rows = _block_rows(t, r, slab, d)
                dq_ref[rows, :] = _rms2_bwd(dqn[r, t], qq[r, t][0], qq[r, t][1], qg, first)
                dk_ref[rows, :] = _rms2_bwd(dkn[r, t], kk[r, t][0], kk[r, t][1], kg, first)
                dv_ref[rows, :] = dvv[r, t]

        start = (hp == 0) & (i == 0)
        fold = lambda a: a[:, :DSW_DH] + a[:, DSW_DH:]

        @pl.when(start)
        def _():
            dqg_ref[...] = fold(dqg_acc) * scale
            dkg_ref[...] = fold(dkg_acc)

        @pl.when(jnp.logical_not(start))
        def _():
            dqg_ref[...] += fold(dqg_acc) * scale
            dkg_ref[...] += fold(dkg_acc)

    def spec(rows, pick, base):
        return pl.BlockSpec((rows, LANES), lambda hp, i: (pick(i), base + hp))

    same = lambda i: i
    before = lambda i: jnp.maximum(i * tb - 1, 0)
    after = lambda i: jnp.minimum((i + 1) * tb, n_slabs - 1)
    cur, cur1 = spec(rt, same, cb), spec(rt, same, 0)
    vec = pl.BlockSpec((1, DSW_DH), lambda hp, i: (0, 0))
    vec2 = pl.BlockSpec((1, LANES), lambda hp, i: (0, 0))
    bspec = pl.BlockSpec((_HP, B, 2 * B), lambda hp, i: (hp, 0, 0))
    shp = jax.ShapeDtypeStruct((S, WT), F32)
    vshp = jax.ShapeDtypeStruct((1, DSW_DH), F32)
    carried = [] if prev_out is None else list(prev_out)
    n_in = 15
    return pl.pallas_call(
        body, name=name, grid=(_DSW_W // LANES, n_tiles),
        in_specs=[cur, spec(slab, after, cb), spec(slab, before, cb), cur, spec(slab, before, cb), cur,
                  cur1, spec(slab, after, 0), cur1, spec(slab, after, 0), cur1, spec(slab, after, 0),
                  bspec, vec2, vec2] + [pl.BlockSpec(memory_space=pl.ANY)] * len(carried),
        out_specs=[cur, cur, cur, bspec, vec, vec],
        out_shape=[shp, shp, shp, jax.ShapeDtypeStruct(bias.shape, F32), vshp, vshp],
        input_output_aliases={n_in + j: j for j in range(len(carried))},
        compiler_params=_params("arbitrary", "arbitrary"),
    )(q, q, k, k, v, v, o, o, lse, lse, do, do, bias, jnp.tile(q_gain, (1, _HP)), jnp.tile(k_gain, (1, _HP)),
      *carried)


def _t5_bucket(dist):
    max_exact = REL_BUCKETS // 2
    scaled = jnp.log(jnp.maximum(dist, 1).astype(F32) / max_exact) / math.log(REL_MAX_DIST / max_exact)
    large = jnp.minimum(max_exact + (scaled * (REL_BUCKETS - max_exact)).astype(jnp.int32), REL_BUCKETS - 1)
    return jnp.where(dist < max_exact, dist, large)


def _dsw_band():
    dist = (jnp.arange(DSW_BLK)[:, None] + DSW_BLK) - jnp.arange(2 * DSW_BLK)[None, :]
    return dist, (dist >= 0) & (dist <= DSW_BLK)


def _dsw_bias(rel_bias):
    dist, band = _dsw_band()
    out = []
    for g, (_, d) in enumerate(DSW_GROUPS):
        hot = jax.nn.one_hot(_t5_bucket(jnp.maximum(dist, 0) * d), REL_BUCKETS, dtype=F32)
        tab = jnp.einsum("qkb,bh->hqk", hot, rel_bias[:, g * DSW_HEADS:(g + 1) * DSW_HEADS],
                         precision=lax.Precision.HIGHEST)
        out.append(jnp.where(band[None], tab, NEG_BIG))
    return jnp.stack(out)


def _dsw_bucket_onehot():
    dist, band = _dsw_band()
    out = []
    for _, d in DSW_GROUPS:
        hot = jax.nn.one_hot(_t5_bucket(jnp.maximum(dist, 0) * d), LANES, dtype=BF16)
        out.append(jnp.where(band[..., None], hot, 0).reshape(-1, LANES))
    return jnp.stack(out)


def _exchange(send, *, gather, name):
    R, C = send.shape[-2:]

    def body(src_ref, dst_ref, send_sems, recv_sems, local_sem):
        x, y, c = lax.axis_index("x"), lax.axis_index("y"), lax.axis_index("c")
        me = 4 * x + 2 * y + c
        mine = pltpu.make_async_copy(src_ref if gather else src_ref.at[me], dst_ref.at[me], local_sem)
        mine.start()
        copies = []
        for rel in range(1, N_DEV):
            px = 1 - x if rel & 4 else x
            py = 1 - y if rel & 2 else y
            pc = 1 - c if rel & 1 else c
            peer = 4 * px + 2 * py + pc
            cp = pltpu.make_async_remote_copy(
                src_ref=src_ref if gather else src_ref.at[peer], dst_ref=dst_ref.at[me],
                send_sem=send_sems.at[rel - 1], recv_sem=recv_sems.at[rel - 1],
                device_id=(px, py, pc), device_id_type=pl.DeviceIdType.MESH)
            cp.start()
            copies.append(cp)
        for cp in copies:
            cp.wait()
        mine.wait()

    return pl.pallas_call(
        body, name=name,
        in_specs=[pl.BlockSpec(memory_space=pl.ANY)], out_specs=pl.BlockSpec(memory_space=pl.ANY),
        out_shape=jax.ShapeDtypeStruct((N_DEV, R, C), send.dtype),
        scratch_shapes=[pltpu.SemaphoreType.DMA((N_DEV - 1,)), pltpu.SemaphoreType.DMA((N_DEV - 1,)),
                        pltpu.SemaphoreType.DMA(())],
    )(send)


def _gather_two_level(send, *, name):
    R, C = send.shape

    def body(src_ref, dst_ref, send_sems, recv_sems, local_sem):
        x, y, c = lax.axis_index("x"), lax.axis_index("y"), lax.axis_index("c")
        me, sibling = (x, y, c), (x, y, 1 - c)
        chips = [(1 - x, y), (x, 1 - y), (1 - x, 1 - y)]

        def slot(px, py, pc):
            return dst_ref.at[4 * px + 2 * py + pc]

        def copy(k, block, to, src=None):
            return pltpu.make_async_remote_copy(
                src_ref=slot(*block) if src is None else src, dst_ref=slot(*block),
                send_sem=send_sems.at[k], recv_sem=recv_sems.at[k],
                device_id=to, device_id_type=pl.DeviceIdType.MESH)

        mine = pltpu.make_async_copy(src_ref, slot(*me), local_sem)
        mine.start()
        first = [copy(0, me, sibling, src=src_ref)]
        first += [copy(1 + j, me, (*chip, c), src=src_ref) for j, chip in enumerate(chips)]
        for cp in first:
            cp.start()
        passed = [copy(4 + j, (*chip, c), sibling) for j, chip in enumerate(chips)]
        for j, chip in enumerate(chips):
            copy(1 + j, (*chip, c), me).wait_recv()
            passed[j].start()
        copy(0, sibling, me).wait_recv()
        for j, chip in enumerate(chips):
            copy(4 + j, (*chip, 1 - c), me).wait_recv()
        for cp in first + passed:
            cp.wait_send()
        mine.wait()

    return pl.pallas_call(
        body, name=name,
        in_specs=[pl.BlockSpec(memory_space=pl.ANY)], out_specs=pl.BlockSpec(memory_space=pl.ANY),
        out_shape=jax.ShapeDtypeStruct((N_DEV, R, C), send.dtype),
        scratch_shapes=[pltpu.SemaphoreType.DMA((N_DEV - 1,)), pltpu.SemaphoreType.DMA((N_DEV - 1,)),
                        pltpu.SemaphoreType.DMA(())],
    )(send)


_ANY = pl.BlockSpec(memory_space=pl.ANY)


def _swap_with_sibling(sends, *, name):
    n = len(sends)

    def body(*refs):
        x, y, c = lax.axis_index("x"), lax.axis_index("y"), lax.axis_index("c")
        send_sems, recv_sems = refs[2 * n:]
        copies = [pltpu.make_async_remote_copy(
            src_ref=refs[a], dst_ref=refs[n + a], send_sem=send_sems.at[a], recv_sem=recv_sems.at[a],
            device_id=(x, y, 1 - c), device_id_type=pl.DeviceIdType.MESH) for a in range(n)]
        for cp in copies:
            cp.start()
        for cp in copies:
            cp.wait()

    return pl.pallas_call(
        body, name=name, in_specs=[_ANY] * n, out_specs=[_ANY] * n,
        out_shape=[jax.ShapeDtypeStruct(s.shape, s.dtype) for s in sends],
        scratch_shapes=[pltpu.SemaphoreType.DMA((n,)), pltpu.SemaphoreType.DMA((n,))],
    )(*sends)


def _fill_from_sibling(bufs, *, name):
    n, n_chips = len(bufs), bufs[0].shape[0]

    def body(*refs):
        x, y, c = lax.axis_index("x"), lax.axis_index("y"), lax.axis_index("c")
        send_sems, recv_sems = refs[2 * n:]
        copies = [pltpu.make_async_remote_copy(
            src_ref=refs[a].at[q, c], dst_ref=refs[n + a].at[q, c],
            send_sem=send_sems.at[a * n_chips + q], recv_sem=recv_sems.at[a * n_chips + q],
            device_id=(x, y, 1 - c), device_id_type=pl.DeviceIdType.MESH) for a in range(n) for q in range(n_chips)]
        for cp in copies:
            cp.start()
        for cp in copies:
            cp.wait()

    return pl.pallas_call(
        body, name=name, in_specs=[_ANY] * n, out_specs=[_ANY] * n,
        out_shape=[jax.ShapeDtypeStruct(b.shape, b.dtype) for b in bufs],
        input_output_aliases={a: a for a in range(n)},
        scratch_shapes=[pltpu.SemaphoreType.DMA((n * n_chips,)), pltpu.SemaphoreType.DMA((n * n_chips,))],
    )(*bufs)


def _exchange_chips(send, *, name):
    def body(src_ref, dst_ref, *sems):
        copies = _chip_copies([src_ref], [dst_ref], *sems)
        for cp in copies:
            cp.start()
        for cp in copies:
            cp.wait()

    return pl.pallas_call(
        body, name=name, in_specs=[_ANY], out_specs=_ANY,
        out_shape=jax.ShapeDtypeStruct(send.shape, send.dtype), scratch_shapes=_chip_sems(1),
    )(send)


def _add_pair(a, b, *, name):
    lead, (R, C) = a.shape[:-2], a.shape[-2:]
    tr = _tile(R, max(8, 1024 * LANES // C))

    def body(a_ref, b_ref, o_ref):
        o_ref[...] = (a_ref[...].astype(F32) + b_ref[...].astype(F32)).astype(o_ref.dtype)

    blk = pl.BlockSpec((None,) * len(lead) + (tr, C), lambda *idx: idx + (0,))
    return pl.pallas_call(
        body, name=name, grid=lead + (R // tr,), in_specs=[blk, blk], out_specs=blk,
        out_shape=jax.ShapeDtypeStruct(a.shape, a.dtype),
        compiler_params=_params(*(("parallel",) * (len(lead) + 1))),
    )(a, b)


_BIG = ("w_ffn_in", "w_ffn_out", "gdn_w_in", "gdn_conv", "gdn_w_out", "dsw_w_in", "dsw_w_out")
_LATE = ("gdn_w_in", "gdn_conv", "gdn_w_out")
_EARLY = tuple(n for n in _BIG if n not in _LATE)
_NATIVE = ("w_ffn_in", "w_ffn_out")
_SHARD_AXIS = {"w_ffn_in": 2, "w_ffn_out": 1, "gdn_w_in": 2, "gdn_conv": 2, "gdn_w_out": 1, "dsw_w_in": 2,
               "dsw_w_out": 2}
_SMALL = ("b_ada", "norm_mix", "norm_ffn", "gdn_a_log", "gdn_dt_bias", "gdn_out_norm", "dsw_q_norm",
          "dsw_k_norm", "rel_bias")
_ROW_ALIGN = 16
_BIG_ALIGN = 1024


def _ceil_to(n, m):
    return -(-n // m) * m


def _seg_rows(shape):
    return _ceil_to(_ceil_to(int(np.prod(shape)), LANES) // LANES, _ROW_ALIGN)


def _pack(arrs, total_align):
    lead = arrs[0][1]
    segs = []
    for a, nlead in arrs:
        assert nlead == lead
        bshape = a.shape[:nlead]
        n = int(np.prod(a.shape[nlead:]))
        rows = _seg_rows(a.shape[nlead:])
        flat = a.reshape(bshape + (n,))
        flat = jnp.pad(flat, [(0, 0)] * nlead + [(0, rows * LANES - n)])
        segs.append(flat.reshape(bshape + (rows, LANES)))
    buf = jnp.concatenate(segs, axis=lead)
    total = _ceil_to(buf.shape[lead], total_align)
    return jnp.pad(buf, [(0, 0)] * lead + [(0, total - buf.shape[lead]), (0, 0)])


def _unpack(buf, shapes, nlead):
    out, off = [], 0
    for shp in shapes:
        n, rows = int(np.prod(shp)), _seg_rows(shp)
        seg = lax.slice_in_dim(buf, off, off + rows, axis=nlead)
        seg = seg.reshape(buf.shape[:nlead] + (rows * LANES,))[..., :n]
        out.append(seg.reshape(buf.shape[:nlead] + tuple(shp)))
        off += rows
    return out


def _to_natural(g, axis):
    n, L, r, c = g.shape
    if axis == 2:
        return jnp.transpose(g, (1, 2, 0, 3)).reshape(L, r, n * c)
    return jnp.transpose(g, (1, 0, 2, 3)).reshape(L, n * r, c)


def _to_blocked(w, axis):
    L, R, C = w.shape
    if axis == 2:
        return jnp.transpose(w.reshape(L, R, N_DEV, C // N_DEV), (2, 0, 1, 3))
    return jnp.transpose(w.reshape(L, N_DEV, R // N_DEV, C), (1, 0, 2, 3))


def _hm(a):
    lead = a.shape[:-1]
    return jnp.swapaxes(a.reshape(lead + (3, GDN_HEADS, GDN_DK)), -3, -2).reshape(lead + (3 * GDN_HEADS * GDN_DK,))


def _un_hm(a):
    lead = a.shape[:-1]
    return jnp.swapaxes(a.reshape(lead + (GDN_HEADS, 3, GDN_DK)), -3, -2).reshape(lead + (3 * GDN_HEADS * GDN_DK,))


_TILES = (1536, 1408, 1024, 768, 512, 384, 256, 128, 64, 32, 16, 8)


def _tile(n, cap):
    for t in _TILES:
        if t <= cap and n % t == 0:
            return t
    return n


def _mm_auto(a, b, mode, name, **kw):
    if mode == "tn":
        (K, M), N = a.shape, b.shape[1]
        tm, tn, tk = _tile(M, 1408), _tile(N, 1408), _tile(K, 1024)
    else:
        M, K = a.shape
        N = b.shape[1] if mode == "nn" else b.shape[0]
        tm, tn, tk = _tile(M, 512), _tile(N, 1536), _tile(K, 1408)
    return _mm(a, b, mode=mode, name=name, tm=tm, tn=tn, tk=tk, **kw)


def _row(v):
    return v.reshape(1, -1)


def _ffn_in_act(h, w_in, *, name):
    S, D = h.shape
    F = w_in.shape[1] // 2
    tm, tn = _tile(S, 512), _tile(F, 1408)
    nj = F // tn

    def body(h_ref, wg_ref, wu_ref, g_ref, u_ref, a_ref):
        hv = h_ref[...]
        gate = jnp.dot(hv, wg_ref[...], preferred_element_type=F32)
        up = jnp.dot(hv, wu_ref[...], preferred_element_type=F32)
        g_ref[...] = gate.astype(BF16)
        u_ref[...] = up.astype(BF16)
        a_ref[...] = (_silu(gate) * up).astype(BF16)

    out = pl.BlockSpec((tm, tn), lambda i, j: (i, j))
    shp = jax.ShapeDtypeStruct((S, F), BF16)
    return pl.pallas_call(
        body, name=name, grid=(S // tm, nj),
        in_specs=[pl.BlockSpec((tm, D), lambda i, j: (i, 0)), pl.BlockSpec((D, tn), lambda i, j: (0, j)),
                  pl.BlockSpec((D, tn), lambda i, j: (0, j + nj))],
        out_specs=[out, out, out], out_shape=[shp, shp, shp],
        compiler_params=_params("parallel", "parallel"),
    )(h, w_in, w_in)


def _ffn_out_dx_act(dy, w_out, gate_vec, pg, pu, *, name):
    S, D = dy.shape
    F = w_out.shape[0]
    tm, tn = _tile(S, 512), _tile(F, 1408)

    def body(dy_ref, w_ref, gv_ref, pg_ref, pu_ref, dg_ref, du_ref):
        dyg = (dy_ref[...] * gv_ref[...]).astype(BF16)
        da = lax.dot_general(dyg, w_ref[...], _DOT_DIMS["nt"], preferred_element_type=F32)
        gate = pg_ref[...].astype(F32)
        up = pu_ref[...].astype(F32)
        sg = _sigmoid(gate)
        dg_ref[...] = (da * up * (sg * (1.0 + gate * (1.0 - sg)))).astype(BF16)
        du_ref[...] = (da * (gate * sg)).astype(BF16)

    blk = pl.BlockSpec((tm, tn), lambda i, j: (i, j))
    shp = jax.ShapeDtypeStruct((S, F), BF16)
    return pl.pallas_call(
        body, name=name, grid=(S // tm, F // tn),
        in_specs=[pl.BlockSpec((tm, D), lambda i, j: (i, 0)), pl.BlockSpec((tn, D), lambda i, j: (j, 0)),
                  pl.BlockSpec((1, D), lambda i, j: (0, 0)), blk, blk],
        out_specs=[blk, blk], out_shape=[shp, shp],
        compiler_params=_params("parallel", "parallel"),
    )(dy, w_out, gate_vec, pg, pu)


def _ffn_fwd(x, mod, gain, w_in, w_out, tag):
    sh, sc, gate = mod
    h = _norm_mod_fwd(x, gain, sc, sh, name=f"ffn_norm_{tag}")
    pg, pu, a = _ffn_in_act(h, w_in, name=f"ffn_in_{tag}")
    y = _mm_auto(a, w_out, "nn", f"ffn_out_{tag}", out_scale=gate, resid=x)
    return y, (x, h, pg, pu, a)


def _ffn_bwd(dy, saved, mod, gain, w_in, w_out, tag):
    sh, sc, gate = mod
    x, h, pg, pu, a = saved
    F = pg.shape[1]
    gmat = _mm_auto(a, dy, "tn", f"ffn_out_g_{tag}")
    dw_out, dgate = _wout_grad(gmat, w_out, gate, name=f"ffn_out_dw_{tag}")
    dpg, dpu = _ffn_out_dx_act(dy, w_out, gate, pg, pu, name=f"ffn_out_dx_{tag}")
    dw_in = jnp.concatenate([_mm_auto(h, dpg, "tn", f"ffn_in_dw_gate_{tag}", out_dtype=BF16),
                             _mm_auto(h, dpu, "tn", f"ffn_in_dw_up_{tag}", out_dtype=BF16)], axis=1)
    tk = _tile(F, 1408)
    dh = _mm_sum_nt([(dpg, w_in, tk, 0), (dpu, w_in, tk, F)], name=f"ffn_in_dx_{tag}")
    dx, dsh, dsc, dgain = _norm_mod_bwd(dh, x, dy, gain, sc, name=f"ffn_norm_bwd_{tag}")
    return dx, dict(w_in=dw_in, w_out=dw_out, gain=dgain, mod=(dsh, dsc, dgate))


def _gdn_fwd(x, mod, gain, W, riding=None):
    sh, sc, gate = mod
    S = x.shape[0]
    h = _norm_mod_fwd(x, gain, sc, sh, name="gdn_norm")
    pq = _mm_auto(h, W["gdn_qkv"], "nn", "gdn_in_qkv", out_dtype=BF16)
    z = _mm_auto(h, W["gdn_z"], "nn", "gdn_in_z", out_dtype=BF16)
    ab = _mm_auto(h, W["gdn_ab"], "nn", "gdn_in_ab")
    qkvn = _gdn_prep_fwd(pq, W["gdn_conv"], name="gdn_prep")
    ab4 = jnp.transpose(ab[:, :2 * GDN_HEADS]).reshape(2 * GDN_HEADS, S // GDN_CHUNK, 1, GDN_CHUNK)
    o, states, tinvs, *rode = _gdn_chunk_fwd(qkvn, ab4, W["gdn_a_log"], W["gdn_dt_bias"], name="gdn_chunk",
                                             riding=riding)
    o2 = _gdn_outnorm_fwd(o, z, W["gdn_out_norm"], name="gdn_outnorm")
    y = _mm_auto(o2, W["gdn_out"], "nn", "gdn_out", out_scale=gate, resid=x)
    return y, (x, h, pq, z, qkvn, ab4, o, states, tinvs, o2), (tuple(rode) if rode else None)


def _gdn_bwd(dy, saved, mod, gain, W, riding=None):
    sh, sc, gate = mod
    x, h, pq, z, qkvn, ab4, o, states, tinvs, o2 = saved
    S = x.shape[0]
    gmat = _mm_auto(o2, dy, "tn", "gdn_out_g")
    dw_out, dgate = _wout_grad(gmat, W["gdn_out"], gate, name="gdn_out_dw")
    do2 = _mm_auto(dy, W["gdn_out"], "nt", "gdn_out_dx", a_scale=gate)
    do, dz, dout_norm = _gdn_outnorm_bwd(do2, o, z, W["gdn_out_norm"], name="gdn_outnorm_bwd")
    dqkvn, dab4, da_log, ddt_bias, *rode = _gdn_chunk_bwd(
        qkvn, ab4, W["gdn_a_log"], W["gdn_dt_bias"], states, tinvs, do, name="gdn_chunk_bwd", riding=riding)
    dc, dconv8 = _gdn_prep_bwd_pre(dqkvn, pq, W["gdn_conv"], name="gdn_prep_bwd")
    dpq = _gdn_conv_bwd_x(dc, W["gdn_conv"], name="gdn_conv_bwd")
    dab = jnp.transpose(dab4.reshape(2 * GDN_HEADS, S))
    dab = jnp.pad(dab, ((0, 0), (0, LANES - 2 * GDN_HEADS))).astype(BF16)
    dw_qkv = _mm_auto(h, dpq, "tn", "gdn_in_qkv_dw", out_dtype=BF16)
    dw_z = _mm_auto(h, dz, "tn", "gdn_in_z_dw", out_dtype=BF16)
    dw_ab = _mm_auto(h, dab, "tn", "gdn_in_ab_dw", out_dtype=BF16)
    dh = _mm_sum_nt([(dpq, W["gdn_qkv"], 1024, 0), (dz, W["gdn_z"], 1024, 0), (dab, W["gdn_ab"], LANES, 0)],
                    name="gdn_in_dx")
    dx, dsh, dsc, dgain = _norm_mod_bwd(dh, x, dy, gain, sc, name="gdn_norm_bwd")
    dw_in = jnp.concatenate([_un_hm(dw_qkv), dw_z, dw_ab[:, :2 * GDN_HEADS]], axis=1)
    return dx, dict(gdn_w_in=dw_in, gdn_conv=_un_hm(dconv8[:GDN_CONV]), gdn_w_out=dw_out, gdn_out_norm=dout_norm,
                    gdn_a_log=da_log.reshape(1, GDN_HEADS), gdn_dt_bias=ddt_bias.reshape(1, GDN_HEADS),
                    gain=dgain, mod=(dsh, dsc, dgate)), (tuple(rode) if rode else None)


def _dsw_fwd(x, mod, gain, W):
    sh, sc, gate = mod
    h = _norm_mod_fwd(x, gain, sc, sh, name="dsw_norm")
    q, k, v = (_mm_auto(h, W[n], "nn", f"dsw_in_{n[-1]}") for n in ("dsw_q", "dsw_k", "dsw_v"))
    outs = None
    for g in range(len(DSW_GROUPS)):
        outs = _dsw_attn_fwd(q, k, v, W["dsw_bias"][g], W["dsw_q_norm"], W["dsw_k_norm"], outs, g=g,
                             name=f"dsw_attn_{g}")
    o, lse = _dsw_merge(*outs, name="dsw_merge")
    y = _mm_auto(o, W["dsw_out"], "nn", "dsw_out", out_scale=gate, resid=x)
    return y, (x, h, q, k, v, o, lse)


def _dsw_bwd(dy, saved, mod, gain, W):
    sh, sc, gate = mod
    x, h, q, k, v, o, lse = saved
    gmat = _mm_auto(o, dy, "tn", "dsw_out_g")
    dw_out, dgate = _wout_grad(gmat, W["dsw_out"], gate, name="dsw_out_dw")
    do = _mm_auto(dy, W["dsw_out"], "nt", "dsw_out_dx", a_scale=gate)
    G = len(DSW_GROUPS)
    dqkv, dbias, dq_norm, dk_norm = None, [], 0.0, 0.0
    for g in range(G):
        *dqkv, db, dqg, dkg = _dsw_attn_bwd(q, k, v, o, lse, do, W["dsw_bias"][g], W["dsw_q_norm"],
                                            W["dsw_k_norm"], dqkv, g=g, name=f"dsw_attn_bwd_{g}")
        dbias.append(db)
        dq_norm, dk_norm = dq_norm + dqg, dk_norm + dkg
    names = ("dsw_q", "dsw_k", "dsw_v")
    dws = [_mm_auto(h, d, "tn", f"dsw_in_{n[-1]}_dw", out_dtype=BF16) for n, d in zip(names, dqkv)]
    dh = _mm_sum_nt([(d, W[n], _tile(d.shape[1], 1024), 0) for n, d in zip(names, dqkv)], name="dsw_in_dx")
    dx, dsh, dsc, dgain = _norm_mod_bwd(dh, x, dy, gain, sc, name="dsw_norm_bwd")
    hot = _dsw_bucket_onehot()
    drel = [_mm(dbias[g].reshape(DSW_HEADS, -1), hot[g], mode="nn", name=f"dsw_rel_bias_{g}", tm=DSW_HEADS,
                tn=LANES, tk=8192)[:, :REL_BUCKETS] for g in range(G)]
    return dx, dict(dsw_w_in=jnp.concatenate(dws, axis=1), dsw_w_out=dw_out, dsw_q_norm=dq_norm,
                    dsw_k_norm=dk_norm, rel_bias=jnp.transpose(jnp.concatenate(drel, axis=0)),
                    gain=dgain, mod=(dsh, dsc, dgate))


def _local_step(x, target, mod, W, late_weights=None, early_pairs=None):
    mods = [[_row(mod[l, i]) for i in range(6)] for l in range(2)]
    nmix = [_row(W["norm_mix"][l]) for l in range(2)]
    nffn = [_row(W["norm_ffn"][l]) for l in range(2)]
    x1, s_gdn, arrived = _gdn_fwd(x, mods[0][:3], nmix[0], W, None if late_weights is None else late_weights[0])
    if late_weights is not None:
        W = {**W, **late_weights[1](arrived)}
    x2, s_f0 = _ffn_fwd(x1, mods[0][3:], nffn[0], W["w_ffn_in"][0], W["w_ffn_out"][0], "0")
    x3, s_dsw = _dsw_fwd(x2, mods[1][:3], nmix[1], W)
    x4, s_f1 = _ffn_fwd(x3, mods[1][3:], nffn[1], W["w_ffn_in"][1], W["w_ffn_out"][1], "1")
    dx4, sse = _loss_head(x4, target, name="loss_head")
    dx3, g_f1 = _ffn_bwd(dx4, s_f1, mods[1][3:], nffn[1], W["w_ffn_in"][1], W["w_ffn_out"][1], "1")
    dx2, g_dsw = _dsw_bwd(dx3, s_dsw, mods[1][:3], nmix[1], W)
    dx1, g_f0 = _ffn_bwd(dx2, s_f0, mods[0][3:], nffn[0], W["w_ffn_in"][0], W["w_ffn_out"][0], "0")
    grads = dict(
        w_ffn_in=jnp.stack([g_f0["w_in"], g_f1["w_in"]]), w_ffn_out=jnp.stack([g_f0["w_out"], g_f1["w_out"]]),
        dsw_w_in=g_dsw["dsw_w_in"][None], dsw_w_out=g_dsw["dsw_w_out"][None])
    riding = None if early_pairs is None else early_pairs(grads)
    dx0, g_gdn, rode = _gdn_bwd(dx1, s_gdn, mods[0][:3], nmix[0], W, riding)
    dmod = jnp.stack([jnp.concatenate(list(g_gdn["mod"]) + list(g_f0["mod"]), axis=0),
                      jnp.concatenate(list(g_dsw["mod"]) + list(g_f1["mod"]), axis=0)])
    grads.update(
        norm_mix=jnp.concatenate([g_gdn["gain"], g_dsw["gain"]], axis=0),
        norm_ffn=jnp.concatenate([g_f0["gain"], g_f1["gain"]], axis=0),
        gdn_w_in=g_gdn["gdn_w_in"][None], gdn_conv=g_gdn["gdn_conv"][None], gdn_w_out=g_gdn["gdn_w_out"][None],
        gdn_out_norm=g_gdn["gdn_out_norm"], gdn_a_log=g_gdn["gdn_a_log"], gdn_dt_bias=g_gdn["gdn_dt_bias"],
        dsw_q_norm=g_dsw["dsw_q_norm"], dsw_k_norm=g_dsw["dsw_k_norm"], rel_bias=g_dsw["rel_bias"])
    return sse, dx0, grads, dmod, rode


def _prepare_first(full, small):
    gw = full["gdn_w_in"][0]
    hk3 = 3 * GDN_HEADS * GDN_DK
    return dict(
        gdn_qkv=_hm(gw[:, :hk3]), gdn_z=gw[:, hk3:hk3 + GDN_HEADS * GDN_DK],
        gdn_ab=jnp.pad(gw[:, hk3 + GDN_HEADS * GDN_DK:], ((0, 0), (0, LANES - 2 * GDN_HEADS))),
        gdn_conv=_hm(full["gdn_conv"][0]), gdn_out=full["gdn_w_out"][0],
        norm_mix=small["norm_mix"], norm_ffn=small["norm_ffn"],
        gdn_a_log=small["gdn_a_log"].reshape(GDN_HEADS, 1, 1), gdn_dt_bias=small["gdn_dt_bias"].reshape(GDN_HEADS, 1, 1),
        gdn_out_norm=small["gdn_out_norm"], dsw_q_norm=small["dsw_q_norm"], dsw_k_norm=small["dsw_k_norm"],
        dsw_bias=_dsw_bias(small["rel_bias"]))


def _prepare_rest(full):
    di = full["dsw_w_in"][0]
    dq = di.shape[1] // 3
    return dict(w_ffn_in=full["w_ffn_in"], w_ffn_out=full["w_ffn_out"],
                dsw_q=di[:, :dq], dsw_k=di[:, dq:2 * dq], dsw_v=di[:, 2 * dq:], dsw_out=full["dsw_w_out"][0])


def _prepare_weights(full, small):
    return {**_prepare_first(full, small), **_prepare_rest(full)}


_W_NAMES = ("w_ada", "b_ada", "norm_mix", "norm_ffn", "w_ffn_in", "w_ffn_out", "gdn_w_in", "gdn_conv",
            "gdn_a_log", "gdn_dt_bias", "gdn_out_norm", "gdn_w_out", "dsw_w_in", "dsw_q_norm", "dsw_k_norm",
            "dsw_w_out", "rel_bias")
_PAD_BATCH = 16


def _pad_rows(a, rows):
    return jnp.pad(a, ((0, rows - a.shape[0]), (0, 0)))


def kernel(x, c, w_ada, b_ada, norm_mix, norm_ffn, w_ffn_in, w_ffn_out, gdn_w_in, gdn_conv, gdn_a_log, gdn_dt_bias, gdn_out_norm, gdn_w_out, dsw_w_in, dsw_q_norm, dsw_k_norm, dsw_w_out, rel_bias, loss_target, m_w_ada, m_b_ada, m_norm_mix, m_norm_ffn, m_w_ffn_in, m_w_ffn_out, m_gdn_w_in, m_gdn_conv, m_gdn_a_log, m_gdn_dt_bias, m_gdn_out_norm, m_gdn_w_out, m_dsw_w_in, m_dsw_q_norm, m_dsw_k_norm, m_dsw_w_out, m_rel_bias, v_w_ada, v_b_ada, v_norm_mix, v_norm_ffn, v_w_ffn_in, v_w_ffn_out, v_gdn_w_in, v_gdn_conv, v_gdn_a_log, v_gdn_dt_bias, v_gdn_out_norm, v_gdn_w_out, v_dsw_w_in, v_dsw_q_norm, v_dsw_k_norm, v_dsw_w_out, v_rel_bias):
    w = dict(zip(_W_NAMES, (w_ada, b_ada, norm_mix, norm_ffn, w_ffn_in, w_ffn_out, gdn_w_in, gdn_conv, gdn_a_log,
                            gdn_dt_bias, gdn_out_norm, gdn_w_out, dsw_w_in, dsw_q_norm, dsw_k_norm, dsw_w_out,
                            rel_bias)))
    m = dict(zip(_W_NAMES, (m_w_ada, m_b_ada, m_norm_mix, m_norm_ffn, m_w_ffn_in, m_w_ffn_out, m_gdn_w_in,
                            m_gdn_conv, m_gdn_a_log, m_gdn_dt_bias, m_gdn_out_norm, m_gdn_w_out, m_dsw_w_in,
                            m_dsw_q_norm, m_dsw_k_norm, m_dsw_w_out, m_rel_bias)))
    v = dict(zip(_W_NAMES, (v_w_ada, v_b_ada, v_norm_mix, v_norm_ffn, v_w_ffn_in, v_w_ffn_out, v_gdn_w_in,
                            v_gdn_conv, v_gdn_a_log, v_gdn_dt_bias, v_gdn_out_norm, v_gdn_w_out, v_dsw_w_in,
                            v_dsw_q_norm, v_dsw_k_norm, v_dsw_w_out, v_rel_bias)))
    D = x.shape[-1]
    n_layers, _, ada_cols = w_ada.shape

    c_all = _exchange(c.reshape(D // LANES, LANES), gather=True, name="gather_cond").reshape(N_DEV, D)
    c_pad = _pad_rows(c_all, _PAD_BATCH)
    proj = [_mm(c_pad, w_ada[l], mode="nn", name=f"ada_proj_{l}", tm=_PAD_BATCH, tn=ada_cols, tk=D, a_silu=True)
            for l in range(n_layers)]
    mod_send = _pack([(jnp.stack([p[:N_DEV] for p in proj], axis=1), 1)], _ROW_ALIGN)
    mod_recv = _exchange(mod_send, gather=False, name="scatter_mod")
    mod = _unpack(mod_recv, [(n_layers, ada_cols)], 1)[0]
    mod = jnp.transpose(mod, (1, 0, 2)).reshape(n_layers, N_DEV * ada_cols) + b_ada
    mod = mod.reshape(n_layers, 6, D)

    conv_hi = gdn_conv.astype(BF16)
    conv_lo = (gdn_conv - conv_hi.astype(F32)).astype(BF16)
    first_send = _pack([(conv_hi if n == "gdn_conv" else w[n].astype(BF16), 0) for n in _LATE] + [(conv_lo, 0)],
                       _ROW_ALIGN)
    parts = _unpack(_gather_two_level(first_send, name="gather_weights_first"),
                    [w[n].shape for n in _LATE] + [gdn_conv.shape], 1)
    full = {n: _to_natural(parts[i], _SHARD_AXIS[n]) for i, n in enumerate(_LATE)}
    full["gdn_conv"] = full["gdn_conv"].astype(F32) + _to_natural(parts[-1], _SHARD_AXIS["gdn_conv"]).astype(F32)
    W = _prepare_first(full, {n: w[n] for n in _SMALL})
    packed_early = tuple(n for n in _EARLY if n not in _NATIVE)
    rest_send = (_pack([(w[n].astype(BF16), 0) for n in packed_early], _ROW_ALIGN),
                 ) + tuple(w[n].astype(BF16) for n in _NATIVE)

    def rest_weights(arrived):
        filled = _fill_from_sibling(arrived, name="swap_weights")
        by_dev = [a.reshape((N_DEV,) + a.shape[2:]) for a in filled]
        blocks = dict(zip(packed_early, _unpack(by_dev[0], [w[n].shape for n in packed_early], 1)))
        blocks.update(zip(_NATIVE, by_dev[1:]))
        return _prepare_rest({n: _to_natural(blocks[n], _SHARD_AXIS[n]) for n in _EARLY})

    my_c = lax.axis_index("c")

    def pair_sums(g, packed, native, tag):
        sends = [_pack([(_to_blocked(g[n].astype(BF16), _SHARD_AXIS[n]), 1) for n in packed], _BIG_ALIGN)]
        sends += [_to_blocked(g[n].astype(BF16), _SHARD_AXIS[n]) for n in native]
        by_core = [s.reshape((N_DEV // 2, 2) + s.shape[1:]) for s in sends]
        keep = [lax.dynamic_index_in_dim(s, my_c, axis=1, keepdims=False) for s in by_core]
        give = [lax.dynamic_index_in_dim(s, 1 - my_c, axis=1, keepdims=False) for s in by_core]
        got = _swap_with_sibling(give, name=f"swap_grads_{tag}")
        return tuple(_add_pair(k, t, name=f"add_sibling_grads_{tag}_{j}") for j, (k, t) in enumerate(zip(keep, got)))

    sse, grad_x, grads, dmod, early_recv = _local_step(
        x[0], loss_target[0], mod, W, late_weights=(rest_send, rest_weights),
        early_pairs=lambda g: pair_sums(g, packed_early, _NATIVE, "early"))
    loss = lax.psum(0.5 * sse[0, 0] / D, ("x", "y", "c"))
    grads["b_ada"] = dmod.reshape(n_layers, 6 * D)
    late_recv = _exchange_chips(pair_sums(grads, _LATE, (), "late")[0], name="scatter_grads_late")
    g_parts = dict(zip(packed_early, _unpack(early_recv[0], [w[n].shape for n in packed_early], 1)))
    g_parts.update(zip(_NATIVE, early_recv[1:]))
    g_parts.update(zip(_LATE, _unpack(late_recv, [w[n].shape for n in _LATE], 1)))

    dmod_send = _pack([(jnp.transpose(dmod.reshape(n_layers, N_DEV, ada_cols), (1, 0, 2)), 1)], _ROW_ALIGN)
    small_send = _pack([(grads[n].reshape(w[n].shape), 0) for n in _SMALL], _ROW_ALIGN)
    s_recv = _exchange(jnp.concatenate(
        [dmod_send, jnp.broadcast_to(small_send[None], (N_DEV,) + small_send.shape)], axis=1),
        gather=False, name="scatter_small")
    dmod_rows = dmod_send.shape[1]

    out = {}
    kinds = ("grad", "delta", "new_m", "new_v")
    for n in _BIG:
        g4 = g_parts[n]
        rows2d = lambda a: a.reshape((-1, w[n].shape[-1]))
        res = _adamw(rows2d(w[n]), g4.reshape((g4.shape[0], -1, w[n].shape[-1])), rows2d(m[n]), rows2d(v[n]),
                     name=f"adamw_{n}")
        for kind, buf in zip(kinds, res):
            out[kind, n] = buf.reshape(w[n].shape)

    dmod_all = _unpack(lax.slice_in_dim(s_recv, 0, dmod_rows, axis=1), [(n_layers, ada_cols)], 1)[0]
    g_ada = jnp.stack([_mm(c_pad, _pad_rows(dmod_all[:, l], _PAD_BATCH), mode="tn", name=f"ada_dw_{l}",
                           tm=D, tn=ada_cols, tk=_PAD_BATCH, a_silu=True) for l in range(n_layers)])
    flat = lambda a: a.reshape(n_layers * D, ada_cols)
    res = _adamw(flat(w_ada), flat(g_ada)[None], flat(m_w_ada), flat(v_w_ada), name="adamw_ada")
    for kind, buf in zip(("grad", "delta", "new_m", "new_v"), res):
        out[kind, "w_ada"] = buf.reshape(w_ada.shape)

    small_parts = lax.slice_in_dim(s_recv, dmod_rows, s_recv.shape[1], axis=1)
    packed = [_pack([(t[n], 0) for n in _SMALL], _ROW_ALIGN) for t in (w, m, v)]
    res = _adamw(packed[0], small_parts, packed[1], packed[2], name="adamw_replicated")
    for kind, buf in zip(("grad", "delta", "new_m", "new_v"), res):
        for n, a in zip(_SMALL, _unpack(buf, [w[n].shape for n in _SMALL], 0)):
            out[kind, n] = a

    return (loss, grad_x[None]) + tuple(out[kind, n] for kind in ("grad", "delta", "new_m", "new_v")
                                        for n in _W_NAMES)
```

```python
import functools
import math

import numpy as np
import jax
import jax.numpy as jnp
from jax import lax
from jax.experimental import pallas as pl
from jax.experimental.pallas import tpu as pltpu

F32 = jnp.float32
BF16 = jnp.bfloat16

N_DEV = 8
RMS_EPS = 1e-6
LANES = 128
V7X_VMEM_LIMIT = 48 * 1024 * 1024

GDN_HEADS = 8
GDN_DK = 128
GDN_CHUNK = 64
GDN_CONV = 4
DSW_GROUPS = ((128, 1), (512, 4), (2048, 16))
DSW_HEADS = 8
DSW_DH = 64
DSW_BLK = 128
REL_BUCKETS = 32
REL_MAX_DIST = 2048

ADAM_LR = 0.001
ADAM_B1 = 0.9
ADAM_B2 = 0.999
ADAM_EPS = 1e-08
ADAM_WD = 0.01
ADAM_STEP = 10

NEG_BIG = -1e30


def _params(*sem):
    return pltpu.CompilerParams(dimension_semantics=sem, vmem_limit_bytes=V7X_VMEM_LIMIT)


def _sigmoid(x):
    return 1.0 / (1.0 + jnp.exp(-x))


def _silu(x):
    return x * _sigmoid(x)


_DOT_DIMS = {
    "nn": (((1,), (0,)), ((), ())),
    "nt": (((1,), (1,)), ((), ())),
    "tn": (((0,), (0,)), ((), ())),
}


def _mm(a, b, *, mode, name, tm, tn, tk, out_dtype=F32, a_scale=None, out_scale=None, resid=None, a_silu=False):
    if mode == "nn":
        (M, K), N = a.shape, b.shape[1]
    elif mode == "nt":
        (M, K), N = a.shape, b.shape[0]
    else:
        (K, M), N = a.shape, b.shape[1]
    tm, tn, tk = min(tm, M), min(tn, N), min(tk, K)
    assert M % tm == 0 and N % tn == 0 and K % tk == 0, (name, M, N, K, tm, tn, tk)
    nk = K // tk

    def body(*refs):
        refs = list(refs)
        a_ref, b_ref = refs.pop(0), refs.pop(0)
        as_ref = refs.pop(0) if a_scale is not None else None
        os_ref = refs.pop(0) if out_scale is not None else None
        r_ref = refs.pop(0) if resid is not None else None
        o_ref = refs.pop(0)
        acc_ref = refs.pop(0) if nk > 1 else None

        av = a_ref[...]
        if a_silu:
            av = _silu(av.astype(F32))
        if as_ref is not None:
            av = av.astype(F32) * as_ref[...]
        part = lax.dot_general(av.astype(BF16), b_ref[...].astype(BF16), _DOT_DIMS[mode],
                               preferred_element_type=F32)

        def finish(r):
            if os_ref is not None:
                r = r * os_ref[...]
            if r_ref is not None:
                r = r + r_ref[...].astype(F32)
            o_ref[...] = r.astype(out_dtype)

        if nk == 1:
            finish(part)
        else:
            k = pl.program_id(2)

            @pl.when(k == 0)
            def _():
                acc_ref[...] = part

            @pl.when(k > 0)
            def _():
                acc_ref[...] += part

            @pl.when(k == nk - 1)
            def _():
                finish(acc_ref[...])

    if mode == "nn":
        a_spec = pl.BlockSpec((tm, tk), lambda i, j, k: (i, k))
        b_spec = pl.BlockSpec((tk, tn), lambda i, j, k: (k, j))
        as_spec = pl.BlockSpec((1, tk), lambda i, j, k: (0, k))
    elif mode == "nt":
        a_spec = pl.BlockSpec((tm, tk), lambda i, j, k: (i, k))
        b_spec = pl.BlockSpec((tn, tk), lambda i, j, k: (j, k))
        as_spec = pl.BlockSpec((1, tk), lambda i, j, k: (0, k))
    else:
        a_spec = pl.BlockSpec((tk, tm), lambda i, j, k: (k, i))
        b_spec = pl.BlockSpec((tk, tn), lambda i, j, k: (k, j))
        as_spec = None
    in_specs, args = [a_spec, b_spec], [a, b]
    if a_scale is not None:
        in_specs.append(as_spec)
        args.append(a_scale)
    if out_scale is not None:
        in_specs.append(pl.BlockSpec((1, tn), lambda i, j, k: (0, j)))
        args.append(out_scale)
    if resid is not None:
        in_specs.append(pl.BlockSpec((tm, tn), lambda i, j, k: (i, j)))
        args.append(resid)
    return pl.pallas_call(
        body, name=name, grid=(M // tm, N // tn, nk),
        in_specs=in_specs, out_specs=pl.BlockSpec((tm, tn), lambda i, j, k: (i, j)),
        out_shape=jax.ShapeDtypeStruct((M, N), out_dtype),
        scratch_shapes=[pltpu.VMEM((tm, tn), F32)] if nk > 1 else [],
        compiler_params=_params("parallel", "parallel", "arbitrary"),
    )(*args)


_MM_ROWS = 1024


def _mm_sum_nt(pairs, *, name, tm=_MM_ROWS, tn=1024):
    M, N = pairs[0][0].shape[0], pairs[0][1].shape[0]
    tm, tn = _tile(M, tm), _tile(N, tn)
    spans, start = [], 0
    for a, b, tk, off in pairs:
        K = a.shape[1]
        assert a.shape[0] == M and b.shape[0] == N and K % tk == 0 and off % tk == 0, name
        spans.append((start, K // tk, tk, off // tk))
        start += K // tk
    total = start

    def body(*refs):
        o_ref, acc_ref = refs[-2:]
        k = pl.program_id(2)

        @pl.when(k == 0)
        def _():
            acc_ref[...] = jnp.zeros_like(acc_ref)

        for p, (s0, nk, _, _) in enumerate(spans):
            a_ref, b_ref = refs[2 * p], refs[2 * p + 1]

            @pl.when((k >= s0) & (k < s0 + nk))
            def _():
                acc_ref[...] += lax.dot_general(a_ref[...].astype(BF16), b_ref[...].astype(BF16), _DOT_DIMS["nt"],
                                                preferred_element_type=F32)

        @pl.when(k == total - 1)
        def _():
            o_ref[...] = acc_ref[...]

    def spec(rows, tk, s0, nk, koff, axis):
        def index(i, j, k):
            return ((i, j)[axis], jnp.clip(k - s0, 0, nk - 1) + koff)
        return pl.BlockSpec((rows, tk), index)

    in_specs, args = [], []
    for (a, b, _, _), (s0, nk, tk, koff) in zip(pairs, spans):
        in_specs += [spec(tm, tk, s0, nk, 0, 0), spec(tn, tk, s0, nk, koff, 1)]
        args += [a, b]
    return pl.pallas_call(
        body, name=name, grid=(M // tm, N // tn, total), in_specs=in_specs,
        out_specs=pl.BlockSpec((tm, tn), lambda i, j, k: (i, j)),
        out_shape=jax.ShapeDtypeStruct((M, N), F32), scratch_shapes=[pltpu.VMEM((tm, tn), F32)],
        compiler_params=_params("parallel", "parallel", "arbitrary"),
    )(*args)


def _norm_mod_fwd(x, gain, sc, sh, *, name):
    S, D = x.shape
    tr = min(512, S)

    def body(x_ref, g_ref, sc_ref, sh_ref, h_ref):
        xv = x_ref[...]
        r = lax.rsqrt(jnp.mean(xv * xv, axis=-1, keepdims=True) + RMS_EPS)
        h_ref[...] = ((xv * r) * g_ref[...] * (1.0 + sc_ref[...]) + sh_ref[...]).astype(BF16)

    row = pl.BlockSpec((tr, D), lambda i: (i, 0))
    vec = pl.BlockSpec((1, D), lambda i: (0, 0))
    return pl.pallas_call(
        body, name=name, grid=(S // tr,), in_specs=[row, vec, vec, vec], out_specs=row,
        out_shape=jax.ShapeDtypeStruct((S, D), BF16), compiler_params=_params("parallel"),
    )(x, gain, sc, sh)


def _norm_mod_bwd(dh, x, dx_res, gain, sc, *, name):
    S, D = x.shape
    tr = min(256, S)
    n_steps = S // tr

    def body(dh_ref, x_ref, dxr_ref, g_ref, sc_ref, dx_ref, dsh_ref, dsc_ref, dgain_ref, acc_sh, acc_a):
        i = pl.program_id(0)
        xv = x_ref[...]
        r = lax.rsqrt(jnp.mean(xv * xv, axis=-1, keepdims=True) + RMS_EPS)
        n = xv * r
        dhv = dh_ref[...].astype(F32)
        dn = dhv * (g_ref[...] * (1.0 + sc_ref[...]))
        dx_ref[...] = dxr_ref[...] + r * (dn - n * jnp.mean(dn * n, axis=-1, keepdims=True))
        p_sh = jnp.sum(dhv, axis=0, keepdims=True)
        p_a = jnp.sum(dhv * n, axis=0, keepdims=True)

        @pl.when(i == 0)
        def _():
            acc_sh[...] = p_sh
            acc_a[...] = p_a

        @pl.when(i > 0)
        def _():
            acc_sh[...] += p_sh
            acc_a[...] += p_a

        @pl.when(i == n_steps - 1)
        def _():
            dsh_ref[...] = acc_sh[...]
            dsc_ref[...] = acc_a[...] * g_ref[...]
            dgain_ref[...] = acc_a[...] * (1.0 + sc_ref[...])

    row = pl.BlockSpec((tr, D), lambda i: (i, 0))
    vec = pl.BlockSpec((1, D), lambda i: (0, 0))
    vshape = jax.ShapeDtypeStruct((1, D), F32)
    return pl.pallas_call(
        body, name=name, grid=(n_steps,), in_specs=[row, row, row, vec, vec],
        out_specs=[row, vec, vec, vec],
        out_shape=[jax.ShapeDtypeStruct((S, D), F32), vshape, vshape, vshape],
        scratch_shapes=[pltpu.VMEM((1, D), F32), pltpu.VMEM((1, D), F32)],
        compiler_params=_params("arbitrary"),
    )(dh, x, dx_res, gain, sc)


def _wout_grad(gmat, w, gate, *, name):
    K, D = w.shape
    tr = min(256, K)
    n_steps = K // tr

    def body(g_ref, w_ref, gate_ref, dw_ref, dgate_ref, acc):
        i = pl.program_id(0)
        gv = g_ref[...]
        dw_ref[...] = (gv * gate_ref[...]).astype(BF16)
        part = jnp.sum(gv * w_ref[...], axis=0, keepdims=True)

        @pl.when(i == 0)
        def _():
            acc[...] = part

        @pl.when(i > 0)
        def _():
            acc[...] += part

        @pl.when(i == n_steps - 1)
        def _():
            dgate_ref[...] = acc[...]

    row = pl.BlockSpec((tr, D), lambda i: (i, 0))
    vec = pl.BlockSpec((1, D), lambda i: (0, 0))
    return pl.pallas_call(
        body, name=name, grid=(n_steps,), in_specs=[row, row, vec], out_specs=[row, vec],
        out_shape=[jax.ShapeDtypeStruct((K, D), BF16), jax.ShapeDtypeStruct((1, D), F32)],
        scratch_shapes=[pltpu.VMEM((1, D), F32)], compiler_params=_params("arbitrary"),
    )(gmat, w, gate)


def _loss_head(y, target, *, name):
    S, D = y.shape
    tr = min(512, S)
    n_steps = S // tr

    def body(y_ref, t_ref, dy_ref, sse_ref, acc):
        i = pl.program_id(0)
        e = y_ref[...] - t_ref[...]
        dy_ref[...] = e * (1.0 / D)
        part = jnp.sum(e * e, axis=0, keepdims=True)

        @pl.when(i == 0)
        def _():
            acc[...] = part

        @pl.when(i > 0)
        def _():
            acc[...] += part

        @pl.when(i == n_steps - 1)
        def _():
            sse_ref[...] = jnp.sum(acc[...], axis=1, keepdims=True)

    row = pl.BlockSpec((tr, D), lambda i: (i, 0))
    return pl.pallas_call(
        body, name=name, grid=(n_steps,), in_specs=[row, row],
        out_specs=[row, pl.BlockSpec((1, 1), lambda i: (0, 0))],
        out_shape=[jax.ShapeDtypeStruct((S, D), F32), jax.ShapeDtypeStruct((1, 1), F32)],
        scratch_shapes=[pltpu.VMEM((1, D), F32)], compiler_params=_params("arbitrary"),
    )(y, target)


def _adamw(w, g_parts, m, v, *, name):
    R, C = w.shape
    P = g_parts.shape[0]
    tr = _tile(R, max(8, 1024 * LANES // C))
    c1 = 1.0 / (1.0 - ADAM_B1 ** ADAM_STEP)
    c2 = 1.0 / (1.0 - ADAM_B2 ** ADAM_STEP)

    def body(w_ref, g_ref, m_ref, v_ref, go_ref, d_ref, mo_ref, vo_ref):
        g = g_ref[0].astype(F32)
        for q in range(1, P):
            g = g + g_ref[q].astype(F32)
        mn = ADAM_B1 * m_ref[...] + (1.0 - ADAM_B1) * g
        vn = ADAM_B2 * v_ref[...] + (1.0 - ADAM_B2) * (g * g)
        go_ref[...] = g
        mo_ref[...] = mn
        vo_ref[...] = vn
        d_ref[...] = -ADAM_LR * ((mn * c1) / (jnp.sqrt(vn * c2) + ADAM_EPS) + ADAM_WD * w_ref[...])

    row = pl.BlockSpec((tr, C), lambda i: (i, 0))
    shp = jax.ShapeDtypeStruct((R, C), F32)
    return pl.pallas_call(
        body, name=name, grid=(R // tr,),
        in_specs=[row, pl.BlockSpec((P, tr, C), lambda i: (0, i, 0)), row, row],
        out_specs=[row, row, row, row], out_shape=[shp, shp, shp, shp],
        compiler_params=_params("parallel"),
    )(w, g_parts, m, v)


_HALO = 16


def _conv_taps(buf, w_ref, rows, cols):
    acc = None
    for j in range(GDN_CONV):
        term = buf[pl.ds(_HALO - (GDN_CONV - 1) + j, rows), cols] * w_ref[j:j + 1, cols]
        acc = term if acc is None else acc + term
    return acc


def _fill_conv_buf(buf, halo_ref, x_ref, rows, first):
    buf[0:_HALO, :] = jnp.where(first, 0.0, halo_ref[...].astype(F32))
    buf[_HALO:_HALO + rows, :] = x_ref[...].astype(F32)


_HM = 3 * GDN_DK
_GDN_ROWS = 256
_PREP_HEADS = 4


def _l2n(seg):
    return lax.rsqrt(jnp.sum(seg * seg, axis=-1, keepdims=True) + RMS_EPS)


def _head_cols(hh):
    return slice(hh * _HM, (hh + 1) * _HM)


def _gdn_prep_fwd(x, conv_w, *, name):
    S, C3 = x.shape
    CB = _PREP_HEADS * _HM
    RB = min(256, S)

    def body(x_ref, halo_ref, w_ref, o_ref, buf):
        i = pl.program_id(0)
        _fill_conv_buf(buf, halo_ref, x_ref, RB, i == 0)
        for hh in range(_PREP_HEADS):
            c0 = hh * _HM
            y = _silu(_conv_taps(buf, w_ref, RB, _head_cols(hh)))
            q, k = y[:, :GDN_DK], y[:, GDN_DK:2 * GDN_DK]
            o_ref[:, c0:c0 + GDN_DK] = q * (_l2n(q) * GDN_DK ** -0.5)
            o_ref[:, c0 + GDN_DK:c0 + 2 * GDN_DK] = k * _l2n(k)
            o_ref[:, c0 + 2 * GDN_DK:c0 + _HM] = y[:, 2 * GDN_DK:]

    hb = RB // _HALO
    return pl.pallas_call(
        body, name=name, grid=(S // RB, C3 // CB),
        in_specs=[pl.BlockSpec((RB, CB), lambda i, j: (i, j)),
                  pl.BlockSpec((_HALO, CB), lambda i, j: (jnp.maximum(i * hb - 1, 0), j)),
                  pl.BlockSpec((GDN_CONV, CB), lambda i, j: (0, j))],
        out_specs=pl.BlockSpec((RB, CB), lambda i, j: (i, j)),
        out_shape=jax.ShapeDtypeStruct((S, C3), F32),
        scratch_shapes=[pltpu.VMEM((RB + _HALO, CB), F32)],
        compiler_params=_params("parallel", "parallel"),
    )(x, x, conv_w)


def _gdn_prep_bwd_pre(dn, x, conv_w, *, name):
    S, C3 = x.shape
    CB = _PREP_HEADS * _HM
    RB = min(256, S)
    n_steps = S // RB

    def body(dn_ref, x_ref, halo_ref, w_ref, dc_ref, dw_ref, buf):
        i = pl.program_id(1)
        _fill_conv_buf(buf, halo_ref, x_ref, RB, i == 0)
        head_parts = []
        for hh in range(_PREP_HEADS):
            c0, cols = hh * _HM, _head_cols(hh)
            acc = _conv_taps(buf, w_ref, RB, cols)
            sg = _sigmoid(acc)
            y = acc * sg
            dsilu = sg * (1.0 + acc * (1.0 - sg))
            for part, scale in ((0, GDN_DK ** -0.5), (1, 1.0)):
                sl = slice(part * GDN_DK, (part + 1) * GDN_DK)
                seg = y[:, sl]
                r = _l2n(seg)
                n = seg * r
                d = dn_ref[:, c0 + part * GDN_DK:c0 + (part + 1) * GDN_DK] * scale
                dc_ref[:, c0 + part * GDN_DK:c0 + (part + 1) * GDN_DK] = (
                    r * (d - n * jnp.sum(d * n, axis=-1, keepdims=True)) * dsilu[:, sl])
            dc_ref[:, c0 + 2 * GDN_DK:c0 + _HM] = dn_ref[:, c0 + 2 * GDN_DK:c0 + _HM] * dsilu[:, 2 * GDN_DK:]
            dc = dc_ref[:, cols]
            taps = [jnp.sum(dc * buf[pl.ds(_HALO - (GDN_CONV - 1) + t, RB), cols], axis=0, keepdims=True)
                    for t in range(GDN_CONV)]
            head_parts.append(jnp.concatenate(taps + [jnp.zeros((8 - GDN_CONV, _HM), F32)], axis=0))
        part = jnp.concatenate(head_parts, axis=1)

        @pl.when(i == 0)
        def _():
            dw_ref[...] = part

        @pl.when(i > 0)
        def _():
            dw_ref[...] += part

    hb = RB // _HALO
    return pl.pallas_call(
        body, name=name, grid=(C3 // CB, n_steps),
        in_specs=[pl.BlockSpec((RB, CB), lambda j, i: (i, j)),
                  pl.BlockSpec((RB, CB), lambda j, i: (i, j)),
                  pl.BlockSpec((_HALO, CB), lambda j, i: (jnp.maximum(i * hb - 1, 0), j)),
                  pl.BlockSpec((GDN_CONV, CB), lambda j, i: (0, j))],
        out_specs=[pl.BlockSpec((RB, CB), lambda j, i: (i, j)),
                   pl.BlockSpec((8, CB), lambda j, i: (0, j))],
        out_shape=[jax.ShapeDtypeStruct((S, C3), F32), jax.ShapeDtypeStruct((8, C3), F32)],
        scratch_shapes=[pltpu.VMEM((RB + _HALO, CB), F32)],
        compiler_params=_params("parallel", "arbitrary"),
    )(dn, x, x, conv_w)


def _gdn_conv_bwd_x(dc, conv_w, *, name):
    S, C3 = dc.shape
    CB = _PREP_HEADS * _HM
    RB = min(256, S)
    n_steps = S // RB

    def body(dc_ref, halo_ref, w_ref, dx_ref, buf):
        i = pl.program_id(0)
        buf[0:RB, :] = dc_ref[...]
        buf[RB:RB + _HALO, :] = jnp.where(i == n_steps - 1, 0.0, halo_ref[...])
        for hh in range(_PREP_HEADS):
            cols = _head_cols(hh)
            acc = None
            for j in range(GDN_CONV):
                term = buf[pl.ds(GDN_CONV - 1 - j, RB), cols] * w_ref[j:j + 1, cols]
                acc = term if acc is None else acc + term
            dx_ref[:, cols] = acc.astype(BF16)

    hb = RB // _HALO
    last = S // _HALO - 1
    return pl.pallas_call(
        body, name=name, grid=(n_steps, C3 // CB),
        in_specs=[pl.BlockSpec((RB, CB), lambda i, j: (i, j)),
                  pl.BlockSpec((_HALO, CB), lambda i, j: (jnp.minimum((i + 1) * hb, last), j)),
                  pl.BlockSpec((GDN_CONV, CB), lambda i, j: (0, j))],
        out_specs=pl.BlockSpec((RB, CB), lambda i, j: (i, j)),
        out_shape=jax.ShapeDtypeStruct((S, C3), BF16),
        scratch_shapes=[pltpu.VMEM((RB + _HALO, CB), F32)],
        compiler_params=_params("parallel", "parallel"),
    )(dc, dc, conv_w)


def _split_bf16(a):
    hi = a.astype(BF16)
    return hi, (a - hi.astype(F32)).astype(BF16)


def _dot(a, b, dims="nn", exact=False):
    def dot(p, q):
        return lax.dot_general(p, q, _DOT_DIMS[dims], preferred_element_type=F32)

    if exact:
        (ah, al), (bh, bl) = _split_bf16(a), _split_bf16(b)
        return dot(ah, bh) + (dot(ah, bl) + dot(al, bh))
    return dot(a.astype(BF16), b.astype(BF16))


def _softplus(x):
    return jnp.maximum(x, 0.0) + jnp.log(1.0 + jnp.exp(-jnp.abs(x)))


def _to_col(row, eye):
    return jnp.sum(jnp.where(eye, row, 0.0), axis=1, keepdims=True)


def _to_row(col, eye):
    return jnp.sum(jnp.where(eye, col, 0.0), axis=0, keepdims=True)


def _unit_lower_inverse(low, ri, ci):
    n = range(len(low))
    C = low[0].shape[0]
    eye = jnp.where(ri == ci, 1.0, 0.0)
    pair = (ri >> 1) == (ci >> 1)
    x = [eye - jnp.where(pair, low[j], 0.0) for j in n]
    m, sh = 2, 1
    while m < C:
        join = ((ri >> (sh + 1)) == (ci >> (sh + 1))) & (((ri >> sh) & 1) == 1) & (((ci >> sh) & 1) == 0)
        y = [_dot(x[j], jnp.where(join, low[j], 0.0)) for j in n]
        x = [x[j] - _dot(y[j], x[j]) for j in n]
        m, sh = 2 * m, sh + 1
    lx = [_dot(low[j], x[j], exact=True) for j in n]
    corr = [_dot(x[j], eye - x[j] - lx[j]) for j in n]
    return [x[j] + corr[j] for j in n]


def _gdn_local_batch(qkv, g_row, beta_row, ri, ci):
    n = range(len(qkv))
    eye, tril, strict = ri == ci, ri >= ci, ri > ci
    q = [qkv[j][:, :GDN_DK] for j in n]
    k = [qkv[j][:, GDN_DK:2 * GDN_DK] for j in n]
    v = [qkv[j][:, 2 * GDN_DK:] for j in n]
    g_col = [_to_col(g_row[j], eye) for j in n]
    beta_col = [_to_col(beta_row[j], eye) for j in n]
    gc_col = [jnp.sum(jnp.where(tril, g_row[j], 0.0), axis=1, keepdims=True) for j in n]
    gc_row = [jnp.sum(jnp.where(ri <= ci, g_col[j], 0.0), axis=0, keepdims=True) for j in n]
    g_last = [jnp.sum(g_row[j], axis=1, keepdims=True) for j in n]
    decay = [jnp.where(tril, jnp.exp(jnp.minimum(gc_col[j] - gc_row[j], 0.0)), 0.0) for j in n]
    e_col = [jnp.exp(gc_col[j]) for j in n]
    f_col = [jnp.exp(g_last[j] - gc_col[j]) for j in n]
    e_last = [jnp.exp(g_last[j]) for j in n]
    kb = [k[j] * beta_col[j] for j in n]
    vb = [v[j] * beta_col[j] for j in n]
    kk = [_dot(kb[j], k[j], "nt") for j in n]
    qk = [_dot(q[j], k[j], "nt") for j in n]
    low = [jnp.where(strict, kk[j] * decay[j], 0.0) for j in n]
    att = [qk[j] * decay[j] for j in n]
    return dict(q=q, k=k, v=v, beta_col=beta_col, decay=decay, e_col=e_col, f_col=f_col, e_last=e_last,
                kb=kb, vb=vb, low=low, att=att, eye=eye, strict=strict, tril=tril)


def _chunk_iotas():
    C = GDN_CHUNK
    return lax.broadcasted_iota(jnp.int32, (C, C), 0), lax.broadcasted_iota(jnp.int32, (C, C), 1)


def _gdn_chunk_fwd(qkv, ab, a_log, dt_bias, *, name, riding=None):
    S = qkv.shape[0]
    H, C, DK = GDN_HEADS, GDN_CHUNK, GDN_DK
    RB = min(_GDN_ROWS, S)
    NCB, NB, NC = RB // C, S // RB, S // C
    heads = range(H)

    def body(qkv_ref, ab_ref, alog_ref, dtb_ref, *rest):
        n_ride = 0 if riding is None else len(riding)
        ride_srcs, rest = rest[:n_ride], rest[n_ride:]
        (o_ref, st_ref, t_ref), rest = rest[:3], rest[3:]
        ride_dsts, rest = rest[:n_ride], rest[n_ride:]
        state, u_s, w_s, qe_s, kf_s, att_s, *ride_sems = rest
        nb = pl.program_id(0)
        if riding is not None:
            finish_ride = _ride(nb == 0, nb == NB - 1, ride_srcs, ride_dsts, ride_sems, True)

        @pl.when(nb == 0)
        def _():
            state[...] = jnp.zeros_like(state)

        ri, ci = _chunk_iotas()
        neg_a = [-jnp.exp(alog_ref[h]) for h in heads]
        e_last = []
        for c in range(NCB):
            rows = pl.ds(c * C, C)
            g_row = [neg_a[h] * _softplus(ab_ref[h, c] + dtb_ref[h]) for h in heads]
            beta_row = [_sigmoid(ab_ref[H + h, c]) for h in heads]
            L = _gdn_local_batch([qkv_ref[rows, h * _HM:(h + 1) * _HM] for h in heads], g_row, beta_row, ri, ci)
            tinv = _unit_lower_inverse(L["low"], ri, ci)
            u = [_dot(tinv[h], L["vb"][h], exact=True) for h in heads]
            w = [_dot(tinv[h], L["kb"][h] * L["e_col"][h], exact=True) for h in heads]
            for h in heads:
                t_ref[h, c] = tinv[h]
                u_s[c, h] = u[h]
                w_s[c, h] = w[h].astype(BF16)
                qe_s[c, h] = (L["q"][h] * L["e_col"][h]).astype(BF16)
                kf_s[c, h] = (L["k"][h] * L["f_col"][h]).astype(BF16)
                att_s[c, h] = L["att"][h].astype(BF16)
            e_last.append(L["e_last"])
        st = [state[h] for h in heads]
        for c in range(NCB):
            rows = pl.ds(c * C, C)
            stb = [st[h].astype(BF16) for h in heads]
            vn = [u_s[c, h] - _dot(w_s[c, h], stb[h]) for h in heads]
            vnb = [vn[h].astype(BF16) for h in heads]
            out = [_dot(qe_s[c, h], stb[h]) + _dot(att_s[c, h], vnb[h]) for h in heads]
            new = [st[h] * e_last[c][h] + _dot(kf_s[c, h], vnb[h], "tn") for h in heads]
            for h in heads:
                o_ref[rows, h * DK:(h + 1) * DK] = out[h]
                st_ref[h, c] = st[h]
            st = new
        for h in heads:
            state[h] = st[h]
        if riding is not None:
            finish_ride()

    ride_args, ride_specs, ride_out, ride_scratch = _riding(riding, True)
    return pl.pallas_call(
        body, name=name, grid=(NB,),
        in_specs=[pl.BlockSpec((RB, H * _HM), lambda n: (n, 0)),
                  pl.BlockSpec((2 * H, NCB, 1, C), lambda n: (0, n, 0, 0)),
                  pl.BlockSpec((H, 1, 1), lambda n: (0, 0, 0)),
                  pl.BlockSpec((H, 1, 1), lambda n: (0, 0, 0))] + ride_specs,
        out_specs=[pl.BlockSpec((RB, H * DK), lambda n: (n, 0)),
                   pl.BlockSpec((H, NCB, DK, DK), lambda n: (0, n, 0, 0)),
                   pl.BlockSpec((H, NCB, C, C), lambda n: (0, n, 0, 0))] + ride_specs,
        out_shape=[jax.ShapeDtypeStruct((S, H * DK), F32),
                   jax.ShapeDtypeStruct((H, NC, DK, DK), F32),
                   jax.ShapeDtypeStruct((H, NC, C, C), F32)] + ride_out,
        scratch_shapes=[pltpu.VMEM((H, DK, DK), F32), pltpu.VMEM((NCB, H, C, DK), F32),
                        pltpu.VMEM((NCB, H, C, DK), BF16), pltpu.VMEM((NCB, H, C, DK), BF16),
                        pltpu.VMEM((NCB, H, C, DK), BF16), pltpu.VMEM((NCB, H, C, C), BF16)] + ride_scratch,
        compiler_params=_params("arbitrary"),
    )(qkv, ab, a_log, dt_bias, *ride_args)


_CHIP_PEERS = N_DEV // 2 - 1


def _chip_copies(src_refs, dst_refs, send_sems, recv_sems, local_sems, gather=False):
    x, y, c = lax.axis_index("x"), lax.axis_index("y"), lax.axis_index("c")
    here = 2 * x + y
    copies = []
    for a, (src_ref, dst_ref) in enumerate(zip(src_refs, dst_refs)):
        landing = dst_ref.at[here, c] if gather else dst_ref.at[here]
        copies.append(pltpu.make_async_copy(src_ref if gather else src_ref.at[here], landing, local_sems.at[a]))
        for rel in range(1, N_DEV // 2):
            px = 1 - x if rel & 2 else x
            py = 1 - y if rel & 1 else y
            k = a * _CHIP_PEERS + rel - 1
            copies.append(pltpu.make_async_remote_copy(
                src_ref=src_ref if gather else src_ref.at[2 * px + py], dst_ref=landing,
                send_sem=send_sems.at[k], recv_sem=recv_sems.at[k],
                device_id=(px, py, c), device_id_type=pl.DeviceIdType.MESH))
    return copies


def _chip_sems(n):
    return [pltpu.SemaphoreType.DMA((n * _CHIP_PEERS,)), pltpu.SemaphoreType.DMA((n * _CHIP_PEERS,)),
            pltpu.SemaphoreType.DMA((n,))]


def _riding(riding, gather):
    if riding is None:
        return [], [], [], []
    shapes = [jax.ShapeDtypeStruct(((N_DEV // 2, 2) + r.shape) if gather else r.shape, r.dtype) for r in riding]
    return list(riding), [pl.BlockSpec(memory_space=pl.ANY)] * len(riding), shapes, _chip_sems(len(riding))


def _ride(first, last, srcs, dsts, sems, gather):
    @pl.when(first)
    def _():
        for cp in _chip_copies(srcs, dsts, *sems, gather=gather):
            cp.start()

    def finish():
        @pl.when(last)
        def _():
            for cp in _chip_copies(srcs, dsts, *sems, gather=gather):
                cp.wait()

    return finish


def _gdn_chunk_bwd(qkv, ab, a_log, dt_bias, states, tinvs, do, *, name, riding=None):
    S = qkv.shape[0]
    H, C, DK = GDN_HEADS, GDN_CHUNK, GDN_DK
    RB = min(_GDN_ROWS, S)
    NCB, NB, NC = RB // C, S // RB, S // C
    heads = range(H)

    def body(qkv_ref, ab_ref, alog_ref, dtb_ref, st_ref, t_ref, do_ref, *rest):
        n_ride = 0 if riding is None else len(riding)
        ride_srcs, rest = rest[:n_ride], rest[n_ride:]
        (dqkv_ref, dab_ref, dalog_ref, ddtb_ref), rest = rest[:4], rest[4:]
        ride_dsts, rest = rest[:n_ride], rest[n_ride:]
        dstate, w_s, vn_s, qe_s, kf_s, att_s, dvn_s, dkf_s, *ride_sems = rest
        nb = pl.program_id(0)
        if riding is not None:
            finish_ride = _ride(nb == 0, nb == NB - 1, ride_srcs, ride_dsts, ride_sems, False)

        @pl.when(nb == 0)
        def _():
            dstate[...] = jnp.zeros_like(dstate)
            dalog_ref[...] = jnp.zeros_like(dalog_ref)
            ddtb_ref[...] = jnp.zeros_like(ddtb_ref)

        ri, ci = _chunk_iotas()
        neg_a = [-jnp.exp(alog_ref[h]) for h in heads]

        def local(c):
            rows = pl.ds(c * C, C)
            a_pre = [ab_ref[h, c] + dtb_ref[h] for h in heads]
            g_row = [neg_a[h] * _softplus(a_pre[h]) for h in heads]
            beta_row = [_sigmoid(ab_ref[H + h, c]) for h in heads]
            L = _gdn_local_batch([qkv_ref[rows, h * _HM:(h + 1) * _HM] for h in heads], g_row, beta_row, ri, ci)
            return L, a_pre, g_row, beta_row

        e_last = [None] * NCB
        for c in range(NCB):
            L, _, _, _ = local(c)
            kbe = [L["kb"][h] * L["e_col"][h] for h in heads]
            u = [_dot(t_ref[h, c], L["vb"][h], exact=True) for h in heads]
            w = [_dot(t_ref[h, c], kbe[h], exact=True) for h in heads]
            vn = [u[h] - _dot(w[h], st_ref[h, c]) for h in heads]
            for h in heads:
                w_s[c, h] = w[h].astype(BF16)
                vn_s[c, h] = vn[h].astype(BF16)
                qe_s[c, h] = (L["q"][h] * L["e_col"][h]).astype(BF16)
                kf_s[c, h] = (L["k"][h] * L["f_col"][h]).astype(BF16)
                att_s[c, h] = L["att"][h].astype(BF16)
            e_last[c] = L["e_last"]

        dst = [dstate[h] for h in heads]
        de_last = [None] * NCB
        for c in reversed(range(NCB)):
            rows = pl.ds(c * C, C)
            dob = [do_ref[rows, h * DK:(h + 1) * DK].astype(BF16) for h in heads]
            dstb = [dst[h].astype(BF16) for h in heads]
            dvn = [_dot(att_s[c, h], dob[h], "tn") + _dot(kf_s[c, h], dstb[h]) for h in heads]
            dkf = [_dot(vn_s[c, h], dstb[h], "nt") for h in heads]
            de_last[c] = [jnp.sum(jnp.sum(dst[h] * st_ref[h, c], axis=1, keepdims=True), axis=0, keepdims=True)
                          for h in heads]
            new = [dst[h] * e_last[c][h] + _dot(qe_s[c, h], dob[h], "tn")
                   - _dot(w_s[c, h], dvn[h].astype(BF16), "tn") for h in heads]
            for h in heads:
                dvn_s[c, h] = dvn[h]
                dkf_s[c, h] = dkf[h]
            dst = new
        for h in heads:
            dstate[h] = dst[h]

        for c in range(NCB):
            rows = pl.ds(c * C, C)
            L, a_pre, g_row, beta_row = local(c)
            q, k, v, kb, vb = L["q"], L["k"], L["v"], L["kb"], L["vb"]
            e_col, f_col, decay, beta_col = L["e_col"], L["f_col"], L["decay"], L["beta_col"]
            eye, strict, tril = L["eye"], L["strict"], L["tril"]
            tinv = [t_ref[h, c] for h in heads]
            stb = [st_ref[h, c].astype(BF16) for h in heads]
            dov = [do_ref[rows, h * DK:(h + 1) * DK] for h in heads]
            dvn = [dvn_s[c, h] for h in heads]
            dkf = [dkf_s[c, h] for h in heads]
            kbe = [kb[h] * e_col[h] for h in heads]
            datt = [jnp.where(tril, _dot(dov[h], vn_s[c, h], "nt"), 0.0) for h in heads]
            dqe = [_dot(dov[h], stb[h], "nt") for h in heads]
            dw = [-_dot(dvn[h], stb[h], "nt") for h in heads]
            dt = [_dot(dvn[h], vb[h], "nt") + _dot(dw[h], kbe[h], "nt") for h in heads]
            dvb = [_dot(tinv[h], dvn[h], "tn", exact=True) for h in heads]
            dkbe = [_dot(tinv[h], dw[h], "tn", exact=True) for h in heads]
            tdt = [_dot(tinv[h], dt[h], "tn", exact=True) for h in heads]
            dlow = [-jnp.where(strict, _dot(tdt[h], tinv[h], "nt", exact=True), 0.0) for h in heads]
            dkk = [dlow[h] * decay[h] for h in heads]
            dqk = [datt[h] * decay[h] for h in heads]
            dkb = [_dot(dkk[h], k[h]) + dkbe[h] * e_col[h] for h in heads]
            dk = [_dot(dkk[h], kb[h], "tn") + _dot(dqk[h], q[h], "tn") + dkf[h] * f_col[h] + dkb[h] * beta_col[h]
                  for h in heads]
            dq = [_dot(dqk[h], k[h]) + dqe[h] * e_col[h] for h in heads]
            for h in heads:
                dqkv_ref[rows, h * _HM:h * _HM + DK] = dq[h]
                dqkv_ref[rows, h * _HM + DK:h * _HM + 2 * DK] = dk[h]
                dqkv_ref[rows, h * _HM + 2 * DK:(h + 1) * _HM] = dvb[h] * beta_col[h]

            dbeta_col = [jnp.sum(k[h] * dkb[h] + v[h] * dvb[h], axis=1, keepdims=True) for h in heads]
            pmat = [dlow[h] * L["low"][h] + datt[h] * L["att"][h] for h in heads]
            df_col = [jnp.sum(k[h] * dkf[h], axis=1, keepdims=True) * f_col[h] for h in heads]
            dgc_col = [jnp.sum(pmat[h], axis=1, keepdims=True)
                       + jnp.sum(q[h] * dqe[h] + kb[h] * dkbe[h], axis=1, keepdims=True) * e_col[h] - df_col[h]
                       for h in heads]
            dgc_row = [_to_row(dgc_col[h], eye) - jnp.sum(pmat[h], axis=0, keepdims=True) for h in heads]
            dg_last = [jnp.sum(df_col[h], axis=0, keepdims=True) + de_last[c][h] * L["e_last"][h] for h in heads]
            dgc_c = [_to_col(dgc_row[h], eye) for h in heads]
            dg_row = [jnp.sum(jnp.where(ri >= ci, dgc_c[h], 0.0), axis=0, keepdims=True) + dg_last[h] for h in heads]
            dbeta_row = [_to_row(dbeta_col[h], eye) for h in heads]
            for h in heads:
                da_row = dg_row[h] * neg_a[h] * _sigmoid(a_pre[h])
                dab_ref[h, c] = da_row
                dab_ref[H + h, c] = dbeta_row[h] * beta_row[h] * (1.0 - beta_row[h])
                dalog_ref[h] += jnp.sum(dg_row[h] * g_row[h], axis=1, keepdims=True)
                ddtb_ref[h] += jnp.sum(da_row, axis=1, keepdims=True)

        if riding is not None:
            finish_ride()

    rev = lambda n: NB - 1 - n
    vec = pl.BlockSpec((H, 1, 1), lambda n: (0, 0, 0))
    gates = pl.BlockSpec((2 * H, NCB, 1, C), lambda n: (0, rev(n), 0, 0))
    wide = pl.BlockSpec((RB, H * _HM), lambda n: (rev(n), 0))
    item = lambda dt: pltpu.VMEM((NCB, H, C, DK), dt)
    ride_args, ride_specs, ride_out, ride_scratch = _riding(riding, False)
    return pl.pallas_call(
        body, name=name, grid=(NB,),
        in_specs=[wide, gates, vec, vec,
                  pl.BlockSpec((H, NCB, DK, DK), lambda n: (0, rev(n), 0, 0)),
                  pl.BlockSpec((H, NCB, C, C), lambda n: (0, rev(n), 0, 0)),
                  pl.BlockSpec((RB, H * DK), lambda n: (rev(n), 0))] + ride_specs,
        out_specs=[wide, gates, vec, vec] + ride_specs,
        out_shape=[jax.ShapeDtypeStruct((S, H * _HM), F32),
                   jax.ShapeDtypeStruct((2 * H, NC, 1, C), F32),
                   jax.ShapeDtypeStruct((H, 1, 1), F32),
                   jax.ShapeDtypeStruct((H, 1, 1), F32)] + ride_out,
        scratch_shapes=[pltpu.VMEM((H, DK, DK), F32), item(BF16), item(BF16), item(BF16), item(BF16),
                        pltpu.VMEM((NCB, H, C, C), BF16), item(F32), item(F32)] + ride_scratch,
        compiler_params=_params("arbitrary"),
    )(qkv, ab, a_log, dt_bias, states, tinvs, do, *ride_args)


def _gdn_outnorm_fwd(o, z, gain, *, name):
    S, HV = o.shape
    RB = min(256, S)

    def body(o_ref, z_ref, g_ref, y_ref):
        for h in range(HV // GDN_DK):
            cols = slice(h * GDN_DK, (h + 1) * GDN_DK)
            ov = o_ref[:, cols]
            r = lax.rsqrt(jnp.mean(ov * ov, axis=-1, keepdims=True) + RMS_EPS)
            y_ref[:, cols] = (ov * r * g_ref[...] * _silu(z_ref[:, cols].astype(F32))).astype(BF16)

    blk = pl.BlockSpec((RB, HV), lambda i: (i, 0))
    return pl.pallas_call(
        body, name=name, grid=(S // RB,),
        in_specs=[blk, blk, pl.BlockSpec((1, GDN_DK), lambda i: (0, 0))], out_specs=blk,
        out_shape=jax.ShapeDtypeStruct((S, HV), BF16), compiler_params=_params("parallel"),
    )(o, z, gain)


def _gdn_outnorm_bwd(dy, o, z, gain, *, name):
    S, HV = o.shape
    RB = min(256, S)

    def body(dy_ref, o_ref, z_ref, g_ref, do_ref, dz_ref, dg_ref):
        part = None
        for h in range(HV // GDN_DK):
            cols = slice(h * GDN_DK, (h + 1) * GDN_DK)
            ov = o_ref[:, cols]
            zv = z_ref[:, cols].astype(F32)
            dyv = dy_ref[:, cols].astype(F32)
            r = lax.rsqrt(jnp.mean(ov * ov, axis=-1, keepdims=True) + RMS_EPS)
            n = ov * r
            sg = _sigmoid(zv)
            dng = dyv * (zv * sg)
            dn = dng * g_ref[...]
            do_ref[:, cols] = r * (dn - n * jnp.mean(dn * n, axis=-1, keepdims=True))
            dz_ref[:, cols] = (dyv * (n * g_ref[...]) * (sg * (1.0 + zv * (1.0 - sg)))).astype(BF16)
            p = jnp.sum(dng * n, axis=0, keepdims=True)
            part = p if part is None else part + p

        @pl.when(pl.program_id(0) == 0)
        def _():
            dg_ref[...] = part

        @pl.when(pl.program_id(0) > 0)
        def _():
            dg_ref[...] += part

    blk = pl.BlockSpec((RB, HV), lambda i: (i, 0))
    vec = pl.BlockSpec((1, GDN_DK), lambda i: (0, 0))
    return pl.pallas_call(
        body, name=name, grid=(S // RB,),
        in_specs=[blk, blk, blk, vec], out_specs=[blk, blk, vec],
        out_shape=[jax.ShapeDtypeStruct((S, HV), F32), jax.ShapeDtypeStruct((S, HV), BF16),
                   jax.ShapeDtypeStruct((1, GDN_DK), F32)],
        compiler_params=_params("arbitrary"),
    )(dy, o, z, gain)


def _head_mask():
    return lax.broadcasted_iota(jnp.int32, (DSW_BLK, LANES), 1) < DSW_DH


def _per_head_sum(t, first):
    s0 = jnp.sum(jnp.where(first, t, 0.0), axis=-1, keepdims=True)
    s1 = jnp.sum(jnp.where(first, 0.0, t), axis=-1, keepdims=True)
    return jnp.where(first, s0, s1)


def _rms2(x, gain, first):
    r = lax.rsqrt(_per_head_sum(x * x, first) * (1.0 / DSW_DH) + RMS_EPS)
    xh = x * r
    return xh, r, xh * gain


def _rms2_bwd(dy, xh, r, gain, first):
    dxh = dy * gain
    return r * (dxh - xh * (_per_head_sum(dxh * xh, first) * (1.0 / DSW_DH)))


def _split_heads(x, first):
    return [jnp.where(first, x, 0.0).astype(BF16), jnp.where(first, 0.0, x).astype(BF16)]


_HP = LANES // DSW_DH
_DSW_W = DSW_HEADS * DSW_DH
_DSW_ROWS = 1024
_DSW_BATCH = 8


def _dsw_geometry(S, g):
    d = DSW_GROUPS[g][1]
    slab = DSW_BLK * d
    tb = max(1, min(_DSW_ROWS, S) // slab)
    return d, slab, tb, S // (tb * slab)


def _block_rows(t, r, slab, d):
    return pl.ds(t * slab + r, DSW_BLK) if d == 1 else pl.ds(t * slab + r, DSW_BLK, stride=d)


def _dsw_attn_fwd(q, k, v, bias, q_gain, k_gain, prev_out, *, g, name):
    S, WT = q.shape
    B = DSW_BLK
    d, slab, tb, n_tiles = _dsw_geometry(S, g)
    rt = tb * slab
    cb = g * (_DSW_W // LANES)
    batch_res = max(1, _DSW_BATCH // tb)

    def body(q_ref, kp_ref, kc_ref, vp_ref, vc_ref, bias_ref, qg_ref, kg_ref, *rest):
        o_ref, lse_ref = rest[-2:]
        i = pl.program_id(1)
        qg, kg = qg_ref[...] * DSW_DH ** -0.5, kg_ref[...]
        col = lax.broadcasted_iota(jnp.int32, (B, 2 * B), 1)
        first = _head_mask()
        heads = range(_HP)
        for r0 in range(0, d, batch_res):
            res = range(r0, min(d, r0 + batch_res))
            k_raw = {(r, -1): kp_ref[_block_rows(0, r, slab, d), :] for r in res}
            v_raw = {(r, -1): vp_ref[_block_rows(0, r, slab, d), :] for r in res}
            q_raw = {}
            for r in res:
                for t in range(tb):
                    rows = _block_rows(t, r, slab, d)
                    q_raw[r, t], k_raw[r, t], v_raw[r, t] = q_ref[rows, :], kc_ref[rows, :], vc_ref[rows, :]
            kn = {key: _rms2(x, kg, first)[2].astype(BF16) for key, x in k_raw.items()}
            vb = {key: x.astype(BF16) for key, x in v_raw.items()}
            qn = {key: _split_heads(_rms2(x, qg, first)[2], first) for key, x in q_raw.items()}
            items = [(r, t, h) for r in res for t in range(tb) for h in heads]
            s = {}
            for r, t, h in items:
                sv = _dot(qn[r, t][h], jnp.concatenate([kn[r, t - 1], kn[r, t]], axis=0), "nt") + bias_ref[h]
                s[r, t, h] = jnp.where((i == 0) & (col < B), NEG_BIG, sv) if t == 0 else sv
            m = {it: jnp.max(s[it], axis=-1, keepdims=True) for it in items}
            p = {it: jnp.exp(s[it] - m[it]) for it in items}
            l = {it: jnp.sum(p[it], axis=-1, keepdims=True) for it in items}
            o = {(r, t, h): _dot(p[r, t, h], jnp.concatenate([vb[r, t - 1], vb[r, t]], axis=0)) for r, t, h in items}
            for r in res:
                for t in range(tb):
                    rows = _block_rows(t, r, slab, d)
                    o_ref[rows, :] = jnp.where(first, o[r, t, 0] / l[r, t, 0], o[r, t, 1] / l[r, t, 1])
                    lse_ref[rows, :] = jnp.where(first, m[r, t, 0] + jnp.log(l[r, t, 0]),
                                                 m[r, t, 1] + jnp.log(l[r, t, 1]))

    cur = pl.BlockSpec((rt, LANES), lambda hp, i: (i, cb + hp))
    prev = pl.BlockSpec((slab, LANES), lambda hp, i: (jnp.maximum(i * tb - 1, 0), cb + hp))
    vec = pl.BlockSpec((1, LANES), lambda hp, i: (0, 0))
    shp = jax.ShapeDtypeStruct((S, WT), F32)
    carried = [] if prev_out is None else list(prev_out)
    n_in = 8
    return pl.pallas_call(
        body, name=name, grid=(_DSW_W // LANES, n_tiles),
        in_specs=[cur, prev, cur, prev, cur, pl.BlockSpec((_HP, B, 2 * B), lambda hp, i: (hp, 0, 0)), vec, vec]
                 + [pl.BlockSpec(memory_space=pl.ANY)] * len(carried),
        out_specs=[cur, cur], out_shape=[shp, shp],
        input_output_aliases={n_in + j: j for j in range(len(carried))},
        compiler_params=_params("parallel", "parallel"),
    )(q, k, k, v, v, bias, jnp.tile(q_gain, (1, _HP)), jnp.tile(k_gain, (1, _HP)), *carried)


def _dsw_merge(o_g, lse_g, *, name):
    S = o_g.shape[0]
    W, G = _DSW_W, len(DSW_GROUPS)
    tr = min(512, S)

    def body(o_ref, l_ref, out_ref, lse_ref):
        ls = [l_ref[:, g * W:(g + 1) * W] for g in range(G)]
        m = ls[0]
        for g in range(1, G):
            m = jnp.maximum(m, ls[g])
        den = jnp.zeros_like(m)
        acc = jnp.zeros_like(m)
        for g in range(G):
            wg = jnp.exp(ls[g] - m)
            den = den + wg
            acc = acc + wg * o_ref[:, g * W:(g + 1) * W]
        out_ref[...] = acc / den
        lse_ref[...] = m + jnp.log(den)

    wide = pl.BlockSpec((tr, G * W), lambda i: (i, 0))
    blk = pl.BlockSpec((tr, W), lambda i: (i, 0))
    shp = jax.ShapeDtypeStruct((S, W), F32)
    return pl.pallas_call(
        body, name=name, grid=(S // tr,), in_specs=[wide, wide], out_specs=[blk, blk],
        out_shape=[shp, shp], compiler_params=_params("parallel"),
    )(o_g, lse_g)


def _dsw_attn_bwd(q, k, v, o, lse, do, bias, q_gain, k_gain, prev_out, *, g, name):
    S, WT = q.shape
    B = DSW_BLK
    d, slab, tb, n_tiles = _dsw_geometry(S, g)
    rt = tb * slab
    cb = g * (_DSW_W // LANES)
    n_slabs = S // slab
    scale = DSW_DH ** -0.5
    batch_res = max(1, _DSW_BATCH // tb)

    def body(q_ref, qx_ref, kp_ref, kc_ref, vp_ref, vc_ref, o_ref, ox_ref, l_ref, lx_ref, do_ref, dox_ref,
             bias_ref, qg_ref, kg_ref, *rest):
        dq_ref, dk_ref, dv_ref, db_ref, dqg_ref, dkg_ref = rest[-6:]
        hp, i = pl.program_id(0), pl.program_id(1)
        qg, kg = qg_ref[...] * scale, kg_ref[...]
        col = lax.broadcasted_iota(jnp.int32, (B, 2 * B), 1)
        has_next = i < n_tiles - 1

        @pl.when(i == 0)
        def _():
            db_ref[...] = jnp.zeros_like(db_ref)

        dqg_acc = jnp.zeros((1, LANES), F32)
        dkg_acc = jnp.zeros((1, LANES), F32)
        first = _head_mask()
        heads = range(_HP)
        for r0 in range(0, d, batch_res):
            res = range(r0, min(d, r0 + batch_res))
            q_raw, k_raw, v_raw, o_raw, l_raw, do_raw = {}, {}, {}, {}, {}, {}
            for r in res:
                first_rows = _block_rows(0, r, slab, d)
                k_raw[r, -1], v_raw[r, -1] = kp_ref[first_rows, :], vp_ref[first_rows, :]
                for t in range(tb):
                    rows = _block_rows(t, r, slab, d)
                    q_raw[r, t], o_raw[r, t], l_raw[r, t], do_raw[r, t] = (
                        q_ref[rows, :], o_ref[rows, :], l_ref[rows, :], do_ref[rows, :])
                    k_raw[r, t], v_raw[r, t] = kc_ref[rows, :], vc_ref[rows, :]
                q_raw[r, tb], o_raw[r, tb], l_raw[r, tb], do_raw[r, tb] = (
                    qx_ref[first_rows, :], ox_ref[first_rows, :], lx_ref[first_rows, :], dox_ref[first_rows, :])
            kk = {key: _rms2(x, kg, first) for key, x in k_raw.items()}
            qq = {key: _rms2(x, qg, first) for key, x in q_raw.items()}
            knb = {key: kk[key][2].astype(BF16) for key in kk}
            qnb = {key: _split_heads(qq[key][2], first) for key in qq}
            vb = {key: x.astype(BF16) for key, x in v_raw.items()}
            dob = {key: _split_heads(x, first) for key, x in do_raw.items()}
            delta = {key: _per_head_sum(do_raw[key] * o_raw[key], first) for key in q_raw}
            pick = lambda x, h: x[:, h * DSW_DH:h * DSW_DH + 1]
            full = [(r, t, h) for r in res for t in range(tb) for h in heads]
            half = [(r, tb, h) for r in res for h in heads]
            s = {}
            for r, t, h in full:
                sv = _dot(qnb[r, t][h], jnp.concatenate([knb[r, t - 1], knb[r, t]], axis=0), "nt") + bias_ref[h]
                s[r, t, h] = jnp.where((i == 0) & (col < B), NEG_BIG, sv) if t == 0 else sv
            for r, t, h in half:
                s[r, t, h] = _dot(qnb[r, t][h], knb[r, t - 1], "nt") + bias_ref[h, :, 0:B]
            p = {(r, t, h): jnp.exp(s[r, t, h] - pick(l_raw[r, t], h)) for r, t, h in full}
            for r, t, h in half:
                p[r, t, h] = jnp.where(has_next, jnp.exp(s[r, t, h] - pick(l_raw[r, t], h)), 0.0)
            dp = {(r, t, h): _dot(dob[r, t][h], jnp.concatenate([vb[r, t - 1], vb[r, t]], axis=0), "nt")
                  for r, t, h in full}
            for r, t, h in half:
                dp[r, t, h] = _dot(dob[r, t][h], vb[r, t - 1], "nt")
            ds = {(r, t, h): p[r, t, h] * (dp[r, t, h] - pick(delta[r, t], h)) for r, t, h in full + half}
            pb = {it: p[it].astype(BF16) for it in ds}
            dsb = {it: ds[it].astype(BF16) for it in ds}
            for h in heads:
                tot = None
                for r in res:
                    for t in range(tb):
                        tot = ds[r, t, h] if tot is None else tot + ds[r, t, h]
                db_ref[h] += tot
            blocks = [(r, t) for r in res for t in range(tb)]
            keys2 = {(r, t): jnp.concatenate([knb[r, t - 1], knb[r, t]], axis=0) for r, t in blocks}
            dqn = {(r, t): jnp.where(first, _dot(dsb[r, t, 0], keys2[r, t]), _dot(dsb[r, t, 1], keys2[r, t]))
                   for r, t in blocks}
            prev_half = lambda x, r, t, h: x[r, t, h][:, :B] if t < tb else x[r, t, h]
            dkn = {(r, t): sum(_dot(dsb[r, t, h][:, B:], qnb[r, t][h], "tn")
                               + _dot(prev_half(dsb, r, t + 1, h), qnb[r, t + 1][h], "tn") for h in heads)
                   for r, t in blocks}
            dvv = {(r, t): sum(_dot(pb[r, t, h][:, B:], dob[r, t][h], "tn")
                               + _dot(prev_half(pb, r, t + 1, h), dob[r, t + 1][h], "tn") for h in heads)
                   for r, t in blocks}
            for r, t in blocks:
                dqg_acc = dqg_acc + jnp.sum(dqn[r, t] * qq[r, t][0], axis=0, keepdims=True)
                dkg_acc = dkg_acc + jnp.sum(dkn[r, t] * kk[r, t][0], axis=0, keepdims=True)
            for r, t in blocks:
                rows = _block_rows(t, r, slab, d)
                dq_ref[rows, :] = _rms2_bwd(dqn[r, t], qq[r, t][0], qq[r, t][1], qg, first)
                dk_ref[rows, :] = _rms2_bwd(dkn[r, t], kk[r, t][0], kk[r, t][1], kg, first)
                dv_ref[rows, :] = dvv[r, t]

        start = (hp == 0) & (i == 0)
        fold = lambda a: a[:, :DSW_DH] + a[:, DSW_DH:]

        @pl.when(start)
        def _():
            dqg_ref[...] = fold(dqg_acc) * scale
            dkg_ref[...] = fold(dkg_acc)

        @pl.when(jnp.logical_not(start))
        def _():
            dqg_ref[...] += fold(dqg_acc) * scale
            dkg_ref[...] += fold(dkg_acc)

    def spec(rows, pick, base):
        return pl.BlockSpec((rows, LANES), lambda hp, i: (pick(i), base + hp))

    same = lambda i: i
    before = lambda i: jnp.maximum(i * tb - 1, 0)
    after = lambda i: jnp.minimum((i + 1) * tb, n_slabs - 1)
    cur, cur1 = spec(rt, same, cb), spec(rt, same, 0)
    vec = pl.BlockSpec((1, DSW_DH), lambda hp, i: (0, 0))
    vec2 = pl.BlockSpec((1, LANES), lambda hp, i: (0, 0))
    bspec = pl.BlockSpec((_HP, B, 2 * B), lambda hp, i: (hp, 0, 0))
    shp = jax.ShapeDtypeStruct((S, WT), F32)
    vshp = jax.ShapeDtypeStruct((1, DSW_DH), F32)
    carried = [] if prev_out is None else list(prev_out)
    n_in = 15
    return pl.pallas_call(
        body, name=name, grid=(_DSW_W // LANES, n_tiles),
        in_specs=[cur, spec(slab, after, cb), spec(slab, before, cb), cur, spec(slab, before, cb), cur,
                  cur1, spec(slab, after, 0), cur1, spec(slab, after, 0), cur1, spec(slab, after, 0),
                  bspec, vec2, vec2] + [pl.BlockSpec(memory_space=pl.ANY)] * len(carried),
        out_specs=[cur, cur, cur, bspec, vec, vec],
        out_shape=[shp, shp, shp, jax.ShapeDtypeStruct(bias.shape, F32), vshp, vshp],
        input_output_aliases={n_in + j: j for j in range(len(carried))},
        compiler_params=_params("arbitrary", "arbitrary"),
    )(q, q, k, k, v, v, o, o, lse, lse, do, do, bias, jnp.tile(q_gain, (1, _HP)), jnp.tile(k_gain, (1, _HP)),
      *carried)


def _t5_bucket(dist):
    max_exact = REL_BUCKETS // 2
    scaled = jnp.log(jnp.maximum(dist, 1).astype(F32) / max_exact) / math.log(REL_MAX_DIST / max_exact)
    large = jnp.minimum(max_exact + (scaled * (REL_BUCKETS - max_exact)).astype(jnp.int32), REL_BUCKETS - 1)
    return jnp.where(dist < max_exact, dist, large)


def _dsw_band():
    dist = (jnp.arange(DSW_BLK)[:, None] + DSW_BLK) - jnp.arange(2 * DSW_BLK)[None, :]
    return dist, (dist >= 0) & (dist <= DSW_BLK)


def _dsw_bias(rel_bias):
    dist, band = _dsw_band()
    out = []
    for g, (_, d) in enumerate(DSW_GROUPS):
        hot = jax.nn.one_hot(_t5_bucket(jnp.maximum(dist, 0) * d), REL_BUCKETS, dtype=F32)
        tab = jnp.einsum("qkb,bh->hqk", hot, rel_bias[:, g * DSW_HEADS:(g + 1) * DSW_HEADS],
                         precision=lax.Precision.HIGHEST)
        out.append(jnp.where(band[None], tab, NEG_BIG))
    return jnp.stack(out)


def _dsw_bucket_onehot():
    dist, band = _dsw_band()
    out = []
    for _, d in DSW_GROUPS:
        hot = jax.nn.one_hot(_t5_bucket(jnp.maximum(dist, 0) * d), LANES, dtype=BF16)
        out.append(jnp.where(band[..., None], hot, 0).reshape(-1, LANES))
    return jnp.stack(out)


def _exchange(send, *, gather, name):
    R, C = send.shape[-2:]

    def body(src_ref, dst_ref, send_sems, recv_sems, local_sem):
        x, y, c = lax.axis_index("x"), lax.axis_index("y"), lax.axis_index("c")
        me = 4 * x + 2 * y + c
        mine = pltpu.make_async_copy(src_ref if gather else src_ref.at[me], dst_ref.at[me], local_sem)
        mine.start()
        copies = []
        for rel in range(1, N_DEV):
            px = 1 - x if rel & 4 else x
            py = 1 - y if rel & 2 else y
            pc = 1 - c if rel & 1 else c
            peer = 4 * px + 2 * py + pc
            cp = pltpu.make_async_remote_copy(
                src_ref=src_ref if gather else src_ref.at[peer], dst_ref=dst_ref.at[me],
                send_sem=send_sems.at[rel - 1], recv_sem=recv_sems.at[rel - 1],
                device_id=(px, py, pc), device_id_type=pl.DeviceIdType.MESH)
            cp.start()
            copies.append(cp)
        for cp in copies:
            cp.wait()
        mine.wait()

    return pl.pallas_call(
        body, name=name,
        in_specs=[pl.BlockSpec(memory_space=pl.ANY)], out_specs=pl.BlockSpec(memory_space=pl.ANY),
        out_shape=jax.ShapeDtypeStruct((N_DEV, R, C), send.dtype),
        scratch_shapes=[pltpu.SemaphoreType.DMA((N_DEV - 1,)), pltpu.SemaphoreType.DMA((N_DEV - 1,)),
                        pltpu.SemaphoreType.DMA(())],
    )(send)


def _gather_two_level(send, *, name):
    R, C = send.shape

    def body(src_ref, dst_ref, send_sems, recv_sems, local_sem):
        x, y, c = lax.axis_index("x"), lax.axis_index("y"), lax.axis_index("c")
        me, sibling = (x, y, c), (x, y, 1 - c)
        chips = [(1 - x, y), (x, 1 - y), (1 - x, 1 - y)]

        def slot(px, py, pc):
            return dst_ref.at[4 * px + 2 * py + pc]

        def copy(k, block, to, src=None):
            return pltpu.make_async_remote_copy(
                src_ref=slot(*block) if src is None else src, dst_ref=slot(*block),
                send_sem=send_sems.at[k], recv_sem=recv_sems.at[k],
                device_id=to, device_id_type=pl.DeviceIdType.MESH)

        mine = pltpu.make_async_copy(src_ref, slot(*me), local_sem)
        mine.start()
        first = [copy(0, me, sibling, src=src_ref)]
        first += [copy(1 + j, me, (*chip, c), src=src_ref) for j, chip in enumerate(chips)]
        for cp in first:
            cp.start()
        passed = [copy(4 + j, (*chip, c), sibling) for j, chip in enumerate(chips)]
        for j, chip in enumerate(chips):
            copy(1 + j, (*chip, c), me).wait_recv()
            passed[j].start()
        copy(0, sibling, me).wait_recv()
        for j, chip in enumerate(chips):
            copy(4 + j, (*chip, 1 - c), me).wait_recv()
        for cp in first + passed:
            cp.wait_send()
        mine.wait()

    return pl.pallas_call(
        body, name=name,
        in_specs=[pl.BlockSpec(memory_space=pl.ANY)], out_specs=pl.BlockSpec(memory_space=pl.ANY),
        out_shape=jax.ShapeDtypeStruct((N_DEV, R, C), send.dtype),
        scratch_shapes=[pltpu.SemaphoreType.DMA((N_DEV - 1,)), pltpu.SemaphoreType.DMA((N_DEV - 1,)),
                        pltpu.SemaphoreType.DMA(())],
    )(send)


_ANY = pl.BlockSpec(memory_space=pl.ANY)


def _swap_with_sibling(sends, *, name):
    n = len(sends)

    def body(*refs):
        x, y, c = lax.axis_index("x"), lax.axis_index("y"), lax.axis_index("c")
        send_sems, recv_sems = refs[2 * n:]
        copies = [pltpu.make_async_remote_copy(
            src_ref=refs[a], dst_ref=refs[n + a], send_sem=send_sems.at[a], recv_sem=recv_sems.at[a],
            device_id=(x, y, 1 - c), device_id_type=pl.DeviceIdType.MESH) for a in range(n)]
        for cp in copies:
            cp.start()
        for cp in copies:
            cp.wait()

    return pl.pallas_call(
        body, name=name, in_specs=[_ANY] * n, out_specs=[_ANY] * n,
        out_shape=[jax.ShapeDtypeStruct(s.shape, s.dtype) for s in sends],
        scratch_shapes=[pltpu.SemaphoreType.DMA((n,)), pltpu.SemaphoreType.DMA((n,))],
    )(*sends)


def _fill_from_sibling(bufs, *, name):
    n, n_chips = len(bufs), bufs[0].shape[0]

    def body(*refs):
        x, y, c = lax.axis_index("x"), lax.axis_index("y"), lax.axis_index("c")
        send_sems, recv_sems = refs[2 * n:]
        copies = [pltpu.make_async_remote_copy(
            src_ref=refs[a].at[q, c], dst_ref=refs[n + a].at[q, c],
            send_sem=send_sems.at[a * n_chips + q], recv_sem=recv_sems.at[a * n_chips + q],
            device_id=(x, y, 1 - c), device_id_type=pl.DeviceIdType.MESH) for a in range(n) for q in range(n_chips)]
        for cp in copies:
            cp.start()
        for cp in copies:
            cp.wait()

    return pl.pallas_call(
        body, name=name, in_specs=[_ANY] * n, out_specs=[_ANY] * n,
        out_shape=[jax.ShapeDtypeStruct(b.shape, b.dtype) for b in bufs],
        input_output_aliases={a: a for a in range(n)},
        scratch_shapes=[pltpu.SemaphoreType.DMA((n * n_chips,)), pltpu.SemaphoreType.DMA((n * n_chips,))],
    )(*bufs)


def _exchange_chips(send, *, name):
    def body(src_ref, dst_ref, *sems):
        copies = _chip_copies([src_ref], [dst_ref], *sems)
        for cp in copies:
            cp.start()
        for cp in copies:
            cp.wait()

    return pl.pallas_call(
        body, name=name, in_specs=[_ANY], out_specs=_ANY,
        out_shape=jax.ShapeDtypeStruct(send.shape, send.dtype), scratch_shapes=_chip_sems(1),
    )(send)


def _add_pair(a, b, *, name):
    lead, (R, C) = a.shape[:-2], a.shape[-2:]
    tr = _tile(R, max(8, 1024 * LANES // C))

    def body(a_ref, b_ref, o_ref):
        o_ref[...] = (a_ref[...].astype(F32) + b_ref[...].astype(F32)).astype(o_ref.dtype)

    blk = pl.BlockSpec((None,) * len(lead) + (tr, C), lambda *idx: idx + (0,))
    return pl.pallas_call(
        body, name=name, grid=lead + (R // tr,), in_specs=[blk, blk], out_specs=blk,
        out_shape=jax.ShapeDtypeStruct(a.shape, a.dtype),
        compiler_params=_params(*(("parallel",) * (len(lead) + 1))),
    )(a, b)


_BIG = ("w_ffn_in", "w_ffn_out", "gdn_w_in", "gdn_conv", "gdn_w_out", "dsw_w_in", "dsw_w_out")
_LATE = ("gdn_w_in", "gdn_conv", "gdn_w_out")
_EARLY = tuple(n for n in _BIG if n not in _LATE)
_NATIVE = ("w_ffn_in", "w_ffn_out", "dsw_w_in")
_SHARD_AXIS = {"w_ffn_in": 2, "w_ffn_out": 1, "gdn_w_in": 2, "gdn_conv": 2, "gdn_w_out": 1, "dsw_w_in": 2,
               "dsw_w_out": 2}
_SMALL = ("b_ada", "norm_mix", "norm_ffn", "gdn_a_log", "gdn_dt_bias", "gdn_out_norm", "dsw_q_norm",
          "dsw_k_norm", "rel_bias")
_ROW_ALIGN = 16
_BIG_ALIGN = 1024


def _ceil_to(n, m):
    return -(-n // m) * m


def _seg_rows(shape):
    return _ceil_to(_ceil_to(int(np.prod(shape)), LANES) // LANES, _ROW_ALIGN)


def _pack(arrs, total_align):
    lead = arrs[0][1]
    segs = []
    for a, nlead in arrs:
        assert nlead == lead
        bshape = a.shape[:nlead]
        n = int(np.prod(a.shape[nlead:]))
        rows = _seg_rows(a.shape[nlead:])
        flat = a.reshape(bshape + (n,))
        flat = jnp.pad(flat, [(0, 0)] * nlead + [(0, rows * LANES - n)])
        segs.append(flat.reshape(bshape + (rows, LANES)))
    buf = jnp.concatenate(segs, axis=lead)
    total = _ceil_to(buf.shape[lead], total_align)
    return jnp.pad(buf, [(0, 0)] * lead + [(0, total - buf.shape[lead]), (0, 0)])


def _unpack(buf, shapes, nlead):
    out, off = [], 0
    for shp in shapes:
        n, rows = int(np.prod(shp)), _seg_rows(shp)
        seg = lax.slice_in_dim(buf, off, off + rows, axis=nlead)
        seg = seg.reshape(buf.shape[:nlead] + (rows * LANES,))[..., :n]
        out.append(seg.reshape(buf.shape[:nlead] + tuple(shp)))
        off += rows
    return out


def _to_natural(g, axis):
    n, L, r, c = g.shape
    if axis == 2:
        return jnp.transpose(g, (1, 2, 0, 3)).reshape(L, r, n * c)
    return jnp.transpose(g, (1, 0, 2, 3)).reshape(L, n * r, c)


def _to_blocked(w, axis):
    L, R, C = w.shape
    if axis == 2:
        return jnp.transpose(w.reshape(L, R, N_DEV, C // N_DEV), (2, 0, 1, 3))
    return jnp.transpose(w.reshape(L, N_DEV, R // N_DEV, C), (1, 0, 2, 3))


def _hm(a):
    lead = a.shape[:-1]
    return jnp.swapaxes(a.reshape(lead + (3, GDN_HEADS, GDN_DK)), -3, -2).reshape(lead + (3 * GDN_HEADS * GDN_DK,))


def _un_hm(a):
    lead = a.shape[:-1]
    return jnp.swapaxes(a.reshape(lead + (GDN_HEADS, 3, GDN_DK)), -3, -2).reshape(lead + (3 * GDN_HEADS * GDN_DK,))


_TILES = (1536, 1408, 1024, 768, 512, 384, 256, 128, 64, 32, 16, 8)


def _tile(n, cap):
    for t in _TILES:
        if t <= cap and n % t == 0:
            return t
    return n


def _mm_auto(a, b, mode, name, **kw):
    if mode == "tn":
        (K, M), N = a.shape, b.shape[1]
        tm, tn, tk = _tile(M, 1408), _tile(N, 1408), _tile(K, 1024)
    else:
        M, K = a.shape
        N = b.shape[1] if mode == "nn" else b.shape[0]
        tm, tn, tk = _tile(M, _MM_ROWS), _tile(N, 1536), _tile(K, 1408)
    return _mm(a, b, mode=mode, name=name, tm=tm, tn=tn, tk=tk, **kw)


def _row(v):
    return v.reshape(1, -1)


def _ffn_in_act(h, w_in, *, name):
    S, D = h.shape
    F = w_in.shape[1] // 2
    tm, tn = _tile(S, _MM_ROWS), _tile(F, 1408)
    nj = F // tn

    def body(h_ref, wg_ref, wu_ref, g_ref, u_ref, a_ref):
        hv = h_ref[...]
        gate = jnp.dot(hv, wg_ref[...], preferred_element_type=F32)
        up = jnp.dot(hv, wu_ref[...], preferred_element_type=F32)
        g_ref[...] = gate.astype(BF16)
        u_ref[...] = up.astype(BF16)
        a_ref[...] = (_silu(gate) * up).astype(BF16)

    out = pl.BlockSpec((tm, tn), lambda i, j: (i, j))
    shp = jax.ShapeDtypeStruct((S, F), BF16)
    return pl.pallas_call(
        body, name=name, grid=(S // tm, nj),
        in_specs=[pl.BlockSpec((tm, D), lambda i, j: (i, 0)), pl.BlockSpec((D, tn), lambda i, j: (0, j)),
                  pl.BlockSpec((D, tn), lambda i, j: (0, j + nj))],
        out_specs=[out, out, out], out_shape=[shp, shp, shp],
        compiler_params=_params("parallel", "parallel"),
    )(h, w_in, w_in)


def _ffn_out_dx_act(dy, w_out, gate_vec, pg, pu, *, name):
    S, D = dy.shape
    F = w_out.shape[0]
    tm, tn = _tile(S, _MM_ROWS), _tile(F, 1408)

    def body(dy_ref, w_ref, gv_ref, pg_ref, pu_ref, dg_ref, du_ref):
        dyg = (dy_ref[...] * gv_ref[...]).astype(BF16)
        da = lax.dot_general(dyg, w_ref[...], _DOT_DIMS["nt"], preferred_element_type=F32)
        gate = pg_ref[...].astype(F32)
        up = pu_ref[...].astype(F32)
        sg = _sigmoid(gate)
        dg_ref[...] = (da * up * (sg * (1.0 + gate * (1.0 - sg)))).astype(BF16)
        du_ref[...] = (da * (gate * sg)).astype(BF16)

    blk = pl.BlockSpec((tm, tn), lambda i, j: (i, j))
    shp = jax.ShapeDtypeStruct((S, F), BF16)
    return pl.pallas_call(
        body, name=name, grid=(S // tm, F // tn),
        in_specs=[pl.BlockSpec((tm, D), lambda i, j: (i, 0)), pl.BlockSpec((tn, D), lambda i, j: (j, 0)),
                  pl.BlockSpec((1, D), lambda i, j: (0, 0)), blk, blk],
        out_specs=[blk, blk], out_shape=[shp, shp],
        compiler_params=_params("parallel", "parallel"),
    )(dy, w_out, gate_vec, pg, pu)


def _ffn_fwd(x, mod, gain, w_in, w_out, tag):
    sh, sc, gate = mod
    h = _norm_mod_fwd(x, gain, sc, sh, name=f"ffn_norm_{tag}")
    pg, pu, a = _ffn_in_act(h, w_in, name=f"ffn_in_{tag}")
    y = _mm_auto(a, w_out, "nn", f"ffn_out_{tag}", out_scale=gate, resid=x)
    return y, (x, h, pg, pu, a)


def _ffn_bwd(dy, saved, mod, gain, w_in, w_out, tag):
    sh, sc, gate = mod
    x, h, pg, pu, a = saved
    F = pg.shape[1]
    gmat = _mm_auto(a, dy, "tn", f"ffn_out_g_{tag}")
    dw_out, dgate = _wout_grad(gmat, w_out, gate, name=f"ffn_out_dw_{tag}")
    dpg, dpu = _ffn_out_dx_act(dy, w_out, gate, pg, pu, name=f"ffn_out_dx_{tag}")
    dw_in = jnp.concatenate([_mm_auto(h, dpg, "tn", f"ffn_in_dw_gate_{tag}", out_dtype=BF16),
                             _mm_auto(h, dpu, "tn", f"ffn_in_dw_up_{tag}", out_dtype=BF16)], axis=1)
    tk = _tile(F, 1408)
    dh = _mm_sum_nt([(dpg, w_in, tk, 0), (dpu, w_in, tk, F)], name=f"ffn_in_dx_{tag}")
    dx, dsh, dsc, dgain = _norm_mod_bwd(dh, x, dy, gain, sc, name=f"ffn_norm_bwd_{tag}")
    return dx, dict(w_in=dw_in, w_out=dw_out, gain=dgain, mod=(dsh, dsc, dgate))


def _gdn_fwd(x, mod, gain, W, riding=None):
    sh, sc, gate = mod
    S = x.shape[0]
    h = _norm_mod_fwd(x, gain, sc, sh, name="gdn_norm")
    pq = _mm_auto(h, W["gdn_qkv"], "nn", "gdn_in_qkv", out_dtype=BF16)
    z = _mm_auto(h, W["gdn_z"], "nn", "gdn_in_z", out_dtype=BF16)
    ab = _mm_auto(h, W["gdn_ab"], "nn", "gdn_in_ab")
    qkvn = _gdn_prep_fwd(pq, W["gdn_conv"], name="gdn_prep")
    ab4 = jnp.transpose(ab[:, :2 * GDN_HEADS]).reshape(2 * GDN_HEADS, S // GDN_CHUNK, 1, GDN_CHUNK)
    o, states, tinvs, *rode = _gdn_chunk_fwd(qkvn, ab4, W["gdn_a_log"], W["gdn_dt_bias"], name="gdn_chunk",
                                             riding=riding)
    o2 = _gdn_outnorm_fwd(o, z, W["gdn_out_norm"], name="gdn_outnorm")
    y = _mm_auto(o2, W["gdn_out"], "nn", "gdn_out", out_scale=gate, resid=x)
    return y, (x, h, pq, z, qkvn, ab4, o, states, tinvs, o2), (tuple(rode) if rode else None)


def _gdn_bwd(dy, saved, mod, gain, W, riding=None):
    sh, sc, gate = mod
    x, h, pq, z, qkvn, ab4, o, states, tinvs, o2 = saved
    S = x.shape[0]
    gmat = _mm_auto(o2, dy, "tn", "gdn_out_g")
    dw_out, dgate = _wout_grad(gmat, W["gdn_out"], gate, name="gdn_out_dw")
    do2 = _mm_auto(dy, W["gdn_out"], "nt", "gdn_out_dx", a_scale=gate)
    do, dz, dout_norm = _gdn_outnorm_bwd(do2, o, z, W["gdn_out_norm"], name="gdn_outnorm_bwd")
    dqkvn, dab4, da_log, ddt_bias, *rode = _gdn_chunk_bwd(
        qkvn, ab4, W["gdn_a_log"], W["gdn_dt_bias"], states, tinvs, do, name="gdn_chunk_bwd", riding=riding)
    dc, dconv8 = _gdn_prep_bwd_pre(dqkvn, pq, W["gdn_conv"], name="gdn_prep_bwd")
    dpq = _gdn_conv_bwd_x(dc, W["gdn_conv"], name="gdn_conv_bwd")
    dab = jnp.transpose(dab4.reshape(2 * GDN_HEADS, S))
    dab = jnp.pad(dab, ((0, 0), (0, LANES - 2 * GDN_HEADS))).astype(BF16)
    dw_qkv = _mm_auto(h, dpq, "tn", "gdn_in_qkv_dw", out_dtype=BF16)
    dw_z = _mm_auto(h, dz, "tn", "gdn_in_z_dw", out_dtype=BF16)
    dw_ab = _mm_auto(h, dab, "tn", "gdn_in_ab_dw", out_dtype=BF16)
    dh = _mm_sum_nt([(dpq, W["gdn_qkv"], 1024, 0), (dz, W["gdn_z"], 1024, 0), (dab, W["gdn_ab"], LANES, 0)],
                    name="gdn_in_dx")
    dx, dsh, dsc, dgain = _norm_mod_bwd(dh, x, dy, gain, sc, name="gdn_norm_bwd")
    dw_in = jnp.concatenate([_un_hm(dw_qkv), dw_z, dw_ab[:, :2 * GDN_HEADS]], axis=1)
    return dx, dict(gdn_w_in=dw_in, gdn_conv=_un_hm(dconv8[:GDN_CONV]), gdn_w_out=dw_out, gdn_out_norm=dout_norm,
                    gdn_a_log=da_log.reshape(1, GDN_HEADS), gdn_dt_bias=ddt_bias.reshape(1, GDN_HEADS),
                    gain=dgain, mod=(dsh, dsc, dgate)), (tuple(rode) if rode else None)


def _dsw_fwd(x, mod, gain, W):
    sh, sc, gate = mod
    h = _norm_mod_fwd(x, gain, sc, sh, name="dsw_norm")
    q, k, v = (_mm_auto(h, W[n], "nn", f"dsw_in_{n[-1]}") for n in ("dsw_q", "dsw_k", "dsw_v"))
    outs = None
    for g in range(len(DSW_GROUPS)):
        outs = _dsw_attn_fwd(q, k, v, W["dsw_bias"][g], W["dsw_q_norm"], W["dsw_k_norm"], outs, g=g,
                             name=f"dsw_attn_{g}")
    o, lse = _dsw_merge(*outs, name="dsw_merge")
    y = _mm_auto(o, W["dsw_out"], "nn", "dsw_out", out_scale=gate, resid=x)
    return y, (x, h, q, k, v, o, lse)


def _dsw_bwd(dy, saved, mod, gain, W):
    sh, sc, gate = mod
    x, h, q, k, v, o, lse = saved
    gmat = _mm_auto(o, dy, "tn", "dsw_out_g")
    dw_out, dgate = _wout_grad(gmat, W["dsw_out"], gate, name="dsw_out_dw")
    do = _mm_auto(dy, W["dsw_out"], "nt", "dsw_out_dx", a_scale=gate)
    G = len(DSW_GROUPS)
    dqkv, dbias, dq_norm, dk_norm = None, [], 0.0, 0.0
    for g in range(G):
        *dqkv, db, dqg, dkg = _dsw_attn_bwd(q, k, v, o, lse, do, W["dsw_bias"][g], W["dsw_q_norm"],
                                            W["dsw_k_norm"], dqkv, g=g, name=f"dsw_attn_bwd_{g}")
        dbias.append(db)
        dq_norm, dk_norm = dq_norm + dqg, dk_norm + dkg
    names = ("dsw_q", "dsw_k", "dsw_v")
    dws = [_mm_auto(h, d, "tn", f"dsw_in_{n[-1]}_dw", out_dtype=BF16) for n, d in zip(names, dqkv)]
    dh = _mm_sum_nt([(d, W[n], _tile(d.shape[1], 1024), 0) for n, d in zip(names, dqkv)], name="dsw_in_dx")
    dx, dsh, dsc, dgain = _norm_mod_bwd(dh, x, dy, gain, sc, name="dsw_norm_bwd")
    hot = _dsw_bucket_onehot()
    drel = [_mm(dbias[g].reshape(DSW_HEADS, -1), hot[g], mode="nn", name=f"dsw_rel_bias_{g}", tm=DSW_HEADS,
                tn=LANES, tk=8192)[:, :REL_BUCKETS] for g in range(G)]
    return dx, dict(dsw_w_in=jnp.concatenate(dws, axis=1), dsw_w_out=dw_out, dsw_q_norm=dq_norm,
                    dsw_k_norm=dk_norm, rel_bias=jnp.transpose(jnp.concatenate(drel, axis=0)),
                    gain=dgain, mod=(dsh, dsc, dgate))


def _local_step(x, target, mod, W, late_weights=None, early_pairs=None):
    mods = [[_row(mod[l, i]) for i in range(6)] for l in range(2)]
    nmix = [_row(W["norm_mix"][l]) for l in range(2)]
    nffn = [_row(W["norm_ffn"][l]) for l in range(2)]
    x1, s_gdn, arrived = _gdn_fwd(x, mods[0][:3], nmix[0], W, None if late_weights is None else late_weights[0])
    if late_weights is not None:
        W = {**W, **late_weights[1](arrived)}
    x2, s_f0 = _ffn_fwd(x1, mods[0][3:], nffn[0], W["w_ffn_in"][0], W["w_ffn_out"][0], "0")
    x3, s_dsw = _dsw_fwd(x2, mods[1][:3], nmix[1], W)
    x4, s_f1 = _ffn_fwd(x3, mods[1][3:], nffn[1], W["w_ffn_in"][1], W["w_ffn_out"][1], "1")
    dx4, sse = _loss_head(x4, target, name="loss_head")
    dx3, g_f1 = _ffn_bwd(dx4, s_f1, mods[1][3:], nffn[1], W["w_ffn_in"][1], W["w_ffn_out"][1], "1")
    dx2, g_dsw = _dsw_bwd(dx3, s_dsw, mods[1][:3], nmix[1], W)
    dx1, g_f0 = _ffn_bwd(dx2, s_f0, mods[0][3:], nffn[0], W["w_ffn_in"][0], W["w_ffn_out"][0], "0")
    grads = dict(
        w_ffn_in=jnp.stack([g_f0["w_in"], g_f1["w_in"]]), w_ffn_out=jnp.stack([g_f0["w_out"], g_f1["w_out"]]),
        dsw_w_in=g_dsw["dsw_w_in"][None], dsw_w_out=g_dsw["dsw_w_out"][None])
    riding = None if early_pairs is None else early_pairs(grads)
    dx0, g_gdn, rode = _gdn_bwd(dx1, s_gdn, mods[0][:3], nmix[0], W, riding)
    dmod = jnp.stack([jnp.concatenate(list(g_gdn["mod"]) + list(g_f0["mod"]), axis=0),
                      jnp.concatenate(list(g_dsw["mod"]) + list(g_f1["mod"]), axis=0)])
    grads.update(
        norm_mix=jnp.concatenate([g_gdn["gain"], g_dsw["gain"]], axis=0),
        norm_ffn=jnp.concatenate([g_f0["gain"], g_f1["gain"]], axis=0),
        gdn_w_in=g_gdn["gdn_w_in"][None], gdn_conv=g_gdn["gdn_conv"][None], gdn_w_out=g_gdn["gdn_w_out"][None],
        gdn_out_norm=g_gdn["gdn_out_norm"], gdn_a_log=g_gdn["gdn_a_log"], gdn_dt_bias=g_gdn["gdn_dt_bias"],
        dsw_q_norm=g_dsw["dsw_q_norm"], dsw_k_norm=g_dsw["dsw_k_norm"], rel_bias=g_dsw["rel_bias"])
    return sse, dx0, grads, dmod, rode


def _prepare_first(full, small):
    gw = full["gdn_w_in"][0]
    hk3 = 3 * GDN_HEADS * GDN_DK
    return dict(
        gdn_qkv=_hm(gw[:, :hk3]), gdn_z=gw[:, hk3:hk3 + GDN_HEADS * GDN_DK],
        gdn_ab=jnp.pad(gw[:, hk3 + GDN_HEADS * GDN_DK:], ((0, 0), (0, LANES - 2 * GDN_HEADS))),
        gdn_conv=_hm(full["gdn_conv"][0]), gdn_out=full["gdn_w_out"][0],
        norm_mix=small["norm_mix"], norm_ffn=small["norm_ffn"],
        gdn_a_log=small["gdn_a_log"].reshape(GDN_HEADS, 1, 1), gdn_dt_bias=small["gdn_dt_bias"].reshape(GDN_HEADS, 1, 1),
        gdn_out_norm=small["gdn_out_norm"], dsw_q_norm=small["dsw_q_norm"], dsw_k_norm=small["dsw_k_norm"],
        dsw_bias=_dsw_bias(small["rel_bias"]))


def _prepare_rest(full):
    di = full["dsw_w_in"][0]
    dq = di.shape[1] // 3
    return dict(w_ffn_in=full["w_ffn_in"], w_ffn_out=full["w_ffn_out"],
                dsw_q=di[:, :dq], dsw_k=di[:, dq:2 * dq], dsw_v=di[:, 2 * dq:], dsw_out=full["dsw_w_out"][0])


def _prepare_weights(full, small):
    return {**_prepare_first(full, small), **_prepare_rest(full)}


_W_NAMES = ("w_ada", "b_ada", "norm_mix", "norm_ffn", "w_ffn_in", "w_ffn_out", "gdn_w_in", "gdn_conv",
            "gdn_a_log", "gdn_dt_bias", "gdn_out_norm", "gdn_w_out", "dsw_w_in", "dsw_q_norm", "dsw_k_norm",
            "dsw_w_out", "rel_bias")
_PAD_BATCH = 16


def _pad_rows(a, rows):
    return jnp.pad(a, ((0, rows - a.shape[0]), (0, 0)))


def kernel(x, c, w_ada, b_ada, norm_mix, norm_ffn, w_ffn_in, w_ffn_out, gdn_w_in, gdn_conv, gdn_a_log, gdn_dt_bias, gdn_out_norm, gdn_w_out, dsw_w_in, dsw_q_norm, dsw_k_norm, dsw_w_out, rel_bias, loss_target, m_w_ada, m_b_ada, m_norm_mix, m_norm_ffn, m_w_ffn_in, m_w_ffn_out, m_gdn_w_in, m_gdn_conv, m_gdn_a_log, m_gdn_dt_bias, m_gdn_out_norm, m_gdn_w_out, m_dsw_w_in, m_dsw_q_norm, m_dsw_k_norm, m_dsw_w_out, m_rel_bias, v_w_ada, v_b_ada, v_norm_mix, v_norm_ffn, v_w_ffn_in, v_w_ffn_out, v_gdn_w_in, v_gdn_conv, v_gdn_a_log, v_gdn_dt_bias, v_gdn_out_norm, v_gdn_w_out, v_dsw_w_in, v_dsw_q_norm, v_dsw_k_norm, v_dsw_w_out, v_rel_bias):
    w = dict(zip(_W_NAMES, (w_ada, b_ada, norm_mix, norm_ffn, w_ffn_in, w_ffn_out, gdn_w_in, gdn_conv, gdn_a_log,
                            gdn_dt_bias, gdn_out_norm, gdn_w_out, dsw_w_in, dsw_q_norm, dsw_k_norm, dsw_w_out,
                            rel_bias)))
    m = dict(zip(_W_NAMES, (m_w_ada, m_b_ada, m_norm_mix, m_norm_ffn, m_w_ffn_in, m_w_ffn_out, m_gdn_w_in,
                            m_gdn_conv, m_gdn_a_log, m_gdn_dt_bias, m_gdn_out_norm, m_gdn_w_out, m_dsw_w_in,
                            m_dsw_q_norm, m_dsw_k_norm, m_dsw_w_out, m_rel_bias)))
    v = dict(zip(_W_NAMES, (v_w_ada, v_b_ada, v_norm_mix, v_norm_ffn, v_w_ffn_in, v_w_ffn_out, v_gdn_w_in,
                            v_gdn_conv, v_gdn_a_log, v_gdn_dt_bias, v_gdn_out_norm, v_gdn_w_out, v_dsw_w_in,
                            v_dsw_q_norm, v_dsw_k_norm, v_dsw_w_out, v_rel_bias)))
    D = x.shape[-1]
    n_layers, _, ada_cols = w_ada.shape

    c_all = _exchange(c.reshape(D // LANES, LANES), gather=True, name="gather_cond").reshape(N_DEV, D)
    c_pad = _pad_rows(c_all, _PAD_BATCH)
    proj = [_mm(c_pad, w_ada[l], mode="nn", name=f"ada_proj_{l}", tm=_PAD_BATCH, tn=ada_cols, tk=D, a_silu=True)
            for l in range(n_layers)]
    mod_send = _pack([(jnp.stack([p[:N_DEV] for p in proj], axis=1), 1)], _ROW_ALIGN)
    mod_recv = _exchange(mod_send, gather=False, name="scatter_mod")
    mod = _unpack(mod_recv, [(n_layers, ada_cols)], 1)[0]
    mod = jnp.transpose(mod, (1, 0, 2)).reshape(n_layers, N_DEV * ada_cols) + b_ada
    mod = mod.reshape(n_layers, 6, D)

    conv_hi = gdn_conv.astype(BF16)
    conv_lo = (gdn_conv - conv_hi.astype(F32)).astype(BF16)
    first_send = _pack([(conv_hi if n == "gdn_conv" else w[n].astype(BF16), 0) for n in _LATE] + [(conv_lo, 0)],
                       _ROW_ALIGN)
    parts = _unpack(_gather_two_level(first_send, name="gather_weights_first"),
                    [w[n].shape for n in _LATE] + [gdn_conv.shape], 1)
    full = {n: _to_natural(parts[i], _SHARD_AXIS[n]) for i, n in enumerate(_LATE)}
    full["gdn_conv"] = full["gdn_conv"].astype(F32) + _to_natural(parts[-1], _SHARD_AXIS["gdn_conv"]).astype(F32)
    W = _prepare_first(full, {n: w[n] for n in _SMALL})
    packed_early = tuple(n for n in _EARLY if n not in _NATIVE)
    rest_send = (_pack([(w[n].astype(BF16), 0) for n in packed_early], _ROW_ALIGN),
                 ) + tuple(w[n].astype(BF16) for n in _NATIVE)

    def rest_weights(arrived):
        filled = _fill_from_sibling(arrived, name="swap_weights")
        by_dev = [a.reshape((N_DEV,) + a.shape[2:]) for a in filled]
        blocks = dict(zip(packed_early, _unpack(by_dev[0], [w[n].shape for n in packed_early], 1)))
        blocks.update(zip(_NATIVE, by_dev[1:]))
        return _prepare_rest({n: _to_natural(blocks[n], _SHARD_AXIS[n]) for n in _EARLY})

    my_c = lax.axis_index("c")

    def pair_sums(g, packed, native, tag):
        sends = [_pack([(_to_blocked(g[n].astype(BF16), _SHARD_AXIS[n]), 1) for n in packed], _BIG_ALIGN)]
        sends += [_to_blocked(g[n].astype(BF16), _SHARD_AXIS[n]) for n in native]
        by_core = [s.reshape((N_DEV // 2, 2) + s.shape[1:]) for s in sends]
        keep = [lax.dynamic_index_in_dim(s, my_c, axis=1, keepdims=False) for s in by_core]
        give = [lax.dynamic_index_in_dim(s, 1 - my_c, axis=1, keepdims=False) for s in by_core]
        got = _swap_with_sibling(give, name=f"swap_grads_{tag}")
        return tuple(_add_pair(k, t, name=f"add_sibling_grads_{tag}_{j}") for j, (k, t) in enumerate(zip(keep, got)))

    sse, grad_x, grads, dmod, early_recv = _local_step(
        x[0], loss_target[0], mod, W, late_weights=(rest_send, rest_weights),
        early_pairs=lambda g: pair_sums(g, packed_early, _NATIVE, "early"))
    loss = lax.psum(0.5 * sse[0, 0] / D, ("x", "y", "c"))
    grads["b_ada"] = dmod.reshape(n_layers, 6 * D)
    late_recv = _exchange_chips(pair_sums(grads, _LATE, (), "late")[0], name="scatter_grads_late")
    g_parts = dict(zip(packed_early, _unpack(early_recv[0], [w[n].shape for n in packed_early], 1)))
    g_parts.update(zip(_NATIVE, early_recv[1:]))
    g_parts.update(zip(_LATE, _unpack(late_recv, [w[n].shape for n in _LATE], 1)))

    dmod_send = _pack([(jnp.transpose(dmod.reshape(n_layers, N_DEV, ada_cols), (1, 0, 2)), 1)], _ROW_ALIGN)
    small_send = _pack([(grads[n].reshape(w[n].shape), 0) for n in _SMALL], _ROW_ALIGN)
    s_recv = _exchange(jnp.concatenate(
        [dmod_send, jnp.broadcast_to(small_send[None], (N_DEV,) + small_send.shape)], axis=1),
        gather=False, name="scatter_small")
    dmod_rows = dmod_send.shape[1]

    out = {}
    kinds = ("grad", "delta", "new_m", "new_v")
    for n in _BIG:
        g4 = g_parts[n]
        rows2d = lambda a: a.reshape((-1, w[n].shape[-1]))
        res = _adamw(rows2d(w[n]), g4.reshape((g4.shape[0], -1, w[n].shape[-1])), rows2d(m[n]), rows2d(v[n]),
                     name=f"adamw_{n}")
        for kind, buf in zip(kinds, res):
            out[kind, n] = buf.reshape(w[n].shape)

    dmod_all = _unpack(lax.slice_in_dim(s_recv, 0, dmod_rows, axis=1), [(n_layers, ada_cols)], 1)[0]
    g_ada = jnp.stack([_mm(c_pad, _pad_rows(dmod_all[:, l], _PAD_BATCH), mode="tn", name=f"ada_dw_{l}",
                           tm=D, tn=ada_cols, tk=_PAD_BATCH, a_silu=True) for l in range(n_layers)])
    flat = lambda a: a.reshape(n_layers * D, ada_cols)
    res = _adamw(flat(w_ada), flat(g_ada)[None], flat(m_w_ada), flat(v_w_ada), name="adamw_ada")
    for kind, buf in zip(("grad", "delta", "new_m", "new_v"), res):
        out[kind, "w_ada"] = buf.reshape(w_ada.shape)

    small_parts = lax.slice_in_dim(s_recv, dmod_rows, s_recv.shape[1], axis=1)
    packed = [_pack([(t[n], 0) for n in _SMALL], _ROW_ALIGN) for t in (w, m, v)]
    res = _adamw(packed[0], small_parts, packed[1], packed[2], name="adamw_replicated")
    for kind, buf in zip(("grad", "delta", "new_m", "new_v"), res):
        for n, a in zip(_SMALL, _unpack(buf, [w[n].shape for n in _SMALL], 0)):
            out[kind, n] = a

    return (loss, grad_x[None]) + tuple(out[kind, n] for kind in ("grad", "delta", "new_m", "new_v")
                                        for n in _W_NAMES)
```

```python
import functools
import math

import numpy as np
import jax
import jax.numpy as jnp
from jax import lax
from jax.experimental import pallas as pl
from jax.experimental.pallas import tpu as pltpu

F32 = jnp.float32
BF16 = jnp.bfloat16

N_DEV = 8
RMS_EPS = 1e-6
LANES = 128
V7X_VMEM_LIMIT = 48 * 1024 * 1024

GDN_HEADS = 8
GDN_DK = 128
GDN_CHUNK = 64
GDN_CONV = 4
DSW_GROUPS = ((128, 1), (512, 4), (2048, 16))
DSW_HEADS = 8
DSW_DH = 64
DSW_BLK = 128
REL_BUCKETS = 32
REL_MAX_DIST = 2048

ADAM_LR = 0.001
ADAM_B1 = 0.9
ADAM_B2 = 0.999
ADAM_EPS = 1e-08
ADAM_WD = 0.01
ADAM_STEP = 10

NEG_BIG = -1e30


def _params(*sem):
    return pltpu.CompilerParams(dimension_semantics=sem, vmem_limit_bytes=V7X_VMEM_LIMIT)


def _sigmoid(x):
    return 1.0 / (1.0 + jnp.exp(-x))


def _silu(x):
    return x * _sigmoid(x)


_DOT_DIMS = {
    "nn": (((1,), (0,)), ((), ())),
    "nt": (((1,), (1,)), ((), ())),
    "tn": (((0,), (0,)), ((), ())),
}


def _mm(a, b, *, mode, name, tm, tn, tk, out_dtype=F32, a_scale=None, out_scale=None, resid=None, a_silu=False):
    if mode == "nn":
        (M, K), N = a.shape, b.shape[1]
    elif mode == "nt":
        (M, K), N = a.shape, b.shape[0]
    else:
        (K, M), N = a.shape, b.shape[1]
    tm, tn, tk = min(tm, M), min(tn, N), min(tk, K)
    assert M % tm == 0 and N % tn == 0 and K % tk == 0, (name, M, N, K, tm, tn, tk)
    nk = K // tk

    def body(*refs):
        refs = list(refs)
        a_ref, b_ref = refs.pop(0), refs.pop(0)
        as_ref = refs.pop(0) if a_scale is not None else None
        os_ref = refs.pop(0) if out_scale is not None else None
        r_ref = refs.pop(0) if resid is not None else None
        o_ref = refs.pop(0)
        acc_ref = refs.pop(0) if nk > 1 else None

        av = a_ref[...]
        if a_silu:
            av = _silu(av.astype(F32))
        if as_ref is not None:
            av = av.astype(F32) * as_ref[...]
        part = lax.dot_general(av.astype(BF16), b_ref[...].astype(BF16), _DOT_DIMS[mode],
                               preferred_element_type=F32)

        def finish(r):
            if os_ref is not None:
                r = r * os_ref[...]
            if r_ref is not None:
                r = r + r_ref[...].astype(F32)
            o_ref[...] = r.astype(out_dtype)

        if nk == 1:
            finish(part)
        else:
            k = pl.program_id(2)

            @pl.when(k == 0)
            def _():
                acc_ref[...] = part

            @pl.when(k > 0)
            def _():
                acc_ref[...] += part

            @pl.when(k == nk - 1)
            def _():
                finish(acc_ref[...])

    if mode == "nn":
        a_spec = pl.BlockSpec((tm, tk), lambda i, j, k: (i, k))
        b_spec = pl.BlockSpec((tk, tn), lambda i, j, k: (k, j))
        as_spec = pl.BlockSpec((1, tk), lambda i, j, k: (0, k))
    elif mode == "nt":
        a_spec = pl.BlockSpec((tm, tk), lambda i, j, k: (i, k))
        b_spec = pl.BlockSpec((tn, tk), lambda i, j, k: (j, k))
        as_spec = pl.BlockSpec((1, tk), lambda i, j, k: (0, k))
    else:
        a_spec = pl.BlockSpec((tk, tm), lambda i, j, k: (k, i))
        b_spec = pl.BlockSpec((tk, tn), lambda i, j, k: (k, j))
        as_spec = None
    in_specs, args = [a_spec, b_spec], [a, b]
    if a_scale is not None:
        in_specs.append(as_spec)
        args.append(a_scale)
    if out_scale is not None:
        in_specs.append(pl.BlockSpec((1, tn), lambda i, j, k: (0, j)))
        args.append(out_scale)
    if resid is not None:
        in_specs.append(pl.BlockSpec((tm, tn), lambda i, j, k: (i, j)))
        args.append(resid)
    return pl.pallas_call(
        body, name=name, grid=(M // tm, N // tn, nk),
        in_specs=in_specs, out_specs=pl.BlockSpec((tm, tn), lambda i, j, k: (i, j)),
        out_shape=jax.ShapeDtypeStruct((M, N), out_dtype),
        scratch_shapes=[pltpu.VMEM((tm, tn), F32)] if nk > 1 else [],
        compiler_params=_params("parallel", "parallel", "arbitrary"),
    )(*args)


_MM_ROWS = 1024


def _mm_sum_nt(pairs, *, name, tm=_MM_ROWS, tn=1024):
    M, N = pairs[0][0].shape[0], pairs[0][1].shape[0]
    tm, tn = _tile(M, tm), _tile(N, tn)
    spans, start = [], 0
    for a, b, tk, off in pairs:
        K = a.shape[1]
        assert a.shape[0] == M and b.shape[0] == N and K % tk == 0 and off % tk == 0, name
        spans.append((start, K // tk, tk, off // tk))
        start += K // tk
    total = start

    def body(*refs):
        o_ref, acc_ref = refs[-2:]
        k = pl.program_id(2)

        @pl.when(k == 0)
        def _():
            acc_ref[...] = jnp.zeros_like(acc_ref)

        for p, (s0, nk, _, _) in enumerate(spans):
            a_ref, b_ref = refs[2 * p], refs[2 * p + 1]

            @pl.when((k >= s0) & (k < s0 + nk))
            def _():
                acc_ref[...] += lax.dot_general(a_ref[...].astype(BF16), b_ref[...].astype(BF16), _DOT_DIMS["nt"],
                                                preferred_element_type=F32)

        @pl.when(k == total - 1)
        def _():
            o_ref[...] = acc_ref[...]

    def spec(rows, tk, s0, nk, koff, axis):
        def index(i, j, k):
            return ((i, j)[axis], jnp.clip(k - s0, 0, nk - 1) + koff)
        return pl.BlockSpec((rows, tk), index)

    in_specs, args = [], []
    for (a, b, _, _), (s0, nk, tk, koff) in zip(pairs, spans):
        in_specs += [spec(tm, tk, s0, nk, 0, 0), spec(tn, tk, s0, nk, koff, 1)]
        args += [a, b]
    return pl.pallas_call(
        body, name=name, grid=(M // tm, N // tn, total), in_specs=in_specs,
        out_specs=pl.BlockSpec((tm, tn), lambda i, j, k: (i, j)),
        out_shape=jax.ShapeDtypeStruct((M, N), F32), scratch_shapes=[pltpu.VMEM((tm, tn), F32)],
        compiler_params=_params("parallel", "parallel", "arbitrary"),
    )(*args)


def _norm_mod_fwd(x, gain, sc, sh, *, name):
    S, D = x.shape
    tr = min(512, S)

    def body(x_ref, g_ref, sc_ref, sh_ref, h_ref):
        xv = x_ref[...]
        r = lax.rsqrt(jnp.mean(xv * xv, axis=-1, keepdims=True) + RMS_EPS)
        h_ref[...] = ((xv * r) * g_ref[...] * (1.0 + sc_ref[...]) + sh_ref[...]).astype(BF16)

    row = pl.BlockSpec((tr, D), lambda i: (i, 0))
    vec = pl.BlockSpec((1, D), lambda i: (0, 0))
    return pl.pallas_call(
        body, name=name, grid=(S // tr,), in_specs=[row, vec, vec, vec], out_specs=row,
        out_shape=jax.ShapeDtypeStruct((S, D), BF16), compiler_params=_params("parallel"),
    )(x, gain, sc, sh)


def _norm_mod_bwd(dh, x, dx_res, gain, sc, *, name):
    S, D = x.shape
    tr = min(256, S)
    n_steps = S // tr

    def body(dh_ref, x_ref, dxr_ref, g_ref, sc_ref, dx_ref, dsh_ref, dsc_ref, dgain_ref, acc_sh, acc_a):
        i = pl.program_id(0)
        xv = x_ref[...]
        r = lax.rsqrt(jnp.mean(xv * xv, axis=-1, keepdims=True) + RMS_EPS)
        n = xv * r
        dhv = dh_ref[...].astype(F32)
        dn = dhv * (g_ref[...] * (1.0 + sc_ref[...]))
        dx_ref[...] = dxr_ref[...] + r * (dn - n * jnp.mean(dn * n, axis=-1, keepdims=True))
        p_sh = jnp.sum(dhv, axis=0, keepdims=True)
        p_a = jnp.sum(dhv * n, axis=0, keepdims=True)

        @pl.when(i == 0)
        def _():
            acc_sh[...] = p_sh
            acc_a[...] = p_a

        @pl.when(i > 0)
        def _():
            acc_sh[...] += p_sh
            acc_a[...] += p_a

        @pl.when(i == n_steps - 1)
        def _():
            dsh_ref[...] = acc_sh[...]
            dsc_ref[...] = acc_a[...] * g_ref[...]
            dgain_ref[...] = acc_a[...] * (1.0 + sc_ref[...])

    row = pl.BlockSpec((tr, D), lambda i: (i, 0))
    vec = pl.BlockSpec((1, D), lambda i: (0, 0))
    vshape = jax.ShapeDtypeStruct((1, D), F32)
    return pl.pallas_call(
        body, name=name, grid=(n_steps,), in_specs=[row, row, row, vec, vec],
        out_specs=[row, vec, vec, vec],
        out_shape=[jax.ShapeDtypeStruct((S, D), F32), vshape, vshape, vshape],
        scratch_shapes=[pltpu.VMEM((1, D), F32), pltpu.VMEM((1, D), F32)],
        compiler_params=_params("arbitrary"),
    )(dh, x, dx_res, gain, sc)


def _wout_grad(gmat, w, gate, *, name):
    K, D = w.shape
    tr = min(256, K)
    n_steps = K // tr

    def body(g_ref, w_ref, gate_ref, dw_ref, dgate_ref, acc):
        i = pl.program_id(0)
        gv = g_ref[...]
        dw_ref[...] = (gv * gate_ref[...]).astype(BF16)
        part = jnp.sum(gv * w_ref[...], axis=0, keepdims=True)

        @pl.when(i == 0)
        def _():
            acc[...] = part

        @pl.when(i > 0)
        def _():
            acc[...] += part

        @pl.when(i == n_steps - 1)
        def _():
            dgate_ref[...] = acc[...]

    row = pl.BlockSpec((tr, D), lambda i: (i, 0))
    vec = pl.BlockSpec((1, D), lambda i: (0, 0))
    return pl.pallas_call(
        body, name=name, grid=(n_steps,), in_specs=[row, row, vec], out_specs=[row, vec],
        out_shape=[jax.ShapeDtypeStruct((K, D), BF16), jax.ShapeDtypeStruct((1, D), F32)],
        scratch_shapes=[pltpu.VMEM((1, D), F32)], compiler_params=_params("arbitrary"),
    )(gmat, w, gate)


def _loss_head(y, target, *, name):
    S, D = y.shape
    tr = min(512, S)
    n_steps = S // tr

    def body(y_ref, t_ref, dy_ref, sse_ref, acc):
        i = pl.program_id(0)
        e = y_ref[...] - t_ref[...]
        dy_ref[...] = e * (1.0 / D)
        part = jnp.sum(e * e, axis=0, keepdims=True)

        @pl.when(i == 0)
        def _():
            acc[...] = part

        @pl.when(i > 0)
        def _():
            acc[...] += part

        @pl.when(i == n_steps - 1)
        def _():
            sse_ref[...] = jnp.sum(acc[...], axis=1, keepdims=True)

    row = pl.BlockSpec((tr, D), lambda i: (i, 0))
    return pl.pallas_call(
        body, name=name, grid=(n_steps,), in_specs=[row, row],
        out_specs=[row, pl.BlockSpec((1, 1), lambda i: (0, 0))],
        out_shape=[jax.ShapeDtypeStruct((S, D), F32), jax.ShapeDtypeStruct((1, 1), F32)],
        scratch_shapes=[pltpu.VMEM((1, D), F32)], compiler_params=_params("arbitrary"),
    )(y, target)


def _adamw(w, g_parts, m, v, *, name):
    R, C = w.shape
    P = g_parts.shape[0]
    tr = _tile(R, max(8, 1024 * LANES // C))
    c1 = 1.0 / (1.0 - ADAM_B1 ** ADAM_STEP)
    c2 = 1.0 / (1.0 - ADAM_B2 ** ADAM_STEP)

    def body(w_ref, g_ref, m_ref, v_ref, go_ref, d_ref, mo_ref, vo_ref):
        g = g_ref[0].astype(F32)
        for q in range(1, P):
            g = g + g_ref[q].astype(F32)
        mn = ADAM_B1 * m_ref[...] + (1.0 - ADAM_B1) * g
        vn = ADAM_B2 * v_ref[...] + (1.0 - ADAM_B2) * (g * g)
        go_ref[...] = g
        mo_ref[...] = mn
        vo_ref[...] = vn
        d_ref[...] = -ADAM_LR * ((mn * c1) / (jnp.sqrt(vn * c2) + ADAM_EPS) + ADAM_WD * w_ref[...])

    row = pl.BlockSpec((tr, C), lambda i: (i, 0))
    shp = jax.ShapeDtypeStruct((R, C), F32)
    return pl.pallas_call(
        body, name=name, grid=(R // tr,),
        in_specs=[row, pl.BlockSpec((P, tr, C), lambda i: (0, i, 0)), row, row],
        out_specs=[row, row, row, row], out_shape=[shp, shp, shp, shp],
        compiler_params=_params("parallel"),
    )(w, g_parts, m, v)


_HALO = 16


def _conv_taps(buf, w_ref, rows, cols):
    acc = None
    for j in range(GDN_CONV):
        term = buf[pl.ds(_HALO - (GDN_CONV - 1) + j, rows), cols] * w_ref[j:j + 1, cols]
        acc = term if acc is None else acc + term
    return acc


def _fill_conv_buf(buf, halo_ref, x_ref, rows, first):
    buf[0:_HALO, :] = jnp.where(first, 0.0, halo_ref[...].astype(F32))
    buf[_HALO:_HALO + rows, :] = x_ref[...].astype(F32)


_HM = 3 * GDN_DK
_GDN_ROWS = 256
_PREP_HEADS = 4


def _l2n(seg):
    return lax.rsqrt(jnp.sum(seg * seg, axis=-1, keepdims=True) + RMS_EPS)


def _head_cols(hh):
    return slice(hh * _HM, (hh + 1) * _HM)


def _gdn_prep_fwd(x, conv_w, *, name):
    S, C3 = x.shape
    CB = _PREP_HEADS * _HM
    RB = min(256, S)

    def body(x_ref, halo_ref, w_ref, o_ref, buf):
        i = pl.program_id(0)
        _fill_conv_buf(buf, halo_ref, x_ref, RB, i == 0)
        for hh in range(_PREP_HEADS):
            c0 = hh * _HM
            y = _silu(_conv_taps(buf, w_ref, RB, _head_cols(hh)))
            q, k = y[:, :GDN_DK], y[:, GDN_DK:2 * GDN_DK]
            o_ref[:, c0:c0 + GDN_DK] = q * (_l2n(q) * GDN_DK ** -0.5)
            o_ref[:, c0 + GDN_DK:c0 + 2 * GDN_DK] = k * _l2n(k)
            o_ref[:, c0 + 2 * GDN_DK:c0 + _HM] = y[:, 2 * GDN_DK:]

    hb = RB // _HALO
    return pl.pallas_call(
        body, name=name, grid=(S // RB, C3 // CB),
        in_specs=[pl.BlockSpec((RB, CB), lambda i, j: (i, j)),
                  pl.BlockSpec((_HALO, CB), lambda i, j: (jnp.maximum(i * hb - 1, 0), j)),
                  pl.BlockSpec((GDN_CONV, CB), lambda i, j: (0, j))],
        out_specs=pl.BlockSpec((RB, CB), lambda i, j: (i, j)),
        out_shape=jax.ShapeDtypeStruct((S, C3), F32),
        scratch_shapes=[pltpu.VMEM((RB + _HALO, CB), F32)],
        compiler_params=_params("parallel", "parallel"),
    )(x, x, conv_w)


def _gdn_prep_bwd_pre(dn, x, conv_w, *, name):
    S, C3 = x.shape
    CB = _PREP_HEADS * _HM
    RB = min(256, S)
    n_steps = S // RB

    def body(dn_ref, x_ref, halo_ref, w_ref, dc_ref, dw_ref, buf):
        i = pl.program_id(1)
        _fill_conv_buf(buf, halo_ref, x_ref, RB, i == 0)
        head_parts = []
        for hh in range(_PREP_HEADS):
            c0, cols = hh * _HM, _head_cols(hh)
            acc = _conv_taps(buf, w_ref, RB, cols)
            sg = _sigmoid(acc)
            y = acc * sg
            dsilu = sg * (1.0 + acc * (1.0 - sg))
            for part, scale in ((0, GDN_DK ** -0.5), (1, 1.0)):
                sl = slice(part * GDN_DK, (part + 1) * GDN_DK)
                seg = y[:, sl]
                r = _l2n(seg)
                n = seg * r
                d = dn_ref[:, c0 + part * GDN_DK:c0 + (part + 1) * GDN_DK] * scale
                dc_ref[:, c0 + part * GDN_DK:c0 + (part + 1) * GDN_DK] = (
                    r * (d - n * jnp.sum(d * n, axis=-1, keepdims=True)) * dsilu[:, sl])
            dc_ref[:, c0 + 2 * GDN_DK:c0 + _HM] = dn_ref[:, c0 + 2 * GDN_DK:c0 + _HM] * dsilu[:, 2 * GDN_DK:]
            dc = dc_ref[:, cols]
            taps = [jnp.sum(dc * buf[pl.ds(_HALO - (GDN_CONV - 1) + t, RB), cols], axis=0, keepdims=True)
                    for t in range(GDN_CONV)]
            head_parts.append(jnp.concatenate(taps + [jnp.zeros((8 - GDN_CONV, _HM), F32)], axis=0))
        part = jnp.concatenate(head_parts, axis=1)

        @pl.when(i == 0)
        def _():
            dw_ref[...] = part

        @pl.when(i > 0)
        def _():
            dw_ref[...] += part

    hb = RB // _HALO
    return pl.pallas_call(
        body, name=name, grid=(C3 // CB, n_steps),
        in_specs=[pl.BlockSpec((RB, CB), lambda j, i: (i, j)),
                  pl.BlockSpec((RB, CB), lambda j, i: (i, j)),
                  pl.BlockSpec((_HALO, CB), lambda j, i: (jnp.maximum(i * hb - 1, 0), j)),
                  pl.BlockSpec((GDN_CONV, CB), lambda j, i: (0, j))],
        out_specs=[pl.BlockSpec((RB, CB), lambda j, i: (i, j)),
                   pl.BlockSpec((8, CB), lambda j, i: (0, j))],
        out_shape=[jax.ShapeDtypeStruct((S, C3), F32), jax.ShapeDtypeStruct((8, C3), F32)],
        scratch_shapes=[pltpu.VMEM((RB + _HALO, CB), F32)],
        compiler_params=_params("parallel", "arbitrary"),
    )(dn, x, x, conv_w)


def _gdn_conv_bwd_x(dc, conv_w, *, name):
    S, C3 = dc.shape
    CB = _PREP_HEADS * _HM
    RB = min(256, S)
    n_steps = S // RB

    def body(dc_ref, halo_ref, w_ref, dx_ref, buf):
        i = pl.program_id(0)
        buf[0:RB, :] = dc_ref[...]
        buf[RB:RB + _HALO, :] = jnp.where(i == n_steps - 1, 0.0, halo_ref[...])
        for hh in range(_PREP_HEADS):
            cols = _head_cols(hh)
            acc = None
            for j in range(GDN_CONV):
                term = buf[pl.ds(GDN_CONV - 1 - j, RB), cols] * w_ref[j:j + 1, cols]
                acc = term if acc is None else acc + term
            dx_ref[:, cols] = acc.astype(BF16)

    hb = RB // _HALO
    last = S // _HALO - 1
    return pl.pallas_call(
        body, name=name, grid=(n_steps, C3 // CB),
        in_specs=[pl.BlockSpec((RB, CB), lambda i, j: (i, j)),
                  pl.BlockSpec((_HALO, CB), lambda i, j: (jnp.minimum((i + 1) * hb, last), j)),
                  pl.BlockSpec((GDN_CONV, CB), lambda i, j: (0, j))],
        out_specs=pl.BlockSpec((RB, CB), lambda i, j: (i, j)),
        out_shape=jax.ShapeDtypeStruct((S, C3), BF16),
        scratch_shapes=[pltpu.VMEM((RB + _HALO, CB), F32)],
        compiler_params=_params("parallel", "parallel"),
    )(dc, dc, conv_w)


def _split_bf16(a):
    hi = a.astype(BF16)
    return hi, (a - hi.astype(F32)).astype(BF16)


def _dot(a, b, dims="nn", exact=False):
    def dot(p, q):
        return lax.dot_general(p, q, _DOT_DIMS[dims], preferred_element_type=F32)

    if exact:
        (ah, al), (bh, bl) = _split_bf16(a), _split_bf16(b)
        return dot(ah, bh) + (dot(ah, bl) + dot(al, bh))
    return dot(a.astype(BF16), b.astype(BF16))


def _softplus(x):
    return jnp.maximum(x, 0.0) + jnp.log(1.0 + jnp.exp(-jnp.abs(x)))


def _to_col(row, eye):
    return jnp.sum(jnp.where(eye, row, 0.0), axis=1, keepdims=True)


def _to_row(col, eye):
    return jnp.sum(jnp.where(eye, col, 0.0), axis=0, keepdims=True)


def _unit_lower_inverse(low, ri, ci):
    n = range(len(low))
    C = low[0].shape[0]
    eye = jnp.where(ri == ci, 1.0, 0.0)
    pair = (ri >> 1) == (ci >> 1)
    x = [eye - jnp.where(pair, low[j], 0.0) for j in n]
    m, sh = 2, 1
    while m < C:
        join = ((ri >> (sh + 1)) == (ci >> (sh + 1))) & (((ri >> sh) & 1) == 1) & (((ci >> sh) & 1) == 0)
        y = [_dot(x[j], jnp.where(join, low[j], 0.0)) for j in n]
        x = [x[j] - _dot(y[j], x[j]) for j in n]
        m, sh = 2 * m, sh + 1
    lx = [_dot(low[j], x[j], exact=True) for j in n]
    corr = [_dot(x[j], eye - x[j] - lx[j]) for j in n]
    return [x[j] + corr[j] for j in n]


def _gdn_local_batch(qkv, g_row, beta_row, ri, ci):
    n = range(len(qkv))
    eye, tril, strict = ri == ci, ri >= ci, ri > ci
    q = [qkv[j][:, :GDN_DK] for j in n]
    k = [qkv[j][:, GDN_DK:2 * GDN_DK] for j in n]
    v = [qkv[j][:, 2 * GDN_DK:] for j in n]
    g_col = [_to_col(g_row[j], eye) for j in n]
    beta_col = [_to_col(beta_row[j], eye) for j in n]
    gc_col = [jnp.sum(jnp.where(tril, g_row[j], 0.0), axis=1, keepdims=True) for j in n]
    gc_row = [jnp.sum(jnp.where(ri <= ci, g_col[j], 0.0), axis=0, keepdims=True) for j in n]
    g_last = [jnp.sum(g_row[j], axis=1, keepdims=True) for j in n]
    decay = [jnp.where(tril, jnp.exp(jnp.minimum(gc_col[j] - gc_row[j], 0.0)), 0.0) for j in n]
    e_col = [jnp.exp(gc_col[j]) for j in n]
    f_col = [jnp.exp(g_last[j] - gc_col[j]) for j in n]
    e_last = [jnp.exp(g_last[j]) for j in n]
    kb = [k[j] * beta_col[j] for j in n]
    vb = [v[j] * beta_col[j] for j in n]
    kk = [_dot(kb[j], k[j], "nt") for j in n]
    qk = [_dot(q[j], k[j], "nt") for j in n]
    low = [jnp.where(strict, kk[j] * decay[j], 0.0) for j in n]
    att = [qk[j] * decay[j] for j in n]
    return dict(q=q, k=k, v=v, beta_col=beta_col, decay=decay, e_col=e_col, f_col=f_col, e_last=e_last,
                kb=kb, vb=vb, low=low, att=att, eye=eye, strict=strict, tril=tril)


def _chunk_iotas():
    C = GDN_CHUNK
    return lax.broadcasted_iota(jnp.int32, (C, C), 0), lax.broadcasted_iota(jnp.int32, (C, C), 1)


def _gdn_chunk_fwd(qkv, ab, a_log, dt_bias, *, name, riding=None):
    S = qkv.shape[0]
    H, C, DK = GDN_HEADS, GDN_CHUNK, GDN_DK
    RB = min(_GDN_ROWS, S)
    NCB, NB, NC = RB // C, S // RB, S // C
    heads = range(H)

    def body(qkv_ref, ab_ref, alog_ref, dtb_ref, *rest):
        n_ride = 0 if riding is None else len(riding)
        ride_srcs, rest = rest[:n_ride], rest[n_ride:]
        (o_ref, st_ref, t_ref), rest = rest[:3], rest[3:]
        ride_dsts, rest = rest[:n_ride], rest[n_ride:]
        state, u_s, w_s, qe_s, kf_s, att_s, *ride_sems = rest
        nb = pl.program_id(0)
        if riding is not None:
            finish_ride = _ride(nb == 0, nb == NB - 1, ride_srcs, ride_dsts, ride_sems, True)

        @pl.when(nb == 0)
        def _():
            state[...] = jnp.zeros_like(state)

        ri, ci = _chunk_iotas()
        neg_a = [-jnp.exp(alog_ref[h]) for h in heads]
        e_last = []
        for c in range(NCB):
            rows = pl.ds(c * C, C)
            g_row = [neg_a[h] * _softplus(ab_ref[h, c] + dtb_ref[h]) for h in heads]
            beta_row = [_sigmoid(ab_ref[H + h, c]) for h in heads]
            L = _gdn_local_batch([qkv_ref[rows, h * _HM:(h + 1) * _HM] for h in heads], g_row, beta_row, ri, ci)
            tinv = _unit_lower_inverse(L["low"], ri, ci)
            u = [_dot(tinv[h], L["vb"][h], exact=True) for h in heads]
            w = [_dot(tinv[h], L["kb"][h] * L["e_col"][h], exact=True) for h in heads]
            for h in heads:
                t_ref[h, c] = tinv[h]
                u_s[c, h] = u[h]
                w_s[c, h] = w[h].astype(BF16)
                qe_s[c, h] = (L["q"][h] * L["e_col"][h]).astype(BF16)
                kf_s[c, h] = (L["k"][h] * L["f_col"][h]).astype(BF16)
                att_s[c, h] = L["att"][h].astype(BF16)
            e_last.append(L["e_last"])
        st = [state[h] for h in heads]
        for c in range(NCB):
            rows = pl.ds(c * C, C)
            stb = [st[h].astype(BF16) for h in heads]
            vn = [u_s[c, h] - _dot(w_s[c, h], stb[h]) for h in heads]
            vnb = [vn[h].astype(BF16) for h in heads]
            out = [_dot(qe_s[c, h], stb[h]) + _dot(att_s[c, h], vnb[h]) for h in heads]
            new = [st[h] * e_last[c][h] + _dot(kf_s[c, h], vnb[h], "tn") for h in heads]
            for h in heads:
                o_ref[rows, h * DK:(h + 1) * DK] = out[h]
                st_ref[h, c] = st[h]
            st = new
        for h in heads:
            state[h] = st[h]
        if riding is not None:
            finish_ride()

    ride_args, ride_specs, ride_out, ride_scratch = _riding(riding, True)
    return pl.pallas_call(
        body, name=name, grid=(NB,),
        in_specs=[pl.BlockSpec((RB, H * _HM), lambda n: (n, 0)),
                  pl.BlockSpec((2 * H, NCB, 1, C), lambda n: (0, n, 0, 0)),
                  pl.BlockSpec((H, 1, 1), lambda n: (0, 0, 0)),
                  pl.BlockSpec((H, 1, 1), lambda n: (0, 0, 0))] + ride_specs,
        out_specs=[pl.BlockSpec((RB, H * DK), lambda n: (n, 0)),
                   pl.BlockSpec((H, NCB, DK, DK), lambda n: (0, n, 0, 0)),
                   pl.BlockSpec((H, NCB, C, C), lambda n: (0, n, 0, 0))] + ride_specs,
        out_shape=[jax.ShapeDtypeStruct((S, H * DK), F32),
                   jax.ShapeDtypeStruct((H, NC, DK, DK), F32),
                   jax.ShapeDtypeStruct((H, NC, C, C), F32)] + ride_out,
        scratch_shapes=[pltpu.VMEM((H, DK, DK), F32), pltpu.VMEM((NCB, H, C, DK), F32),
                        pltpu.VMEM((NCB, H, C, DK), BF16), pltpu.VMEM((NCB, H, C, DK), BF16),
                        pltpu.VMEM((NCB, H, C, DK), BF16), pltpu.VMEM((NCB, H, C, C), BF16)] + ride_scratch,
        compiler_params=_params("arbitrary"),
    )(qkv, ab, a_log, dt_bias, *ride_args)


_CHIP_PEERS = N_DEV // 2 - 1


def _chip_copies(src_refs, dst_refs, send_sems, recv_sems, local_sems, gather=False):
    x, y, c = lax.axis_index("x"), lax.axis_index("y"), lax.axis_index("c")
    here = 2 * x + y
    copies = []
    for a, (src_ref, dst_ref) in enumerate(zip(src_refs, dst_refs)):
        landing = dst_ref.at[here, c] if gather else dst_ref.at[here]
        copies.append(pltpu.make_async_copy(src_ref if gather else src_ref.at[here], landing, local_sems.at[a]))
        for rel in range(1, N_DEV // 2):
            px = 1 - x if rel & 2 else x
            py = 1 - y if rel & 1 else y
            k = a * _CHIP_PEERS + rel - 1
            copies.append(pltpu.make_async_remote_copy(
                src_ref=src_ref if gather else src_ref.at[2 * px + py], dst_ref=landing,
                send_sem=send_sems.at[k], recv_sem=recv_sems.at[k],
                device_id=(px, py, c), device_id_type=pl.DeviceIdType.MESH))
    return copies


def _chip_sems(n):
    return [pltpu.SemaphoreType.DMA((n * _CHIP_PEERS,)), pltpu.SemaphoreType.DMA((n * _CHIP_PEERS,)),
            pltpu.SemaphoreType.DMA((n,))]


def _riding(riding, gather):
    if riding is None:
        return [], [], [], []
    shapes = [jax.ShapeDtypeStruct(((N_DEV // 2, 2) + r.shape) if gather else r.shape, r.dtype) for r in riding]
    return list(riding), [pl.BlockSpec(memory_space=pl.ANY)] * len(riding), shapes, _chip_sems(len(riding))


def _ride(first, last, srcs, dsts, sems, gather):
    @pl.when(first)
    def _():
        for cp in _chip_copies(srcs, dsts, *sems, gather=gather):
            cp.start()

    def finish():
        @pl.when(last)
        def _():
            for cp in _chip_copies(srcs, dsts, *sems, gather=gather):
                cp.wait()

    return finish


def _gdn_chunk_bwd(qkv, ab, a_log, dt_bias, states, tinvs, do, *, name, riding=None):
    S = qkv.shape[0]
    H, C, DK = GDN_HEADS, GDN_CHUNK, GDN_DK
    RB = min(_GDN_ROWS, S)
    NCB, NB, NC = RB // C, S // RB, S // C
    heads = range(H)

    def body(qkv_ref, ab_ref, alog_ref, dtb_ref, st_ref, t_ref, do_ref, *rest):
        n_ride = 0 if riding is None else len(riding)
        ride_srcs, rest = rest[:n_ride], rest[n_ride:]
        (dqkv_ref, dab_ref, dalog_ref, ddtb_ref), rest = rest[:4], rest[4:]
        ride_dsts, rest = rest[:n_ride], rest[n_ride:]
        dstate, w_s, vn_s, qe_s, kf_s, att_s, dvn_s, dkf_s, *ride_sems = rest
        nb = pl.program_id(0)
        if riding is not None:
            finish_ride = _ride(nb == 0, nb == NB - 1, ride_srcs, ride_dsts, ride_sems, False)

        @pl.when(nb == 0)
        def _():
            dstate[...] = jnp.zeros_like(dstate)
            dalog_ref[...] = jnp.zeros_like(dalog_ref)
            ddtb_ref[...] = jnp.zeros_like(ddtb_ref)

        ri, ci = _chunk_iotas()
        neg_a = [-jnp.exp(alog_ref[h]) for h in heads]

        def local(c):
            rows = pl.ds(c * C, C)
            a_pre = [ab_ref[h, c] + dtb_ref[h] for h in heads]
            g_row = [neg_a[h] * _softplus(a_pre[h]) for h in heads]
            beta_row = [_sigmoid(ab_ref[H + h, c]) for h in heads]
            L = _gdn_local_batch([qkv_ref[rows, h * _HM:(h + 1) * _HM] for h in heads], g_row, beta_row, ri, ci)
            return L, a_pre, g_row, beta_row

        e_last = [None] * NCB
        for c in range(NCB):
            L, _, _, _ = local(c)
            kbe = [L["kb"][h] * L["e_col"][h] for h in heads]
            u = [_dot(t_ref[h, c], L["vb"][h], exact=True) for h in heads]
            w = [_dot(t_ref[h, c], kbe[h], exact=True) for h in heads]
            vn = [u[h] - _dot(w[h], st_ref[h, c]) for h in heads]
            for h in heads:
                w_s[c, h] = w[h].astype(BF16)
                vn_s[c, h] = vn[h].astype(BF16)
                qe_s[c, h] = (L["q"][h] * L["e_col"][h]).astype(BF16)
                kf_s[c, h] = (L["k"][h] * L["f_col"][h]).astype(BF16)
                att_s[c, h] = L["att"][h].astype(BF16)
            e_last[c] = L["e_last"]

        dst = [dstate[h] for h in heads]
        de_last = [None] * NCB
        for c in reversed(range(NCB)):
            rows = pl.ds(c * C, C)
            dob = [do_ref[rows, h * DK:(h + 1) * DK].astype(BF16) for h in heads]
            dstb = [dst[h].astype(BF16) for h in heads]
            dvn = [_dot(att_s[c, h], dob[h], "tn") + _dot(kf_s[c, h], dstb[h]) for h in heads]
            dkf = [_dot(vn_s[c, h], dstb[h], "nt") for h in heads]
            de_last[c] = [jnp.sum(jnp.sum(dst[h] * st_ref[h, c], axis=1, keepdims=True), axis=0, keepdims=True)
                          for h in heads]
            new = [dst[h] * e_last[c][h] + _dot(qe_s[c, h], dob[h], "tn")
                   - _dot(w_s[c, h], dvn[h].astype(BF16), "tn") for h in heads]
            for h in heads:
                dvn_s[c, h] = dvn[h]
                dkf_s[c, h] = dkf[h]
            dst = new
        for h in heads:
            dstate[h] = dst[h]

        for c in range(NCB):
            rows = pl.ds(c * C, C)
            L, a_pre, g_row, beta_row = local(c)
            q, k, v, kb, vb = L["q"], L["k"], L["v"], L["kb"], L["vb"]
            e_col, f_col, decay, beta_col = L["e_col"], L["f_col"], L["decay"], L["beta_col"]
            eye, strict, tril = L["eye"], L["strict"], L["tril"]
            tinv = [t_ref[h, c] for h in heads]
            stb = [st_ref[h, c].astype(BF16) for h in heads]
            dov = [do_ref[rows, h * DK:(h + 1) * DK] for h in heads]
            dvn = [dvn_s[c, h] for h in heads]
            dkf = [dkf_s[c, h] for h in heads]
            kbe = [kb[h] * e_col[h] for h in heads]
            datt = [jnp.where(tril, _dot(dov[h], vn_s[c, h], "nt"), 0.0) for h in heads]
            dqe = [_dot(dov[h], stb[h], "nt") for h in heads]
            dw = [-_dot(dvn[h], stb[h], "nt") for h in heads]
            dt = [_dot(dvn[h], vb[h], "nt") + _dot(dw[h], kbe[h], "nt") for h in heads]
            dvb = [_dot(tinv[h], dvn[h], "tn", exact=True) for h in heads]
            dkbe = [_dot(tinv[h], dw[h], "tn", exact=True) for h in heads]
            tdt = [_dot(tinv[h], dt[h], "tn", exact=True) for h in heads]
            dlow = [-jnp.where(strict, _dot(tdt[h], tinv[h], "nt", exact=True), 0.0) for h in heads]
            dkk = [dlow[h] * decay[h] for h in heads]
            dqk = [datt[h] * decay[h] for h in heads]
            dkb = [_dot(dkk[h], k[h]) + dkbe[h] * e_col[h] for h in heads]
            dk = [_dot(dkk[h], kb[h], "tn") + _dot(dqk[h], q[h], "tn") + dkf[h] * f_col[h] + dkb[h] * beta_col[h]
                  for h in heads]
            dq = [_dot(dqk[h], k[h]) + dqe[h] * e_col[h] for h in heads]
            for h in heads:
                dqkv_ref[rows, h * _HM:h * _HM + DK] = dq[h]
                dqkv_ref[rows, h * _HM + DK:h * _HM + 2 * DK] = dk[h]
                dqkv_ref[rows, h * _HM + 2 * DK:(h + 1) * _HM] = dvb[h] * beta_col[h]

            dbeta_col = [jnp.sum(k[h] * dkb[h] + v[h] * dvb[h], axis=1, keepdims=True) for h in heads]
            pmat = [dlow[h] * L["low"][h] + datt[h] * L["att"][h] for h in heads]
            df_col = [jnp.sum(k[h] * dkf[h], axis=1, keepdims=True) * f_col[h] for h in heads]
            dgc_col = [jnp.sum(pmat[h], axis=1, keepdims=True)
                       + jnp.sum(q[h] * dqe[h] + kb[h] * dkbe[h], axis=1, keepdims=True) * e_col[h] - df_col[h]
                       for h in heads]
            dgc_row = [_to_row(dgc_col[h], eye) - jnp.sum(pmat[h], axis=0, keepdims=True) for h in heads]
            dg_last = [jnp.sum(df_col[h], axis=0, keepdims=True) + de_last[c][h] * L["e_last"][h] for h in heads]
            dgc_c = [_to_col(dgc_row[h], eye) for h in heads]
            dg_row = [jnp.sum(jnp.where(ri >= ci, dgc_c[h], 0.0), axis=0, keepdims=True) + dg_last[h] for h in heads]
            dbeta_row = [_to_row(dbeta_col[h], eye) for h in heads]
            for h in heads:
                da_row = dg_row[h] * neg_a[h] * _sigmoid(a_pre[h])
                dab_ref[h, c] = da_row
                dab_ref[H + h, c] = dbeta_row[h] * beta_row[h] * (1.0 - beta_row[h])
                dalog_ref[h] += jnp.sum(dg_row[h] * g_row[h], axis=1, keepdims=True)
                ddtb_ref[h] += jnp.sum(da_row, axis=1, keepdims=True)

        if riding is not None:
            finish_ride()

    rev = lambda n: NB - 1 - n
    vec = pl.BlockSpec((H, 1, 1), lambda n: (0, 0, 0))
    gates = pl.BlockSpec((2 * H, NCB, 1, C), lambda n: (0, rev(n), 0, 0))
    wide = pl.BlockSpec((RB, H * _HM), lambda n: (rev(n), 0))
    item = lambda dt: pltpu.VMEM((NCB, H, C, DK), dt)
    ride_args, ride_specs, ride_out, ride_scratch = _riding(riding, False)
    return pl.pallas_call(
        body, name=name, grid=(NB,),
        in_specs=[wide, gates, vec, vec,
                  pl.BlockSpec((H, NCB, DK, DK), lambda n: (0, rev(n), 0, 0)),
                  pl.BlockSpec((H, NCB, C, C), lambda n: (0, rev(n), 0, 0)),
                  pl.BlockSpec((RB, H * DK), lambda n: (rev(n), 0))] + ride_specs,
        out_specs=[wide, gates, vec, vec] + ride_specs,
        out_shape=[jax.ShapeDtypeStruct((S, H * _HM), F32),
                   jax.ShapeDtypeStruct((2 * H, NC, 1, C), F32),
                   jax.ShapeDtypeStruct((H, 1, 1), F32),
                   jax.ShapeDtypeStruct((H, 1, 1), F32)] + ride_out,
        scratch_shapes=[pltpu.VMEM((H, DK, DK), F32), item(BF16), item(BF16), item(BF16), item(BF16),
                        pltpu.VMEM((NCB, H, C, C), BF16), item(F32), item(F32)] + ride_scratch,
        compiler_params=_params("arbitrary"),
    )(qkv, ab, a_log, dt_bias, states, tinvs, do, *ride_args)


def _gdn_outnorm_fwd(o, z, gain, *, name):
    S, HV = o.shape
    RB = min(256, S)

    def body(o_ref, z_ref, g_ref, y_ref):
        for h in range(HV // GDN_DK):
            cols = slice(h * GDN_DK, (h + 1) * GDN_DK)
            ov = o_ref[:, cols]
            r = lax.rsqrt(jnp.mean(ov * ov, axis=-1, keepdims=True) + RMS_EPS)
            y_ref[:, cols] = (ov * r * g_ref[...] * _silu(z_ref[:, cols].astype(F32))).astype(BF16)

    blk = pl.BlockSpec((RB, HV), lambda i: (i, 0))
    return pl.pallas_call(
        body, name=name, grid=(S // RB,),
        in_specs=[blk, blk, pl.BlockSpec((1, GDN_DK), lambda i: (0, 0))], out_specs=blk,
        out_shape=jax.ShapeDtypeStruct((S, HV), BF16), compiler_params=_params("parallel"),
    )(o, z, gain)


def _gdn_outnorm_bwd(dy, o, z, gain, *, name):
    S, HV = o.shape
    RB = min(256, S)

    def body(dy_ref, o_ref, z_ref, g_ref, do_ref, dz_ref, dg_ref):
        part = None
        for h in range(HV // GDN_DK):
            cols = slice(h * GDN_DK, (h + 1) * GDN_DK)
            ov = o_ref[:, cols]
            zv = z_ref[:, cols].astype(F32)
            dyv = dy_ref[:, cols].astype(F32)
            r = lax.rsqrt(jnp.mean(ov * ov, axis=-1, keepdims=True) + RMS_EPS)
            n = ov * r
            sg = _sigmoid(zv)
            dng = dyv * (zv * sg)
            dn = dng * g_ref[...]
            do_ref[:, cols] = r * (dn - n * jnp.mean(dn * n, axis=-1, keepdims=True))
            dz_ref[:, cols] = (dyv * (n * g_ref[...]) * (sg * (1.0 + zv * (1.0 - sg)))).astype(BF16)
            p = jnp.sum(dng * n, axis=0, keepdims=True)
            part = p if part is None else part + p

        @pl.when(pl.program_id(0) == 0)
        def _():
            dg_ref[...] = part

        @pl.when(pl.program_id(0) > 0)
        def _():
            dg_ref[...] += part

    blk = pl.BlockSpec((RB, HV), lambda i: (i, 0))
    vec = pl.BlockSpec((1, GDN_DK), lambda i: (0, 0))
    return pl.pallas_call(
        body, name=name, grid=(S // RB,),
        in_specs=[blk, blk, blk, vec], out_specs=[blk, blk, vec],
        out_shape=[jax.ShapeDtypeStruct((S, HV), F32), jax.ShapeDtypeStruct((S, HV), BF16),
                   jax.ShapeDtypeStruct((1, GDN_DK), F32)],
        compiler_params=_params("arbitrary"),
    )(dy, o, z, gain)


def _head_mask():
    return lax.broadcasted_iota(jnp.int32, (DSW_BLK, LANES), 1) < DSW_DH


def _per_head_sum(t, first):
    s0 = jnp.sum(jnp.where(first, t, 0.0), axis=-1, keepdims=True)
    s1 = jnp.sum(jnp.where(first, 0.0, t), axis=-1, keepdims=True)
    return jnp.where(first, s0, s1)


def _rms2(x, gain, first):
    r = lax.rsqrt(_per_head_sum(x * x, first) * (1.0 / DSW_DH) + RMS_EPS)
    xh = x * r
    return xh, r, xh * gain


def _rms2_bwd(dy, xh, r, gain, first):
    dxh = dy * gain
    return r * (dxh - xh * (_per_head_sum(dxh * xh, first) * (1.0 / DSW_DH)))


def _split_heads(x, first):
    return [jnp.where(first, x, 0.0).astype(BF16), jnp.where(first, 0.0, x).astype(BF16)]


_HP = LANES // DSW_DH
_DSW_W = DSW_HEADS * DSW_DH
_DSW_ROWS = 1024
_DSW_BATCH = 8


def _dsw_geometry(S, g):
    d = DSW_GROUPS[g][1]
    slab = DSW_BLK * d
    tb = max(1, min(_DSW_ROWS, S) // slab)
    return d, slab, tb, S // (tb * slab)


def _block_rows(t, r, slab, d):
    return pl.ds(t * slab + r, DSW_BLK) if d == 1 else pl.ds(t * slab + r, DSW_BLK, stride=d)


def _dsw_attn_fwd(q, k, v, bias, q_gain, k_gain, prev_out, *, g, name):
    S, WT = q.shape
    B = DSW_BLK
    d, slab, tb, n_tiles = _dsw_geometry(S, g)
    rt = tb * slab
    cb = g * (_DSW_W // LANES)
    batch_res = max(1, _DSW_BATCH // tb)

    def body(q_ref, kp_ref, kc_ref, vp_ref, vc_ref, bias_ref, qg_ref, kg_ref, *rest):
        o_ref, lse_ref = rest[-2:]
        i = pl.program_id(1)
        qg, kg = qg_ref[...] * DSW_DH ** -0.5, kg_ref[...]
        col = lax.broadcasted_iota(jnp.int32, (B, 2 * B), 1)
        first = _head_mask()
        heads = range(_HP)
        for r0 in range(0, d, batch_res):
            res = range(r0, min(d, r0 + batch_res))
            k_raw = {(r, -1): kp_ref[_block_rows(0, r, slab, d), :] for r in res}
            v_raw = {(r, -1): vp_ref[_block_rows(0, r, slab, d), :] for r in res}
            q_raw = {}
            for r in res:
                for t in range(tb):
                    rows = _block_rows(t, r, slab, d)
                    q_raw[r, t], k_raw[r, t], v_raw[r, t] = q_ref[rows, :], kc_ref[rows, :], vc_ref[rows, :]
            kn = {key: _rms2(x, kg, first)[2].astype(BF16) for key, x in k_raw.items()}
            vb = {key: x.astype(BF16) for key, x in v_raw.items()}
            qn = {key: _split_heads(_rms2(x, qg, first)[2], first) for key, x in q_raw.items()}
            items = [(r, t, h) for r in res for t in range(tb) for h in heads]
            s = {}
            for r, t, h in items:
                sv = _dot(qn[r, t][h], jnp.concatenate([kn[r, t - 1], kn[r, t]], axis=0), "nt") + bias_ref[h]
                s[r, t, h] = jnp.where((i == 0) & (col < B), NEG_BIG, sv) if t == 0 else sv
            m = {it: jnp.max(s[it], axis=-1, keepdims=True) for it in items}
            p = {it: jnp.exp(s[it] - m[it]) for it in items}
            l = {it: jnp.sum(p[it], axis=-1, keepdims=True) for it in items}
            o = {(r, t, h): _dot(p[r, t, h], jnp.concatenate([vb[r, t - 1], vb[r, t]], axis=0)) for r, t, h in items}
            for r in res:
                for t in range(tb):
                    rows = _block_rows(t, r, slab, d)
                    o_ref[rows, :] = jnp.where(first, o[r, t, 0] / l[r, t, 0], o[r, t, 1] / l[r, t, 1])
                    lse_ref[rows, :] = jnp.where(first, m[r, t, 0] + jnp.log(l[r, t, 0]),
                                                 m[r, t, 1] + jnp.log(l[r, t, 1]))

    cur = pl.BlockSpec((rt, LANES), lambda hp, i: (i, cb + hp))
    prev = pl.BlockSpec((slab, LANES), lambda hp, i: (jnp.maximum(i * tb - 1, 0), cb + hp))
    vec = pl.BlockSpec((1, LANES), lambda hp, i: (0, 0))
    shp = jax.ShapeDtypeStruct((S, WT), F32)
    carried = [] if prev_out is None else list(prev_out)
    n_in = 8
    return pl.pallas_call(
        body, name=name, grid=(_DSW_W // LANES, n_tiles),
        in_specs=[cur, prev, cur, prev, cur, pl.BlockSpec((_HP, B, 2 * B), lambda hp, i: (hp, 0, 0)), vec, vec]
                 + [pl.BlockSpec(memory_space=pl.ANY)] * len(carried),
        out_specs=[cur, cur], out_shape=[shp, shp],
        input_output_aliases={n_in + j: j for j in range(len(carried))},
        compiler_params=_params("parallel", "parallel"),
    )(q, k, k, v, v, bias, jnp.tile(q_gain, (1, _HP)), jnp.tile(k_gain, (1, _HP)), *carried)


def _dsw_merge(o_g, lse_g, *, name):
    S = o_g.shape[0]
    W, G = _DSW_W, len(DSW_GROUPS)
    tr = min(512, S)

    def body(o_ref, l_ref, out_ref, lse_ref):
        ls = [l_ref[:, g * W:(g + 1) * W] for g in range(G)]
        m = ls[0]
        for g in range(1, G):
            m = jnp.maximum(m, ls[g])
        den = jnp.zeros_like(m)
        acc = jnp.zeros_like(m)
        for g in range(G):
            wg = jnp.exp(ls[g] - m)
            den = den + wg
            acc = acc + wg * o_ref[:, g * W:(g + 1) * W]
        out_ref[...] = acc / den
        lse_ref[...] = m + jnp.log(den)

    wide = pl.BlockSpec((tr, G * W), lambda i: (i, 0))
    blk = pl.BlockSpec((tr, W), lambda i: (i, 0))
    shp = jax.ShapeDtypeStruct((S, W), F32)
    return pl.pallas_call(
        body, name=name, grid=(S // tr,), in_specs=[wide, wide], out_specs=[blk, blk],
        out_shape=[shp, shp], compiler_params=_params("parallel"),
    )(o_g, lse_g)


def _dsw_attn_bwd(q, k, v, o, lse, do, bias, q_gain, k_gain, prev_out, *, g, name):
    S, WT = q.shape
    B = DSW_BLK
    d, slab, tb, n_tiles = _dsw_geometry(S, g)
    rt = tb * slab
    cb = g * (_DSW_W // LANES)
    n_slabs = S // slab
    scale = DSW_DH ** -0.5
    batch_res = max(1, _DSW_BATCH // tb)

    def body(q_ref, qx_ref, kp_ref, kc_ref, vp_ref, vc_ref, o_ref, ox_ref, l_ref, lx_ref, do_ref, dox_ref,
             bias_ref, qg_ref, kg_ref, *rest):
        dq_ref, dk_ref, dv_ref, db_ref, dqg_ref, dkg_ref = rest[-6:]
        hp, i = pl.program_id(0), pl.program_id(1)
        qg, kg = qg_ref[...] * scale, kg_ref[...]
        col = lax.broadcasted_iota(jnp.int32, (B, 2 * B), 1)
        has_next = i < n_tiles - 1

        @pl.when(i == 0)
        def _():
            db_ref[...] = jnp.zeros_like(db_ref)

        dqg_acc = jnp.zeros((1, LANES), F32)
        dkg_acc = jnp.zeros((1, LANES), F32)
        first = _head_mask()
        heads = range(_HP)
        for r0 in range(0, d, batch_res):
            res = range(r0, min(d, r0 + batch_res))
            q_raw, k_raw, v_raw, o_raw, l_raw, do_raw = {}, {}, {}, {}, {}, {}
            for r in res:
                first_rows = _block_rows(0, r, slab, d)
                k_raw[r, -1], v_raw[r, -1] = kp_ref[first_rows, :], vp_ref[first_rows, :]
                for t in range(tb):
                    rows = _block_rows(t, r, slab, d)
                    q_raw[r, t], o_raw[r, t], l_raw[r, t], do_raw[r, t] = (
                        q_ref[rows, :], o_ref[rows, :], l_ref[rows, :], do_ref[rows, :])
                    k_raw[r, t], v_raw[r, t] = kc_ref[rows, :], vc_ref[rows, :]
                q_raw[r, tb], o_raw[r, tb], l_raw[r, tb], do_raw[r, tb] = (
                    qx_ref[first_rows, :], ox_ref[first_rows, :], lx_ref[first_rows, :], dox_ref[first_rows, :])
            kk = {key: _rms2(x, kg, first) for key, x in k_raw.items()}
            qq = {key: _rms2(x, qg, first) for key, x in q_raw.items()}
            knb = {key: kk[key][2].astype(BF16) for key in kk}
            qnb = {key: _split_heads(qq[key][2], first) for key in qq}
            vb = {key: x.astype(BF16) for key, x in v_raw.items()}
            dob = {key: _split_heads(x, first) for key, x in do_raw.items()}
            delta = {key: _per_head_sum(do_raw[key] * o_raw[key], first) for key in q_raw}
            pick = lambda x, h: x[:, h * DSW_DH:h * DSW_DH + 1]
            full = [(r, t, h) for r in res for t in range(tb) for h in heads]
            half = [(r, tb, h) for r in res for h in heads]
            s = {}
            for r, t, h in full:
                sv = _dot(qnb[r, t][h], jnp.concatenate([knb[r, t - 1], knb[r, t]], axis=0), "nt") + bias_ref[h]
                s[r, t, h] = jnp.where((i == 0) & (col < B), NEG_BIG, sv) if t == 0 else sv
            for r, t, h in half:
                s[r, t, h] = _dot(qnb[r, t][h], knb[r, t - 1], "nt") + bias_ref[h, :, 0:B]
            p = {(r, t, h): jnp.exp(s[r, t, h] - pick(l_raw[r, t], h)) for r, t, h in full}
            for r, t, h in half:
                p[r, t, h] = jnp.where(has_next, jnp.exp(s[r, t, h] - pick(l_raw[r, t], h)), 0.0)
            dp = {(r, t, h): _dot(dob[r, t][h], jnp.concatenate([vb[r, t - 1], vb[r, t]], axis=0), "nt")
                  for r, t, h in full}
            for r, t, h in half:
                dp[r, t, h] = _dot(dob[r, t][h], vb[r, t - 1], "nt")
            ds = {(r, t, h): p[r, t, h] * (dp[r, t, h] - pick(delta[r, t], h)) for r, t, h in full + half}
            pb = {it: p[it].astype(BF16) for it in ds}
            dsb = {it: ds[it].astype(BF16) for it in ds}
            for h in heads:
                tot = None
                for r in res:
                    for t in range(tb):
                        tot = ds[r, t, h] if tot is None else tot + ds[r, t, h]
                db_ref[h] += tot
            blocks = [(r, t) for r in res for t in range(tb)]
            keys2 = {(r, t): jnp.concatenate([knb[r, t - 1], knb[r, t]], axis=0) for r, t in blocks}
            dqn = {(r, t): jnp.where(first, _dot(dsb[r, t, 0], keys2[r, t]), _dot(dsb[r, t, 1], keys2[r, t]))
                   for r, t in blocks}
            prev_half = lambda x, r, t, h: x[r, t, h][:, :B] if t < tb else x[r, t, h]
            dkn = {(r, t): sum(_dot(dsb[r, t, h][:, B:], qnb[r, t][h], "tn")
                               + _dot(prev_half(dsb, r, t + 1, h), qnb[r, t + 1][h], "tn") for h in heads)
                   for r, t in blocks}
            dvv = {(r, t): sum(_dot(pb[r, t, h][:, B:], dob[r, t][h], "tn")
                               + _dot(prev_half(pb, r, t + 1, h), dob[r, t + 1][h], "tn") for h in heads)
                   for r, t in blocks}
            for r, t in blocks:
                dqg_acc = dqg_acc + jnp.sum(dqn[r, t] * qq[r, t][0], axis=0, keepdims=True)
                dkg_acc = dkg_acc + jnp.sum(dkn[r, t] * kk[r, t][0], axis=0, keepdims=True)
            for r, t in blocks:
                rows = _block_rows(t, r, slab, d)
                dq_ref[rows, :] = _rms2_bwd(dqn[r, t], qq[r, t][0], qq[r, t][1], qg, first)
                dk_ref[rows, :] = _rms2_bwd(dkn[r, t], kk[r, t][0], kk[r, t][1], kg, first)
                dv_ref[rows, :] = dvv[r, t]

        start = (hp == 0) & (i == 0)
        fold = lambda a: a[:, :DSW_DH] + a[:, DSW_DH:]

        @pl.when(start)
        def _():
            dqg_ref[...] = fold(dqg_acc) * scale
            dkg_ref[...] = fold(dkg_acc)

        @pl.when(jnp.logical_not(start))
        def _():
            dqg_ref[...] += fold(dqg_acc) * scale
            dkg_ref[...] += fold(dkg_acc)

    def spec(rows, pick, base):
        return pl.BlockSpec((rows, LANES), lambda hp, i: (pick(i), base + hp))

    same = lambda i: i
    before = lambda i: jnp.maximum(i * tb - 1, 0)
    after = lambda i: jnp.minimum((i + 1) * tb, n_slabs - 1)
    cur, cur1 = spec(rt, same, cb), spec(rt, same, 0)
    vec = pl.BlockSpec((1, DSW_DH), lambda hp, i: (0, 0))
    vec2 = pl.BlockSpec((1, LANES), lambda hp, i: (0, 0))
    bspec = pl.BlockSpec((_HP, B, 2 * B), lambda hp, i: (hp, 0, 0))
    shp = jax.ShapeDtypeStruct((S, WT), F32)
    vshp = jax.ShapeDtypeStruct((1, DSW_DH), F32)
    carried = [] if prev_out is None else list(prev_out)
    n_in = 15
    return pl.pallas_call(
        body, name=name, grid=(_DSW_W // LANES, n_tiles),
        in_specs=[cur, spec(slab, after, cb), spec(slab, before, cb), cur, spec(slab, before, cb), cur,
                  cur1, spec(slab, after, 0), cur1, spec(slab, after, 0), cur1, spec(slab, after, 0),
                  bspec, vec2, vec2] + [pl.BlockSpec(memory_space=pl.ANY)] * len(carried),
        out_specs=[cur, cur, cur, bspec, vec, vec],
        out_shape=[shp, shp, shp, jax.ShapeDtypeStruct(bias.shape, F32), vshp, vshp],
        input_output_aliases={n_in + j: j for j in range(len(carried))},
        compiler_params=_params("arbitrary", "arbitrary"),
    )(q, q, k, k, v, v, o, o, lse, lse, do, do, bias, jnp.tile(q_gain, (1, _HP)), jnp.tile(k_gain, (1, _HP)),
      *carried)


def _t5_bucket(dist):
    max_exact = REL_BUCKETS // 2
    scaled = jnp.log(jnp.maximum(dist, 1).astype(F32) / max_exact) / math.log(REL_MAX_DIST / max_exact)
    large = jnp.minimum(max_exact + (scaled * (REL_BUCKETS - max_exact)).astype(jnp.int32), REL_BUCKETS - 1)
    return jnp.where(dist < max_exact, dist, large)


def _dsw_band():
    dist = (jnp.arange(DSW_BLK)[:, None] + DSW_BLK) - jnp.arange(2 * DSW_BLK)[None, :]
    return dist, (dist >= 0) & (dist <= DSW_BLK)


def _dsw_bias(rel_bias):
    dist, band = _dsw_band()
    out = []
    for g, (_, d) in enumerate(DSW_GROUPS):
        hot = jax.nn.one_hot(_t5_bucket(jnp.maximum(dist, 0) * d), REL_BUCKETS, dtype=F32)
        tab = jnp.einsum("qkb,bh->hqk", hot, rel_bias[:, g * DSW_HEADS:(g + 1) * DSW_HEADS],
                         precision=lax.Precision.HIGHEST)
        out.append(jnp.where(band[None], tab, NEG_BIG))
    return jnp.stack(out)


def _dsw_bucket_onehot():
    dist, band = _dsw_band()
    out = []
    for _, d in DSW_GROUPS:
        hot = jax.nn.one_hot(_t5_bucket(jnp.maximum(dist, 0) * d), LANES, dtype=BF16)
        out.append(jnp.where(band[..., None], hot, 0).reshape(-1, LANES))
    return jnp.stack(out)


def _exchange(send, *, gather, name):
    R, C = send.shape[-2:]

    def body(src_ref, dst_ref, send_sems, recv_sems, local_sem):
        x, y, c = lax.axis_index("x"), lax.axis_index("y"), lax.axis_index("c")
        me = 4 * x + 2 * y + c
        mine = pltpu.make_async_copy(src_ref if gather else src_ref.at[me], dst_ref.at[me], local_sem)
        mine.start()
        copies = []
        for rel in range(1, N_DEV):
            px = 1 - x if rel & 4 else x
            py = 1 - y if rel & 2 else y
            pc = 1 - c if rel & 1 else c
            peer = 4 * px + 2 * py + pc
            cp = pltpu.make_async_remote_copy(
                src_ref=src_ref if gather else src_ref.at[peer], dst_ref=dst_ref.at[me],
                send_sem=send_sems.at[rel - 1], recv_sem=recv_sems.at[rel - 1],
                device_id=(px, py, pc), device_id_type=pl.DeviceIdType.MESH)
            cp.start()
            copies.append(cp)
        for cp in copies:
            cp.wait()
        mine.wait()

    return pl.pallas_call(
        body, name=name,
        in_specs=[pl.BlockSpec(memory_space=pl.ANY)], out_specs=pl.BlockSpec(memory_space=pl.ANY),
        out_shape=jax.ShapeDtypeStruct((N_DEV, R, C), send.dtype),
        scratch_shapes=[pltpu.SemaphoreType.DMA((N_DEV - 1,)), pltpu.SemaphoreType.DMA((N_DEV - 1,)),
                        pltpu.SemaphoreType.DMA(())],
    )(send)


def _gather_two_level(send, *, name):
    R, C = send.shape

    def body(src_ref, dst_ref, send_sems, recv_sems, local_sem):
        x, y, c = lax.axis_index("x"), lax.axis_index("y"), lax.axis_index("c")
        me, sibling = (x, y, c), (x, y, 1 - c)
        chips = [(1 - x, y), (x, 1 - y), (1 - x, 1 - y)]

        def slot(px, py, pc):
            return dst_ref.at[4 * px + 2 * py + pc]

        def copy(k, block, to, src=None):
            return pltpu.make_async_remote_copy(
                src_ref=slot(*block) if src is None else src, dst_ref=slot(*block),
                send_sem=send_sems.at[k], recv_sem=recv_sems.at[k],
                device_id=to, device_id_type=pl.DeviceIdType.MESH)

        mine = pltpu.make_async_copy(src_ref, slot(*me), local_sem)
        mine.start()
        first = [copy(0, me, sibling, src=src_ref)]
        first += [copy(1 + j, me, (*chip, c), src=src_ref) for j, chip in enumerate(chips)]
        for cp in first:
            cp.start()
        passed = [copy(4 + j, (*chip, c), sibling) for j, chip in enumerate(chips)]
        for j, chip in enumerate(chips):
            copy(1 + j, (*chip, c), me).wait_recv()
            passed[j].start()
        copy(0, sibling, me).wait_recv()
        for j, chip in enumerate(chips):
            copy(4 + j, (*chip, 1 - c), me).wait_recv()
        for cp in first + passed:
            cp.wait_send()
        mine.wait()

    return pl.pallas_call(
        body, name=name,
        in_specs=[pl.BlockSpec(memory_space=pl.ANY)], out_specs=pl.BlockSpec(memory_space=pl.ANY),
        out_shape=jax.ShapeDtypeStruct((N_DEV, R, C), send.dtype),
        scratch_shapes=[pltpu.SemaphoreType.DMA((N_DEV - 1,)), pltpu.SemaphoreType.DMA((N_DEV - 1,)),
                        pltpu.SemaphoreType.DMA(())],
    )(send)


_ANY = pl.BlockSpec(memory_space=pl.ANY)


def _swap_with_sibling(sends, *, name):
    n = len(sends)

    def body(*refs):
        x, y, c = lax.axis_index("x"), lax.axis_index("y"), lax.axis_index("c")
        send_sems, recv_sems = refs[2 * n:]
        copies = [pltpu.make_async_remote_copy(
            src_ref=refs[a], dst_ref=refs[n + a], send_sem=send_sems.at[a], recv_sem=recv_sems.at[a],
            device_id=(x, y, 1 - c), device_id_type=pl.DeviceIdType.MESH) for a in range(n)]
        for cp in copies:
            cp.start()
        for cp in copies:
            cp.wait()

    return pl.pallas_call(
        body, name=name, in_specs=[_ANY] * n, out_specs=[_ANY] * n,
        out_shape=[jax.ShapeDtypeStruct(s.shape, s.dtype) for s in sends],
        scratch_shapes=[pltpu.SemaphoreType.DMA((n,)), pltpu.SemaphoreType.DMA((n,))],
    )(*sends)


def _fill_from_sibling(bufs, *, name):
    n, n_chips = len(bufs), bufs[0].shape[0]

    def body(*refs):
        x, y, c = lax.axis_index("x"), lax.axis_index("y"), lax.axis_index("c")
        send_sems, recv_sems = refs[2 * n:]
        copies = [pltpu.make_async_remote_copy(
            src_ref=refs[a].at[q, c], dst_ref=refs[n + a].at[q, c],
            send_sem=send_sems.at[a * n_chips + q], recv_sem=recv_sems.at[a * n_chips + q],
            device_id=(x, y, 1 - c), device_id_type=pl.DeviceIdType.MESH) for a in range(n) for q in range(n_chips)]
        for cp in copies:
            cp.start()
        for cp in copies:
            cp.wait()

    return pl.pallas_call(
        body, name=name, in_specs=[_ANY] * n, out_specs=[_ANY] * n,
        out_shape=[jax.ShapeDtypeStruct(b.shape, b.dtype) for b in bufs],
        input_output_aliases={a: a for a in range(n)},
        scratch_shapes=[pltpu.SemaphoreType.DMA((n * n_chips,)), pltpu.SemaphoreType.DMA((n * n_chips,))],
    )(*bufs)


def _exchange_chips(send, *, name):
    def body(src_ref, dst_ref, *sems):
        copies = _chip_copies([src_ref], [dst_ref], *sems)
        for cp in copies:
            cp.start()
        for cp in copies:
            cp.wait()

    return pl.pallas_call(
        body, name=name, in_specs=[_ANY], out_specs=_ANY,
        out_shape=jax.ShapeDtypeStruct(send.shape, send.dtype), scratch_shapes=_chip_sems(1),
    )(send)


def _add_pair(a, b, *, name):
    lead, (R, C) = a.shape[:-2], a.shape[-2:]
    tr = _tile(R, max(8, 1024 * LANES // C))

    def body(a_ref, b_ref, o_ref):
        o_ref[...] = (a_ref[...].astype(F32) + b_ref[...].astype(F32)).astype(o_ref.dtype)

    blk = pl.BlockSpec((None,) * len(lead) + (tr, C), lambda *idx: idx + (0,))
    return pl.pallas_call(
        body, name=name, grid=lead + (R // tr,), in_specs=[blk, blk], out_specs=blk,
        out_shape=jax.ShapeDtypeStruct(a.shape, a.dtype),
        compiler_params=_params(*(("parallel",) * (len(lead) + 1))),
    )(a, b)


_BIG = ("w_ffn_in", "w_ffn_out", "gdn_w_in", "gdn_conv", "gdn_w_out", "dsw_w_in", "dsw_w_out")
_LATE = ("gdn_w_in", "gdn_conv", "gdn_w_out")
_EARLY = tuple(n for n in _BIG if n not in _LATE)
_NATIVE = ("w_ffn_in", "w_ffn_out", "dsw_w_in")
_SHARD_AXIS = {"w_ffn_in": 2, "w_ffn_out": 1, "gdn_w_in": 2, "gdn_conv": 2, "gdn_w_out": 1, "dsw_w_in": 2,
               "dsw_w_out": 2}
_SMALL = ("b_ada", "norm_mix", "norm_ffn", "gdn_a_log", "gdn_dt_bias", "gdn_out_norm", "dsw_q_norm",
          "dsw_k_norm", "rel_bias")
_ROW_ALIGN = 16
_BIG_ALIGN = 1024


def _ceil_to(n, m):
    return -(-n // m) * m


def _seg_rows(shape):
    return _ceil_to(_ceil_to(int(np.prod(shape)), LANES) // LANES, _ROW_ALIGN)


def _pack(arrs, total_align):
    lead = arrs[0][1]
    segs = []
    for a, nlead in arrs:
        assert nlead == lead
        bshape = a.shape[:nlead]
        n = int(np.prod(a.shape[nlead:]))
        rows = _seg_rows(a.shape[nlead:])
        flat = a.reshape(bshape + (n,))
        flat = jnp.pad(flat, [(0, 0)] * nlead + [(0, rows * LANES - n)])
        segs.append(flat.reshape(bshape + (rows, LANES)))
    buf = jnp.concatenate(segs, axis=lead)
    total = _ceil_to(buf.shape[lead], total_align)
    return jnp.pad(buf, [(0, 0)] * lead + [(0, total - buf.shape[lead]), (0, 0)])


def _unpack(buf, shapes, nlead):
    out, off = [], 0
    for shp in shapes:
        n, rows = int(np.prod(shp)), _seg_rows(shp)
        seg = lax.slice_in_dim(buf, off, off + rows, axis=nlead)
        seg = seg.reshape(buf.shape[:nlead] + (rows * LANES,))[..., :n]
        out.append(seg.reshape(buf.shape[:nlead] + tuple(shp)))
        off += rows
    return out


def _to_natural(g, axis):
    n, L, r, c = g.shape
    if axis == 2:
        return jnp.transpose(g, (1, 2, 0, 3)).reshape(L, r, n * c)
    return jnp.transpose(g, (1, 0, 2, 3)).reshape(L, n * r, c)


def _to_blocked(w, axis):
    L, R, C = w.shape
    if axis == 2:
        return jnp.transpose(w.reshape(L, R, N_DEV, C // N_DEV), (2, 0, 1, 3))
    return jnp.transpose(w.reshape(L, N_DEV, R // N_DEV, C), (1, 0, 2, 3))


def _hm(a):
    lead = a.shape[:-1]
    return jnp.swapaxes(a.reshape(lead + (3, GDN_HEADS, GDN_DK)), -3, -2).reshape(lead + (3 * GDN_HEADS * GDN_DK,))


def _un_hm(a):
    lead = a.shape[:-1]
    return jnp.swapaxes(a.reshape(lead + (GDN_HEADS, 3, GDN_DK)), -3, -2).reshape(lead + (3 * GDN_HEADS * GDN_DK,))


_TILES = (2048, 1536, 1408, 1024, 768, 512, 384, 256, 128, 64, 32, 16, 8)


def _tile(n, cap):
    for t in _TILES:
        if t <= cap and n % t == 0:
            return t
    return n


def _mm_auto(a, b, mode, name, **kw):
    if mode == "tn":
        (K, M), N = a.shape, b.shape[1]
        deep = 2048 if a.dtype == BF16 and b.dtype == BF16 else 1024
        tm, tn, tk = _tile(M, 1408), _tile(N, 1408), _tile(K, deep)
    else:
        M, K = a.shape
        N = b.shape[1] if mode == "nn" else b.shape[0]
        tm, tn, tk = _tile(M, _MM_ROWS), _tile(N, 1536), _tile(K, 1408)
    return _mm(a, b, mode=mode, name=name, tm=tm, tn=tn, tk=tk, **kw)


def _row(v):
    return v.reshape(1, -1)


def _ffn_in_act(h, w_in, *, name):
    S, D = h.shape
    F = w_in.shape[1] // 2
    tm, tn = _tile(S, _MM_ROWS), _tile(F, 1408)
    nj = F // tn

    def body(h_ref, wg_ref, wu_ref, g_ref, u_ref, a_ref):
        hv = h_ref[...]
        gate = jnp.dot(hv, wg_ref[...], preferred_element_type=F32)
        up = jnp.dot(hv, wu_ref[...], preferred_element_type=F32)
        g_ref[...] = gate.astype(BF16)
        u_ref[...] = up.astype(BF16)
        a_ref[...] = (_silu(gate) * up).astype(BF16)

    out = pl.BlockSpec((tm, tn), lambda i, j: (i, j))
    shp = jax.ShapeDtypeStruct((S, F), BF16)
    return pl.pallas_call(
        body, name=name, grid=(S // tm, nj),
        in_specs=[pl.BlockSpec((tm, D), lambda i, j: (i, 0)), pl.BlockSpec((D, tn), lambda i, j: (0, j)),
                  pl.BlockSpec((D, tn), lambda i, j: (0, j + nj))],
        out_specs=[out, out, out], out_shape=[shp, shp, shp],
        compiler_params=_params("parallel", "parallel"),
    )(h, w_in, w_in)


def _ffn_out_dx_act(dy, w_out, gate_vec, pg, pu, *, name):
    S, D = dy.shape
    F = w_out.shape[0]
    tm, tn = _tile(S, _MM_ROWS), _tile(F, 1408)

    def body(dy_ref, w_ref, gv_ref, pg_ref, pu_ref, dg_ref, du_ref):
        dyg = (dy_ref[...] * gv_ref[...]).astype(BF16)
        da = lax.dot_general(dyg, w_ref[...], _DOT_DIMS["nt"], preferred_element_type=F32)
        gate = pg_ref[...].astype(F32)
        up = pu_ref[...].astype(F32)
        sg = _sigmoid(gate)
        dg_ref[...] = (da * up * (sg * (1.0 + gate * (1.0 - sg)))).astype(BF16)
        du_ref[...] = (da * (gate * sg)).astype(BF16)

    blk = pl.BlockSpec((tm, tn), lambda i, j: (i, j))
    shp = jax.ShapeDtypeStruct((S, F), BF16)
    return pl.pallas_call(
        body, name=name, grid=(S // tm, F // tn),
        in_specs=[pl.BlockSpec((tm, D), lambda i, j: (i, 0)), pl.BlockSpec((tn, D), lambda i, j: (j, 0)),
                  pl.BlockSpec((1, D), lambda i, j: (0, 0)), blk, blk],
        out_specs=[blk, blk], out_shape=[shp, shp],
        compiler_params=_params("parallel", "parallel"),
    )(dy, w_out, gate_vec, pg, pu)


def _ffn_fwd(x, mod, gain, w_in, w_out, tag):
    sh, sc, gate = mod
    h = _norm_mod_fwd(x, gain, sc, sh, name=f"ffn_norm_{tag}")
    pg, pu, a = _ffn_in_act(h, w_in, name=f"ffn_in_{tag}")
    y = _mm_auto(a, w_out, "nn", f"ffn_out_{tag}", out_scale=gate, resid=x)
    return y, (x, h, pg, pu, a)


def _ffn_bwd(dy, saved, mod, gain, w_in, w_out, tag):
    sh, sc, gate = mod
    x, h, pg, pu, a = saved
    F = pg.shape[1]
    gmat = _mm_auto(a, dy, "tn", f"ffn_out_g_{tag}")
    dw_out, dgate = _wout_grad(gmat, w_out, gate, name=f"ffn_out_dw_{tag}")
    dpg, dpu = _ffn_out_dx_act(dy, w_out, gate, pg, pu, name=f"ffn_out_dx_{tag}")
    dw_in = jnp.concatenate([_mm_auto(h, dpg, "tn", f"ffn_in_dw_gate_{tag}", out_dtype=BF16),
                             _mm_auto(h, dpu, "tn", f"ffn_in_dw_up_{tag}", out_dtype=BF16)], axis=1)
    tk = _tile(F, 1408)
    dh = _mm_sum_nt([(dpg, w_in, tk, 0), (dpu, w_in, tk, F)], name=f"ffn_in_dx_{tag}")
    dx, dsh, dsc, dgain = _norm_mod_bwd(dh, x, dy, gain, sc, name=f"ffn_norm_bwd_{tag}")
    return dx, dict(w_in=dw_in, w_out=dw_out, gain=dgain, mod=(dsh, dsc, dgate))


def _gdn_fwd(x, mod, gain, W, riding=None):
    sh, sc, gate = mod
    S = x.shape[0]
    h = _norm_mod_fwd(x, gain, sc, sh, name="gdn_norm")
    pq = _mm_auto(h, W["gdn_qkv"], "nn", "gdn_in_qkv", out_dtype=BF16)
    z = _mm_auto(h, W["gdn_z"], "nn", "gdn_in_z", out_dtype=BF16)
    ab = _mm_auto(h, W["gdn_ab"], "nn", "gdn_in_ab")
    qkvn = _gdn_prep_fwd(pq, W["gdn_conv"], name="gdn_prep")
    ab4 = jnp.transpose(ab[:, :2 * GDN_HEADS]).reshape(2 * GDN_HEADS, S // GDN_CHUNK, 1, GDN_CHUNK)
    o, states, tinvs, *rode = _gdn_chunk_fwd(qkvn, ab4, W["gdn_a_log"], W["gdn_dt_bias"], name="gdn_chunk",
                                             riding=riding)
    o2 = _gdn_outnorm_fwd(o, z, W["gdn_out_norm"], name="gdn_outnorm")
    y = _mm_auto(o2, W["gdn_out"], "nn", "gdn_out", out_scale=gate, resid=x)
    return y, (x, h, pq, z, qkvn, ab4, o, states, tinvs, o2), (tuple(rode) if rode else None)


def _gdn_bwd(dy, saved, mod, gain, W, riding=None):
    sh, sc, gate = mod
    x, h, pq, z, qkvn, ab4, o, states, tinvs, o2 = saved
    S = x.shape[0]
    gmat = _mm_auto(o2, dy, "tn", "gdn_out_g")
    dw_out, dgate = _wout_grad(gmat, W["gdn_out"], gate, name="gdn_out_dw")
    do2 = _mm_auto(dy, W["gdn_out"], "nt", "gdn_out_dx", a_scale=gate)
    do, dz, dout_norm = _gdn_outnorm_bwd(do2, o, z, W["gdn_out_norm"], name="gdn_outnorm_bwd")
    dqkvn, dab4, da_log, ddt_bias, *rode = _gdn_chunk_bwd(
        qkvn, ab4, W["gdn_a_log"], W["gdn_dt_bias"], states, tinvs, do, name="gdn_chunk_bwd", riding=riding)
    dc, dconv8 = _gdn_prep_bwd_pre(dqkvn, pq, W["gdn_conv"], name="gdn_prep_bwd")
    dpq = _gdn_conv_bwd_x(dc, W["gdn_conv"], name="gdn_conv_bwd")
    dab = jnp.transpose(dab4.reshape(2 * GDN_HEADS, S))
    dab = jnp.pad(dab, ((0, 0), (0, LANES - 2 * GDN_HEADS))).astype(BF16)
    dw_qkv = _mm_auto(h, dpq, "tn", "gdn_in_qkv_dw", out_dtype=BF16)
    dw_z = _mm_auto(h, dz, "tn", "gdn_in_z_dw", out_dtype=BF16)
    dw_ab = _mm_auto(h, dab, "tn", "gdn_in_ab_dw", out_dtype=BF16)
    dh = _mm_sum_nt([(dpq, W["gdn_qkv"], 1024, 0), (dz, W["gdn_z"], 1024, 0), (dab, W["gdn_ab"], LANES, 0)],
                    name="gdn_in_dx")
    dx, dsh, dsc, dgain = _norm_mod_bwd(dh, x, dy, gain, sc, name="gdn_norm_bwd")
    dw_in = jnp.concatenate([_un_hm(dw_qkv), dw_z, dw_ab[:, :2 * GDN_HEADS]], axis=1)
    return dx, dict(gdn_w_in=dw_in, gdn_conv=_un_hm(dconv8[:GDN_CONV]), gdn_w_out=dw_out, gdn_out_norm=dout_norm,
                    gdn_a_log=da_log.reshape(1, GDN_HEADS), gdn_dt_bias=ddt_bias.reshape(1, GDN_HEADS),
                    gain=dgain, mod=(dsh, dsc, dgate)), (tuple(rode) if rode else None)


def _dsw_fwd(x, mod, gain, W):
    sh, sc, gate = mod
    h = _norm_mod_fwd(x, gain, sc, sh, name="dsw_norm")
    q, k, v = (_mm_auto(h, W[n], "nn", f"dsw_in_{n[-1]}") for n in ("dsw_q", "dsw_k", "dsw_v"))
    outs = None
    for g in range(len(DSW_GROUPS)):
        outs = _dsw_attn_fwd(q, k, v, W["dsw_bias"][g], W["dsw_q_norm"], W["dsw_k_norm"], outs, g=g,
                             name=f"dsw_attn_{g}")
    o, lse = _dsw_merge(*outs, name="dsw_merge")
    y = _mm_auto(o, W["dsw_out"], "nn", "dsw_out", out_scale=gate, resid=x)
    return y, (x, h, q, k, v, o, lse)


def _dsw_bwd(dy, saved, mod, gain, W):
    sh, sc, gate = mod
    x, h, q, k, v, o, lse = saved
    gmat = _mm_auto(o, dy, "tn", "dsw_out_g")
    dw_out, dgate = _wout_grad(gmat, W["dsw_out"], gate, name="dsw_out_dw")
    do = _mm_auto(dy, W["dsw_out"], "nt", "dsw_out_dx", a_scale=gate)
    G = len(DSW_GROUPS)
    dqkv, dbias, dq_norm, dk_norm = None, [], 0.0, 0.0
    for g in range(G):
        *dqkv, db, dqg, dkg = _dsw_attn_bwd(q, k, v, o, lse, do, W["dsw_bias"][g], W["dsw_q_norm"],
                                            W["dsw_k_norm"], dqkv, g=g, name=f"dsw_attn_bwd_{g}")
        dbias.append(db)
        dq_norm, dk_norm = dq_norm + dqg, dk_norm + dkg
    names = ("dsw_q", "dsw_k", "dsw_v")
    dws = [_mm_auto(h, d, "tn", f"dsw_in_{n[-1]}_dw", out_dtype=BF16) for n, d in zip(names, dqkv)]
    dh = _mm_sum_nt([(d, W[n], _tile(d.shape[1], 1024), 0) for n, d in zip(names, dqkv)], name="dsw_in_dx")
    dx, dsh, dsc, dgain = _norm_mod_bwd(dh, x, dy, gain, sc, name="dsw_norm_bwd")
    hot = _dsw_bucket_onehot()
    drel = [_mm(dbias[g].reshape(DSW_HEADS, -1), hot[g], mode="nn", name=f"dsw_rel_bias_{g}", tm=DSW_HEADS,
                tn=LANES, tk=8192)[:, :REL_BUCKETS] for g in range(G)]
    return dx, dict(dsw_w_in=jnp.concatenate(dws, axis=1), dsw_w_out=dw_out, dsw_q_norm=dq_norm,
                    dsw_k_norm=dk_norm, rel_bias=jnp.transpose(jnp.concatenate(drel, axis=0)),
                    gain=dgain, mod=(dsh, dsc, dgate))


def _local_step(x, target, mod, W, late_weights=None, early_pairs=None):
    mods = [[_row(mod[l, i]) for i in range(6)] for l in range(2)]
    nmix = [_row(W["norm_mix"][l]) for l in range(2)]
    nffn = [_row(W["norm_ffn"][l]) for l in range(2)]
    x1, s_gdn, arrived = _gdn_fwd(x, mods[0][:3], nmix[0], W, None if late_weights is None else late_weights[0])
    if late_weights is not None:
        W = {**W, **late_weights[1](arrived)}
    x2, s_f0 = _ffn_fwd(x1, mods[0][3:], nffn[0], W["w_ffn_in"][0], W["w_ffn_out"][0], "0")
    x3, s_dsw = _dsw_fwd(x2, mods[1][:3], nmix[1], W)
    x4, s_f1 = _ffn_fwd(x3, mods[1][3:], nffn[1], W["w_ffn_in"][1], W["w_ffn_out"][1], "1")
    dx4, sse = _loss_head(x4, target, name="loss_head")
    dx3, g_f1 = _ffn_bwd(dx4, s_f1, mods[1][3:], nffn[1], W["w_ffn_in"][1], W["w_ffn_out"][1], "1")
    dx2, g_dsw = _dsw_bwd(dx3, s_dsw, mods[1][:3], nmix[1], W)
    dx1, g_f0 = _ffn_bwd(dx2, s_f0, mods[0][3:], nffn[0], W["w_ffn_in"][0], W["w_ffn_out"][0], "0")
    grads = dict(
        w_ffn_in=jnp.stack([g_f0["w_in"], g_f1["w_in"]]), w_ffn_out=jnp.stack([g_f0["w_out"], g_f1["w_out"]]),
        dsw_w_in=g_dsw["dsw_w_in"][None], dsw_w_out=g_dsw["dsw_w_out"][None])
    riding = None if early_pairs is None else early_pairs(grads)
    dx0, g_gdn, rode = _gdn_bwd(dx1, s_gdn, mods[0][:3], nmix[0], W, riding)
    dmod = jnp.stack([jnp.concatenate(list(g_gdn["mod"]) + list(g_f0["mod"]), axis=0),
                      jnp.concatenate(list(g_dsw["mod"]) + list(g_f1["mod"]), axis=0)])
    grads.update(
        norm_mix=jnp.concatenate([g_gdn["gain"], g_dsw["gain"]], axis=0),
        norm_ffn=jnp.concatenate([g_f0["gain"], g_f1["gain"]], axis=0),
        gdn_w_in=g_gdn["gdn_w_in"][None], gdn_conv=g_gdn["gdn_conv"][None], gdn_w_out=g_gdn["gdn_w_out"][None],
        gdn_out_norm=g_gdn["gdn_out_norm"], gdn_a_log=g_gdn["gdn_a_log"], gdn_dt_bias=g_gdn["gdn_dt_bias"],
        dsw_q_norm=g_dsw["dsw_q_norm"], dsw_k_norm=g_dsw["dsw_k_norm"], rel_bias=g_dsw["rel_bias"])
    return sse, dx0, grads, dmod, rode


def _prepare_first(full, small):
    gw = full["gdn_w_in"][0]
    hk3 = 3 * GDN_HEADS * GDN_DK
    return dict(
        gdn_qkv=_hm(gw[:, :hk3]), gdn_z=gw[:, hk3:hk3 + GDN_HEADS * GDN_DK],
        gdn_ab=jnp.pad(gw[:, hk3 + GDN_HEADS * GDN_DK:], ((0, 0), (0, LANES - 2 * GDN_HEADS))),
        gdn_conv=_hm(full["gdn_conv"][0]), gdn_out=full["gdn_w_out"][0],
        norm_mix=small["norm_mix"], norm_ffn=small["norm_ffn"],
        gdn_a_log=small["gdn_a_log"].reshape(GDN_HEADS, 1, 1), gdn_dt_bias=small["gdn_dt_bias"].reshape(GDN_HEADS, 1, 1),
        gdn_out_norm=small["gdn_out_norm"], dsw_q_norm=small["dsw_q_norm"], dsw_k_norm=small["dsw_k_norm"],
        dsw_bias=_dsw_bias(small["rel_bias"]))


def _prepare_rest(full):
    di = full["dsw_w_in"][0]
    dq = di.shape[1] // 3
    return dict(w_ffn_in=full["w_ffn_in"], w_ffn_out=full["w_ffn_out"],
                dsw_q=di[:, :dq], dsw_k=di[:, dq:2 * dq], dsw_v=di[:, 2 * dq:], dsw_out=full["dsw_w_out"][0])


def _prepare_weights(full, small):
    return {**_prepare_first(full, small), **_prepare_rest(full)}


_W_NAMES = ("w_ada", "b_ada", "norm_mix", "norm_ffn", "w_ffn_in", "w_ffn_out", "gdn_w_in", "gdn_conv",
            "gdn_a_log", "gdn_dt_bias", "gdn_out_norm", "gdn_w_out", "dsw_w_in", "dsw_q_norm", "dsw_k_norm",
            "dsw_w_out", "rel_bias")
_PAD_BATCH = 16


def _pad_rows(a, rows):
    return jnp.pad(a, ((0, rows - a.shape[0]), (0, 0)))


def kernel(x, c, w_ada, b_ada, norm_mix, norm_ffn, w_ffn_in, w_ffn_out, gdn_w_in, gdn_conv, gdn_a_log, gdn_dt_bias, gdn_out_norm, gdn_w_out, dsw_w_in, dsw_q_norm, dsw_k_norm, dsw_w_out, rel_bias, loss_target, m_w_ada, m_b_ada, m_norm_mix, m_norm_ffn, m_w_ffn_in, m_w_ffn_out, m_gdn_w_in, m_gdn_conv, m_gdn_a_log, m_gdn_dt_bias, m_gdn_out_norm, m_gdn_w_out, m_dsw_w_in, m_dsw_q_norm, m_dsw_k_norm, m_dsw_w_out, m_rel_bias, v_w_ada, v_b_ada, v_norm_mix, v_norm_ffn, v_w_ffn_in, v_w_ffn_out, v_gdn_w_in, v_gdn_conv, v_gdn_a_log, v_gdn_dt_bias, v_gdn_out_norm, v_gdn_w_out, v_dsw_w_in, v_dsw_q_norm, v_dsw_k_norm, v_dsw_w_out, v_rel_bias):
    w = dict(zip(_W_NAMES, (w_ada, b_ada, norm_mix, norm_ffn, w_ffn_in, w_ffn_out, gdn_w_in, gdn_conv, gdn_a_log,
                            gdn_dt_bias, gdn_out_norm, gdn_w_out, dsw_w_in, dsw_q_norm, dsw_k_norm, dsw_w_out,
                            rel_bias)))
    m = dict(zip(_W_NAMES, (m_w_ada, m_b_ada, m_norm_mix, m_norm_ffn, m_w_ffn_in, m_w_ffn_out, m_gdn_w_in,
                            m_gdn_conv, m_gdn_a_log, m_gdn_dt_bias, m_gdn_out_norm, m_gdn_w_out, m_dsw_w_in,
                            m_dsw_q_norm, m_dsw_k_norm, m_dsw_w_out, m_rel_bias)))
    v = dict(zip(_W_NAMES, (v_w_ada, v_b_ada, v_norm_mix, v_norm_ffn, v_w_ffn_in, v_w_ffn_out, v_gdn_w_in,
                            v_gdn_conv, v_gdn_a_log, v_gdn_dt_bias, v_gdn_out_norm, v_gdn_w_out, v_dsw_w_in,
                            v_dsw_q_norm, v_dsw_k_norm, v_dsw_w_out, v_rel_bias)))
    D = x.shape[-1]
    n_layers, _, ada_cols = w_ada.shape

    c_all = _exchange(c.reshape(D // LANES, LANES), gather=True, name="gather_cond").reshape(N_DEV, D)
    c_pad = _pad_rows(c_all, _PAD_BATCH)
    proj = [_mm(c_pad, w_ada[l], mode="nn", name=f"ada_proj_{l}", tm=_PAD_BATCH, tn=ada_cols, tk=D, a_silu=True)
            for l in range(n_layers)]
    mod_send = _pack([(jnp.stack([p[:N_DEV] for p in proj], axis=1), 1)], _ROW_ALIGN)
    mod_recv = _exchange(mod_send, gather=False, name="scatter_mod")
    mod = _unpack(mod_recv, [(n_layers, ada_cols)], 1)[0]
    mod = jnp.transpose(mod, (1, 0, 2)).reshape(n_layers, N_DEV * ada_cols) + b_ada
    mod = mod.reshape(n_layers, 6, D)

    conv_hi = gdn_conv.astype(BF16)
    conv_lo = (gdn_conv - conv_hi.astype(F32)).astype(BF16)
    first_send = _pack([(conv_hi if n == "gdn_conv" else w[n].astype(BF16), 0) for n in _LATE] + [(conv_lo, 0)],
                       _ROW_ALIGN)
    parts = _unpack(_gather_two_level(first_send, name="gather_weights_first"),
                    [w[n].shape for n in _LATE] + [gdn_conv.shape], 1)
    full = {n: _to_natural(parts[i], _SHARD_AXIS[n]) for i, n in enumerate(_LATE)}
    full["gdn_conv"] = full["gdn_conv"].astype(F32) + _to_natural(parts[-1], _SHARD_AXIS["gdn_conv"]).astype(F32)
    W = _prepare_first(full, {n: w[n] for n in _SMALL})
    packed_early = tuple(n for n in _EARLY if n not in _NATIVE)
    rest_send = (_pack([(w[n].astype(BF16), 0) for n in packed_early], _ROW_ALIGN),
                 ) + tuple(w[n].astype(BF16) for n in _NATIVE)

    def rest_weights(arrived):
        filled = _fill_from_sibling(arrived, name="swap_weights")
        by_dev = [a.reshape((N_DEV,) + a.shape[2:]) for a in filled]
        blocks = dict(zip(packed_early, _unpack(by_dev[0], [w[n].shape for n in packed_early], 1)))
        blocks.update(zip(_NATIVE, by_dev[1:]))
        return _prepare_rest({n: _to_natural(blocks[n], _SHARD_AXIS[n]) for n in _EARLY})

    my_c = lax.axis_index("c")

    def pair_sums(g, packed, native, tag):
        sends = [_pack([(_to_blocked(g[n].astype(BF16), _SHARD_AXIS[n]), 1) for n in packed], _BIG_ALIGN)]
        sends += [_to_blocked(g[n].astype(BF16), _SHARD_AXIS[n]) for n in native]
        by_core = [s.reshape((N_DEV // 2, 2) + s.shape[1:]) for s in sends]
        keep = [lax.dynamic_index_in_dim(s, my_c, axis=1, keepdims=False) for s in by_core]
        give = [lax.dynamic_index_in_dim(s, 1 - my_c, axis=1, keepdims=False) for s in by_core]
        got = _swap_with_sibling(give, name=f"swap_grads_{tag}")
        return tuple(_add_pair(k, t, name=f"add_sibling_grads_{tag}_{j}") for j, (k, t) in enumerate(zip(keep, got)))

    sse, grad_x, grads, dmod, early_recv = _local_step(
        x[0], loss_target[0], mod, W, late_weights=(rest_send, rest_weights),
        early_pairs=lambda g: pair_sums(g, packed_early, _NATIVE, "early"))
    loss = lax.psum(0.5 * sse[0, 0] / D, ("x", "y", "c"))
    grads["b_ada"] = dmod.reshape(n_layers, 6 * D)
    late_recv = _exchange_chips(pair_sums(grads, _LATE, (), "late")[0], name="scatter_grads_late")
    g_parts = dict(zip(packed_early, _unpack(early_recv[0], [w[n].shape for n in packed_early], 1)))
    g_parts.update(zip(_NATIVE, early_recv[1:]))
    g_parts.update(zip(_LATE, _unpack(late_recv, [w[n].shape for n in _LATE], 1)))

    dmod_send = _pack([(jnp.transpose(dmod.reshape(n_layers, N_DEV, ada_cols), (1, 0, 2)), 1)], _ROW_ALIGN)
    small_send = _pack([(grads[n].reshape(w[n].shape), 0) for n in _SMALL], _ROW_ALIGN)
    s_recv = _exchange(jnp.concatenate(
        [dmod_send, jnp.broadcast_to(small_send[None], (N_DEV,) + small_send.shape)], axis=1),
        gather=False, name="scatter_small")
    dmod_rows = dmod_send.shape[1]

    out = {}
    kinds = ("grad", "delta", "new_m", "new_v")
    for n in _BIG:
        g4 = g_parts[n]
        rows2d = lambda a: a.reshape((-1, w[n].shape[-1]))
        res = _adamw(rows2d(w[n]), g4.reshape((g4.shape[0], -1, w[n].shape[-1])), rows2d(m[n]), rows2d(v[n]),
                     name=f"adamw_{n}")
        for kind, buf in zip(kinds, res):
            out[kind, n] = buf.reshape(w[n].shape)

    dmod_all = _unpack(lax.slice_in_dim(s_recv, 0, dmod_rows, axis=1), [(n_layers, ada_cols)], 1)[0]
    g_ada = jnp.stack([_mm(c_pad, _pad_rows(dmod_all[:, l], _PAD_BATCH), mode="tn", name=f"ada_dw_{l}",
                           tm=D, tn=ada_cols, tk=_PAD_BATCH, a_silu=True) for l in range(n_layers)])
    flat = lambda a: a.reshape(n_layers * D, ada_cols)
    res = _adamw(flat(w_ada), flat(g_ada)[None], flat(m_w_ada), flat(v_w_ada), name="adamw_ada")
    for kind, buf in zip(("grad", "delta", "new_m", "new_v"), res):
        out[kind, "w_ada"] = buf.reshape(w_ada.shape)

    small_parts = lax.slice_in_dim(s_recv, dmod_rows, s_recv.shape[1], axis=1)
    packed = [_pack([(t[n], 0) for n in _SMALL], _ROW_ALIGN) for t in (w, m, v)]
    res = _adamw(packed[0], small_parts, packed[1], packed[2], name="adamw_replicated")
    for kind, buf in zip(("grad", "delta", "new_m", "new_v"), res):
        for n, a in zip(_SMALL, _unpack(buf, [w[n].shape for n in _SMALL], 0)):
            out[kind, n] = a

    return (loss, grad_x[None]) + tuple(out[kind, n] for kind in ("grad", "delta", "new_m", "new_v")
                                        for n in _W_NAMES)
```

```python
import functools
import math

import numpy as np
import jax
import jax.numpy as jnp
from jax import lax
from jax.experimental import pallas as pl
from jax.experimental.pallas import tpu as pltpu

F32 = jnp.float32
BF16 = jnp.bfloat16

N_DEV = 8
RMS_EPS = 1e-6
LANES = 128
V7X_VMEM_LIMIT = 48 * 1024 * 1024

GDN_HEADS = 8
GDN_DK = 128
GDN_CHUNK = 64
GDN_CONV = 4
DSW_GROUPS = ((128, 1), (512, 4), (2048, 16))
DSW_HEADS = 8
DSW_DH = 64
DSW_BLK = 128
REL_BUCKETS = 32
REL_MAX_DIST = 2048

ADAM_LR = 0.001
ADAM_B1 = 0.9
ADAM_B2 = 0.999
ADAM_EPS = 1e-08
ADAM_WD = 0.01
ADAM_STEP = 10

NEG_BIG = -1e30


def _params(*sem):
    return pltpu.CompilerParams(dimension_semantics=sem, vmem_limit_bytes=V7X_VMEM_LIMIT)


def _sigmoid(x):
    return 1.0 / (1.0 + jnp.exp(-x))


def _silu(x):
    return x * _sigmoid(x)


_DOT_DIMS = {
    "nn": (((1,), (0,)), ((), ())),
    "nt": (((1,), (1,)), ((), ())),
    "tn": (((0,), (0,)), ((), ())),
}


def _mm(a, b, *, mode, name, tm, tn, tk, out_dtype=F32, a_scale=None, out_scale=None, resid=None, a_silu=False):
    if mode == "nn":
        (M, K), N = a.shape, b.shape[1]
    elif mode == "nt":
        (M, K), N = a.shape, b.shape[0]
    else:
        (K, M), N = a.shape, b.shape[1]
    tm, tn, tk = min(tm, M), min(tn, N), min(tk, K)
    assert M % tm == 0 and N % tn == 0 and K % tk == 0, (name, M, N, K, tm, tn, tk)
    nk = K // tk

    def body(*refs):
        refs = list(refs)
        a_ref, b_ref = refs.pop(0), refs.pop(0)
        as_ref = refs.pop(0) if a_scale is not None else None
        os_ref = refs.pop(0) if out_scale is not None else None
        r_ref = refs.pop(0) if resid is not None else None
        o_ref = refs.pop(0)
        acc_ref = refs.pop(0) if nk > 1 else None

        av = a_ref[...]
        if a_silu:
            av = _silu(av.astype(F32))
        if as_ref is not None:
            av = av.astype(F32) * as_ref[...]
        part = lax.dot_general(av.astype(BF16), b_ref[...].astype(BF16), _DOT_DIMS[mode],
                               preferred_element_type=F32)

        def finish(r):
            if os_ref is not None:
                r = r * os_ref[...]
            if r_ref is not None:
                r = r + r_ref[...].astype(F32)
            o_ref[...] = r.astype(out_dtype)

        if nk == 1:
            finish(part)
        else:
            k = pl.program_id(2)

            @pl.when(k == 0)
            def _():
                acc_ref[...] = part

            @pl.when(k > 0)
            def _():
                acc_ref[...] += part

            @pl.when(k == nk - 1)
            def _():
                finish(acc_ref[...])

    if mode == "nn":
        a_spec = pl.BlockSpec((tm, tk), lambda i, j, k: (i, k))
        b_spec = pl.BlockSpec((tk, tn), lambda i, j, k: (k, j))
        as_spec = pl.BlockSpec((1, tk), lambda i, j, k: (0, k))
    elif mode == "nt":
        a_spec = pl.BlockSpec((tm, tk), lambda i, j, k: (i, k))
        b_spec = pl.BlockSpec((tn, tk), lambda i, j, k: (j, k))
        as_spec = pl.BlockSpec((1, tk), lambda i, j, k: (0, k))
    else:
        a_spec = pl.BlockSpec((tk, tm), lambda i, j, k: (k, i))
        b_spec = pl.BlockSpec((tk, tn), lambda i, j, k: (k, j))
        as_spec = None
    in_specs, args = [a_spec, b_spec], [a, b]
    if a_scale is not None:
        in_specs.append(as_spec)
        args.append(a_scale)
    if out_scale is not None:
        in_specs.append(pl.BlockSpec((1, tn), lambda i, j, k: (0, j)))
        args.append(out_scale)
    if resid is not None:
        in_specs.append(pl.BlockSpec((tm, tn), lambda i, j, k: (i, j)))
        args.append(resid)
    return pl.pallas_call(
        body, name=name, grid=(M // tm, N // tn, nk),
        in_specs=in_specs, out_specs=pl.BlockSpec((tm, tn), lambda i, j, k: (i, j)),
        out_shape=jax.ShapeDtypeStruct((M, N), out_dtype),
        scratch_shapes=[pltpu.VMEM((tm, tn), F32)] if nk > 1 else [],
        compiler_params=_params("parallel", "parallel", "arbitrary"),
    )(*args)


_MM_ROWS = 1024


def _mm_sum_nt(pairs, *, name, tm=_MM_ROWS, tn=1024):
    M, N = pairs[0][0].shape[0], pairs[0][1].shape[0]
    tm, tn = _tile(M, tm), _tile(N, tn)
    spans, start = [], 0
    for a, b, tk, off in pairs:
        K = a.shape[1]
        assert a.shape[0] == M and b.shape[0] == N and K % tk == 0 and off % tk == 0, name
        spans.append((start, K // tk, tk, off // tk))
        start += K // tk
    total = start

    def body(*refs):
        o_ref, acc_ref = refs[-2:]
        k = pl.program_id(2)

        @pl.when(k == 0)
        def _():
            acc_ref[...] = jnp.zeros_like(acc_ref)

        for p, (s0, nk, _, _) in enumerate(spans):
            a_ref, b_ref = refs[2 * p], refs[2 * p + 1]

            @pl.when((k >= s0) & (k < s0 + nk))
            def _():
                acc_ref[...] += lax.dot_general(a_ref[...].astype(BF16), b_ref[...].astype(BF16), _DOT_DIMS["nt"],
                                                preferred_element_type=F32)

        @pl.when(k == total - 1)
        def _():
            o_ref[...] = acc_ref[...]

    def spec(rows, tk, s0, nk, koff, axis):
        def index(i, j, k):
            return ((i, j)[axis], jnp.clip(k - s0, 0, nk - 1) + koff)
        return pl.BlockSpec((rows, tk), index)

    in_specs, args = [], []
    for (a, b, _, _), (s0, nk, tk, koff) in zip(pairs, spans):
        in_specs += [spec(tm, tk, s0, nk, 0, 0), spec(tn, tk, s0, nk, koff, 1)]
        args += [a, b]
    return pl.pallas_call(
        body, name=name, grid=(M // tm, N // tn, total), in_specs=in_specs,
        out_specs=pl.BlockSpec((tm, tn), lambda i, j, k: (i, j)),
        out_shape=jax.ShapeDtypeStruct((M, N), F32), scratch_shapes=[pltpu.VMEM((tm, tn), F32)],
        compiler_params=_params("parallel", "parallel", "arbitrary"),
    )(*args)


def _norm_mod_fwd(x, gain, sc, sh, *, name):
    S, D = x.shape
    tr = min(512, S)

    def body(x_ref, g_ref, sc_ref, sh_ref, h_ref):
        xv = x_ref[...]
        r = lax.rsqrt(jnp.mean(xv * xv, axis=-1, keepdims=True) + RMS_EPS)
        h_ref[...] = ((xv * r) * g_ref[...] * (1.0 + sc_ref[...]) + sh_ref[...]).astype(BF16)

    row = pl.BlockSpec((tr, D), lambda i: (i, 0))
    vec = pl.BlockSpec((1, D), lambda i: (0, 0))
    return pl.pallas_call(
        body, name=name, grid=(S // tr,), in_specs=[row, vec, vec, vec], out_specs=row,
        out_shape=jax.ShapeDtypeStruct((S, D), BF16), compiler_params=_params("parallel"),
    )(x, gain, sc, sh)


def _norm_mod_bwd(dh, x, dx_res, gain, sc, *, name):
    S, D = x.shape
    tr = min(512, S)
    n_steps = S // tr

    def body(dh_ref, x_ref, dxr_ref, g_ref, sc_ref, dx_ref, dsh_ref, dsc_ref, dgain_ref, acc_sh, acc_a):
        i = pl.program_id(0)
        xv = x_ref[...]
        r = lax.rsqrt(jnp.mean(xv * xv, axis=-1, keepdims=True) + RMS_EPS)
        n = xv * r
        dhv = dh_ref[...].astype(F32)
        dn = dhv * (g_ref[...] * (1.0 + sc_ref[...]))
        dx_ref[...] = dxr_ref[...] + r * (dn - n * jnp.mean(dn * n, axis=-1, keepdims=True))
        p_sh = jnp.sum(dhv, axis=0, keepdims=True)
        p_a = jnp.sum(dhv * n, axis=0, keepdims=True)

        @pl.when(i == 0)
        def _():
            acc_sh[...] = p_sh
            acc_a[...] = p_a

        @pl.when(i > 0)
        def _():
            acc_sh[...] += p_sh
            acc_a[...] += p_a

        @pl.when(i == n_steps - 1)
        def _():
            dsh_ref[...] = acc_sh[...]
            dsc_ref[...] = acc_a[...] * g_ref[...]
            dgain_ref[...] = acc_a[...] * (1.0 + sc_ref[...])

    row = pl.BlockSpec((tr, D), lambda i: (i, 0))
    vec = pl.BlockSpec((1, D), lambda i: (0, 0))
    vshape = jax.ShapeDtypeStruct((1, D), F32)
    return pl.pallas_call(
        body, name=name, grid=(n_steps,), in_specs=[row, row, row, vec, vec],
        out_specs=[row, vec, vec, vec],
        out_shape=[jax.ShapeDtypeStruct((S, D), F32), vshape, vshape, vshape],
        scratch_shapes=[pltpu.VMEM((1, D), F32), pltpu.VMEM((1, D), F32)],
        compiler_params=_params("arbitrary"),
    )(dh, x, dx_res, gain, sc)


def _wout_grad(gmat, w, gate, *, name):
    K, D = w.shape
    tr = min(256, K)
    n_steps = K // tr

    def body(g_ref, w_ref, gate_ref, dw_ref, dgate_ref, acc):
        i = pl.program_id(0)
        gv = g_ref[...]
        dw_ref[...] = (gv * gate_ref[...]).astype(BF16)
        part = jnp.sum(gv * w_ref[...], axis=0, keepdims=True)

        @pl.when(i == 0)
        def _():
            acc[...] = part

        @pl.when(i > 0)
        def _():
            acc[...] += part

        @pl.when(i == n_steps - 1)
        def _():
            dgate_ref[...] = acc[...]

    row = pl.BlockSpec((tr, D), lambda i: (i, 0))
    vec = pl.BlockSpec((1, D), lambda i: (0, 0))
    return pl.pallas_call(
        body, name=name, grid=(n_steps,), in_specs=[row, row, vec], out_specs=[row, vec],
        out_shape=[jax.ShapeDtypeStruct((K, D), BF16), jax.ShapeDtypeStruct((1, D), F32)],
        scratch_shapes=[pltpu.VMEM((1, D), F32)], compiler_params=_params("arbitrary"),
    )(gmat, w, gate)


def _loss_head(y, target, *, name):
    S, D = y.shape
    tr = min(512, S)
    n_steps = S // tr

    def body(y_ref, t_ref, dy_ref, sse_ref, acc):
        i = pl.program_id(0)
        e = y_ref[...] - t_ref[...]
        dy_ref[...] = e * (1.0 / D)
        part = jnp.sum(e * e, axis=0, keepdims=True)

        @pl.when(i == 0)
        def _():
            acc[...] = part

        @pl.when(i > 0)
        def _():
            acc[...] += part

        @pl.when(i == n_steps - 1)
        def _():
            sse_ref[...] = jnp.sum(acc[...], axis=1, keepdims=True)

    row = pl.BlockSpec((tr, D), lambda i: (i, 0))
    return pl.pallas_call(
        body, name=name, grid=(n_steps,), in_specs=[row, row],
        out_specs=[row, pl.BlockSpec((1, 1), lambda i: (0, 0))],
        out_shape=[jax.ShapeDtypeStruct((S, D), F32), jax.ShapeDtypeStruct((1, 1), F32)],
        scratch_shapes=[pltpu.VMEM((1, D), F32)], compiler_params=_params("arbitrary"),
    )(y, target)


def _adamw(w, g_parts, m, v, *, name):
    R, C = w.shape
    P = g_parts.shape[0]
    tr = _tile(R, max(8, 1024 * LANES // C))
    c1 = 1.0 / (1.0 - ADAM_B1 ** ADAM_STEP)
    c2 = 1.0 / (1.0 - ADAM_B2 ** ADAM_STEP)

    def body(w_ref, g_ref, m_ref, v_ref, go_ref, d_ref, mo_ref, vo_ref):
        g = g_ref[0].astype(F32)
        for q in range(1, P):
            g = g + g_ref[q].astype(F32)
        mn = ADAM_B1 * m_ref[...] + (1.0 - ADAM_B1) * g
        vn = ADAM_B2 * v_ref[...] + (1.0 - ADAM_B2) * (g * g)
        go_ref[...] = g
        mo_ref[...] = mn
        vo_ref[...] = vn
        d_ref[...] = -ADAM_LR * ((mn * c1) / (jnp.sqrt(vn * c2) + ADAM_EPS) + ADAM_WD * w_ref[...])

    row = pl.BlockSpec((tr, C), lambda i: (i, 0))
    shp = jax.ShapeDtypeStruct((R, C), F32)
    return pl.pallas_call(
        body, name=name, grid=(R // tr,),
        in_specs=[row, pl.BlockSpec((P, tr, C), lambda i: (0, i, 0)), row, row],
        out_specs=[row, row, row, row], out_shape=[shp, shp, shp, shp],
        compiler_params=_params("parallel"),
    )(w, g_parts, m, v)


_HALO = 16


def _conv_taps(buf, w_ref, rows, cols):
    acc = None
    for j in range(GDN_CONV):
        term = buf[pl.ds(_HALO - (GDN_CONV - 1) + j, rows), cols] * w_ref[j:j + 1, cols]
        acc = term if acc is None else acc + term
    return acc


def _fill_conv_buf(buf, halo_ref, x_ref, rows, first):
    buf[0:_HALO, :] = jnp.where(first, 0.0, halo_ref[...].astype(F32))
    buf[_HALO:_HALO + rows, :] = x_ref[...].astype(F32)


_HM = 3 * GDN_DK
_GDN_ROWS = 256
_PREP_HEADS = 4


def _l2n(seg):
    return lax.rsqrt(jnp.sum(seg * seg, axis=-1, keepdims=True) + RMS_EPS)


def _head_cols(hh):
    return slice(hh * _HM, (hh + 1) * _HM)


def _gdn_prep_fwd(x, conv_w, *, name):
    S, C3 = x.shape
    CB = _PREP_HEADS * _HM
    RB = min(512, S)

    def body(x_ref, halo_ref, w_ref, o_ref, buf):
        i = pl.program_id(0)
        _fill_conv_buf(buf, halo_ref, x_ref, RB, i == 0)
        for hh in range(_PREP_HEADS):
            c0 = hh * _HM
            y = _silu(_conv_taps(buf, w_ref, RB, _head_cols(hh)))
            q, k = y[:, :GDN_DK], y[:, GDN_DK:2 * GDN_DK]
            o_ref[:, c0:c0 + GDN_DK] = q * (_l2n(q) * GDN_DK ** -0.5)
            o_ref[:, c0 + GDN_DK:c0 + 2 * GDN_DK] = k * _l2n(k)
            o_ref[:, c0 + 2 * GDN_DK:c0 + _HM] = y[:, 2 * GDN_DK:]

    hb = RB // _HALO
    return pl.pallas_call(
        body, name=name, grid=(S // RB, C3 // CB),
        in_specs=[pl.BlockSpec((RB, CB), lambda i, j: (i, j)),
                  pl.BlockSpec((_HALO, CB), lambda i, j: (jnp.maximum(i * hb - 1, 0), j)),
                  pl.BlockSpec((GDN_CONV, CB), lambda i, j: (0, j))],
        out_specs=pl.BlockSpec((RB, CB), lambda i, j: (i, j)),
        out_shape=jax.ShapeDtypeStruct((S, C3), F32),
        scratch_shapes=[pltpu.VMEM((RB + _HALO, CB), F32)],
        compiler_params=_params("parallel", "parallel"),
    )(x, x, conv_w)


def _gdn_prep_bwd_pre(dn, x, conv_w, *, name):
    S, C3 = x.shape
    CB = _PREP_HEADS * _HM
    RB = min(512, S)
    n_steps = S // RB

    def body(dn_ref, x_ref, halo_ref, w_ref, dc_ref, dw_ref, buf):
        i = pl.program_id(1)
        _fill_conv_buf(buf, halo_ref, x_ref, RB, i == 0)
        head_parts = []
        for hh in range(_PREP_HEADS):
            c0, cols = hh * _HM, _head_cols(hh)
            acc = _conv_taps(buf, w_ref, RB, cols)
            sg = _sigmoid(acc)
            y = acc * sg
            dsilu = sg * (1.0 + acc * (1.0 - sg))
            for part, scale in ((0, GDN_DK ** -0.5), (1, 1.0)):
                sl = slice(part * GDN_DK, (part + 1) * GDN_DK)
                seg = y[:, sl]
                r = _l2n(seg)
                n = seg * r
                d = dn_ref[:, c0 + part * GDN_DK:c0 + (part + 1) * GDN_DK] * scale
                dc_ref[:, c0 + part * GDN_DK:c0 + (part + 1) * GDN_DK] = (
                    r * (d - n * jnp.sum(d * n, axis=-1, keepdims=True)) * dsilu[:, sl])
            dc_ref[:, c0 + 2 * GDN_DK:c0 + _HM] = dn_ref[:, c0 + 2 * GDN_DK:c0 + _HM] * dsilu[:, 2 * GDN_DK:]
            dc = dc_ref[:, cols]
            taps = [jnp.sum(dc * buf[pl.ds(_HALO - (GDN_CONV - 1) + t, RB), cols], axis=0, keepdims=True)
                    for t in range(GDN_CONV)]
            head_parts.append(jnp.concatenate(taps + [jnp.zeros((8 - GDN_CONV, _HM), F32)], axis=0))
        part = jnp.concatenate(head_parts, axis=1)

        @pl.when(i == 0)
        def _():
            dw_ref[...] = part

        @pl.when(i > 0)
        def _():
            dw_ref[...] += part

    hb = RB // _HALO
    return pl.pallas_call(
        body, name=name, grid=(C3 // CB, n_steps),
        in_specs=[pl.BlockSpec((RB, CB), lambda j, i: (i, j)),
                  pl.BlockSpec((RB, CB), lambda j, i: (i, j)),
                  pl.BlockSpec((_HALO, CB), lambda j, i: (jnp.maximum(i * hb - 1, 0), j)),
                  pl.BlockSpec((GDN_CONV, CB), lambda j, i: (0, j))],
        out_specs=[pl.BlockSpec((RB, CB), lambda j, i: (i, j)),
                   pl.BlockSpec((8, CB), lambda j, i: (0, j))],
        out_shape=[jax.ShapeDtypeStruct((S, C3), F32), jax.ShapeDtypeStruct((8, C3), F32)],
        scratch_shapes=[pltpu.VMEM((RB + _HALO, CB), F32)],
        compiler_params=_params("parallel", "arbitrary"),
    )(dn, x, x, conv_w)


def _gdn_conv_bwd_x(dc, conv_w, *, name):
    S, C3 = dc.shape
    CB = _PREP_HEADS * _HM
    RB = min(512, S)
    n_steps = S // RB

    def body(dc_ref, halo_ref, w_ref, dx_ref, buf):
        i = pl.program_id(0)
        buf[0:RB, :] = dc_ref[...]
        buf[RB:RB + _HALO, :] = jnp.where(i == n_steps - 1, 0.0, halo_ref[...])
        for hh in range(_PREP_HEADS):
            cols = _head_cols(hh)
            acc = None
            for j in range(GDN_CONV):
                term = buf[pl.ds(GDN_CONV - 1 - j, RB), cols] * w_ref[j:j + 1, cols]
                acc = term if acc is None else acc + term
            dx_ref[:, cols] = acc.astype(BF16)

    hb = RB // _HALO
    last = S // _HALO - 1
    return pl.pallas_call(
        body, name=name, grid=(n_steps, C3 // CB),
        in_specs=[pl.BlockSpec((RB, CB), lambda i, j: (i, j)),
                  pl.BlockSpec((_HALO, CB), lambda i, j: (jnp.minimum((i + 1) * hb, last), j)),
                  pl.BlockSpec((GDN_CONV, CB), lambda i, j: (0, j))],
        out_specs=pl.BlockSpec((RB, CB), lambda i, j: (i, j)),
        out_shape=jax.ShapeDtypeStruct((S, C3), BF16),
        scratch_shapes=[pltpu.VMEM((RB + _HALO, CB), F32)],
        compiler_params=_params("parallel", "parallel"),
    )(dc, dc, conv_w)


def _split_bf16(a):
    hi = a.astype(BF16)
    return hi, (a - hi.astype(F32)).astype(BF16)


def _dot(a, b, dims="nn", exact=False):
    def dot(p, q):
        return lax.dot_general(p, q, _DOT_DIMS[dims], preferred_element_type=F32)

    if exact:
        (ah, al), (bh, bl) = _split_bf16(a), _split_bf16(b)
        return dot(ah, bh) + (dot(ah, bl) + dot(al, bh))
    return dot(a.astype(BF16), b.astype(BF16))


def _softplus(x):
    return jnp.maximum(x, 0.0) + jnp.log(1.0 + jnp.exp(-jnp.abs(x)))


def _to_col(row, eye):
    return jnp.sum(jnp.where(eye, row, 0.0), axis=1, keepdims=True)


def _to_row(col, eye):
    return jnp.sum(jnp.where(eye, col, 0.0), axis=0, keepdims=True)


def _unit_lower_inverse(low, ri, ci):
    n = range(len(low))
    C = low[0].shape[0]
    eye = jnp.where(ri == ci, 1.0, 0.0)
    pair = (ri >> 1) == (ci >> 1)
    x = [eye - jnp.where(pair, low[j], 0.0) for j in n]
    m, sh = 2, 1
    while m < C:
        join = ((ri >> (sh + 1)) == (ci >> (sh + 1))) & (((ri >> sh) & 1) == 1) & (((ci >> sh) & 1) == 0)
        y = [_dot(x[j], jnp.where(join, low[j], 0.0)) for j in n]
        x = [x[j] - _dot(y[j], x[j]) for j in n]
        m, sh = 2 * m, sh + 1
    lx = [_dot(low[j], x[j], exact=True) for j in n]
    corr = [_dot(x[j], eye - x[j] - lx[j]) for j in n]
    return [x[j] + corr[j] for j in n]


def _gdn_local_batch(qkv, g_row, beta_row, ri, ci):
    n = range(len(qkv))
    eye, tril, strict = ri == ci, ri >= ci, ri > ci
    q = [qkv[j][:, :GDN_DK] for j in n]
    k = [qkv[j][:, GDN_DK:2 * GDN_DK] for j in n]
    v = [qkv[j][:, 2 * GDN_DK:] for j in n]
    g_col = [_to_col(g_row[j], eye) for j in n]
    beta_col = [_to_col(beta_row[j], eye) for j in n]
    gc_col = [jnp.sum(jnp.where(tril, g_row[j], 0.0), axis=1, keepdims=True) for j in n]
    gc_row = [jnp.sum(jnp.where(ri <= ci, g_col[j], 0.0), axis=0, keepdims=True) for j in n]
    g_last = [jnp.sum(g_row[j], axis=1, keepdims=True) for j in n]
    decay = [jnp.where(tril, jnp.exp(jnp.minimum(gc_col[j] - gc_row[j], 0.0)), 0.0) for j in n]
    e_col = [jnp.exp(gc_col[j]) for j in n]
    f_col = [jnp.exp(g_last[j] - gc_col[j]) for j in n]
    e_last = [jnp.exp(g_last[j]) for j in n]
    kb = [k[j] * beta_col[j] for j in n]
    vb = [v[j] * beta_col[j] for j in n]
    kk = [_dot(kb[j], k[j], "nt") for j in n]
    qk = [_dot(q[j], k[j], "nt") for j in n]
    low = [jnp.where(strict, kk[j] * decay[j], 0.0) for j in n]
    att = [qk[j] * decay[j] for j in n]
    return dict(q=q, k=k, v=v, beta_col=beta_col, decay=decay, e_col=e_col, f_col=f_col, e_last=e_last,
                kb=kb, vb=vb, low=low, att=att, eye=eye, strict=strict, tril=tril)


def _chunk_iotas():
    C = GDN_CHUNK
    return lax.broadcasted_iota(jnp.int32, (C, C), 0), lax.broadcasted_iota(jnp.int32, (C, C), 1)


def _gdn_chunk_fwd(qkv, ab, a_log, dt_bias, *, name, riding=None):
    S = qkv.shape[0]
    H, C, DK = GDN_HEADS, GDN_CHUNK, GDN_DK
    RB = min(_GDN_ROWS, S)
    NCB, NB, NC = RB // C, S // RB, S // C
    heads = range(H)

    def body(qkv_ref, ab_ref, alog_ref, dtb_ref, *rest):
        n_ride = 0 if riding is None else len(riding)
        ride_srcs, rest = rest[:n_ride], rest[n_ride:]
        (o_ref, st_ref, t_ref), rest = rest[:3], rest[3:]
        ride_dsts, rest = rest[:n_ride], rest[n_ride:]
        state, u_s, w_s, qe_s, kf_s, att_s, *ride_sems = rest
        nb = pl.program_id(0)
        if riding is not None:
            finish_ride = _ride(nb == 0, nb == NB - 1, ride_srcs, ride_dsts, ride_sems, True)

        @pl.when(nb == 0)
        def _():
            state[...] = jnp.zeros_like(state)

        ri, ci = _chunk_iotas()
        neg_a = [-jnp.exp(alog_ref[h]) for h in heads]
        e_last = []
        for c in range(NCB):
            rows = pl.ds(c * C, C)
            g_row = [neg_a[h] * _softplus(ab_ref[h, c] + dtb_ref[h]) for h in heads]
            beta_row = [_sigmoid(ab_ref[H + h, c]) for h in heads]
            L = _gdn_local_batch([qkv_ref[rows, h * _HM:(h + 1) * _HM] for h in heads], g_row, beta_row, ri, ci)
            tinv = _unit_lower_inverse(L["low"], ri, ci)
            u = [_dot(tinv[h], L["vb"][h], exact=True) for h in heads]
            w = [_dot(tinv[h], L["kb"][h] * L["e_col"][h], exact=True) for h in heads]
            for h in heads:
                t_ref[h, c] = tinv[h]
                u_s[c, h] = u[h]
                w_s[c, h] = w[h].astype(BF16)
                qe_s[c, h] = (L["q"][h] * L["e_col"][h]).astype(BF16)
                kf_s[c, h] = (L["k"][h] * L["f_col"][h]).astype(BF16)
                att_s[c, h] = L["att"][h].astype(BF16)
            e_last.append(L["e_last"])
        st = [state[h] for h in heads]
        for c in range(NCB):
            rows = pl.ds(c * C, C)
            stb = [st[h].astype(BF16) for h in heads]
            vn = [u_s[c, h] - _dot(w_s[c, h], stb[h]) for h in heads]
            vnb = [vn[h].astype(BF16) for h in heads]
            out = [_dot(qe_s[c, h], stb[h]) + _dot(att_s[c, h], vnb[h]) for h in heads]
            new = [st[h] * e_last[c][h] + _dot(kf_s[c, h], vnb[h], "tn") for h in heads]
            for h in heads:
                o_ref[rows, h * DK:(h + 1) * DK] = out[h]
                st_ref[h, c] = st[h]
            st = new
        for h in heads:
            state[h] = st[h]
        if riding is not None:
            finish_ride()

    ride_args, ride_specs, ride_out, ride_scratch = _riding(riding, True)
    return pl.pallas_call(
        body, name=name, grid=(NB,),
        in_specs=[pl.BlockSpec((RB, H * _HM), lambda n: (n, 0)),
                  pl.BlockSpec((2 * H, NCB, 1, C), lambda n: (0, n, 0, 0)),
                  pl.BlockSpec((H, 1, 1), lambda n: (0, 0, 0)),
                  pl.BlockSpec((H, 1, 1), lambda n: (0, 0, 0))] + ride_specs,
        out_specs=[pl.BlockSpec((RB, H * DK), lambda n: (n, 0)),
                   pl.BlockSpec((H, NCB, DK, DK), lambda n: (0, n, 0, 0)),
                   pl.BlockSpec((H, NCB, C, C), lambda n: (0, n, 0, 0))] + ride_specs,
        out_shape=[jax.ShapeDtypeStruct((S, H * DK), F32),
                   jax.ShapeDtypeStruct((H, NC, DK, DK), F32),
                   jax.ShapeDtypeStruct((H, NC, C, C), F32)] + ride_out,
        scratch_shapes=[pltpu.VMEM((H, DK, DK), F32), pltpu.VMEM((NCB, H, C, DK), F32),
                        pltpu.VMEM((NCB, H, C, DK), BF16), pltpu.VMEM((NCB, H, C, DK), BF16),
                        pltpu.VMEM((NCB, H, C, DK), BF16), pltpu.VMEM((NCB, H, C, C), BF16)] + ride_scratch,
        compiler_params=_params("arbitrary"),
    )(qkv, ab, a_log, dt_bias, *ride_args)


_CHIP_PEERS = N_DEV // 2 - 1


def _chip_copies(src_refs, dst_refs, send_sems, recv_sems, local_sems, gather=False):
    x, y, c = lax.axis_index("x"), lax.axis_index("y"), lax.axis_index("c")
    here = 2 * x + y
    copies = []
    for a, (src_ref, dst_ref) in enumerate(zip(src_refs, dst_refs)):
        landing = dst_ref.at[here, c] if gather else dst_ref.at[here]
        copies.append(pltpu.make_async_copy(src_ref if gather else src_ref.at[here], landing, local_sems.at[a]))
        for rel in range(1, N_DEV // 2):
            px = 1 - x if rel & 2 else x
            py = 1 - y if rel & 1 else y
            k = a * _CHIP_PEERS + rel - 1
            copies.append(pltpu.make_async_remote_copy(
                src_ref=src_ref if gather else src_ref.at[2 * px + py], dst_ref=landing,
                send_sem=send_sems.at[k], recv_sem=recv_sems.at[k],
                device_id=(px, py, c), device_id_type=pl.DeviceIdType.MESH))
    return copies


def _chip_sems(n):
    return [pltpu.SemaphoreType.DMA((n * _CHIP_PEERS,)), pltpu.SemaphoreType.DMA((n * _CHIP_PEERS,)),
            pltpu.SemaphoreType.DMA((n,))]


def _riding(riding, gather):
    if riding is None:
        return [], [], [], []
    shapes = [jax.ShapeDtypeStruct(((N_DEV // 2, 2) + r.shape) if gather else r.shape, r.dtype) for r in riding]
    return list(riding), [pl.BlockSpec(memory_space=pl.ANY)] * len(riding), shapes, _chip_sems(len(riding))


def _ride(first, last, srcs, dsts, sems, gather):
    @pl.when(first)
    def _():
        for cp in _chip_copies(srcs, dsts, *sems, gather=gather):
            cp.start()

    def finish():
        @pl.when(last)
        def _():
            for cp in _chip_copies(srcs, dsts, *sems, gather=gather):
                cp.wait()

    return finish


def _gdn_chunk_bwd(qkv, ab, a_log, dt_bias, states, tinvs, do, *, name, riding=None):
    S = qkv.shape[0]
    H, C, DK = GDN_HEADS, GDN_CHUNK, GDN_DK
    RB = min(_GDN_ROWS, S)
    NCB, NB, NC = RB // C, S // RB, S // C
    heads = range(H)

    def body(qkv_ref, ab_ref, alog_ref, dtb_ref, st_ref, t_ref, do_ref, *rest):
        n_ride = 0 if riding is None else len(riding)
        ride_srcs, rest = rest[:n_ride], rest[n_ride:]
        (dqkv_ref, dab_ref, dalog_ref, ddtb_ref), rest = rest[:4], rest[4:]
        ride_dsts, rest = rest[:n_ride], rest[n_ride:]
        dstate, w_s, vn_s, qe_s, kf_s, att_s, dvn_s, dkf_s, *ride_sems = rest
        nb = pl.program_id(0)
        if riding is not None:
            finish_ride = _ride(nb == 0, nb == NB - 1, ride_srcs, ride_dsts, ride_sems, False)

        @pl.when(nb == 0)
        def _():
            dstate[...] = jnp.zeros_like(dstate)
            dalog_ref[...] = jnp.zeros_like(dalog_ref)
            ddtb_ref[...] = jnp.zeros_like(ddtb_ref)

        ri, ci = _chunk_iotas()
        neg_a = [-jnp.exp(alog_ref[h]) for h in heads]

        def local(c):
            rows = pl.ds(c * C, C)
            a_pre = [ab_ref[h, c] + dtb_ref[h] for h in heads]
            g_row = [neg_a[h] * _softplus(a_pre[h]) for h in heads]
            beta_row = [_sigmoid(ab_ref[H + h, c]) for h in heads]
            L = _gdn_local_batch([qkv_ref[rows, h * _HM:(h + 1) * _HM] for h in heads], g_row, beta_row, ri, ci)
            return L, a_pre, g_row, beta_row

        e_last = [None] * NCB
        for c in range(NCB):
            L, _, _, _ = local(c)
            kbe = [L["kb"][h] * L["e_col"][h] for h in heads]
            u = [_dot(t_ref[h, c], L["vb"][h], exact=True) for h in heads]
            w = [_dot(t_ref[h, c], kbe[h], exact=True) for h in heads]
            vn = [u[h] - _dot(w[h], st_ref[h, c]) for h in heads]
            for h in heads:
                w_s[c, h] = w[h].astype(BF16)
                vn_s[c, h] = vn[h].astype(BF16)
                qe_s[c, h] = (L["q"][h] * L["e_col"][h]).astype(BF16)
                kf_s[c, h] = (L["k"][h] * L["f_col"][h]).astype(BF16)
                att_s[c, h] = L["att"][h].astype(BF16)
            e_last[c] = L["e_last"]

        dst = [dstate[h] for h in heads]
        de_last = [None] * NCB
        for c in reversed(range(NCB)):
            rows = pl.ds(c * C, C)
            dob = [do_ref[rows, h * DK:(h + 1) * DK].astype(BF16) for h in heads]
            dstb = [dst[h].astype(BF16) for h in heads]
            dvn = [_dot(att_s[c, h], dob[h], "tn") + _dot(kf_s[c, h], dstb[h]) for h in heads]
            dkf = [_dot(vn_s[c, h], dstb[h], "nt") for h in heads]
            de_last[c] = [jnp.sum(jnp.sum(dst[h] * st_ref[h, c], axis=1, keepdims=True), axis=0, keepdims=True)
                          for h in heads]
            new = [dst[h] * e_last[c][h] + _dot(qe_s[c, h], dob[h], "tn")
                   - _dot(w_s[c, h], dvn[h].astype(BF16), "tn") for h in heads]
            for h in heads:
                dvn_s[c, h] = dvn[h]
                dkf_s[c, h] = dkf[h]
            dst = new
        for h in heads:
            dstate[h] = dst[h]

        for c in range(NCB):
            rows = pl.ds(c * C, C)
            L, a_pre, g_row, beta_row = local(c)
            q, k, v, kb, vb = L["q"], L["k"], L["v"], L["kb"], L["vb"]
            e_col, f_col, decay, beta_col = L["e_col"], L["f_col"], L["decay"], L["beta_col"]
            eye, strict, tril = L["eye"], L["strict"], L["tril"]
            tinv = [t_ref[h, c] for h in heads]
            stb = [st_ref[h, c].astype(BF16) for h in heads]
            dov = [do_ref[rows, h * DK:(h + 1) * DK] for h in heads]
            dvn = [dvn_s[c, h] for h in heads]
            dkf = [dkf_s[c, h] for h in heads]
            kbe = [kb[h] * e_col[h] for h in heads]
            datt = [jnp.where(tril, _dot(dov[h], vn_s[c, h], "nt"), 0.0) for h in heads]
            dqe = [_dot(dov[h], stb[h], "nt") for h in heads]
            dw = [-_dot(dvn[h], stb[h], "nt") for h in heads]
            dt = [_dot(dvn[h], vb[h], "nt") + _dot(dw[h], kbe[h], "nt") for h in heads]
            dvb = [_dot(tinv[h], dvn[h], "tn", exact=True) for h in heads]
            dkbe = [_dot(tinv[h], dw[h], "tn", exact=True) for h in heads]
            tdt = [_dot(tinv[h], dt[h], "tn", exact=True) for h in heads]
            dlow = [-jnp.where(strict, _dot(tdt[h], tinv[h], "nt", exact=True), 0.0) for h in heads]
            dkk = [dlow[h] * decay[h] for h in heads]
            dqk = [datt[h] * decay[h] for h in heads]
            dkb = [_dot(dkk[h], k[h]) + dkbe[h] * e_col[h] for h in heads]
            dk = [_dot(dkk[h], kb[h], "tn") + _dot(dqk[h], q[h], "tn") + dkf[h] * f_col[h] + dkb[h] * beta_col[h]
                  for h in heads]
            dq = [_dot(dqk[h], k[h]) + dqe[h] * e_col[h] for h in heads]
            for h in heads:
                dqkv_ref[rows, h * _HM:h * _HM + DK] = dq[h]
                dqkv_ref[rows, h * _HM + DK:h * _HM + 2 * DK] = dk[h]
                dqkv_ref[rows, h * _HM + 2 * DK:(h + 1) * _HM] = dvb[h] * beta_col[h]

            dbeta_col = [jnp.sum(k[h] * dkb[h] + v[h] * dvb[h], axis=1, keepdims=True) for h in heads]
            pmat = [dlow[h] * L["low"][h] + datt[h] * L["att"][h] for h in heads]
            df_col = [jnp.sum(k[h] * dkf[h], axis=1, keepdims=True) * f_col[h] for h in heads]
            dgc_col = [jnp.sum(pmat[h], axis=1, keepdims=True)
                       + jnp.sum(q[h] * dqe[h] + kb[h] * dkbe[h], axis=1, keepdims=True) * e_col[h] - df_col[h]
                       for h in heads]
            dgc_row = [_to_row(dgc_col[h], eye) - jnp.sum(pmat[h], axis=0, keepdims=True) for h in heads]
            dg_last = [jnp.sum(df_col[h], axis=0, keepdims=True) + de_last[c][h] * L["e_last"][h] for h in heads]
            dgc_c = [_to_col(dgc_row[h], eye) for h in heads]
            dg_row = [jnp.sum(jnp.where(ri >= ci, dgc_c[h], 0.0), axis=0, keepdims=True) + dg_last[h] for h in heads]
            dbeta_row = [_to_row(dbeta_col[h], eye) for h in heads]
            for h in heads:
                da_row = dg_row[h] * neg_a[h] * _sigmoid(a_pre[h])
                dab_ref[h, c] = da_row
                dab_ref[H + h, c] = dbeta_row[h] * beta_row[h] * (1.0 - beta_row[h])
                dalog_ref[h] += jnp.sum(dg_row[h] * g_row[h], axis=1, keepdims=True)
                ddtb_ref[h] += jnp.sum(da_row, axis=1, keepdims=True)

        if riding is not None:
            finish_ride()

    rev = lambda n: NB - 1 - n
    vec = pl.BlockSpec((H, 1, 1), lambda n: (0, 0, 0))
    gates = pl.BlockSpec((2 * H, NCB, 1, C), lambda n: (0, rev(n), 0, 0))
    wide = pl.BlockSpec((RB, H * _HM), lambda n: (rev(n), 0))
    item = lambda dt: pltpu.VMEM((NCB, H, C, DK), dt)
    ride_args, ride_specs, ride_out, ride_scratch = _riding(riding, False)
    return pl.pallas_call(
        body, name=name, grid=(NB,),
        in_specs=[wide, gates, vec, vec,
                  pl.BlockSpec((H, NCB, DK, DK), lambda n: (0, rev(n), 0, 0)),
                  pl.BlockSpec((H, NCB, C, C), lambda n: (0, rev(n), 0, 0)),
                  pl.BlockSpec((RB, H * DK), lambda n: (rev(n), 0))] + ride_specs,
        out_specs=[wide, gates, vec, vec] + ride_specs,
        out_shape=[jax.ShapeDtypeStruct((S, H * _HM), F32),
                   jax.ShapeDtypeStruct((2 * H, NC, 1, C), F32),
                   jax.ShapeDtypeStruct((H, 1, 1), F32),
                   jax.ShapeDtypeStruct((H, 1, 1), F32)] + ride_out,
        scratch_shapes=[pltpu.VMEM((H, DK, DK), F32), item(BF16), item(BF16), item(BF16), item(BF16),
                        pltpu.VMEM((NCB, H, C, C), BF16), item(F32), item(F32)] + ride_scratch,
        compiler_params=_params("arbitrary"),
    )(qkv, ab, a_log, dt_bias, states, tinvs, do, *ride_args)


def _gdn_outnorm_fwd(o, z, gain, *, name):
    S, HV = o.shape
    RB = min(512, S)

    def body(o_ref, z_ref, g_ref, y_ref):
        for h in range(HV // GDN_DK):
            cols = slice(h * GDN_DK, (h + 1) * GDN_DK)
            ov = o_ref[:, cols]
            r = lax.rsqrt(jnp.mean(ov * ov, axis=-1, keepdims=True) + RMS_EPS)
            y_ref[:, cols] = (ov * r * g_ref[...] * _silu(z_ref[:, cols].astype(F32))).astype(BF16)

    blk = pl.BlockSpec((RB, HV), lambda i: (i, 0))
    return pl.pallas_call(
        body, name=name, grid=(S // RB,),
        in_specs=[blk, blk, pl.BlockSpec((1, GDN_DK), lambda i: (0, 0))], out_specs=blk,
        out_shape=jax.ShapeDtypeStruct((S, HV), BF16), compiler_params=_params("parallel"),
    )(o, z, gain)


def _gdn_outnorm_bwd(dy, o, z, gain, *, name):
    S, HV = o.shape
    RB = min(512, S)

    def body(dy_ref, o_ref, z_ref, g_ref, do_ref, dz_ref, dg_ref):
        part = None
        for h in range(HV // GDN_DK):
            cols = slice(h * GDN_DK, (h + 1) * GDN_DK)
            ov = o_ref[:, cols]
            zv = z_ref[:, cols].astype(F32)
            dyv = dy_ref[:, cols].astype(F32)
            r = lax.rsqrt(jnp.mean(ov * ov, axis=-1, keepdims=True) + RMS_EPS)
            n = ov * r
            sg = _sigmoid(zv)
            dng = dyv * (zv * sg)
            dn = dng * g_ref[...]
            do_ref[:, cols] = r * (dn - n * jnp.mean(dn * n, axis=-1, keepdims=True))
            dz_ref[:, cols] = (dyv * (n * g_ref[...]) * (sg * (1.0 + zv * (1.0 - sg)))).astype(BF16)
            p = jnp.sum(dng * n, axis=0, keepdims=True)
            part = p if part is None else part + p

        @pl.when(pl.program_id(0) == 0)
        def _():
            dg_ref[...] = part

        @pl.when(pl.program_id(0) > 0)
        def _():
            dg_ref[...] += part

    blk = pl.BlockSpec((RB, HV), lambda i: (i, 0))
    vec = pl.BlockSpec((1, GDN_DK), lambda i: (0, 0))
    return pl.pallas_call(
        body, name=name, grid=(S // RB,),
        in_specs=[blk, blk, blk, vec], out_specs=[blk, blk, vec],
        out_shape=[jax.ShapeDtypeStruct((S, HV), F32), jax.ShapeDtypeStruct((S, HV), BF16),
                   jax.ShapeDtypeStruct((1, GDN_DK), F32)],
        compiler_params=_params("arbitrary"),
    )(dy, o, z, gain)


def _head_mask():
    return lax.broadcasted_iota(jnp.int32, (DSW_BLK, LANES), 1) < DSW_DH


def _per_head_sum(t, first):
    s0 = jnp.sum(jnp.where(first, t, 0.0), axis=-1, keepdims=True)
    s1 = jnp.sum(jnp.where(first, 0.0, t), axis=-1, keepdims=True)
    return jnp.where(first, s0, s1)


def _rms2(x, gain, first):
    r = lax.rsqrt(_per_head_sum(x * x, first) * (1.0 / DSW_DH) + RMS_EPS)
    xh = x * r
    return xh, r, xh * gain


def _rms2_bwd(dy, xh, r, gain, first):
    dxh = dy * gain
    return r * (dxh - xh * (_per_head_sum(dxh * xh, first) * (1.0 / DSW_DH)))


def _split_heads(x, first):
    return [jnp.where(first, x, 0.0).astype(BF16), jnp.where(first, 0.0, x).astype(BF16)]


_HP = LANES // DSW_DH
_DSW_W = DSW_HEADS * DSW_DH
_DSW_ROWS = 1024
_DSW_BATCH = 8


def _dsw_geometry(S, g):
    d = DSW_GROUPS[g][1]
    slab = DSW_BLK * d
    tb = max(1, min(_DSW_ROWS, S) // slab)
    return d, slab, tb, S // (tb * slab)


def _block_rows(t, r, slab, d):
    return pl.ds(t * slab + r, DSW_BLK) if d == 1 else pl.ds(t * slab + r, DSW_BLK, stride=d)


def _dsw_attn_fwd(q, k, v, bias, q_gain, k_gain, prev_out, *, g, name):
    S, WT = q.shape
    B = DSW_BLK
    d, slab, tb, n_tiles = _dsw_geometry(S, g)
    rt = tb * slab
    cb = g * (_DSW_W // LANES)
    batch_res = max(1, _DSW_BATCH // tb)

    def body(q_ref, kp_ref, kc_ref, vp_ref, vc_ref, bias_ref, qg_ref, kg_ref, *rest):
        o_ref, lse_ref = rest[-2:]
        i = pl.program_id(1)
        qg, kg = qg_ref[...] * DSW_DH ** -0.5, kg_ref[...]
        col = lax.broadcasted_iota(jnp.int32, (B, 2 * B), 1)
        first = _head_mask()
        heads = range(_HP)
        for r0 in range(0, d, batch_res):
            res = range(r0, min(d, r0 + batch_res))
            k_raw = {(r, -1): kp_ref[_block_rows(0, r, slab, d), :] for r in res}
            v_raw = {(r, -1): vp_ref[_block_rows(0, r, slab, d), :] for r in res}
            q_raw = {}
            for r in res:
                for t in range(tb):
                    rows = _block_rows(t, r, slab, d)
                    q_raw[r, t], k_raw[r, t], v_raw[r, t] = q_ref[rows, :], kc_ref[rows, :], vc_ref[rows, :]
            kn = {key: _rms2(x, kg, first)[2].astype(BF16) for key, x in k_raw.items()}
            vb = {key: x.astype(BF16) for key, x in v_raw.items()}
            qn = {key: _split_heads(_rms2(x, qg, first)[2], first) for key, x in q_raw.items()}
            items = [(r, t, h) for r in res for t in range(tb) for h in heads]
            s = {}
            for r, t, h in items:
                sv = _dot(qn[r, t][h], jnp.concatenate([kn[r, t - 1], kn[r, t]], axis=0), "nt") + bias_ref[h]
                s[r, t, h] = jnp.where((i == 0) & (col < B), NEG_BIG, sv) if t == 0 else sv
            m = {it: jnp.max(s[it], axis=-1, keepdims=True) for it in items}
            p = {it: jnp.exp(s[it] - m[it]) for it in items}
            l = {it: jnp.sum(p[it], axis=-1, keepdims=True) for it in items}
            o = {(r, t, h): _dot(p[r, t, h], jnp.concatenate([vb[r, t - 1], vb[r, t]], axis=0)) for r, t, h in items}
            for r in res:
                for t in range(tb):
                    rows = _block_rows(t, r, slab, d)
                    o_ref[rows, :] = jnp.where(first, o[r, t, 0] / l[r, t, 0], o[r, t, 1] / l[r, t, 1])
                    lse_ref[rows, :] = jnp.where(first, m[r, t, 0] + jnp.log(l[r, t, 0]),
                                                 m[r, t, 1] + jnp.log(l[r, t, 1]))

    cur = pl.BlockSpec((rt, LANES), lambda hp, i: (i, cb + hp))
    prev = pl.BlockSpec((slab, LANES), lambda hp, i: (jnp.maximum(i * tb - 1, 0), cb + hp))
    vec = pl.BlockSpec((1, LANES), lambda hp, i: (0, 0))
    shp = jax.ShapeDtypeStruct((S, WT), F32)
    carried = [] if prev_out is None else list(prev_out)
    n_in = 8
    return pl.pallas_call(
        body, name=name, grid=(_DSW_W // LANES, n_tiles),
        in_specs=[cur, prev, cur, prev, cur, pl.BlockSpec((_HP, B, 2 * B), lambda hp, i: (hp, 0, 0)), vec, vec]
                 + [pl.BlockSpec(memory_space=pl.ANY)] * len(carried),
        out_specs=[cur, cur], out_shape=[shp, shp],
        input_output_aliases={n_in + j: j for j in range(len(carried))},
        compiler_params=_params("parallel", "parallel"),
    )(q, k, k, v, v, bias, jnp.tile(q_gain, (1, _HP)), jnp.tile(k_gain, (1, _HP)), *carried)


def _dsw_merge(o_g, lse_g, *, name):
    S = o_g.shape[0]
    W, G = _DSW_W, len(DSW_GROUPS)
    tr = min(512, S)

    def body(o_ref, l_ref, out_ref, lse_ref):
        ls = [l_ref[:, g * W:(g + 1) * W] for g in range(G)]
        m = ls[0]
        for g in range(1, G):
            m = jnp.maximum(m, ls[g])
        den = jnp.zeros_like(m)
        acc = jnp.zeros_like(m)
        for g in range(G):
            wg = jnp.exp(ls[g] - m)
            den = den + wg
            acc = acc + wg * o_ref[:, g * W:(g + 1) * W]
        out_ref[...] = acc / den
        lse_ref[...] = m + jnp.log(den)

    wide = pl.BlockSpec((tr, G * W), lambda i: (i, 0))
    blk = pl.BlockSpec((tr, W), lambda i: (i, 0))
    shp = jax.ShapeDtypeStruct((S, W), F32)
    return pl.pallas_call(
        body, name=name, grid=(S // tr,), in_specs=[wide, wide], out_specs=[blk, blk],
        out_shape=[shp, shp], compiler_params=_params("parallel"),
    )(o_g, lse_g)


def _dsw_attn_bwd(q, k, v, o, lse, do, bias, q_gain, k_gain, prev_out, *, g, name):
    S, WT = q.shape
    B = DSW_BLK
    d, slab, tb, n_tiles = _dsw_geometry(S, g)
    rt = tb * slab
    cb = g * (_DSW_W // LANES)
    n_slabs = S // slab
    scale = DSW_DH ** -0.5
    batch_res = max(1, _DSW_BATCH // tb)

    def body(q_ref, qx_ref, kp_ref, kc_ref, vp_ref, vc_ref, o_ref, ox_ref, l_ref, lx_ref, do_ref, dox_ref,
             bias_ref, qg_ref, kg_ref, *rest):
        dq_ref, dk_ref, dv_ref, db_ref, dqg_ref, dkg_ref = rest[-6:]
        hp, i = pl.program_id(0), pl.program_id(1)
        qg, kg = qg_ref[...] * scale, kg_ref[...]
        col = lax.broadcasted_iota(jnp.int32, (B, 2 * B), 1)
        has_next = i < n_tiles - 1

        @pl.when(i == 0)
        def _():
            db_ref[...] = jnp.zeros_like(db_ref)

        dqg_acc = jnp.zeros((1, LANES), F32)
        dkg_acc = jnp.zeros((1, LANES), F32)
        first = _head_mask()
        heads = range(_HP)
        for r0 in range(0, d, batch_res):
            res = range(r0, min(d, r0 + batch_res))
            q_raw, k_raw, v_raw, o_raw, l_raw, do_raw = {}, {}, {}, {}, {}, {}
            for r in res:
                first_rows = _block_rows(0, r, slab, d)
                k_raw[r, -1], v_raw[r, -1] = kp_ref[first_rows, :], vp_ref[first_rows, :]
                for t in range(tb):
                    rows = _block_rows(t, r, slab, d)
                    q_raw[r, t], o_raw[r, t], l_raw[r, t], do_raw[r, t] = (
                        q_ref[rows, :], o_ref[rows, :], l_ref[rows, :], do_ref[rows, :])
                    k_raw[r, t], v_raw[r, t] = kc_ref[rows, :], vc_ref[rows, :]
                q_raw[r, tb], o_raw[r, tb], l_raw[r, tb], do_raw[r, tb] = (
                    qx_ref[first_rows, :], ox_ref[first_rows, :], lx_ref[first_rows, :], dox_ref[first_rows, :])
            kk = {key: _rms2(x, kg, first) for key, x in k_raw.items()}
            qq = {key: _rms2(x, qg, first) for key, x in q_raw.items()}
            knb = {key: kk[key][2].astype(BF16) for key in kk}
            qnb = {key: _split_heads(qq[key][2], first) for key in qq}
            vb = {key: x.astype(BF16) for key, x in v_raw.items()}
            dob = {key: _split_heads(x, first) for key, x in do_raw.items()}
            delta = {key: _per_head_sum(do_raw[key] * o_raw[key], first) for key in q_raw}
            pick = lambda x, h: x[:, h * DSW_DH:h * DSW_DH + 1]
            full = [(r, t, h) for r in res for t in range(tb) for h in heads]
            half = [(r, tb, h) for r in res for h in heads]
            s = {}
            for r, t, h in full:
                sv = _dot(qnb[r, t][h], jnp.concatenate([knb[r, t - 1], knb[r, t]], axis=0), "nt") + bias_ref[h]
                s[r, t, h] = jnp.where((i == 0) & (col < B), NEG_BIG, sv) if t == 0 else sv
            for r, t, h in half:
                s[r, t, h] = _dot(qnb[r, t][h], knb[r, t - 1], "nt") + bias_ref[h, :, 0:B]
            p = {(r, t, h): jnp.exp(s[r, t, h] - pick(l_raw[r, t], h)) for r, t, h in full}
            for r, t, h in half:
                p[r, t, h] = jnp.where(has_next, jnp.exp(s[r, t, h] - pick(l_raw[r, t], h)), 0.0)
            dp = {(r, t, h): _dot(dob[r, t][h], jnp.concatenate([vb[r, t - 1], vb[r, t]], axis=0), "nt")
                  for r, t, h in full}
            for r, t, h in half:
                dp[r, t, h] = _dot(dob[r, t][h], vb[r, t - 1], "nt")
            ds = {(r, t, h): p[r, t, h] * (dp[r, t, h] - pick(delta[r, t], h)) for r, t, h in full + half}
            pb = {it: p[it].astype(BF16) for it in ds}
            dsb = {it: ds[it].astype(BF16) for it in ds}
            for h in heads:
                tot = None
                for r in res:
                    for t in range(tb):
                        tot = ds[r, t, h] if tot is None else tot + ds[r, t, h]
                db_ref[h] += tot
            blocks = [(r, t) for r in res for t in range(tb)]
            keys2 = {(r, t): jnp.concatenate([knb[r, t - 1], knb[r, t]], axis=0) for r, t in blocks}
            dqn = {(r, t): jnp.where(first, _dot(dsb[r, t, 0], keys2[r, t]), _dot(dsb[r, t, 1], keys2[r, t]))
                   for r, t in blocks}
            prev_half = lambda x, r, t, h: x[r, t, h][:, :B] if t < tb else x[r, t, h]
            dkn = {(r, t): sum(_dot(dsb[r, t, h][:, B:], qnb[r, t][h], "tn")
                               + _dot(prev_half(dsb, r, t + 1, h), qnb[r, t + 1][h], "tn") for h in heads)
                   for r, t in blocks}
            dvv = {(r, t): sum(_dot(pb[r, t, h][:, B:], dob[r, t][h], "tn")
                               + _dot(prev_half(pb, r, t + 1, h), dob[r, t + 1][h], "tn") for h in heads)
                   for r, t in blocks}
            for r, t in blocks:
                dqg_acc = dqg_acc + jnp.sum(dqn[r, t] * qq[r, t][0], axis=0, keepdims=True)
                dkg_acc = dkg_acc + jnp.sum(dkn[r, t] * kk[r, t][0], axis=0, keepdims=True)
            for r, t in blocks:
                rows = _block_rows(t, r, slab, d)
                dq_ref[rows, :] = _rms2_bwd(dqn[r, t], qq[r, t][0], qq[r, t][1], qg, first)
                dk_ref[rows, :] = _rms2_bwd(dkn[r, t], kk[r, t][0], kk[r, t][1], kg, first)
                dv_ref[rows, :] = dvv[r, t]

        start = (hp == 0) & (i == 0)
        fold = lambda a: a[:, :DSW_DH] + a[:, DSW_DH:]

        @pl.when(start)
        def _():
            dqg_ref[...] = fold(dqg_acc) * scale
            dkg_ref[...] = fold(dkg_acc)

        @pl.when(jnp.logical_not(start))
        def _():
            dqg_ref[...] += fold(dqg_acc) * scale
            dkg_ref[...] += fold(dkg_acc)

    def spec(rows, pick, base):
        return pl.BlockSpec((rows, LANES), lambda hp, i: (pick(i), base + hp))

    same = lambda i: i
    before = lambda i: jnp.maximum(i * tb - 1, 0)
    after = lambda i: jnp.minimum((i + 1) * tb, n_slabs - 1)
    cur, cur1 = spec(rt, same, cb), spec(rt, same, 0)
    vec = pl.BlockSpec((1, DSW_DH), lambda hp, i: (0, 0))
    vec2 = pl.BlockSpec((1, LANES), lambda hp, i: (0, 0))
    bspec = pl.BlockSpec((_HP, B, 2 * B), lambda hp, i: (hp, 0, 0))
    shp = jax.ShapeDtypeStruct((S, WT), F32)
    vshp = jax.ShapeDtypeStruct((1, DSW_DH), F32)
    carried = [] if prev_out is None else list(prev_out)
    n_in = 15
    return pl.pallas_call(
        body, name=name, grid=(_DSW_W // LANES, n_tiles),
        in_specs=[cur, spec(slab, after, cb), spec(slab, before, cb), cur, spec(slab, before, cb), cur,
                  cur1, spec(slab, after, 0), cur1, spec(slab, after, 0), cur1, spec(slab, after, 0),
                  bspec, vec2, vec2] + [pl.BlockSpec(memory_space=pl.ANY)] * len(carried),
        out_specs=[cur, cur, cur, bspec, vec, vec],
        out_shape=[shp, shp, shp, jax.ShapeDtypeStruct(bias.shape, F32), vshp, vshp],
        input_output_aliases={n_in + j: j for j in range(len(carried))},
        compiler_params=_params("arbitrary", "arbitrary"),
    )(q, q, k, k, v, v, o, o, lse, lse, do, do, bias, jnp.tile(q_gain, (1, _HP)), jnp.tile(k_gain, (1, _HP)),
      *carried)


def _t5_bucket(dist):
    max_exact = REL_BUCKETS // 2
    scaled = jnp.log(jnp.maximum(dist, 1).astype(F32) / max_exact) / math.log(REL_MAX_DIST / max_exact)
    large = jnp.minimum(max_exact + (scaled * (REL_BUCKETS - max_exact)).astype(jnp.int32), REL_BUCKETS - 1)
    return jnp.where(dist < max_exact, dist, large)


def _dsw_band():
    dist = (jnp.arange(DSW_BLK)[:, None] + DSW_BLK) - jnp.arange(2 * DSW_BLK)[None, :]
    return dist, (dist >= 0) & (dist <= DSW_BLK)


def _dsw_bias(rel_bias):
    dist, band = _dsw_band()
    out = []
    for g, (_, d) in enumerate(DSW_GROUPS):
        hot = jax.nn.one_hot(_t5_bucket(jnp.maximum(dist, 0) * d), REL_BUCKETS, dtype=F32)
        tab = jnp.einsum("qkb,bh->hqk", hot, rel_bias[:, g * DSW_HEADS:(g + 1) * DSW_HEADS],
                         precision=lax.Precision.HIGHEST)
        out.append(jnp.where(band[None], tab, NEG_BIG))
    return jnp.stack(out)


def _dsw_bucket_onehot():
    dist, band = _dsw_band()
    out = []
    for _, d in DSW_GROUPS:
        hot = jax.nn.one_hot(_t5_bucket(jnp.maximum(dist, 0) * d), LANES, dtype=BF16)
        out.append(jnp.where(band[..., None], hot, 0).reshape(-1, LANES))
    return jnp.stack(out)


def _exchange(send, *, gather, name):
    R, C = send.shape[-2:]

    def body(src_ref, dst_ref, send_sems, recv_sems, local_sem):
        x, y, c = lax.axis_index("x"), lax.axis_index("y"), lax.axis_index("c")
        me = 4 * x + 2 * y + c
        mine = pltpu.make_async_copy(src_ref if gather else src_ref.at[me], dst_ref.at[me], local_sem)
        mine.start()
        copies = []
        for rel in range(1, N_DEV):
            px = 1 - x if rel & 4 else x
            py = 1 - y if rel & 2 else y
            pc = 1 - c if rel & 1 else c
            peer = 4 * px + 2 * py + pc
            cp = pltpu.make_async_remote_copy(
                src_ref=src_ref if gather else src_ref.at[peer], dst_ref=dst_ref.at[me],
                send_sem=send_sems.at[rel - 1], recv_sem=recv_sems.at[rel - 1],
                device_id=(px, py, pc), device_id_type=pl.DeviceIdType.MESH)
            cp.start()
            copies.append(cp)
        for cp in copies:
            cp.wait()
        mine.wait()

    return pl.pallas_call(
        body, name=name,
        in_specs=[pl.BlockSpec(memory_space=pl.ANY)], out_specs=pl.BlockSpec(memory_space=pl.ANY),
        out_shape=jax.ShapeDtypeStruct((N_DEV, R, C), send.dtype),
        scratch_shapes=[pltpu.SemaphoreType.DMA((N_DEV - 1,)), pltpu.SemaphoreType.DMA((N_DEV - 1,)),
                        pltpu.SemaphoreType.DMA(())],
    )(send)


def _gather_two_level(send, *, name):
    R, C = send.shape

    def body(src_ref, dst_ref, send_sems, recv_sems, local_sem):
        x, y, c = lax.axis_index("x"), lax.axis_index("y"), lax.axis_index("c")
        me, sibling = (x, y, c), (x, y, 1 - c)
        chips = [(1 - x, y), (x, 1 - y), (1 - x, 1 - y)]

        def slot(px, py, pc):
            return dst_ref.at[4 * px + 2 * py + pc]

        def copy(k, block, to, src=None):
            return pltpu.make_async_remote_copy(
                src_ref=slot(*block) if src is None else src, dst_ref=slot(*block),
                send_sem=send_sems.at[k], recv_sem=recv_sems.at[k],
                device_id=to, device_id_type=pl.DeviceIdType.MESH)

        mine = pltpu.make_async_copy(src_ref, slot(*me), local_sem)
        mine.start()
        first = [copy(0, me, sibling, src=src_ref)]
        first += [copy(1 + j, me, (*chip, c), src=src_ref) for j, chip in enumerate(chips)]
        for cp in first:
            cp.start()
        passed = [copy(4 + j, (*chip, c), sibling) for j, chip in enumerate(chips)]
        for j, chip in enumerate(chips):
            copy(1 + j, (*chip, c), me).wait_recv()
            passed[j].start()
        copy(0, sibling, me).wait_recv()
        for j, chip in enumerate(chips):
            copy(4 + j, (*chip, 1 - c), me).wait_recv()
        for cp in first + passed:
            cp.wait_send()
        mine.wait()

    return pl.pallas_call(
        body, name=name,
        in_specs=[pl.BlockSpec(memory_space=pl.ANY)], out_specs=pl.BlockSpec(memory_space=pl.ANY),
        out_shape=jax.ShapeDtypeStruct((N_DEV, R, C), send.dtype),
        scratch_shapes=[pltpu.SemaphoreType.DMA((N_DEV - 1,)), pltpu.SemaphoreType.DMA((N_DEV - 1,)),
                        pltpu.SemaphoreType.DMA(())],
    )(send)


_ANY = pl.BlockSpec(memory_space=pl.ANY)


def _swap_with_sibling(sends, *, name):
    n = len(sends)

    def body(*refs):
        x, y, c = lax.axis_index("x"), lax.axis_index("y"), lax.axis_index("c")
        send_sems, recv_sems = refs[2 * n:]
        copies = [pltpu.make_async_remote_copy(
            src_ref=refs[a], dst_ref=refs[n + a], send_sem=send_sems.at[a], recv_sem=recv_sems.at[a],
            device_id=(x, y, 1 - c), device_id_type=pl.DeviceIdType.MESH) for a in range(n)]
        for cp in copies:
            cp.start()
        for cp in copies:
            cp.wait()

    return pl.pallas_call(
        body, name=name, in_specs=[_ANY] * n, out_specs=[_ANY] * n,
        out_shape=[jax.ShapeDtypeStruct(s.shape, s.dtype) for s in sends],
        scratch_shapes=[pltpu.SemaphoreType.DMA((n,)), pltpu.SemaphoreType.DMA((n,))],
    )(*sends)


def _fill_from_sibling(bufs, *, name):
    n, n_chips = len(bufs), bufs[0].shape[0]

    def body(*refs):
        x, y, c = lax.axis_index("x"), lax.axis_index("y"), lax.axis_index("c")
        send_sems, recv_sems = refs[2 * n:]
        copies = [pltpu.make_async_remote_copy(
            src_ref=refs[a].at[q, c], dst_ref=refs[n + a].at[q, c],
            send_sem=send_sems.at[a * n_chips + q], recv_sem=recv_sems.at[a * n_chips + q],
            device_id=(x, y, 1 - c), device_id_type=pl.DeviceIdType.MESH) for a in range(n) for q in range(n_chips)]
        for cp in copies:
            cp.start()
        for cp in copies:
            cp.wait()

    return pl.pallas_call(
        body, name=name, in_specs=[_ANY] * n, out_specs=[_ANY] * n,
        out_shape=[jax.ShapeDtypeStruct(b.shape, b.dtype) for b in bufs],
        input_output_aliases={a: a for a in range(n)},
        scratch_shapes=[pltpu.SemaphoreType.DMA((n * n_chips,)), pltpu.SemaphoreType.DMA((n * n_chips,))],
    )(*bufs)


def _exchange_chips(send, *, name):
    def body(src_ref, dst_ref, *sems):
        copies = _chip_copies([src_ref], [dst_ref], *sems)
        for cp in copies:
            cp.start()
        for cp in copies:
            cp.wait()

    return pl.pallas_call(
        body, name=name, in_specs=[_ANY], out_specs=_ANY,
        out_shape=jax.ShapeDtypeStruct(send.shape, send.dtype), scratch_shapes=_chip_sems(1),
    )(send)


def _add_pair(a, b, *, name):
    lead, (R, C) = a.shape[:-2], a.shape[-2:]
    tr = _tile(R, max(8, 1024 * LANES // C))

    def body(a_ref, b_ref, o_ref):
        o_ref[...] = (a_ref[...].astype(F32) + b_ref[...].astype(F32)).astype(o_ref.dtype)

    blk = pl.BlockSpec((None,) * len(lead) + (tr, C), lambda *idx: idx + (0,))
    return pl.pallas_call(
        body, name=name, grid=lead + (R // tr,), in_specs=[blk, blk], out_specs=blk,
        out_shape=jax.ShapeDtypeStruct(a.shape, a.dtype),
        compiler_params=_params(*(("parallel",) * (len(lead) + 1))),
    )(a, b)


_BIG = ("w_ffn_in", "w_ffn_out", "gdn_w_in", "gdn_conv", "gdn_w_out", "dsw_w_in", "dsw_w_out")
_LATE = ("gdn_w_in", "gdn_conv", "gdn_w_out")
_EARLY = tuple(n for n in _BIG if n not in _LATE)
_NATIVE = ("w_ffn_in", "w_ffn_out", "dsw_w_in")
_SHARD_AXIS = {"w_ffn_in": 2, "w_ffn_out": 1, "gdn_w_in": 2, "gdn_conv": 2, "gdn_w_out": 1, "dsw_w_in": 2,
               "dsw_w_out": 2}
_SMALL = ("b_ada", "norm_mix", "norm_ffn", "gdn_a_log", "gdn_dt_bias", "gdn_out_norm", "dsw_q_norm",
          "dsw_k_norm", "rel_bias")
_ROW_ALIGN = 16
_BIG_ALIGN = 1024


def _ceil_to(n, m):
    return -(-n // m) * m


def _seg_rows(shape):
    return _ceil_to(_ceil_to(int(np.prod(shape)), LANES) // LANES, _ROW_ALIGN)


def _pack(arrs, total_align):
    lead = arrs[0][1]
    segs = []
    for a, nlead in arrs:
        assert nlead == lead
        bshape = a.shape[:nlead]
        n = int(np.prod(a.shape[nlead:]))
        rows = _seg_rows(a.shape[nlead:])
        flat = a.reshape(bshape + (n,))
        flat = jnp.pad(flat, [(0, 0)] * nlead + [(0, rows * LANES - n)])
        segs.append(flat.reshape(bshape + (rows, LANES)))
    buf = jnp.concatenate(segs, axis=lead)
    total = _ceil_to(buf.shape[lead], total_align)
    return jnp.pad(buf, [(0, 0)] * lead + [(0, total - buf.shape[lead]), (0, 0)])


def _unpack(buf, shapes, nlead):
    out, off = [], 0
    for shp in shapes:
        n, rows = int(np.prod(shp)), _seg_rows(shp)
        seg = lax.slice_in_dim(buf, off, off + rows, axis=nlead)
        seg = seg.reshape(buf.shape[:nlead] + (rows * LANES,))[..., :n]
        out.append(seg.reshape(buf.shape[:nlead] + tuple(shp)))
        off += rows
    return out


def _to_natural(g, axis):
    n, L, r, c = g.shape
    if axis == 2:
        return jnp.transpose(g, (1, 2, 0, 3)).reshape(L, r, n * c)
    return jnp.transpose(g, (1, 0, 2, 3)).reshape(L, n * r, c)


def _to_blocked(w, axis):
    L, R, C = w.shape
    if axis == 2:
        return jnp.transpose(w.reshape(L, R, N_DEV, C // N_DEV), (2, 0, 1, 3))
    return jnp.transpose(w.reshape(L, N_DEV, R // N_DEV, C), (1, 0, 2, 3))


def _hm(a):
    lead = a.shape[:-1]
    return jnp.swapaxes(a.reshape(lead + (3, GDN_HEADS, GDN_DK)), -3, -2).reshape(lead + (3 * GDN_HEADS * GDN_DK,))


def _un_hm(a):
    lead = a.shape[:-1]
    return jnp.swapaxes(a.reshape(lead + (GDN_HEADS, 3, GDN_DK)), -3, -2).reshape(lead + (3 * GDN_HEADS * GDN_DK,))


_TILES = (2048, 1536, 1408, 1024, 768, 512, 384, 256, 128, 64, 32, 16, 8)


def _tile(n, cap):
    for t in _TILES:
        if t <= cap and n % t == 0:
            return t
    return n


def _mm_auto(a, b, mode, name, **kw):
    if mode == "tn":
        (K, M), N = a.shape, b.shape[1]
        deep = 2048 if a.dtype == BF16 and b.dtype == BF16 else 1024
        tm, tn, tk = _tile(M, 1408), _tile(N, 1408), _tile(K, deep)
    else:
        M, K = a.shape
        N = b.shape[1] if mode == "nn" else b.shape[0]
        tm, tn, tk = _tile(M, _MM_ROWS), _tile(N, 1536), _tile(K, 1408)
    return _mm(a, b, mode=mode, name=name, tm=tm, tn=tn, tk=tk, **kw)


def _row(v):
    return v.reshape(1, -1)


def _ffn_in_act(h, w_in, *, name):
    S, D = h.shape
    F = w_in.shape[1] // 2
    tm, tn = _tile(S, _MM_ROWS), _tile(F, 1408)
    nj = F // tn

    def body(h_ref, wg_ref, wu_ref, g_ref, u_ref, a_ref):
        hv = h_ref[...]
        gate = jnp.dot(hv, wg_ref[...], preferred_element_type=F32)
        up = jnp.dot(hv, wu_ref[...], preferred_element_type=F32)
        g_ref[...] = gate.astype(BF16)
        u_ref[...] = up.astype(BF16)
        a_ref[...] = (_silu(gate) * up).astype(BF16)

    out = pl.BlockSpec((tm, tn), lambda i, j: (i, j))
    shp = jax.ShapeDtypeStruct((S, F), BF16)
    return pl.pallas_call(
        body, name=name, grid=(S // tm, nj),
        in_specs=[pl.BlockSpec((tm, D), lambda i, j: (i, 0)), pl.BlockSpec((D, tn), lambda i, j: (0, j)),
                  pl.BlockSpec((D, tn), lambda i, j: (0, j + nj))],
        out_specs=[out, out, out], out_shape=[shp, shp, shp],
        compiler_params=_params("parallel", "parallel"),
    )(h, w_in, w_in)


def _ffn_out_dx_act(dy, w_out, gate_vec, pg, pu, *, name):
    S, D = dy.shape
    F = w_out.shape[0]
    tm, tn = _tile(S, _MM_ROWS), _tile(F, 1408)

    def body(dy_ref, w_ref, gv_ref, pg_ref, pu_ref, dg_ref, du_ref):
        dyg = (dy_ref[...] * gv_ref[...]).astype(BF16)
        da = lax.dot_general(dyg, w_ref[...], _DOT_DIMS["nt"], preferred_element_type=F32)
        gate = pg_ref[...].astype(F32)
        up = pu_ref[...].astype(F32)
        sg = _sigmoid(gate)
        dg_ref[...] = (da * up * (sg * (1.0 + gate * (1.0 - sg)))).astype(BF16)
        du_ref[...] = (da * (gate * sg)).astype(BF16)

    blk = pl.BlockSpec((tm, tn), lambda i, j: (i, j))
    shp = jax.ShapeDtypeStruct((S, F), BF16)
    return pl.pallas_call(
        body, name=name, grid=(S // tm, F // tn),
        in_specs=[pl.BlockSpec((tm, D), lambda i, j: (i, 0)), pl.BlockSpec((tn, D), lambda i, j: (j, 0)),
                  pl.BlockSpec((1, D), lambda i, j: (0, 0)), blk, blk],
        out_specs=[blk, blk], out_shape=[shp, shp],
        compiler_params=_params("parallel", "parallel"),
    )(dy, w_out, gate_vec, pg, pu)


def _ffn_fwd(x, mod, gain, w_in, w_out, tag):
    sh, sc, gate = mod
    h = _norm_mod_fwd(x, gain, sc, sh, name=f"ffn_norm_{tag}")
    pg, pu, a = _ffn_in_act(h, w_in, name=f"ffn_in_{tag}")
    y = _mm_auto(a, w_out, "nn", f"ffn_out_{tag}", out_scale=gate, resid=x)
    return y, (x, h, pg, pu, a)


def _ffn_bwd(dy, saved, mod, gain, w_in, w_out, tag):
    sh, sc, gate = mod
    x, h, pg, pu, a = saved
    F = pg.shape[1]
    gmat = _mm_auto(a, dy, "tn", f"ffn_out_g_{tag}")
    dw_out, dgate = _wout_grad(gmat, w_out, gate, name=f"ffn_out_dw_{tag}")
    dpg, dpu = _ffn_out_dx_act(dy, w_out, gate, pg, pu, name=f"ffn_out_dx_{tag}")
    dw_in = jnp.concatenate([_mm_auto(h, dpg, "tn", f"ffn_in_dw_gate_{tag}", out_dtype=BF16),
                             _mm_auto(h, dpu, "tn", f"ffn_in_dw_up_{tag}", out_dtype=BF16)], axis=1)
    tk = _tile(F, 1408)
    dh = _mm_sum_nt([(dpg, w_in, tk, 0), (dpu, w_in, tk, F)], name=f"ffn_in_dx_{tag}")
    dx, dsh, dsc, dgain = _norm_mod_bwd(dh, x, dy, gain, sc, name=f"ffn_norm_bwd_{tag}")
    return dx, dict(w_in=dw_in, w_out=dw_out, gain=dgain, mod=(dsh, dsc, dgate))


def _gdn_fwd(x, mod, gain, W, riding=None):
    sh, sc, gate = mod
    S = x.shape[0]
    h = _norm_mod_fwd(x, gain, sc, sh, name="gdn_norm")
    pq = _mm_auto(h, W["gdn_qkv"], "nn", "gdn_in_qkv", out_dtype=BF16)
    z = _mm_auto(h, W["gdn_z"], "nn", "gdn_in_z", out_dtype=BF16)
    ab = _mm_auto(h, W["gdn_ab"], "nn", "gdn_in_ab")
    qkvn = _gdn_prep_fwd(pq, W["gdn_conv"], name="gdn_prep")
    ab4 = jnp.transpose(ab[:, :2 * GDN_HEADS]).reshape(2 * GDN_HEADS, S // GDN_CHUNK, 1, GDN_CHUNK)
    o, states, tinvs, *rode = _gdn_chunk_fwd(qkvn, ab4, W["gdn_a_log"], W["gdn_dt_bias"], name="gdn_chunk",
                                             riding=riding)
    o2 = _gdn_outnorm_fwd(o, z, W["gdn_out_norm"], name="gdn_outnorm")
    y = _mm_auto(o2, W["gdn_out"], "nn", "gdn_out", out_scale=gate, resid=x)
    return y, (x, h, pq, z, qkvn, ab4, o, states, tinvs, o2), (tuple(rode) if rode else None)


def _gdn_bwd(dy, saved, mod, gain, W, riding=None):
    sh, sc, gate = mod
    x, h, pq, z, qkvn, ab4, o, states, tinvs, o2 = saved
    S = x.shape[0]
    gmat = _mm_auto(o2, dy, "tn", "gdn_out_g")
    dw_out, dgate = _wout_grad(gmat, W["gdn_out"], gate, name="gdn_out_dw")
    do2 = _mm_auto(dy, W["gdn_out"], "nt", "gdn_out_dx", a_scale=gate)
    do, dz, dout_norm = _gdn_outnorm_bwd(do2, o, z, W["gdn_out_norm"], name="gdn_outnorm_bwd")
    dqkvn, dab4, da_log, ddt_bias, *rode = _gdn_chunk_bwd(
        qkvn, ab4, W["gdn_a_log"], W["gdn_dt_bias"], states, tinvs, do, name="gdn_chunk_bwd", riding=riding)
    dc, dconv8 = _gdn_prep_bwd_pre(dqkvn, pq, W["gdn_conv"], name="gdn_prep_bwd")
    dpq = _gdn_conv_bwd_x(dc, W["gdn_conv"], name="gdn_conv_bwd")
    dab = jnp.transpose(dab4.reshape(2 * GDN_HEADS, S))
    dab = jnp.pad(dab, ((0, 0), (0, LANES - 2 * GDN_HEADS))).astype(BF16)
    dw_qkv = _mm_auto(h, dpq, "tn", "gdn_in_qkv_dw", out_dtype=BF16)
    dw_z = _mm_auto(h, dz, "tn", "gdn_in_z_dw", out_dtype=BF16)
    dw_ab = _mm_auto(h, dab, "tn", "gdn_in_ab_dw", out_dtype=BF16)
    dh = _mm_sum_nt([(dpq, W["gdn_qkv"], 1024, 0), (dz, W["gdn_z"], 1024, 0), (dab, W["gdn_ab"], LANES, 0)],
                    name="gdn_in_dx")
    dx, dsh, dsc, dgain = _norm_mod_bwd(dh, x, dy, gain, sc, name="gdn_norm_bwd")
    dw_in = jnp.concatenate([_un_hm(dw_qkv), dw_z, dw_ab[:, :2 * GDN_HEADS]], axis=1)
    return dx, dict(gdn_w_in=dw_in, gdn_conv=_un_hm(dconv8[:GDN_CONV]), gdn_w_out=dw_out, gdn_out_norm=dout_norm,
                    gdn_a_log=da_log.reshape(1, GDN_HEADS), gdn_dt_bias=ddt_bias.reshape(1, GDN_HEADS),
                    gain=dgain, mod=(dsh, dsc, dgate)), (tuple(rode) if rode else None)


def _dsw_fwd(x, mod, gain, W):
    sh, sc, gate = mod
    h = _norm_mod_fwd(x, gain, sc, sh, name="dsw_norm")
    q, k, v = (_mm_auto(h, W[n], "nn", f"dsw_in_{n[-1]}") for n in ("dsw_q", "dsw_k", "dsw_v"))
    outs = None
    for g in range(len(DSW_GROUPS)):
        outs = _dsw_attn_fwd(q, k, v, W["dsw_bias"][g], W["dsw_q_norm"], W["dsw_k_norm"], outs, g=g,
                             name=f"dsw_attn_{g}")
    o, lse = _dsw_merge(*outs, name="dsw_merge")
    y = _mm_auto(o, W["dsw_out"], "nn", "dsw_out", out_scale=gate, resid=x)
    return y, (x, h, q, k, v, o, lse)


def _dsw_bwd(dy, saved, mod, gain, W):
    sh, sc, gate = mod
    x, h, q, k, v, o, lse = saved
    gmat = _mm_auto(o, dy, "tn", "dsw_out_g")
    dw_out, dgate = _wout_grad(gmat, W["dsw_out"], gate, name="dsw_out_dw")
    do = _mm_auto(dy, W["dsw_out"], "nt", "dsw_out_dx", a_scale=gate)
    G = len(DSW_GROUPS)
    dqkv, dbias, dq_norm, dk_norm = None, [], 0.0, 0.0
    for g in range(G):
        *dqkv, db, dqg, dkg = _dsw_attn_bwd(q, k, v, o, lse, do, W["dsw_bias"][g], W["dsw_q_norm"],
                                            W["dsw_k_norm"], dqkv, g=g, name=f"dsw_attn_bwd_{g}")
        dbias.append(db)
        dq_norm, dk_norm = dq_norm + dqg, dk_norm + dkg
    names = ("dsw_q", "dsw_k", "dsw_v")
    dws = [_mm_auto(h, d, "tn", f"dsw_in_{n[-1]}_dw", out_dtype=BF16) for n, d in zip(names, dqkv)]
    dh = _mm_sum_nt([(d, W[n], _tile(d.shape[1], 1024), 0) for n, d in zip(names, dqkv)], name="dsw_in_dx")
    dx, dsh, dsc, dgain = _norm_mod_bwd(dh, x, dy, gain, sc, name="dsw_norm_bwd")
    hot = _dsw_bucket_onehot()
    drel = [_mm(dbias[g].reshape(DSW_HEADS, -1), hot[g], mode="nn", name=f"dsw_rel_bias_{g}", tm=DSW_HEADS,
                tn=LANES, tk=8192)[:, :REL_BUCKETS] for g in range(G)]
    return dx, dict(dsw_w_in=jnp.concatenate(dws, axis=1), dsw_w_out=dw_out, dsw_q_norm=dq_norm,
                    dsw_k_norm=dk_norm, rel_bias=jnp.transpose(jnp.concatenate(drel, axis=0)),
                    gain=dgain, mod=(dsh, dsc, dgate))


def _local_step(x, target, mod, W, late_weights=None, early_pairs=None):
    mods = [[_row(mod[l, i]) for i in range(6)] for l in range(2)]
    nmix = [_row(W["norm_mix"][l]) for l in range(2)]
    nffn = [_row(W["norm_ffn"][l]) for l in range(2)]
    x1, s_gdn, arrived = _gdn_fwd(x, mods[0][:3], nmix[0], W, None if late_weights is None else late_weights[0])
    if late_weights is not None:
        W = {**W, **late_weights[1](arrived)}
    x2, s_f0 = _ffn_fwd(x1, mods[0][3:], nffn[0], W["w_ffn_in"][0], W["w_ffn_out"][0], "0")
    x3, s_dsw = _dsw_fwd(x2, mods[1][:3], nmix[1], W)
    x4, s_f1 = _ffn_fwd(x3, mods[1][3:], nffn[1], W["w_ffn_in"][1], W["w_ffn_out"][1], "1")
    dx4, sse = _loss_head(x4, target, name="loss_head")
    dx3, g_f1 = _ffn_bwd(dx4, s_f1, mods[1][3:], nffn[1], W["w_ffn_in"][1], W["w_ffn_out"][1], "1")
    dx2, g_dsw = _dsw_bwd(dx3, s_dsw, mods[1][:3], nmix[1], W)
    dx1, g_f0 = _ffn_bwd(dx2, s_f0, mods[0][3:], nffn[0], W["w_ffn_in"][0], W["w_ffn_out"][0], "0")
    grads = dict(
        w_ffn_in=jnp.stack([g_f0["w_in"], g_f1["w_in"]]), w_ffn_out=jnp.stack([g_f0["w_out"], g_f1["w_out"]]),
        dsw_w_in=g_dsw["dsw_w_in"][None], dsw_w_out=g_dsw["dsw_w_out"][None])
    riding = None if early_pairs is None else early_pairs(grads)
    dx0, g_gdn, rode = _gdn_bwd(dx1, s_gdn, mods[0][:3], nmix[0], W, riding)
    dmod = jnp.stack([jnp.concatenate(list(g_gdn["mod"]) + list(g_f0["mod"]), axis=0),
                      jnp.concatenate(list(g_dsw["mod"]) + list(g_f1["mod"]), axis=0)])
    grads.update(
        norm_mix=jnp.concatenate([g_gdn["gain"], g_dsw["gain"]], axis=0),
        norm_ffn=jnp.concatenate([g_f0["gain"], g_f1["gain"]], axis=0),
        gdn_w_in=g_gdn["gdn_w_in"][None], gdn_conv=g_gdn["gdn_conv"][None], gdn_w_out=g_gdn["gdn_w_out"][None],
        gdn_out_norm=g_gdn["gdn_out_norm"], gdn_a_log=g_gdn["gdn_a_log"], gdn_dt_bias=g_gdn["gdn_dt_bias"],
        dsw_q_norm=g_dsw["dsw_q_norm"], dsw_k_norm=g_dsw["dsw_k_norm"], rel_bias=g_dsw["rel_bias"])
    return sse, dx0, grads, dmod, rode


def _prepare_first(full, small):
    gw = full["gdn_w_in"][0]
    hk3 = 3 * GDN_HEADS * GDN_DK
    return dict(
        gdn_qkv=_hm(gw[:, :hk3]), gdn_z=gw[:, hk3:hk3 + GDN_HEADS * GDN_DK],
        gdn_ab=jnp.pad(gw[:, hk3 + GDN_HEADS * GDN_DK:], ((0, 0), (0, LANES - 2 * GDN_HEADS))),
        gdn_conv=_hm(full["gdn_conv"][0]), gdn_out=full["gdn_w_out"][0],
        norm_mix=small["norm_mix"], norm_ffn=small["norm_ffn"],
        gdn_a_log=small["gdn_a_log"].reshape(GDN_HEADS, 1, 1), gdn_dt_bias=small["gdn_dt_bias"].reshape(GDN_HEADS, 1, 1),
        gdn_out_norm=small["gdn_out_norm"], dsw_q_norm=small["dsw_q_norm"], dsw_k_norm=small["dsw_k_norm"],
        dsw_bias=_dsw_bias(small["rel_bias"]))


def _prepare_rest(full):
    di = full["dsw_w_in"][0]
    dq = di.shape[1] // 3
    return dict(w_ffn_in=full["w_ffn_in"], w_ffn_out=full["w_ffn_out"],
                dsw_q=di[:, :dq], dsw_k=di[:, dq:2 * dq], dsw_v=di[:, 2 * dq:], dsw_out=full["dsw_w_out"][0])


def _prepare_weights(full, small):
    return {**_prepare_first(full, small), **_prepare_rest(full)}


_W_NAMES = ("w_ada", "b_ada", "norm_mix", "norm_ffn", "w_ffn_in", "w_ffn_out", "gdn_w_in", "gdn_conv",
            "gdn_a_log", "gdn_dt_bias", "gdn_out_norm", "gdn_w_out", "dsw_w_in", "dsw_q_norm", "dsw_k_norm",
            "dsw_w_out", "rel_bias")
_PAD_BATCH = 16


def _pad_rows(a, rows):
    return jnp.pad(a, ((0, rows - a.shape[0]), (0, 0)))


def kernel(x, c, w_ada, b_ada, norm_mix, norm_ffn, w_ffn_in, w_ffn_out, gdn_w_in, gdn_conv, gdn_a_log, gdn_dt_bias, gdn_out_norm, gdn_w_out, dsw_w_in, dsw_q_norm, dsw_k_norm, dsw_w_out, rel_bias, loss_target, m_w_ada, m_b_ada, m_norm_mix, m_norm_ffn, m_w_ffn_in, m_w_ffn_out, m_gdn_w_in, m_gdn_conv, m_gdn_a_log, m_gdn_dt_bias, m_gdn_out_norm, m_gdn_w_out, m_dsw_w_in, m_dsw_q_norm, m_dsw_k_norm, m_dsw_w_out, m_rel_bias, v_w_ada, v_b_ada, v_norm_mix, v_norm_ffn, v_w_ffn_in, v_w_ffn_out, v_gdn_w_in, v_gdn_conv, v_gdn_a_log, v_gdn_dt_bias, v_gdn_out_norm, v_gdn_w_out, v_dsw_w_in, v_dsw_q_norm, v_dsw_k_norm, v_dsw_w_out, v_rel_bias):
    w = dict(zip(_W_NAMES, (w_ada, b_ada, norm_mix, norm_ffn, w_ffn_in, w_ffn_out, gdn_w_in, gdn_conv, gdn_a_log,
                            gdn_dt_bias, gdn_out_norm, gdn_w_out, dsw_w_in, dsw_q_norm, dsw_k_norm, dsw_w_out,
                            rel_bias)))
    m = dict(zip(_W_NAMES, (m_w_ada, m_b_ada, m_norm_mix, m_norm_ffn, m_w_ffn_in, m_w_ffn_out, m_gdn_w_in,
                            m_gdn_conv, m_gdn_a_log, m_gdn_dt_bias, m_gdn_out_norm, m_gdn_w_out, m_dsw_w_in,
                            m_dsw_q_norm, m_dsw_k_norm, m_dsw_w_out, m_rel_bias)))
    v = dict(zip(_W_NAMES, (v_w_ada, v_b_ada, v_norm_mix, v_norm_ffn, v_w_ffn_in, v_w_ffn_out, v_gdn_w_in,
                            v_gdn_conv, v_gdn_a_log, v_gdn_dt_bias, v_gdn_out_norm, v_gdn_w_out, v_dsw_w_in,
                            v_dsw_q_norm, v_dsw_k_norm, v_dsw_w_out, v_rel_bias)))
    D = x.shape[-1]
    n_layers, _, ada_cols = w_ada.shape

    c_all = _exchange(c.reshape(D // LANES, LANES), gather=True, name="gather_cond").reshape(N_DEV, D)
    c_pad = _pad_rows(c_all, _PAD_BATCH)
    proj = [_mm(c_pad, w_ada[l], mode="nn", name=f"ada_proj_{l}", tm=_PAD_BATCH, tn=ada_cols, tk=D, a_silu=True)
            for l in range(n_layers)]
    mod_send = _pack([(jnp.stack([p[:N_DEV] for p in proj], axis=1), 1)], _ROW_ALIGN)
    mod_recv = _exchange(mod_send, gather=False, name="scatter_mod")
    mod = _unpack(mod_recv, [(n_layers, ada_cols)], 1)[0]
    mod = jnp.transpose(mod, (1, 0, 2)).reshape(n_layers, N_DEV * ada_cols) + b_ada
    mod = mod.reshape(n_layers, 6, D)

    conv_hi = gdn_conv.astype(BF16)
    conv_lo = (gdn_conv - conv_hi.astype(F32)).astype(BF16)
    first_send = _pack([(conv_hi if n == "gdn_conv" else w[n].astype(BF16), 0) for n in _LATE] + [(conv_lo, 0)],
                       _ROW_ALIGN)
    parts = _unpack(_gather_two_level(first_send, name="gather_weights_first"),
                    [w[n].shape for n in _LATE] + [gdn_conv.shape], 1)
    full = {n: _to_natural(parts[i], _SHARD_AXIS[n]) for i, n in enumerate(_LATE)}
    full["gdn_conv"] = full["gdn_conv"].astype(F32) + _to_natural(parts[-1], _SHARD_AXIS["gdn_conv"]).astype(F32)
    W = _prepare_first(full, {n: w[n] for n in _SMALL})
    packed_early = tuple(n for n in _EARLY if n not in _NATIVE)
    rest_send = (_pack([(w[n].astype(BF16), 0) for n in packed_early], _ROW_ALIGN),
                 ) + tuple(w[n].astype(BF16) for n in _NATIVE)

    def rest_weights(arrived):
        filled = _fill_from_sibling(arrived, name="swap_weights")
        by_dev = [a.reshape((N_DEV,) + a.shape[2:]) for a in filled]
        blocks = dict(zip(packed_early, _unpack(by_dev[0], [w[n].shape for n in packed_early], 1)))
        blocks.update(zip(_NATIVE, by_dev[1:]))
        return _prepare_rest({n: _to_natural(blocks[n], _SHARD_AXIS[n]) for n in _EARLY})

    my_c = lax.axis_index("c")

    def pair_sums(g, packed, native, tag):
        sends = [_pack([(_to_blocked(g[n].astype(BF16), _SHARD_AXIS[n]), 1) for n in packed], _BIG_ALIGN)]
        sends += [_to_blocked(g[n].astype(BF16), _SHARD_AXIS[n]) for n in native]
        by_core = [s.reshape((N_DEV // 2, 2) + s.shape[1:]) for s in sends]
        keep = [lax.dynamic_index_in_dim(s, my_c, axis=1, keepdims=False) for s in by_core]
        give = [lax.dynamic_index_in_dim(s, 1 - my_c, axis=1, keepdims=False) for s in by_core]
        got = _swap_with_sibling(give, name=f"swap_grads_{tag}")
        return tuple(_add_pair(k, t, name=f"add_sibling_grads_{tag}_{j}") for j, (k, t) in enumerate(zip(keep, got)))

    sse, grad_x, grads, dmod, early_recv = _local_step(
        x[0], loss_target[0], mod, W, late_weights=(rest_send, rest_weights),
        early_pairs=lambda g: pair_sums(g, packed_early, _NATIVE, "early"))
    loss = lax.psum(0.5 * sse[0, 0] / D, ("x", "y", "c"))
    grads["b_ada"] = dmod.reshape(n_layers, 6 * D)
    late_recv = _exchange_chips(pair_sums(grads, _LATE, (), "late")[0], name="scatter_grads_late")
    g_parts = dict(zip(packed_early, _unpack(early_recv[0], [w[n].shape for n in packed_early], 1)))
    g_parts.update(zip(_NATIVE, early_recv[1:]))
    g_parts.update(zip(_LATE, _unpack(late_recv, [w[n].shape for n in _LATE], 1)))

    dmod_send = _pack([(jnp.transpose(dmod.reshape(n_layers, N_DEV, ada_cols), (1, 0, 2)), 1)], _ROW_ALIGN)
    small_send = _pack([(grads[n].reshape(w[n].shape), 0) for n in _SMALL], _ROW_ALIGN)
    s_recv = _exchange(jnp.concatenate(
        [dmod_send, jnp.broadcast_to(small_send[None], (N_DEV,) + small_send.shape)], axis=1),
        gather=False, name="scatter_small")
    dmod_rows = dmod_send.shape[1]

    out = {}
    kinds = ("grad", "delta", "new_m", "new_v")
    for n in _BIG:
        g4 = g_parts[n]
        rows2d = lambda a: a.reshape((-1, w[n].shape[-1]))
        res = _adamw(rows2d(w[n]), g4.reshape((g4.shape[0], -1, w[n].shape[-1])), rows2d(m[n]), rows2d(v[n]),
                     name=f"adamw_{n}")
        for kind, buf in zip(kinds, res):
            out[kind, n] = buf.reshape(w[n].shape)

    dmod_all = _unpack(lax.slice_in_dim(s_recv, 0, dmod_rows, axis=1), [(n_layers, ada_cols)], 1)[0]
    g_ada = jnp.stack([_mm(c_pad, _pad_rows(dmod_all[:, l], _PAD_BATCH), mode="tn", name=f"ada_dw_{l}",
                           tm=D, tn=ada_cols, tk=_PAD_BATCH, a_silu=True) for l in range(n_layers)])
    flat = lambda a: a.reshape(n_layers * D, ada_cols)
    res = _adamw(flat(w_ada), flat(g_ada)[None], flat(m_w_ada), flat(v_w_ada), name="adamw_ada")
    for kind, buf in zip(("grad", "delta", "new_m", "new_v"), res):
        out[kind, "w_ada"] = buf.reshape(w_ada.shape)

    small_parts = lax.slice_in_dim(s_recv, dmod_rows, s_recv.shape[1], axis=1)
    packed = [_pack([(t[n], 0) for n in _SMALL], _ROW_ALIGN) for t in (w, m, v)]
    res = _adamw(packed[0], small_parts, packed[1], packed[2], name="adamw_replicated")
    for kind, buf in zip(("grad", "delta", "new_m", "new_v"), res):
        for n, a in zip(_SMALL, _unpack(buf, [w[n].shape for n in _SMALL], 0)):
            out[kind, n] = a

    return (loss, grad_x[None]) + tuple(out[kind, n] for kind in ("grad", "delta", "new_m", "new_v")
                                        for n in _W_NAMES)
```

```python
import functools
import math

import numpy as np
import jax
import jax.numpy as jnp
from jax import lax
from jax.experimental import pallas as pl
from jax.experimental.pallas import tpu as pltpu

F32 = jnp.float32
BF16 = jnp.bfloat16

N_DEV = 8
RMS_EPS = 1e-6
LANES = 128
V7X_VMEM_LIMIT = 48 * 1024 * 1024

GDN_HEADS = 8
GDN_DK = 128
GDN_CHUNK = 64
GDN_CONV = 4
DSW_GROUPS = ((128, 1), (512, 4), (2048, 16))
DSW_HEADS = 8
DSW_DH = 64
DSW_BLK = 128
REL_BUCKETS = 32
REL_MAX_DIST = 2048

ADAM_LR = 0.001
ADAM_B1 = 0.9
ADAM_B2 = 0.999
ADAM_EPS = 1e-08
ADAM_WD = 0.01
ADAM_STEP = 10

NEG_BIG = -1e30


def _params(*sem):
    return pltpu.CompilerParams(dimension_semantics=sem, vmem_limit_bytes=V7X_VMEM_LIMIT)


def _sigmoid(x):
    return 1.0 / (1.0 + jnp.exp(-x))


def _silu(x):
    return x * _sigmoid(x)


_DOT_DIMS = {
    "nn": (((1,), (0,)), ((), ())),
    "nt": (((1,), (1,)), ((), ())),
    "tn": (((0,), (0,)), ((), ())),
}


def _mm(a, b, *, mode, name, tm, tn, tk, out_dtype=F32, a_scale=None, out_scale=None, resid=None, a_silu=False):
    if mode == "nn":
        (M, K), N = a.shape, b.shape[1]
    elif mode == "nt":
        (M, K), N = a.shape, b.shape[0]
    else:
        (K, M), N = a.shape, b.shape[1]
    tm, tn, tk = min(tm, M), min(tn, N), min(tk, K)
    assert M % tm == 0 and N % tn == 0 and K % tk == 0, (name, M, N, K, tm, tn, tk)
    nk = K // tk

    def body(*refs):
        refs = list(refs)
        a_ref, b_ref = refs.pop(0), refs.pop(0)
        as_ref = refs.pop(0) if a_scale is not None else None
        os_ref = refs.pop(0) if out_scale is not None else None
        r_ref = refs.pop(0) if resid is not None else None
        o_ref = refs.pop(0)
        acc_ref = refs.pop(0) if nk > 1 else None

        av = a_ref[...]
        if a_silu:
            av = _silu(av.astype(F32))
        if as_ref is not None:
            av = av.astype(F32) * as_ref[...]
        part = lax.dot_general(av.astype(BF16), b_ref[...].astype(BF16), _DOT_DIMS[mode],
                               preferred_element_type=F32)

        def finish(r):
            if os_ref is not None:
                r = r * os_ref[...]
            if r_ref is not None:
                r = r + r_ref[...].astype(F32)
            o_ref[...] = r.astype(out_dtype)

        if nk == 1:
            finish(part)
        else:
            k = pl.program_id(2)

            @pl.when(k == 0)
            def _():
                acc_ref[...] = part

            @pl.when(k > 0)
            def _():
                acc_ref[...] += part

            @pl.when(k == nk - 1)
            def _():
                finish(acc_ref[...])

    if mode == "nn":
        a_spec = pl.BlockSpec((tm, tk), lambda i, j, k: (i, k))
        b_spec = pl.BlockSpec((tk, tn), lambda i, j, k: (k, j))
        as_spec = pl.BlockSpec((1, tk), lambda i, j, k: (0, k))
    elif mode == "nt":
        a_spec = pl.BlockSpec((tm, tk), lambda i, j, k: (i, k))
        b_spec = pl.BlockSpec((tn, tk), lambda i, j, k: (j, k))
        as_spec = pl.BlockSpec((1, tk), lambda i, j, k: (0, k))
    else:
        a_spec = pl.BlockSpec((tk, tm), lambda i, j, k: (k, i))
        b_spec = pl.BlockSpec((tk, tn), lambda i, j, k: (k, j))
        as_spec = None
    in_specs, args = [a_spec, b_spec], [a, b]
    if a_scale is not None:
        in_specs.append(as_spec)
        args.append(a_scale)
    if out_scale is not None:
        in_specs.append(pl.BlockSpec((1, tn), lambda i, j, k: (0, j)))
        args.append(out_scale)
    if resid is not None:
        in_specs.append(pl.BlockSpec((tm, tn), lambda i, j, k: (i, j)))
        args.append(resid)
    return pl.pallas_call(
        body, name=name, grid=(M // tm, N // tn, nk),
        in_specs=in_specs, out_specs=pl.BlockSpec((tm, tn), lambda i, j, k: (i, j)),
        out_shape=jax.ShapeDtypeStruct((M, N), out_dtype),
        scratch_shapes=[pltpu.VMEM((tm, tn), F32)] if nk > 1 else [],
        compiler_params=_params("parallel", "parallel", "arbitrary"),
    )(*args)


_MM_ROWS = 1024


def _mm_sum_nt(pairs, *, name, tm=_MM_ROWS, tn=1024):
    M, N = pairs[0][0].shape[0], pairs[0][1].shape[0]
    tm, tn = _tile(M, tm), _tile(N, tn)
    spans, start = [], 0
    for a, b, tk, off in pairs:
        K = a.shape[1]
        assert a.shape[0] == M and b.shape[0] == N and K % tk == 0 and off % tk == 0, name
        spans.append((start, K // tk, tk, off // tk))
        start += K // tk
    total = start

    def body(*refs):
        o_ref, acc_ref = refs[-2:]
        k = pl.program_id(2)

        @pl.when(k == 0)
        def _():
            acc_ref[...] = jnp.zeros_like(acc_ref)

        for p, (s0, nk, _, _) in enumerate(spans):
            a_ref, b_ref = refs[2 * p], refs[2 * p + 1]

            @pl.when((k >= s0) & (k < s0 + nk))
            def _():
                acc_ref[...] += lax.dot_general(a_ref[...].astype(BF16), b_ref[...].astype(BF16), _DOT_DIMS["nt"],
                                                preferred_element_type=F32)

        @pl.when(k == total - 1)
        def _():
            o_ref[...] = acc_ref[...]

    def spec(rows, tk, s0, nk, koff, axis):
        def index(i, j, k):
            return ((i, j)[axis], jnp.clip(k - s0, 0, nk - 1) + koff)
        return pl.BlockSpec((rows, tk), index)

    in_specs, args = [], []
    for (a, b, _, _), (s0, nk, tk, koff) in zip(pairs, spans):
        in_specs += [spec(tm, tk, s0, nk, 0, 0), spec(tn, tk, s0, nk, koff, 1)]
        args += [a, b]
    return pl.pallas_call(
        body, name=name, grid=(M // tm, N // tn, total), in_specs=in_specs,
        out_specs=pl.BlockSpec((tm, tn), lambda i, j, k: (i, j)),
        out_shape=jax.ShapeDtypeStruct((M, N), F32), scratch_shapes=[pltpu.VMEM((tm, tn), F32)],
        compiler_params=_params("parallel", "parallel", "arbitrary"),
    )(*args)


def _norm_mod_fwd(x, gain, sc, sh, *, name):
    S, D = x.shape
    tr = min(512, S)

    def body(x_ref, g_ref, sc_ref, sh_ref, h_ref):
        xv = x_ref[...]
        r = lax.rsqrt(jnp.mean(xv * xv, axis=-1, keepdims=True) + RMS_EPS)
        h_ref[...] = ((xv * r) * g_ref[...] * (1.0 + sc_ref[...]) + sh_ref[...]).astype(BF16)

    row = pl.BlockSpec((tr, D), lambda i: (i, 0))
    vec = pl.BlockSpec((1, D), lambda i: (0, 0))
    return pl.pallas_call(
        body, name=name, grid=(S // tr,), in_specs=[row, vec, vec, vec], out_specs=row,
        out_shape=jax.ShapeDtypeStruct((S, D), BF16), compiler_params=_params("parallel"),
    )(x, gain, sc, sh)


def _norm_mod_bwd(dh, x, dx_res, gain, sc, *, name):
    S, D = x.shape
    tr = min(512, S)
    n_steps = S // tr

    def body(dh_ref, x_ref, dxr_ref, g_ref, sc_ref, dx_ref, dsh_ref, dsc_ref, dgain_ref, acc_sh, acc_a):
        i = pl.program_id(0)
        xv = x_ref[...]
        r = lax.rsqrt(jnp.mean(xv * xv, axis=-1, keepdims=True) + RMS_EPS)
        n = xv * r
        dhv = dh_ref[...].astype(F32)
        dn = dhv * (g_ref[...] * (1.0 + sc_ref[...]))
        dx_ref[...] = dxr_ref[...] + r * (dn - n * jnp.mean(dn * n, axis=-1, keepdims=True))
        p_sh = jnp.sum(dhv, axis=0, keepdims=True)
        p_a = jnp.sum(dhv * n, axis=0, keepdims=True)

        @pl.when(i == 0)
        def _():
            acc_sh[...] = p_sh
            acc_a[...] = p_a

        @pl.when(i > 0)
        def _():
            acc_sh[...] += p_sh
            acc_a[...] += p_a

        @pl.when(i == n_steps - 1)
        def _():
            dsh_ref[...] = acc_sh[...]
            dsc_ref[...] = acc_a[...] * g_ref[...]
            dgain_ref[...] = acc_a[...] * (1.0 + sc_ref[...])

    row = pl.BlockSpec((tr, D), lambda i: (i, 0))
    vec = pl.BlockSpec((1, D), lambda i: (0, 0))
    vshape = jax.ShapeDtypeStruct((1, D), F32)
    return pl.pallas_call(
        body, name=name, grid=(n_steps,), in_specs=[row, row, row, vec, vec],
        out_specs=[row, vec, vec, vec],
        out_shape=[jax.ShapeDtypeStruct((S, D), F32), vshape, vshape, vshape],
        scratch_shapes=[pltpu.VMEM((1, D), F32), pltpu.VMEM((1, D), F32)],
        compiler_params=_params("arbitrary"),
    )(dh, x, dx_res, gain, sc)


def _wout_grad(gmat, w, gate, *, name):
    K, D = w.shape
    tr = min(256, K)
    n_steps = K // tr

    def body(g_ref, w_ref, gate_ref, dw_ref, dgate_ref, acc):
        i = pl.program_id(0)
        gv = g_ref[...]
        dw_ref[...] = (gv * gate_ref[...]).astype(BF16)
        part = jnp.sum(gv * w_ref[...], axis=0, keepdims=True)

        @pl.when(i == 0)
        def _():
            acc[...] = part

        @pl.when(i > 0)
        def _():
            acc[...] += part

        @pl.when(i == n_steps - 1)
        def _():
            dgate_ref[...] = acc[...]

    row = pl.BlockSpec((tr, D), lambda i: (i, 0))
    vec = pl.BlockSpec((1, D), lambda i: (0, 0))
    return pl.pallas_call(
        body, name=name, grid=(n_steps,), in_specs=[row, row, vec], out_specs=[row, vec],
        out_shape=[jax.ShapeDtypeStruct((K, D), BF16), jax.ShapeDtypeStruct((1, D), F32)],
        scratch_shapes=[pltpu.VMEM((1, D), F32)], compiler_params=_params("arbitrary"),
    )(gmat, w, gate)


def _loss_head(y, target, *, name):
    S, D = y.shape
    tr = min(512, S)
    n_steps = S // tr

    def body(y_ref, t_ref, dy_ref, sse_ref, acc):
        i = pl.program_id(0)
        e = y_ref[...] - t_ref[...]
        dy_ref[...] = e * (1.0 / D)
        part = jnp.sum(e * e, axis=0, keepdims=True)

        @pl.when(i == 0)
        def _():
            acc[...] = part

        @pl.when(i > 0)
        def _():
            acc[...] += part

        @pl.when(i == n_steps - 1)
        def _():
            sse_ref[...] = jnp.sum(acc[...], axis=1, keepdims=True)

    row = pl.BlockSpec((tr, D), lambda i: (i, 0))
    return pl.pallas_call(
        body, name=name, grid=(n_steps,), in_specs=[row, row],
        out_specs=[row, pl.BlockSpec((1, 1), lambda i: (0, 0))],
        out_shape=[jax.ShapeDtypeStruct((S, D), F32), jax.ShapeDtypeStruct((1, 1), F32)],
        scratch_shapes=[pltpu.VMEM((1, D), F32)], compiler_params=_params("arbitrary"),
    )(y, target)


def _adamw(w, g_parts, m, v, *, name):
    R, C = w.shape
    P = g_parts.shape[0]
    tr = _tile(R, max(8, 4096 * LANES // C))
    c1 = 1.0 / (1.0 - ADAM_B1 ** ADAM_STEP)
    c2 = 1.0 / (1.0 - ADAM_B2 ** ADAM_STEP)

    def body(w_ref, g_ref, m_ref, v_ref, go_ref, d_ref, mo_ref, vo_ref):
        g = g_ref[0].astype(F32)
        for q in range(1, P):
            g = g + g_ref[q].astype(F32)
        mn = ADAM_B1 * m_ref[...] + (1.0 - ADAM_B1) * g
        vn = ADAM_B2 * v_ref[...] + (1.0 - ADAM_B2) * (g * g)
        go_ref[...] = g
        mo_ref[...] = mn
        vo_ref[...] = vn
        d_ref[...] = -ADAM_LR * ((mn * c1) / (jnp.sqrt(vn * c2) + ADAM_EPS) + ADAM_WD * w_ref[...])

    row = pl.BlockSpec((tr, C), lambda i: (i, 0))
    shp = jax.ShapeDtypeStruct((R, C), F32)
    return pl.pallas_call(
        body, name=name, grid=(R // tr,),
        in_specs=[row, pl.BlockSpec((P, tr, C), lambda i: (0, i, 0)), row, row],
        out_specs=[row, row, row, row], out_shape=[shp, shp, shp, shp],
        compiler_params=_params("parallel"),
    )(w, g_parts, m, v)


_HALO = 16


def _conv_taps(buf, w_ref, rows, cols):
    acc = None
    for j in range(GDN_CONV):
        term = buf[pl.ds(_HALO - (GDN_CONV - 1) + j, rows), cols] * w_ref[j:j + 1, cols]
        acc = term if acc is None else acc + term
    return acc


def _fill_conv_buf(buf, halo_ref, x_ref, rows, first):
    buf[0:_HALO, :] = jnp.where(first, 0.0, halo_ref[...].astype(F32))
    buf[_HALO:_HALO + rows, :] = x_ref[...].astype(F32)


_HM = 3 * GDN_DK
_GDN_ROWS = 256
_PREP_HEADS = 4


def _l2n(seg):
    return lax.rsqrt(jnp.sum(seg * seg, axis=-1, keepdims=True) + RMS_EPS)


def _head_cols(hh):
    return slice(hh * _HM, (hh + 1) * _HM)


def _gdn_prep_fwd(x, conv_w, *, name):
    S, C3 = x.shape
    CB = _PREP_HEADS * _HM
    RB = min(512, S)

    def body(x_ref, halo_ref, w_ref, o_ref, buf):
        i = pl.program_id(0)
        _fill_conv_buf(buf, halo_ref, x_ref, RB, i == 0)
        for hh in range(_PREP_HEADS):
            c0 = hh * _HM
            y = _silu(_conv_taps(buf, w_ref, RB, _head_cols(hh)))
            q, k = y[:, :GDN_DK], y[:, GDN_DK:2 * GDN_DK]
            o_ref[:, c0:c0 + GDN_DK] = q * (_l2n(q) * GDN_DK ** -0.5)
            o_ref[:, c0 + GDN_DK:c0 + 2 * GDN_DK] = k * _l2n(k)
            o_ref[:, c0 + 2 * GDN_DK:c0 + _HM] = y[:, 2 * GDN_DK:]

    hb = RB // _HALO
    return pl.pallas_call(
        body, name=name, grid=(S // RB, C3 // CB),
        in_specs=[pl.BlockSpec((RB, CB), lambda i, j: (i, j)),
                  pl.BlockSpec((_HALO, CB), lambda i, j: (jnp.maximum(i * hb - 1, 0), j)),
                  pl.BlockSpec((GDN_CONV, CB), lambda i, j: (0, j))],
        out_specs=pl.BlockSpec((RB, CB), lambda i, j: (i, j)),
        out_shape=jax.ShapeDtypeStruct((S, C3), F32),
        scratch_shapes=[pltpu.VMEM((RB + _HALO, CB), F32)],
        compiler_params=_params("parallel", "parallel"),
    )(x, x, conv_w)


def _gdn_prep_bwd_pre(dn, x, conv_w, *, name):
    S, C3 = x.shape
    CB = _PREP_HEADS * _HM
    RB = min(512, S)
    n_steps = S // RB

    def body(dn_ref, x_ref, halo_ref, w_ref, dc_ref, dw_ref, buf):
        i = pl.program_id(1)
        _fill_conv_buf(buf, halo_ref, x_ref, RB, i == 0)
        head_parts = []
        for hh in range(_PREP_HEADS):
            c0, cols = hh * _HM, _head_cols(hh)
            acc = _conv_taps(buf, w_ref, RB, cols)
            sg = _sigmoid(acc)
            y = acc * sg
            dsilu = sg * (1.0 + acc * (1.0 - sg))
            for part, scale in ((0, GDN_DK ** -0.5), (1, 1.0)):
                sl = slice(part * GDN_DK, (part + 1) * GDN_DK)
                seg = y[:, sl]
                r = _l2n(seg)
                n = seg * r
                d = dn_ref[:, c0 + part * GDN_DK:c0 + (part + 1) * GDN_DK] * scale
                dc_ref[:, c0 + part * GDN_DK:c0 + (part + 1) * GDN_DK] = (
                    r * (d - n * jnp.sum(d * n, axis=-1, keepdims=True)) * dsilu[:, sl])
            dc_ref[:, c0 + 2 * GDN_DK:c0 + _HM] = dn_ref[:, c0 + 2 * GDN_DK:c0 + _HM] * dsilu[:, 2 * GDN_DK:]
            dc = dc_ref[:, cols]
            taps = [jnp.sum(dc * buf[pl.ds(_HALO - (GDN_CONV - 1) + t, RB), cols], axis=0, keepdims=True)
                    for t in range(GDN_CONV)]
            head_parts.append(jnp.concatenate(taps + [jnp.zeros((8 - GDN_CONV, _HM), F32)], axis=0))
        part = jnp.concatenate(head_parts, axis=1)

        @pl.when(i == 0)
        def _():
            dw_ref[...] = part

        @pl.when(i > 0)
        def _():
            dw_ref[...] += part

    hb = RB // _HALO
    return pl.pallas_call(
        body, name=name, grid=(C3 // CB, n_steps),
        in_specs=[pl.BlockSpec((RB, CB), lambda j, i: (i, j)),
                  pl.BlockSpec((RB, CB), lambda j, i: (i, j)),
                  pl.BlockSpec((_HALO, CB), lambda j, i: (jnp.maximum(i * hb - 1, 0), j)),
                  pl.BlockSpec((GDN_CONV, CB), lambda j, i: (0, j))],
        out_specs=[pl.BlockSpec((RB, CB), lambda j, i: (i, j)),
                   pl.BlockSpec((8, CB), lambda j, i: (0, j))],
        out_shape=[jax.ShapeDtypeStruct((S, C3), F32), jax.ShapeDtypeStruct((8, C3), F32)],
        scratch_shapes=[pltpu.VMEM((RB + _HALO, CB), F32)],
        compiler_params=_params("parallel", "arbitrary"),
    )(dn, x, x, conv_w)


def _gdn_conv_bwd_x(dc, conv_w, *, name):
    S, C3 = dc.shape
    CB = _PREP_HEADS * _HM
    RB = min(512, S)
    n_steps = S // RB

    def body(dc_ref, halo_ref, w_ref, dx_ref, buf):
        i = pl.program_id(0)
        buf[0:RB, :] = dc_ref[...]
        buf[RB:RB + _HALO, :] = jnp.where(i == n_steps - 1, 0.0, halo_ref[...])
        for hh in range(_PREP_HEADS):
            cols = _head_cols(hh)
            acc = None
            for j in range(GDN_CONV):
                term = buf[pl.ds(GDN_CONV - 1 - j, RB), cols] * w_ref[j:j + 1, cols]
                acc = term if acc is None else acc + term
            dx_ref[:, cols] = acc.astype(BF16)

    hb = RB // _HALO
    last = S // _HALO - 1
    return pl.pallas_call(
        body, name=name, grid=(n_steps, C3 // CB),
        in_specs=[pl.BlockSpec((RB, CB), lambda i, j: (i, j)),
                  pl.BlockSpec((_HALO, CB), lambda i, j: (jnp.minimum((i + 1) * hb, last), j)),
                  pl.BlockSpec((GDN_CONV, CB), lambda i, j: (0, j))],
        out_specs=pl.BlockSpec((RB, CB), lambda i, j: (i, j)),
        out_shape=jax.ShapeDtypeStruct((S, C3), BF16),
        scratch_shapes=[pltpu.VMEM((RB + _HALO, CB), F32)],
        compiler_params=_params("parallel", "parallel"),
    )(dc, dc, conv_w)


def _split_bf16(a):
    hi = a.astype(BF16)
    return hi, (a - hi.astype(F32)).astype(BF16)


def _dot(a, b, dims="nn", exact=False):
    def dot(p, q):
        return lax.dot_general(p, q, _DOT_DIMS[dims], preferred_element_type=F32)

    if exact:
        (ah, al), (bh, bl) = _split_bf16(a), _split_bf16(b)
        return dot(ah, bh) + (dot(ah, bl) + dot(al, bh))
    return dot(a.astype(BF16), b.astype(BF16))


def _softplus(x):
    return jnp.maximum(x, 0.0) + jnp.log(1.0 + jnp.exp(-jnp.abs(x)))


def _to_col(row, eye):
    return jnp.sum(jnp.where(eye, row, 0.0), axis=1, keepdims=True)


def _to_row(col, eye):
    return jnp.sum(jnp.where(eye, col, 0.0), axis=0, keepdims=True)


def _unit_lower_inverse(low, ri, ci):
    n = range(len(low))
    C = low[0].shape[0]
    eye = jnp.where(ri == ci, 1.0, 0.0)
    pair = (ri >> 1) == (ci >> 1)
    x = [eye - jnp.where(pair, low[j], 0.0) for j in n]
    m, sh = 2, 1
    while m < C:
        join = ((ri >> (sh + 1)) == (ci >> (sh + 1))) & (((ri >> sh) & 1) == 1) & (((ci >> sh) & 1) == 0)
        y = [_dot(x[j], jnp.where(join, low[j], 0.0)) for j in n]
        x = [x[j] - _dot(y[j], x[j]) for j in n]
        m, sh = 2 * m, sh + 1
    lx = [_dot(low[j], x[j], exact=True) for j in n]
    corr = [_dot(x[j], eye - x[j] - lx[j]) for j in n]
    return [x[j] + corr[j] for j in n]


def _gdn_local_batch(qkv, g_row, beta_row, ri, ci):
    n = range(len(qkv))
    eye, tril, strict = ri == ci, ri >= ci, ri > ci
    q = [qkv[j][:, :GDN_DK] for j in n]
    k = [qkv[j][:, GDN_DK:2 * GDN_DK] for j in n]
    v = [qkv[j][:, 2 * GDN_DK:] for j in n]
    g_col = [_to_col(g_row[j], eye) for j in n]
    beta_col = [_to_col(beta_row[j], eye) for j in n]
    gc_col = [jnp.sum(jnp.where(tril, g_row[j], 0.0), axis=1, keepdims=True) for j in n]
    gc_row = [jnp.sum(jnp.where(ri <= ci, g_col[j], 0.0), axis=0, keepdims=True) for j in n]
    g_last = [jnp.sum(g_row[j], axis=1, keepdims=True) for j in n]
    decay = [jnp.where(tril, jnp.exp(jnp.minimum(gc_col[j] - gc_row[j], 0.0)), 0.0) for j in n]
    e_col = [jnp.exp(gc_col[j]) for j in n]
    f_col = [jnp.exp(g_last[j] - gc_col[j]) for j in n]
    e_last = [jnp.exp(g_last[j]) for j in n]
    kb = [k[j] * beta_col[j] for j in n]
    vb = [v[j] * beta_col[j] for j in n]
    kk = [_dot(kb[j], k[j], "nt") for j in n]
    qk = [_dot(q[j], k[j], "nt") for j in n]
    low = [jnp.where(strict, kk[j] * decay[j], 0.0) for j in n]
    att = [qk[j] * decay[j] for j in n]
    return dict(q=q, k=k, v=v, beta_col=beta_col, decay=decay, e_col=e_col, f_col=f_col, e_last=e_last,
                kb=kb, vb=vb, low=low, att=att, eye=eye, strict=strict, tril=tril)


def _chunk_iotas():
    C = GDN_CHUNK
    return lax.broadcasted_iota(jnp.int32, (C, C), 0), lax.broadcasted_iota(jnp.int32, (C, C), 1)


def _gdn_chunk_fwd(qkv, ab, a_log, dt_bias, *, name, riding=None):
    S = qkv.shape[0]
    H, C, DK = GDN_HEADS, GDN_CHUNK, GDN_DK
    RB = min(_GDN_ROWS, S)
    NCB, NB, NC = RB // C, S // RB, S // C
    heads = range(H)

    def body(qkv_ref, ab_ref, alog_ref, dtb_ref, *rest):
        n_ride = 0 if riding is None else len(riding)
        ride_srcs, rest = rest[:n_ride], rest[n_ride:]
        (o_ref, st_ref, t_ref), rest = rest[:3], rest[3:]
        ride_dsts, rest = rest[:n_ride], rest[n_ride:]
        state, u_s, w_s, qe_s, kf_s, att_s, *ride_sems = rest
        nb = pl.program_id(0)
        if riding is not None:
            finish_ride = _ride(nb == 0, nb == NB - 1, ride_srcs, ride_dsts, ride_sems, True)

        @pl.when(nb == 0)
        def _():
            state[...] = jnp.zeros_like(state)

        ri, ci = _chunk_iotas()
        neg_a = [-jnp.exp(alog_ref[h]) for h in heads]
        e_last = []
        for c in range(NCB):
            rows = pl.ds(c * C, C)
            g_row = [neg_a[h] * _softplus(ab_ref[h, c] + dtb_ref[h]) for h in heads]
            beta_row = [_sigmoid(ab_ref[H + h, c]) for h in heads]
            L = _gdn_local_batch([qkv_ref[rows, h * _HM:(h + 1) * _HM] for h in heads], g_row, beta_row, ri, ci)
            tinv = _unit_lower_inverse(L["low"], ri, ci)
            u = [_dot(tinv[h], L["vb"][h], exact=True) for h in heads]
            w = [_dot(tinv[h], L["kb"][h] * L["e_col"][h], exact=True) for h in heads]
            for h in heads:
                t_ref[h, c] = tinv[h]
                u_s[c, h] = u[h]
                w_s[c, h] = w[h].astype(BF16)
                qe_s[c, h] = (L["q"][h] * L["e_col"][h]).astype(BF16)
                kf_s[c, h] = (L["k"][h] * L["f_col"][h]).astype(BF16)
                att_s[c, h] = L["att"][h].astype(BF16)
            e_last.append(L["e_last"])
        st = [state[h] for h in heads]
        for c in range(NCB):
            rows = pl.ds(c * C, C)
            stb = [st[h].astype(BF16) for h in heads]
            vn = [u_s[c, h] - _dot(w_s[c, h], stb[h]) for h in heads]
            vnb = [vn[h].astype(BF16) for h in heads]
            out = [_dot(qe_s[c, h], stb[h]) + _dot(att_s[c, h], vnb[h]) for h in heads]
            new = [st[h] * e_last[c][h] + _dot(kf_s[c, h], vnb[h], "tn") for h in heads]
            for h in heads:
                o_ref[rows, h * DK:(h + 1) * DK] = out[h]
                st_ref[h, c] = st[h]
            st = new
        for h in heads:
            state[h] = st[h]
        if riding is not None:
            finish_ride()

    ride_args, ride_specs, ride_out, ride_scratch = _riding(riding, True)
    return pl.pallas_call(
        body, name=name, grid=(NB,),
        in_specs=[pl.BlockSpec((RB, H * _HM), lambda n: (n, 0)),
                  pl.BlockSpec((2 * H, NCB, 1, C), lambda n: (0, n, 0, 0)),
                  pl.BlockSpec((H, 1, 1), lambda n: (0, 0, 0)),
                  pl.BlockSpec((H, 1, 1), lambda n: (0, 0, 0))] + ride_specs,
        out_specs=[pl.BlockSpec((RB, H * DK), lambda n: (n, 0)),
                   pl.BlockSpec((H, NCB, DK, DK), lambda n: (0, n, 0, 0)),
                   pl.BlockSpec((H, NCB, C, C), lambda n: (0, n, 0, 0))] + ride_specs,
        out_shape=[jax.ShapeDtypeStruct((S, H * DK), F32),
                   jax.ShapeDtypeStruct((H, NC, DK, DK), F32),
                   jax.ShapeDtypeStruct((H, NC, C, C), F32)] + ride_out,
        scratch_shapes=[pltpu.VMEM((H, DK, DK), F32), pltpu.VMEM((NCB, H, C, DK), F32),
                        pltpu.VMEM((NCB, H, C, DK), BF16), pltpu.VMEM((NCB, H, C, DK), BF16),
                        pltpu.VMEM((NCB, H, C, DK), BF16), pltpu.VMEM((NCB, H, C, C), BF16)] + ride_scratch,
        compiler_params=_params("arbitrary"),
    )(qkv, ab, a_log, dt_bias, *ride_args)


_CHIP_PEERS = N_DEV // 2 - 1


def _chip_copies(src_refs, dst_refs, send_sems, recv_sems, local_sems, gather=False):
    x, y, c = lax.axis_index("x"), lax.axis_index("y"), lax.axis_index("c")
    here = 2 * x + y
    copies = []
    for a, (src_ref, dst_ref) in enumerate(zip(src_refs, dst_refs)):
        landing = dst_ref.at[here, c] if gather else dst_ref.at[here]
        copies.append(pltpu.make_async_copy(src_ref if gather else src_ref.at[here], landing, local_sems.at[a]))
        for rel in range(1, N_DEV // 2):
            px = 1 - x if rel & 2 else x
            py = 1 - y if rel & 1 else y
            k = a * _CHIP_PEERS + rel - 1
            copies.append(pltpu.make_async_remote_copy(
                src_ref=src_ref if gather else src_ref.at[2 * px + py], dst_ref=landing,
                send_sem=send_sems.at[k], recv_sem=recv_sems.at[k],
                device_id=(px, py, c), device_id_type=pl.DeviceIdType.MESH))
    return copies


def _chip_sems(n):
    return [pltpu.SemaphoreType.DMA((n * _CHIP_PEERS,)), pltpu.SemaphoreType.DMA((n * _CHIP_PEERS,)),
            pltpu.SemaphoreType.DMA((n,))]


def _riding(riding, gather):
    if riding is None:
        return [], [], [], []
    shapes = [jax.ShapeDtypeStruct(((N_DEV // 2, 2) + r.shape) if gather else r.shape, r.dtype) for r in riding]
    return list(riding), [pl.BlockSpec(memory_space=pl.ANY)] * len(riding), shapes, _chip_sems(len(riding))


def _ride(first, last, srcs, dsts, sems, gather):
    @pl.when(first)
    def _():
        for cp in _chip_copies(srcs, dsts, *sems, gather=gather):
            cp.start()

    def finish():
        @pl.when(last)
        def _():
            for cp in _chip_copies(srcs, dsts, *sems, gather=gather):
                cp.wait()

    return finish


def _gdn_chunk_bwd(qkv, ab, a_log, dt_bias, states, tinvs, do, *, name, riding=None):
    S = qkv.shape[0]
    H, C, DK = GDN_HEADS, GDN_CHUNK, GDN_DK
    RB = min(_GDN_ROWS, S)
    NCB, NB, NC = RB // C, S // RB, S // C
    heads = range(H)

    def body(qkv_ref, ab_ref, alog_ref, dtb_ref, st_ref, t_ref, do_ref, *rest):
        n_ride = 0 if riding is None else len(riding)
        ride_srcs, rest = rest[:n_ride], rest[n_ride:]
        (dqkv_ref, dab_ref, dalog_ref, ddtb_ref), rest = rest[:4], rest[4:]
        ride_dsts, rest = rest[:n_ride], rest[n_ride:]
        dstate, w_s, vn_s, qe_s, kf_s, att_s, dvn_s, dkf_s, *ride_sems = rest
        nb = pl.program_id(0)
        if riding is not None:
            finish_ride = _ride(nb == 0, nb == NB - 1, ride_srcs, ride_dsts, ride_sems, False)

        @pl.when(nb == 0)
        def _():
            dstate[...] = jnp.zeros_like(dstate)
            dalog_ref[...] = jnp.zeros_like(dalog_ref)
            ddtb_ref[...] = jnp.zeros_like(ddtb_ref)

        ri, ci = _chunk_iotas()
        neg_a = [-jnp.exp(alog_ref[h]) for h in heads]

        def local(c):
            rows = pl.ds(c * C, C)
            a_pre = [ab_ref[h, c] + dtb_ref[h] for h in heads]
            g_row = [neg_a[h] * _softplus(a_pre[h]) for h in heads]
            beta_row = [_sigmoid(ab_ref[H + h, c]) for h in heads]
            L = _gdn_local_batch([qkv_ref[rows, h * _HM:(h + 1) * _HM] for h in heads], g_row, beta_row, ri, ci)
            return L, a_pre, g_row, beta_row

        e_last = [None] * NCB
        for c in range(NCB):
            L, _, _, _ = local(c)
            kbe = [L["kb"][h] * L["e_col"][h] for h in heads]
            u = [_dot(t_ref[h, c], L["vb"][h], exact=True) for h in heads]
            w = [_dot(t_ref[h, c], kbe[h], exact=True) for h in heads]
            vn = [u[h] - _dot(w[h], st_ref[h, c]) for h in heads]
            for h in heads:
                w_s[c, h] = w[h].astype(BF16)
                vn_s[c, h] = vn[h].astype(BF16)
                qe_s[c, h] = (L["q"][h] * L["e_col"][h]).astype(BF16)
                kf_s[c, h] = (L["k"][h] * L["f_col"][h]).astype(BF16)
                att_s[c, h] = L["att"][h].astype(BF16)
            e_last[c] = L["e_last"]

        dst = [dstate[h] for h in heads]
        de_last = [None] * NCB
        for c in reversed(range(NCB)):
            rows = pl.ds(c * C, C)
            dob = [do_ref[rows, h * DK:(h + 1) * DK].astype(BF16) for h in heads]
            dstb = [dst[h].astype(BF16) for h in heads]
            dvn = [_dot(att_s[c, h], dob[h], "tn") + _dot(kf_s[c, h], dstb[h]) for h in heads]
            dkf = [_dot(vn_s[c, h], dstb[h], "nt") for h in heads]
            de_last[c] = [jnp.sum(jnp.sum(dst[h] * st_ref[h, c], axis=1, keepdims=True), axis=0, keepdims=True)
                          for h in heads]
            new = [dst[h] * e_last[c][h] + _dot(qe_s[c, h], dob[h], "tn")
                   - _dot(w_s[c, h], dvn[h].astype(BF16), "tn") for h in heads]
            for h in heads:
                dvn_s[c, h] = dvn[h]
                dkf_s[c, h] = dkf[h]
            dst = new
        for h in heads:
            dstate[h] = dst[h]

        for c in range(NCB):
            rows = pl.ds(c * C, C)
            L, a_pre, g_row, beta_row = local(c)
            q, k, v, kb, vb = L["q"], L["k"], L["v"], L["kb"], L["vb"]
            e_col, f_col, decay, beta_col = L["e_col"], L["f_col"], L["decay"], L["beta_col"]
            eye, strict, tril = L["eye"], L["strict"], L["tril"]
            tinv = [t_ref[h, c] for h in heads]
            stb = [st_ref[h, c].astype(BF16) for h in heads]
            dov = [do_ref[rows, h * DK:(h + 1) * DK] for h in heads]
            dvn = [dvn_s[c, h] for h in heads]
            dkf = [dkf_s[c, h] for h in heads]
            kbe = [kb[h] * e_col[h] for h in heads]
            datt = [jnp.where(tril, _dot(dov[h], vn_s[c, h], "nt"), 0.0) for h in heads]
            dqe = [_dot(dov[h], stb[h], "nt") for h in heads]
            dw = [-_dot(dvn[h], stb[h], "nt") for h in heads]
            dt = [_dot(dvn[h], vb[h], "nt") + _dot(dw[h], kbe[h], "nt") for h in heads]
            dvb = [_dot(tinv[h], dvn[h], "tn", exact=True) for h in heads]
            dkbe = [_dot(tinv[h], dw[h], "tn", exact=True) for h in heads]
            tdt = [_dot(tinv[h], dt[h], "tn", exact=True) for h in heads]
            dlow = [-jnp.where(strict, _dot(tdt[h], tinv[h], "nt", exact=True), 0.0) for h in heads]
            dkk = [dlow[h] * decay[h] for h in heads]
            dqk = [datt[h] * decay[h] for h in heads]
            dkb = [_dot(dkk[h], k[h]) + dkbe[h] * e_col[h] for h in heads]
            dk = [_dot(dkk[h], kb[h], "tn") + _dot(dqk[h], q[h], "tn") + dkf[h] * f_col[h] + dkb[h] * beta_col[h]
                  for h in heads]
            dq = [_dot(dqk[h], k[h]) + dqe[h] * e_col[h] for h in heads]
            for h in heads:
                dqkv_ref[rows, h * _HM:h * _HM + DK] = dq[h]
                dqkv_ref[rows, h * _HM + DK:h * _HM + 2 * DK] = dk[h]
                dqkv_ref[rows, h * _HM + 2 * DK:(h + 1) * _HM] = dvb[h] * beta_col[h]

            dbeta_col = [jnp.sum(k[h] * dkb[h] + v[h] * dvb[h], axis=1, keepdims=True) for h in heads]
            pmat = [dlow[h] * L["low"][h] + datt[h] * L["att"][h] for h in heads]
            df_col = [jnp.sum(k[h] * dkf[h], axis=1, keepdims=True) * f_col[h] for h in heads]
            dgc_col = [jnp.sum(pmat[h], axis=1, keepdims=True)
                       + jnp.sum(q[h] * dqe[h] + kb[h] * dkbe[h], axis=1, keepdims=True) * e_col[h] - df_col[h]
                       for h in heads]
            dgc_row = [_to_row(dgc_col[h], eye) - jnp.sum(pmat[h], axis=0, keepdims=True) for h in heads]
            dg_last = [jnp.sum(df_col[h], axis=0, keepdims=True) + de_last[c][h] * L["e_last"][h] for h in heads]
            dgc_c = [_to_col(dgc_row[h], eye) for h in heads]
            dg_row = [jnp.sum(jnp.where(ri >= ci, dgc_c[h], 0.0), axis=0, keepdims=True) + dg_last[h] for h in heads]
            dbeta_row = [_to_row(dbeta_col[h], eye) for h in heads]
            for h in heads:
                da_row = dg_row[h] * neg_a[h] * _sigmoid(a_pre[h])
                dab_ref[h, c] = da_row
                dab_ref[H + h, c] = dbeta_row[h] * beta_row[h] * (1.0 - beta_row[h])
                dalog_ref[h] += jnp.sum(dg_row[h] * g_row[h], axis=1, keepdims=True)
                ddtb_ref[h] += jnp.sum(da_row, axis=1, keepdims=True)

        if riding is not None:
            finish_ride()

    rev = lambda n: NB - 1 - n
    vec = pl.BlockSpec((H, 1, 1), lambda n: (0, 0, 0))
    gates = pl.BlockSpec((2 * H, NCB, 1, C), lambda n: (0, rev(n), 0, 0))
    wide = pl.BlockSpec((RB, H * _HM), lambda n: (rev(n), 0))
    item = lambda dt: pltpu.VMEM((NCB, H, C, DK), dt)
    ride_args, ride_specs, ride_out, ride_scratch = _riding(riding, False)
    return pl.pallas_call(
        body, name=name, grid=(NB,),
        in_specs=[wide, gates, vec, vec,
                  pl.BlockSpec((H, NCB, DK, DK), lambda n: (0, rev(n), 0, 0)),
                  pl.BlockSpec((H, NCB, C, C), lambda n: (0, rev(n), 0, 0)),
                  pl.BlockSpec((RB, H * DK), lambda n: (rev(n), 0))] + ride_specs,
        out_specs=[wide, gates, vec, vec] + ride_specs,
        out_shape=[jax.ShapeDtypeStruct((S, H * _HM), F32),
                   jax.ShapeDtypeStruct((2 * H, NC, 1, C), F32),
                   jax.ShapeDtypeStruct((H, 1, 1), F32),
                   jax.ShapeDtypeStruct((H, 1, 1), F32)] + ride_out,
        scratch_shapes=[pltpu.VMEM((H, DK, DK), F32), item(BF16), item(BF16), item(BF16), item(BF16),
                        pltpu.VMEM((NCB, H, C, C), BF16), item(F32), item(F32)] + ride_scratch,
        compiler_params=_params("arbitrary"),
    )(qkv, ab, a_log, dt_bias, states, tinvs, do, *ride_args)


def _gdn_outnorm_fwd(o, z, gain, *, name):
    S, HV = o.shape
    RB = min(512, S)

    def body(o_ref, z_ref, g_ref, y_ref):
        for h in range(HV // GDN_DK):
            cols = slice(h * GDN_DK, (h + 1) * GDN_DK)
            ov = o_ref[:, cols]
            r = lax.rsqrt(jnp.mean(ov * ov, axis=-1, keepdims=True) + RMS_EPS)
            y_ref[:, cols] = (ov * r * g_ref[...] * _silu(z_ref[:, cols].astype(F32))).astype(BF16)

    blk = pl.BlockSpec((RB, HV), lambda i: (i, 0))
    return pl.pallas_call(
        body, name=name, grid=(S // RB,),
        in_specs=[blk, blk, pl.BlockSpec((1, GDN_DK), lambda i: (0, 0))], out_specs=blk,
        out_shape=jax.ShapeDtypeStruct((S, HV), BF16), compiler_params=_params("parallel"),
    )(o, z, gain)


def _gdn_outnorm_bwd(dy, o, z, gain, *, name):
    S, HV = o.shape
    RB = min(512, S)

    def body(dy_ref, o_ref, z_ref, g_ref, do_ref, dz_ref, dg_ref):
        part = None
        for h in range(HV // GDN_DK):
            cols = slice(h * GDN_DK, (h + 1) * GDN_DK)
            ov = o_ref[:, cols]
            zv = z_ref[:, cols].astype(F32)
            dyv = dy_ref[:, cols].astype(F32)
            r = lax.rsqrt(jnp.mean(ov * ov, axis=-1, keepdims=True) + RMS_EPS)
            n = ov * r
            sg = _sigmoid(zv)
            dng = dyv * (zv * sg)
            dn = dng * g_ref[...]
            do_ref[:, cols] = r * (dn - n * jnp.mean(dn * n, axis=-1, keepdims=True))
            dz_ref[:, cols] = (dyv * (n * g_ref[...]) * (sg * (1.0 + zv * (1.0 - sg)))).astype(BF16)
            p = jnp.sum(dng * n, axis=0, keepdims=True)
            part = p if part is None else part + p

        @pl.when(pl.program_id(0) == 0)
        def _():
            dg_ref[...] = part

        @pl.when(pl.program_id(0) > 0)
        def _():
            dg_ref[...] += part

    blk = pl.BlockSpec((RB, HV), lambda i: (i, 0))
    vec = pl.BlockSpec((1, GDN_DK), lambda i: (0, 0))
    return pl.pallas_call(
        body, name=name, grid=(S // RB,),
        in_specs=[blk, blk, blk, vec], out_specs=[blk, blk, vec],
        out_shape=[jax.ShapeDtypeStruct((S, HV), F32), jax.ShapeDtypeStruct((S, HV), BF16),
                   jax.ShapeDtypeStruct((1, GDN_DK), F32)],
        compiler_params=_params("arbitrary"),
    )(dy, o, z, gain)


def _head_mask():
    return lax.broadcasted_iota(jnp.int32, (DSW_BLK, LANES), 1) < DSW_DH


def _per_head_sum(t, first):
    s0 = jnp.sum(jnp.where(first, t, 0.0), axis=-1, keepdims=True)
    s1 = jnp.sum(jnp.where(first, 0.0, t), axis=-1, keepdims=True)
    return jnp.where(first, s0, s1)


def _rms2(x, gain, first):
    r = lax.rsqrt(_per_head_sum(x * x, first) * (1.0 / DSW_DH) + RMS_EPS)
    xh = x * r
    return xh, r, xh * gain


def _rms2_bwd(dy, xh, r, gain, first):
    dxh = dy * gain
    return r * (dxh - xh * (_per_head_sum(dxh * xh, first) * (1.0 / DSW_DH)))


def _split_heads(x, first):
    return [jnp.where(first, x, 0.0).astype(BF16), jnp.where(first, 0.0, x).astype(BF16)]


_HP = LANES // DSW_DH
_DSW_W = DSW_HEADS * DSW_DH
_DSW_ROWS = 1024
_DSW_BATCH = 8


def _dsw_geometry(S, g):
    d = DSW_GROUPS[g][1]
    slab = DSW_BLK * d
    tb = max(1, min(_DSW_ROWS, S) // slab)
    return d, slab, tb, S // (tb * slab)


def _block_rows(t, r, slab, d):
    return pl.ds(t * slab + r, DSW_BLK) if d == 1 else pl.ds(t * slab + r, DSW_BLK, stride=d)


def _dsw_attn_fwd(q, k, v, bias, q_gain, k_gain, prev_out, *, g, name):
    S, WT = q.shape
    B = DSW_BLK
    d, slab, tb, n_tiles = _dsw_geometry(S, g)
    rt = tb * slab
    cb = g * (_DSW_W // LANES)
    batch_res = max(1, _DSW_BATCH // tb)

    def body(q_ref, kp_ref, kc_ref, vp_ref, vc_ref, bias_ref, qg_ref, kg_ref, *rest):
        o_ref, lse_ref = rest[-2:]
        i = pl.program_id(1)
        qg, kg = qg_ref[...] * DSW_DH ** -0.5, kg_ref[...]
        col = lax.broadcasted_iota(jnp.int32, (B, 2 * B), 1)
        first = _head_mask()
        heads = range(_HP)
        for r0 in range(0, d, batch_res):
            res = range(r0, min(d, r0 + batch_res))
            k_raw = {(r, -1): kp_ref[_block_rows(0, r, slab, d), :] for r in res}
            v_raw = {(r, -1): vp_ref[_block_rows(0, r, slab, d), :] for r in res}
            q_raw = {}
            for r in res:
                for t in range(tb):
                    rows = _block_rows(t, r, slab, d)
                    q_raw[r, t], k_raw[r, t], v_raw[r, t] = q_ref[rows, :], kc_ref[rows, :], vc_ref[rows, :]
            kn = {key: _rms2(x, kg, first)[2].astype(BF16) for key, x in k_raw.items()}
            vb = {key: x.astype(BF16) for key, x in v_raw.items()}
            qn = {key: _split_heads(_rms2(x, qg, first)[2], first) for key, x in q_raw.items()}
            items = [(r, t, h) for r in res for t in range(tb) for h in heads]
            s = {}
            for r, t, h in items:
                sv = _dot(qn[r, t][h], jnp.concatenate([kn[r, t - 1], kn[r, t]], axis=0), "nt") + bias_ref[h]
                s[r, t, h] = jnp.where((i == 0) & (col < B), NEG_BIG, sv) if t == 0 else sv
            m = {it: jnp.max(s[it], axis=-1, keepdims=True) for it in items}
            p = {it: jnp.exp(s[it] - m[it]) for it in items}
            l = {it: jnp.sum(p[it], axis=-1, keepdims=True) for it in items}
            o = {(r, t, h): _dot(p[r, t, h], jnp.concatenate([vb[r, t - 1], vb[r, t]], axis=0)) for r, t, h in items}
            for r in res:
                for t in range(tb):
                    rows = _block_rows(t, r, slab, d)
                    o_ref[rows, :] = jnp.where(first, o[r, t, 0] / l[r, t, 0], o[r, t, 1] / l[r, t, 1])
                    lse_ref[rows, :] = jnp.where(first, m[r, t, 0] + jnp.log(l[r, t, 0]),
                                                 m[r, t, 1] + jnp.log(l[r, t, 1]))

    cur = pl.BlockSpec((rt, LANES), lambda hp, i: (i, cb + hp))
    prev = pl.BlockSpec((slab, LANES), lambda hp, i: (jnp.maximum(i * tb - 1, 0), cb + hp))
    vec = pl.BlockSpec((1, LANES), lambda hp, i: (0, 0))
    shp = jax.ShapeDtypeStruct((S, WT), F32)
    carried = [] if prev_out is None else list(prev_out)
    n_in = 8
    return pl.pallas_call(
        body, name=name, grid=(_DSW_W // LANES, n_tiles),
        in_specs=[cur, prev, cur, prev, cur, pl.BlockSpec((_HP, B, 2 * B), lambda hp, i: (hp, 0, 0)), vec, vec]
                 + [pl.BlockSpec(memory_space=pl.ANY)] * len(carried),
        out_specs=[cur, cur], out_shape=[shp, shp],
        input_output_aliases={n_in + j: j for j in range(len(carried))},
        compiler_params=_params("parallel", "parallel"),
    )(q, k, k, v, v, bias, jnp.tile(q_gain, (1, _HP)), jnp.tile(k_gain, (1, _HP)), *carried)


def _dsw_merge(o_g, lse_g, *, name):
    S = o_g.shape[0]
    W, G = _DSW_W, len(DSW_GROUPS)
    tr = min(512, S)

    def body(o_ref, l_ref, out_ref, lse_ref):
        ls = [l_ref[:, g * W:(g + 1) * W] for g in range(G)]
        m = ls[0]
        for g in range(1, G):
            m = jnp.maximum(m, ls[g])
        den = jnp.zeros_like(m)
        acc = jnp.zeros_like(m)
        for g in range(G):
            wg = jnp.exp(ls[g] - m)
            den = den + wg
            acc = acc + wg * o_ref[:, g * W:(g + 1) * W]
        out_ref[...] = acc / den
        lse_ref[...] = m + jnp.log(den)

    wide = pl.BlockSpec((tr, G * W), lambda i: (i, 0))
    blk = pl.BlockSpec((tr, W), lambda i: (i, 0))
    shp = jax.ShapeDtypeStruct((S, W), F32)
    return pl.pallas_call(
        body, name=name, grid=(S // tr,), in_specs=[wide, wide], out_specs=[blk, blk],
        out_shape=[shp, shp], compiler_params=_params("parallel"),
    )(o_g, lse_g)


def _dsw_attn_bwd(q, k, v, o, lse, do, bias, q_gain, k_gain, prev_out, *, g, name):
    S, WT = q.shape
    B = DSW_BLK
    d, slab, tb, n_tiles = _dsw_geometry(S, g)
    rt = tb * slab
    cb = g * (_DSW_W // LANES)
    n_slabs = S // slab
    scale = DSW_DH ** -0.5
    batch_res = max(1, _DSW_BATCH // tb)

    def body(q_ref, qx_ref, kp_ref, kc_ref, vp_ref, vc_ref, o_ref, ox_ref, l_ref, lx_ref, do_ref, dox_ref,
             bias_ref, qg_ref, kg_ref, *rest):
        dq_ref, dk_ref, dv_ref, db_ref, dqg_ref, dkg_ref = rest[-6:]
        hp, i = pl.program_id(0), pl.program_id(1)
        qg, kg = qg_ref[...] * scale, kg_ref[...]
        col = lax.broadcasted_iota(jnp.int32, (B, 2 * B), 1)
        has_next = i < n_tiles - 1

        @pl.when(i == 0)
        def _():
            db_ref[...] = jnp.zeros_like(db_ref)

        dqg_acc = jnp.zeros((1, LANES), F32)
        dkg_acc = jnp.zeros((1, LANES), F32)
        first = _head_mask()
        heads = range(_HP)
        for r0 in range(0, d, batch_res):
            res = range(r0, min(d, r0 + batch_res))
            q_raw, k_raw, v_raw, o_raw, l_raw, do_raw = {}, {}, {}, {}, {}, {}
            for r in res:
                first_rows = _block_rows(0, r, slab, d)
                k_raw[r, -1], v_raw[r, -1] = kp_ref[first_rows, :], vp_ref[first_rows, :]
                for t in range(tb):
                    rows = _block_rows(t, r, slab, d)
                    q_raw[r, t], o_raw[r, t], l_raw[r, t], do_raw[r, t] = (
                        q_ref[rows, :], o_ref[rows, :], l_ref[rows, :], do_ref[rows, :])
                    k_raw[r, t], v_raw[r, t] = kc_ref[rows, :], vc_ref[rows, :]
                q_raw[r, tb], o_raw[r, tb], l_raw[r, tb], do_raw[r, tb] = (
                    qx_ref[first_rows, :], ox_ref[first_rows, :], lx_ref[first_rows, :], dox_ref[first_rows, :])
            kk = {key: _rms2(x, kg, first) for key, x in k_raw.items()}
            qq = {key: _rms2(x, qg, first) for key, x in q_raw.items()}
            knb = {key: kk[key][2].astype(BF16) for key in kk}
            qnb = {key: _split_heads(qq[key][2], first) for key in qq}
            vb = {key: x.astype(BF16) for key, x in v_raw.items()}
            dob = {key: _split_heads(x, first) for key, x in do_raw.items()}
            delta = {key: _per_head_sum(do_raw[key] * o_raw[key], first) for key in q_raw}
            pick = lambda x, h: x[:, h * DSW_DH:h * DSW_DH + 1]
            full = [(r, t, h) for r in res for t in range(tb) for h in heads]
            half = [(r, tb, h) for r in res for h in heads]
            s = {}
            for r, t, h in full:
                sv = _dot(qnb[r, t][h], jnp.concatenate([knb[r, t - 1], knb[r, t]], axis=0), "nt") + bias_ref[h]
                s[r, t, h] = jnp.where((i == 0) & (col < B), NEG_BIG, sv) if t == 0 else sv
            for r, t, h in half:
                s[r, t, h] = _dot(qnb[r, t][h], knb[r, t - 1], "nt") + bias_ref[h, :, 0:B]
            p = {(r, t, h): jnp.exp(s[r, t, h] - pick(l_raw[r, t], h)) for r, t, h in full}
            for r, t, h in half:
                p[r, t, h] = jnp.where(has_next, jnp.exp(s[r, t, h] - pick(l_raw[r, t], h)), 0.0)
            dp = {(r, t, h): _dot(dob[r, t][h], jnp.concatenate([vb[r, t - 1], vb[r, t]], axis=0), "nt")
                  for r, t, h in full}
            for r, t, h in half:
                dp[r, t, h] = _dot(dob[r, t][h], vb[r, t - 1], "nt")
            ds = {(r, t, h): p[r, t, h] * (dp[r, t, h] - pick(delta[r, t], h)) for r, t, h in full + half}
            pb = {it: p[it].astype(BF16) for it in ds}
            dsb = {it: ds[it].astype(BF16) for it in ds}
            for h in heads:
                tot = None
                for r in res:
                    for t in range(tb):
                        tot = ds[r, t, h] if tot is None else tot + ds[r, t, h]
                db_ref[h] += tot
            blocks = [(r, t) for r in res for t in range(tb)]
            keys2 = {(r, t): jnp.concatenate([knb[r, t - 1], knb[r, t]], axis=0) for r, t in blocks}
            dqn = {(r, t): jnp.where(first, _dot(dsb[r, t, 0], keys2[r, t]), _dot(dsb[r, t, 1], keys2[r, t]))
                   for r, t in blocks}
            prev_half = lambda x, r, t, h: x[r, t, h][:, :B] if t < tb else x[r, t, h]
            dkn = {(r, t): sum(_dot(dsb[r, t, h][:, B:], qnb[r, t][h], "tn")
                               + _dot(prev_half(dsb, r, t + 1, h), qnb[r, t + 1][h], "tn") for h in heads)
                   for r, t in blocks}
            dvv = {(r, t): sum(_dot(pb[r, t, h][:, B:], dob[r, t][h], "tn")
                               + _dot(prev_half(pb, r, t + 1, h), dob[r, t + 1][h], "tn") for h in heads)
                   for r, t in blocks}
            for r, t in blocks:
                dqg_acc = dqg_acc + jnp.sum(dqn[r, t] * qq[r, t][0], axis=0, keepdims=True)
                dkg_acc = dkg_acc + jnp.sum(dkn[r, t] * kk[r, t][0], axis=0, keepdims=True)
            for r, t in blocks:
                rows = _block_rows(t, r, slab, d)
                dq_ref[rows, :] = _rms2_bwd(dqn[r, t], qq[r, t][0], qq[r, t][1], qg, first)
                dk_ref[rows, :] = _rms2_bwd(dkn[r, t], kk[r, t][0], kk[r, t][1], kg, first)
                dv_ref[rows, :] = dvv[r, t]

        start = (hp == 0) & (i == 0)
        fold = lambda a: a[:, :DSW_DH] + a[:, DSW_DH:]

        @pl.when(start)
        def _():
            dqg_ref[...] = fold(dqg_acc) * scale
            dkg_ref[...] = fold(dkg_acc)

        @pl.when(jnp.logical_not(start))
        def _():
            dqg_ref[...] += fold(dqg_acc) * scale
            dkg_ref[...] += fold(dkg_acc)

    def spec(rows, pick, base):
        return pl.BlockSpec((rows, LANES), lambda hp, i: (pick(i), base + hp))

    same = lambda i: i
    before = lambda i: jnp.maximum(i * tb - 1, 0)
    after = lambda i: jnp.minimum((i + 1) * tb, n_slabs - 1)
    cur, cur1 = spec(rt, same, cb), spec(rt, same, 0)
    vec = pl.BlockSpec((1, DSW_DH), lambda hp, i: (0, 0))
    vec2 = pl.BlockSpec((1, LANES), lambda hp, i: (0, 0))
    bspec = pl.BlockSpec((_HP, B, 2 * B), lambda hp, i: (hp, 0, 0))
    shp = jax.ShapeDtypeStruct((S, WT), F32)
    vshp = jax.ShapeDtypeStruct((1, DSW_DH), F32)
    carried = [] if prev_out is None else list(prev_out)
    n_in = 15
    return pl.pallas_call(
        body, name=name, grid=(_DSW_W // LANES, n_tiles),
        in_specs=[cur, spec(slab, after, cb), spec(slab, before, cb), cur, spec(slab, before, cb), cur,
                  cur1, spec(slab, after, 0), cur1, spec(slab, after, 0), cur1, spec(slab, after, 0),
                  bspec, vec2, vec2] + [pl.BlockSpec(memory_space=pl.ANY)] * len(carried),
        out_specs=[cur, cur, cur, bspec, vec, vec],
        out_shape=[shp, shp, shp, jax.ShapeDtypeStruct(bias.shape, F32), vshp, vshp],
        input_output_aliases={n_in + j: j for j in range(len(carried))},
        compiler_params=_params("arbitrary", "arbitrary"),
    )(q, q, k, k, v, v, o, o, lse, lse, do, do, bias, jnp.tile(q_gain, (1, _HP)), jnp.tile(k_gain, (1, _HP)),
      *carried)


def _t5_bucket(dist):
    max_exact = REL_BUCKETS // 2
    scaled = jnp.log(jnp.maximum(dist, 1).astype(F32) / max_exact) / math.log(REL_MAX_DIST / max_exact)
    large = jnp.minimum(max_exact + (scaled * (REL_BUCKETS - max_exact)).astype(jnp.int32), REL_BUCKETS - 1)
    return jnp.where(dist < max_exact, dist, large)


def _dsw_band():
    dist = (jnp.arange(DSW_BLK)[:, None] + DSW_BLK) - jnp.arange(2 * DSW_BLK)[None, :]
    return dist, (dist >= 0) & (dist <= DSW_BLK)


def _dsw_bias(rel_bias):
    dist, band = _dsw_band()
    out = []
    for g, (_, d) in enumerate(DSW_GROUPS):
        hot = jax.nn.one_hot(_t5_bucket(jnp.maximum(dist, 0) * d), REL_BUCKETS, dtype=F32)
        tab = jnp.einsum("qkb,bh->hqk", hot, rel_bias[:, g * DSW_HEADS:(g + 1) * DSW_HEADS],
                         precision=lax.Precision.HIGHEST)
        out.append(jnp.where(band[None], tab, NEG_BIG))
    return jnp.stack(out)


def _dsw_bucket_onehot():
    dist, band = _dsw_band()
    out = []
    for _, d in DSW_GROUPS:
        hot = jax.nn.one_hot(_t5_bucket(jnp.maximum(dist, 0) * d), LANES, dtype=BF16)
        out.append(jnp.where(band[..., None], hot, 0).reshape(-1, LANES))
    return jnp.stack(out)


def _exchange(send, *, gather, name):
    R, C = send.shape[-2:]

    def body(src_ref, dst_ref, send_sems, recv_sems, local_sem):
        x, y, c = lax.axis_index("x"), lax.axis_index("y"), lax.axis_index("c")
        me = 4 * x + 2 * y + c
        mine = pltpu.make_async_copy(src_ref if gather else src_ref.at[me], dst_ref.at[me], local_sem)
        mine.start()
        copies = []
        for rel in range(1, N_DEV):
            px = 1 - x if rel & 4 else x
            py = 1 - y if rel & 2 else y
            pc = 1 - c if rel & 1 else c
            peer = 4 * px + 2 * py + pc
            cp = pltpu.make_async_remote_copy(
                src_ref=src_ref if gather else src_ref.at[peer], dst_ref=dst_ref.at[me],
                send_sem=send_sems.at[rel - 1], recv_sem=recv_sems.at[rel - 1],
                device_id=(px, py, pc), device_id_type=pl.DeviceIdType.MESH)
            cp.start()
            copies.append(cp)
        for cp in copies:
            cp.wait()
        mine.wait()

    return pl.pallas_call(
        body, name=name,
        in_specs=[pl.BlockSpec(memory_space=pl.ANY)], out_specs=pl.BlockSpec(memory_space=pl.ANY),
        out_shape=jax.ShapeDtypeStruct((N_DEV, R, C), send.dtype),
        scratch_shapes=[pltpu.SemaphoreType.DMA((N_DEV - 1,)), pltpu.SemaphoreType.DMA((N_DEV - 1,)),
                        pltpu.SemaphoreType.DMA(())],
    )(send)


def _gather_two_level(send, *, name):
    R, C = send.shape

    def body(src_ref, dst_ref, send_sems, recv_sems, local_sem):
        x, y, c = lax.axis_index("x"), lax.axis_index("y"), lax.axis_index("c")
        me, sibling = (x, y, c), (x, y, 1 - c)
        chips = [(1 - x, y), (x, 1 - y), (1 - x, 1 - y)]

        def slot(px, py, pc):
            return dst_ref.at[4 * px + 2 * py + pc]

        def copy(k, block, to, src=None):
            return pltpu.make_async_remote_copy(
                src_ref=slot(*block) if src is None else src, dst_ref=slot(*block),
                send_sem=send_sems.at[k], recv_sem=recv_sems.at[k],
                device_id=to, device_id_type=pl.DeviceIdType.MESH)

        mine = pltpu.make_async_copy(src_ref, slot(*me), local_sem)
        mine.start()
        first = [copy(0, me, sibling, src=src_ref)]
        first += [copy(1 + j, me, (*chip, c), src=src_ref) for j, chip in enumerate(chips)]
        for cp in first:
            cp.start()
        passed = [copy(4 + j, (*chip, c), sibling) for j, chip in enumerate(chips)]
        for j, chip in enumerate(chips):
            copy(1 + j, (*chip, c), me).wait_recv()
            passed[j].start()
        copy(0, sibling, me).wait_recv()
        for j, chip in enumerate(chips):
            copy(4 + j, (*chip, 1 - c), me).wait_recv()
        for cp in first + passed:
            cp.wait_send()
        mine.wait()

    return pl.pallas_call(
        body, name=name,
        in_specs=[pl.BlockSpec(memory_space=pl.ANY)], out_specs=pl.BlockSpec(memory_space=pl.ANY),
        out_shape=jax.ShapeDtypeStruct((N_DEV, R, C), send.dtype),
        scratch_shapes=[pltpu.SemaphoreType.DMA((N_DEV - 1,)), pltpu.SemaphoreType.DMA((N_DEV - 1,)),
                        pltpu.SemaphoreType.DMA(())],
    )(send)


_ANY = pl.BlockSpec(memory_space=pl.ANY)


def _swap_with_sibling(sends, *, name):
    n = len(sends)

    def body(*refs):
        x, y, c = lax.axis_index("x"), lax.axis_index("y"), lax.axis_index("c")
        send_sems, recv_sems = refs[2 * n:]
        copies = [pltpu.make_async_remote_copy(
            src_ref=refs[a], dst_ref=refs[n + a], send_sem=send_sems.at[a], recv_sem=recv_sems.at[a],
            device_id=(x, y, 1 - c), device_id_type=pl.DeviceIdType.MESH) for a in range(n)]
        for cp in copies:
            cp.start()
        for cp in copies:
            cp.wait()

    return pl.pallas_call(
        body, name=name, in_specs=[_ANY] * n, out_specs=[_ANY] * n,
        out_shape=[jax.ShapeDtypeStruct(s.shape, s.dtype) for s in sends],
        scratch_shapes=[pltpu.SemaphoreType.DMA((n,)), pltpu.SemaphoreType.DMA((n,))],
    )(*sends)


def _fill_from_sibling(bufs, *, name):
    n, n_chips = len(bufs), bufs[0].shape[0]

    def body(*refs):
        x, y, c = lax.axis_index("x"), lax.axis_index("y"), lax.axis_index("c")
        send_sems, recv_sems = refs[2 * n:]
        copies = [pltpu.make_async_remote_copy(
            src_ref=refs[a].at[q, c], dst_ref=refs[n + a].at[q, c],
            send_sem=send_sems.at[a * n_chips + q], recv_sem=recv_sems.at[a * n_chips + q],
            device_id=(x, y, 1 - c), device_id_type=pl.DeviceIdType.MESH) for a in range(n) for q in range(n_chips)]
        for cp in copies:
            cp.start()
        for cp in copies:
            cp.wait()

    return pl.pallas_call(
        body, name=name, in_specs=[_ANY] * n, out_specs=[_ANY] * n,
        out_shape=[jax.ShapeDtypeStruct(b.shape, b.dtype) for b in bufs],
        input_output_aliases={a: a for a in range(n)},
        scratch_shapes=[pltpu.SemaphoreType.DMA((n * n_chips,)), pltpu.SemaphoreType.DMA((n * n_chips,))],
    )(*bufs)


def _exchange_chips(send, *, name):
    def body(src_ref, dst_ref, *sems):
        copies = _chip_copies([src_ref], [dst_ref], *sems)
        for cp in copies:
            cp.start()
        for cp in copies:
            cp.wait()

    return pl.pallas_call(
        body, name=name, in_specs=[_ANY], out_specs=_ANY,
        out_shape=jax.ShapeDtypeStruct(send.shape, send.dtype), scratch_shapes=_chip_sems(1),
    )(send)


def _add_pair(a, b, *, name):
    lead, (R, C) = a.shape[:-2], a.shape[-2:]
    tr = _tile(R, max(8, 1024 * LANES // C))

    def body(a_ref, b_ref, o_ref):
        o_ref[...] = (a_ref[...].astype(F32) + b_ref[...].astype(F32)).astype(o_ref.dtype)

    blk = pl.BlockSpec((None,) * len(lead) + (tr, C), lambda *idx: idx + (0,))
    return pl.pallas_call(
        body, name=name, grid=lead + (R // tr,), in_specs=[blk, blk], out_specs=blk,
        out_shape=jax.ShapeDtypeStruct(a.shape, a.dtype),
        compiler_params=_params(*(("parallel",) * (len(lead) + 1))),
    )(a, b)


_BIG = ("w_ffn_in", "w_ffn_out", "gdn_w_in", "gdn_conv", "gdn_w_out", "dsw_w_in", "dsw_w_out")
_LATE = ("gdn_w_in", "gdn_conv", "gdn_w_out")
_EARLY = tuple(n for n in _BIG if n not in _LATE)
_NATIVE = ("w_ffn_in", "w_ffn_out", "dsw_w_in")
_SHARD_AXIS = {"w_ffn_in": 2, "w_ffn_out": 1, "gdn_w_in": 2, "gdn_conv": 2, "gdn_w_out": 1, "dsw_w_in": 2,
               "dsw_w_out": 2}
_SMALL = ("b_ada", "norm_mix", "norm_ffn", "gdn_a_log", "gdn_dt_bias", "gdn_out_norm", "dsw_q_norm",
          "dsw_k_norm", "rel_bias")
_ROW_ALIGN = 16
_BIG_ALIGN = 1024


def _ceil_to(n, m):
    return -(-n // m) * m


def _seg_rows(shape):
    return _ceil_to(_ceil_to(int(np.prod(shape)), LANES) // LANES, _ROW_ALIGN)


def _pack(arrs, total_align):
    lead = arrs[0][1]
    segs = []
    for a, nlead in arrs:
        assert nlead == lead
        bshape = a.shape[:nlead]
        n = int(np.prod(a.shape[nlead:]))
        rows = _seg_rows(a.shape[nlead:])
        flat = a.reshape(bshape + (n,))
        flat = jnp.pad(flat, [(0, 0)] * nlead + [(0, rows * LANES - n)])
        segs.append(flat.reshape(bshape + (rows, LANES)))
    buf = jnp.concatenate(segs, axis=lead)
    total = _ceil_to(buf.shape[lead], total_align)
    return jnp.pad(buf, [(0, 0)] * lead + [(0, total - buf.shape[lead]), (0, 0)])


def _unpack(buf, shapes, nlead):
    out, off = [], 0
    for shp in shapes:
        n, rows = int(np.prod(shp)), _seg_rows(shp)
        seg = lax.slice_in_dim(buf, off, off + rows, axis=nlead)
        seg = seg.reshape(buf.shape[:nlead] + (rows * LANES,))[..., :n]
        out.append(seg.reshape(buf.shape[:nlead] + tuple(shp)))
        off += rows
    return out


def _to_natural(g, axis):
    n, L, r, c = g.shape
    if axis == 2:
        return jnp.transpose(g, (1, 2, 0, 3)).reshape(L, r, n * c)
    return jnp.transpose(g, (1, 0, 2, 3)).reshape(L, n * r, c)


def _to_blocked(w, axis):
    L, R, C = w.shape
    if axis == 2:
        return jnp.transpose(w.reshape(L, R, N_DEV, C // N_DEV), (2, 0, 1, 3))
    return jnp.transpose(w.reshape(L, N_DEV, R // N_DEV, C), (1, 0, 2, 3))


def _hm(a):
    lead = a.shape[:-1]
    return jnp.swapaxes(a.reshape(lead + (3, GDN_HEADS, GDN_DK)), -3, -2).reshape(lead + (3 * GDN_HEADS * GDN_DK,))


def _un_hm(a):
    lead = a.shape[:-1]
    return jnp.swapaxes(a.reshape(lead + (GDN_HEADS, 3, GDN_DK)), -3, -2).reshape(lead + (3 * GDN_HEADS * GDN_DK,))


_TILES = (2048, 1536, 1408, 1024, 768, 512, 384, 256, 128, 64, 32, 16, 8)


def _tile(n, cap):
    for t in _TILES:
        if t <= cap and n % t == 0:
            return t
    return n


def _mm_auto(a, b, mode, name, **kw):
    if mode == "tn":
        (K, M), N = a.shape, b.shape[1]
        deep = 2048 if a.dtype == BF16 and b.dtype == BF16 else 1024
        tm, tn, tk = _tile(M, 1408), _tile(N, 1408), _tile(K, deep)
    else:
        M, K = a.shape
        N = b.shape[1] if mode == "nn" else b.shape[0]
        tm, tn, tk = _tile(M, _MM_ROWS), _tile(N, 1536), _tile(K, 1408)
    return _mm(a, b, mode=mode, name=name, tm=tm, tn=tn, tk=tk, **kw)


def _row(v):
    return v.reshape(1, -1)


def _ffn_in_act(h, w_in, *, name):
    S, D = h.shape
    F = w_in.shape[1] // 2
    tm, tn = _tile(S, _MM_ROWS), _tile(F, 1408)
    nj = F // tn

    def body(h_ref, wg_ref, wu_ref, g_ref, u_ref, a_ref):
        hv = h_ref[...]
        gate = jnp.dot(hv, wg_ref[...], preferred_element_type=F32)
        up = jnp.dot(hv, wu_ref[...], preferred_element_type=F32)
        g_ref[...] = gate.astype(BF16)
        u_ref[...] = up.astype(BF16)
        a_ref[...] = (_silu(gate) * up).astype(BF16)

    out = pl.BlockSpec((tm, tn), lambda i, j: (i, j))
    shp = jax.ShapeDtypeStruct((S, F), BF16)
    return pl.pallas_call(
        body, name=name, grid=(S // tm, nj),
        in_specs=[pl.BlockSpec((tm, D), lambda i, j: (i, 0)), pl.BlockSpec((D, tn), lambda i, j: (0, j)),
                  pl.BlockSpec((D, tn), lambda i, j: (0, j + nj))],
        out_specs=[out, out, out], out_shape=[shp, shp, shp],
        compiler_params=_params("parallel", "parallel"),
    )(h, w_in, w_in)


def _ffn_out_dx_act(dy, w_out, gate_vec, pg, pu, *, name):
    S, D = dy.shape
    F = w_out.shape[0]
    tm, tn = _tile(S, _MM_ROWS), _tile(F, 1408)

    def body(dy_ref, w_ref, gv_ref, pg_ref, pu_ref, dg_ref, du_ref):
        dyg = (dy_ref[...] * gv_ref[...]).astype(BF16)
        da = lax.dot_general(dyg, w_ref[...], _DOT_DIMS["nt"], preferred_element_type=F32)
        gate = pg_ref[...].astype(F32)
        up = pu_ref[...].astype(F32)
        sg = _sigmoid(gate)
        dg_ref[...] = (da * up * (sg * (1.0 + gate * (1.0 - sg)))).astype(BF16)
        du_ref[...] = (da * (gate * sg)).astype(BF16)

    blk = pl.BlockSpec((tm, tn), lambda i, j: (i, j))
    shp = jax.ShapeDtypeStruct((S, F), BF16)
    return pl.pallas_call(
        body, name=name, grid=(S // tm, F // tn),
        in_specs=[pl.BlockSpec((tm, D), lambda i, j: (i, 0)), pl.BlockSpec((tn, D), lambda i, j: (j, 0)),
                  pl.BlockSpec((1, D), lambda i, j: (0, 0)), blk, blk],
        out_specs=[blk, blk], out_shape=[shp, shp],
        compiler_params=_params("parallel", "parallel"),
    )(dy, w_out, gate_vec, pg, pu)


def _ffn_fwd(x, mod, gain, w_in, w_out, tag):
    sh, sc, gate = mod
    h = _norm_mod_fwd(x, gain, sc, sh, name=f"ffn_norm_{tag}")
    pg, pu, a = _ffn_in_act(h, w_in, name=f"ffn_in_{tag}")
    y = _mm_auto(a, w_out, "nn", f"ffn_out_{tag}", out_scale=gate, resid=x)
    return y, (x, h, pg, pu, a)


def _ffn_bwd(dy, saved, mod, gain, w_in, w_out, tag):
    sh, sc, gate = mod
    x, h, pg, pu, a = saved
    F = pg.shape[1]
    gmat = _mm_auto(a, dy, "tn", f"ffn_out_g_{tag}")
    dw_out, dgate = _wout_grad(gmat, w_out, gate, name=f"ffn_out_dw_{tag}")
    dpg, dpu = _ffn_out_dx_act(dy, w_out, gate, pg, pu, name=f"ffn_out_dx_{tag}")
    dw_in = jnp.concatenate([_mm_auto(h, dpg, "tn", f"ffn_in_dw_gate_{tag}", out_dtype=BF16),
                             _mm_auto(h, dpu, "tn", f"ffn_in_dw_up_{tag}", out_dtype=BF16)], axis=1)
    tk = _tile(F, 1408)
    dh = _mm_sum_nt([(dpg, w_in, tk, 0), (dpu, w_in, tk, F)], name=f"ffn_in_dx_{tag}")
    dx, dsh, dsc, dgain = _norm_mod_bwd(dh, x, dy, gain, sc, name=f"ffn_norm_bwd_{tag}")
    return dx, dict(w_in=dw_in, w_out=dw_out, gain=dgain, mod=(dsh, dsc, dgate))


def _gdn_fwd(x, mod, gain, W, riding=None):
    sh, sc, gate = mod
    S = x.shape[0]
    h = _norm_mod_fwd(x, gain, sc, sh, name="gdn_norm")
    pq = _mm_auto(h, W["gdn_qkv"], "nn", "gdn_in_qkv", out_dtype=BF16)
    z = _mm_auto(h, W["gdn_z"], "nn", "gdn_in_z", out_dtype=BF16)
    ab = _mm_auto(h, W["gdn_ab"], "nn", "gdn_in_ab")
    qkvn = _gdn_prep_fwd(pq, W["gdn_conv"], name="gdn_prep")
    ab4 = jnp.transpose(ab[:, :2 * GDN_HEADS]).reshape(2 * GDN_HEADS, S // GDN_CHUNK, 1, GDN_CHUNK)
    o, states, tinvs, *rode = _gdn_chunk_fwd(qkvn, ab4, W["gdn_a_log"], W["gdn_dt_bias"], name="gdn_chunk",
                                             riding=riding)
    o2 = _gdn_outnorm_fwd(o, z, W["gdn_out_norm"], name="gdn_outnorm")
    y = _mm_auto(o2, W["gdn_out"], "nn", "gdn_out", out_scale=gate, resid=x)
    return y, (x, h, pq, z, qkvn, ab4, o, states, tinvs, o2), (tuple(rode) if rode else None)


def _gdn_bwd(dy, saved, mod, gain, W, riding=None):
    sh, sc, gate = mod
    x, h, pq, z, qkvn, ab4, o, states, tinvs, o2 = saved
    S = x.shape[0]
    gmat = _mm_auto(o2, dy, "tn", "gdn_out_g")
    dw_out, dgate = _wout_grad(gmat, W["gdn_out"], gate, name="gdn_out_dw")
    do2 = _mm_auto(dy, W["gdn_out"], "nt", "gdn_out_dx", a_scale=gate)
    do, dz, dout_norm = _gdn_outnorm_bwd(do2, o, z, W["gdn_out_norm"], name="gdn_outnorm_bwd")
    dqkvn, dab4, da_log, ddt_bias, *rode = _gdn_chunk_bwd(
        qkvn, ab4, W["gdn_a_log"], W["gdn_dt_bias"], states, tinvs, do, name="gdn_chunk_bwd", riding=riding)
    dc, dconv8 = _gdn_prep_bwd_pre(dqkvn, pq, W["gdn_conv"], name="gdn_prep_bwd")
    dpq = _gdn_conv_bwd_x(dc, W["gdn_conv"], name="gdn_conv_bwd")
    dab = jnp.transpose(dab4.reshape(2 * GDN_HEADS, S))
    dab = jnp.pad(dab, ((0, 0), (0, LANES - 2 * GDN_HEADS))).astype(BF16)
    dw_qkv = _mm_auto(h, dpq, "tn", "gdn_in_qkv_dw", out_dtype=BF16)
    dw_z = _mm_auto(h, dz, "tn", "gdn_in_z_dw", out_dtype=BF16)
    dw_ab = _mm_auto(h, dab, "tn", "gdn_in_ab_dw", out_dtype=BF16)
    dh = _mm_sum_nt([(dpq, W["gdn_qkv"], 1024, 0), (dz, W["gdn_z"], 1024, 0), (dab, W["gdn_ab"], LANES, 0)],
                    name="gdn_in_dx")
    dx, dsh, dsc, dgain = _norm_mod_bwd(dh, x, dy, gain, sc, name="gdn_norm_bwd")
    dw_in = jnp.concatenate([_un_hm(dw_qkv), dw_z, dw_ab[:, :2 * GDN_HEADS]], axis=1)
    return dx, dict(gdn_w_in=dw_in, gdn_conv=_un_hm(dconv8[:GDN_CONV]), gdn_w_out=dw_out, gdn_out_norm=dout_norm,
                    gdn_a_log=da_log.reshape(1, GDN_HEADS), gdn_dt_bias=ddt_bias.reshape(1, GDN_HEADS),
                    gain=dgain, mod=(dsh, dsc, dgate)), (tuple(rode) if rode else None)


def _dsw_fwd(x, mod, gain, W):
    sh, sc, gate = mod
    h = _norm_mod_fwd(x, gain, sc, sh, name="dsw_norm")
    q, k, v = (_mm_auto(h, W[n], "nn", f"dsw_in_{n[-1]}") for n in ("dsw_q", "dsw_k", "dsw_v"))
    outs = None
    for g in range(len(DSW_GROUPS)):
        outs = _dsw_attn_fwd(q, k, v, W["dsw_bias"][g], W["dsw_q_norm"], W["dsw_k_norm"], outs, g=g,
                             name=f"dsw_attn_{g}")
    o, lse = _dsw_merge(*outs, name="dsw_merge")
    y = _mm_auto(o, W["dsw_out"], "nn", "dsw_out", out_scale=gate, resid=x)
    return y, (x, h, q, k, v, o, lse)


def _dsw_bwd(dy, saved, mod, gain, W):
    sh, sc, gate = mod
    x, h, q, k, v, o, lse = saved
    gmat = _mm_auto(o, dy, "tn", "dsw_out_g")
    dw_out, dgate = _wout_grad(gmat, W["dsw_out"], gate, name="dsw_out_dw")
    do = _mm_auto(dy, W["dsw_out"], "nt", "dsw_out_dx", a_scale=gate)
    G = len(DSW_GROUPS)
    dqkv, dbias, dq_norm, dk_norm = None, [], 0.0, 0.0
    for g in range(G):
        *dqkv, db, dqg, dkg = _dsw_attn_bwd(q, k, v, o, lse, do, W["dsw_bias"][g], W["dsw_q_norm"],
                                            W["dsw_k_norm"], dqkv, g=g, name=f"dsw_attn_bwd_{g}")
        dbias.append(db)
        dq_norm, dk_norm = dq_norm + dqg, dk_norm + dkg
    names = ("dsw_q", "dsw_k", "dsw_v")
    dws = [_mm_auto(h, d, "tn", f"dsw_in_{n[-1]}_dw", out_dtype=BF16) for n, d in zip(names, dqkv)]
    dh = _mm_sum_nt([(d, W[n], _tile(d.shape[1], 1024), 0) for n, d in zip(names, dqkv)], name="dsw_in_dx")
    dx, dsh, dsc, dgain = _norm_mod_bwd(dh, x, dy, gain, sc, name="dsw_norm_bwd")
    hot = _dsw_bucket_onehot()
    drel = [_mm(dbias[g].reshape(DSW_HEADS, -1), hot[g], mode="nn", name=f"dsw_rel_bias_{g}", tm=DSW_HEADS,
                tn=LANES, tk=8192)[:, :REL_BUCKETS] for g in range(G)]
    return dx, dict(dsw_w_in=jnp.concatenate(dws, axis=1), dsw_w_out=dw_out, dsw_q_norm=dq_norm,
                    dsw_k_norm=dk_norm, rel_bias=jnp.transpose(jnp.concatenate(drel, axis=0)),
                    gain=dgain, mod=(dsh, dsc, dgate))


def _local_step(x, target, mod, W, late_weights=None, early_pairs=None):
    mods = [[_row(mod[l, i]) for i in range(6)] for l in range(2)]
    nmix = [_row(W["norm_mix"][l]) for l in range(2)]
    nffn = [_row(W["norm_ffn"][l]) for l in range(2)]
    x1, s_gdn, arrived = _gdn_fwd(x, mods[0][:3], nmix[0], W, None if late_weights is None else late_weights[0])
    if late_weights is not None:
        W = {**W, **late_weights[1](arrived)}
    x2, s_f0 = _ffn_fwd(x1, mods[0][3:], nffn[0], W["w_ffn_in"][0], W["w_ffn_out"][0], "0")
    x3, s_dsw = _dsw_fwd(x2, mods[1][:3], nmix[1], W)
    x4, s_f1 = _ffn_fwd(x3, mods[1][3:], nffn[1], W["w_ffn_in"][1], W["w_ffn_out"][1], "1")
    dx4, sse = _loss_head(x4, target, name="loss_head")
    dx3, g_f1 = _ffn_bwd(dx4, s_f1, mods[1][3:], nffn[1], W["w_ffn_in"][1], W["w_ffn_out"][1], "1")
    dx2, g_dsw = _dsw_bwd(dx3, s_dsw, mods[1][:3], nmix[1], W)
    dx1, g_f0 = _ffn_bwd(dx2, s_f0, mods[0][3:], nffn[0], W["w_ffn_in"][0], W["w_ffn_out"][0], "0")
    grads = dict(
        w_ffn_in=jnp.stack([g_f0["w_in"], g_f1["w_in"]]), w_ffn_out=jnp.stack([g_f0["w_out"], g_f1["w_out"]]),
        dsw_w_in=g_dsw["dsw_w_in"][None], dsw_w_out=g_dsw["dsw_w_out"][None])
    riding = None if early_pairs is None else early_pairs(grads)
    dx0, g_gdn, rode = _gdn_bwd(dx1, s_gdn, mods[0][:3], nmix[0], W, riding)
    dmod = jnp.stack([jnp.concatenate(list(g_gdn["mod"]) + list(g_f0["mod"]), axis=0),
                      jnp.concatenate(list(g_dsw["mod"]) + list(g_f1["mod"]), axis=0)])
    grads.update(
        norm_mix=jnp.concatenate([g_gdn["gain"], g_dsw["gain"]], axis=0),
        norm_ffn=jnp.concatenate([g_f0["gain"], g_f1["gain"]], axis=0),
        gdn_w_in=g_gdn["gdn_w_in"][None], gdn_conv=g_gdn["gdn_conv"][None], gdn_w_out=g_gdn["gdn_w_out"][None],
        gdn_out_norm=g_gdn["gdn_out_norm"], gdn_a_log=g_gdn["gdn_a_log"], gdn_dt_bias=g_gdn["gdn_dt_bias"],
        dsw_q_norm=g_dsw["dsw_q_norm"], dsw_k_norm=g_dsw["dsw_k_norm"], rel_bias=g_dsw["rel_bias"])
    return sse, dx0, grads, dmod, rode


def _prepare_first(full, small):
    gw = full["gdn_w_in"][0]
    hk3 = 3 * GDN_HEADS * GDN_DK
    return dict(
        gdn_qkv=_hm(gw[:, :hk3]), gdn_z=gw[:, hk3:hk3 + GDN_HEADS * GDN_DK],
        gdn_ab=jnp.pad(gw[:, hk3 + GDN_HEADS * GDN_DK:], ((0, 0), (0, LANES - 2 * GDN_HEADS))),
        gdn_conv=_hm(full["gdn_conv"][0]), gdn_out=full["gdn_w_out"][0],
        norm_mix=small["norm_mix"], norm_ffn=small["norm_ffn"],
        gdn_a_log=small["gdn_a_log"].reshape(GDN_HEADS, 1, 1), gdn_dt_bias=small["gdn_dt_bias"].reshape(GDN_HEADS, 1, 1),
        gdn_out_norm=small["gdn_out_norm"], dsw_q_norm=small["dsw_q_norm"], dsw_k_norm=small["dsw_k_norm"],
        dsw_bias=_dsw_bias(small["rel_bias"]))


def _prepare_rest(full):
    di = full["dsw_w_in"][0]
    dq = di.shape[1] // 3
    return dict(w_ffn_in=full["w_ffn_in"], w_ffn_out=full["w_ffn_out"],
                dsw_q=di[:, :dq], dsw_k=di[:, dq:2 * dq], dsw_v=di[:, 2 * dq:], dsw_out=full["dsw_w_out"][0])


def _prepare_weights(full, small):
    return {**_prepare_first(full, small), **_prepare_rest(full)}


_W_NAMES = ("w_ada", "b_ada", "norm_mix", "norm_ffn", "w_ffn_in", "w_ffn_out", "gdn_w_in", "gdn_conv",
            "gdn_a_log", "gdn_dt_bias", "gdn_out_norm", "gdn_w_out", "dsw_w_in", "dsw_q_norm", "dsw_k_norm",
            "dsw_w_out", "rel_bias")
_PAD_BATCH = 16


def _pad_rows(a, rows):
    return jnp.pad(a, ((0, rows - a.shape[0]), (0, 0)))


def kernel(x, c, w_ada, b_ada, norm_mix, norm_ffn, w_ffn_in, w_ffn_out, gdn_w_in, gdn_conv, gdn_a_log, gdn_dt_bias, gdn_out_norm, gdn_w_out, dsw_w_in, dsw_q_norm, dsw_k_norm, dsw_w_out, rel_bias, loss_target, m_w_ada, m_b_ada, m_norm_mix, m_norm_ffn, m_w_ffn_in, m_w_ffn_out, m_gdn_w_in, m_gdn_conv, m_gdn_a_log, m_gdn_dt_bias, m_gdn_out_norm, m_gdn_w_out, m_dsw_w_in, m_dsw_q_norm, m_dsw_k_norm, m_dsw_w_out, m_rel_bias, v_w_ada, v_b_ada, v_norm_mix, v_norm_ffn, v_w_ffn_in, v_w_ffn_out, v_gdn_w_in, v_gdn_conv, v_gdn_a_log, v_gdn_dt_bias, v_gdn_out_norm, v_gdn_w_out, v_dsw_w_in, v_dsw_q_norm, v_dsw_k_norm, v_dsw_w_out, v_rel_bias):
    w = dict(zip(_W_NAMES, (w_ada, b_ada, norm_mix, norm_ffn, w_ffn_in, w_ffn_out, gdn_w_in, gdn_conv, gdn_a_log,
                            gdn_dt_bias, gdn_out_norm, gdn_w_out, dsw_w_in, dsw_q_norm, dsw_k_norm, dsw_w_out,
                            rel_bias)))
    m = dict(zip(_W_NAMES, (m_w_ada, m_b_ada, m_norm_mix, m_norm_ffn, m_w_ffn_in, m_w_ffn_out, m_gdn_w_in,
                            m_gdn_conv, m_gdn_a_log, m_gdn_dt_bias, m_gdn_out_norm, m_gdn_w_out, m_dsw_w_in,
                            m_dsw_q_norm, m_dsw_k_norm, m_dsw_w_out, m_rel_bias)))
    v = dict(zip(_W_NAMES, (v_w_ada, v_b_ada, v_norm_mix, v_norm_ffn, v_w_ffn_in, v_w_ffn_out, v_gdn_w_in,
                            v_gdn_conv, v_gdn_a_log, v_gdn_dt_bias, v_gdn_out_norm, v_gdn_w_out, v_dsw_w_in,
                            v_dsw_q_norm, v_dsw_k_norm, v_dsw_w_out, v_rel_bias)))
    D = x.shape[-1]
    n_layers, _, ada_cols = w_ada.shape

    c_all = _exchange(c.reshape(D // LANES, LANES), gather=True, name="gather_cond").reshape(N_DEV, D)
    c_pad = _pad_rows(c_all, _PAD_BATCH)
    proj = [_mm(c_pad, w_ada[l], mode="nn", name=f"ada_proj_{l}", tm=_PAD_BATCH, tn=ada_cols, tk=D, a_silu=True)
            for l in range(n_layers)]
    mod_send = _pack([(jnp.stack([p[:N_DEV] for p in proj], axis=1), 1)], _ROW_ALIGN)
    mod_recv = _exchange(mod_send, gather=False, name="scatter_mod")
    mod = _unpack(mod_recv, [(n_layers, ada_cols)], 1)[0]
    mod = jnp.transpose(mod, (1, 0, 2)).reshape(n_layers, N_DEV * ada_cols) + b_ada
    mod = mod.reshape(n_layers, 6, D)

    conv_hi = gdn_conv.astype(BF16)
    conv_lo = (gdn_conv - conv_hi.astype(F32)).astype(BF16)
    first_send = _pack([(conv_hi if n == "gdn_conv" else w[n].astype(BF16), 0) for n in _LATE] + [(conv_lo, 0)],
                       _ROW_ALIGN)
    parts = _unpack(_gather_two_level(first_send, name="gather_weights_first"),
                    [w[n].shape for n in _LATE] + [gdn_conv.shape], 1)
    full = {n: _to_natural(parts[i], _SHARD_AXIS[n]) for i, n in enumerate(_LATE)}
    full["gdn_conv"] = full["gdn_conv"].astype(F32) + _to_natural(parts[-1], _SHARD_AXIS["gdn_conv"]).astype(F32)
    W = _prepare_first(full, {n: w[n] for n in _SMALL})
    packed_early = tuple(n for n in _EARLY if n not in _NATIVE)
    rest_send = (_pack([(w[n].astype(BF16), 0) for n in packed_early], _ROW_ALIGN),
                 ) + tuple(w[n].astype(BF16) for n in _NATIVE)

    def rest_weights(arrived):
        filled = _fill_from_sibling(arrived, name="swap_weights")
        by_dev = [a.reshape((N_DEV,) + a.shape[2:]) for a in filled]
        blocks = dict(zip(packed_early, _unpack(by_dev[0], [w[n].shape for n in packed_early], 1)))
        blocks.update(zip(_NATIVE, by_dev[1:]))
        return _prepare_rest({n: _to_natural(blocks[n], _SHARD_AXIS[n]) for n in _EARLY})

    my_c = lax.axis_index("c")

    def pair_sums(g, packed, native, tag):
        sends = [_pack([(_to_blocked(g[n].astype(BF16), _SHARD_AXIS[n]), 1) for n in packed], _BIG_ALIGN)]
        sends += [_to_blocked(g[n].astype(BF16), _SHARD_AXIS[n]) for n in native]
        by_core = [s.reshape((N_DEV // 2, 2) + s.shape[1:]) for s in sends]
        keep = [lax.dynamic_index_in_dim(s, my_c, axis=1, keepdims=False) for s in by_core]
        give = [lax.dynamic_index_in_dim(s, 1 - my_c, axis=1, keepdims=False) for s in by_core]
        got = _swap_with_sibling(give, name=f"swap_grads_{tag}")
        return tuple(_add_pair(k, t, name=f"add_sibling_grads_{tag}_{j}") for j, (k, t) in enumerate(zip(keep, got)))

    sse, grad_x, grads, dmod, early_recv = _local_step(
        x[0], loss_target[0], mod, W, late_weights=(rest_send, rest_weights),
        early_pairs=lambda g: pair_sums(g, packed_early, _NATIVE, "early"))
    loss = lax.psum(0.5 * sse[0, 0] / D, ("x", "y", "c"))
    grads["b_ada"] = dmod.reshape(n_layers, 6 * D)
    late_recv = _exchange_chips(pair_sums(grads, _LATE, (), "late")[0], name="scatter_grads_late")
    g_parts = dict(zip(packed_early, _unpack(early_recv[0], [w[n].shape for n in packed_early], 1)))
    g_parts.update(zip(_NATIVE, early_recv[1:]))
    g_parts.update(zip(_LATE, _unpack(late_recv, [w[n].shape for n in _LATE], 1)))

    dmod_send = _pack([(jnp.transpose(dmod.reshape(n_layers, N_DEV, ada_cols), (1, 0, 2)), 1)], _ROW_ALIGN)
    small_send = _pack([(grads[n].reshape(w[n].shape), 0) for n in _SMALL], _ROW_ALIGN)
    s_recv = _exchange(jnp.concatenate(
        [dmod_send, jnp.broadcast_to(small_send[None], (N_DEV,) + small_send.shape)], axis=1),
        gather=False, name="scatter_small")
    dmod_rows = dmod_send.shape[1]

    out = {}
    kinds = ("grad", "delta", "new_m", "new_v")
    for n in _BIG:
        g4 = g_parts[n]
        rows2d = lambda a: a.reshape((-1, w[n].shape[-1]))
        res = _adamw(rows2d(w[n]), g4.reshape((g4.shape[0], -1, w[n].shape[-1])), rows2d(m[n]), rows2d(v[n]),
                     name=f"adamw_{n}")
        for kind, buf in zip(kinds, res):
            out[kind, n] = buf.reshape(w[n].shape)

    dmod_all = _unpack(lax.slice_in_dim(s_recv, 0, dmod_rows, axis=1), [(n_layers, ada_cols)], 1)[0]
    g_ada = jnp.stack([_mm(c_pad, _pad_rows(dmod_all[:, l], _PAD_BATCH), mode="tn", name=f"ada_dw_{l}",
                           tm=D, tn=ada_cols, tk=_PAD_BATCH, a_silu=True) for l in range(n_layers)])
    flat = lambda a: a.reshape(n_layers * D, ada_cols)
    res = _adamw(flat(w_ada), flat(g_ada)[None], flat(m_w_ada), flat(v_w_ada), name="adamw_ada")
    for kind, buf in zip(("grad", "delta", "new_m", "new_v"), res):
        out[kind, "w_ada"] = buf.reshape(w_ada.shape)

    small_parts = lax.slice_in_dim(s_recv, dmod_rows, s_recv.shape[1], axis=1)
    packed = [_pack([(t[n], 0) for n in _SMALL], _ROW_ALIGN) for t in (w, m, v)]
    res = _adamw(packed[0], small_parts, packed[1], packed[2], name="adamw_replicated")
    for kind, buf in zip(("grad", "delta", "new_m", "new_v"), res):
        for n, a in zip(_SMALL, _unpack(buf, [w[n].shape for n in _SMALL], 0)):
            out[kind, n] = a

    return (loss, grad_x[None]) + tuple(out[kind, n] for kind in ("grad", "delta", "new_m", "new_v")
                                        for n in _W_NAMES)
```

```python
import functools
import math

import numpy as np
import jax
import jax.numpy as jnp
from jax import lax
from jax.experimental import pallas as pl
from jax.experimental.pallas import tpu as pltpu

F32 = jnp.float32
BF16 = jnp.bfloat16

N_DEV = 8
RMS_EPS = 1e-6
LANES = 128
V7X_VMEM_LIMIT = 48 * 1024 * 1024

GDN_HEADS = 8
GDN_DK = 128
GDN_CHUNK = 64
GDN_CONV = 4
DSW_GROUPS = ((128, 1), (512, 4), (2048, 16))
DSW_HEADS = 8
DSW_DH = 64
DSW_BLK = 128
REL_BUCKETS = 32
REL_MAX_DIST = 2048

ADAM_LR = 0.001
ADAM_B1 = 0.9
ADAM_B2 = 0.999
ADAM_EPS = 1e-08
ADAM_WD = 0.01
ADAM_STEP = 10

NEG_BIG = -1e30


def _params(*sem):
    return pltpu.CompilerParams(dimension_semantics=sem, vmem_limit_bytes=V7X_VMEM_LIMIT)


def _sigmoid(x):
    return 1.0 / (1.0 + jnp.exp(-x))


def _silu(x):
    return x * _sigmoid(x)


_DOT_DIMS = {
    "nn": (((1,), (0,)), ((), ())),
    "nt": (((1,), (1,)), ((), ())),
    "tn": (((0,), (0,)), ((), ())),
}


def _mm(a, b, *, mode, name, tm, tn, tk, out_dtype=F32, a_scale=None, out_scale=None, resid=None, a_silu=False):
    if mode == "nn":
        (M, K), N = a.shape, b.shape[1]
    elif mode == "nt":
        (M, K), N = a.shape, b.shape[0]
    else:
        (K, M), N = a.shape, b.shape[1]
    tm, tn, tk = min(tm, M), min(tn, N), min(tk, K)
    assert M % tm == 0 and N % tn == 0 and K % tk == 0, (name, M, N, K, tm, tn, tk)
    nk = K // tk

    def body(*refs):
        refs = list(refs)
        a_ref, b_ref = refs.pop(0), refs.pop(0)
        as_ref = refs.pop(0) if a_scale is not None else None
        os_ref = refs.pop(0) if out_scale is not None else None
        r_ref = refs.pop(0) if resid is not None else None
        o_ref = refs.pop(0)
        acc_ref = refs.pop(0) if nk > 1 else None

        av = a_ref[...]
        if a_silu:
            av = _silu(av.astype(F32))
        if as_ref is not None:
            av = av.astype(F32) * as_ref[...]
        part = lax.dot_general(av.astype(BF16), b_ref[...].astype(BF16), _DOT_DIMS[mode],
                               preferred_element_type=F32)

        def finish(r):
            if os_ref is not None:
                r = r * os_ref[...]
            if r_ref is not None:
                r = r + r_ref[...].astype(F32)
            o_ref[...] = r.astype(out_dtype)

        if nk == 1:
            finish(part)
        else:
            k = pl.program_id(2)

            @pl.when(k == 0)
            def _():
                acc_ref[...] = part

            @pl.when(k > 0)
            def _():
                acc_ref[...] += part

            @pl.when(k == nk - 1)
            def _():
                finish(acc_ref[...])

    if mode == "nn":
        a_spec = pl.BlockSpec((tm, tk), lambda i, j, k: (i, k))
        b_spec = pl.BlockSpec((tk, tn), lambda i, j, k: (k, j))
        as_spec = pl.BlockSpec((1, tk), lambda i, j, k: (0, k))
    elif mode == "nt":
        a_spec = pl.BlockSpec((tm, tk), lambda i, j, k: (i, k))
        b_spec = pl.BlockSpec((tn, tk), lambda i, j, k: (j, k))
        as_spec = pl.BlockSpec((1, tk), lambda i, j, k: (0, k))
    else:
        a_spec = pl.BlockSpec((tk, tm), lambda i, j, k: (k, i))
        b_spec = pl.BlockSpec((tk, tn), lambda i, j, k: (k, j))
        as_spec = None
    in_specs, args = [a_spec, b_spec], [a, b]
    if a_scale is not None:
        in_specs.append(as_spec)
        args.append(a_scale)
    if out_scale is not None:
        in_specs.append(pl.BlockSpec((1, tn), lambda i, j, k: (0, j)))
        args.append(out_scale)
    if resid is not None:
        in_specs.append(pl.BlockSpec((tm, tn), lambda i, j, k: (i, j)))
        args.append(resid)
    return pl.pallas_call(
        body, name=name, grid=(M // tm, N // tn, nk),
        in_specs=in_specs, out_specs=pl.BlockSpec((tm, tn), lambda i, j, k: (i, j)),
        out_shape=jax.ShapeDtypeStruct((M, N), out_dtype),
        scratch_shapes=[pltpu.VMEM((tm, tn), F32)] if nk > 1 else [],
        compiler_params=_params("parallel", "parallel", "arbitrary"),
    )(*args)


_MM_ROWS = 1024


def _mm_sum_nt(pairs, *, name, tm=_MM_ROWS, tn=1024):
    M, N = pairs[0][0].shape[0], pairs[0][1].shape[0]
    tm, tn = _tile(M, tm), _tile(N, tn)
    spans, start = [], 0
    for a, b, tk, off in pairs:
        K = a.shape[1]
        assert a.shape[0] == M and b.shape[0] == N and K % tk == 0 and off % tk == 0, name
        spans.append((start, K // tk, tk, off // tk))
        start += K // tk
    total = start

    def body(*refs):
        o_ref, acc_ref = refs[-2:]
        k = pl.program_id(2)

        @pl.when(k == 0)
        def _():
            acc_ref[...] = jnp.zeros_like(acc_ref)

        for p, (s0, nk, _, _) in enumerate(spans):
            a_ref, b_ref = refs[2 * p], refs[2 * p + 1]

            @pl.when((k >= s0) & (k < s0 + nk))
            def _():
                acc_ref[...] += lax.dot_general(a_ref[...].astype(BF16), b_ref[...].astype(BF16), _DOT_DIMS["nt"],
                                                preferred_element_type=F32)

        @pl.when(k == total - 1)
        def _():
            o_ref[...] = acc_ref[...]

    def spec(rows, tk, s0, nk, koff, axis):
        def index(i, j, k):
            return ((i, j)[axis], jnp.clip(k - s0, 0, nk - 1) + koff)
        return pl.BlockSpec((rows, tk), index)

    in_specs, args = [], []
    for (a, b, _, _), (s0, nk, tk, koff) in zip(pairs, spans):
        in_specs += [spec(tm, tk, s0, nk, 0, 0), spec(tn, tk, s0, nk, koff, 1)]
        args += [a, b]
    return pl.pallas_call(
        body, name=name, grid=(M // tm, N // tn, total), in_specs=in_specs,
        out_specs=pl.BlockSpec((tm, tn), lambda i, j, k: (i, j)),
        out_shape=jax.ShapeDtypeStruct((M, N), F32), scratch_shapes=[pltpu.VMEM((tm, tn), F32)],
        compiler_params=_params("parallel", "parallel", "arbitrary"),
    )(*args)


def _norm_mod_fwd(x, gain, sc, sh, *, name):
    S, D = x.shape
    tr = min(1024, S)

    def body(x_ref, g_ref, sc_ref, sh_ref, h_ref):
        xv = x_ref[...]
        r = lax.rsqrt(jnp.mean(xv * xv, axis=-1, keepdims=True) + RMS_EPS)
        h_ref[...] = ((xv * r) * g_ref[...] * (1.0 + sc_ref[...]) + sh_ref[...]).astype(BF16)

    row = pl.BlockSpec((tr, D), lambda i: (i, 0))
    vec = pl.BlockSpec((1, D), lambda i: (0, 0))
    return pl.pallas_call(
        body, name=name, grid=(S // tr,), in_specs=[row, vec, vec, vec], out_specs=row,
        out_shape=jax.ShapeDtypeStruct((S, D), BF16), compiler_params=_params("parallel"),
    )(x, gain, sc, sh)


def _norm_mod_bwd(dh, x, dx_res, gain, sc, *, name):
    S, D = x.shape
    tr = min(512, S)
    n_steps = S // tr

    def body(dh_ref, x_ref, dxr_ref, g_ref, sc_ref, dx_ref, dsh_ref, dsc_ref, dgain_ref, acc_sh, acc_a):
        i = pl.program_id(0)
        xv = x_ref[...]
        r = lax.rsqrt(jnp.mean(xv * xv, axis=-1, keepdims=True) + RMS_EPS)
        n = xv * r
        dhv = dh_ref[...].astype(F32)
        dn = dhv * (g_ref[...] * (1.0 + sc_ref[...]))
        dx_ref[...] = dxr_ref[...] + r * (dn - n * jnp.mean(dn * n, axis=-1, keepdims=True))
        p_sh = jnp.sum(dhv, axis=0, keepdims=True)
        p_a = jnp.sum(dhv * n, axis=0, keepdims=True)

        @pl.when(i == 0)
        def _():
            acc_sh[...] = p_sh
            acc_a[...] = p_a

        @pl.when(i > 0)
        def _():
            acc_sh[...] += p_sh
            acc_a[...] += p_a

        @pl.when(i == n_steps - 1)
        def _():
            dsh_ref[...] = acc_sh[...]
            dsc_ref[...] = acc_a[...] * g_ref[...]
            dgain_ref[...] = acc_a[...] * (1.0 + sc_ref[...])

    row = pl.BlockSpec((tr, D), lambda i: (i, 0))
    vec = pl.BlockSpec((1, D), lambda i: (0, 0))
    vshape = jax.ShapeDtypeStruct((1, D), F32)
    return pl.pallas_call(
        body, name=name, grid=(n_steps,), in_specs=[row, row, row, vec, vec],
        out_specs=[row, vec, vec, vec],
        out_shape=[jax.ShapeDtypeStruct((S, D), F32), vshape, vshape, vshape],
        scratch_shapes=[pltpu.VMEM((1, D), F32), pltpu.VMEM((1, D), F32)],
        compiler_params=_params("arbitrary"),
    )(dh, x, dx_res, gain, sc)


def _wout_grad(gmat, w, gate, *, name):
    K, D = w.shape
    tr = min(256, K)
    n_steps = K // tr

    def body(g_ref, w_ref, gate_ref, dw_ref, dgate_ref, acc):
        i = pl.program_id(0)
        gv = g_ref[...]
        dw_ref[...] = (gv * gate_ref[...]).astype(BF16)
        part = jnp.sum(gv * w_ref[...], axis=0, keepdims=True)

        @pl.when(i == 0)
        def _():
            acc[...] = part

        @pl.when(i > 0)
        def _():
            acc[...] += part

        @pl.when(i == n_steps - 1)
        def _():
            dgate_ref[...] = acc[...]

    row = pl.BlockSpec((tr, D), lambda i: (i, 0))
    vec = pl.BlockSpec((1, D), lambda i: (0, 0))
    return pl.pallas_call(
        body, name=name, grid=(n_steps,), in_specs=[row, row, vec], out_specs=[row, vec],
        out_shape=[jax.ShapeDtypeStruct((K, D), BF16), jax.ShapeDtypeStruct((1, D), F32)],
        scratch_shapes=[pltpu.VMEM((1, D), F32)], compiler_params=_params("arbitrary"),
    )(gmat, w, gate)


def _loss_head(y, target, *, name):
    S, D = y.shape
    tr = min(1024, S)
    n_steps = S // tr

    def body(y_ref, t_ref, dy_ref, sse_ref, acc):
        i = pl.program_id(0)
        e = y_ref[...] - t_ref[...]
        dy_ref[...] = e * (1.0 / D)
        part = jnp.sum(e * e, axis=0, keepdims=True)

        @pl.when(i == 0)
        def _():
            acc[...] = part

        @pl.when(i > 0)
        def _():
            acc[...] += part

        @pl.when(i == n_steps - 1)
        def _():
            sse_ref[...] = jnp.sum(acc[...], axis=1, keepdims=True)

    row = pl.BlockSpec((tr, D), lambda i: (i, 0))
    return pl.pallas_call(
        body, name=name, grid=(n_steps,), in_specs=[row, row],
        out_specs=[row, pl.BlockSpec((1, 1), lambda i: (0, 0))],
        out_shape=[jax.ShapeDtypeStruct((S, D), F32), jax.ShapeDtypeStruct((1, 1), F32)],
        scratch_shapes=[pltpu.VMEM((1, D), F32)], compiler_params=_params("arbitrary"),
    )(y, target)


def _adamw(w, g_parts, m, v, *, name):
    R, C = w.shape
    P = g_parts.shape[0]
    tr = _tile(R, max(8, 4096 * LANES // C))
    c1 = 1.0 / (1.0 - ADAM_B1 ** ADAM_STEP)
    c2 = 1.0 / (1.0 - ADAM_B2 ** ADAM_STEP)

    def body(w_ref, g_ref, m_ref, v_ref, go_ref, d_ref, mo_ref, vo_ref):
        g = g_ref[0].astype(F32)
        for q in range(1, P):
            g = g + g_ref[q].astype(F32)
        mn = ADAM_B1 * m_ref[...] + (1.0 - ADAM_B1) * g
        vn = ADAM_B2 * v_ref[...] + (1.0 - ADAM_B2) * (g * g)
        go_ref[...] = g
        mo_ref[...] = mn
        vo_ref[...] = vn
        d_ref[...] = -ADAM_LR * ((mn * c1) / (jnp.sqrt(vn * c2) + ADAM_EPS) + ADAM_WD * w_ref[...])

    row = pl.BlockSpec((tr, C), lambda i: (i, 0))
    shp = jax.ShapeDtypeStruct((R, C), F32)
    return pl.pallas_call(
        body, name=name, grid=(R // tr,),
        in_specs=[row, pl.BlockSpec((P, tr, C), lambda i: (0, i, 0)), row, row],
        out_specs=[row, row, row, row], out_shape=[shp, shp, shp, shp],
        compiler_params=_params("parallel"),
    )(w, g_parts, m, v)


_HALO = 16


def _conv_taps(buf, w_ref, rows, cols):
    acc = None
    for j in range(GDN_CONV):
        term = buf[pl.ds(_HALO - (GDN_CONV - 1) + j, rows), cols] * w_ref[j:j + 1, cols]
        acc = term if acc is None else acc + term
    return acc


def _fill_conv_buf(buf, halo_ref, x_ref, rows, first):
    buf[0:_HALO, :] = jnp.where(first, 0.0, halo_ref[...].astype(F32))
    buf[_HALO:_HALO + rows, :] = x_ref[...].astype(F32)


_HM = 3 * GDN_DK
_GDN_ROWS = 256
_PREP_HEADS = 4


def _l2n(seg):
    return lax.rsqrt(jnp.sum(seg * seg, axis=-1, keepdims=True) + RMS_EPS)


def _head_cols(hh):
    return slice(hh * _HM, (hh + 1) * _HM)


def _gdn_prep_fwd(x, conv_w, *, name):
    S, C3 = x.shape
    CB = _PREP_HEADS * _HM
    RB = min(512, S)

    def body(x_ref, halo_ref, w_ref, o_ref, buf):
        i = pl.program_id(0)
        _fill_conv_buf(buf, halo_ref, x_ref, RB, i == 0)
        for hh in range(_PREP_HEADS):
            c0 = hh * _HM
            y = _silu(_conv_taps(buf, w_ref, RB, _head_cols(hh)))
            q, k = y[:, :GDN_DK], y[:, GDN_DK:2 * GDN_DK]
            o_ref[:, c0:c0 + GDN_DK] = q * (_l2n(q) * GDN_DK ** -0.5)
            o_ref[:, c0 + GDN_DK:c0 + 2 * GDN_DK] = k * _l2n(k)
            o_ref[:, c0 + 2 * GDN_DK:c0 + _HM] = y[:, 2 * GDN_DK:]

    hb = RB // _HALO
    return pl.pallas_call(
        body, name=name, grid=(S // RB, C3 // CB),
        in_specs=[pl.BlockSpec((RB, CB), lambda i, j: (i, j)),
                  pl.BlockSpec((_HALO, CB), lambda i, j: (jnp.maximum(i * hb - 1, 0), j)),
                  pl.BlockSpec((GDN_CONV, CB), lambda i, j: (0, j))],
        out_specs=pl.BlockSpec((RB, CB), lambda i, j: (i, j)),
        out_shape=jax.ShapeDtypeStruct((S, C3), F32),
        scratch_shapes=[pltpu.VMEM((RB + _HALO, CB), F32)],
        compiler_params=_params("parallel", "parallel"),
    )(x, x, conv_w)


def _gdn_prep_bwd_pre(dn, x, conv_w, *, name):
    S, C3 = x.shape
    CB = _PREP_HEADS * _HM
    RB = min(512, S)
    n_steps = S // RB

    def body(dn_ref, x_ref, halo_ref, w_ref, dc_ref, dw_ref, buf):
        i = pl.program_id(1)
        _fill_conv_buf(buf, halo_ref, x_ref, RB, i == 0)
        head_parts = []
        for hh in range(_PREP_HEADS):
            c0, cols = hh * _HM, _head_cols(hh)
            acc = _conv_taps(buf, w_ref, RB, cols)
            sg = _sigmoid(acc)
            y = acc * sg
            dsilu = sg * (1.0 + acc * (1.0 - sg))
            for part, scale in ((0, GDN_DK ** -0.5), (1, 1.0)):
                sl = slice(part * GDN_DK, (part + 1) * GDN_DK)
                seg = y[:, sl]
                r = _l2n(seg)
                n = seg * r
                d = dn_ref[:, c0 + part * GDN_DK:c0 + (part + 1) * GDN_DK] * scale
                dc_ref[:, c0 + part * GDN_DK:c0 + (part + 1) * GDN_DK] = (
                    r * (d - n * jnp.sum(d * n, axis=-1, keepdims=True)) * dsilu[:, sl])
            dc_ref[:, c0 + 2 * GDN_DK:c0 + _HM] = dn_ref[:, c0 + 2 * GDN_DK:c0 + _HM] * dsilu[:, 2 * GDN_DK:]
            dc = dc_ref[:, cols]
            taps = [jnp.sum(dc * buf[pl.ds(_HALO - (GDN_CONV - 1) + t, RB), cols], axis=0, keepdims=True)
                    for t in range(GDN_CONV)]
            head_parts.append(jnp.concatenate(taps + [jnp.zeros((8 - GDN_CONV, _HM), F32)], axis=0))
        part = jnp.concatenate(head_parts, axis=1)

        @pl.when(i == 0)
        def _():
            dw_ref[...] = part

        @pl.when(i > 0)
        def _():
            dw_ref[...] += part

    hb = RB // _HALO
    return pl.pallas_call(
        body, name=name, grid=(C3 // CB, n_steps),
        in_specs=[pl.BlockSpec((RB, CB), lambda j, i: (i, j)),
                  pl.BlockSpec((RB, CB), lambda j, i: (i, j)),
                  pl.BlockSpec((_HALO, CB), lambda j, i: (jnp.maximum(i * hb - 1, 0), j)),
                  pl.BlockSpec((GDN_CONV, CB), lambda j, i: (0, j))],
        out_specs=[pl.BlockSpec((RB, CB), lambda j, i: (i, j)),
                   pl.BlockSpec((8, CB), lambda j, i: (0, j))],
        out_shape=[jax.ShapeDtypeStruct((S, C3), F32), jax.ShapeDtypeStruct((8, C3), F32)],
        scratch_shapes=[pltpu.VMEM((RB + _HALO, CB), F32)],
        compiler_params=_params("parallel", "arbitrary"),
    )(dn, x, x, conv_w)


def _gdn_conv_bwd_x(dc, conv_w, *, name):
    S, C3 = dc.shape
    CB = _PREP_HEADS * _HM
    RB = min(512, S)
    n_steps = S // RB

    def body(dc_ref, halo_ref, w_ref, dx_ref, buf):
        i = pl.program_id(0)
        buf[0:RB, :] = dc_ref[...]
        buf[RB:RB + _HALO, :] = jnp.where(i == n_steps - 1, 0.0, halo_ref[...])
        for hh in range(_PREP_HEADS):
            cols = _head_cols(hh)
            acc = None
            for j in range(GDN_CONV):
                term = buf[pl.ds(GDN_CONV - 1 - j, RB), cols] * w_ref[j:j + 1, cols]
                acc = term if acc is None else acc + term
            dx_ref[:, cols] = acc.astype(BF16)

    hb = RB // _HALO
    last = S // _HALO - 1
    return pl.pallas_call(
        body, name=name, grid=(n_steps, C3 // CB),
        in_specs=[pl.BlockSpec((RB, CB), lambda i, j: (i, j)),
                  pl.BlockSpec((_HALO, CB), lambda i, j: (jnp.minimum((i + 1) * hb, last), j)),
                  pl.BlockSpec((GDN_CONV, CB), lambda i, j: (0, j))],
        out_specs=pl.BlockSpec((RB, CB), lambda i, j: (i, j)),
        out_shape=jax.ShapeDtypeStruct((S, C3), BF16),
        scratch_shapes=[pltpu.VMEM((RB + _HALO, CB), F32)],
        compiler_params=_params("parallel", "parallel"),
    )(dc, dc, conv_w)


def _split_bf16(a):
    hi = a.astype(BF16)
    return hi, (a - hi.astype(F32)).astype(BF16)


def _dot(a, b, dims="nn", exact=False):
    def dot(p, q):
        return lax.dot_general(p, q, _DOT_DIMS[dims], preferred_element_type=F32)

    if exact:
        (ah, al), (bh, bl) = _split_bf16(a), _split_bf16(b)
        return dot(ah, bh) + (dot(ah, bl) + dot(al, bh))
    return dot(a.astype(BF16), b.astype(BF16))


def _softplus(x):
    return jnp.maximum(x, 0.0) + jnp.log(1.0 + jnp.exp(-jnp.abs(x)))


def _to_col(row, eye):
    return jnp.sum(jnp.where(eye, row, 0.0), axis=1, keepdims=True)


def _to_row(col, eye):
    return jnp.sum(jnp.where(eye, col, 0.0), axis=0, keepdims=True)


def _unit_lower_inverse(low, ri, ci):
    n = range(len(low))
    C = low[0].shape[0]
    eye = jnp.where(ri == ci, 1.0, 0.0)
    pair = (ri >> 1) == (ci >> 1)
    x = [eye - jnp.where(pair, low[j], 0.0) for j in n]
    m, sh = 2, 1
    while m < C:
        join = ((ri >> (sh + 1)) == (ci >> (sh + 1))) & (((ri >> sh) & 1) == 1) & (((ci >> sh) & 1) == 0)
        y = [_dot(x[j], jnp.where(join, low[j], 0.0)) for j in n]
        x = [x[j] - _dot(y[j], x[j]) for j in n]
        m, sh = 2 * m, sh + 1
    lx = [_dot(low[j], x[j], exact=True) for j in n]
    corr = [_dot(x[j], eye - x[j] - lx[j]) for j in n]
    return [x[j] + corr[j] for j in n]


def _gdn_local_batch(qkv, g_row, beta_row, ri, ci):
    n = range(len(qkv))
    eye, tril, strict = ri == ci, ri >= ci, ri > ci
    q = [qkv[j][:, :GDN_DK] for j in n]
    k = [qkv[j][:, GDN_DK:2 * GDN_DK] for j in n]
    v = [qkv[j][:, 2 * GDN_DK:] for j in n]
    g_col = [_to_col(g_row[j], eye) for j in n]
    beta_col = [_to_col(beta_row[j], eye) for j in n]
    gc_col = [jnp.sum(jnp.where(tril, g_row[j], 0.0), axis=1, keepdims=True) for j in n]
    gc_row = [jnp.sum(jnp.where(ri <= ci, g_col[j], 0.0), axis=0, keepdims=True) for j in n]
    g_last = [jnp.sum(g_row[j], axis=1, keepdims=True) for j in n]
    decay = [jnp.where(tril, jnp.exp(jnp.minimum(gc_col[j] - gc_row[j], 0.0)), 0.0) for j in n]
    e_col = [jnp.exp(gc_col[j]) for j in n]
    f_col = [jnp.exp(g_last[j] - gc_col[j]) for j in n]
    e_last = [jnp.exp(g_last[j]) for j in n]
    kb = [k[j] * beta_col[j] for j in n]
    vb = [v[j] * beta_col[j] for j in n]
    kk = [_dot(kb[j], k[j], "nt") for j in n]
    qk = [_dot(q[j], k[j], "nt") for j in n]
    low = [jnp.where(strict, kk[j] * decay[j], 0.0) for j in n]
    att = [qk[j] * decay[j] for j in n]
    return dict(q=q, k=k, v=v, beta_col=beta_col, decay=decay, e_col=e_col, f_col=f_col, e_last=e_last,
                kb=kb, vb=vb, low=low, att=att, eye=eye, strict=strict, tril=tril)


def _chunk_iotas():
    C = GDN_CHUNK
    return lax.broadcasted_iota(jnp.int32, (C, C), 0), lax.broadcasted_iota(jnp.int32, (C, C), 1)


def _gdn_chunk_fwd(qkv, ab, a_log, dt_bias, *, name, riding=None):
    S = qkv.shape[0]
    H, C, DK = GDN_HEADS, GDN_CHUNK, GDN_DK
    RB = min(_GDN_ROWS, S)
    NCB, NB, NC = RB // C, S // RB, S // C
    heads = range(H)

    def body(qkv_ref, ab_ref, alog_ref, dtb_ref, *rest):
        n_ride = 0 if riding is None else len(riding)
        ride_srcs, rest = rest[:n_ride], rest[n_ride:]
        (o_ref, st_ref, t_ref), rest = rest[:3], rest[3:]
        ride_dsts, rest = rest[:n_ride], rest[n_ride:]
        state, u_s, w_s, qe_s, kf_s, att_s, *ride_sems = rest
        nb = pl.program_id(0)
        if riding is not None:
            finish_ride = _ride(nb == 0, nb == NB - 1, ride_srcs, ride_dsts, ride_sems, True)

        @pl.when(nb == 0)
        def _():
            state[...] = jnp.zeros_like(state)

        ri, ci = _chunk_iotas()
        neg_a = [-jnp.exp(alog_ref[h]) for h in heads]
        e_last = []
        for c in range(NCB):
            rows = pl.ds(c * C, C)
            g_row = [neg_a[h] * _softplus(ab_ref[h, c] + dtb_ref[h]) for h in heads]
            beta_row = [_sigmoid(ab_ref[H + h, c]) for h in heads]
            L = _gdn_local_batch([qkv_ref[rows, h * _HM:(h + 1) * _HM] for h in heads], g_row, beta_row, ri, ci)
            tinv = _unit_lower_inverse(L["low"], ri, ci)
            u = [_dot(tinv[h], L["vb"][h], exact=True) for h in heads]
            w = [_dot(tinv[h], L["kb"][h] * L["e_col"][h], exact=True) for h in heads]
            for h in heads:
                t_ref[h, c] = tinv[h]
                u_s[c, h] = u[h]
                w_s[c, h] = w[h].astype(BF16)
                qe_s[c, h] = (L["q"][h] * L["e_col"][h]).astype(BF16)
                kf_s[c, h] = (L["k"][h] * L["f_col"][h]).astype(BF16)
                att_s[c, h] = L["att"][h].astype(BF16)
            e_last.append(L["e_last"])
        st = [state[h] for h in heads]
        for c in range(NCB):
            rows = pl.ds(c * C, C)
            stb = [st[h].astype(BF16) for h in heads]
            vn = [u_s[c, h] - _dot(w_s[c, h], stb[h]) for h in heads]
            vnb = [vn[h].astype(BF16) for h in heads]
            out = [_dot(qe_s[c, h], stb[h]) + _dot(att_s[c, h], vnb[h]) for h in heads]
            new = [st[h] * e_last[c][h] + _dot(kf_s[c, h], vnb[h], "tn") for h in heads]
            for h in heads:
                o_ref[rows, h * DK:(h + 1) * DK] = out[h]
                st_ref[h, c] = st[h]
            st = new
        for h in heads:
            state[h] = st[h]
        if riding is not None:
            finish_ride()

    ride_args, ride_specs, ride_out, ride_scratch = _riding(riding, True)
    return pl.pallas_call(
        body, name=name, grid=(NB,),
        in_specs=[pl.BlockSpec((RB, H * _HM), lambda n: (n, 0)),
                  pl.BlockSpec((2 * H, NCB, 1, C), lambda n: (0, n, 0, 0)),
                  pl.BlockSpec((H, 1, 1), lambda n: (0, 0, 0)),
                  pl.BlockSpec((H, 1, 1), lambda n: (0, 0, 0))] + ride_specs,
        out_specs=[pl.BlockSpec((RB, H * DK), lambda n: (n, 0)),
                   pl.BlockSpec((H, NCB, DK, DK), lambda n: (0, n, 0, 0)),
                   pl.BlockSpec((H, NCB, C, C), lambda n: (0, n, 0, 0))] + ride_specs,
        out_shape=[jax.ShapeDtypeStruct((S, H * DK), F32),
                   jax.ShapeDtypeStruct((H, NC, DK, DK), F32),
                   jax.ShapeDtypeStruct((H, NC, C, C), F32)] + ride_out,
        scratch_shapes=[pltpu.VMEM((H, DK, DK), F32), pltpu.VMEM((NCB, H, C, DK), F32),
                        pltpu.VMEM((NCB, H, C, DK), BF16), pltpu.VMEM((NCB, H, C, DK), BF16),
                        pltpu.VMEM((NCB, H, C, DK), BF16), pltpu.VMEM((NCB, H, C, C), BF16)] + ride_scratch,
        compiler_params=_params("arbitrary"),
    )(qkv, ab, a_log, dt_bias, *ride_args)


_CHIP_PEERS = N_DEV // 2 - 1


def _chip_copies(src_refs, dst_refs, send_sems, recv_sems, local_sems, gather=False):
    x, y, c = lax.axis_index("x"), lax.axis_index("y"), lax.axis_index("c")
    here = 2 * x + y
    copies = []
    for a, (src_ref, dst_ref) in enumerate(zip(src_refs, dst_refs)):
        landing = dst_ref.at[here, c] if gather else dst_ref.at[here]
        copies.append(pltpu.make_async_copy(src_ref if gather else src_ref.at[here], landing, local_sems.at[a]))
        for rel in range(1, N_DEV // 2):
            px = 1 - x if rel & 2 else x
            py = 1 - y if rel & 1 else y
            k = a * _CHIP_PEERS + rel - 1
            copies.append(pltpu.make_async_remote_copy(
                src_ref=src_ref if gather else src_ref.at[2 * px + py], dst_ref=landing,
                send_sem=send_sems.at[k], recv_sem=recv_sems.at[k],
                device_id=(px, py, c), device_id_type=pl.DeviceIdType.MESH))
    return copies


def _chip_sems(n):
    return [pltpu.SemaphoreType.DMA((n * _CHIP_PEERS,)), pltpu.SemaphoreType.DMA((n * _CHIP_PEERS,)),
            pltpu.SemaphoreType.DMA((n,))]


def _riding(riding, gather):
    if riding is None:
        return [], [], [], []
    shapes = [jax.ShapeDtypeStruct(((N_DEV // 2, 2) + r.shape) if gather else r.shape, r.dtype) for r in riding]
    return list(riding), [pl.BlockSpec(memory_space=pl.ANY)] * len(riding), shapes, _chip_sems(len(riding))


def _ride(first, last, srcs, dsts, sems, gather):
    @pl.when(first)
    def _():
        for cp in _chip_copies(srcs, dsts, *sems, gather=gather):
            cp.start()

    def finish():
        @pl.when(last)
        def _():
            for cp in _chip_copies(srcs, dsts, *sems, gather=gather):
                cp.wait()

    return finish


def _gdn_chunk_bwd(qkv, ab, a_log, dt_bias, states, tinvs, do, *, name, riding=None):
    S = qkv.shape[0]
    H, C, DK = GDN_HEADS, GDN_CHUNK, GDN_DK
    RB = min(_GDN_ROWS, S)
    NCB, NB, NC = RB // C, S // RB, S // C
    heads = range(H)

    def body(qkv_ref, ab_ref, alog_ref, dtb_ref, st_ref, t_ref, do_ref, *rest):
        n_ride = 0 if riding is None else len(riding)
        ride_srcs, rest = rest[:n_ride], rest[n_ride:]
        (dqkv_ref, dab_ref, dalog_ref, ddtb_ref), rest = rest[:4], rest[4:]
        ride_dsts, rest = rest[:n_ride], rest[n_ride:]
        dstate, w_s, vn_s, qe_s, kf_s, att_s, dvn_s, dkf_s, *ride_sems = rest
        nb = pl.program_id(0)
        if riding is not None:
            finish_ride = _ride(nb == 0, nb == NB - 1, ride_srcs, ride_dsts, ride_sems, False)

        @pl.when(nb == 0)
        def _():
            dstate[...] = jnp.zeros_like(dstate)
            dalog_ref[...] = jnp.zeros_like(dalog_ref)
            ddtb_ref[...] = jnp.zeros_like(ddtb_ref)

        ri, ci = _chunk_iotas()
        neg_a = [-jnp.exp(alog_ref[h]) for h in heads]

        def local(c):
            rows = pl.ds(c * C, C)
            a_pre = [ab_ref[h, c] + dtb_ref[h] for h in heads]
            g_row = [neg_a[h] * _softplus(a_pre[h]) for h in heads]
            beta_row = [_sigmoid(ab_ref[H + h, c]) for h in heads]
            L = _gdn_local_batch([qkv_ref[rows, h * _HM:(h + 1) * _HM] for h in heads], g_row, beta_row, ri, ci)
            return L, a_pre, g_row, beta_row

        e_last = [None] * NCB
        for c in range(NCB):
            L, _, _, _ = local(c)
            kbe = [L["kb"][h] * L["e_col"][h] for h in heads]
            u = [_dot(t_ref[h, c], L["vb"][h], exact=True) for h in heads]
            w = [_dot(t_ref[h, c], kbe[h], exact=True) for h in heads]
            vn = [u[h] - _dot(w[h], st_ref[h, c]) for h in heads]
            for h in heads:
                w_s[c, h] = w[h].astype(BF16)
                vn_s[c, h] = vn[h].astype(BF16)
                qe_s[c, h] = (L["q"][h] * L["e_col"][h]).astype(BF16)
                kf_s[c, h] = (L["k"][h] * L["f_col"][h]).astype(BF16)
                att_s[c, h] = L["att"][h].astype(BF16)
            e_last[c] = L["e_last"]

        dst = [dstate[h] for h in heads]
        de_last = [None] * NCB
        for c in reversed(range(NCB)):
            rows = pl.ds(c * C, C)
            dob = [do_ref[rows, h * DK:(h + 1) * DK].astype(BF16) for h in heads]
            dstb = [dst[h].astype(BF16) for h in heads]
            dvn = [_dot(att_s[c, h], dob[h], "tn") + _dot(kf_s[c, h], dstb[h]) for h in heads]
            dkf = [_dot(vn_s[c, h], dstb[h], "nt") for h in heads]
            de_last[c] = [jnp.sum(jnp.sum(dst[h] * st_ref[h, c], axis=1, keepdims=True), axis=0, keepdims=True)
                          for h in heads]
            new = [dst[h] * e_last[c][h] + _dot(qe_s[c, h], dob[h], "tn")
                   - _dot(w_s[c, h], dvn[h].astype(BF16), "tn") for h in heads]
            for h in heads:
                dvn_s[c, h] = dvn[h]
                dkf_s[c, h] = dkf[h]
            dst = new
        for h in heads:
            dstate[h] = dst[h]

        for c in range(NCB):
            rows = pl.ds(c * C, C)
            L, a_pre, g_row, beta_row = local(c)
            q, k, v, kb, vb = L["q"], L["k"], L["v"], L["kb"], L["vb"]
            e_col, f_col, decay, beta_col = L["e_col"], L["f_col"], L["decay"], L["beta_col"]
            eye, strict, tril = L["eye"], L["strict"], L["tril"]
            tinv = [t_ref[h, c] for h in heads]
            stb = [st_ref[h, c].astype(BF16) for h in heads]
            dov = [do_ref[rows, h * DK:(h + 1) * DK] for h in heads]
            dvn = [dvn_s[c, h] for h in heads]
            dkf = [dkf_s[c, h] for h in heads]
            kbe = [kb[h] * e_col[h] for h in heads]
            datt = [jnp.where(tril, _dot(dov[h], vn_s[c, h], "nt"), 0.0) for h in heads]
            dqe = [_dot(dov[h], stb[h], "nt") for h in heads]
            dw = [-_dot(dvn[h], stb[h], "nt") for h in heads]
            dt = [_dot(dvn[h], vb[h], "nt") + _dot(dw[h], kbe[h], "nt") for h in heads]
            dvb = [_dot(tinv[h], dvn[h], "tn", exact=True) for h in heads]
            dkbe = [_dot(tinv[h], dw[h], "tn", exact=True) for h in heads]
            tdt = [_dot(tinv[h], dt[h], "tn", exact=True) for h in heads]
            dlow = [-jnp.where(strict, _dot(tdt[h], tinv[h], "nt", exact=True), 0.0) for h in heads]
            dkk = [dlow[h] * decay[h] for h in heads]
            dqk = [datt[h] * decay[h] for h in heads]
            dkb = [_dot(dkk[h], k[h]) + dkbe[h] * e_col[h] for h in heads]
            dk = [_dot(dkk[h], kb[h], "tn") + _dot(dqk[h], q[h], "tn") + dkf[h] * f_col[h] + dkb[h] * beta_col[h]
                  for h in heads]
            dq = [_dot(dqk[h], k[h]) + dqe[h] * e_col[h] for h in heads]
            for h in heads:
                dqkv_ref[rows, h * _HM:h * _HM + DK] = dq[h]
                dqkv_ref[rows, h * _HM + DK:h * _HM + 2 * DK] = dk[h]
                dqkv_ref[rows, h * _HM + 2 * DK:(h + 1) * _HM] = dvb[h] * beta_col[h]

            dbeta_col = [jnp.sum(k[h] * dkb[h] + v[h] * dvb[h], axis=1, keepdims=True) for h in heads]
            pmat = [dlow[h] * L["low"][h] + datt[h] * L["att"][h] for h in heads]
            df_col = [jnp.sum(k[h] * dkf[h], axis=1, keepdims=True) * f_col[h] for h in heads]
            dgc_col = [jnp.sum(pmat[h], axis=1, keepdims=True)
                       + jnp.sum(q[h] * dqe[h] + kb[h] * dkbe[h], axis=1, keepdims=True) * e_col[h] - df_col[h]
                       for h in heads]
            dgc_row = [_to_row(dgc_col[h], eye) - jnp.sum(pmat[h], axis=0, keepdims=True) for h in heads]
            dg_last = [jnp.sum(df_col[h], axis=0, keepdims=True) + de_last[c][h] * L["e_last"][h] for h in heads]
            dgc_c = [_to_col(dgc_row[h], eye) for h in heads]
            dg_row = [jnp.sum(jnp.where(ri >= ci, dgc_c[h], 0.0), axis=0, keepdims=True) + dg_last[h] for h in heads]
            dbeta_row = [_to_row(dbeta_col[h], eye) for h in heads]
            for h in heads:
                da_row = dg_row[h] * neg_a[h] * _sigmoid(a_pre[h])
                dab_ref[h, c] = da_row
                dab_ref[H + h, c] = dbeta_row[h] * beta_row[h] * (1.0 - beta_row[h])
                dalog_ref[h] += jnp.sum(dg_row[h] * g_row[h], axis=1, keepdims=True)
                ddtb_ref[h] += jnp.sum(da_row, axis=1, keepdims=True)

        if riding is not None:
            finish_ride()

    rev = lambda n: NB - 1 - n
    vec = pl.BlockSpec((H, 1, 1), lambda n: (0, 0, 0))
    gates = pl.BlockSpec((2 * H, NCB, 1, C), lambda n: (0, rev(n), 0, 0))
    wide = pl.BlockSpec((RB, H * _HM), lambda n: (rev(n), 0))
    item = lambda dt: pltpu.VMEM((NCB, H, C, DK), dt)
    ride_args, ride_specs, ride_out, ride_scratch = _riding(riding, False)
    return pl.pallas_call(
        body, name=name, grid=(NB,),
        in_specs=[wide, gates, vec, vec,
                  pl.BlockSpec((H, NCB, DK, DK), lambda n: (0, rev(n), 0, 0)),
                  pl.BlockSpec((H, NCB, C, C), lambda n: (0, rev(n), 0, 0)),
                  pl.BlockSpec((RB, H * DK), lambda n: (rev(n), 0))] + ride_specs,
        out_specs=[wide, gates, vec, vec] + ride_specs,
        out_shape=[jax.ShapeDtypeStruct((S, H * _HM), F32),
                   jax.ShapeDtypeStruct((2 * H, NC, 1, C), F32),
                   jax.ShapeDtypeStruct((H, 1, 1), F32),
                   jax.ShapeDtypeStruct((H, 1, 1), F32)] + ride_out,
        scratch_shapes=[pltpu.VMEM((H, DK, DK), F32), item(BF16), item(BF16), item(BF16), item(BF16),
                        pltpu.VMEM((NCB, H, C, C), BF16), item(F32), item(F32)] + ride_scratch,
        compiler_params=_params("arbitrary"),
    )(qkv, ab, a_log, dt_bias, states, tinvs, do, *ride_args)


def _gdn_outnorm_fwd(o, z, gain, *, name):
    S, HV = o.shape
    RB = min(512, S)

    def body(o_ref, z_ref, g_ref, y_ref):
        for h in range(HV // GDN_DK):
            cols = slice(h * GDN_DK, (h + 1) * GDN_DK)
            ov = o_ref[:, cols]
            r = lax.rsqrt(jnp.mean(ov * ov, axis=-1, keepdims=True) + RMS_EPS)
            y_ref[:, cols] = (ov * r * g_ref[...] * _silu(z_ref[:, cols].astype(F32))).astype(BF16)

    blk = pl.BlockSpec((RB, HV), lambda i: (i, 0))
    return pl.pallas_call(
        body, name=name, grid=(S // RB,),
        in_specs=[blk, blk, pl.BlockSpec((1, GDN_DK), lambda i: (0, 0))], out_specs=blk,
        out_shape=jax.ShapeDtypeStruct((S, HV), BF16), compiler_params=_params("parallel"),
    )(o, z, gain)


def _gdn_outnorm_bwd(dy, o, z, gain, *, name):
    S, HV = o.shape
    RB = min(512, S)

    def body(dy_ref, o_ref, z_ref, g_ref, do_ref, dz_ref, dg_ref):
        part = None
        for h in range(HV // GDN_DK):
            cols = slice(h * GDN_DK, (h + 1) * GDN_DK)
            ov = o_ref[:, cols]
            zv = z_ref[:, cols].astype(F32)
            dyv = dy_ref[:, cols].astype(F32)
            r = lax.rsqrt(jnp.mean(ov * ov, axis=-1, keepdims=True) + RMS_EPS)
            n = ov * r
            sg = _sigmoid(zv)
            dng = dyv * (zv * sg)
            dn = dng * g_ref[...]
            do_ref[:, cols] = r * (dn - n * jnp.mean(dn * n, axis=-1, keepdims=True))
            dz_ref[:, cols] = (dyv * (n * g_ref[...]) * (sg * (1.0 + zv * (1.0 - sg)))).astype(BF16)
            p = jnp.sum(dng * n, axis=0, keepdims=True)
            part = p if part is None else part + p

        @pl.when(pl.program_id(0) == 0)
        def _():
            dg_ref[...] = part

        @pl.when(pl.program_id(0) > 0)
        def _():
            dg_ref[...] += part

    blk = pl.BlockSpec((RB, HV), lambda i: (i, 0))
    vec = pl.BlockSpec((1, GDN_DK), lambda i: (0, 0))
    return pl.pallas_call(
        body, name=name, grid=(S // RB,),
        in_specs=[blk, blk, blk, vec], out_specs=[blk, blk, vec],
        out_shape=[jax.ShapeDtypeStruct((S, HV), F32), jax.ShapeDtypeStruct((S, HV), BF16),
                   jax.ShapeDtypeStruct((1, GDN_DK), F32)],
        compiler_params=_params("arbitrary"),
    )(dy, o, z, gain)


def _head_mask():
    return lax.broadcasted_iota(jnp.int32, (DSW_BLK, LANES), 1) < DSW_DH


def _per_head_sum(t, first):
    s0 = jnp.sum(jnp.where(first, t, 0.0), axis=-1, keepdims=True)
    s1 = jnp.sum(jnp.where(first, 0.0, t), axis=-1, keepdims=True)
    return jnp.where(first, s0, s1)


def _rms2(x, gain, first):
    r = lax.rsqrt(_per_head_sum(x * x, first) * (1.0 / DSW_DH) + RMS_EPS)
    xh = x * r
    return xh, r, xh * gain


def _rms2_bwd(dy, xh, r, gain, first):
    dxh = dy * gain
    return r * (dxh - xh * (_per_head_sum(dxh * xh, first) * (1.0 / DSW_DH)))


def _split_heads(x, first):
    return [jnp.where(first, x, 0.0).astype(BF16), jnp.where(first, 0.0, x).astype(BF16)]


_HP = LANES // DSW_DH
_DSW_W = DSW_HEADS * DSW_DH
_DSW_ROWS = 1024
_DSW_BATCH = 8


def _dsw_geometry(S, g):
    d = DSW_GROUPS[g][1]
    slab = DSW_BLK * d
    tb = max(1, min(_DSW_ROWS, S) // slab)
    return d, slab, tb, S // (tb * slab)


def _block_rows(t, r, slab, d):
    return pl.ds(t * slab + r, DSW_BLK) if d == 1 else pl.ds(t * slab + r, DSW_BLK, stride=d)


def _dsw_attn_fwd(q, k, v, bias, q_gain, k_gain, prev_out, *, g, name):
    S, WT = q.shape
    B = DSW_BLK
    d, slab, tb, n_tiles = _dsw_geometry(S, g)
    rt = tb * slab
    cb = g * (_DSW_W // LANES)
    batch_res = max(1, _DSW_BATCH // tb)

    def body(q_ref, kp_ref, kc_ref, vp_ref, vc_ref, bias_ref, qg_ref, kg_ref, *rest):
        o_ref, lse_ref = rest[-2:]
        i = pl.program_id(1)
        qg, kg = qg_ref[...] * DSW_DH ** -0.5, kg_ref[...]
        col = lax.broadcasted_iota(jnp.int32, (B, 2 * B), 1)
        first = _head_mask()
        heads = range(_HP)
        for r0 in range(0, d, batch_res):
            res = range(r0, min(d, r0 + batch_res))
            k_raw = {(r, -1): kp_ref[_block_rows(0, r, slab, d), :] for r in res}
            v_raw = {(r, -1): vp_ref[_block_rows(0, r, slab, d), :] for r in res}
            q_raw = {}
            for r in res:
                for t in range(tb):
                    rows = _block_rows(t, r, slab, d)
                    q_raw[r, t], k_raw[r, t], v_raw[r, t] = q_ref[rows, :], kc_ref[rows, :], vc_ref[rows, :]
            kn = {key: _rms2(x, kg, first)[2].astype(BF16) for key, x in k_raw.items()}
            vb = {key: x.astype(BF16) for key, x in v_raw.items()}
            qn = {key: _split_heads(_rms2(x, qg, first)[2], first) for key, x in q_raw.items()}
            items = [(r, t, h) for r in res for t in range(tb) for h in heads]
            s = {}
            for r, t, h in items:
                sv = _dot(qn[r, t][h], jnp.concatenate([kn[r, t - 1], kn[r, t]], axis=0), "nt") + bias_ref[h]
                s[r, t, h] = jnp.where((i == 0) & (col < B), NEG_BIG, sv) if t == 0 else sv
            m = {it: jnp.max(s[it], axis=-1, keepdims=True) for it in items}
            p = {it: jnp.exp(s[it] - m[it]) for it in items}
            l = {it: jnp.sum(p[it], axis=-1, keepdims=True) for it in items}
            o = {(r, t, h): _dot(p[r, t, h], jnp.concatenate([vb[r, t - 1], vb[r, t]], axis=0)) for r, t, h in items}
            for r in res:
                for t in range(tb):
                    rows = _block_rows(t, r, slab, d)
                    o_ref[rows, :] = jnp.where(first, o[r, t, 0] / l[r, t, 0], o[r, t, 1] / l[r, t, 1])
                    lse_ref[rows, :] = jnp.where(first, m[r, t, 0] + jnp.log(l[r, t, 0]),
                                                 m[r, t, 1] + jnp.log(l[r, t, 1]))

    cur = pl.BlockSpec((rt, LANES), lambda hp, i: (i, cb + hp))
    prev = pl.BlockSpec((slab, LANES), lambda hp, i: (jnp.maximum(i * tb - 1, 0), cb + hp))
    vec = pl.BlockSpec((1, LANES), lambda hp, i: (0, 0))
    shp = jax.ShapeDtypeStruct((S, WT), F32)
    carried = [] if prev_out is None else list(prev_out)
    n_in = 8
    return pl.pallas_call(
        body, name=name, grid=(_DSW_W // LANES, n_tiles),
        in_specs=[cur, prev, cur, prev, cur, pl.BlockSpec((_HP, B, 2 * B), lambda hp, i: (hp, 0, 0)), vec, vec]
                 + [pl.BlockSpec(memory_space=pl.ANY)] * len(carried),
        out_specs=[cur, cur], out_shape=[shp, shp],
        input_output_aliases={n_in + j: j for j in range(len(carried))},
        compiler_params=_params("parallel", "parallel"),
    )(q, k, k, v, v, bias, jnp.tile(q_gain, (1, _HP)), jnp.tile(k_gain, (1, _HP)), *carried)


def _dsw_merge(o_g, lse_g, *, name):
    S = o_g.shape[0]
    W, G = _DSW_W, len(DSW_GROUPS)
    tr = min(512, S)

    def body(o_ref, l_ref, out_ref, lse_ref):
        ls = [l_ref[:, g * W:(g + 1) * W] for g in range(G)]
        m = ls[0]
        for g in range(1, G):
            m = jnp.maximum(m, ls[g])
        den = jnp.zeros_like(m)
        acc = jnp.zeros_like(m)
        for g in range(G):
            wg = jnp.exp(ls[g] - m)
            den = den + wg
            acc = acc + wg * o_ref[:, g * W:(g + 1) * W]
        out_ref[...] = acc / den
        lse_ref[...] = m + jnp.log(den)

    wide = pl.BlockSpec((tr, G * W), lambda i: (i, 0))
    blk = pl.BlockSpec((tr, W), lambda i: (i, 0))
    shp = jax.ShapeDtypeStruct((S, W), F32)
    return pl.pallas_call(
        body, name=name, grid=(S // tr,), in_specs=[wide, wide], out_specs=[blk, blk],
        out_shape=[shp, shp], compiler_params=_params("parallel"),
    )(o_g, lse_g)


def _dsw_attn_bwd(q, k, v, o, lse, do, bias, q_gain, k_gain, prev_out, *, g, name):
    S, WT = q.shape
    B = DSW_BLK
    d, slab, tb, n_tiles = _dsw_geometry(S, g)
    rt = tb * slab
    cb = g * (_DSW_W // LANES)
    n_slabs = S // slab
    scale = DSW_DH ** -0.5
    batch_res = max(1, _DSW_BATCH // tb)

    def body(q_ref, qx_ref, kp_ref, kc_ref, vp_ref, vc_ref, o_ref, ox_ref, l_ref, lx_ref, do_ref, dox_ref,
             bias_ref, qg_ref, kg_ref, *rest):
        dq_ref, dk_ref, dv_ref, db_ref, dqg_ref, dkg_ref = rest[-6:]
        hp, i = pl.program_id(0), pl.program_id(1)
        qg, kg = qg_ref[...] * scale, kg_ref[...]
        col = lax.broadcasted_iota(jnp.int32, (B, 2 * B), 1)
        has_next = i < n_tiles - 1

        @pl.when(i == 0)
        def _():
            db_ref[...] = jnp.zeros_like(db_ref)

        dqg_acc = jnp.zeros((1, LANES), F32)
        dkg_acc = jnp.zeros((1, LANES), F32)
        first = _head_mask()
        heads = range(_HP)
        for r0 in range(0, d, batch_res):
            res = range(r0, min(d, r0 + batch_res))
            q_raw, k_raw, v_raw, o_raw, l_raw, do_raw = {}, {}, {}, {}, {}, {}
            for r in res:
                first_rows = _block_rows(0, r, slab, d)
                k_raw[r, -1], v_raw[r, -1] = kp_ref[first_rows, :], vp_ref[first_rows, :]
                for t in range(tb):
                    rows = _block_rows(t, r, slab, d)
                    q_raw[r, t], o_raw[r, t], l_raw[r, t], do_raw[r, t] = (
                        q_ref[rows, :], o_ref[rows, :], l_ref[rows, :], do_ref[rows, :])
                    k_raw[r, t], v_raw[r, t] = kc_ref[rows, :], vc_ref[rows, :]
                q_raw[r, tb], o_raw[r, tb], l_raw[r, tb], do_raw[r, tb] = (
                    qx_ref[first_rows, :], ox_ref[first_rows, :], lx_ref[first_rows, :], dox_ref[first_rows, :])
            kk = {key: _rms2(x, kg, first) for key, x in k_raw.items()}
            qq = {key: _rms2(x, qg, first) for key, x in q_raw.items()}
            knb = {key: kk[key][2].astype(BF16) for key in kk}
            qnb = {key: _split_heads(qq[key][2], first) for key in qq}
            vb = {key: x.astype(BF16) for key, x in v_raw.items()}
            dob = {key: _split_heads(x, first) for key, x in do_raw.items()}
            delta = {key: _per_head_sum(do_raw[key] * o_raw[key], first) for key in q_raw}
            pick = lambda x, h: x[:, h * DSW_DH:h * DSW_DH + 1]
            full = [(r, t, h) for r in res for t in range(tb) for h in heads]
            half = [(r, tb, h) for r in res for h in heads]
            s = {}
            for r, t, h in full:
                sv = _dot(qnb[r, t][h], jnp.concatenate([knb[r, t - 1], knb[r, t]], axis=0), "nt") + bias_ref[h]
                s[r, t, h] = jnp.where((i == 0) & (col < B), NEG_BIG, sv) if t == 0 else sv
            for r, t, h in half:
                s[r, t, h] = _dot(qnb[r, t][h], knb[r, t - 1], "nt") + bias_ref[h, :, 0:B]
            p = {(r, t, h): jnp.exp(s[r, t, h] - pick(l_raw[r, t], h)) for r, t, h in full}
            for r, t, h in half:
                p[r, t, h] = jnp.where(has_next, jnp.exp(s[r, t, h] - pick(l_raw[r, t], h)), 0.0)
            dp = {(r, t, h): _dot(dob[r, t][h], jnp.concatenate([vb[r, t - 1], vb[r, t]], axis=0), "nt")
                  for r, t, h in full}
            for r, t, h in half:
                dp[r, t, h] = _dot(dob[r, t][h], vb[r, t - 1], "nt")
            ds = {(r, t, h): p[r, t, h] * (dp[r, t, h] - pick(delta[r, t], h)) for r, t, h in full + half}
            pb = {it: p[it].astype(BF16) for it in ds}
            dsb = {it: ds[it].astype(BF16) for it in ds}
            for h in heads:
                tot = None
                for r in res:
                    for t in range(tb):
                        tot = ds[r, t, h] if tot is None else tot + ds[r, t, h]
                db_ref[h] += tot
            blocks = [(r, t) for r in res for t in range(tb)]
            keys2 = {(r, t): jnp.concatenate([knb[r, t - 1], knb[r, t]], axis=0) for r, t in blocks}
            dqn = {(r, t): jnp.where(first, _dot(dsb[r, t, 0], keys2[r, t]), _dot(dsb[r, t, 1], keys2[r, t]))
                   for r, t in blocks}
            prev_half = lambda x, r, t, h: x[r, t, h][:, :B] if t < tb else x[r, t, h]
            dkn = {(r, t): sum(_dot(dsb[r, t, h][:, B:], qnb[r, t][h], "tn")
                               + _dot(prev_half(dsb, r, t + 1, h), qnb[r, t + 1][h], "tn") for h in heads)
                   for r, t in blocks}
            dvv = {(r, t): sum(_dot(pb[r, t, h][:, B:], dob[r, t][h], "tn")
                               + _dot(prev_half(pb, r, t + 1, h), dob[r, t + 1][h], "tn") for h in heads)
                   for r, t in blocks}
            for r, t in blocks:
                dqg_acc = dqg_acc + jnp.sum(dqn[r, t] * qq[r, t][0], axis=0, keepdims=True)
                dkg_acc = dkg_acc + jnp.sum(dkn[r, t] * kk[r, t][0], axis=0, keepdims=True)
            for r, t in blocks:
                rows = _block_rows(t, r, slab, d)
                dq_ref[rows, :] = _rms2_bwd(dqn[r, t], qq[r, t][0], qq[r, t][1], qg, first)
                dk_ref[rows, :] = _rms2_bwd(dkn[r, t], kk[r, t][0], kk[r, t][1], kg, first)
                dv_ref[rows, :] = dvv[r, t]

        start = (hp == 0) & (i == 0)
        fold = lambda a: a[:, :DSW_DH] + a[:, DSW_DH:]

        @pl.when(start)
        def _():
            dqg_ref[...] = fold(dqg_acc) * scale
            dkg_ref[...] = fold(dkg_acc)

        @pl.when(jnp.logical_not(start))
        def _():
            dqg_ref[...] += fold(dqg_acc) * scale
            dkg_ref[...] += fold(dkg_acc)

    def spec(rows, pick, base):
        return pl.BlockSpec((rows, LANES), lambda hp, i: (pick(i), base + hp))

    same = lambda i: i
    before = lambda i: jnp.maximum(i * tb - 1, 0)
    after = lambda i: jnp.minimum((i + 1) * tb, n_slabs - 1)
    cur, cur1 = spec(rt, same, cb), spec(rt, same, 0)
    vec = pl.BlockSpec((1, DSW_DH), lambda hp, i: (0, 0))
    vec2 = pl.BlockSpec((1, LANES), lambda hp, i: (0, 0))
    bspec = pl.BlockSpec((_HP, B, 2 * B), lambda hp, i: (hp, 0, 0))
    shp = jax.ShapeDtypeStruct((S, WT), F32)
    vshp = jax.ShapeDtypeStruct((1, DSW_DH), F32)
    carried = [] if prev_out is None else list(prev_out)
    n_in = 15
    return pl.pallas_call(
        body, name=name, grid=(_DSW_W // LANES, n_tiles),
        in_specs=[cur, spec(slab, after, cb), spec(slab, before, cb), cur, spec(slab, before, cb), cur,
                  cur1, spec(slab, after, 0), cur1, spec(slab, after, 0), cur1, spec(slab, after, 0),
                  bspec, vec2, vec2] + [pl.BlockSpec(memory_space=pl.ANY)] * len(carried),
        out_specs=[cur, cur, cur, bspec, vec, vec],
        out_shape=[shp, shp, shp, jax.ShapeDtypeStruct(bias.shape, F32), vshp, vshp],
        input_output_aliases={n_in + j: j for j in range(len(carried))},
        compiler_params=_params("arbitrary", "arbitrary"),
    )(q, q, k, k, v, v, o, o, lse, lse, do, do, bias, jnp.tile(q_gain, (1, _HP)), jnp.tile(k_gain, (1, _HP)),
      *carried)


def _t5_bucket(dist):
    max_exact = REL_BUCKETS // 2
    scaled = jnp.log(jnp.maximum(dist, 1).astype(F32) / max_exact) / math.log(REL_MAX_DIST / max_exact)
    large = jnp.minimum(max_exact + (scaled * (REL_BUCKETS - max_exact)).astype(jnp.int32), REL_BUCKETS - 1)
    return jnp.where(dist < max_exact, dist, large)


def _dsw_band():
    dist = (jnp.arange(DSW_BLK)[:, None] + DSW_BLK) - jnp.arange(2 * DSW_BLK)[None, :]
    return dist, (dist >= 0) & (dist <= DSW_BLK)


def _dsw_bias(rel_bias):
    dist, band = _dsw_band()
    out = []
    for g, (_, d) in enumerate(DSW_GROUPS):
        hot = jax.nn.one_hot(_t5_bucket(jnp.maximum(dist, 0) * d), REL_BUCKETS, dtype=F32)
        tab = jnp.einsum("qkb,bh->hqk", hot, rel_bias[:, g * DSW_HEADS:(g + 1) * DSW_HEADS],
                         precision=lax.Precision.HIGHEST)
        out.append(jnp.where(band[None], tab, NEG_BIG))
    return jnp.stack(out)


def _dsw_bucket_onehot():
    dist, band = _dsw_band()
    out = []
    for _, d in DSW_GROUPS:
        hot = jax.nn.one_hot(_t5_bucket(jnp.maximum(dist, 0) * d), LANES, dtype=BF16)
        out.append(jnp.where(band[..., None], hot, 0).reshape(-1, LANES))
    return jnp.stack(out)


def _exchange(send, *, gather, name):
    R, C = send.shape[-2:]

    def body(src_ref, dst_ref, send_sems, recv_sems, local_sem):
        x, y, c = lax.axis_index("x"), lax.axis_index("y"), lax.axis_index("c")
        me = 4 * x + 2 * y + c
        mine = pltpu.make_async_copy(src_ref if gather else src_ref.at[me], dst_ref.at[me], local_sem)
        mine.start()
        copies = []
        for rel in range(1, N_DEV):
            px = 1 - x if rel & 4 else x
            py = 1 - y if rel & 2 else y
            pc = 1 - c if rel & 1 else c
            peer = 4 * px + 2 * py + pc
            cp = pltpu.make_async_remote_copy(
                src_ref=src_ref if gather else src_ref.at[peer], dst_ref=dst_ref.at[me],
                send_sem=send_sems.at[rel - 1], recv_sem=recv_sems.at[rel - 1],
                device_id=(px, py, pc), device_id_type=pl.DeviceIdType.MESH)
            cp.start()
            copies.append(cp)
        for cp in copies:
            cp.wait()
        mine.wait()

    return pl.pallas_call(
        body, name=name,
        in_specs=[pl.BlockSpec(memory_space=pl.ANY)], out_specs=pl.BlockSpec(memory_space=pl.ANY),
        out_shape=jax.ShapeDtypeStruct((N_DEV, R, C), send.dtype),
        scratch_shapes=[pltpu.SemaphoreType.DMA((N_DEV - 1,)), pltpu.SemaphoreType.DMA((N_DEV - 1,)),
                        pltpu.SemaphoreType.DMA(())],
    )(send)


def _gather_two_level(send, *, name):
    R, C = send.shape

    def body(src_ref, dst_ref, send_sems, recv_sems, local_sem):
        x, y, c = lax.axis_index("x"), lax.axis_index("y"), lax.axis_index("c")
        me, sibling = (x, y, c), (x, y, 1 - c)
        chips = [(1 - x, y), (x, 1 - y), (1 - x, 1 - y)]

        def slot(px, py, pc):
            return dst_ref.at[4 * px + 2 * py + pc]

        def copy(k, block, to, src=None):
            return pltpu.make_async_remote_copy(
                src_ref=slot(*block) if src is None else src, dst_ref=slot(*block),
                send_sem=send_sems.at[k], recv_sem=recv_sems.at[k],
                device_id=to, device_id_type=pl.DeviceIdType.MESH)

        mine = pltpu.make_async_copy(src_ref, slot(*me), local_sem)
        mine.start()
        first = [copy(0, me, sibling, src=src_ref)]
        first += [copy(1 + j, me, (*chip, c), src=src_ref) for j, chip in enumerate(chips)]
        for cp in first:
            cp.start()
        passed = [copy(4 + j, (*chip, c), sibling) for j, chip in enumerate(chips)]
        for j, chip in enumerate(chips):
            copy(1 + j, (*chip, c), me).wait_recv()
            passed[j].start()
        copy(0, sibling, me).wait_recv()
        for j, chip in enumerate(chips):
            copy(4 + j, (*chip, 1 - c), me).wait_recv()
        for cp in first + passed:
            cp.wait_send()
        mine.wait()

    return pl.pallas_call(
        body, name=name,
        in_specs=[pl.BlockSpec(memory_space=pl.ANY)], out_specs=pl.BlockSpec(memory_space=pl.ANY),
        out_shape=jax.ShapeDtypeStruct((N_DEV, R, C), send.dtype),
        scratch_shapes=[pltpu.SemaphoreType.DMA((N_DEV - 1,)), pltpu.SemaphoreType.DMA((N_DEV - 1,)),
                        pltpu.SemaphoreType.DMA(())],
    )(send)


_ANY = pl.BlockSpec(memory_space=pl.ANY)


def _swap_with_sibling(sends, *, name):
    n = len(sends)

    def body(*refs):
        x, y, c = lax.axis_index("x"), lax.axis_index("y"), lax.axis_index("c")
        send_sems, recv_sems = refs[2 * n:]
        copies = [pltpu.make_async_remote_copy(
            src_ref=refs[a], dst_ref=refs[n + a], send_sem=send_sems.at[a], recv_sem=recv_sems.at[a],
            device_id=(x, y, 1 - c), device_id_type=pl.DeviceIdType.MESH) for a in range(n)]
        for cp in copies:
            cp.start()
        for cp in copies:
            cp.wait()

    return pl.pallas_call(
        body, name=name, in_specs=[_ANY] * n, out_specs=[_ANY] * n,
        out_shape=[jax.ShapeDtypeStruct(s.shape, s.dtype) for s in sends],
        scratch_shapes=[pltpu.SemaphoreType.DMA((n,)), pltpu.SemaphoreType.DMA((n,))],
    )(*sends)


def _fill_from_sibling(bufs, *, name):
    n, n_chips = len(bufs), bufs[0].shape[0]

    def body(*refs):
        x, y, c = lax.axis_index("x"), lax.axis_index("y"), lax.axis_index("c")
        send_sems, recv_sems = refs[2 * n:]
        copies = [pltpu.make_async_remote_copy(
            src_ref=refs[a].at[q, c], dst_ref=refs[n + a].at[q, c],
            send_sem=send_sems.at[a * n_chips + q], recv_sem=recv_sems.at[a * n_chips + q],
            device_id=(x, y, 1 - c), device_id_type=pl.DeviceIdType.MESH) for a in range(n) for q in range(n_chips)]
        for cp in copies:
            cp.start()
        for cp in copies:
            cp.wait()

    return pl.pallas_call(
        body, name=name, in_specs=[_ANY] * n, out_specs=[_ANY] * n,
        out_shape=[jax.ShapeDtypeStruct(b.shape, b.dtype) for b in bufs],
        input_output_aliases={a: a for a in range(n)},
        scratch_shapes=[pltpu.SemaphoreType.DMA((n * n_chips,)), pltpu.SemaphoreType.DMA((n * n_chips,))],
    )(*bufs)


def _exchange_chips(send, *, name):
    def body(src_ref, dst_ref, *sems):
        copies = _chip_copies([src_ref], [dst_ref], *sems)
        for cp in copies:
            cp.start()
        for cp in copies:
            cp.wait()

    return pl.pallas_call(
        body, name=name, in_specs=[_ANY], out_specs=_ANY,
        out_shape=jax.ShapeDtypeStruct(send.shape, send.dtype), scratch_shapes=_chip_sems(1),
    )(send)


def _add_pair(a, b, *, name):
    lead, (R, C) = a.shape[:-2], a.shape[-2:]
    tr = _tile(R, max(8, 1024 * LANES // C))

    def body(a_ref, b_ref, o_ref):
        o_ref[...] = (a_ref[...].astype(F32) + b_ref[...].astype(F32)).astype(o_ref.dtype)

    blk = pl.BlockSpec((None,) * len(lead) + (tr, C), lambda *idx: idx + (0,))
    return pl.pallas_call(
        body, name=name, grid=lead + (R // tr,), in_specs=[blk, blk], out_specs=blk,
        out_shape=jax.ShapeDtypeStruct(a.shape, a.dtype),
        compiler_params=_params(*(("parallel",) * (len(lead) + 1))),
    )(a, b)


_BIG = ("w_ffn_in", "w_ffn_out", "gdn_w_in", "gdn_conv", "gdn_w_out", "dsw_w_in", "dsw_w_out")
_LATE = ("gdn_w_in", "gdn_conv", "gdn_w_out")
_EARLY = tuple(n for n in _BIG if n not in _LATE)
_NATIVE = ("w_ffn_in", "w_ffn_out", "dsw_w_in")
_SHARD_AXIS = {"w_ffn_in": 2, "w_ffn_out": 1, "gdn_w_in": 2, "gdn_conv": 2, "gdn_w_out": 1, "dsw_w_in": 2,
               "dsw_w_out": 2}
_SMALL = ("b_ada", "norm_mix", "norm_ffn", "gdn_a_log", "gdn_dt_bias", "gdn_out_norm", "dsw_q_norm",
          "dsw_k_norm", "rel_bias")
_ROW_ALIGN = 16
_BIG_ALIGN = 1024


def _ceil_to(n, m):
    return -(-n // m) * m


def _seg_rows(shape):
    return _ceil_to(_ceil_to(int(np.prod(shape)), LANES) // LANES, _ROW_ALIGN)


def _pack(arrs, total_align):
    lead = arrs[0][1]
    segs = []
    for a, nlead in arrs:
        assert nlead == lead
        bshape = a.shape[:nlead]
        n = int(np.prod(a.shape[nlead:]))
        rows = _seg_rows(a.shape[nlead:])
        flat = a.reshape(bshape + (n,))
        flat = jnp.pad(flat, [(0, 0)] * nlead + [(0, rows * LANES - n)])
        segs.append(flat.reshape(bshape + (rows, LANES)))
    buf = jnp.concatenate(segs, axis=lead)
    total = _ceil_to(buf.shape[lead], total_align)
    return jnp.pad(buf, [(0, 0)] * lead + [(0, total - buf.shape[lead]), (0, 0)])


def _unpack(buf, shapes, nlead):
    out, off = [], 0
    for shp in shapes:
        n, rows = int(np.prod(shp)), _seg_rows(shp)
        seg = lax.slice_in_dim(buf, off, off + rows, axis=nlead)
        seg = seg.reshape(buf.shape[:nlead] + (rows * LANES,))[..., :n]
        out.append(seg.reshape(buf.shape[:nlead] + tuple(shp)))
        off += rows
    return out


def _to_natural(g, axis):
    n, L, r, c = g.shape
    if axis == 2:
        return jnp.transpose(g, (1, 2, 0, 3)).reshape(L, r, n * c)
    return jnp.transpose(g, (1, 0, 2, 3)).reshape(L, n * r, c)


def _to_blocked(w, axis):
    L, R, C = w.shape
    if axis == 2:
        return jnp.transpose(w.reshape(L, R, N_DEV, C // N_DEV), (2, 0, 1, 3))
    return jnp.transpose(w.reshape(L, N_DEV, R // N_DEV, C), (1, 0, 2, 3))


def _hm(a):
    lead = a.shape[:-1]
    return jnp.swapaxes(a.reshape(lead + (3, GDN_HEADS, GDN_DK)), -3, -2).reshape(lead + (3 * GDN_HEADS * GDN_DK,))


def _un_hm(a):
    lead = a.shape[:-1]
    return jnp.swapaxes(a.reshape(lead + (GDN_HEADS, 3, GDN_DK)), -3, -2).reshape(lead + (3 * GDN_HEADS * GDN_DK,))


_TILES = (2048, 1536, 1408, 1024, 768, 512, 384, 256, 128, 64, 32, 16, 8)


def _tile(n, cap):
    for t in _TILES:
        if t <= cap and n % t == 0:
            return t
    return n


def _mm_auto(a, b, mode, name, **kw):
    if mode == "tn":
        (K, M), N = a.shape, b.shape[1]
        deep = 2048 if a.dtype == BF16 and b.dtype == BF16 else 1024
        tm, tn, tk = _tile(M, 1408), _tile(N, 1408), _tile(K, deep)
    else:
        M, K = a.shape
        N = b.shape[1] if mode == "nn" else b.shape[0]
        tm, tn, tk = _tile(M, _MM_ROWS), _tile(N, 1536), _tile(K, 1408)
    return _mm(a, b, mode=mode, name=name, tm=tm, tn=tn, tk=tk, **kw)


def _row(v):
    return v.reshape(1, -1)


def _ffn_in_act(h, w_in, *, name):
    S, D = h.shape
    F = w_in.shape[1] // 2
    tm, tn = _tile(S, _MM_ROWS), _tile(F, 1408)
    nj = F // tn

    def body(h_ref, wg_ref, wu_ref, g_ref, u_ref, a_ref):
        hv = h_ref[...]
        gate = jnp.dot(hv, wg_ref[...], preferred_element_type=F32)
        up = jnp.dot(hv, wu_ref[...], preferred_element_type=F32)
        g_ref[...] = gate.astype(BF16)
        u_ref[...] = up.astype(BF16)
        a_ref[...] = (_silu(gate) * up).astype(BF16)

    out = pl.BlockSpec((tm, tn), lambda i, j: (i, j))
    shp = jax.ShapeDtypeStruct((S, F), BF16)
    return pl.pallas_call(
        body, name=name, grid=(S // tm, nj),
        in_specs=[pl.BlockSpec((tm, D), lambda i, j: (i, 0)), pl.BlockSpec((D, tn), lambda i, j: (0, j)),
                  pl.BlockSpec((D, tn), lambda i, j: (0, j + nj))],
        out_specs=[out, out, out], out_shape=[shp, shp, shp],
        compiler_params=_params("parallel", "parallel"),
    )(h, w_in, w_in)


def _ffn_out_dx_act(dy, w_out, gate_vec, pg, pu, *, name):
    S, D = dy.shape
    F = w_out.shape[0]
    tm, tn = _tile(S, _MM_ROWS), _tile(F, 1408)

    def body(dy_ref, w_ref, gv_ref, pg_ref, pu_ref, dg_ref, du_ref):
        dyg = (dy_ref[...] * gv_ref[...]).astype(BF16)
        da = lax.dot_general(dyg, w_ref[...], _DOT_DIMS["nt"], preferred_element_type=F32)
        gate = pg_ref[...].astype(F32)
        up = pu_ref[...].astype(F32)
        sg = _sigmoid(gate)
        dg_ref[...] = (da * up * (sg * (1.0 + gate * (1.0 - sg)))).astype(BF16)
        du_ref[...] = (da * (gate * sg)).astype(BF16)

    blk = pl.BlockSpec((tm, tn), lambda i, j: (i, j))
    shp = jax.ShapeDtypeStruct((S, F), BF16)
    return pl.pallas_call(
        body, name=name, grid=(S // tm, F // tn),
        in_specs=[pl.BlockSpec((tm, D), lambda i, j: (i, 0)), pl.BlockSpec((tn, D), lambda i, j: (j, 0)),
                  pl.BlockSpec((1, D), lambda i, j: (0, 0)), blk, blk],
        out_specs=[blk, blk], out_shape=[shp, shp],
        compiler_params=_params("parallel", "parallel"),
    )(dy, w_out, gate_vec, pg, pu)


def _ffn_fwd(x, mod, gain, w_in, w_out, tag):
    sh, sc, gate = mod
    h = _norm_mod_fwd(x, gain, sc, sh, name=f"ffn_norm_{tag}")
    pg, pu, a = _ffn_in_act(h, w_in, name=f"ffn_in_{tag}")
    y = _mm_auto(a, w_out, "nn", f"ffn_out_{tag}", out_scale=gate, resid=x)
    return y, (x, h, pg, pu, a)


def _ffn_bwd(dy, saved, mod, gain, w_in, w_out, tag):
    sh, sc, gate = mod
    x, h, pg, pu, a = saved
    F = pg.shape[1]
    gmat = _mm_auto(a, dy, "tn", f"ffn_out_g_{tag}")
    dw_out, dgate = _wout_grad(gmat, w_out, gate, name=f"ffn_out_dw_{tag}")
    dpg, dpu = _ffn_out_dx_act(dy, w_out, gate, pg, pu, name=f"ffn_out_dx_{tag}")
    dw_in = jnp.concatenate([_mm_auto(h, dpg, "tn", f"ffn_in_dw_gate_{tag}", out_dtype=BF16),
                             _mm_auto(h, dpu, "tn", f"ffn_in_dw_up_{tag}", out_dtype=BF16)], axis=1)
    tk = _tile(F, 1408)
    dh = _mm_sum_nt([(dpg, w_in, tk, 0), (dpu, w_in, tk, F)], name=f"ffn_in_dx_{tag}")
    dx, dsh, dsc, dgain = _norm_mod_bwd(dh, x, dy, gain, sc, name=f"ffn_norm_bwd_{tag}")
    return dx, dict(w_in=dw_in, w_out=dw_out, gain=dgain, mod=(dsh, dsc, dgate))


def _gdn_fwd(x, mod, gain, W, riding=None):
    sh, sc, gate = mod
    S = x.shape[0]
    h = _norm_mod_fwd(x, gain, sc, sh, name="gdn_norm")
    pq = _mm_auto(h, W["gdn_qkv"], "nn", "gdn_in_qkv", out_dtype=BF16)
    z = _mm_auto(h, W["gdn_z"], "nn", "gdn_in_z", out_dtype=BF16)
    ab = _mm_auto(h, W["gdn_ab"], "nn", "gdn_in_ab")
    qkvn = _gdn_prep_fwd(pq, W["gdn_conv"], name="gdn_prep")
    ab4 = jnp.transpose(ab[:, :2 * GDN_HEADS]).reshape(2 * GDN_HEADS, S // GDN_CHUNK, 1, GDN_CHUNK)
    o, states, tinvs, *rode = _gdn_chunk_fwd(qkvn, ab4, W["gdn_a_log"], W["gdn_dt_bias"], name="gdn_chunk",
                                             riding=riding)
    o2 = _gdn_outnorm_fwd(o, z, W["gdn_out_norm"], name="gdn_outnorm")
    y = _mm_auto(o2, W["gdn_out"], "nn", "gdn_out", out_scale=gate, resid=x)
    return y, (x, h, pq, z, qkvn, ab4, o, states, tinvs, o2), (tuple(rode) if rode else None)


def _gdn_bwd(dy, saved, mod, gain, W, riding=None):
    sh, sc, gate = mod
    x, h, pq, z, qkvn, ab4, o, states, tinvs, o2 = saved
    S = x.shape[0]
    gmat = _mm_auto(o2, dy, "tn", "gdn_out_g")
    dw_out, dgate = _wout_grad(gmat, W["gdn_out"], gate, name="gdn_out_dw")
    do2 = _mm_auto(dy, W["gdn_out"], "nt", "gdn_out_dx", a_scale=gate)
    do, dz, dout_norm = _gdn_outnorm_bwd(do2, o, z, W["gdn_out_norm"], name="gdn_outnorm_bwd")
    dqkvn, dab4, da_log, ddt_bias, *rode = _gdn_chunk_bwd(
        qkvn, ab4, W["gdn_a_log"], W["gdn_dt_bias"], states, tinvs, do, name="gdn_chunk_bwd", riding=riding)
    dc, dconv8 = _gdn_prep_bwd_pre(dqkvn, pq, W["gdn_conv"], name="gdn_prep_bwd")
    dpq = _gdn_conv_bwd_x(dc, W["gdn_conv"], name="gdn_conv_bwd")
    dab = jnp.transpose(dab4.reshape(2 * GDN_HEADS, S))
    dab = jnp.pad(dab, ((0, 0), (0, LANES - 2 * GDN_HEADS))).astype(BF16)
    dw_qkv = _mm_auto(h, dpq, "tn", "gdn_in_qkv_dw", out_dtype=BF16)
    dw_z = _mm_auto(h, dz, "tn", "gdn_in_z_dw", out_dtype=BF16)
    dw_ab = _mm_auto(h, dab, "tn", "gdn_in_ab_dw", out_dtype=BF16)
    dh = _mm_sum_nt([(dpq, W["gdn_qkv"], 1024, 0), (dz, W["gdn_z"], 1024, 0), (dab, W["gdn_ab"], LANES, 0)],
                    name="gdn_in_dx")
    dx, dsh, dsc, dgain = _norm_mod_bwd(dh, x, dy, gain, sc, name="gdn_norm_bwd")
    dw_in = jnp.concatenate([_un_hm(dw_qkv), dw_z, dw_ab[:, :2 * GDN_HEADS]], axis=1)
    return dx, dict(gdn_w_in=dw_in, gdn_conv=_un_hm(dconv8[:GDN_CONV]), gdn_w_out=dw_out, gdn_out_norm=dout_norm,
                    gdn_a_log=da_log.reshape(1, GDN_HEADS), gdn_dt_bias=ddt_bias.reshape(1, GDN_HEADS),
                    gain=dgain, mod=(dsh, dsc, dgate)), (tuple(rode) if rode else None)


def _dsw_fwd(x, mod, gain, W):
    sh, sc, gate = mod
    h = _norm_mod_fwd(x, gain, sc, sh, name="dsw_norm")
    q, k, v = (_mm_auto(h, W[n], "nn", f"dsw_in_{n[-1]}") for n in ("dsw_q", "dsw_k", "dsw_v"))
    outs = None
    for g in range(len(DSW_GROUPS)):
        outs = _dsw_attn_fwd(q, k, v, W["dsw_bias"][g], W["dsw_q_norm"], W["dsw_k_norm"], outs, g=g,
                             name=f"dsw_attn_{g}")
    o, lse = _dsw_merge(*outs, name="dsw_merge")
    y = _mm_auto(o, W["dsw_out"], "nn", "dsw_out", out_scale=gate, resid=x)
    return y, (x, h, q, k, v, o, lse)


def _dsw_bwd(dy, saved, mod, gain, W):
    sh, sc, gate = mod
    x, h, q, k, v, o, lse = saved
    gmat = _mm_auto(o, dy, "tn", "dsw_out_g")
    dw_out, dgate = _wout_grad(gmat, W["dsw_out"], gate, name="dsw_out_dw")
    do = _mm_auto(dy, W["dsw_out"], "nt", "dsw_out_dx", a_scale=gate)
    G = len(DSW_GROUPS)
    dqkv, dbias, dq_norm, dk_norm = None, [], 0.0, 0.0
    for g in range(G):
        *dqkv, db, dqg, dkg = _dsw_attn_bwd(q, k, v, o, lse, do, W["dsw_bias"][g], W["dsw_q_norm"],
                                            W["dsw_k_norm"], dqkv, g=g, name=f"dsw_attn_bwd_{g}")
        dbias.append(db)
        dq_norm, dk_norm = dq_norm + dqg, dk_norm + dkg
    names = ("dsw_q", "dsw_k", "dsw_v")
    dws = [_mm_auto(h, d, "tn", f"dsw_in_{n[-1]}_dw", out_dtype=BF16) for n, d in zip(names, dqkv)]
    dh = _mm_sum_nt([(d, W[n], _tile(d.shape[1], 1024), 0) for n, d in zip(names, dqkv)], name="dsw_in_dx")
    dx, dsh, dsc, dgain = _norm_mod_bwd(dh, x, dy, gain, sc, name="dsw_norm_bwd")
    hot = _dsw_bucket_onehot()
    drel = [_mm(dbias[g].reshape(DSW_HEADS, -1), hot[g], mode="nn", name=f"dsw_rel_bias_{g}", tm=DSW_HEADS,
                tn=LANES, tk=8192)[:, :REL_BUCKETS] for g in range(G)]
    return dx, dict(dsw_w_in=jnp.concatenate(dws, axis=1), dsw_w_out=dw_out, dsw_q_norm=dq_norm,
                    dsw_k_norm=dk_norm, rel_bias=jnp.transpose(jnp.concatenate(drel, axis=0)),
                    gain=dgain, mod=(dsh, dsc, dgate))


def _local_step(x, target, mod, W, late_weights=None, early_pairs=None):
    mods = [[_row(mod[l, i]) for i in range(6)] for l in range(2)]
    nmix = [_row(W["norm_mix"][l]) for l in range(2)]
    nffn = [_row(W["norm_ffn"][l]) for l in range(2)]
    x1, s_gdn, arrived = _gdn_fwd(x, mods[0][:3], nmix[0], W, None if late_weights is None else late_weights[0])
    if late_weights is not None:
        W = {**W, **late_weights[1](arrived)}
    x2, s_f0 = _ffn_fwd(x1, mods[0][3:], nffn[0], W["w_ffn_in"][0], W["w_ffn_out"][0], "0")
    x3, s_dsw = _dsw_fwd(x2, mods[1][:3], nmix[1], W)
    x4, s_f1 = _ffn_fwd(x3, mods[1][3:], nffn[1], W["w_ffn_in"][1], W["w_ffn_out"][1], "1")
    dx4, sse = _loss_head(x4, target, name="loss_head")
    dx3, g_f1 = _ffn_bwd(dx4, s_f1, mods[1][3:], nffn[1], W["w_ffn_in"][1], W["w_ffn_out"][1], "1")
    dx2, g_dsw = _dsw_bwd(dx3, s_dsw, mods[1][:3], nmix[1], W)
    dx1, g_f0 = _ffn_bwd(dx2, s_f0, mods[0][3:], nffn[0], W["w_ffn_in"][0], W["w_ffn_out"][0], "0")
    grads = dict(
        w_ffn_in=jnp.stack([g_f0["w_in"], g_f1["w_in"]]), w_ffn_out=jnp.stack([g_f0["w_out"], g_f1["w_out"]]),
        dsw_w_in=g_dsw["dsw_w_in"][None], dsw_w_out=g_dsw["dsw_w_out"][None])
    riding = None if early_pairs is None else early_pairs(grads)
    dx0, g_gdn, rode = _gdn_bwd(dx1, s_gdn, mods[0][:3], nmix[0], W, riding)
    dmod = jnp.stack([jnp.concatenate(list(g_gdn["mod"]) + list(g_f0["mod"]), axis=0),
                      jnp.concatenate(list(g_dsw["mod"]) + list(g_f1["mod"]), axis=0)])
    grads.update(
        norm_mix=jnp.concatenate([g_gdn["gain"], g_dsw["gain"]], axis=0),
        norm_ffn=jnp.concatenate([g_f0["gain"], g_f1["gain"]], axis=0),
        gdn_w_in=g_gdn["gdn_w_in"][None], gdn_conv=g_gdn["gdn_conv"][None], gdn_w_out=g_gdn["gdn_w_out"][None],
        gdn_out_norm=g_gdn["gdn_out_norm"], gdn_a_log=g_gdn["gdn_a_log"], gdn_dt_bias=g_gdn["gdn_dt_bias"],
        dsw_q_norm=g_dsw["dsw_q_norm"], dsw_k_norm=g_dsw["dsw_k_norm"], rel_bias=g_dsw["rel_bias"])
    return sse, dx0, grads, dmod, rode


def _prepare_first(full, small):
    gw = full["gdn_w_in"][0]
    hk3 = 3 * GDN_HEADS * GDN_DK
    return dict(
        gdn_qkv=_hm(gw[:, :hk3]), gdn_z=gw[:, hk3:hk3 + GDN_HEADS * GDN_DK],
        gdn_ab=jnp.pad(gw[:, hk3 + GDN_HEADS * GDN_DK:], ((0, 0), (0, LANES - 2 * GDN_HEADS))),
        gdn_conv=_hm(full["gdn_conv"][0]), gdn_out=full["gdn_w_out"][0],
        norm_mix=small["norm_mix"], norm_ffn=small["norm_ffn"],
        gdn_a_log=small["gdn_a_log"].reshape(GDN_HEADS, 1, 1), gdn_dt_bias=small["gdn_dt_bias"].reshape(GDN_HEADS, 1, 1),
        gdn_out_norm=small["gdn_out_norm"], dsw_q_norm=small["dsw_q_norm"], dsw_k_norm=small["dsw_k_norm"],
        dsw_bias=_dsw_bias(small["rel_bias"]))


def _prepare_rest(full):
    di = full["dsw_w_in"][0]
    dq = di.shape[1] // 3
    return dict(w_ffn_in=full["w_ffn_in"], w_ffn_out=full["w_ffn_out"],
                dsw_q=di[:, :dq], dsw_k=di[:, dq:2 * dq], dsw_v=di[:, 2 * dq:], dsw_out=full["dsw_w_out"][0])


def _prepare_weights(full, small):
    return {**_prepare_first(full, small), **_prepare_rest(full)}


_W_NAMES = ("w_ada", "b_ada", "norm_mix", "norm_ffn", "w_ffn_in", "w_ffn_out", "gdn_w_in", "gdn_conv",
            "gdn_a_log", "gdn_dt_bias", "gdn_out_norm", "gdn_w_out", "dsw_w_in", "dsw_q_norm", "dsw_k_norm",
            "dsw_w_out", "rel_bias")
_PAD_BATCH = 16


def _pad_rows(a, rows):
    return jnp.pad(a, ((0, rows - a.shape[0]), (0, 0)))


def kernel(x, c, w_ada, b_ada, norm_mix, norm_ffn, w_ffn_in, w_ffn_out, gdn_w_in, gdn_conv, gdn_a_log, gdn_dt_bias, gdn_out_norm, gdn_w_out, dsw_w_in, dsw_q_norm, dsw_k_norm, dsw_w_out, rel_bias, loss_target, m_w_ada, m_b_ada, m_norm_mix, m_norm_ffn, m_w_ffn_in, m_w_ffn_out, m_gdn_w_in, m_gdn_conv, m_gdn_a_log, m_gdn_dt_bias, m_gdn_out_norm, m_gdn_w_out, m_dsw_w_in, m_dsw_q_norm, m_dsw_k_norm, m_dsw_w_out, m_rel_bias, v_w_ada, v_b_ada, v_norm_mix, v_norm_ffn, v_w_ffn_in, v_w_ffn_out, v_gdn_w_in, v_gdn_conv, v_gdn_a_log, v_gdn_dt_bias, v_gdn_out_norm, v_gdn_w_out, v_dsw_w_in, v_dsw_q_norm, v_dsw_k_norm, v_dsw_w_out, v_rel_bias):
    w = dict(zip(_W_NAMES, (w_ada, b_ada, norm_mix, norm_ffn, w_ffn_in, w_ffn_out, gdn_w_in, gdn_conv, gdn_a_log,
                            gdn_dt_bias, gdn_out_norm, gdn_w_out, dsw_w_in, dsw_q_norm, dsw_k_norm, dsw_w_out,
                            rel_bias)))
    m = dict(zip(_W_NAMES, (m_w_ada, m_b_ada, m_norm_mix, m_norm_ffn, m_w_ffn_in, m_w_ffn_out, m_gdn_w_in,
                            m_gdn_conv, m_gdn_a_log, m_gdn_dt_bias, m_gdn_out_norm, m_gdn_w_out, m_dsw_w_in,
                            m_dsw_q_norm, m_dsw_k_norm, m_dsw_w_out, m_rel_bias)))
    v = dict(zip(_W_NAMES, (v_w_ada, v_b_ada, v_norm_mix, v_norm_ffn, v_w_ffn_in, v_w_ffn_out, v_gdn_w_in,
                            v_gdn_conv, v_gdn_a_log, v_gdn_dt_bias, v_gdn_out_norm, v_gdn_w_out, v_dsw_w_in,
                            v_dsw_q_norm, v_dsw_k_norm, v_dsw_w_out, v_rel_bias)))
    D = x.shape[-1]
    n_layers, _, ada_cols = w_ada.shape

    c_all = _exchange(c.reshape(D // LANES, LANES), gather=True, name="gather_cond").reshape(N_DEV, D)
    c_pad = _pad_rows(c_all, _PAD_BATCH)
    proj = [_mm(c_pad, w_ada[l], mode="nn", name=f"ada_proj_{l}", tm=_PAD_BATCH, tn=ada_cols, tk=D, a_silu=True)
            for l in range(n_layers)]
    mod_send = _pack([(jnp.stack([p[:N_DEV] for p in proj], axis=1), 1)], _ROW_ALIGN)
    mod_recv = _exchange(mod_send, gather=False, name="scatter_mod")
    mod = _unpack(mod_recv, [(n_layers, ada_cols)], 1)[0]
    mod = jnp.transpose(mod, (1, 0, 2)).reshape(n_layers, N_DEV * ada_cols) + b_ada
    mod = mod.reshape(n_layers, 6, D)

    conv_hi = gdn_conv.astype(BF16)
    conv_lo = (gdn_conv - conv_hi.astype(F32)).astype(BF16)
    first_send = _pack([(conv_hi if n == "gdn_conv" else w[n].astype(BF16), 0) for n in _LATE] + [(conv_lo, 0)],
                       _ROW_ALIGN)
    parts = _unpack(_gather_two_level(first_send, name="gather_weights_first"),
                    [w[n].shape for n in _LATE] + [gdn_conv.shape], 1)
    full = {n: _to_natural(parts[i], _SHARD_AXIS[n]) for i, n in enumerate(_LATE)}
    full["gdn_conv"] = full["gdn_conv"].astype(F32) + _to_natural(parts[-1], _SHARD_AXIS["gdn_conv"]).astype(F32)
    W = _prepare_first(full, {n: w[n] for n in _SMALL})
    packed_early = tuple(n for n in _EARLY if n not in _NATIVE)
    rest_send = (_pack([(w[n].astype(BF16), 0) for n in packed_early], _ROW_ALIGN),
                 ) + tuple(w[n].astype(BF16) for n in _NATIVE)

    def rest_weights(arrived):
        filled = _fill_from_sibling(arrived, name="swap_weights")
        by_dev = [a.reshape((N_DEV,) + a.shape[2:]) for a in filled]
        blocks = dict(zip(packed_early, _unpack(by_dev[0], [w[n].shape for n in packed_early], 1)))
        blocks.update(zip(_NATIVE, by_dev[1:]))
        return _prepare_rest({n: _to_natural(blocks[n], _SHARD_AXIS[n]) for n in _EARLY})

    my_c = lax.axis_index("c")

    def pair_sums(g, packed, native, tag):
        sends = [_pack([(_to_blocked(g[n].astype(BF16), _SHARD_AXIS[n]), 1) for n in packed], _BIG_ALIGN)]
        sends += [_to_blocked(g[n].astype(BF16), _SHARD_AXIS[n]) for n in native]
        by_core = [s.reshape((N_DEV // 2, 2) + s.shape[1:]) for s in sends]
        keep = [lax.dynamic_index_in_dim(s, my_c, axis=1, keepdims=False) for s in by_core]
        give = [lax.dynamic_index_in_dim(s, 1 - my_c, axis=1, keepdims=False) for s in by_core]
        got = _swap_with_sibling(give, name=f"swap_grads_{tag}")
        return tuple(_add_pair(k, t, name=f"add_sibling_grads_{tag}_{j}") for j, (k, t) in enumerate(zip(keep, got)))

    sse, grad_x, grads, dmod, early_recv = _local_step(
        x[0], loss_target[0], mod, W, late_weights=(rest_send, rest_weights),
        early_pairs=lambda g: pair_sums(g, packed_early, _NATIVE, "early"))
    loss = lax.psum(0.5 * sse[0, 0] / D, ("x", "y", "c"))
    grads["b_ada"] = dmod.reshape(n_layers, 6 * D)
    late_recv = _exchange_chips(pair_sums(grads, _LATE, (), "late")[0], name="scatter_grads_late")
    g_parts = dict(zip(packed_early, _unpack(early_recv[0], [w[n].shape for n in packed_early], 1)))
    g_parts.update(zip(_NATIVE, early_recv[1:]))
    g_parts.update(zip(_LATE, _unpack(late_recv, [w[n].shape for n in _LATE], 1)))

    dmod_send = _pack([(jnp.transpose(dmod.reshape(n_layers, N_DEV, ada_cols), (1, 0, 2)), 1)], _ROW_ALIGN)
    small_send = _pack([(grads[n].reshape(w[n].shape), 0) for n in _SMALL], _ROW_ALIGN)
    s_recv = _exchange(jnp.concatenate(
        [dmod_send, jnp.broadcast_to(small_send[None], (N_DEV,) + small_send.shape)], axis=1),
        gather=False, name="scatter_small")
    dmod_rows = dmod_send.shape[1]

    out = {}
    kinds = ("grad", "delta", "new_m", "new_v")
    for n in _BIG:
        g4 = g_parts[n]
        rows2d = lambda a: a.reshape((-1, w[n].shape[-1]))
        res = _adamw(rows2d(w[n]), g4.reshape((g4.shape[0], -1, w[n].shape[-1])), rows2d(m[n]), rows2d(v[n]),
                     name=f"adamw_{n}")
        for kind, buf in zip(kinds, res):
            out[kind, n] = buf.reshape(w[n].shape)

    dmod_all = _unpack(lax.slice_in_dim(s_recv, 0, dmod_rows, axis=1), [(n_layers, ada_cols)], 1)[0]
    g_ada = jnp.stack([_mm(c_pad, _pad_rows(dmod_all[:, l], _PAD_BATCH), mode="tn", name=f"ada_dw_{l}",
                           tm=D, tn=ada_cols, tk=_PAD_BATCH, a_silu=True) for l in range(n_layers)])
    flat = lambda a: a.reshape(n_layers * D, ada_cols)
    res = _adamw(flat(w_ada), flat(g_ada)[None], flat(m_w_ada), flat(v_w_ada), name="adamw_ada")
    for kind, buf in zip(("grad", "delta", "new_m", "new_v"), res):
        out[kind, "w_ada"] = buf.reshape(w_ada.shape)

    small_parts = lax.slice_in_dim(s_recv, dmod_rows, s_recv.shape[1], axis=1)
    packed = [_pack([(t[n], 0) for n in _SMALL], _ROW_ALIGN) for t in (w, m, v)]
    res = _adamw(packed[0], small_parts, packed[1], packed[2], name="adamw_replicated")
    for kind, buf in zip(("grad", "delta", "new_m", "new_v"), res):
        for n, a in zip(_SMALL, _unpack(buf, [w[n].shape for n in _SMALL], 0)):
            out[kind, n] = a

    return (loss, grad_x[None]) + tuple(out[kind, n] for kind in ("grad", "delta", "new_m", "new_v")
                                        for n in _W_NAMES)
```

```python
import functools
import math

import numpy as np
import jax
import jax.numpy as jnp
from jax import lax
from jax.experimental import pallas as pl
from jax.experimental.pallas import tpu as pltpu

F32 = jnp.float32
BF16 = jnp.bfloat16

N_DEV = 8
RMS_EPS = 1e-6
LANES = 128
V7X_VMEM_LIMIT = 48 * 1024 * 1024

GDN_HEADS = 8
GDN_DK = 128
GDN_CHUNK = 64
GDN_CONV = 4
DSW_GROUPS = ((128, 1), (512, 4), (2048, 16))
DSW_HEADS = 8
DSW_DH = 64
DSW_BLK = 128
REL_BUCKETS = 32
REL_MAX_DIST = 2048

ADAM_LR = 0.001
ADAM_B1 = 0.9
ADAM_B2 = 0.999
ADAM_EPS = 1e-08
ADAM_WD = 0.01
ADAM_STEP = 10

NEG_BIG = -1e30


def _params(*sem):
    return pltpu.CompilerParams(dimension_semantics=sem, vmem_limit_bytes=V7X_VMEM_LIMIT)


def _sigmoid(x):
    return 1.0 / (1.0 + jnp.exp(-x))


def _silu(x):
    return x * _sigmoid(x)


_DOT_DIMS = {
    "nn": (((1,), (0,)), ((), ())),
    "nt": (((1,), (1,)), ((), ())),
    "tn": (((0,), (0,)), ((), ())),
}


def _mm(a, b, *, mode, name, tm, tn, tk, out_dtype=F32, a_scale=None, out_scale=None, resid=None, a_silu=False):
    if mode == "nn":
        (M, K), N = a.shape, b.shape[1]
    elif mode == "nt":
        (M, K), N = a.shape, b.shape[0]
    else:
        (K, M), N = a.shape, b.shape[1]
    tm, tn, tk = min(tm, M), min(tn, N), min(tk, K)
    assert M % tm == 0 and N % tn == 0 and K % tk == 0, (name, M, N, K, tm, tn, tk)
    nk = K // tk

    def body(*refs):
        refs = list(refs)
        a_ref, b_ref = refs.pop(0), refs.pop(0)
        as_ref = refs.pop(0) if a_scale is not None else None
        os_ref = refs.pop(0) if out_scale is not None else None
        r_ref = refs.pop(0) if resid is not None else None
        o_ref = refs.pop(0)
        acc_ref = refs.pop(0) if nk > 1 else None

        av = a_ref[...]
        if a_silu:
            av = _silu(av.astype(F32))
        if as_ref is not None:
            av = av.astype(F32) * as_ref[...]
        part = lax.dot_general(av.astype(BF16), b_ref[...].astype(BF16), _DOT_DIMS[mode],
                               preferred_element_type=F32)

        def finish(r):
            if os_ref is not None:
                r = r * os_ref[...]
            if r_ref is not None:
                r = r + r_ref[...].astype(F32)
            o_ref[...] = r.astype(out_dtype)

        if nk == 1:
            finish(part)
        else:
            k = pl.program_id(2)

            @pl.when(k == 0)
            def _():
                acc_ref[...] = part

            @pl.when(k > 0)
            def _():
                acc_ref[...] += part

            @pl.when(k == nk - 1)
            def _():
                finish(acc_ref[...])

    if mode == "nn":
        a_spec = pl.BlockSpec((tm, tk), lambda i, j, k: (i, k))
        b_spec = pl.BlockSpec((tk, tn), lambda i, j, k: (k, j))
        as_spec = pl.BlockSpec((1, tk), lambda i, j, k: (0, k))
    elif mode == "nt":
        a_spec = pl.BlockSpec((tm, tk), lambda i, j, k: (i, k))
        b_spec = pl.BlockSpec((tn, tk), lambda i, j, k: (j, k))
        as_spec = pl.BlockSpec((1, tk), lambda i, j, k: (0, k))
    else:
        a_spec = pl.BlockSpec((tk, tm), lambda i, j, k: (k, i))
        b_spec = pl.BlockSpec((tk, tn), lambda i, j, k: (k, j))
        as_spec = None
    in_specs, args = [a_spec, b_spec], [a, b]
    if a_scale is not None:
        in_specs.append(as_spec)
        args.append(a_scale)
    if out_scale is not None:
        in_specs.append(pl.BlockSpec((1, tn), lambda i, j, k: (0, j)))
        args.append(out_scale)
    if resid is not None:
        in_specs.append(pl.BlockSpec((tm, tn), lambda i, j, k: (i, j)))
        args.append(resid)
    return pl.pallas_call(
        body, name=name, grid=(M // tm, N // tn, nk),
        in_specs=in_specs, out_specs=pl.BlockSpec((tm, tn), lambda i, j, k: (i, j)),
        out_shape=jax.ShapeDtypeStruct((M, N), out_dtype),
        scratch_shapes=[pltpu.VMEM((tm, tn), F32)] if nk > 1 else [],
        compiler_params=_params("parallel", "parallel", "arbitrary"),
    )(*args)


_MM_ROWS = 1024


def _mm_sum_nt(pairs, *, name, tm=_MM_ROWS, tn=1024):
    M, N = pairs[0][0].shape[0], pairs[0][1].shape[0]
    tm, tn = _tile(M, tm), _tile(N, tn)
    spans, start = [], 0
    for a, b, tk, off in pairs:
        K = a.shape[1]
        assert a.shape[0] == M and b.shape[0] == N and K % tk == 0 and off % tk == 0, name
        spans.append((start, K // tk, tk, off // tk))
        start += K // tk
    total = start

    def body(*refs):
        o_ref, acc_ref = refs[-2:]
        k = pl.program_id(2)

        @pl.when(k == 0)
        def _():
            acc_ref[...] = jnp.zeros_like(acc_ref)

        for p, (s0, nk, _, _) in enumerate(spans):
            a_ref, b_ref = refs[2 * p], refs[2 * p + 1]

            @pl.when((k >= s0) & (k < s0 + nk))
            def _():
                acc_ref[...] += lax.dot_general(a_ref[...].astype(BF16), b_ref[...].astype(BF16), _DOT_DIMS["nt"],
                                                preferred_element_type=F32)

        @pl.when(k == total - 1)
        def _():
            o_ref[...] = acc_ref[...]

    def spec(rows, tk, s0, nk, koff, axis):
        def index(i, j, k):
            return ((i, j)[axis], jnp.clip(k - s0, 0, nk - 1) + koff)
        return pl.BlockSpec((rows, tk), index)

    in_specs, args = [], []
    for (a, b, _, _), (s0, nk, tk, koff) in zip(pairs, spans):
        in_specs += [spec(tm, tk, s0, nk, 0, 0), spec(tn, tk, s0, nk, koff, 1)]
        args += [a, b]
    return pl.pallas_call(
        body, name=name, grid=(M // tm, N // tn, total), in_specs=in_specs,
        out_specs=pl.BlockSpec((tm, tn), lambda i, j, k: (i, j)),
        out_shape=jax.ShapeDtypeStruct((M, N), F32), scratch_shapes=[pltpu.VMEM((tm, tn), F32)],
        compiler_params=_params("parallel", "parallel", "arbitrary"),
    )(*args)


def _norm_mod_fwd(x, gain, sc, sh, *, name):
    S, D = x.shape
    tr = min(1024, S)

    def body(x_ref, g_ref, sc_ref, sh_ref, h_ref):
        xv = x_ref[...]
        r = lax.rsqrt(jnp.mean(xv * xv, axis=-1, keepdims=True) + RMS_EPS)
        h_ref[...] = ((xv * r) * g_ref[...] * (1.0 + sc_ref[...]) + sh_ref[...]).astype(BF16)

    row = pl.BlockSpec((tr, D), lambda i: (i, 0))
    vec = pl.BlockSpec((1, D), lambda i: (0, 0))
    return pl.pallas_call(
        body, name=name, grid=(S // tr,), in_specs=[row, vec, vec, vec], out_specs=row,
        out_shape=jax.ShapeDtypeStruct((S, D), BF16), compiler_params=_params("parallel"),
    )(x, gain, sc, sh)


def _norm_mod_bwd(dh, x, dx_res, gain, sc, *, name):
    S, D = x.shape
    tr = min(512, S)
    n_steps = S // tr

    def body(dh_ref, x_ref, dxr_ref, g_ref, sc_ref, dx_ref, dsh_ref, dsc_ref, dgain_ref, acc_sh, acc_a):
        i = pl.program_id(0)
        xv = x_ref[...]
        r = lax.rsqrt(jnp.mean(xv * xv, axis=-1, keepdims=True) + RMS_EPS)
        n = xv * r
        dhv = dh_ref[...].astype(F32)
        dn = dhv * (g_ref[...] * (1.0 + sc_ref[...]))
        dx_ref[...] = dxr_ref[...] + r * (dn - n * jnp.mean(dn * n, axis=-1, keepdims=True))
        p_sh = jnp.sum(dhv, axis=0, keepdims=True)
        p_a = jnp.sum(dhv * n, axis=0, keepdims=True)

        @pl.when(i == 0)
        def _():
            acc_sh[...] = p_sh
            acc_a[...] = p_a

        @pl.when(i > 0)
        def _():
            acc_sh[...] += p_sh
            acc_a[...] += p_a

        @pl.when(i == n_steps - 1)
        def _():
            dsh_ref[...] = acc_sh[...]
            dsc_ref[...] = acc_a[...] * g_ref[...]
            dgain_ref[...] = acc_a[...] * (1.0 + sc_ref[...])

    row = pl.BlockSpec((tr, D), lambda i: (i, 0))
    vec = pl.BlockSpec((1, D), lambda i: (0, 0))
    vshape = jax.ShapeDtypeStruct((1, D), F32)
    return pl.pallas_call(
        body, name=name, grid=(n_steps,), in_specs=[row, row, row, vec, vec],
        out_specs=[row, vec, vec, vec],
        out_shape=[jax.ShapeDtypeStruct((S, D), F32), vshape, vshape, vshape],
        scratch_shapes=[pltpu.VMEM((1, D), F32), pltpu.VMEM((1, D), F32)],
        compiler_params=_params("arbitrary"),
    )(dh, x, dx_res, gain, sc)


def _wout_grad(gmat, w, gate, *, name):
    K, D = w.shape
    tr = min(256, K)
    n_steps = K // tr

    def body(g_ref, w_ref, gate_ref, dw_ref, dgate_ref, acc):
        i = pl.program_id(0)
        gv = g_ref[...]
        dw_ref[...] = (gv * gate_ref[...]).astype(BF16)
        part = jnp.sum(gv * w_ref[...], axis=0, keepdims=True)

        @pl.when(i == 0)
        def _():
            acc[...] = part

        @pl.when(i > 0)
        def _():
            acc[...] += part

        @pl.when(i == n_steps - 1)
        def _():
            dgate_ref[...] = acc[...]

    row = pl.BlockSpec((tr, D), lambda i: (i, 0))
    vec = pl.BlockSpec((1, D), lambda i: (0, 0))
    return pl.pallas_call(
        body, name=name, grid=(n_steps,), in_specs=[row, row, vec], out_specs=[row, vec],
        out_shape=[jax.ShapeDtypeStruct((K, D), BF16), jax.ShapeDtypeStruct((1, D), F32)],
        scratch_shapes=[pltpu.VMEM((1, D), F32)], compiler_params=_params("arbitrary"),
    )(gmat, w, gate)


def _loss_head(y, target, *, name):
    S, D = y.shape
    tr = min(1024, S)
    n_steps = S // tr

    def body(y_ref, t_ref, dy_ref, sse_ref, acc):
        i = pl.program_id(0)
        e = y_ref[...] - t_ref[...]
        dy_ref[...] = e * (1.0 / D)
        part = jnp.sum(e * e, axis=0, keepdims=True)

        @pl.when(i == 0)
        def _():
            acc[...] = part

        @pl.when(i > 0)
        def _():
            acc[...] += part

        @pl.when(i == n_steps - 1)
        def _():
            sse_ref[...] = jnp.sum(acc[...], axis=1, keepdims=True)

    row = pl.BlockSpec((tr, D), lambda i: (i, 0))
    return pl.pallas_call(
        body, name=name, grid=(n_steps,), in_specs=[row, row],
        out_specs=[row, pl.BlockSpec((1, 1), lambda i: (0, 0))],
        out_shape=[jax.ShapeDtypeStruct((S, D), F32), jax.ShapeDtypeStruct((1, 1), F32)],
        scratch_shapes=[pltpu.VMEM((1, D), F32)], compiler_params=_params("arbitrary"),
    )(y, target)


def _adamw(w, g_parts, m, v, *, name):
    R, C = w.shape
    P = g_parts.shape[0]
    tr = _tile(R, max(8, 4096 * LANES // C))
    c1 = 1.0 / (1.0 - ADAM_B1 ** ADAM_STEP)
    c2 = 1.0 / (1.0 - ADAM_B2 ** ADAM_STEP)

    def body(w_ref, g_ref, m_ref, v_ref, go_ref, d_ref, mo_ref, vo_ref):
        g = g_ref[0].astype(F32)
        for q in range(1, P):
            g = g + g_ref[q].astype(F32)
        mn = ADAM_B1 * m_ref[...] + (1.0 - ADAM_B1) * g
        vn = ADAM_B2 * v_ref[...] + (1.0 - ADAM_B2) * (g * g)
        go_ref[...] = g
        mo_ref[...] = mn
        vo_ref[...] = vn
        d_ref[...] = -ADAM_LR * ((mn * c1) / (jnp.sqrt(vn * c2) + ADAM_EPS) + ADAM_WD * w_ref[...])

    row = pl.BlockSpec((tr, C), lambda i: (i, 0))
    shp = jax.ShapeDtypeStruct((R, C), F32)
    return pl.pallas_call(
        body, name=name, grid=(R // tr,),
        in_specs=[row, pl.BlockSpec((P, tr, C), lambda i: (0, i, 0)), row, row],
        out_specs=[row, row, row, row], out_shape=[shp, shp, shp, shp],
        compiler_params=_params("parallel"),
    )(w, g_parts, m, v)


_HALO = 16


def _conv_taps(buf, w_ref, rows, cols):
    acc = None
    for j in range(GDN_CONV):
        term = buf[pl.ds(_HALO - (GDN_CONV - 1) + j, rows), cols] * w_ref[j:j + 1, cols]
        acc = term if acc is None else acc + term
    return acc


def _fill_conv_buf(buf, halo_ref, x_ref, rows, first):
    buf[0:_HALO, :] = jnp.where(first, 0.0, halo_ref[...].astype(F32))
    buf[_HALO:_HALO + rows, :] = x_ref[...].astype(F32)


_HM = 3 * GDN_DK
_GDN_ROWS = 256
_PREP_HEADS = 4


def _l2n(seg):
    return lax.rsqrt(jnp.sum(seg * seg, axis=-1, keepdims=True) + RMS_EPS)


def _head_cols(hh):
    return slice(hh * _HM, (hh + 1) * _HM)


def _gdn_prep_fwd(x, conv_w, *, name):
    S, C3 = x.shape
    CB = _PREP_HEADS * _HM
    RB = min(512, S)

    def body(x_ref, halo_ref, w_ref, o_ref, buf):
        i = pl.program_id(0)
        _fill_conv_buf(buf, halo_ref, x_ref, RB, i == 0)
        for hh in range(_PREP_HEADS):
            c0 = hh * _HM
            y = _silu(_conv_taps(buf, w_ref, RB, _head_cols(hh)))
            q, k = y[:, :GDN_DK], y[:, GDN_DK:2 * GDN_DK]
            o_ref[:, c0:c0 + GDN_DK] = q * (_l2n(q) * GDN_DK ** -0.5)
            o_ref[:, c0 + GDN_DK:c0 + 2 * GDN_DK] = k * _l2n(k)
            o_ref[:, c0 + 2 * GDN_DK:c0 + _HM] = y[:, 2 * GDN_DK:]

    hb = RB // _HALO
    return pl.pallas_call(
        body, name=name, grid=(S // RB, C3 // CB),
        in_specs=[pl.BlockSpec((RB, CB), lambda i, j: (i, j)),
                  pl.BlockSpec((_HALO, CB), lambda i, j: (jnp.maximum(i * hb - 1, 0), j)),
                  pl.BlockSpec((GDN_CONV, CB), lambda i, j: (0, j))],
        out_specs=pl.BlockSpec((RB, CB), lambda i, j: (i, j)),
        out_shape=jax.ShapeDtypeStruct((S, C3), F32),
        scratch_shapes=[pltpu.VMEM((RB + _HALO, CB), F32)],
        compiler_params=_params("parallel", "parallel"),
    )(x, x, conv_w)


def _gdn_prep_bwd_pre(dn, x, conv_w, *, name):
    S, C3 = x.shape
    CB = _PREP_HEADS * _HM
    RB = min(512, S)
    n_steps = S // RB

    def body(dn_ref, x_ref, halo_ref, w_ref, dc_ref, dw_ref, buf):
        i = pl.program_id(1)
        _fill_conv_buf(buf, halo_ref, x_ref, RB, i == 0)
        head_parts = []
        for hh in range(_PREP_HEADS):
            c0, cols = hh * _HM, _head_cols(hh)
            acc = _conv_taps(buf, w_ref, RB, cols)
            sg = _sigmoid(acc)
            y = acc * sg
            dsilu = sg * (1.0 + acc * (1.0 - sg))
            for part, scale in ((0, GDN_DK ** -0.5), (1, 1.0)):
                sl = slice(part * GDN_DK, (part + 1) * GDN_DK)
                seg = y[:, sl]
                r = _l2n(seg)
                n = seg * r
                d = dn_ref[:, c0 + part * GDN_DK:c0 + (part + 1) * GDN_DK] * scale
                dc_ref[:, c0 + part * GDN_DK:c0 + (part + 1) * GDN_DK] = (
                    r * (d - n * jnp.sum(d * n, axis=-1, keepdims=True)) * dsilu[:, sl])
            dc_ref[:, c0 + 2 * GDN_DK:c0 + _HM] = dn_ref[:, c0 + 2 * GDN_DK:c0 + _HM] * dsilu[:, 2 * GDN_DK:]
            dc = dc_ref[:, cols]
            taps = [jnp.sum(dc * buf[pl.ds(_HALO - (GDN_CONV - 1) + t, RB), cols], axis=0, keepdims=True)
                    for t in range(GDN_CONV)]
            head_parts.append(jnp.concatenate(taps + [jnp.zeros((8 - GDN_CONV, _HM), F32)], axis=0))
        part = jnp.concatenate(head_parts, axis=1)

        @pl.when(i == 0)
        def _():
            dw_ref[...] = part

        @pl.when(i > 0)
        def _():
            dw_ref[...] += part

    hb = RB // _HALO
    return pl.pallas_call(
        body, name=name, grid=(C3 // CB, n_steps),
        in_specs=[pl.BlockSpec((RB, CB), lambda j, i: (i, j)),
                  pl.BlockSpec((RB, CB), lambda j, i: (i, j)),
                  pl.BlockSpec((_HALO, CB), lambda j, i: (jnp.maximum(i * hb - 1, 0), j)),
                  pl.BlockSpec((GDN_CONV, CB), lambda j, i: (0, j))],
        out_specs=[pl.BlockSpec((RB, CB), lambda j, i: (i, j)),
                   pl.BlockSpec((8, CB), lambda j, i: (0, j))],
        out_shape=[jax.ShapeDtypeStruct((S, C3), F32), jax.ShapeDtypeStruct((8, C3), F32)],
        scratch_shapes=[pltpu.VMEM((RB + _HALO, CB), F32)],
        compiler_params=_params("parallel", "arbitrary"),
    )(dn, x, x, conv_w)


def _gdn_conv_bwd_x(dc, conv_w, *, name):
    S, C3 = dc.shape
    CB = _PREP_HEADS * _HM
    RB = min(512, S)
    n_steps = S // RB

    def body(dc_ref, halo_ref, w_ref, dx_ref, buf):
        i = pl.program_id(0)
        buf[0:RB, :] = dc_ref[...]
        buf[RB:RB + _HALO, :] = jnp.where(i == n_steps - 1, 0.0, halo_ref[...])
        for hh in range(_PREP_HEADS):
            cols = _head_cols(hh)
            acc = None
            for j in range(GDN_CONV):
                term = buf[pl.ds(GDN_CONV - 1 - j, RB), cols] * w_ref[j:j + 1, cols]
                acc = term if acc is None else acc + term
            dx_ref[:, cols] = acc.astype(BF16)

    hb = RB // _HALO
    last = S // _HALO - 1
    return pl.pallas_call(
        body, name=name, grid=(n_steps, C3 // CB),
        in_specs=[pl.BlockSpec((RB, CB), lambda i, j: (i, j)),
                  pl.BlockSpec((_HALO, CB), lambda i, j: (jnp.minimum((i + 1) * hb, last), j)),
                  pl.BlockSpec((GDN_CONV, CB), lambda i, j: (0, j))],
        out_specs=pl.BlockSpec((RB, CB), lambda i, j: (i, j)),
        out_shape=jax.ShapeDtypeStruct((S, C3), BF16),
        scratch_shapes=[pltpu.VMEM((RB + _HALO, CB), F32)],
        compiler_params=_params("parallel", "parallel"),
    )(dc, dc, conv_w)


def _split_bf16(a):
    hi = a.astype(BF16)
    return hi, (a - hi.astype(F32)).astype(BF16)


def _dot(a, b, dims="nn", exact=False):
    def dot(p, q):
        return lax.dot_general(p, q, _DOT_DIMS[dims], preferred_element_type=F32)

    if exact:
        (ah, al), (bh, bl) = _split_bf16(a), _split_bf16(b)
        return dot(ah, bh) + (dot(ah, bl) + dot(al, bh))
    return dot(a.astype(BF16), b.astype(BF16))


def _softplus(x):
    return jnp.maximum(x, 0.0) + jnp.log(1.0 + jnp.exp(-jnp.abs(x)))


def _to_col(row, eye):
    return jnp.sum(jnp.where(eye, row, 0.0), axis=1, keepdims=True)


def _to_row(col, eye):
    return jnp.sum(jnp.where(eye, col, 0.0), axis=0, keepdims=True)


def _unit_lower_inverse(low, ri, ci):
    n = range(len(low))
    C = low[0].shape[0]
    eye = jnp.where(ri == ci, 1.0, 0.0)
    pair = (ri >> 1) == (ci >> 1)
    x = [eye - jnp.where(pair, low[j], 0.0) for j in n]
    m, sh = 2, 1
    while m < C:
        join = ((ri >> (sh + 1)) == (ci >> (sh + 1))) & (((ri >> sh) & 1) == 1) & (((ci >> sh) & 1) == 0)
        y = [_dot(x[j], jnp.where(join, low[j], 0.0)) for j in n]
        x = [x[j] - _dot(y[j], x[j]) for j in n]
        m, sh = 2 * m, sh + 1
    lx = [_dot(low[j], x[j], exact=True) for j in n]
    corr = [_dot(x[j], eye - x[j] - lx[j]) for j in n]
    return [x[j] + corr[j] for j in n]


def _gdn_local_batch(qkv, g_row, beta_row, ri, ci):
    n = range(len(qkv))
    eye, tril, strict = ri == ci, ri >= ci, ri > ci
    q = [qkv[j][:, :GDN_DK] for j in n]
    k = [qkv[j][:, GDN_DK:2 * GDN_DK] for j in n]
    v = [qkv[j][:, 2 * GDN_DK:] for j in n]
    g_col = [_to_col(g_row[j], eye) for j in n]
    beta_col = [_to_col(beta_row[j], eye) for j in n]
    gc_col = [jnp.sum(jnp.where(tril, g_row[j], 0.0), axis=1, keepdims=True) for j in n]
    gc_row = [jnp.sum(jnp.where(ri <= ci, g_col[j], 0.0), axis=0, keepdims=True) for j in n]
    g_last = [jnp.sum(g_row[j], axis=1, keepdims=True) for j in n]
    decay = [jnp.where(tril, jnp.exp(jnp.minimum(gc_col[j] - gc_row[j], 0.0)), 0.0) for j in n]
    e_col = [jnp.exp(gc_col[j]) for j in n]
    f_col = [jnp.exp(g_last[j] - gc_col[j]) for j in n]
    e_last = [jnp.exp(g_last[j]) for j in n]
    kb = [k[j] * beta_col[j] for j in n]
    vb = [v[j] * beta_col[j] for j in n]
    kk = [_dot(kb[j], k[j], "nt") for j in n]
    qk = [_dot(q[j], k[j], "nt") for j in n]
    low = [jnp.where(strict, kk[j] * decay[j], 0.0) for j in n]
    att = [qk[j] * decay[j] for j in n]
    return dict(q=q, k=k, v=v, beta_col=beta_col, decay=decay, e_col=e_col, f_col=f_col, e_last=e_last,
                kb=kb, vb=vb, low=low, att=att, eye=eye, strict=strict, tril=tril)


def _chunk_iotas():
    C = GDN_CHUNK
    return lax.broadcasted_iota(jnp.int32, (C, C), 0), lax.broadcasted_iota(jnp.int32, (C, C), 1)


def _gdn_chunk_fwd(qkv, ab, a_log, dt_bias, *, name, riding=None):
    S = qkv.shape[0]
    H, C, DK = GDN_HEADS, GDN_CHUNK, GDN_DK
    RB = min(_GDN_ROWS, S)
    NCB, NB, NC = RB // C, S // RB, S // C
    heads = range(H)

    def body(qkv_ref, ab_ref, alog_ref, dtb_ref, *rest):
        n_ride = 0 if riding is None else len(riding)
        ride_srcs, rest = rest[:n_ride], rest[n_ride:]
        (o_ref, st_ref, t_ref), rest = rest[:3], rest[3:]
        ride_dsts, rest = rest[:n_ride], rest[n_ride:]
        state, u_s, w_s, qe_s, kf_s, att_s, *ride_sems = rest
        nb = pl.program_id(0)
        if riding is not None:
            finish_ride = _ride(nb == 0, nb == NB - 1, ride_srcs, ride_dsts, ride_sems, True)

        @pl.when(nb == 0)
        def _():
            state[...] = jnp.zeros_like(state)

        ri, ci = _chunk_iotas()
        neg_a = [-jnp.exp(alog_ref[h]) for h in heads]
        e_last = []
        for c in range(NCB):
            rows = pl.ds(c * C, C)
            g_row = [neg_a[h] * _softplus(ab_ref[h, c] + dtb_ref[h]) for h in heads]
            beta_row = [_sigmoid(ab_ref[H + h, c]) for h in heads]
            L = _gdn_local_batch([qkv_ref[rows, h * _HM:(h + 1) * _HM] for h in heads], g_row, beta_row, ri, ci)
            tinv = _unit_lower_inverse(L["low"], ri, ci)
            u = [_dot(tinv[h], L["vb"][h], exact=True) for h in heads]
            w = [_dot(tinv[h], L["kb"][h] * L["e_col"][h], exact=True) for h in heads]
            for h in heads:
                t_ref[h, c] = tinv[h]
                u_s[c, h] = u[h]
                w_s[c, h] = w[h].astype(BF16)
                qe_s[c, h] = (L["q"][h] * L["e_col"][h]).astype(BF16)
                kf_s[c, h] = (L["k"][h] * L["f_col"][h]).astype(BF16)
                att_s[c, h] = L["att"][h].astype(BF16)
            e_last.append(L["e_last"])
        st = [state[h] for h in heads]
        for c in range(NCB):
            rows = pl.ds(c * C, C)
            stb = [st[h].astype(BF16) for h in heads]
            vn = [u_s[c, h] - _dot(w_s[c, h], stb[h]) for h in heads]
            vnb = [vn[h].astype(BF16) for h in heads]
            out = [_dot(qe_s[c, h], stb[h]) + _dot(att_s[c, h], vnb[h]) for h in heads]
            new = [st[h] * e_last[c][h] + _dot(kf_s[c, h], vnb[h], "tn") for h in heads]
            for h in heads:
                o_ref[rows, h * DK:(h + 1) * DK] = out[h]
                st_ref[h, c] = st[h]
            st = new
        for h in heads:
            state[h] = st[h]
        if riding is not None:
            finish_ride()

    ride_args, ride_specs, ride_out, ride_scratch = _riding(riding, True)
    return pl.pallas_call(
        body, name=name, grid=(NB,),
        in_specs=[pl.BlockSpec((RB, H * _HM), lambda n: (n, 0)),
                  pl.BlockSpec((2 * H, NCB, 1, C), lambda n: (0, n, 0, 0)),
                  pl.BlockSpec((H, 1, 1), lambda n: (0, 0, 0)),
                  pl.BlockSpec((H, 1, 1), lambda n: (0, 0, 0))] + ride_specs,
        out_specs=[pl.BlockSpec((RB, H * DK), lambda n: (n, 0)),
                   pl.BlockSpec((H, NCB, DK, DK), lambda n: (0, n, 0, 0)),
                   pl.BlockSpec((H, NCB, C, C), lambda n: (0, n, 0, 0))] + ride_specs,
        out_shape=[jax.ShapeDtypeStruct((S, H * DK), F32),
                   jax.ShapeDtypeStruct((H, NC, DK, DK), F32),
                   jax.ShapeDtypeStruct((H, NC, C, C), F32)] + ride_out,
        scratch_shapes=[pltpu.VMEM((H, DK, DK), F32), pltpu.VMEM((NCB, H, C, DK), F32),
                        pltpu.VMEM((NCB, H, C, DK), BF16), pltpu.VMEM((NCB, H, C, DK), BF16),
                        pltpu.VMEM((NCB, H, C, DK), BF16), pltpu.VMEM((NCB, H, C, C), BF16)] + ride_scratch,
        compiler_params=_params("arbitrary"),
    )(qkv, ab, a_log, dt_bias, *ride_args)


_CHIP_PEERS = N_DEV // 2 - 1


def _chip_copies(src_refs, dst_refs, send_sems, recv_sems, local_sems, gather=False):
    x, y, c = lax.axis_index("x"), lax.axis_index("y"), lax.axis_index("c")
    here = 2 * x + y
    copies = []
    for a, (src_ref, dst_ref) in enumerate(zip(src_refs, dst_refs)):
        landing = dst_ref.at[here, c] if gather else dst_ref.at[here]
        copies.append(pltpu.make_async_copy(src_ref if gather else src_ref.at[here], landing, local_sems.at[a]))
        for rel in range(1, N_DEV // 2):
            px = 1 - x if rel & 2 else x
            py = 1 - y if rel & 1 else y
            k = a * _CHIP_PEERS + rel - 1
            copies.append(pltpu.make_async_remote_copy(
                src_ref=src_ref if gather else src_ref.at[2 * px + py], dst_ref=landing,
                send_sem=send_sems.at[k], recv_sem=recv_sems.at[k],
                device_id=(px, py, c), device_id_type=pl.DeviceIdType.MESH))
    return copies


def _chip_sems(n):
    return [pltpu.SemaphoreType.DMA((n * _CHIP_PEERS,)), pltpu.SemaphoreType.DMA((n * _CHIP_PEERS,)),
            pltpu.SemaphoreType.DMA((n,))]


def _riding(riding, gather):
    if riding is None:
        return [], [], [], []
    shapes = [jax.ShapeDtypeStruct(((N_DEV // 2, 2) + r.shape) if gather else r.shape, r.dtype) for r in riding]
    return list(riding), [pl.BlockSpec(memory_space=pl.ANY)] * len(riding), shapes, _chip_sems(len(riding))


def _ride(first, last, srcs, dsts, sems, gather):
    @pl.when(first)
    def _():
        for cp in _chip_copies(srcs, dsts, *sems, gather=gather):
            cp.start()

    def finish():
        @pl.when(last)
        def _():
            for cp in _chip_copies(srcs, dsts, *sems, gather=gather):
                cp.wait()

    return finish


def _gdn_chunk_bwd(qkv, ab, a_log, dt_bias, states, tinvs, do, *, name, riding=None):
    S = qkv.shape[0]
    H, C, DK = GDN_HEADS, GDN_CHUNK, GDN_DK
    RB = min(_GDN_ROWS, S)
    NCB, NB, NC = RB // C, S // RB, S // C
    heads = range(H)

    def body(qkv_ref, ab_ref, alog_ref, dtb_ref, st_ref, t_ref, do_ref, *rest):
        n_ride = 0 if riding is None else len(riding)
        ride_srcs, rest = rest[:n_ride], rest[n_ride:]
        (dqkv_ref, dab_ref, dalog_ref, ddtb_ref), rest = rest[:4], rest[4:]
        ride_dsts, rest = rest[:n_ride], rest[n_ride:]
        dstate, w_s, vn_s, qe_s, kf_s, att_s, dvn_s, dkf_s, *ride_sems = rest
        nb = pl.program_id(0)
        if riding is not None:
            finish_ride = _ride(nb == 0, nb == NB - 1, ride_srcs, ride_dsts, ride_sems, False)

        @pl.when(nb == 0)
        def _():
            dstate[...] = jnp.zeros_like(dstate)
            dalog_ref[...] = jnp.zeros_like(dalog_ref)
            ddtb_ref[...] = jnp.zeros_like(ddtb_ref)

        ri, ci = _chunk_iotas()
        neg_a = [-jnp.exp(alog_ref[h]) for h in heads]

        def local(c):
            rows = pl.ds(c * C, C)
            a_pre = [ab_ref[h, c] + dtb_ref[h] for h in heads]
            g_row = [neg_a[h] * _softplus(a_pre[h]) for h in heads]
            beta_row = [_sigmoid(ab_ref[H + h, c]) for h in heads]
            L = _gdn_local_batch([qkv_ref[rows, h * _HM:(h + 1) * _HM] for h in heads], g_row, beta_row, ri, ci)
            return L, a_pre, g_row, beta_row

        e_last = [None] * NCB
        for c in range(NCB):
            L, _, _, _ = local(c)
            kbe = [L["kb"][h] * L["e_col"][h] for h in heads]
            u = [_dot(t_ref[h, c], L["vb"][h], exact=True) for h in heads]
            w = [_dot(t_ref[h, c], kbe[h], exact=True) for h in heads]
            vn = [u[h] - _dot(w[h], st_ref[h, c]) for h in heads]
            for h in heads:
                w_s[c, h] = w[h].astype(BF16)
                vn_s[c, h] = vn[h].astype(BF16)
                qe_s[c, h] = (L["q"][h] * L["e_col"][h]).astype(BF16)
                kf_s[c, h] = (L["k"][h] * L["f_col"][h]).astype(BF16)
                att_s[c, h] = L["att"][h].astype(BF16)
            e_last[c] = L["e_last"]

        dst = [dstate[h] for h in heads]
        de_last = [None] * NCB
        for c in reversed(range(NCB)):
            rows = pl.ds(c * C, C)
            dob = [do_ref[rows, h * DK:(h + 1) * DK].astype(BF16) for h in heads]
            dstb = [dst[h].astype(BF16) for h in heads]
            dvn = [_dot(att_s[c, h], dob[h], "tn") + _dot(kf_s[c, h], dstb[h]) for h in heads]
            dkf = [_dot(vn_s[c, h], dstb[h], "nt") for h in heads]
            de_last[c] = [jnp.sum(jnp.sum(dst[h] * st_ref[h, c], axis=1, keepdims=True), axis=0, keepdims=True)
                          for h in heads]
            new = [dst[h] * e_last[c][h] + _dot(qe_s[c, h], dob[h], "tn")
                   - _dot(w_s[c, h], dvn[h].astype(BF16), "tn") for h in heads]
            for h in heads:
                dvn_s[c, h] = dvn[h]
                dkf_s[c, h] = dkf[h]
            dst = new
        for h in heads:
            dstate[h] = dst[h]

        for c in range(NCB):
            rows = pl.ds(c * C, C)
            L, a_pre, g_row, beta_row = local(c)
            q, k, v, kb, vb = L["q"], L["k"], L["v"], L["kb"], L["vb"]
            e_col, f_col, decay, beta_col = L["e_col"], L["f_col"], L["decay"], L["beta_col"]
            eye, strict, tril = L["eye"], L["strict"], L["tril"]
            tinv = [t_ref[h, c] for h in heads]
            stb = [st_ref[h, c].astype(BF16) for h in heads]
            dov = [do_ref[rows, h * DK:(h + 1) * DK] for h in heads]
            dvn = [dvn_s[c, h] for h in heads]
            dkf = [dkf_s[c, h] for h in heads]
            kbe = [kb[h] * e_col[h] for h in heads]
            datt = [jnp.where(tril, _dot(dov[h], vn_s[c, h], "nt"), 0.0) for h in heads]
            dqe = [_dot(dov[h], stb[h], "nt") for h in heads]
            dw = [-_dot(dvn[h], stb[h], "nt") for h in heads]
            dt = [_dot(dvn[h], vb[h], "nt") + _dot(dw[h], kbe[h], "nt") for h in heads]
            dvb = [_dot(tinv[h], dvn[h], "tn", exact=True) for h in heads]
            dkbe = [_dot(tinv[h], dw[h], "tn", exact=True) for h in heads]
            tdt = [_dot(tinv[h], dt[h], "tn", exact=True) for h in heads]
            dlow = [-jnp.where(strict, _dot(tdt[h], tinv[h], "nt", exact=True), 0.0) for h in heads]
            dkk = [dlow[h] * decay[h] for h in heads]
            dqk = [datt[h] * decay[h] for h in heads]
            dkb = [_dot(dkk[h], k[h]) + dkbe[h] * e_col[h] for h in heads]
            dk = [_dot(dkk[h], kb[h], "tn") + _dot(dqk[h], q[h], "tn") + dkf[h] * f_col[h] + dkb[h] * beta_col[h]
                  for h in heads]
            dq = [_dot(dqk[h], k[h]) + dqe[h] * e_col[h] for h in heads]
            for h in heads:
                dqkv_ref[rows, h * _HM:h * _HM + DK] = dq[h]
                dqkv_ref[rows, h * _HM + DK:h * _HM + 2 * DK] = dk[h]
                dqkv_ref[rows, h * _HM + 2 * DK:(h + 1) * _HM] = dvb[h] * beta_col[h]

            dbeta_col = [jnp.sum(k[h] * dkb[h] + v[h] * dvb[h], axis=1, keepdims=True) for h in heads]
            pmat = [dlow[h] * L["low"][h] + datt[h] * L["att"][h] for h in heads]
            df_col = [jnp.sum(k[h] * dkf[h], axis=1, keepdims=True) * f_col[h] for h in heads]
            dgc_col = [jnp.sum(pmat[h], axis=1, keepdims=True)
                       + jnp.sum(q[h] * dqe[h] + kb[h] * dkbe[h], axis=1, keepdims=True) * e_col[h] - df_col[h]
                       for h in heads]
            dgc_row = [_to_row(dgc_col[h], eye) - jnp.sum(pmat[h], axis=0, keepdims=True) for h in heads]
            dg_last = [jnp.sum(df_col[h], axis=0, keepdims=True) + de_last[c][h] * L["e_last"][h] for h in heads]
            dgc_c = [_to_col(dgc_row[h], eye) for h in heads]
            dg_row = [jnp.sum(jnp.where(ri >= ci, dgc_c[h], 0.0), axis=0, keepdims=True) + dg_last[h] for h in heads]
            dbeta_row = [_to_row(dbeta_col[h], eye) for h in heads]
            for h in heads:
                da_row = dg_row[h] * neg_a[h] * _sigmoid(a_pre[h])
                dab_ref[h, c] = da_row
                dab_ref[H + h, c] = dbeta_row[h] * beta_row[h] * (1.0 - beta_row[h])
                dalog_ref[h] += jnp.sum(dg_row[h] * g_row[h], axis=1, keepdims=True)
                ddtb_ref[h] += jnp.sum(da_row, axis=1, keepdims=True)

        if riding is not None:
            finish_ride()

    rev = lambda n: NB - 1 - n
    vec = pl.BlockSpec((H, 1, 1), lambda n: (0, 0, 0))
    gates = pl.BlockSpec((2 * H, NCB, 1, C), lambda n: (0, rev(n), 0, 0))
    wide = pl.BlockSpec((RB, H * _HM), lambda n: (rev(n), 0))
    item = lambda dt: pltpu.VMEM((NCB, H, C, DK), dt)
    ride_args, ride_specs, ride_out, ride_scratch = _riding(riding, False)
    return pl.pallas_call(
        body, name=name, grid=(NB,),
        in_specs=[wide, gates, vec, vec,
                  pl.BlockSpec((H, NCB, DK, DK), lambda n: (0, rev(n), 0, 0)),
                  pl.BlockSpec((H, NCB, C, C), lambda n: (0, rev(n), 0, 0)),
                  pl.BlockSpec((RB, H * DK), lambda n: (rev(n), 0))] + ride_specs,
        out_specs=[wide, gates, vec, vec] + ride_specs,
        out_shape=[jax.ShapeDtypeStruct((S, H * _HM), F32),
                   jax.ShapeDtypeStruct((2 * H, NC, 1, C), F32),
                   jax.ShapeDtypeStruct((H, 1, 1), F32),
                   jax.ShapeDtypeStruct((H, 1, 1), F32)] + ride_out,
        scratch_shapes=[pltpu.VMEM((H, DK, DK), F32), item(BF16), item(BF16), item(BF16), item(BF16),
                        pltpu.VMEM((NCB, H, C, C), BF16), item(F32), item(F32)] + ride_scratch,
        compiler_params=_params("arbitrary"),
    )(qkv, ab, a_log, dt_bias, states, tinvs, do, *ride_args)


def _gdn_outnorm_fwd(o, z, gain, *, name):
    S, HV = o.shape
    RB = min(512, S)

    def body(o_ref, z_ref, g_ref, y_ref):
        for h in range(HV // GDN_DK):
            cols = slice(h * GDN_DK, (h + 1) * GDN_DK)
            ov = o_ref[:, cols]
            r = lax.rsqrt(jnp.mean(ov * ov, axis=-1, keepdims=True) + RMS_EPS)
            y_ref[:, cols] = (ov * r * g_ref[...] * _silu(z_ref[:, cols].astype(F32))).astype(BF16)

    blk = pl.BlockSpec((RB, HV), lambda i: (i, 0))
    return pl.pallas_call(
        body, name=name, grid=(S // RB,),
        in_specs=[blk, blk, pl.BlockSpec((1, GDN_DK), lambda i: (0, 0))], out_specs=blk,
        out_shape=jax.ShapeDtypeStruct((S, HV), BF16), compiler_params=_params("parallel"),
    )(o, z, gain)


def _gdn_outnorm_bwd(dy, o, z, gain, *, name):
    S, HV = o.shape
    RB = min(512, S)

    def body(dy_ref, o_ref, z_ref, g_ref, do_ref, dz_ref, dg_ref):
        part = None
        for h in range(HV // GDN_DK):
            cols = slice(h * GDN_DK, (h + 1) * GDN_DK)
            ov = o_ref[:, cols]
            zv = z_ref[:, cols].astype(F32)
            dyv = dy_ref[:, cols].astype(F32)
            r = lax.rsqrt(jnp.mean(ov * ov, axis=-1, keepdims=True) + RMS_EPS)
            n = ov * r
            sg = _sigmoid(zv)
            dng = dyv * (zv * sg)
            dn = dng * g_ref[...]
            do_ref[:, cols] = r * (dn - n * jnp.mean(dn * n, axis=-1, keepdims=True))
            dz_ref[:, cols] = (dyv * (n * g_ref[...]) * (sg * (1.0 + zv * (1.0 - sg)))).astype(BF16)
            p = jnp.sum(dng * n, axis=0, keepdims=True)
            part = p if part is None else part + p

        @pl.when(pl.program_id(0) == 0)
        def _():
            dg_ref[...] = part

        @pl.when(pl.program_id(0) > 0)
        def _():
            dg_ref[...] += part

    blk = pl.BlockSpec((RB, HV), lambda i: (i, 0))
    vec = pl.BlockSpec((1, GDN_DK), lambda i: (0, 0))
    return pl.pallas_call(
        body, name=name, grid=(S // RB,),
        in_specs=[blk, blk, blk, vec], out_specs=[blk, blk, vec],
        out_shape=[jax.ShapeDtypeStruct((S, HV), F32), jax.ShapeDtypeStruct((S, HV), BF16),
                   jax.ShapeDtypeStruct((1, GDN_DK), F32)],
        compiler_params=_params("arbitrary"),
    )(dy, o, z, gain)


def _head_mask():
    return lax.broadcasted_iota(jnp.int32, (DSW_BLK, LANES), 1) < DSW_DH


def _per_head_sum(t, first):
    s0 = jnp.sum(jnp.where(first, t, 0.0), axis=-1, keepdims=True)
    s1 = jnp.sum(jnp.where(first, 0.0, t), axis=-1, keepdims=True)
    return jnp.where(first, s0, s1)


def _rms2(x, gain, first):
    r = lax.rsqrt(_per_head_sum(x * x, first) * (1.0 / DSW_DH) + RMS_EPS)
    xh = x * r
    return xh, r, xh * gain


def _rms2_bwd(dy, xh, r, gain, first):
    dxh = dy * gain
    return r * (dxh - xh * (_per_head_sum(dxh * xh, first) * (1.0 / DSW_DH)))


def _split_heads(x, first):
    return [jnp.where(first, x, 0.0).astype(BF16), jnp.where(first, 0.0, x).astype(BF16)]


_HP = LANES // DSW_DH
_DSW_W = DSW_HEADS * DSW_DH
_DSW_ROWS = 1024
_DSW_BATCH = 8
_DSW_BASES = tuple(j * len(DSW_GROUPS) * _DSW_W // LANES for j in range(3))


def _dsw_geometry(S, g):
    d = DSW_GROUPS[g][1]
    slab = DSW_BLK * d
    tb = max(1, min(_DSW_ROWS, S) // slab)
    return d, slab, tb, S // (tb * slab)


def _block_rows(t, r, slab, d):
    return pl.ds(t * slab + r, DSW_BLK) if d == 1 else pl.ds(t * slab + r, DSW_BLK, stride=d)


def _dsw_attn_fwd(q, k, v, bias, q_gain, k_gain, prev_out, *, g, name, bases=(0, 0, 0)):
    S, WT = q.shape[0], _DSW_W * len(DSW_GROUPS)
    B = DSW_BLK
    d, slab, tb, n_tiles = _dsw_geometry(S, g)
    rt = tb * slab
    cb = g * (_DSW_W // LANES)
    batch_res = max(1, _DSW_BATCH // tb)

    def body(q_ref, kp_ref, kc_ref, vp_ref, vc_ref, bias_ref, qg_ref, kg_ref, *rest):
        o_ref, lse_ref = rest[-2:]
        i = pl.program_id(1)
        qg, kg = qg_ref[...] * DSW_DH ** -0.5, kg_ref[...]
        col = lax.broadcasted_iota(jnp.int32, (B, 2 * B), 1)
        first = _head_mask()
        heads = range(_HP)
        for r0 in range(0, d, batch_res):
            res = range(r0, min(d, r0 + batch_res))
            k_raw = {(r, -1): kp_ref[_block_rows(0, r, slab, d), :] for r in res}
            v_raw = {(r, -1): vp_ref[_block_rows(0, r, slab, d), :] for r in res}
            q_raw = {}
            for r in res:
                for t in range(tb):
                    rows = _block_rows(t, r, slab, d)
                    q_raw[r, t], k_raw[r, t], v_raw[r, t] = q_ref[rows, :], kc_ref[rows, :], vc_ref[rows, :]
            kn = {key: _rms2(x, kg, first)[2].astype(BF16) for key, x in k_raw.items()}
            vb = {key: x.astype(BF16) for key, x in v_raw.items()}
            qn = {key: _split_heads(_rms2(x, qg, first)[2], first) for key, x in q_raw.items()}
            items = [(r, t, h) for r in res for t in range(tb) for h in heads]
            s = {}
            for r, t, h in items:
                sv = _dot(qn[r, t][h], jnp.concatenate([kn[r, t - 1], kn[r, t]], axis=0), "nt") + bias_ref[h]
                s[r, t, h] = jnp.where((i == 0) & (col < B), NEG_BIG, sv) if t == 0 else sv
            m = {it: jnp.max(s[it], axis=-1, keepdims=True) for it in items}
            p = {it: jnp.exp(s[it] - m[it]) for it in items}
            l = {it: jnp.sum(p[it], axis=-1, keepdims=True) for it in items}
            o = {(r, t, h): _dot(p[r, t, h], jnp.concatenate([vb[r, t - 1], vb[r, t]], axis=0)) for r, t, h in items}
            for r in res:
                for t in range(tb):
                    rows = _block_rows(t, r, slab, d)
                    o_ref[rows, :] = jnp.where(first, o[r, t, 0] / l[r, t, 0], o[r, t, 1] / l[r, t, 1])
                    lse_ref[rows, :] = jnp.where(first, m[r, t, 0] + jnp.log(l[r, t, 0]),
                                                 m[r, t, 1] + jnp.log(l[r, t, 1]))

    def cur_at(base):
        return pl.BlockSpec((rt, LANES), lambda hp, i: (i, base + cb + hp))

    def prev_at(base):
        return pl.BlockSpec((slab, LANES), lambda hp, i: (jnp.maximum(i * tb - 1, 0), base + cb + hp))

    bq, bk, bv = bases
    cur = cur_at(0)
    vec = pl.BlockSpec((1, LANES), lambda hp, i: (0, 0))
    shp = jax.ShapeDtypeStruct((S, WT), F32)
    carried = [] if prev_out is None else list(prev_out)
    n_in = 8
    return pl.pallas_call(
        body, name=name, grid=(_DSW_W // LANES, n_tiles),
        in_specs=[cur_at(bq), prev_at(bk), cur_at(bk), prev_at(bv), cur_at(bv),
                  pl.BlockSpec((_HP, B, 2 * B), lambda hp, i: (hp, 0, 0)), vec, vec]
                 + [pl.BlockSpec(memory_space=pl.ANY)] * len(carried),
        out_specs=[cur, cur], out_shape=[shp, shp],
        input_output_aliases={n_in + j: j for j in range(len(carried))},
        compiler_params=_params("parallel", "parallel"),
    )(q, k, k, v, v, bias, jnp.tile(q_gain, (1, _HP)), jnp.tile(k_gain, (1, _HP)), *carried)


def _dsw_merge(o_g, lse_g, *, name):
    S = o_g.shape[0]
    W, G = _DSW_W, len(DSW_GROUPS)
    tr = min(512, S)

    def body(o_ref, l_ref, out_ref, lse_ref):
        ls = [l_ref[:, g * W:(g + 1) * W] for g in range(G)]
        m = ls[0]
        for g in range(1, G):
            m = jnp.maximum(m, ls[g])
        den = jnp.zeros_like(m)
        acc = jnp.zeros_like(m)
        for g in range(G):
            wg = jnp.exp(ls[g] - m)
            den = den + wg
            acc = acc + wg * o_ref[:, g * W:(g + 1) * W]
        out_ref[...] = acc / den
        lse_ref[...] = m + jnp.log(den)

    wide = pl.BlockSpec((tr, G * W), lambda i: (i, 0))
    blk = pl.BlockSpec((tr, W), lambda i: (i, 0))
    shp = jax.ShapeDtypeStruct((S, W), F32)
    return pl.pallas_call(
        body, name=name, grid=(S // tr,), in_specs=[wide, wide], out_specs=[blk, blk],
        out_shape=[shp, shp], compiler_params=_params("parallel"),
    )(o_g, lse_g)


def _dsw_attn_bwd(q, k, v, o, lse, do, bias, q_gain, k_gain, prev_out, *, g, name, bases=(0, 0, 0)):
    S, WT = q.shape[0], _DSW_W * len(DSW_GROUPS)
    bq, bk, bv = bases
    B = DSW_BLK
    d, slab, tb, n_tiles = _dsw_geometry(S, g)
    rt = tb * slab
    cb = g * (_DSW_W // LANES)
    n_slabs = S // slab
    scale = DSW_DH ** -0.5
    batch_res = max(1, _DSW_BATCH // tb)

    def body(q_ref, qx_ref, kp_ref, kc_ref, vp_ref, vc_ref, o_ref, ox_ref, l_ref, lx_ref, do_ref, dox_ref,
             bias_ref, qg_ref, kg_ref, *rest):
        dq_ref, dk_ref, dv_ref, db_ref, dqg_ref, dkg_ref = rest[-6:]
        hp, i = pl.program_id(0), pl.program_id(1)
        qg, kg = qg_ref[...] * scale, kg_ref[...]
        col = lax.broadcasted_iota(jnp.int32, (B, 2 * B), 1)
        has_next = i < n_tiles - 1

        @pl.when(i == 0)
        def _():
            db_ref[...] = jnp.zeros_like(db_ref)

        dqg_acc = jnp.zeros((1, LANES), F32)
        dkg_acc = jnp.zeros((1, LANES), F32)
        first = _head_mask()
        heads = range(_HP)
        for r0 in range(0, d, batch_res):
            res = range(r0, min(d, r0 + batch_res))
            q_raw, k_raw, v_raw, o_raw, l_raw, do_raw = {}, {}, {}, {}, {}, {}
            for r in res:
                first_rows = _block_rows(0, r, slab, d)
                k_raw[r, -1], v_raw[r, -1] = kp_ref[first_rows, :], vp_ref[first_rows, :]
                for t in range(tb):
                    rows = _block_rows(t, r, slab, d)
                    q_raw[r, t], o_raw[r, t], l_raw[r, t], do_raw[r, t] = (
                        q_ref[rows, :], o_ref[rows, :], l_ref[rows, :], do_ref[rows, :])
                    k_raw[r, t], v_raw[r, t] = kc_ref[rows, :], vc_ref[rows, :]
                q_raw[r, tb], o_raw[r, tb], l_raw[r, tb], do_raw[r, tb] = (
                    qx_ref[first_rows, :], ox_ref[first_rows, :], lx_ref[first_rows, :], dox_ref[first_rows, :])
            kk = {key: _rms2(x, kg, first) for key, x in k_raw.items()}
            qq = {key: _rms2(x, qg, first) for key, x in q_raw.items()}
            knb = {key: kk[key][2].astype(BF16) for key in kk}
            qnb = {key: _split_heads(qq[key][2], first) for key in qq}
            vb = {key: x.astype(BF16) for key, x in v_raw.items()}
            dob = {key: _split_heads(x, first) for key, x in do_raw.items()}
            delta = {key: _per_head_sum(do_raw[key] * o_raw[key], first) for key in q_raw}
            pick = lambda x, h: x[:, h * DSW_DH:h * DSW_DH + 1]
            full = [(r, t, h) for r in res for t in range(tb) for h in heads]
            half = [(r, tb, h) for r in res for h in heads]
            s = {}
            for r, t, h in full:
                sv = _dot(qnb[r, t][h], jnp.concatenate([knb[r, t - 1], knb[r, t]], axis=0), "nt") + bias_ref[h]
                s[r, t, h] = jnp.where((i == 0) & (col < B), NEG_BIG, sv) if t == 0 else sv
            for r, t, h in half:
                s[r, t, h] = _dot(qnb[r, t][h], knb[r, t - 1], "nt") + bias_ref[h, :, 0:B]
            p = {(r, t, h): jnp.exp(s[r, t, h] - pick(l_raw[r, t], h)) for r, t, h in full}
            for r, t, h in half:
                p[r, t, h] = jnp.where(has_next, jnp.exp(s[r, t, h] - pick(l_raw[r, t], h)), 0.0)
            dp = {(r, t, h): _dot(dob[r, t][h], jnp.concatenate([vb[r, t - 1], vb[r, t]], axis=0), "nt")
                  for r, t, h in full}
            for r, t, h in half:
                dp[r, t, h] = _dot(dob[r, t][h], vb[r, t - 1], "nt")
            ds = {(r, t, h): p[r, t, h] * (dp[r, t, h] - pick(delta[r, t], h)) for r, t, h in full + half}
            pb = {it: p[it].astype(BF16) for it in ds}
            dsb = {it: ds[it].astype(BF16) for it in ds}
            for h in heads:
                tot = None
                for r in res:
                    for t in range(tb):
                        tot = ds[r, t, h] if tot is None else tot + ds[r, t, h]
                db_ref[h] += tot
            blocks = [(r, t) for r in res for t in range(tb)]
            keys2 = {(r, t): jnp.concatenate([knb[r, t - 1], knb[r, t]], axis=0) for r, t in blocks}
            dqn = {(r, t): jnp.where(first, _dot(dsb[r, t, 0], keys2[r, t]), _dot(dsb[r, t, 1], keys2[r, t]))
                   for r, t in blocks}
            prev_half = lambda x, r, t, h: x[r, t, h][:, :B] if t < tb else x[r, t, h]
            dkn = {(r, t): sum(_dot(dsb[r, t, h][:, B:], qnb[r, t][h], "tn")
                               + _dot(prev_half(dsb, r, t + 1, h), qnb[r, t + 1][h], "tn") for h in heads)
                   for r, t in blocks}
            dvv = {(r, t): sum(_dot(pb[r, t, h][:, B:], dob[r, t][h], "tn")
                               + _dot(prev_half(pb, r, t + 1, h), dob[r, t + 1][h], "tn") for h in heads)
                   for r, t in blocks}
            for r, t in blocks:
                dqg_acc = dqg_acc + jnp.sum(dqn[r, t] * qq[r, t][0], axis=0, keepdims=True)
                dkg_acc = dkg_acc + jnp.sum(dkn[r, t] * kk[r, t][0], axis=0, keepdims=True)
            for r, t in blocks:
                rows = _block_rows(t, r, slab, d)
                dq_ref[rows, :] = _rms2_bwd(dqn[r, t], qq[r, t][0], qq[r, t][1], qg, first)
                dk_ref[rows, :] = _rms2_bwd(dkn[r, t], kk[r, t][0], kk[r, t][1], kg, first)
                dv_ref[rows, :] = dvv[r, t]

        start = (hp == 0) & (i == 0)
        fold = lambda a: a[:, :DSW_DH] + a[:, DSW_DH:]

        @pl.when(start)
        def _():
            dqg_ref[...] = fold(dqg_acc) * scale
            dkg_ref[...] = fold(dkg_acc)

        @pl.when(jnp.logical_not(start))
        def _():
            dqg_ref[...] += fold(dqg_acc) * scale
            dkg_ref[...] += fold(dkg_acc)

    def spec(rows, pick, base):
        return pl.BlockSpec((rows, LANES), lambda hp, i: (pick(i), base + hp))

    same = lambda i: i
    before = lambda i: jnp.maximum(i * tb - 1, 0)
    after = lambda i: jnp.minimum((i + 1) * tb, n_slabs - 1)
    cur, cur1 = spec(rt, same, cb), spec(rt, same, 0)
    vec = pl.BlockSpec((1, DSW_DH), lambda hp, i: (0, 0))
    vec2 = pl.BlockSpec((1, LANES), lambda hp, i: (0, 0))
    bspec = pl.BlockSpec((_HP, B, 2 * B), lambda hp, i: (hp, 0, 0))
    shp = jax.ShapeDtypeStruct((S, WT), F32)
    vshp = jax.ShapeDtypeStruct((1, DSW_DH), F32)
    carried = [] if prev_out is None else list(prev_out)
    n_in = 15
    return pl.pallas_call(
        body, name=name, grid=(_DSW_W // LANES, n_tiles),
        in_specs=[spec(rt, same, bq + cb), spec(slab, after, bq + cb), spec(slab, before, bk + cb),
                  spec(rt, same, bk + cb), spec(slab, before, bv + cb), spec(rt, same, bv + cb),
                  cur1, spec(slab, after, 0), cur1, spec(slab, after, 0), cur1, spec(slab, after, 0),
                  bspec, vec2, vec2] + [pl.BlockSpec(memory_space=pl.ANY)] * len(carried),
        out_specs=[cur, cur, cur, bspec, vec, vec],
        out_shape=[shp, shp, shp, jax.ShapeDtypeStruct(bias.shape, F32), vshp, vshp],
        input_output_aliases={n_in + j: j for j in range(len(carried))},
        compiler_params=_params("arbitrary", "arbitrary"),
    )(q, q, k, k, v, v, o, o, lse, lse, do, do, bias, jnp.tile(q_gain, (1, _HP)), jnp.tile(k_gain, (1, _HP)),
      *carried)


def _t5_bucket(dist):
    max_exact = REL_BUCKETS // 2
    scaled = jnp.log(jnp.maximum(dist, 1).astype(F32) / max_exact) / math.log(REL_MAX_DIST / max_exact)
    large = jnp.minimum(max_exact + (scaled * (REL_BUCKETS - max_exact)).astype(jnp.int32), REL_BUCKETS - 1)
    return jnp.where(dist < max_exact, dist, large)


def _dsw_band():
    dist = (jnp.arange(DSW_BLK)[:, None] + DSW_BLK) - jnp.arange(2 * DSW_BLK)[None, :]
    return dist, (dist >= 0) & (dist <= DSW_BLK)


def _dsw_bias(rel_bias):
    dist, band = _dsw_band()
    out = []
    for g, (_, d) in enumerate(DSW_GROUPS):
        hot = jax.nn.one_hot(_t5_bucket(jnp.maximum(dist, 0) * d), REL_BUCKETS, dtype=F32)
        tab = jnp.einsum("qkb,bh->hqk", hot, rel_bias[:, g * DSW_HEADS:(g + 1) * DSW_HEADS],
                         precision=lax.Precision.HIGHEST)
        out.append(jnp.where(band[None], tab, NEG_BIG))
    return jnp.stack(out)


def _dsw_bucket_onehot():
    dist, band = _dsw_band()
    out = []
    for _, d in DSW_GROUPS:
        hot = jax.nn.one_hot(_t5_bucket(jnp.maximum(dist, 0) * d), LANES, dtype=BF16)
        out.append(jnp.where(band[..., None], hot, 0).reshape(-1, LANES))
    return jnp.stack(out)


def _exchange(send, *, gather, name):
    R, C = send.shape[-2:]

    def body(src_ref, dst_ref, send_sems, recv_sems, local_sem):
        x, y, c = lax.axis_index("x"), lax.axis_index("y"), lax.axis_index("c")
        me = 4 * x + 2 * y + c
        mine = pltpu.make_async_copy(src_ref if gather else src_ref.at[me], dst_ref.at[me], local_sem)
        mine.start()
        copies = []
        for rel in range(1, N_DEV):
            px = 1 - x if rel & 4 else x
            py = 1 - y if rel & 2 else y
            pc = 1 - c if rel & 1 else c
            peer = 4 * px + 2 * py + pc
            cp = pltpu.make_async_remote_copy(
                src_ref=src_ref if gather else src_ref.at[peer], dst_ref=dst_ref.at[me],
                send_sem=send_sems.at[rel - 1], recv_sem=recv_sems.at[rel - 1],
                device_id=(px, py, pc), device_id_type=pl.DeviceIdType.MESH)
            cp.start()
            copies.append(cp)
        for cp in copies:
            cp.wait()
        mine.wait()

    return pl.pallas_call(
        body, name=name,
        in_specs=[pl.BlockSpec(memory_space=pl.ANY)], out_specs=pl.BlockSpec(memory_space=pl.ANY),
        out_shape=jax.ShapeDtypeStruct((N_DEV, R, C), send.dtype),
        scratch_shapes=[pltpu.SemaphoreType.DMA((N_DEV - 1,)), pltpu.SemaphoreType.DMA((N_DEV - 1,)),
                        pltpu.SemaphoreType.DMA(())],
    )(send)


def _gather_two_level(send, *, name):
    R, C = send.shape

    def body(src_ref, dst_ref, send_sems, recv_sems, local_sem):
        x, y, c = lax.axis_index("x"), lax.axis_index("y"), lax.axis_index("c")
        me, sibling = (x, y, c), (x, y, 1 - c)
        chips = [(1 - x, y), (x, 1 - y), (1 - x, 1 - y)]

        def slot(px, py, pc):
            return dst_ref.at[4 * px + 2 * py + pc]

        def copy(k, block, to, src=None):
            return pltpu.make_async_remote_copy(
                src_ref=slot(*block) if src is None else src, dst_ref=slot(*block),
                send_sem=send_sems.at[k], recv_sem=recv_sems.at[k],
                device_id=to, device_id_type=pl.DeviceIdType.MESH)

        mine = pltpu.make_async_copy(src_ref, slot(*me), local_sem)
        mine.start()
        first = [copy(0, me, sibling, src=src_ref)]
        first += [copy(1 + j, me, (*chip, c), src=src_ref) for j, chip in enumerate(chips)]
        for cp in first:
            cp.start()
        passed = [copy(4 + j, (*chip, c), sibling) for j, chip in enumerate(chips)]
        for j, chip in enumerate(chips):
            copy(1 + j, (*chip, c), me).wait_recv()
            passed[j].start()
        copy(0, sibling, me).wait_recv()
        for j, chip in enumerate(chips):
            copy(4 + j, (*chip, 1 - c), me).wait_recv()
        for cp in first + passed:
            cp.wait_send()
        mine.wait()

    return pl.pallas_call(
        body, name=name,
        in_specs=[pl.BlockSpec(memory_space=pl.ANY)], out_specs=pl.BlockSpec(memory_space=pl.ANY),
        out_shape=jax.ShapeDtypeStruct((N_DEV, R, C), send.dtype),
        scratch_shapes=[pltpu.SemaphoreType.DMA((N_DEV - 1,)), pltpu.SemaphoreType.DMA((N_DEV - 1,)),
                        pltpu.SemaphoreType.DMA(())],
    )(send)


_ANY = pl.BlockSpec(memory_space=pl.ANY)


def _swap_with_sibling(sends, *, name):
    n = len(sends)

    def body(*refs):
        x, y, c = lax.axis_index("x"), lax.axis_index("y"), lax.axis_index("c")
        send_sems, recv_sems = refs[2 * n:]
        copies = [pltpu.make_async_remote_copy(
            src_ref=refs[a], dst_ref=refs[n + a], send_sem=send_sems.at[a], recv_sem=recv_sems.at[a],
            device_id=(x, y, 1 - c), device_id_type=pl.DeviceIdType.MESH) for a in range(n)]
        for cp in copies:
            cp.start()
        for cp in copies:
            cp.wait()

    return pl.pallas_call(
        body, name=name, in_specs=[_ANY] * n, out_specs=[_ANY] * n,
        out_shape=[jax.ShapeDtypeStruct(s.shape, s.dtype) for s in sends],
        scratch_shapes=[pltpu.SemaphoreType.DMA((n,)), pltpu.SemaphoreType.DMA((n,))],
    )(*sends)


def _fill_from_sibling(bufs, *, name):
    n, n_chips = len(bufs), bufs[0].shape[0]

    def body(*refs):
        x, y, c = lax.axis_index("x"), lax.axis_index("y"), lax.axis_index("c")
        send_sems, recv_sems = refs[2 * n:]
        copies = [pltpu.make_async_remote_copy(
            src_ref=refs[a].at[q, c], dst_ref=refs[n + a].at[q, c],
            send_sem=send_sems.at[a * n_chips + q], recv_sem=recv_sems.at[a * n_chips + q],
            device_id=(x, y, 1 - c), device_id_type=pl.DeviceIdType.MESH) for a in range(n) for q in range(n_chips)]
        for cp in copies:
            cp.start()
        for cp in copies:
            cp.wait()

    return pl.pallas_call(
        body, name=name, in_specs=[_ANY] * n, out_specs=[_ANY] * n,
        out_shape=[jax.ShapeDtypeStruct(b.shape, b.dtype) for b in bufs],
        input_output_aliases={a: a for a in range(n)},
        scratch_shapes=[pltpu.SemaphoreType.DMA((n * n_chips,)), pltpu.SemaphoreType.DMA((n * n_chips,))],
    )(*bufs)


def _exchange_chips(send, *, name):
    def body(src_ref, dst_ref, *sems):
        copies = _chip_copies([src_ref], [dst_ref], *sems)
        for cp in copies:
            cp.start()
        for cp in copies:
            cp.wait()

    return pl.pallas_call(
        body, name=name, in_specs=[_ANY], out_specs=_ANY,
        out_shape=jax.ShapeDtypeStruct(send.shape, send.dtype), scratch_shapes=_chip_sems(1),
    )(send)


def _add_pair(a, b, *, name):
    lead, (R, C) = a.shape[:-2], a.shape[-2:]
    tr = _tile(R, max(8, 1024 * LANES // C))

    def body(a_ref, b_ref, o_ref):
        o_ref[...] = (a_ref[...].astype(F32) + b_ref[...].astype(F32)).astype(o_ref.dtype)

    blk = pl.BlockSpec((None,) * len(lead) + (tr, C), lambda *idx: idx + (0,))
    return pl.pallas_call(
        body, name=name, grid=lead + (R // tr,), in_specs=[blk, blk], out_specs=blk,
        out_shape=jax.ShapeDtypeStruct(a.shape, a.dtype),
        compiler_params=_params(*(("parallel",) * (len(lead) + 1))),
    )(a, b)


_BIG = ("w_ffn_in", "w_ffn_out", "gdn_w_in", "gdn_conv", "gdn_w_out", "dsw_w_in", "dsw_w_out")
_LATE = ("gdn_w_in", "gdn_conv", "gdn_w_out")
_EARLY = tuple(n for n in _BIG if n not in _LATE)
_NATIVE = ("w_ffn_in", "w_ffn_out", "dsw_w_in")
_SHARD_AXIS = {"w_ffn_in": 2, "w_ffn_out": 1, "gdn_w_in": 2, "gdn_conv": 2, "gdn_w_out": 1, "dsw_w_in": 2,
               "dsw_w_out": 2}
_SMALL = ("b_ada", "norm_mix", "norm_ffn", "gdn_a_log", "gdn_dt_bias", "gdn_out_norm", "dsw_q_norm",
          "dsw_k_norm", "rel_bias")
_ROW_ALIGN = 16
_BIG_ALIGN = 1024


def _ceil_to(n, m):
    return -(-n // m) * m


def _seg_rows(shape):
    return _ceil_to(_ceil_to(int(np.prod(shape)), LANES) // LANES, _ROW_ALIGN)


def _pack(arrs, total_align):
    lead = arrs[0][1]
    segs = []
    for a, nlead in arrs:
        assert nlead == lead
        bshape = a.shape[:nlead]
        n = int(np.prod(a.shape[nlead:]))
        rows = _seg_rows(a.shape[nlead:])
        flat = a.reshape(bshape + (n,))
        flat = jnp.pad(flat, [(0, 0)] * nlead + [(0, rows * LANES - n)])
        segs.append(flat.reshape(bshape + (rows, LANES)))
    buf = jnp.concatenate(segs, axis=lead)
    total = _ceil_to(buf.shape[lead], total_align)
    return jnp.pad(buf, [(0, 0)] * lead + [(0, total - buf.shape[lead]), (0, 0)])


def _unpack(buf, shapes, nlead):
    out, off = [], 0
    for shp in shapes:
        n, rows = int(np.prod(shp)), _seg_rows(shp)
        seg = lax.slice_in_dim(buf, off, off + rows, axis=nlead)
        seg = seg.reshape(buf.shape[:nlead] + (rows * LANES,))[..., :n]
        out.append(seg.reshape(buf.shape[:nlead] + tuple(shp)))
        off += rows
    return out


def _to_natural(g, axis):
    n, L, r, c = g.shape
    if axis == 2:
        return jnp.transpose(g, (1, 2, 0, 3)).reshape(L, r, n * c)
    return jnp.transpose(g, (1, 0, 2, 3)).reshape(L, n * r, c)


def _to_blocked(w, axis):
    L, R, C = w.shape
    if axis == 2:
        return jnp.transpose(w.reshape(L, R, N_DEV, C // N_DEV), (2, 0, 1, 3))
    return jnp.transpose(w.reshape(L, N_DEV, R // N_DEV, C), (1, 0, 2, 3))


def _hm(a):
    lead = a.shape[:-1]
    return jnp.swapaxes(a.reshape(lead + (3, GDN_HEADS, GDN_DK)), -3, -2).reshape(lead + (3 * GDN_HEADS * GDN_DK,))


def _un_hm(a):
    lead = a.shape[:-1]
    return jnp.swapaxes(a.reshape(lead + (GDN_HEADS, 3, GDN_DK)), -3, -2).reshape(lead + (3 * GDN_HEADS * GDN_DK,))


_TILES = (2048, 1536, 1408, 1024, 768, 512, 384, 256, 128, 64, 32, 16, 8)


def _tile(n, cap):
    for t in _TILES:
        if t <= cap and n % t == 0:
            return t
    return n


def _mm_auto(a, b, mode, name, **kw):
    if mode == "tn":
        (K, M), N = a.shape, b.shape[1]
        deep = 2048 if a.dtype == BF16 and b.dtype == BF16 else 1024
        tm, tn, tk = _tile(M, 1408), _tile(N, 1408), _tile(K, deep)
    else:
        M, K = a.shape
        N = b.shape[1] if mode == "nn" else b.shape[0]
        tm, tn, tk = _tile(M, _MM_ROWS), _tile(N, 1536), _tile(K, 1408)
    return _mm(a, b, mode=mode, name=name, tm=tm, tn=tn, tk=tk, **kw)


def _row(v):
    return v.reshape(1, -1)


def _ffn_in_act(h, w_in, *, name):
    S, D = h.shape
    F = w_in.shape[1] // 2
    tm, tn = _tile(S, _MM_ROWS), _tile(F, 1408)
    nj = F // tn

    def body(h_ref, wg_ref, wu_ref, g_ref, u_ref, a_ref):
        hv = h_ref[...]
        gate = jnp.dot(hv, wg_ref[...], preferred_element_type=F32)
        up = jnp.dot(hv, wu_ref[...], preferred_element_type=F32)
        g_ref[...] = gate.astype(BF16)
        u_ref[...] = up.astype(BF16)
        a_ref[...] = (_silu(gate) * up).astype(BF16)

    out = pl.BlockSpec((tm, tn), lambda i, j: (i, j))
    shp = jax.ShapeDtypeStruct((S, F), BF16)
    return pl.pallas_call(
        body, name=name, grid=(S // tm, nj),
        in_specs=[pl.BlockSpec((tm, D), lambda i, j: (i, 0)), pl.BlockSpec((D, tn), lambda i, j: (0, j)),
                  pl.BlockSpec((D, tn), lambda i, j: (0, j + nj))],
        out_specs=[out, out, out], out_shape=[shp, shp, shp],
        compiler_params=_params("parallel", "parallel"),
    )(h, w_in, w_in)


def _ffn_out_dx_act(dy, w_out, gate_vec, pg, pu, *, name):
    S, D = dy.shape
    F = w_out.shape[0]
    tm, tn = _tile(S, _MM_ROWS), _tile(F, 1408)

    def body(dy_ref, w_ref, gv_ref, pg_ref, pu_ref, dg_ref, du_ref):
        dyg = (dy_ref[...] * gv_ref[...]).astype(BF16)
        da = lax.dot_general(dyg, w_ref[...], _DOT_DIMS["nt"], preferred_element_type=F32)
        gate = pg_ref[...].astype(F32)
        up = pu_ref[...].astype(F32)
        sg = _sigmoid(gate)
        dg_ref[...] = (da * up * (sg * (1.0 + gate * (1.0 - sg)))).astype(BF16)
        du_ref[...] = (da * (gate * sg)).astype(BF16)

    blk = pl.BlockSpec((tm, tn), lambda i, j: (i, j))
    shp = jax.ShapeDtypeStruct((S, F), BF16)
    return pl.pallas_call(
        body, name=name, grid=(S // tm, F // tn),
        in_specs=[pl.BlockSpec((tm, D), lambda i, j: (i, 0)), pl.BlockSpec((tn, D), lambda i, j: (j, 0)),
                  pl.BlockSpec((1, D), lambda i, j: (0, 0)), blk, blk],
        out_specs=[blk, blk], out_shape=[shp, shp],
        compiler_params=_params("parallel", "parallel"),
    )(dy, w_out, gate_vec, pg, pu)


def _ffn_fwd(x, mod, gain, w_in, w_out, tag):
    sh, sc, gate = mod
    h = _norm_mod_fwd(x, gain, sc, sh, name=f"ffn_norm_{tag}")
    pg, pu, a = _ffn_in_act(h, w_in, name=f"ffn_in_{tag}")
    y = _mm_auto(a, w_out, "nn", f"ffn_out_{tag}", out_scale=gate, resid=x)
    return y, (x, h, pg, pu, a)


def _ffn_bwd(dy, saved, mod, gain, w_in, w_out, tag):
    sh, sc, gate = mod
    x, h, pg, pu, a = saved
    F = pg.shape[1]
    gmat = _mm_auto(a, dy, "tn", f"ffn_out_g_{tag}")
    dw_out, dgate = _wout_grad(gmat, w_out, gate, name=f"ffn_out_dw_{tag}")
    dpg, dpu = _ffn_out_dx_act(dy, w_out, gate, pg, pu, name=f"ffn_out_dx_{tag}")
    dw_in = jnp.concatenate([_mm_auto(h, dpg, "tn", f"ffn_in_dw_gate_{tag}", out_dtype=BF16),
                             _mm_auto(h, dpu, "tn", f"ffn_in_dw_up_{tag}", out_dtype=BF16)], axis=1)
    tk = _tile(F, 1408)
    dh = _mm_sum_nt([(dpg, w_in, tk, 0), (dpu, w_in, tk, F)], name=f"ffn_in_dx_{tag}")
    dx, dsh, dsc, dgain = _norm_mod_bwd(dh, x, dy, gain, sc, name=f"ffn_norm_bwd_{tag}")
    return dx, dict(w_in=dw_in, w_out=dw_out, gain=dgain, mod=(dsh, dsc, dgate))


def _gdn_fwd(x, mod, gain, W, riding=None):
    sh, sc, gate = mod
    S = x.shape[0]
    h = _norm_mod_fwd(x, gain, sc, sh, name="gdn_norm")
    pq = _mm_auto(h, W["gdn_qkv"], "nn", "gdn_in_qkv", out_dtype=BF16)
    z = _mm_auto(h, W["gdn_z"], "nn", "gdn_in_z", out_dtype=BF16)
    ab = _mm_auto(h, W["gdn_ab"], "nn", "gdn_in_ab")
    qkvn = _gdn_prep_fwd(pq, W["gdn_conv"], name="gdn_prep")
    ab4 = jnp.transpose(ab[:, :2 * GDN_HEADS]).reshape(2 * GDN_HEADS, S // GDN_CHUNK, 1, GDN_CHUNK)
    o, states, tinvs, *rode = _gdn_chunk_fwd(qkvn, ab4, W["gdn_a_log"], W["gdn_dt_bias"], name="gdn_chunk",
                                             riding=riding)
    o2 = _gdn_outnorm_fwd(o, z, W["gdn_out_norm"], name="gdn_outnorm")
    y = _mm_auto(o2, W["gdn_out"], "nn", "gdn_out", out_scale=gate, resid=x)
    return y, (x, h, pq, z, qkvn, ab4, o, states, tinvs, o2), (tuple(rode) if rode else None)


def _gdn_bwd(dy, saved, mod, gain, W, riding=None):
    sh, sc, gate = mod
    x, h, pq, z, qkvn, ab4, o, states, tinvs, o2 = saved
    S = x.shape[0]
    gmat = _mm_auto(o2, dy, "tn", "gdn_out_g")
    dw_out, dgate = _wout_grad(gmat, W["gdn_out"], gate, name="gdn_out_dw")
    do2 = _mm_auto(dy, W["gdn_out"], "nt", "gdn_out_dx", a_scale=gate)
    do, dz, dout_norm = _gdn_outnorm_bwd(do2, o, z, W["gdn_out_norm"], name="gdn_outnorm_bwd")
    dqkvn, dab4, da_log, ddt_bias, *rode = _gdn_chunk_bwd(
        qkvn, ab4, W["gdn_a_log"], W["gdn_dt_bias"], states, tinvs, do, name="gdn_chunk_bwd", riding=riding)
    dc, dconv8 = _gdn_prep_bwd_pre(dqkvn, pq, W["gdn_conv"], name="gdn_prep_bwd")
    dpq = _gdn_conv_bwd_x(dc, W["gdn_conv"], name="gdn_conv_bwd")
    dab = jnp.transpose(dab4.reshape(2 * GDN_HEADS, S))
    dab = jnp.pad(dab, ((0, 0), (0, LANES - 2 * GDN_HEADS))).astype(BF16)
    dw_qkv = _mm_auto(h, dpq, "tn", "gdn_in_qkv_dw", out_dtype=BF16)
    dw_z = _mm_auto(h, dz, "tn", "gdn_in_z_dw", out_dtype=BF16)
    dw_ab = _mm_auto(h, dab, "tn", "gdn_in_ab_dw", out_dtype=BF16)
    dh = _mm_sum_nt([(dpq, W["gdn_qkv"], 1024, 0), (dz, W["gdn_z"], 1024, 0), (dab, W["gdn_ab"], LANES, 0)],
                    name="gdn_in_dx")
    dx, dsh, dsc, dgain = _norm_mod_bwd(dh, x, dy, gain, sc, name="gdn_norm_bwd")
    dw_in = jnp.concatenate([_un_hm(dw_qkv), dw_z, dw_ab[:, :2 * GDN_HEADS]], axis=1)
    return dx, dict(gdn_w_in=dw_in, gdn_conv=_un_hm(dconv8[:GDN_CONV]), gdn_w_out=dw_out, gdn_out_norm=dout_norm,
                    gdn_a_log=da_log.reshape(1, GDN_HEADS), gdn_dt_bias=ddt_bias.reshape(1, GDN_HEADS),
                    gain=dgain, mod=(dsh, dsc, dgate)), (tuple(rode) if rode else None)


def _dsw_fwd(x, mod, gain, W):
    sh, sc, gate = mod
    h = _norm_mod_fwd(x, gain, sc, sh, name="dsw_norm")
    qkv = _mm_auto(h, W["dsw_in"], "nn", "dsw_in")
    outs = None
    for g in range(len(DSW_GROUPS)):
        outs = _dsw_attn_fwd(qkv, qkv, qkv, W["dsw_bias"][g], W["dsw_q_norm"], W["dsw_k_norm"], outs, g=g,
                             name=f"dsw_attn_{g}", bases=_DSW_BASES)
    o, lse = _dsw_merge(*outs, name="dsw_merge")
    y = _mm_auto(o, W["dsw_out"], "nn", "dsw_out", out_scale=gate, resid=x)
    return y, (x, h, qkv, o, lse)


def _dsw_bwd(dy, saved, mod, gain, W):
    sh, sc, gate = mod
    x, h, qkv, o, lse = saved
    gmat = _mm_auto(o, dy, "tn", "dsw_out_g")
    dw_out, dgate = _wout_grad(gmat, W["dsw_out"], gate, name="dsw_out_dw")
    do = _mm_auto(dy, W["dsw_out"], "nt", "dsw_out_dx", a_scale=gate)
    G = len(DSW_GROUPS)
    dqkv, dbias, dq_norm, dk_norm = None, [], 0.0, 0.0
    for g in range(G):
        *dqkv, db, dqg, dkg = _dsw_attn_bwd(qkv, qkv, qkv, o, lse, do, W["dsw_bias"][g], W["dsw_q_norm"],
                                            W["dsw_k_norm"], dqkv, g=g, name=f"dsw_attn_bwd_{g}",
                                            bases=_DSW_BASES)
        dbias.append(db)
        dq_norm, dk_norm = dq_norm + dqg, dk_norm + dkg
    dws = [_mm_auto(h, d, "tn", f"dsw_in_{n}_dw", out_dtype=BF16) for n, d in zip("qkv", dqkv)]
    width = dqkv[0].shape[1]
    dh = _mm_sum_nt([(d, W["dsw_in"], _tile(width, 1024), j * width) for j, d in enumerate(dqkv)],
                    name="dsw_in_dx")
    dx, dsh, dsc, dgain = _norm_mod_bwd(dh, x, dy, gain, sc, name="dsw_norm_bwd")
    hot = _dsw_bucket_onehot()
    drel = [_mm(dbias[g].reshape(DSW_HEADS, -1), hot[g], mode="nn", name=f"dsw_rel_bias_{g}", tm=DSW_HEADS,
                tn=LANES, tk=8192)[:, :REL_BUCKETS] for g in range(G)]
    return dx, dict(dsw_w_in=jnp.concatenate(dws, axis=1), dsw_w_out=dw_out, dsw_q_norm=dq_norm,
                    dsw_k_norm=dk_norm, rel_bias=jnp.transpose(jnp.concatenate(drel, axis=0)),
                    gain=dgain, mod=(dsh, dsc, dgate))


def _local_step(x, target, mod, W, late_weights=None, early_pairs=None):
    mods = [[_row(mod[l, i]) for i in range(6)] for l in range(2)]
    nmix = [_row(W["norm_mix"][l]) for l in range(2)]
    nffn = [_row(W["norm_ffn"][l]) for l in range(2)]
    x1, s_gdn, arrived = _gdn_fwd(x, mods[0][:3], nmix[0], W, None if late_weights is None else late_weights[0])
    if late_weights is not None:
        W = {**W, **late_weights[1](arrived)}
    x2, s_f0 = _ffn_fwd(x1, mods[0][3:], nffn[0], W["w_ffn_in"][0], W["w_ffn_out"][0], "0")
    x3, s_dsw = _dsw_fwd(x2, mods[1][:3], nmix[1], W)
    x4, s_f1 = _ffn_fwd(x3, mods[1][3:], nffn[1], W["w_ffn_in"][1], W["w_ffn_out"][1], "1")
    dx4, sse = _loss_head(x4, target, name="loss_head")
    dx3, g_f1 = _ffn_bwd(dx4, s_f1, mods[1][3:], nffn[1], W["w_ffn_in"][1], W["w_ffn_out"][1], "1")
    dx2, g_dsw = _dsw_bwd(dx3, s_dsw, mods[1][:3], nmix[1], W)
    dx1, g_f0 = _ffn_bwd(dx2, s_f0, mods[0][3:], nffn[0], W["w_ffn_in"][0], W["w_ffn_out"][0], "0")
    grads = dict(
        w_ffn_in=jnp.stack([g_f0["w_in"], g_f1["w_in"]]), w_ffn_out=jnp.stack([g_f0["w_out"], g_f1["w_out"]]),
        dsw_w_in=g_dsw["dsw_w_in"][None], dsw_w_out=g_dsw["dsw_w_out"][None])
    riding = None if early_pairs is None else early_pairs(grads)
    dx0, g_gdn, rode = _gdn_bwd(dx1, s_gdn, mods[0][:3], nmix[0], W, riding)
    dmod = jnp.stack([jnp.concatenate(list(g_gdn["mod"]) + list(g_f0["mod"]), axis=0),
                      jnp.concatenate(list(g_dsw["mod"]) + list(g_f1["mod"]), axis=0)])
    grads.update(
        norm_mix=jnp.concatenate([g_gdn["gain"], g_dsw["gain"]], axis=0),
        norm_ffn=jnp.concatenate([g_f0["gain"], g_f1["gain"]], axis=0),
        gdn_w_in=g_gdn["gdn_w_in"][None], gdn_conv=g_gdn["gdn_conv"][None], gdn_w_out=g_gdn["gdn_w_out"][None],
        gdn_out_norm=g_gdn["gdn_out_norm"], gdn_a_log=g_gdn["gdn_a_log"], gdn_dt_bias=g_gdn["gdn_dt_bias"],
        dsw_q_norm=g_dsw["dsw_q_norm"], dsw_k_norm=g_dsw["dsw_k_norm"], rel_bias=g_dsw["rel_bias"])
    return sse, dx0, grads, dmod, rode


def _prepare_first(full, small):
    gw = full["gdn_w_in"][0]
    hk3 = 3 * GDN_HEADS * GDN_DK
    return dict(
        gdn_qkv=_hm(gw[:, :hk3]), gdn_z=gw[:, hk3:hk3 + GDN_HEADS * GDN_DK],
        gdn_ab=jnp.pad(gw[:, hk3 + GDN_HEADS * GDN_DK:], ((0, 0), (0, LANES - 2 * GDN_HEADS))),
        gdn_conv=_hm(full["gdn_conv"][0]), gdn_out=full["gdn_w_out"][0],
        norm_mix=small["norm_mix"], norm_ffn=small["norm_ffn"],
        gdn_a_log=small["gdn_a_log"].reshape(GDN_HEADS, 1, 1), gdn_dt_bias=small["gdn_dt_bias"].reshape(GDN_HEADS, 1, 1),
        gdn_out_norm=small["gdn_out_norm"], dsw_q_norm=small["dsw_q_norm"], dsw_k_norm=small["dsw_k_norm"],
        dsw_bias=_dsw_bias(small["rel_bias"]))


def _prepare_rest(full):
    di = full["dsw_w_in"][0]
    dq = di.shape[1] // 3
    return dict(w_ffn_in=full["w_ffn_in"], w_ffn_out=full["w_ffn_out"],
                dsw_in=di, dsw_out=full["dsw_w_out"][0])


def _prepare_weights(full, small):
    return {**_prepare_first(full, small), **_prepare_rest(full)}


_W_NAMES = ("w_ada", "b_ada", "norm_mix", "norm_ffn", "w_ffn_in", "w_ffn_out", "gdn_w_in", "gdn_conv",
            "gdn_a_log", "gdn_dt_bias", "gdn_out_norm", "gdn_w_out", "dsw_w_in", "dsw_q_norm", "dsw_k_norm",
            "dsw_w_out", "rel_bias")
_PAD_BATCH = 16


def _pad_rows(a, rows):
    return jnp.pad(a, ((0, rows - a.shape[0]), (0, 0)))


def kernel(x, c, w_ada, b_ada, norm_mix, norm_ffn, w_ffn_in, w_ffn_out, gdn_w_in, gdn_conv, gdn_a_log, gdn_dt_bias, gdn_out_norm, gdn_w_out, dsw_w_in, dsw_q_norm, dsw_k_norm, dsw_w_out, rel_bias, loss_target, m_w_ada, m_b_ada, m_norm_mix, m_norm_ffn, m_w_ffn_in, m_w_ffn_out, m_gdn_w_in, m_gdn_conv, m_gdn_a_log, m_gdn_dt_bias, m_gdn_out_norm, m_gdn_w_out, m_dsw_w_in, m_dsw_q_norm, m_dsw_k_norm, m_dsw_w_out, m_rel_bias, v_w_ada, v_b_ada, v_norm_mix, v_norm_ffn, v_w_ffn_in, v_w_ffn_out, v_gdn_w_in, v_gdn_conv, v_gdn_a_log, v_gdn_dt_bias, v_gdn_out_norm, v_gdn_w_out, v_dsw_w_in, v_dsw_q_norm, v_dsw_k_norm, v_dsw_w_out, v_rel_bias):
    w = dict(zip(_W_NAMES, (w_ada, b_ada, norm_mix, norm_ffn, w_ffn_in, w_ffn_out, gdn_w_in, gdn_conv, gdn_a_log,
                            gdn_dt_bias, gdn_out_norm, gdn_w_out, dsw_w_in, dsw_q_norm, dsw_k_norm, dsw_w_out,
                            rel_bias)))
    m = dict(zip(_W_NAMES, (m_w_ada, m_b_ada, m_norm_mix, m_norm_ffn, m_w_ffn_in, m_w_ffn_out, m_gdn_w_in,
                            m_gdn_conv, m_gdn_a_log, m_gdn_dt_bias, m_gdn_out_norm, m_gdn_w_out, m_dsw_w_in,
                            m_dsw_q_norm, m_dsw_k_norm, m_dsw_w_out, m_rel_bias)))
    v = dict(zip(_W_NAMES, (v_w_ada, v_b_ada, v_norm_mix, v_norm_ffn, v_w_ffn_in, v_w_ffn_out, v_gdn_w_in,
                            v_gdn_conv, v_gdn_a_log, v_gdn_dt_bias, v_gdn_out_norm, v_gdn_w_out, v_dsw_w_in,
                            v_dsw_q_norm, v_dsw_k_norm, v_dsw_w_out, v_rel_bias)))
    D = x.shape[-1]
    n_layers, _, ada_cols = w_ada.shape

    c_all = _exchange(c.reshape(D // LANES, LANES), gather=True, name="gather_cond").reshape(N_DEV, D)
    c_pad = _pad_rows(c_all, _PAD_BATCH)
    proj = [_mm(c_pad, w_ada[l], mode="nn", name=f"ada_proj_{l}", tm=_PAD_BATCH, tn=ada_cols, tk=D, a_silu=True)
            for l in range(n_layers)]
    mod_send = _pack([(jnp.stack([p[:N_DEV] for p in proj], axis=1), 1)], _ROW_ALIGN)
    mod_recv = _exchange(mod_send, gather=False, name="scatter_mod")
    mod = _unpack(mod_recv, [(n_layers, ada_cols)], 1)[0]
    mod = jnp.transpose(mod, (1, 0, 2)).reshape(n_layers, N_DEV * ada_cols) + b_ada
    mod = mod.reshape(n_layers, 6, D)

    conv_hi = gdn_conv.astype(BF16)
    conv_lo = (gdn_conv - conv_hi.astype(F32)).astype(BF16)
    first_send = _pack([(conv_hi if n == "gdn_conv" else w[n].astype(BF16), 0) for n in _LATE] + [(conv_lo, 0)],
                       _ROW_ALIGN)
    parts = _unpack(_gather_two_level(first_send, name="gather_weights_first"),
                    [w[n].shape for n in _LATE] + [gdn_conv.shape], 1)
    full = {n: _to_natural(parts[i], _SHARD_AXIS[n]) for i, n in enumerate(_LATE)}
    full["gdn_conv"] = full["gdn_conv"].astype(F32) + _to_natural(parts[-1], _SHARD_AXIS["gdn_conv"]).astype(F32)
    W = _prepare_first(full, {n: w[n] for n in _SMALL})
    packed_early = tuple(n for n in _EARLY if n not in _NATIVE)
    rest_send = (_pack([(w[n].astype(BF16), 0) for n in packed_early], _ROW_ALIGN),
                 ) + tuple(w[n].astype(BF16) for n in _NATIVE)

    def rest_weights(arrived):
        filled = _fill_from_sibling(arrived, name="swap_weights")
        by_dev = [a.reshape((N_DEV,) + a.shape[2:]) for a in filled]
        blocks = dict(zip(packed_early, _unpack(by_dev[0], [w[n].shape for n in packed_early], 1)))
        blocks.update(zip(_NATIVE, by_dev[1:]))
        return _prepare_rest({n: _to_natural(blocks[n], _SHARD_AXIS[n]) for n in _EARLY})

    my_c = lax.axis_index("c")

    def pair_sums(g, packed, native, tag):
        sends = [_pack([(_to_blocked(g[n].astype(BF16), _SHARD_AXIS[n]), 1) for n in packed], _BIG_ALIGN)]
        sends += [_to_blocked(g[n].astype(BF16), _SHARD_AXIS[n]) for n in native]
        by_core = [s.reshape((N_DEV // 2, 2) + s.shape[1:]) for s in sends]
        keep = [lax.dynamic_index_in_dim(s, my_c, axis=1, keepdims=False) for s in by_core]
        give = [lax.dynamic_index_in_dim(s, 1 - my_c, axis=1, keepdims=False) for s in by_core]
        got = _swap_with_sibling(give, name=f"swap_grads_{tag}")
        return tuple(_add_pair(k, t, name=f"add_sibling_grads_{tag}_{j}") for j, (k, t) in enumerate(zip(keep, got)))

    sse, grad_x, grads, dmod, early_recv = _local_step(
        x[0], loss_target[0], mod, W, late_weights=(rest_send, rest_weights),
        early_pairs=lambda g: pair_sums(g, packed_early, _NATIVE, "early"))
    loss = lax.psum(0.5 * sse[0, 0] / D, ("x", "y", "c"))
    grads["b_ada"] = dmod.reshape(n_layers, 6 * D)
    late_recv = _exchange_chips(pair_sums(grads, _LATE, (), "late")[0], name="scatter_grads_late")
    g_parts = dict(zip(packed_early, _unpack(early_recv[0], [w[n].shape for n in packed_early], 1)))
    g_parts.update(zip(_NATIVE, early_recv[1:]))
    g_parts.update(zip(_LATE, _unpack(late_recv, [w[n].shape for n in _LATE], 1)))

    dmod_send = _pack([(jnp.transpose(dmod.reshape(n_layers, N_DEV, ada_cols), (1, 0, 2)), 1)], _ROW_ALIGN)
    small_send = _pack([(grads[n].reshape(w[n].shape), 0) for n in _SMALL], _ROW_ALIGN)
    s_recv = _exchange(jnp.concatenate(
        [dmod_send, jnp.broadcast_to(small_send[None], (N_DEV,) + small_send.shape)], axis=1),
        gather=False, name="scatter_small")
    dmod_rows = dmod_send.shape[1]

    out = {}
    kinds = ("grad", "delta", "new_m", "new_v")
    for n in _BIG:
        g4 = g_parts[n]
        rows2d = lambda a: a.reshape((-1, w[n].shape[-1]))
        res = _adamw(rows2d(w[n]), g4.reshape((g4.shape[0], -1, w[n].shape[-1])), rows2d(m[n]), rows2d(v[n]),
                     name=f"adamw_{n}")
        for kind, buf in zip(kinds, res):
            out[kind, n] = buf.reshape(w[n].shape)

    dmod_all = _unpack(lax.slice_in_dim(s_recv, 0, dmod_rows, axis=1), [(n_layers, ada_cols)], 1)[0]
    g_ada = jnp.stack([_mm(c_pad, _pad_rows(dmod_all[:, l], _PAD_BATCH), mode="tn", name=f"ada_dw_{l}",
                           tm=D, tn=ada_cols, tk=_PAD_BATCH, a_silu=True) for l in range(n_layers)])
    flat = lambda a: a.reshape(n_layers * D, ada_cols)
    res = _adamw(flat(w_ada), flat(g_ada)[None], flat(m_w_ada), flat(v_w_ada), name="adamw_ada")
    for kind, buf in zip(("grad", "delta", "new_m", "new_v"), res):
        out[kind, "w_ada"] = buf.reshape(w_ada.shape)

    small_parts = lax.slice_in_dim(s_recv, dmod_rows, s_recv.shape[1], axis=1)
    packed = [_pack([(t[n], 0) for n in _SMALL], _ROW_ALIGN) for t in (w, m, v)]
    res = _adamw(packed[0], small_parts, packed[1], packed[2], name="adamw_replicated")
    for kind, buf in zip(("grad", "delta", "new_m", "new_v"), res):
        for n, a in zip(_SMALL, _unpack(buf, [w[n].shape for n in _SMALL], 0)):
            out[kind, n] = a

    return (loss, grad_x[None]) + tuple(out[kind, n] for kind in ("grad", "delta", "new_m", "new_v")
                                        for n in _W_NAMES)
```
